```python
import jax, jax.numpy as jnp
from jax import lax
import numpy as np

D_MODEL = 2048
BATCH = 8
SEQ = 2048
DEPTH = 1

CHUNK = 64
H_A = 8
DK_A = 128
DV_A = 128
D_A = H_A * DK_A
H_B = 8
DH_B = 128
D_B = H_B * DH_B
N_PAST_CHUNKS = 8
BAND = N_PAST_CHUNKS + 1
REL_FUTURE = CHUNK - 1
REL_PAST = 2 * CHUNK - 1
N_REL = REL_FUTURE + REL_PAST + 1
D_FF = -(-8 * D_MODEL // (3 * 256)) * 256
N_IN = 4 * D_A + 3 * D_B + 2 * D_MODEL
EPS = 1e-6

kernel_name = "hybrid_hgrn2_chunkattn_gated_block"


def rms_norm(x, gain):
    xf = x.astype(jnp.float32)
    y = xf * lax.rsqrt(jnp.mean(xf * xf, axis=-1, keepdims=True) + EPS)
    return (y * gain.astype(jnp.float32)).astype(x.dtype)


def hgrn_lower_bounds(lb_logits):
    p = jax.nn.softmax(lb_logits.astype(jnp.float32), axis=0)
    return jnp.cumsum(p, axis=0)[:DEPTH]


def hgrn2_mixer(q, f_logit, i, g, lb, out_gain):
    B, T, _ = q.shape
    n_chunks = T // CHUNK
    f32 = jnp.float32
    lbf = lb.astype(f32)
    f = lbf + (1.0 - lbf) * jax.nn.sigmoid(f_logit.astype(f32))
    log_f = jnp.log(f)
    k = 1.0 - f
    qs = jax.nn.silu(q.astype(f32))

    def to_chunks(t, dh):
        return t.reshape(B, n_chunks, CHUNK, H_A, dh).transpose(1, 0, 3, 2, 4)

    qc, kc, lfc = to_chunks(qs, DK_A), to_chunks(k, DK_A), to_chunks(log_f, DK_A)
    vc = to_chunks(i.astype(f32), DV_A)
    causal = jnp.tril(jnp.ones((CHUNK, CHUNK), dtype=bool))[:, :, None]

    def step(S, inp):
        qj, kj, lfj, vj = inp
        b = jnp.cumsum(lfj, axis=2)
        o_inter = jnp.einsum('bhtk,bhkv->bhtv', qj * jnp.exp(b), S)
        rel = jnp.where(causal, b[:, :, :, None, :] - b[:, :, None, :, :], -jnp.inf)
        decay = jnp.exp(rel)
        scores = jnp.einsum('bhtk,bhsk,bhtsk->bhts', qj, kj, decay)
        o_intra = jnp.einsum('bhts,bhsv->bhtv', scores, vj)
        b_last = b[:, :, -1:, :]
        S_new = jnp.exp(b_last[:, :, 0, :, None]) * S + jnp.einsum(
            'bhsk,bhsv->bhkv', kj * jnp.exp(b_last - b), vj)
        return S_new, o_inter + o_intra

    S0 = jnp.zeros((B, H_A, DK_A, DV_A), f32)
    _, o = lax.scan(step, S0, (qc, kc, lfc, vc))
    o = o.transpose(1, 0, 3, 2, 4).reshape(B, T, H_A, DV_A)
    o = o * lax.rsqrt(jnp.mean(o * o, axis=-1, keepdims=True) + EPS)
    o = o.reshape(B, T, D_A) * out_gain.astype(f32)
    o = o * jax.nn.silu(g.astype(f32))
    return o.astype(q.dtype)


def head_rms_norm(t, gain):
    tf = t.astype(jnp.float32)
    y = tf * lax.rsqrt(jnp.mean(tf * tf, axis=-1, keepdims=True) + EPS)
    return y * gain.astype(jnp.float32)


def chunked_relpos_attention(q, k, v, q_gain, k_gain, rel_bias):
    B, T, _ = q.shape
    n_chunks = T // CHUNK

    def heads(t):
        return t.reshape(B, n_chunks, CHUNK, H_B, DH_B).transpose(0, 3, 1, 2, 4)

    qh = head_rms_norm(heads(q), q_gain)
    kh = head_rms_norm(heads(k), k_gain)
    vh = heads(v).astype(jnp.float32)

    pad = ((0, 0), (0, 0), (N_PAST_CHUNKS, 0), (0, 0), (0, 0))
    band_idx = jnp.arange(n_chunks)[:, None] + jnp.arange(BAND)[None, :]
    k_band = jnp.pad(kh, pad)[:, :, band_idx].reshape(B, H_B, n_chunks, BAND * CHUNK, DH_B)
    v_band = jnp.pad(vh, pad)[:, :, band_idx].reshape(B, H_B, n_chunks, BAND * CHUNK, DH_B)

    q_pos = jnp.arange(n_chunks)[:, None] * CHUNK + jnp.arange(CHUNK)[None, :]
    k_chunk = band_idx - N_PAST_CHUNKS
    k_pos = (k_chunk[:, :, None] * CHUNK + jnp.arange(CHUNK)[None, None, :]).reshape(
        n_chunks, BAND * CHUNK)
    valid = k_pos >= 0
    dist = q_pos[:, :, None] - k_pos[:, None, :]
    rel_idx = jnp.clip(dist, -REL_FUTURE, REL_PAST) + REL_FUTURE
    bias = rel_bias.astype(jnp.float32)[:, rel_idx]

    scale = DH_B ** -0.5
    scores = jnp.einsum('bhnqd,bhnkd->bhnqk', qh, k_band) * scale + bias[None]
    scores = jnp.where(valid[None, None, :, None, :], scores, -jnp.inf)
    p = jax.nn.softmax(scores, axis=-1)
    o = jnp.einsum('bhnqk,bhnkd->bhnqd', p, v_band)
    return o.transpose(0, 2, 3, 1, 4).reshape(B, T, D_B).astype(q.dtype)


def _fwd_setup_inputs(seed: int = 0) -> dict:
    key = jax.random.key(seed)
    ks = jax.random.split(key, 16)
    f32 = jnp.float32

    def nrm(k, shape, scale):
        return jax.random.normal(k, shape, f32) * scale

    return {
        "x": nrm(ks[0], (BATCH, SEQ, D_MODEL), 1.0),
        "w_in": nrm(ks[1], (DEPTH, D_MODEL, N_IN), D_MODEL ** -0.5),
        "b_gate": nrm(ks[2], (DEPTH, 2 * D_MODEL), 0.02),
        "norm_mix": 1.0 + nrm(ks[3], (DEPTH, D_MODEL), 0.02),
        "norm_ffn": 1.0 + nrm(ks[4], (DEPTH, D_MODEL), 0.02),
        "hgrn_lb_logits": nrm(ks[5], (DEPTH + 1, D_A), 0.5),
        "hgrn_out_gain": 1.0 + nrm(ks[6], (DEPTH, D_A), 0.02),
        "q_gain": 1.0 + nrm(ks[7], (DEPTH, DH_B), 0.02),
        "k_gain": 1.0 + nrm(ks[8], (DEPTH, DH_B), 0.02),
        "rel_bias": nrm(ks[9], (DEPTH, H_B, N_REL), 0.1),
        "w_proj_a": nrm(ks[10], (DEPTH, D_A, D_MODEL), D_A ** -0.5),
        "w_proj_b": nrm(ks[11], (DEPTH, D_B, D_MODEL), D_B ** -0.5),
        "w_out": nrm(ks[12], (DEPTH, D_MODEL, D_MODEL), D_MODEL ** -0.5),
        "w_ffn_in": nrm(ks[13], (DEPTH, D_MODEL, 2 * D_FF), D_MODEL ** -0.5),
        "w_ffn_out": nrm(ks[14], (DEPTH, D_FF, D_MODEL), D_FF ** -0.5),
    }


def _fwd_reference(x, w_in, b_gate, norm_mix, norm_ffn, hgrn_lb_logits, hgrn_out_gain,
              q_gain, k_gain, rel_bias, w_proj_a, w_proj_b, w_out, w_ffn_in, w_ffn_out):
    lower_bounds = hgrn_lower_bounds(hgrn_lb_logits)
    split_pts = [D_A, 2 * D_A, 3 * D_A, 4 * D_A,
                 4 * D_A + D_B, 4 * D_A + 2 * D_B, 4 * D_A + 3 * D_B]
    for l in range(DEPTH):
        h = rms_norm(x, norm_mix[l])
        proj = jnp.einsum('btd,dn->btn', h, w_in[l])
        q_a, f_a, i_a, g_a, q_b, k_b, v_b, gate_logits = jnp.split(proj, split_pts, axis=-1)
        gates = jax.nn.sigmoid((gate_logits + b_gate[l]).astype(jnp.float32)).astype(x.dtype)
        gate_a, gate_b = jnp.split(gates, 2, axis=-1)

        y_a = hgrn2_mixer(q_a, f_a, i_a, g_a, lower_bounds[l], hgrn_out_gain[l])
        y_b = chunked_relpos_attention(q_b, k_b, v_b, q_gain[l], k_gain[l], rel_bias[l])

        merged = (gate_a * jnp.einsum('btc,cd->btd', y_a, w_proj_a[l])
                  + gate_b * jnp.einsum('btc,cd->btd', y_b, w_proj_b[l]))
        x = x + jnp.einsum('btd,de->bte', merged, w_out[l])

        h = rms_norm(x, norm_ffn[l])
        gate_up = jnp.einsum('btd,df->btf', h, w_ffn_in[l])
        ff_gate, ff_up = jnp.split(gate_up, 2, axis=-1)
        x = x + jnp.einsum('btf,fd->btd', jax.nn.silu(ff_gate) * ff_up, w_ffn_out[l])
    return x


import jax as _jax
import jax.numpy as _jnp

TWIN_FORMAT = 'train_step'
FWD_PARAMS = ['x', 'w_in', 'b_gate', 'norm_mix', 'norm_ffn', 'hgrn_lb_logits', 'hgrn_out_gain', 'q_gain', 'k_gain', 'rel_bias', 'w_proj_a', 'w_proj_b', 'w_out', 'w_ffn_in', 'w_ffn_out']
TWIN_WEIGHTS = ['w_in', 'b_gate', 'norm_mix', 'norm_ffn', 'hgrn_lb_logits', 'hgrn_out_gain', 'q_gain', 'k_gain', 'rel_bias', 'w_proj_a', 'w_proj_b', 'w_out', 'w_ffn_in', 'w_ffn_out']
TWIN_DIFF_INPUT = 'x'
TWIN_INPUTS = ['x', 'w_in', 'b_gate', 'norm_mix', 'norm_ffn', 'hgrn_lb_logits', 'hgrn_out_gain', 'q_gain', 'k_gain', 'rel_bias', 'w_proj_a', 'w_proj_b', 'w_out', 'w_ffn_in', 'w_ffn_out', 'loss_target', 'm_w_in', 'm_b_gate', 'm_norm_mix', 'm_norm_ffn', 'm_hgrn_lb_logits', 'm_hgrn_out_gain', 'm_q_gain', 'm_k_gain', 'm_rel_bias', 'm_w_proj_a', 'm_w_proj_b', 'm_w_out', 'm_w_ffn_in', 'm_w_ffn_out', 'v_w_in', 'v_b_gate', 'v_norm_mix', 'v_norm_ffn', 'v_hgrn_lb_logits', 'v_hgrn_out_gain', 'v_q_gain', 'v_k_gain', 'v_rel_bias', 'v_w_proj_a', 'v_w_proj_b', 'v_w_out', 'v_w_ffn_in', 'v_w_ffn_out']
TWIN_OUTPUTS = ['loss', 'grad_x', 'grad_w_in', 'grad_b_gate', 'grad_norm_mix', 'grad_norm_ffn', 'grad_hgrn_lb_logits', 'grad_hgrn_out_gain', 'grad_q_gain', 'grad_k_gain', 'grad_rel_bias', 'grad_w_proj_a', 'grad_w_proj_b', 'grad_w_out', 'grad_w_ffn_in', 'grad_w_ffn_out', 'delta_w_in', 'delta_b_gate', 'delta_norm_mix', 'delta_norm_ffn', 'delta_hgrn_lb_logits', 'delta_hgrn_out_gain', 'delta_q_gain', 'delta_k_gain', 'delta_rel_bias', 'delta_w_proj_a', 'delta_w_proj_b', 'delta_w_out', 'delta_w_ffn_in', 'delta_w_ffn_out', 'new_m_w_in', 'new_m_b_gate', 'new_m_norm_mix', 'new_m_norm_ffn', 'new_m_hgrn_lb_logits', 'new_m_hgrn_out_gain', 'new_m_q_gain', 'new_m_k_gain', 'new_m_rel_bias', 'new_m_w_proj_a', 'new_m_w_proj_b', 'new_m_w_out', 'new_m_w_ffn_in', 'new_m_w_ffn_out', 'new_v_w_in', 'new_v_b_gate', 'new_v_norm_mix', 'new_v_norm_ffn', 'new_v_hgrn_lb_logits', 'new_v_hgrn_out_gain', 'new_v_q_gain', 'new_v_k_gain', 'new_v_rel_bias', 'new_v_w_proj_a', 'new_v_w_proj_b', 'new_v_w_out', 'new_v_w_ffn_in', 'new_v_w_ffn_out']
TWIN_LEAF_KINDS = {'loss': 'loss', 'grad_x': 'grad_x', 'grad_w_in': 'grad_w', 'grad_b_gate': 'grad_w', 'grad_norm_mix': 'grad_w', 'grad_norm_ffn': 'grad_w', 'grad_hgrn_lb_logits': 'grad_w', 'grad_hgrn_out_gain': 'grad_w', 'grad_q_gain': 'grad_w', 'grad_k_gain': 'grad_w', 'grad_rel_bias': 'grad_w', 'grad_w_proj_a': 'grad_w', 'grad_w_proj_b': 'grad_w', 'grad_w_out': 'grad_w', 'grad_w_ffn_in': 'grad_w', 'grad_w_ffn_out': 'grad_w', 'delta_w_in': 'delta_w', 'delta_b_gate': 'delta_w', 'delta_norm_mix': 'delta_w', 'delta_norm_ffn': 'delta_w', 'delta_hgrn_lb_logits': 'delta_w', 'delta_hgrn_out_gain': 'delta_w', 'delta_q_gain': 'delta_w', 'delta_k_gain': 'delta_w', 'delta_rel_bias': 'delta_w', 'delta_w_proj_a': 'delta_w', 'delta_w_proj_b': 'delta_w', 'delta_w_out': 'delta_w', 'delta_w_ffn_in': 'delta_w', 'delta_w_ffn_out': 'delta_w', 'new_m_w_in': 'new_m', 'new_m_b_gate': 'new_m', 'new_m_norm_mix': 'new_m', 'new_m_norm_ffn': 'new_m', 'new_m_hgrn_lb_logits': 'new_m', 'new_m_hgrn_out_gain': 'new_m', 'new_m_q_gain': 'new_m', 'new_m_k_gain': 'new_m', 'new_m_rel_bias': 'new_m', 'new_m_w_proj_a': 'new_m', 'new_m_w_proj_b': 'new_m', 'new_m_w_out': 'new_m', 'new_m_w_ffn_in': 'new_m', 'new_m_w_ffn_out': 'new_m', 'new_v_w_in': 'new_v', 'new_v_b_gate': 'new_v', 'new_v_norm_mix': 'new_v', 'new_v_norm_ffn': 'new_v', 'new_v_hgrn_lb_logits': 'new_v', 'new_v_hgrn_out_gain': 'new_v', 'new_v_q_gain': 'new_v', 'new_v_k_gain': 'new_v', 'new_v_rel_bias': 'new_v', 'new_v_w_proj_a': 'new_v', 'new_v_w_proj_b': 'new_v', 'new_v_w_out': 'new_v', 'new_v_w_ffn_in': 'new_v', 'new_v_w_ffn_out': 'new_v'}


def _forward(args):
    return _fwd_reference(*[args[k] for k in FWD_PARAMS])


def _output_shape():
    out = _jax.eval_shape(lambda: _forward(_fwd_setup_inputs(0)))
    return out.shape, out.dtype

N_MICROBATCH = 1
ADAM_LR = 0.001
ADAM_B1 = 0.9
ADAM_B2 = 0.999
ADAM_EPS = 1e-08
ADAM_WD = 0.01
ADAM_STEP = 10
PER_EXAMPLE_BATCH_AXIS = {'x': 0, 'loss_target': 0}
SHARED_INPUTS = []
_WEIGHT_DTYPES = {'w_in': _jnp.float32, 'b_gate': _jnp.float32, 'norm_mix': _jnp.float32, 'norm_ffn': _jnp.float32, 'hgrn_lb_logits': _jnp.float32, 'hgrn_out_gain': _jnp.float32, 'q_gain': _jnp.float32, 'k_gain': _jnp.float32, 'rel_bias': _jnp.float32, 'w_proj_a': _jnp.float32, 'w_proj_b': _jnp.float32, 'w_out': _jnp.float32, 'w_ffn_in': _jnp.float32, 'w_ffn_out': _jnp.float32}
MOMENT_SCALE = {'w_in': 3.385376e-02, 'b_gate': 2.043180e-01, 'norm_mix': 1.037490e+00, 'norm_ffn': 6.179473e+00, 'hgrn_lb_logits': 6.503919e-03, 'hgrn_out_gain': 1.644574e+00, 'q_gain': 1.684745e-01, 'k_gain': 1.684475e-01, 'rel_bias': 1.223297e-02, 'w_proj_a': 5.155418e-02, 'w_proj_b': 1.276158e-02, 'w_out': 4.991528e-02, 'w_ffn_in': 4.720282e-02, 'w_ffn_out': 6.955555e-02}


def _to_microbatches(a, axis):
    t = _jnp.moveaxis(a, axis, 0)
    t = t.reshape((N_MICROBATCH, t.shape[0] // N_MICROBATCH) + t.shape[1:])
    return _jnp.moveaxis(t, 1, axis + 1)


def setup_inputs(seed: int = 0) -> dict:
    inp = _fwd_setup_inputs(seed)
    key = _jax.random.fold_in(_jax.random.key(seed), 7919)
    shape, _ = _output_shape()
    out = dict(inp)
    out["loss_target"] = _jax.random.normal(_jax.random.fold_in(key, 0), shape, _jnp.float32)
    for i, name in enumerate(TWIN_WEIGHTS):
        w = inp[name].astype(_jnp.float32)
        if MOMENT_SCALE is None:
            s = _jnp.sqrt(_jnp.mean(_jnp.square(w)) + 1e-30)
        else:
            s = MOMENT_SCALE[name]
        km, kv = _jax.random.split(_jax.random.fold_in(key, i + 1))
        out[name] = w
        out["m_" + name] = s * _jax.random.normal(km, w.shape, _jnp.float32)
        out["v_" + name] = (s * s) * _jax.random.uniform(kv, w.shape, _jnp.float32, 0.5, 1.5)
    if N_MICROBATCH > 1:
        for name, axis in PER_EXAMPLE_BATCH_AXIS.items():
            out[name] = _to_microbatches(out[name], axis)
    return {'x': out['x'], 'w_in': out['w_in'], 'b_gate': out['b_gate'], 'norm_mix': out['norm_mix'], 'norm_ffn': out['norm_ffn'], 'hgrn_lb_logits': out['hgrn_lb_logits'], 'hgrn_out_gain': out['hgrn_out_gain'], 'q_gain': out['q_gain'], 'k_gain': out['k_gain'], 'rel_bias': out['rel_bias'], 'w_proj_a': out['w_proj_a'], 'w_proj_b': out['w_proj_b'], 'w_out': out['w_out'], 'w_ffn_in': out['w_ffn_in'], 'w_ffn_out': out['w_ffn_out'], 'loss_target': out['loss_target'], 'm_w_in': out['m_w_in'], 'm_b_gate': out['m_b_gate'], 'm_norm_mix': out['m_norm_mix'], 'm_norm_ffn': out['m_norm_ffn'], 'm_hgrn_lb_logits': out['m_hgrn_lb_logits'], 'm_hgrn_out_gain': out['m_hgrn_out_gain'], 'm_q_gain': out['m_q_gain'], 'm_k_gain': out['m_k_gain'], 'm_rel_bias': out['m_rel_bias'], 'm_w_proj_a': out['m_w_proj_a'], 'm_w_proj_b': out['m_w_proj_b'], 'm_w_out': out['m_w_out'], 'm_w_ffn_in': out['m_w_ffn_in'], 'm_w_ffn_out': out['m_w_ffn_out'], 'v_w_in': out['v_w_in'], 'v_b_gate': out['v_b_gate'], 'v_norm_mix': out['v_norm_mix'], 'v_norm_ffn': out['v_norm_ffn'], 'v_hgrn_lb_logits': out['v_hgrn_lb_logits'], 'v_hgrn_out_gain': out['v_hgrn_out_gain'], 'v_q_gain': out['v_q_gain'], 'v_k_gain': out['v_k_gain'], 'v_rel_bias': out['v_rel_bias'], 'v_w_proj_a': out['v_w_proj_a'], 'v_w_proj_b': out['v_w_proj_b'], 'v_w_out': out['v_w_out'], 'v_w_ffn_in': out['v_w_ffn_in'], 'v_w_ffn_out': out['v_w_ffn_out']}


def _loss(weights, diff, rest, loss_target):
    with _jax.named_scope("forward"):
        args = {**rest, TWIN_DIFF_INPUT: diff, **{k: w.astype(_WEIGHT_DTYPES[k]) for k, w in weights.items()}}
        y = _forward(args)
    with _jax.named_scope("loss_head"):
        err = _jnp.square(y.astype(_jnp.float32) - loss_target)
        return 0.5 * _jnp.sum(_jnp.mean(err, axis=-1)) if err.ndim else 0.5 * err


def _adamw(w, g, m, v):
    m = ADAM_B1 * m + (1.0 - ADAM_B1) * g
    v = ADAM_B2 * v + (1.0 - ADAM_B2) * _jnp.square(g)
    m_hat = m / (1.0 - ADAM_B1 ** ADAM_STEP)
    v_hat = v / (1.0 - ADAM_B2 ** ADAM_STEP)
    delta = -ADAM_LR * (m_hat / (_jnp.sqrt(v_hat) + ADAM_EPS) + ADAM_WD * w)
    return delta, m, v


def reference(x, w_in, b_gate, norm_mix, norm_ffn, hgrn_lb_logits, hgrn_out_gain, q_gain, k_gain, rel_bias, w_proj_a, w_proj_b, w_out, w_ffn_in, w_ffn_out, loss_target, m_w_in, m_b_gate, m_norm_mix, m_norm_ffn, m_hgrn_lb_logits, m_hgrn_out_gain, m_q_gain, m_k_gain, m_rel_bias, m_w_proj_a, m_w_proj_b, m_w_out, m_w_ffn_in, m_w_ffn_out, v_w_in, v_b_gate, v_norm_mix, v_norm_ffn, v_hgrn_lb_logits, v_hgrn_out_gain, v_q_gain, v_k_gain, v_rel_bias, v_w_proj_a, v_w_proj_b, v_w_out, v_w_ffn_in, v_w_ffn_out):
    given = dict(x=x, w_in=w_in, b_gate=b_gate, norm_mix=norm_mix, norm_ffn=norm_ffn, hgrn_lb_logits=hgrn_lb_logits, hgrn_out_gain=hgrn_out_gain, q_gain=q_gain, k_gain=k_gain, rel_bias=rel_bias, w_proj_a=w_proj_a, w_proj_b=w_proj_b, w_out=w_out, w_ffn_in=w_ffn_in, w_ffn_out=w_ffn_out, loss_target=loss_target, m_w_in=m_w_in, m_b_gate=m_b_gate, m_norm_mix=m_norm_mix, m_norm_ffn=m_norm_ffn, m_hgrn_lb_logits=m_hgrn_lb_logits, m_hgrn_out_gain=m_hgrn_out_gain, m_q_gain=m_q_gain, m_k_gain=m_k_gain, m_rel_bias=m_rel_bias, m_w_proj_a=m_w_proj_a, m_w_proj_b=m_w_proj_b, m_w_out=m_w_out, m_w_ffn_in=m_w_ffn_in, m_w_ffn_out=m_w_ffn_out, v_w_in=v_w_in, v_b_gate=v_b_gate, v_norm_mix=v_norm_mix, v_norm_ffn=v_norm_ffn, v_hgrn_lb_logits=v_hgrn_lb_logits, v_hgrn_out_gain=v_hgrn_out_gain, v_q_gain=v_q_gain, v_k_gain=v_k_gain, v_rel_bias=v_rel_bias, v_w_proj_a=v_w_proj_a, v_w_proj_b=v_w_proj_b, v_w_out=v_w_out, v_w_ffn_in=v_w_ffn_in, v_w_ffn_out=v_w_ffn_out)
    weights = {n: given[n] for n in TWIN_WEIGHTS}
    shared = {n: given[n] for n in SHARED_INPUTS}
    per_example = {n: given[n] for n in ['x']}
    grad_fn = _jax.value_and_grad(_loss, argnums=(0, 1))

    def one_microbatch(ex, loss_target):
        ex = dict(ex)
        diff = ex.pop(TWIN_DIFF_INPUT)
        return grad_fn(weights, diff, {**shared, **ex}, loss_target)

    if N_MICROBATCH == 1:
        loss, (grad_w, grad_x) = one_microbatch(per_example, given["loss_target"])
    else:
        def body(carry, xs):
            loss_sum, grad_sum = carry
            l_k, (gw_k, gx_k) = one_microbatch(xs[0], xs[1])
            with _jax.named_scope("update"):
                return (loss_sum + l_k, _jax.tree.map(_jnp.add, grad_sum, gw_k)), gx_k

        init = (_jnp.zeros((), _jnp.float32), _jax.tree.map(_jnp.zeros_like, weights))
        (loss, grad_w), grad_x = _jax.lax.scan(body, init, (per_example, given["loss_target"]))
    with _jax.named_scope("update"):
        delta_w, new_m, new_v = {}, {}, {}
        for n in TWIN_WEIGHTS:
            delta_w[n], new_m[n], new_v[n] = _adamw(weights[n], grad_w[n], given["m_" + n], given["v_" + n])
    return (loss, grad_x, *[grad_w[n] for n in TWIN_WEIGHTS], *[delta_w[n] for n in TWIN_WEIGHTS],
            *[new_m[n] for n in TWIN_WEIGHTS], *[new_v[n] for n in TWIN_WEIGHTS])
```

```python
import functools

import jax
import jax.numpy as jnp
from jax import lax
from jax.experimental import pallas as pl
from jax.experimental.pallas import tpu as pltpu

F32 = jnp.float32
BF16 = jnp.bfloat16
SDS = jax.ShapeDtypeStruct
MESH = pl.DeviceIdType.MESH
HIGHEST = lax.Precision.HIGHEST

CHUNK = 64
SUB = 16
HEAD = 128
N_PAST = 8
BAND = (N_PAST + 1) * CHUNK
PAD = N_PAST * CHUNK
REL_FUTURE = CHUNK - 1
REL_PAST = 2 * CHUNK - 1
N_REL = REL_FUTURE + REL_PAST + 1
REL_LANES = 256
EPS = 1e-6
EXP_CLAMP = 80.0

ADAM_LR = 0.001
ADAM_B1 = 0.9
ADAM_B2 = 0.999
ADAM_EPS = 1e-08
ADAM_WD = 0.01
ADAM_STEP = 10

VMEM_LIMIT = 56 * 1024 * 1024

NT = (((1,), (1,)), ((), ()))
TN = (((0,), (0,)), ((), ()))
NN = (((1,), (0,)), ((), ()))


def _params(sem=None, **kw):
    return pltpu.CompilerParams(dimension_semantics=sem, vmem_limit_bytes=VMEM_LIMIT, **kw)


def _tile(n, pref, unit=128):
    if n <= pref:
        return n
    t = pref - pref % unit
    while n % t:
        t -= unit
    return t


def _sigmoid(x):
    return 1.0 / (1.0 + jnp.exp(-x))


def _dsilu(x, s):
    return s * (1.0 + x * (1.0 - s))


def _bdot(a, b, dims=NN):
    return lax.dot_general(a.astype(BF16), b.astype(BF16), dims, preferred_element_type=F32)


def _split(a):
    hi = a.astype(BF16)
    return hi, (a - hi.astype(F32)).astype(BF16)


def _dot3(a, b, dims):
    dot = lambda u, v: lax.dot_general(u, v, dims, preferred_element_type=F32)
    return dot(a[0], b[1]) + dot(a[1], b[0]) + dot(a[0], b[0])


def _fdot(a, b):
    return lax.dot_general(a, b, NN, precision=HIGHEST, preferred_element_type=F32)


def _matmul(a, b, *, ta=False, tb=False, res=None, out_dtype=F32, name, tm=1024, tn=512, tk=1024):
    m, k = (a.shape[1], a.shape[0]) if ta else a.shape
    n = b.shape[0] if tb else b.shape[1]
    tm, tn, tk = _tile(m, tm), _tile(n, tn), _tile(k, tk)
    nk = k // tk
    dims = ((((0,) if ta else (1,)), ((1,) if tb else (0,))), ((), ()))

    def body(*refs):
        if res is None:
            a_ref, b_ref, o_ref, acc_ref = refs
            r_ref = None
        else:
            a_ref, b_ref, r_ref, o_ref, acc_ref = refs
        kk = pl.program_id(2)

        @pl.when(kk == 0)
        def _():
            acc_ref[...] = jnp.zeros_like(acc_ref)

        acc_ref[...] += lax.dot_general(a_ref[...].astype(BF16), b_ref[...].astype(BF16), dims,
                                        preferred_element_type=F32)

        @pl.when(kk == nk - 1)
        def _():
            out = acc_ref[...]
            if r_ref is not None:
                out = out + r_ref[...]
            o_ref[...] = out.astype(o_ref.dtype)

    a_spec = pl.BlockSpec((tk, tm), lambda i, j, l: (l, i)) if ta else pl.BlockSpec((tm, tk), lambda i, j, l: (i, l))
    b_spec = pl.BlockSpec((tn, tk), lambda i, j, l: (j, l)) if tb else pl.BlockSpec((tk, tn), lambda i, j, l: (l, j))
    o_spec = pl.BlockSpec((tm, tn), lambda i, j, l: (i, j))
    in_specs = [a_spec, b_spec] + ([o_spec] if res is not None else [])
    args = (a, b) + ((res,) if res is not None else ())
    return pl.pallas_call(
        body, grid=(m // tm, n // tn, nk), in_specs=in_specs, out_specs=o_spec,
        out_shape=SDS((m, n), out_dtype), scratch_shapes=[pltpu.VMEM((tm, tn), F32)],
        compiler_params=_params(("parallel", "parallel", "arbitrary")), name=name)(*args)


def _cast_bf16(w, name):
    r, c = w.shape
    tr = _tile(r, 512, 16)

    def body(w_ref, o_ref):
        o_ref[...] = w_ref[...].astype(BF16)

    return pl.pallas_call(body, grid=(r // tr,), in_specs=[pl.BlockSpec((tr, c), lambda i: (i, 0))],
                          out_specs=pl.BlockSpec((tr, c), lambda i: (i, 0)), out_shape=SDS((r, c), BF16),
                          compiler_params=_params(("parallel",)), name=name)(w)


def _rmsnorm_fwd(x, gain, name):
    t, d = x.shape
    tm = _tile(t, 256)

    def body(x_ref, g_ref, h_ref, r_ref):
        xv = x_ref[...]
        r = lax.rsqrt(jnp.mean(xv * xv, axis=-1, keepdims=True) + EPS)
        h_ref[...] = (xv * r * g_ref[...]).astype(BF16)
        r_ref[...] = r

    return pl.pallas_call(
        body, grid=(t // tm,),
        in_specs=[pl.BlockSpec((tm, d), lambda i: (i, 0)), pl.BlockSpec((1, d), lambda i: (0, 0))],
        out_specs=[pl.BlockSpec((tm, d), lambda i: (i, 0)), pl.BlockSpec((tm, 1), lambda i: (i, 0))],
        out_shape=[SDS((t, d), BF16), SDS((t, 1), F32)], compiler_params=_params(("parallel",)), name=name)(x, gain)


def _rmsnorm_bwd(dh, x, r, gain, dres, name):
    t, d = x.shape
    tm = _tile(t, 256)

    def body(dh_ref, x_ref, r_ref, g_ref, dres_ref, dx_ref, dg_ref):
        @pl.when(pl.program_id(0) == 0)
        def _():
            dg_ref[...] = jnp.zeros_like(dg_ref)

        dhv, xv, rv = dh_ref[...], x_ref[...], r_ref[...]
        dg_ref[...] += jnp.sum(dhv * (xv * rv), axis=0, keepdims=True)
        u = dhv * g_ref[...]
        dx_ref[...] = dres_ref[...] + rv * u - xv * (rv * rv * rv) * jnp.mean(u * xv, axis=-1, keepdims=True)

    row = pl.BlockSpec((tm, d), lambda i: (i, 0))
    vec = pl.BlockSpec((1, d), lambda i: (0, 0))
    return pl.pallas_call(
        body, grid=(t // tm,), in_specs=[row, row, pl.BlockSpec((tm, 1), lambda i: (i, 0)), vec, row],
        out_specs=[row, vec], out_shape=[SDS((t, d), F32), SDS((1, d), F32)],
        compiler_params=_params(("arbitrary",)), name=name)(dh, x, r, gain, dres)


def _merge_fwd(proj, b_gate, pa, pb, off):
    t, d = pa.shape
    tm, tc = _tile(t, 512), _tile(d, 512)
    nj = d // tc
    oa, ob = off // tc, off // tc + nj

    def body(la_ref, lb_ref, ba_ref, bb_ref, pa_ref, pb_ref, o_ref):
        ga = _sigmoid(la_ref[...] + ba_ref[...])
        gb = _sigmoid(lb_ref[...] + bb_ref[...])
        o_ref[...] = (ga * pa_ref[...] + gb * pb_ref[...]).astype(BF16)

    tile = pl.BlockSpec((tm, tc), lambda i, j: (i, j))
    return pl.pallas_call(
        body, grid=(t // tm, nj),
        in_specs=[pl.BlockSpec((tm, tc), lambda i, j: (i, oa + j)), pl.BlockSpec((tm, tc), lambda i, j: (i, ob + j)),
                  pl.BlockSpec((1, tc), lambda i, j: (0, j)), pl.BlockSpec((1, tc), lambda i, j: (0, nj + j)), tile, tile],
        out_specs=tile, out_shape=SDS((t, d), BF16), compiler_params=_params(("parallel", "parallel")),
        name="merge_fwd")(proj, proj, b_gate, b_gate, pa, pb)


def _merge_bwd(dmerged, proj, b_gate, pa, pb, off):
    t, d = pa.shape
    tm, tc = _tile(t, 512), _tile(d, 512)
    nj, ni = d // tc, t // tm
    o0 = off // tc

    def body(dm_ref, l_ref, b_ref, pa_ref, pb_ref, dp_ref, dl_ref, db_ref):
        s, i = pl.program_id(0), pl.program_id(2)
        p = jnp.where(s == 0, pa_ref[...], pb_ref[...])
        g = _sigmoid(l_ref[...] + b_ref[...])
        dm = dm_ref[...]
        dp_ref[0] = (dm * g).astype(BF16)
        dl = dm * p * g * (1.0 - g)
        dl_ref[...] = dl

        @pl.when(i == 0)
        def _():
            db_ref[...] = jnp.zeros_like(db_ref)

        db_ref[...] += jnp.sum(dl, axis=0, keepdims=True)

    tile = pl.BlockSpec((tm, tc), lambda s, j, i: (i, j))
    return pl.pallas_call(
        body, grid=(2, nj, ni),
        in_specs=[tile, pl.BlockSpec((tm, tc), lambda s, j, i: (i, o0 + s * nj + j)),
                  pl.BlockSpec((1, tc), lambda s, j, i: (0, s * nj + j)), tile, tile],
        out_specs=[pl.BlockSpec((1, tm, tc), lambda s, j, i: (s, i, j)),
                   pl.BlockSpec((tm, tc), lambda s, j, i: (i, o0 + s * nj + j)),
                   pl.BlockSpec((1, tc), lambda s, j, i: (0, s * nj + j))],
        out_shape=[SDS((2, t, d), BF16), SDS(proj.shape, F32), SDS((1, 2 * d), F32)],
        compiler_params=_params(("arbitrary", "arbitrary", "arbitrary")),
        name="merge_bwd")(dmerged, proj, b_gate, pa, pb)


def _swiglu_fwd(gu):
    t, f2 = gu.shape
    f = f2 // 2
    tm, tc = _tile(t, 512), _tile(f, 512)
    nj = f // tc

    def body(g_ref, u_ref, o_ref):
        g = g_ref[...]
        o_ref[...] = (g * _sigmoid(g) * u_ref[...]).astype(BF16)

    return pl.pallas_call(
        body, grid=(t // tm, nj),
        in_specs=[pl.BlockSpec((tm, tc), lambda i, j: (i, j)), pl.BlockSpec((tm, tc), lambda i, j: (i, nj + j))],
        out_specs=pl.BlockSpec((tm, tc), lambda i, j: (i, j)), out_shape=SDS((t, f), BF16),
        compiler_params=_params(("parallel", "parallel")), name="swiglu_fwd")(gu, gu)


def _swiglu_bwd(dact, gu):
    t, f2 = gu.shape
    f = f2 // 2
    tm, tc = _tile(t, 512), _tile(f, 512)
    nj = f // tc

    def body(d_ref, g_ref, u_ref, o_ref):
        s = pl.program_id(1)
        g, dv = g_ref[...], d_ref[...]
        sg = _sigmoid(g)

        @pl.when(s == 0)
        def _():
            o_ref[...] = dv * u_ref[...] * _dsilu(g, sg)

        @pl.when(s == 1)
        def _():
            o_ref[...] = dv * (g * sg)

    return pl.pallas_call(
        body, grid=(t // tm, 2, nj),
        in_specs=[pl.BlockSpec((tm, tc), lambda i, s, j: (i, j)), pl.BlockSpec((tm, tc), lambda i, s, j: (i, j)),
                  pl.BlockSpec((tm, tc), lambda i, s, j: (i, nj + j))],
        out_specs=pl.BlockSpec((tm, tc), lambda i, s, j: (i, s * nj + j)), out_shape=SDS((t, f2), F32),
        compiler_params=_params(("parallel", "parallel", "parallel")), name="swiglu_bwd")(dact, gu, gu)


def _loss_head(y, target):
    t, d = y.shape
    tm = _tile(t, 256)

    def body(y_ref, t_ref, dy_ref, l_ref):
        @pl.when(pl.program_id(0) == 0)
        def _():
            l_ref[...] = jnp.zeros_like(l_ref)

        e = y_ref[...] - t_ref[...]
        dy_ref[...] = e * (1.0 / d)
        l_ref[...] += 0.5 * jnp.sum(jnp.mean(e * e, axis=-1, keepdims=True), axis=0, keepdims=True)

    row = pl.BlockSpec((tm, d), lambda i: (i, 0))
    return pl.pallas_call(
        body, grid=(t // tm,), in_specs=[row, row], out_specs=[row, pl.BlockSpec((1, 1), lambda i: (0, 0))],
        out_shape=[SDS((t, d), F32), SDS((1, 1), F32)], compiler_params=_params(("arbitrary",)),
        name="loss_head")(y, target)


def _rel_onehot(qi):
    p = lax.broadcasted_iota(jnp.int32, (REL_LANES, BAND), 1)
    r = lax.broadcasted_iota(jnp.int32, (REL_LANES, BAND), 0)
    idx = jnp.clip(qi + PAD - p, -REL_FUTURE, REL_PAST) + REL_FUTURE
    return (idx == r).astype(F32)


def _relbias_expand(rb):
    h = rb.shape[0]

    def body(rb_ref, o_ref):
        def step(qi, _):
            o_ref[qi] = _fdot(rb_ref[...], _rel_onehot(qi))
            return 0

        lax.fori_loop(0, CHUNK, step, 0)

    return pl.pallas_call(body, out_shape=SDS((CHUNK, h, BAND), F32), compiler_params=_params(),
                          name="relbias_expand")(rb)


def _relbias_reduce(dbias):
    h = dbias.shape[1]

    def body(db_ref, o_ref):
        def step(qi, acc):
            return acc + lax.dot_general(db_ref[qi], _rel_onehot(qi), NT, precision=HIGHEST,
                                         preferred_element_type=F32)

        o_ref[...] = lax.fori_loop(0, CHUNK, step, jnp.zeros((h, REL_LANES), F32))

    return pl.pallas_call(body, out_shape=SDS((h, REL_LANES), F32), compiler_params=_params(),
                          name="relbias_reduce")(dbias)


def _lower_bound(l_ref):
    l0, l1 = l_ref[0:1, :], l_ref[1:2, :]
    m = jnp.maximum(l0, l1)
    e0, e1 = jnp.exp(l0 - m), jnp.exp(l1 - m)
    return e0 / (e0 + e1)


def _tri(lower):
    r = lax.broadcasted_iota(jnp.int32, (CHUNK, CHUNK), 0)
    c = lax.broadcasted_iota(jnp.int32, (CHUNK, CHUNK), 1)
    return r >= c if lower else r <= c


def _hgrn_intra(qs, kk, b_s):
    rows = lax.broadcasted_iota(jnp.int32, (CHUNK, HEAD), 0)
    tr = lax.broadcasted_iota(jnp.int32, (SUB, CHUNK), 0)
    tc = lax.broadcasted_iota(jnp.int32, (SUB, CHUNK), 1)
    b = b_s[...]
    out = []
    for i in range(CHUNK // SUB):
        lo = i * SUB
        ref = jnp.zeros((1, HEAD), F32) if i == 0 else b_s[lo - 1:lo, :]
        eq = jnp.exp(b[lo:lo + SUB] - ref)
        qt = _split(qs[lo:lo + SUB] * eq)
        e = jnp.where(rows < lo + SUB, jnp.exp(jnp.minimum(ref - b, EXP_CLAMP)), 0.0)
        kt = _split(kk * e)
        a = jnp.where(tc <= tr + lo, _dot3(qt, kt, NT), 0.0)
        out.append((eq, qt, e, kt, a))
    return out


def _hgrn_fwd(proj, lb_logits, gain, n_heads):
    t = proj.shape[0]
    nc = t // CHUNK
    da = n_heads * HEAD

    def body(q_ref, f_ref, i_ref, g_ref, l_ref, gain_ref, y_ref, o_ref, st_ref, state, b_s):
        state[...] = jnp.zeros_like(state)
        lb = _lower_bound(l_ref)
        gain_v = gain_ref[...]
        tril = _tri(True).astype(F32)

        def chunk(j, _):
            sl = pl.ds(pl.multiple_of(j * CHUNK, CHUNK), CHUNK)
            fg = lb + (1.0 - lb) * _sigmoid(f_ref[sl, :])
            kk = 1.0 - fg
            qv = q_ref[sl, :]
            qs = qv * _sigmoid(qv)
            vb = i_ref[sl, :].astype(BF16)
            b = _fdot(tril, jnp.log(fg))
            b_s[...] = b
            s_in = state[...]
            st_ref[0, j] = s_in
            o = _bdot(qs * jnp.exp(b), s_in, NT)
            a = jnp.concatenate([blk[4] for blk in _hgrn_intra(qs, kk, b_s)], axis=0)
            o = o + _bdot(a, vb)
            o_ref[sl, :] = o
            bl = b_s[CHUNK - 1:CHUNK, :]
            state[...] = s_in * jnp.exp(bl) + _bdot(vb, kk * jnp.exp(bl - b), TN)
            rr = lax.rsqrt(jnp.mean(o * o, axis=-1, keepdims=True) + EPS)
            gv = g_ref[sl, :]
            y_ref[sl, :] = (o * rr * gain_v * (gv * _sigmoid(gv))).astype(BF16)
            return 0

        lax.fori_loop(0, nc, chunk, 0)

    col = lambda k: pl.BlockSpec((t, HEAD), lambda h: (0, k * n_heads + h))
    vec = pl.BlockSpec((1, HEAD), lambda h: (0, h))
    return pl.pallas_call(
        body, grid=(n_heads,),
        in_specs=[col(0), col(1), col(2), col(3), pl.BlockSpec((2, HEAD), lambda h: (0, h)), vec],
        out_specs=[pl.BlockSpec((t, HEAD), lambda h: (0, h)), pl.BlockSpec((t, HEAD), lambda h: (0, h)),
                   pl.BlockSpec((1, nc, HEAD, HEAD), lambda h: (h, 0, 0, 0))],
        out_shape=[SDS((t, da), BF16), SDS((t, da), F32), SDS((n_heads, nc, HEAD, HEAD), F32)],
        scratch_shapes=[pltpu.VMEM((HEAD, HEAD), F32), pltpu.VMEM((CHUNK, HEAD), F32)],
        compiler_params=_params(("parallel",)), name="hgrn_fwd")(proj, proj, proj, proj, lb_logits, gain)


def _hgrn_bwd(dproj, proj, o_pre, states, dy, lb_logits, gain, n_heads):
    t = proj.shape[0]
    nc = t // CHUNK
    da = n_heads * HEAD

    def body(dproj_any, q_ref, f_ref, i_ref, g_ref, o_ref, st_ref, dy_ref, l_ref, gain_ref,
             dproj_ref, dl_ref, dgain_ref, res, dstate, b_s):
        @pl.when(pl.program_id(1) == 0)
        def _():
            dstate[...] = jnp.zeros_like(dstate)
            lb = _lower_bound(l_ref)
            gain_v = gain_ref[...]
            tril, triu = _tri(True), _tri(False).astype(F32)
            last = lax.broadcasted_iota(jnp.int32, (CHUNK, HEAD), 0) == CHUNK - 1

            def chunk(jj, carry):
                dlb_acc, dgain_acc = carry
                j = nc - 1 - jj
                sl = pl.ds(pl.multiple_of(j * CHUNK, CHUNK), CHUNK)
                sg = _sigmoid(f_ref[sl, :])
                fg = lb + (1.0 - lb) * sg
                kk = 1.0 - fg
                qv = q_ref[sl, :]
                sq = _sigmoid(qv)
                qs = qv * sq
                vb = i_ref[sl, :].astype(BF16)
                gv = g_ref[sl, :]
                sgg = _sigmoid(gv)
                silg = gv * sgg
                b = _fdot(tril.astype(F32), jnp.log(fg))
                b_s[...] = b
                o = o_ref[sl, :]
                dyv = dy_ref[sl, :]
                rr = lax.rsqrt(jnp.mean(o * o, axis=-1, keepdims=True) + EPS)
                on = o * rr
                dgain_acc = dgain_acc + jnp.sum(dyv * on * silg, axis=0, keepdims=True)
                dg = dyv * on * gain_v * _dsilu(gv, sgg)
                don = dyv * gain_v * silg
                do = (rr * don - o * (rr * rr * rr) * jnp.mean(don * o, axis=-1, keepdims=True)).astype(BF16)
                s_in = st_ref[0, j]
                ds_out = dstate[...]
                eb = jnp.exp(b)
                bl = b_s[CHUNK - 1:CHUNK, :]
                ebl = jnp.exp(bl)
                ekd = jnp.exp(bl - b)
                dq = _bdot(do, s_in) * eb
                da_m = jnp.where(tril, _bdot(do, vb, NT), 0.0)
                a_rows, dq_rows = [], []
                dk = jnp.zeros((CHUNK, HEAD), F32)
                for i, (eq, qt, e, kt, a) in enumerate(_hgrn_intra(qs, kk, b_s)):
                    da_i = _split(da_m[i * SUB:(i + 1) * SUB])
                    a_rows.append(a)
                    dq_rows.append(eq * _dot3(da_i, kt, NN))
                    dk = dk + e * _dot3(da_i, qt, TN)
                dq = dq + jnp.concatenate(dq_rows, axis=0)
                dv = _bdot(jnp.concatenate(a_rows, axis=0), do, TN) + _bdot(kk * ekd, ds_out, NT)
                dk_state = ekd * _bdot(vb, ds_out)
                dk = dk + dk_state
                db = qs * dq - kk * dk
                extra = (jnp.sum(kk * dk_state, axis=0, keepdims=True)
                         + ebl * jnp.sum(s_in * ds_out, axis=0, keepdims=True))
                db = db + jnp.where(last, extra, 0.0)
                dfg = _fdot(triu, db) / fg - dk
                dlb_acc = dlb_acc + jnp.sum(dfg * (1.0 - sg), axis=0, keepdims=True)
                dstate[...] = ds_out * ebl + _bdot(do, qs * eb, TN)
                res[0, sl, :] = dq * _dsilu(qv, sq)
                res[1, sl, :] = dfg * (1.0 - lb) * sg * (1.0 - sg)
                res[2, sl, :] = dv
                res[3, sl, :] = dg
                return dlb_acc, dgain_acc

            zero = jnp.zeros((1, HEAD), F32)
            dlb, dgain = lax.fori_loop(0, nc, chunk, (zero, zero))
            dgain_ref[...] = dgain
            dl0 = dlb * lb * (1.0 - lb)
            dl_ref[0:1, :] = dl0
            dl_ref[1:2, :] = -dl0

        dproj_ref[...] = res[pl.program_id(1)]

    col = lambda k: pl.BlockSpec((t, HEAD), lambda h, p: (0, k * n_heads + h))
    head = pl.BlockSpec((t, HEAD), lambda h, p: (0, h))
    vec = pl.BlockSpec((1, HEAD), lambda h, p: (0, h))
    return pl.pallas_call(
        body, grid=(n_heads, 4),
        in_specs=[pl.BlockSpec(memory_space=pl.ANY), col(0), col(1), col(2), col(3), head,
                  pl.BlockSpec((1, nc, HEAD, HEAD), lambda h, p: (h, 0, 0, 0)),
                  head, pl.BlockSpec((2, HEAD), lambda h, p: (0, h)), vec],
        out_specs=[pl.BlockSpec((t, HEAD), lambda h, p: (0, p * n_heads + h)),
                   pl.BlockSpec((2, HEAD), lambda h, p: (0, h)), vec],
        out_shape=[SDS(dproj.shape, F32), SDS((2, da), F32), SDS((1, da), F32)],
        scratch_shapes=[pltpu.VMEM((4, t, HEAD), F32), pltpu.VMEM((HEAD, HEAD), F32), pltpu.VMEM((CHUNK, HEAD), F32)],
        input_output_aliases={0: 0}, compiler_params=_params(("arbitrary", "arbitrary")),
        name="hgrn_bwd")(dproj, proj, proj, proj, proj, o_pre, states, dy, lb_logits, gain)


ROWS = 256


def _head_norm(x_ref, gain, dst, dst_off, t):
    def step(i, _):
        sl = pl.ds(pl.multiple_of(i * ROWS, ROWS), ROWS)
        xv = x_ref[sl, :]
        r = lax.rsqrt(jnp.mean(xv * xv, axis=-1, keepdims=True) + EPS)
        dst[pl.ds(pl.multiple_of(dst_off + i * ROWS, ROWS), ROWS), :] = (xv * r * gain).astype(BF16)
        return 0

    lax.fori_loop(0, t // ROWS, step, 0)


def _head_norm_bwd(x_ref, gain, dn_ref, dn_off, out, slot, t):
    def step(i, acc):
        sl = pl.ds(pl.multiple_of(i * ROWS, ROWS), ROWS)
        xv = x_ref[sl, :]
        dn = dn_ref[pl.ds(pl.multiple_of(dn_off + i * ROWS, ROWS), ROWS), :]
        r = lax.rsqrt(jnp.mean(xv * xv, axis=-1, keepdims=True) + EPS)
        u = dn * gain
        out[slot, sl, :] = r * u - xv * (r * r * r) * jnp.mean(u * xv, axis=-1, keepdims=True)
        return acc + jnp.sum(dn * (xv * r), axis=0, keepdims=True)

    return lax.fori_loop(0, t // ROWS, step, jnp.zeros((1, HEAD), F32))


def _attn_probs(qn, kpad, bias_ref, n):
    qc = qn[pl.ds(pl.multiple_of(n * CHUNK, CHUNK), CHUNK), :]
    band = pl.ds(pl.multiple_of(n * CHUNK, CHUNK), BAND)
    s = lax.dot_general(qc, kpad[band, :], NT, preferred_element_type=F32) * (HEAD ** -0.5) + bias_ref[0]
    col = lax.broadcasted_iota(jnp.int32, (CHUNK, BAND), 1)
    s = jnp.where(col >= PAD - n * CHUNK, s, -jnp.inf)
    p = jnp.exp(s - jnp.max(s, axis=-1, keepdims=True))
    return qc, band, p / jnp.sum(p, axis=-1, keepdims=True)


def _attn_fwd(proj, q_gain, k_gain, bias, n_heads, col0):
    t = proj.shape[0]
    nc = t // CHUNK

    def body(q_ref, k_ref, v_ref, qg_ref, kg_ref, bias_ref, y_ref, qn, kpad, vpad):
        kpad[0:PAD, :] = jnp.zeros((PAD, HEAD), BF16)
        vpad[0:PAD, :] = jnp.zeros((PAD, HEAD), BF16)
        _head_norm(q_ref, qg_ref[...], qn, 0, t)
        _head_norm(k_ref, kg_ref[...], kpad, PAD, t)

        def copy_v(i, _):
            vpad[pl.ds(pl.multiple_of(PAD + i * ROWS, ROWS), ROWS), :] = v_ref[
                pl.ds(pl.multiple_of(i * ROWS, ROWS), ROWS), :].astype(BF16)
            return 0

        lax.fori_loop(0, t // ROWS, copy_v, 0)

        def chunk(n, _):
            _, band, p = _attn_probs(qn, kpad, bias_ref, n)
            y_ref[pl.ds(pl.multiple_of(n * CHUNK, CHUNK), CHUNK), :] = _bdot(p, vpad[band, :]).astype(BF16)
            return 0

        lax.fori_loop(0, nc, chunk, 0)

    col = lambda k: pl.BlockSpec((t, HEAD), lambda h: (0, col0 + k * n_heads + h))
    vec = pl.BlockSpec((1, HEAD), lambda h: (0, 0))
    return pl.pallas_call(
        body, grid=(n_heads,),
        in_specs=[col(0), col(1), col(2), vec, vec, pl.BlockSpec((1, CHUNK, BAND), lambda h: (h, 0, 0))],
        out_specs=pl.BlockSpec((t, HEAD), lambda h: (0, h)), out_shape=SDS((t, n_heads * HEAD), BF16),
        scratch_shapes=[pltpu.VMEM((t, HEAD), BF16), pltpu.VMEM((t + PAD, HEAD), BF16), pltpu.VMEM((t + PAD, HEAD), BF16)],
        compiler_params=_params(("parallel",)), name="attn_fwd")(proj, proj, proj, q_gain, k_gain, bias)


def _attn_bwd(dproj, proj, q_gain, k_gain, bias, dy, n_heads, col0):
    t = proj.shape[0]
    nc = t // CHUNK

    def body(dproj_any, q_ref, k_ref, v_ref, qg_ref, kg_ref, bias_ref, dy_ref,
             dproj_ref, dbias_ref, dqg_ref, dkg_ref, qn, kpad, vpad, dqn, dk_acc, dv_acc, res):
        h, part = pl.program_id(0), pl.program_id(1)

        @pl.when(part == 0)
        def _():
            kpad[0:PAD, :] = jnp.zeros((PAD, HEAD), BF16)
            vpad[0:PAD, :] = jnp.zeros((PAD, HEAD), BF16)
            _head_norm(q_ref, qg_ref[...], qn, 0, t)
            _head_norm(k_ref, kg_ref[...], kpad, PAD, t)

            def prep(i, _):
                sl = pl.ds(pl.multiple_of(PAD + i * ROWS, ROWS), ROWS)
                vpad[sl, :] = v_ref[pl.ds(pl.multiple_of(i * ROWS, ROWS), ROWS), :].astype(BF16)
                return 0

            lax.fori_loop(0, t // ROWS, prep, 0)

            def clear(i, _):
                sl = pl.ds(pl.multiple_of(i * ROWS, ROWS), ROWS)
                dk_acc[sl, :] = jnp.zeros((ROWS, HEAD), F32)
                dv_acc[sl, :] = jnp.zeros((ROWS, HEAD), F32)
                return 0

            lax.fori_loop(0, (t + PAD) // ROWS, clear, 0)
            dbias_ref[0] = jnp.zeros((CHUNK, BAND), F32)

            def chunk(n, _):
                qc, band, p = _attn_probs(qn, kpad, bias_ref, n)
                do = dy_ref[pl.ds(pl.multiple_of(n * CHUNK, CHUNK), CHUNK), :].astype(BF16)
                dp = lax.dot_general(do, vpad[band, :], NT, preferred_element_type=F32)
                ds = p * (dp - jnp.sum(dp * p, axis=-1, keepdims=True))
                dbias_ref[0] += ds
                dss = (ds * (HEAD ** -0.5)).astype(BF16)
                dqn[pl.ds(pl.multiple_of(n * CHUNK, CHUNK), CHUNK), :] = lax.dot_general(
                    dss, kpad[band, :], NN, preferred_element_type=F32)
                dk_acc[band, :] += lax.dot_general(dss, qc, TN, preferred_element_type=F32)
                dv_acc[band, :] += lax.dot_general(p.astype(BF16), do, TN, preferred_element_type=F32)
                return 0

            lax.fori_loop(0, nc, chunk, 0)
            dqg = _head_norm_bwd(q_ref, qg_ref[...], dqn, 0, res, 0, t)
            dkg = _head_norm_bwd(k_ref, kg_ref[...], dk_acc, PAD, res, 1, t)

            def put_v(i, _):
                sl = pl.ds(pl.multiple_of(i * ROWS, ROWS), ROWS)
                res[2, sl, :] = dv_acc[pl.ds(pl.multiple_of(PAD + i * ROWS, ROWS), ROWS), :]
                return 0

            lax.fori_loop(0, t // ROWS, put_v, 0)

            @pl.when(h == 0)
            def _():
                dqg_ref[...] = jnp.zeros_like(dqg_ref)
                dkg_ref[...] = jnp.zeros_like(dkg_ref)

            dqg_ref[...] += dqg
            dkg_ref[...] += dkg

        dproj_ref[...] = res[part]

    col = lambda k: pl.BlockSpec((t, HEAD), lambda h, p: (0, col0 + k * n_heads + h))
    vec = pl.BlockSpec((1, HEAD), lambda h, p: (0, 0))
    btile = pl.BlockSpec((1, CHUNK, BAND), lambda h, p: (h, 0, 0))
    return pl.pallas_call(
        body, grid=(n_heads, 3),
        in_specs=[pl.BlockSpec(memory_space=pl.ANY), col(0), col(1), col(2), vec, vec, btile,
                  pl.BlockSpec((t, HEAD), lambda h, p: (0, h))],
        out_specs=[pl.BlockSpec((t, HEAD), lambda h, p: (0, col0 + p * n_heads + h)), btile, vec, vec],
        out_shape=[SDS(dproj.shape, F32), SDS((n_heads, CHUNK, BAND), F32), SDS((1, HEAD), F32), SDS((1, HEAD), F32)],
        scratch_shapes=[pltpu.VMEM((t, HEAD), BF16), pltpu.VMEM((t + PAD, HEAD), BF16), pltpu.VMEM((t + PAD, HEAD), BF16),
                        pltpu.VMEM((t, HEAD), F32), pltpu.VMEM((t + PAD, HEAD), F32), pltpu.VMEM((t + PAD, HEAD), F32),
                        pltpu.VMEM((3, t, HEAD), F32)],
        input_output_aliases={0: 0}, compiler_params=_params(("arbitrary", "arbitrary")),
        name="attn_bwd")(dproj, proj, proj, proj, q_gain, k_gain, bias, dy)


def _place():
    x, y, c = lax.axis_index("x"), lax.axis_index("y"), lax.axis_index("c")
    others = [(1 - x, y), (x, 1 - y), (1 - x, 1 - y)]
    return x, y, c, others


def _chunk_of(ref, kind, chip, half, shard_shape):
    r, n = shard_shape
    hr = r // 2
    if kind == "col":
        rows = pl.ds(0, r) if half is None else pl.ds(half * hr, hr)
        return ref.at[rows, pl.ds(chip * n, n)]
    rows = pl.ds(chip * r, r) if half is None else pl.ds(chip * r + half * hr, hr)
    return ref.at[rows, :]


def _half_of(ref, half):
    hr = ref.shape[0] // 2
    return ref.at[pl.ds(half * hr, hr), :]


HBM = pl.BlockSpec(memory_space=pltpu.HBM)


def _gather_weights(shards, kinds):
    nw = len(shards)
    full = []
    for s, kind in zip(shards, kinds):
        r, n = s.shape
        full.append(SDS((r, 4 * n) if kind == "col" else (4 * r, n), BF16))

    def body(*refs):
        ins, outs = refs[:nw], refs[nw:2 * nw]
        send, recv, fsend, frecv, lsem = refs[2 * nw:]
        x, y, c, others = _place()
        me = 2 * x + y
        local = []
        for w in range(nw):
            cp = pltpu.make_async_copy(ins[w], _chunk_of(outs[w], kinds[w], me, None, shards[w].shape), lsem.at[w])
            cp.start()
            local.append(cp)
        first, passed = [], []
        for w in range(nw):
            for p, (px, py) in enumerate(others):
                cp = pltpu.make_async_remote_copy(
                    src_ref=_half_of(ins[w], c), dst_ref=_chunk_of(outs[w], kinds[w], me, c, shards[w].shape),
                    send_sem=send.at[w, p], recv_sem=recv.at[w, p], device_id=(px, py, c), device_id_type=MESH)
                cp.start()
                first.append(cp)
        for w in range(nw):
            for p, (px, py) in enumerate(others):
                blk = _chunk_of(outs[w], kinds[w], 2 * px + py, c, shards[w].shape)
                pltpu.make_async_remote_copy(src_ref=blk, dst_ref=blk, send_sem=send.at[w, p], recv_sem=recv.at[w, p],
                                             device_id=(px, py, c), device_id_type=MESH).wait_recv()
                cp = pltpu.make_async_remote_copy(src_ref=blk, dst_ref=blk, send_sem=fsend.at[w, p],
                                                  recv_sem=frecv.at[w, p], device_id=(x, y, 1 - c), device_id_type=MESH)
                cp.start()
                passed.append(cp)
        for w in range(nw):
            for p, (px, py) in enumerate(others):
                blk = _chunk_of(outs[w], kinds[w], 2 * px + py, 1 - c, shards[w].shape)
                pltpu.make_async_remote_copy(src_ref=blk, dst_ref=blk, send_sem=fsend.at[w, p], recv_sem=frecv.at[w, p],
                                             device_id=(x, y, 1 - c), device_id_type=MESH).wait_recv()
        for cp in first + passed:
            cp.wait_send()
        for cp in local:
            cp.wait()

    return pl.pallas_call(
        body, in_specs=[HBM] * nw, out_specs=[HBM] * nw, out_shape=full,
        scratch_shapes=[pltpu.SemaphoreType.DMA((nw, 3)), pltpu.SemaphoreType.DMA((nw, 3)),
                        pltpu.SemaphoreType.DMA((nw, 3)), pltpu.SemaphoreType.DMA((nw, 3)),
                        pltpu.SemaphoreType.DMA((nw,))],
        compiler_params=pltpu.CompilerParams(has_side_effects=True), name="gather_weights")(*shards)


def _pair_exchange(grads, kinds, shard_shapes):
    nw = len(grads)
    outs_shape = [SDS((4, r // 2, n), F32) for (r, n) in shard_shapes]

    def body(*refs):
        ins, outs = refs[:nw], refs[nw:2 * nw]
        send, recv = refs[2 * nw:]
        x, y, c, _ = _place()
        cps = []
        for w in range(nw):
            for k in range(4):
                cp = pltpu.make_async_remote_copy(
                    src_ref=_chunk_of(ins[w], kinds[w], k, 1 - c, shard_shapes[w]), dst_ref=outs[w].at[k],
                    send_sem=send.at[w, k], recv_sem=recv.at[w, k], device_id=(x, y, 1 - c), device_id_type=MESH)
                cp.start()
                cps.append(cp)
        for cp in cps:
            cp.wait()

    return pl.pallas_call(
        body, in_specs=[HBM] * nw, out_specs=[HBM] * nw, out_shape=outs_shape,
        scratch_shapes=[pltpu.SemaphoreType.DMA((nw, 4)), pltpu.SemaphoreType.DMA((nw, 4))],
        compiler_params=pltpu.CompilerParams(has_side_effects=True), name="pair_exchange")(*grads)


def _pair_add(grad, got, kind, shard_shape, c_arr, name):
    r, n = shard_shape
    hr = r // 2
    tr, tn = _tile(hr, 256, 16), _tile(n, 1408)
    nr, nn = hr // tr, n // tn
    if kind == "col":
        g_spec = pl.BlockSpec((tr, tn), lambda k, i, j, c: (c[0] * nr + i, k * nn + j))
    else:
        g_spec = pl.BlockSpec((tr, tn), lambda k, i, j, c: ((2 * k + c[0]) * nr + i, j))
    o_spec = pl.BlockSpec((1, tr, tn), lambda k, i, j, c: (k, i, j))

    def body(c_ref, g_ref, r_ref, o32_ref, o16_ref):
        s = g_ref[...] + r_ref[0]
        o32_ref[0] = s
        o16_ref[0] = s.astype(BF16)

    return pl.pallas_call(
        body,
        grid_spec=pltpu.PrefetchScalarGridSpec(num_scalar_prefetch=1, grid=(4, nr, nn), in_specs=[g_spec, o_spec],
                                               out_specs=[o_spec, o_spec]),
        out_shape=[SDS((4, hr, n), F32), SDS((4, hr, n), BF16)],
        compiler_params=_params(("parallel", "parallel", "parallel")), name=name)(c_arr, grad, got)


def _chip_exchange(parts16):
    nw = len(parts16)
    outs_shape = [SDS((3,) + p.shape[1:], BF16) for p in parts16]

    def body(*refs):
        ins, outs = refs[:nw], refs[nw:2 * nw]
        send, recv = refs[2 * nw:]
        x, y, c, others = _place()
        cps = []
        for w in range(nw):
            for p, (px, py) in enumerate(others):
                cp = pltpu.make_async_remote_copy(
                    src_ref=ins[w].at[2 * px + py], dst_ref=outs[w].at[p], send_sem=send.at[w, p],
                    recv_sem=recv.at[w, p], device_id=(px, py, c), device_id_type=MESH)
                cp.start()
                cps.append(cp)
        for cp in cps:
            cp.wait()

    return pl.pallas_call(
        body, in_specs=[HBM] * nw, out_specs=[HBM] * nw, out_shape=outs_shape,
        scratch_shapes=[pltpu.SemaphoreType.DMA((nw, 3)), pltpu.SemaphoreType.DMA((nw, 3))],
        compiler_params=pltpu.CompilerParams(has_side_effects=True), name="chip_exchange")(*parts16)


def _chip_add(part32, got16, chip_arr, name):
    _, hr, n = part32.shape
    tr, tn = _tile(hr, 256, 16), _tile(n, 1408)
    own = pl.BlockSpec((1, tr, tn), lambda i, j, k: (k[0], i, j))
    oth = pl.BlockSpec((3, tr, tn), lambda i, j, k: (0, i, j))

    def body(k_ref, own_ref, oth_ref, o_ref):
        o_ref[...] = ((own_ref[0] + oth_ref[0].astype(F32)) + oth_ref[1].astype(F32)) + oth_ref[2].astype(F32)

    return pl.pallas_call(
        body,
        grid_spec=pltpu.PrefetchScalarGridSpec(num_scalar_prefetch=1, grid=(hr // tr, n // tn), in_specs=[own, oth],
                                               out_specs=pl.BlockSpec((tr, tn), lambda i, j, k: (i, j))),
        out_shape=SDS((hr, n), F32), compiler_params=_params(("parallel", "parallel")), name=name)(chip_arr, part32, got16)


def _pair_share(halves):
    nw = len(halves)
    outs_shape = [SDS((2,) + h.shape, F32) for h in halves]

    def body(*refs):
        ins, outs = refs[:nw], refs[nw:2 * nw]
        send, recv, lsem = refs[2 * nw:]
        x, y, c, _ = _place()
        cps = []
        for w in range(nw):
            cp = pltpu.make_async_copy(ins[w], outs[w].at[c], lsem.at[w])
            cp.start()
            cps.append(cp)
            cp = pltpu.make_async_remote_copy(src_ref=ins[w], dst_ref=outs[w].at[c], send_sem=send.at[w],
                                              recv_sem=recv.at[w], device_id=(x, y, 1 - c), device_id_type=MESH)
            cp.start()
            cps.append(cp)
        for cp in cps:
            cp.wait()

    return pl.pallas_call(
        body, in_specs=[HBM] * nw, out_specs=[HBM] * nw, out_shape=outs_shape,
        scratch_shapes=[pltpu.SemaphoreType.DMA((nw,)), pltpu.SemaphoreType.DMA((nw,)), pltpu.SemaphoreType.DMA((nw,))],
        compiler_params=pltpu.CompilerParams(has_side_effects=True), name="pair_share")(*halves)


def _adamw_math(w, g, m, v):
    m = ADAM_B1 * m + (1.0 - ADAM_B1) * g
    v = ADAM_B2 * v + (1.0 - ADAM_B2) * (g * g)
    m_hat = m / (1.0 - ADAM_B1 ** ADAM_STEP)
    v_hat = v / (1.0 - ADAM_B2 ** ADAM_STEP)
    return -ADAM_LR * (m_hat / (jnp.sqrt(v_hat) + ADAM_EPS) + ADAM_WD * w), m, v


def _adamw(w, g, m, v, name):
    r, n = w.shape
    tr, tn = _tile(r, 256, 16), _tile(n, 1408)

    def body(w_ref, g_ref, m_ref, v_ref, d_ref, nm_ref, nv_ref):
        d_ref[...], nm_ref[...], nv_ref[...] = _adamw_math(w_ref[...], g_ref[...], m_ref[...], v_ref[...])

    tile = pl.BlockSpec((tr, tn), lambda i, j: (i, j))
    return pl.pallas_call(
        body, grid=(r // tr, n // tn), in_specs=[tile] * 4, out_specs=[tile] * 3, out_shape=[SDS((r, n), F32)] * 3,
        compiler_params=_params(("parallel", "parallel")), name=name)(w, g, m, v)


def _small_allreduce_adamw(g, w, m, v):
    length = g.shape[1]

    def body(g_ref, w_ref, m_ref, v_ref, gs_ref, d_ref, nm_ref, nv_ref, buf, send, recv):
        x, y, c = lax.axis_index("x"), lax.axis_index("y"), lax.axis_index("c")
        me = 4 * x + 2 * y + c
        buf[me] = g_ref[...]
        cps = []
        for d in range(1, 8):
            peer = (x ^ (d >> 2), y ^ ((d >> 1) & 1), c ^ (d & 1))
            cp = pltpu.make_async_remote_copy(src_ref=buf.at[me], dst_ref=buf.at[me], send_sem=send.at[d - 1],
                                              recv_sem=recv.at[d - 1], device_id=peer, device_id_type=MESH)
            cp.start()
            cps.append(cp)
        for cp in cps:
            cp.wait()
        total = buf[0]
        for d in range(1, 8):
            total = total + buf[d]
        gs_ref[...] = total
        d_ref[...], nm_ref[...], nv_ref[...] = _adamw_math(w_ref[...], total, m_ref[...], v_ref[...])

    vm = pl.BlockSpec(memory_space=pltpu.VMEM)
    return pl.pallas_call(
        body, in_specs=[vm] * 4, out_specs=[vm] * 4, out_shape=[SDS((1, length), F32)] * 4,
        scratch_shapes=[pltpu.VMEM((8, 1, length), F32), pltpu.SemaphoreType.DMA((7,)), pltpu.SemaphoreType.DMA((7,))],
        compiler_params=pltpu.CompilerParams(has_side_effects=True), name="small_allreduce_adamw")(g, w, m, v)


def kernel(x, w_in, b_gate, norm_mix, norm_ffn, hgrn_lb_logits, hgrn_out_gain, q_gain, k_gain, rel_bias, w_proj_a, w_proj_b, w_out, w_ffn_in, w_ffn_out, loss_target, m_w_in, m_b_gate, m_norm_mix, m_norm_ffn, m_hgrn_lb_logits, m_hgrn_out_gain, m_q_gain, m_k_gain, m_rel_bias, m_w_proj_a, m_w_proj_b, m_w_out, m_w_ffn_in, m_w_ffn_out, v_w_in, v_b_gate, v_norm_mix, v_norm_ffn, v_hgrn_lb_logits, v_hgrn_out_gain, v_q_gain, v_k_gain, v_rel_bias, v_w_proj_a, v_w_proj_b, v_w_out, v_w_ffn_in, v_w_ffn_out):
    t, d = x.shape[1], x.shape[2]
    d_a = hgrn_out_gain.shape[1]
    h_a = d_a // HEAD
    h_b = rel_bias.shape[1]
    d_b = h_b * HEAD
    n_in = 4 * w_in.shape[2]
    x0 = x.reshape(t, d)
    target = loss_target.reshape(t, d)

    big = [w_in[0], w_proj_a[0], w_proj_b[0], w_out[0], w_ffn_in[0], w_ffn_out[0]]
    big_m = [m_w_in[0], m_w_proj_a[0], m_w_proj_b[0], m_w_out[0], m_w_ffn_in[0], m_w_ffn_out[0]]
    big_v = [v_w_in[0], v_w_proj_a[0], v_w_proj_b[0], v_w_out[0], v_w_ffn_in[0], v_w_ffn_out[0]]
    names = ["w_in", "w_proj_a", "w_proj_b", "w_out", "w_ffn_in", "w_ffn_out"]
    kinds = ["col", "col", "col", "row", "col", "row"]
    shard_shapes = [w.shape for w in big]
    shards16 = [_cast_bf16(w, "cast_" + nm) for w, nm in zip(big, names)]
    wg_in, wg_pa, wg_pb, wg_out, wg_fin, wg_fout = _gather_weights(shards16, kinds)

    h1, r1 = _rmsnorm_fwd(x0, norm_mix, "rmsnorm_mix")
    proj = _matmul(h1, wg_in, name="proj_in")
    y_a, o_pre, states = _hgrn_fwd(proj, hgrn_lb_logits, hgrn_out_gain, h_a)
    rb = jnp.pad(rel_bias[0], ((0, 0), (0, REL_LANES - N_REL)))
    bias = _relbias_expand(rb).transpose(1, 0, 2)
    col_b = 4 * d_a // HEAD
    y_b = _attn_fwd(proj, q_gain, k_gain, bias, h_b, col_b)
    pa = _matmul(y_a, wg_pa, name="proj_a")
    pb = _matmul(y_b, wg_pb, name="proj_b")
    gate_off = 4 * d_a + 3 * d_b
    merged = _merge_fwd(proj, b_gate, pa, pb, gate_off)
    x2 = _matmul(merged, wg_out, res=x0, name="out_proj")
    h2, r2 = _rmsnorm_fwd(x2, norm_ffn, "rmsnorm_ffn")
    gu = _matmul(h2, wg_fin, name="ffn_in")
    act = _swiglu_fwd(gu)
    y = _matmul(act, wg_fout, res=x2, name="ffn_out")
    dy, loss_part = _loss_head(y, target)

    g_fout = _matmul(act, dy, ta=True, name="dw_ffn_out")
    dact = _matmul(dy, wg_fout, tb=True, name="d_act")
    dgu = _swiglu_bwd(dact, gu)
    g_fin = _matmul(h2, dgu, ta=True, name="dw_ffn_in")
    dh2 = _matmul(dgu, wg_fin, tb=True, name="d_h2")
    dx2, g_norm_ffn = _rmsnorm_bwd(dh2, x2, r2, norm_ffn, dy, "rmsnorm_ffn_bwd")
    g_out = _matmul(merged, dx2, ta=True, name="dw_out")
    dmerged = _matmul(dx2, wg_out, tb=True, name="d_merged")
    dp_ab, dproj, g_bgate = _merge_bwd(dmerged, proj, b_gate, pa, pb, gate_off)
    g_pa = _matmul(y_a, dp_ab[0], ta=True, name="dw_proj_a")
    g_pb = _matmul(y_b, dp_ab[1], ta=True, name="dw_proj_b")
    dy_a = _matmul(dp_ab[0], wg_pa, tb=True, name="d_y_a")
    dy_b = _matmul(dp_ab[1], wg_pb, tb=True, name="d_y_b")
    dproj, dbias, g_qg, g_kg = _attn_bwd(dproj, proj, q_gain, k_gain, bias, dy_b, h_b, col_b)
    dproj, g_lb, g_gain = _hgrn_bwd(dproj, proj, o_pre, states, dy_a, hgrn_lb_logits, hgrn_out_gain, h_a)
    g_rb = _relbias_reduce(dbias.transpose(1, 0, 2))[:, :N_REL]
    g_in = _matmul(h1, dproj, ta=True, name="dw_in")
    dh1 = _matmul(dproj, wg_in, tb=True, name="d_h1")
    dx, g_norm_mix = _rmsnorm_bwd(dh1, x0, r1, norm_mix, dx2, "rmsnorm_mix_bwd")

    c_arr = lax.axis_index("c").astype(jnp.int32).reshape(1)
    chip_arr = (2 * lax.axis_index("x") + lax.axis_index("y")).astype(jnp.int32).reshape(1)
    grads = [g_in, g_pa, g_pb, g_out, g_fin, g_fout]
    got_pair = _pair_exchange(grads, kinds, shard_shapes)
    parts = [_pair_add(g, r, kind, ss, c_arr, "pair_add_" + nm)
             for g, r, kind, ss, nm in zip(grads, got_pair, kinds, shard_shapes, names)]
    got_chip = _chip_exchange([p16 for _, p16 in parts])
    halves = [_chip_add(p32, r, chip_arr, "chip_add_" + nm) for (p32, _), r, nm in zip(parts, got_chip, names)]
    g_big = [f.reshape(ss) for f, ss in zip(_pair_share(halves), shard_shapes)]
    upd = [_adamw(w, g, m, v, "adamw_" + nm) for w, g, m, v, nm in zip(big, g_big, big_m, big_v, names)]

    small_w = [b_gate, norm_mix, norm_ffn, hgrn_lb_logits, hgrn_out_gain, q_gain, k_gain, rel_bias]
    small_m = [m_b_gate, m_norm_mix, m_norm_ffn, m_hgrn_lb_logits, m_hgrn_out_gain, m_q_gain, m_k_gain, m_rel_bias]
    small_v = [v_b_gate, v_norm_mix, v_norm_ffn, v_hgrn_lb_logits, v_hgrn_out_gain, v_q_gain, v_k_gain, v_rel_bias]
    small_g = [g_bgate, g_norm_mix, g_norm_ffn, g_lb, g_gain, g_qg, g_kg, g_rb]
    sizes = [w.size for w in small_w]
    length = -(-(sum(sizes) + 1) // 128) * 128

    def pack(parts_):
        flat = jnp.concatenate([p.reshape(1, -1) for p in parts_], axis=1)
        return jnp.pad(flat, ((0, 0), (0, length - flat.shape[1])))

    one = jnp.ones((1, 1), F32)
    packed = _small_allreduce_adamw(pack(small_g + [loss_part]), pack(small_w + [one]), pack(small_m + [one]),
                                    pack(small_v + [one]))

    def unpack(vec):
        out, at = [], 0
        for w, n in zip(small_w, sizes):
            out.append(vec[0, at:at + n].reshape(w.shape))
            at += n
        return out, vec[0, at]

    (sg, loss), (sd, _), (sm, _), (sv, _) = [unpack(p) for p in packed]

    def ordered(small, bigs):
        bigs = [b[None] for b in bigs]
        return [bigs[0]] + small + bigs[1:]

    return (loss, dx.reshape(x.shape), *ordered(sg, g_big), *ordered(sd, [u[0] for u in upd]),
            *ordered(sm, [u[1] for u in upd]), *ordered(sv, [u[2] for u in upd]))
```

```python
import functools

import jax
import jax.numpy as jnp
from jax import lax
from jax.experimental import pallas as pl
from jax.experimental.pallas import tpu as pltpu

F32 = jnp.float32
BF16 = jnp.bfloat16
SDS = jax.ShapeDtypeStruct
MESH = pl.DeviceIdType.MESH
HIGHEST = lax.Precision.HIGHEST

CHUNK = 64
SUB = 16
HEAD = 128
N_PAST = 8
BAND = (N_PAST + 1) * CHUNK
PAD = N_PAST * CHUNK
REL_FUTURE = CHUNK - 1
REL_PAST = 2 * CHUNK - 1
N_REL = REL_FUTURE + REL_PAST + 1
REL_LANES = 256
EPS = 1e-6
EXP_CLAMP = 80.0

ADAM_LR = 0.001
ADAM_B1 = 0.9
ADAM_B2 = 0.999
ADAM_EPS = 1e-08
ADAM_WD = 0.01
ADAM_STEP = 10

VMEM_LIMIT = 56 * 1024 * 1024

HBM = pl.BlockSpec(memory_space=pltpu.HBM)
ANY = pl.BlockSpec(memory_space=pl.ANY)
SEM = pl.BlockSpec(memory_space=pltpu.SEMAPHORE)

NT = (((1,), (1,)), ((), ()))
TN = (((0,), (0,)), ((), ()))
NN = (((1,), (0,)), ((), ()))


def _params(sem=None, **kw):
    return pltpu.CompilerParams(dimension_semantics=sem, vmem_limit_bytes=VMEM_LIMIT, **kw)


def _tile(n, pref, unit=128):
    if n <= pref:
        return n
    t = pref - pref % unit
    while n % t:
        t -= unit
    return t


def _sigmoid(x):
    return 1.0 / (1.0 + jnp.exp(-x))


def _dsilu(x, s):
    return s * (1.0 + x * (1.0 - s))


def _bdot(a, b, dims=NN):
    return lax.dot_general(a.astype(BF16), b.astype(BF16), dims, preferred_element_type=F32)


def _split(a):
    hi = a.astype(BF16)
    return hi, (a - hi.astype(F32)).astype(BF16)


def _dot3(a, b, dims):
    dot = lambda u, v: lax.dot_general(u, v, dims, preferred_element_type=F32)
    return dot(a[0], b[1]) + dot(a[1], b[0]) + dot(a[0], b[0])


def _fdot(a, b):
    return lax.dot_general(a, b, NN, precision=HIGHEST, preferred_element_type=F32)


def _matmul(a, b, *, ta=False, tb=False, res=None, out_dtype=F32, name, tm=1024, tn=512, tk=1024, deps=()):
    m, k = (a.shape[1], a.shape[0]) if ta else a.shape
    n = b.shape[0] if tb else b.shape[1]
    tm, tn, tk = _tile(m, tm), _tile(n, tn), _tile(k, tk)
    nk = k // tk
    dims = ((((0,) if ta else (1,)), ((1,) if tb else (0,))), ((), ()))

    def body(*refs):
        refs = refs[:len(refs) - 2 - len(deps)] + refs[len(refs) - 2:]
        if res is None:
            a_ref, b_ref, o_ref, acc_ref = refs
            r_ref = None
        else:
            a_ref, b_ref, r_ref, o_ref, acc_ref = refs
        kk = pl.program_id(2)

        @pl.when(kk == 0)
        def _():
            acc_ref[...] = jnp.zeros_like(acc_ref)

        acc_ref[...] += lax.dot_general(a_ref[...].astype(BF16), b_ref[...].astype(BF16), dims,
                                        preferred_element_type=F32)

        @pl.when(kk == nk - 1)
        def _():
            out = acc_ref[...]
            if r_ref is not None:
                out = out + r_ref[...]
            o_ref[...] = out.astype(o_ref.dtype)

    a_spec = pl.BlockSpec((tk, tm), lambda i, j, l: (l, i)) if ta else pl.BlockSpec((tm, tk), lambda i, j, l: (i, l))
    b_spec = pl.BlockSpec((tn, tk), lambda i, j, l: (j, l)) if tb else pl.BlockSpec((tk, tn), lambda i, j, l: (l, j))
    o_spec = pl.BlockSpec((tm, tn), lambda i, j, l: (i, j))
    in_specs = [a_spec, b_spec] + ([o_spec] if res is not None else []) + [ANY] * len(deps)
    args = (a, b) + ((res,) if res is not None else ()) + tuple(deps)
    return pl.pallas_call(
        body, grid=(m // tm, n // tn, nk), in_specs=in_specs, out_specs=o_spec,
        out_shape=SDS((m, n), out_dtype), scratch_shapes=[pltpu.VMEM((tm, tn), F32)],
        compiler_params=_params(("parallel", "parallel", "arbitrary")), name=name)(*args)


def _cast_into_full(w, kind, pos, name):
    r, n = w.shape
    tr = _tile(r, 512, 16)
    nr = r // tr
    if kind == "col":
        shape, o_spec = (r, 4 * n), pl.BlockSpec((tr, n), lambda i, p: (i, p[0]))
    else:
        shape, o_spec = (4 * r, n), pl.BlockSpec((tr, n), lambda i, p: (p[0] * nr + i, 0))

    def body(p_ref, w_ref, o_ref):
        o_ref[...] = w_ref[...].astype(BF16)

    return pl.pallas_call(
        body,
        grid_spec=pltpu.PrefetchScalarGridSpec(num_scalar_prefetch=1, grid=(nr,),
                                               in_specs=[pl.BlockSpec((tr, n), lambda i, p: (i, 0))], out_specs=o_spec),
        out_shape=SDS(shape, BF16), compiler_params=_params(("parallel",)), name=name)(pos, w)


def _rmsnorm_fwd(x, gain, name):
    t, d = x.shape
    tm = _tile(t, 256)

    def body(x_ref, g_ref, h_ref, r_ref):
        xv = x_ref[...]
        r = lax.rsqrt(jnp.mean(xv * xv, axis=-1, keepdims=True) + EPS)
        h_ref[...] = (xv * r * g_ref[...]).astype(BF16)
        r_ref[...] = r

    return pl.pallas_call(
        body, grid=(t // tm,),
        in_specs=[pl.BlockSpec((tm, d), lambda i: (i, 0)), pl.BlockSpec((1, d), lambda i: (0, 0))],
        out_specs=[pl.BlockSpec((tm, d), lambda i: (i, 0)), pl.BlockSpec((tm, 1), lambda i: (i, 0))],
        out_shape=[SDS((t, d), BF16), SDS((t, 1), F32)], compiler_params=_params(("parallel",)), name=name)(x, gain)


def _rmsnorm_bwd(dh, x, r, gain, dres, name):
    t, d = x.shape
    tm = _tile(t, 256)

    def body(dh_ref, x_ref, r_ref, g_ref, dres_ref, dx_ref, dg_ref):
        @pl.when(pl.program_id(0) == 0)
        def _():
            dg_ref[...] = jnp.zeros_like(dg_ref)

        dhv, xv, rv = dh_ref[...], x_ref[...], r_ref[...]
        dg_ref[...] += jnp.sum(dhv * (xv * rv), axis=0, keepdims=True)
        u = dhv * g_ref[...]
        dx_ref[...] = dres_ref[...] + rv * u - xv * (rv * rv * rv) * jnp.mean(u * xv, axis=-1, keepdims=True)

    row = pl.BlockSpec((tm, d), lambda i: (i, 0))
    vec = pl.BlockSpec((1, d), lambda i: (0, 0))
    return pl.pallas_call(
        body, grid=(t // tm,), in_specs=[row, row, pl.BlockSpec((tm, 1), lambda i: (i, 0)), vec, row],
        out_specs=[row, vec], out_shape=[SDS((t, d), F32), SDS((1, d), F32)],
        compiler_params=_params(("arbitrary",)), name=name)(dh, x, r, gain, dres)


def _merge_fwd(proj, b_gate, pa, pb, off):
    t, d = pa.shape
    tm, tc = _tile(t, 512), _tile(d, 512)
    nj = d // tc
    oa, ob = off // tc, off // tc + nj

    def body(la_ref, lb_ref, ba_ref, bb_ref, pa_ref, pb_ref, o_ref):
        ga = _sigmoid(la_ref[...] + ba_ref[...])
        gb = _sigmoid(lb_ref[...] + bb_ref[...])
        o_ref[...] = (ga * pa_ref[...] + gb * pb_ref[...]).astype(BF16)

    tile = pl.BlockSpec((tm, tc), lambda i, j: (i, j))
    return pl.pallas_call(
        body, grid=(t // tm, nj),
        in_specs=[pl.BlockSpec((tm, tc), lambda i, j: (i, oa + j)), pl.BlockSpec((tm, tc), lambda i, j: (i, ob + j)),
                  pl.BlockSpec((1, tc), lambda i, j: (0, j)), pl.BlockSpec((1, tc), lambda i, j: (0, nj + j)), tile, tile],
        out_specs=tile, out_shape=SDS((t, d), BF16), compiler_params=_params(("parallel", "parallel")),
        name="merge_fwd")(proj, proj, b_gate, b_gate, pa, pb)


def _merge_bwd(dmerged, proj, b_gate, pa, pb, off):
    t, d = pa.shape
    tm, tc = _tile(t, 512), _tile(d, 512)
    nj, ni = d // tc, t // tm
    o0 = off // tc

    def body(dm_ref, l_ref, b_ref, pa_ref, pb_ref, dp_ref, dl_ref, db_ref):
        s, i = pl.program_id(0), pl.program_id(2)
        p = jnp.where(s == 0, pa_ref[...], pb_ref[...])
        g = _sigmoid(l_ref[...] + b_ref[...])
        dm = dm_ref[...]
        dp_ref[0] = (dm * g).astype(BF16)
        dl = dm * p * g * (1.0 - g)
        dl_ref[...] = dl

        @pl.when(i == 0)
        def _():
            db_ref[...] = jnp.zeros_like(db_ref)

        db_ref[...] += jnp.sum(dl, axis=0, keepdims=True)

    tile = pl.BlockSpec((tm, tc), lambda s, j, i: (i, j))
    return pl.pallas_call(
        body, grid=(2, nj, ni),
        in_specs=[tile, pl.BlockSpec((tm, tc), lambda s, j, i: (i, o0 + s * nj + j)),
                  pl.BlockSpec((1, tc), lambda s, j, i: (0, s * nj + j)), tile, tile],
        out_specs=[pl.BlockSpec((1, tm, tc), lambda s, j, i: (s, i, j)),
                   pl.BlockSpec((tm, tc), lambda s, j, i: (i, o0 + s * nj + j)),
                   pl.BlockSpec((1, tc), lambda s, j, i: (0, s * nj + j))],
        out_shape=[SDS((2, t, d), BF16), SDS(proj.shape, F32), SDS((1, 2 * d), F32)],
        compiler_params=_params(("arbitrary", "arbitrary", "arbitrary")),
        name="merge_bwd")(dmerged, proj, b_gate, pa, pb)


def _swiglu_fwd(gu):
    t, f2 = gu.shape
    f = f2 // 2
    tm, tc = _tile(t, 512), _tile(f, 512)
    nj = f // tc

    def body(g_ref, u_ref, o_ref):
        g = g_ref[...]
        o_ref[...] = (g * _sigmoid(g) * u_ref[...]).astype(BF16)

    return pl.pallas_call(
        body, grid=(t // tm, nj),
        in_specs=[pl.BlockSpec((tm, tc), lambda i, j: (i, j)), pl.BlockSpec((tm, tc), lambda i, j: (i, nj + j))],
        out_specs=pl.BlockSpec((tm, tc), lambda i, j: (i, j)), out_shape=SDS((t, f), BF16),
        compiler_params=_params(("parallel", "parallel")), name="swiglu_fwd")(gu, gu)


def _swiglu_bwd(dact, gu):
    t, f2 = gu.shape
    f = f2 // 2
    tm, tc = _tile(t, 512), _tile(f, 512)
    nj = f // tc

    def body(d_ref, g_ref, u_ref, o_ref):
        s = pl.program_id(1)
        g, dv = g_ref[...], d_ref[...]
        sg = _sigmoid(g)

        @pl.when(s == 0)
        def _():
            o_ref[...] = dv * u_ref[...] * _dsilu(g, sg)

        @pl.when(s == 1)
        def _():
            o_ref[...] = dv * (g * sg)

    return pl.pallas_call(
        body, grid=(t // tm, 2, nj),
        in_specs=[pl.BlockSpec((tm, tc), lambda i, s, j: (i, j)), pl.BlockSpec((tm, tc), lambda i, s, j: (i, j)),
                  pl.BlockSpec((tm, tc), lambda i, s, j: (i, nj + j))],
        out_specs=pl.BlockSpec((tm, tc), lambda i, s, j: (i, s * nj + j)), out_shape=SDS((t, f2), F32),
        compiler_params=_params(("parallel", "parallel", "parallel")), name="swiglu_bwd")(dact, gu, gu)


def _loss_head(y, target):
    t, d = y.shape
    tm = _tile(t, 256)

    def body(y_ref, t_ref, dy_ref, l_ref):
        @pl.when(pl.program_id(0) == 0)
        def _():
            l_ref[...] = jnp.zeros_like(l_ref)

        e = y_ref[...] - t_ref[...]
        dy_ref[...] = e * (1.0 / d)
        l_ref[...] += 0.5 * jnp.sum(jnp.mean(e * e, axis=-1, keepdims=True), axis=0, keepdims=True)

    row = pl.BlockSpec((tm, d), lambda i: (i, 0))
    return pl.pallas_call(
        body, grid=(t // tm,), in_specs=[row, row], out_specs=[row, pl.BlockSpec((1, 1), lambda i: (0, 0))],
        out_shape=[SDS((t, d), F32), SDS((1, 1), F32)], compiler_params=_params(("arbitrary",)),
        name="loss_head")(y, target)


def _rel_onehot(qi):
    p = lax.broadcasted_iota(jnp.int32, (REL_LANES, BAND), 1)
    r = lax.broadcasted_iota(jnp.int32, (REL_LANES, BAND), 0)
    idx = jnp.clip(qi + PAD - p, -REL_FUTURE, REL_PAST) + REL_FUTURE
    return (idx == r).astype(F32)


def _relbias_expand(rb):
    h = rb.shape[0]

    def body(rb_ref, o_ref):
        def step(qi, _):
            o_ref[qi] = _fdot(rb_ref[...], _rel_onehot(qi))
            return 0

        lax.fori_loop(0, CHUNK, step, 0)

    return pl.pallas_call(body, out_shape=SDS((CHUNK, h, BAND), F32), compiler_params=_params(),
                          name="relbias_expand")(rb)


def _relbias_reduce(dbias):
    h = dbias.shape[1]

    def body(db_ref, o_ref):
        def step(qi, acc):
            return acc + lax.dot_general(db_ref[qi], _rel_onehot(qi), NT, precision=HIGHEST,
                                         preferred_element_type=F32)

        o_ref[...] = lax.fori_loop(0, CHUNK, step, jnp.zeros((h, REL_LANES), F32))

    return pl.pallas_call(body, out_shape=SDS((h, REL_LANES), F32), compiler_params=_params(),
                          name="relbias_reduce")(dbias)


def _lower_bound(l_ref):
    l0, l1 = l_ref[0:1, :], l_ref[1:2, :]
    m = jnp.maximum(l0, l1)
    e0, e1 = jnp.exp(l0 - m), jnp.exp(l1 - m)
    return e0 / (e0 + e1)


def _tri(lower):
    r = lax.broadcasted_iota(jnp.int32, (CHUNK, CHUNK), 0)
    c = lax.broadcasted_iota(jnp.int32, (CHUNK, CHUNK), 1)
    return r >= c if lower else r <= c


def _hgrn_intra(qs, kk, b_s):
    rows = lax.broadcasted_iota(jnp.int32, (CHUNK, HEAD), 0)
    tr = lax.broadcasted_iota(jnp.int32, (SUB, CHUNK), 0)
    tc = lax.broadcasted_iota(jnp.int32, (SUB, CHUNK), 1)
    b = b_s[...]
    out = []
    for i in range(CHUNK // SUB):
        lo = i * SUB
        ref = jnp.zeros((1, HEAD), F32) if i == 0 else b_s[lo - 1:lo, :]
        eq = jnp.exp(b[lo:lo + SUB] - ref)
        qt = _split(qs[lo:lo + SUB] * eq)
        e = jnp.where(rows < lo + SUB, jnp.exp(jnp.minimum(ref - b, EXP_CLAMP)), 0.0)
        kt = _split(kk * e)
        a = jnp.where(tc <= tr + lo, _dot3(qt, kt, NT), 0.0)
        out.append((eq, qt, e, kt, a))
    return out


def _hgrn_fwd(proj, lb_logits, gain, n_heads):
    t = proj.shape[0]
    nc = t // CHUNK
    da = n_heads * HEAD

    def body(q_ref, f_ref, i_ref, g_ref, l_ref, gain_ref, y_ref, o_ref, st_ref, state, b_s):
        state[...] = jnp.zeros_like(state)
        lb = _lower_bound(l_ref)
        gain_v = gain_ref[...]
        tril = _tri(True).astype(F32)

        def chunk(j, _):
            sl = pl.ds(pl.multiple_of(j * CHUNK, CHUNK), CHUNK)
            fg = lb + (1.0 - lb) * _sigmoid(f_ref[sl, :])
            kk = 1.0 - fg
            qv = q_ref[sl, :]
            qs = qv * _sigmoid(qv)
            vb = i_ref[sl, :].astype(BF16)
            b = _fdot(tril, jnp.log(fg))
            b_s[...] = b
            s_in = state[...]
            st_ref[0, j] = s_in
            o = _bdot(qs * jnp.exp(b), s_in, NT)
            a = jnp.concatenate([blk[4] for blk in _hgrn_intra(qs, kk, b_s)], axis=0)
            o = o + _bdot(a, vb)
            o_ref[sl, :] = o
            bl = b_s[CHUNK - 1:CHUNK, :]
            state[...] = s_in * jnp.exp(bl) + _bdot(vb, kk * jnp.exp(bl - b), TN)
            rr = lax.rsqrt(jnp.mean(o * o, axis=-1, keepdims=True) + EPS)
            gv = g_ref[sl, :]
            y_ref[sl, :] = (o * rr * gain_v * (gv * _sigmoid(gv))).astype(BF16)
            return 0

        lax.fori_loop(0, nc, chunk, 0)

    col = lambda k: pl.BlockSpec((t, HEAD), lambda h: (0, k * n_heads + h))
    vec = pl.BlockSpec((1, HEAD), lambda h: (0, h))
    return pl.pallas_call(
        body, grid=(n_heads,),
        in_specs=[col(0), col(1), col(2), col(3), pl.BlockSpec((2, HEAD), lambda h: (0, h)), vec],
        out_specs=[pl.BlockSpec((t, HEAD), lambda h: (0, h)), pl.BlockSpec((t, HEAD), lambda h: (0, h)),
                   pl.BlockSpec((1, nc, HEAD, HEAD), lambda h: (h, 0, 0, 0))],
        out_shape=[SDS((t, da), BF16), SDS((t, da), F32), SDS((n_heads, nc, HEAD, HEAD), F32)],
        scratch_shapes=[pltpu.VMEM((HEAD, HEAD), F32), pltpu.VMEM((CHUNK, HEAD), F32)],
        compiler_params=_params(("parallel",)), name="hgrn_fwd")(proj, proj, proj, proj, lb_logits, gain)


def _hgrn_bwd(dproj, proj, o_pre, states, dy, lb_logits, gain, n_heads, deps=()):
    t = proj.shape[0]
    nc = t // CHUNK
    da = n_heads * HEAD

    def body(*refs):
        (q_ref, f_ref, i_ref, g_ref, o_ref, st_ref, dy_ref, l_ref, gain_ref,
         dproj_ref, dl_ref, dgain_ref, res, dstate, b_s) = refs[1 + len(deps):]

        @pl.when(pl.program_id(1) == 0)
        def _():
            dstate[...] = jnp.zeros_like(dstate)
            lb = _lower_bound(l_ref)
            gain_v = gain_ref[...]
            tril, triu = _tri(True), _tri(False).astype(F32)
            last = lax.broadcasted_iota(jnp.int32, (CHUNK, HEAD), 0) == CHUNK - 1

            def chunk(jj, carry):
                dlb_acc, dgain_acc = carry
                j = nc - 1 - jj
                sl = pl.ds(pl.multiple_of(j * CHUNK, CHUNK), CHUNK)
                sg = _sigmoid(f_ref[sl, :])
                fg = lb + (1.0 - lb) * sg
                kk = 1.0 - fg
                qv = q_ref[sl, :]
                sq = _sigmoid(qv)
                qs = qv * sq
                vb = i_ref[sl, :].astype(BF16)
                gv = g_ref[sl, :]
                sgg = _sigmoid(gv)
                silg = gv * sgg
                b = _fdot(tril.astype(F32), jnp.log(fg))
                b_s[...] = b
                o = o_ref[sl, :]
                dyv = dy_ref[sl, :]
                rr = lax.rsqrt(jnp.mean(o * o, axis=-1, keepdims=True) + EPS)
                on = o * rr
                dgain_acc = dgain_acc + jnp.sum(dyv * on * silg, axis=0, keepdims=True)
                dg = dyv * on * gain_v * _dsilu(gv, sgg)
                don = dyv * gain_v * silg
                do = (rr * don - o * (rr * rr * rr) * jnp.mean(don * o, axis=-1, keepdims=True)).astype(BF16)
                s_in = st_ref[0, j]
                ds_out = dstate[...]
                eb = jnp.exp(b)
                bl = b_s[CHUNK - 1:CHUNK, :]
                ebl = jnp.exp(bl)
                ekd = jnp.exp(bl - b)
                dq = _bdot(do, s_in) * eb
                da_m = jnp.where(tril, _bdot(do, vb, NT), 0.0)
                a_rows, dq_rows = [], []
                dk = jnp.zeros((CHUNK, HEAD), F32)
                for i, (eq, qt, e, kt, a) in enumerate(_hgrn_intra(qs, kk, b_s)):
                    da_i = _split(da_m[i * SUB:(i + 1) * SUB])
                    a_rows.append(a)
                    dq_rows.append(eq * _dot3(da_i, kt, NN))
                    dk = dk + e * _dot3(da_i, qt, TN)
                dq = dq + jnp.concatenate(dq_rows, axis=0)
                dv = _bdot(jnp.concatenate(a_rows, axis=0), do, TN) + _bdot(kk * ekd, ds_out, NT)
                dk_state = ekd * _bdot(vb, ds_out)
                dk = dk + dk_state
                db = qs * dq - kk * dk
                extra = (jnp.sum(kk * dk_state, axis=0, keepdims=True)
                         + ebl * jnp.sum(s_in * ds_out, axis=0, keepdims=True))
                db = db + jnp.where(last, extra, 0.0)
                dfg = _fdot(triu, db) / fg - dk
                dlb_acc = dlb_acc + jnp.sum(dfg * (1.0 - sg), axis=0, keepdims=True)
                dstate[...] = ds_out * ebl + _bdot(do, qs * eb, TN)
                res[0, sl, :] = dq * _dsilu(qv, sq)
                res[1, sl, :] = dfg * (1.0 - lb) * sg * (1.0 - sg)
                res[2, sl, :] = dv
                res[3, sl, :] = dg
                return dlb_acc, dgain_acc

            zero = jnp.zeros((1, HEAD), F32)
            dlb, dgain = lax.fori_loop(0, nc, chunk, (zero, zero))
            dgain_ref[...] = dgain
            dl0 = dlb * lb * (1.0 - lb)
            dl_ref[0:1, :] = dl0
            dl_ref[1:2, :] = -dl0

        dproj_ref[...] = res[pl.program_id(1)]

    col = lambda k: pl.BlockSpec((t, HEAD), lambda h, p: (0, k * n_heads + h))
    head = pl.BlockSpec((t, HEAD), lambda h, p: (0, h))
    vec = pl.BlockSpec((1, HEAD), lambda h, p: (0, h))
    return pl.pallas_call(
        body, grid=(n_heads, 4),
        in_specs=[ANY] * (1 + len(deps)) + [col(0), col(1), col(2), col(3), head,
                  pl.BlockSpec((1, nc, HEAD, HEAD), lambda h, p: (h, 0, 0, 0)),
                  head, pl.BlockSpec((2, HEAD), lambda h, p: (0, h)), vec],
        out_specs=[pl.BlockSpec((t, HEAD), lambda h, p: (0, p * n_heads + h)),
                   pl.BlockSpec((2, HEAD), lambda h, p: (0, h)), vec],
        out_shape=[SDS(dproj.shape, F32), SDS((2, da), F32), SDS((1, da), F32)],
        scratch_shapes=[pltpu.VMEM((4, t, HEAD), F32), pltpu.VMEM((HEAD, HEAD), F32), pltpu.VMEM((CHUNK, HEAD), F32)],
        input_output_aliases={0: 0}, compiler_params=_params(("arbitrary", "arbitrary")),
        name="hgrn_bwd")(dproj, *deps, proj, proj, proj, proj, o_pre, states, dy, lb_logits, gain)


ROWS = 256


def _head_norm(x_ref, gain, dst, dst_off, t):
    def step(i, _):
        sl = pl.ds(pl.multiple_of(i * ROWS, ROWS), ROWS)
        xv = x_ref[sl, :]
        r = lax.rsqrt(jnp.mean(xv * xv, axis=-1, keepdims=True) + EPS)
        dst[pl.ds(pl.multiple_of(dst_off + i * ROWS, ROWS), ROWS), :] = (xv * r * gain).astype(BF16)
        return 0

    lax.fori_loop(0, t // ROWS, step, 0)


def _head_norm_bwd(x_ref, gain, dn_ref, dn_off, out, slot, t):
    def step(i, acc):
        sl = pl.ds(pl.multiple_of(i * ROWS, ROWS), ROWS)
        xv = x_ref[sl, :]
        dn = dn_ref[pl.ds(pl.multiple_of(dn_off + i * ROWS, ROWS), ROWS), :]
        r = lax.rsqrt(jnp.mean(xv * xv, axis=-1, keepdims=True) + EPS)
        u = dn * gain
        out[slot, sl, :] = r * u - xv * (r * r * r) * jnp.mean(u * xv, axis=-1, keepdims=True)
        return acc + jnp.sum(dn * (xv * r), axis=0, keepdims=True)

    return lax.fori_loop(0, t // ROWS, step, jnp.zeros((1, HEAD), F32))


def _attn_probs(qn, kpad, bias_ref, n):
    qc = qn[pl.ds(pl.multiple_of(n * CHUNK, CHUNK), CHUNK), :]
    band = pl.ds(pl.multiple_of(n * CHUNK, CHUNK), BAND)
    s = lax.dot_general(qc, kpad[band, :], NT, preferred_element_type=F32) * (HEAD ** -0.5) + bias_ref[0]
    col = lax.broadcasted_iota(jnp.int32, (CHUNK, BAND), 1)
    s = jnp.where(col >= PAD - n * CHUNK, s, -jnp.inf)
    p = jnp.exp(s - jnp.max(s, axis=-1, keepdims=True))
    return qc, band, p / jnp.sum(p, axis=-1, keepdims=True)


def _attn_fwd(proj, q_gain, k_gain, bias, n_heads, col0):
    t = proj.shape[0]
    nc = t // CHUNK

    def body(q_ref, k_ref, v_ref, qg_ref, kg_ref, bias_ref, y_ref, qn, kpad, vpad):
        kpad[0:PAD, :] = jnp.zeros((PAD, HEAD), BF16)
        vpad[0:PAD, :] = jnp.zeros((PAD, HEAD), BF16)
        _head_norm(q_ref, qg_ref[...], qn, 0, t)
        _head_norm(k_ref, kg_ref[...], kpad, PAD, t)

        def copy_v(i, _):
            vpad[pl.ds(pl.multiple_of(PAD + i * ROWS, ROWS), ROWS), :] = v_ref[
                pl.ds(pl.multiple_of(i * ROWS, ROWS), ROWS), :].astype(BF16)
            return 0

        lax.fori_loop(0, t // ROWS, copy_v, 0)

        def chunk(n, _):
            _, band, p = _attn_probs(qn, kpad, bias_ref, n)
            y_ref[pl.ds(pl.multiple_of(n * CHUNK, CHUNK), CHUNK), :] = _bdot(p, vpad[band, :]).astype(BF16)
            return 0

        lax.fori_loop(0, nc, chunk, 0)

    col = lambda k: pl.BlockSpec((t, HEAD), lambda h: (0, col0 + k * n_heads + h))
    vec = pl.BlockSpec((1, HEAD), lambda h: (0, 0))
    return pl.pallas_call(
        body, grid=(n_heads,),
        in_specs=[col(0), col(1), col(2), vec, vec, pl.BlockSpec((1, CHUNK, BAND), lambda h: (h, 0, 0))],
        out_specs=pl.BlockSpec((t, HEAD), lambda h: (0, h)), out_shape=SDS((t, n_heads * HEAD), BF16),
        scratch_shapes=[pltpu.VMEM((t, HEAD), BF16), pltpu.VMEM((t + PAD, HEAD), BF16), pltpu.VMEM((t + PAD, HEAD), BF16)],
        compiler_params=_params(("parallel",)), name="attn_fwd")(proj, proj, proj, q_gain, k_gain, bias)


def _attn_bwd(dproj, proj, q_gain, k_gain, bias, dy, n_heads, col0, deps=()):
    t = proj.shape[0]
    nc = t // CHUNK

    def body(*refs):
        (q_ref, k_ref, v_ref, qg_ref, kg_ref, bias_ref, dy_ref,
         dproj_ref, dbias_ref, dqg_ref, dkg_ref, qn, kpad, vpad, dqn, dk_acc, dv_acc, res) = refs[1 + len(deps):]
        h, part = pl.program_id(0), pl.program_id(1)

        @pl.when(part == 0)
        def _():
            kpad[0:PAD, :] = jnp.zeros((PAD, HEAD), BF16)
            vpad[0:PAD, :] = jnp.zeros((PAD, HEAD), BF16)
            _head_norm(q_ref, qg_ref[...], qn, 0, t)
            _head_norm(k_ref, kg_ref[...], kpad, PAD, t)

            def prep(i, _):
                sl = pl.ds(pl.multiple_of(PAD + i * ROWS, ROWS), ROWS)
                vpad[sl, :] = v_ref[pl.ds(pl.multiple_of(i * ROWS, ROWS), ROWS), :].astype(BF16)
                return 0

            lax.fori_loop(0, t // ROWS, prep, 0)

            def clear(i, _):
                sl = pl.ds(pl.multiple_of(i * ROWS, ROWS), ROWS)
                dk_acc[sl, :] = jnp.zeros((ROWS, HEAD), F32)
                dv_acc[sl, :] = jnp.zeros((ROWS, HEAD), F32)
                return 0

            lax.fori_loop(0, (t + PAD) // ROWS, clear, 0)
            dbias_ref[0] = jnp.zeros((CHUNK, BAND), F32)

            def chunk(n, _):
                qc, band, p = _attn_probs(qn, kpad, bias_ref, n)
                do = dy_ref[pl.ds(pl.multiple_of(n * CHUNK, CHUNK), CHUNK), :].astype(BF16)
                dp = lax.dot_general(do, vpad[band, :], NT, preferred_element_type=F32)
                ds = p * (dp - jnp.sum(dp * p, axis=-1, keepdims=True))
                dbias_ref[0] += ds
                dss = (ds * (HEAD ** -0.5)).astype(BF16)
                dqn[pl.ds(pl.multiple_of(n * CHUNK, CHUNK), CHUNK), :] = lax.dot_general(
                    dss, kpad[band, :], NN, preferred_element_type=F32)
                dk_acc[band, :] += lax.dot_general(dss, qc, TN, preferred_element_type=F32)
                dv_acc[band, :] += lax.dot_general(p.astype(BF16), do, TN, preferred_element_type=F32)
                return 0

            lax.fori_loop(0, nc, chunk, 0)
            dqg = _head_norm_bwd(q_ref, qg_ref[...], dqn, 0, res, 0, t)
            dkg = _head_norm_bwd(k_ref, kg_ref[...], dk_acc, PAD, res, 1, t)

            def put_v(i, _):
                sl = pl.ds(pl.multiple_of(i * ROWS, ROWS), ROWS)
                res[2, sl, :] = dv_acc[pl.ds(pl.multiple_of(PAD + i * ROWS, ROWS), ROWS), :]
                return 0

            lax.fori_loop(0, t // ROWS, put_v, 0)

            @pl.when(h == 0)
            def _():
                dqg_ref[...] = jnp.zeros_like(dqg_ref)
                dkg_ref[...] = jnp.zeros_like(dkg_ref)

            dqg_ref[...] += dqg
            dkg_ref[...] += dkg

        dproj_ref[...] = res[part]

    col = lambda k: pl.BlockSpec((t, HEAD), lambda h, p: (0, col0 + k * n_heads + h))
    vec = pl.BlockSpec((1, HEAD), lambda h, p: (0, 0))
    btile = pl.BlockSpec((1, CHUNK, BAND), lambda h, p: (h, 0, 0))
    return pl.pallas_call(
        body, grid=(n_heads, 3),
        in_specs=[ANY] * (1 + len(deps)) + [col(0), col(1), col(2), vec, vec, btile,
                                            pl.BlockSpec((t, HEAD), lambda h, p: (0, h))],
        out_specs=[pl.BlockSpec((t, HEAD), lambda h, p: (0, col0 + p * n_heads + h)), btile, vec, vec],
        out_shape=[SDS(dproj.shape, F32), SDS((n_heads, CHUNK, BAND), F32), SDS((1, HEAD), F32), SDS((1, HEAD), F32)],
        scratch_shapes=[pltpu.VMEM((t, HEAD), BF16), pltpu.VMEM((t + PAD, HEAD), BF16), pltpu.VMEM((t + PAD, HEAD), BF16),
                        pltpu.VMEM((t, HEAD), F32), pltpu.VMEM((t + PAD, HEAD), F32), pltpu.VMEM((t + PAD, HEAD), F32),
                        pltpu.VMEM((3, t, HEAD), F32)],
        input_output_aliases={0: 0}, compiler_params=_params(("arbitrary", "arbitrary")),
        name="attn_bwd")(dproj, *deps, proj, proj, proj, q_gain, k_gain, bias, dy)


def _place():
    x, y, c = lax.axis_index("x"), lax.axis_index("y"), lax.axis_index("c")
    others = [(1 - x, y), (x, 1 - y), (1 - x, 1 - y)]
    return x, y, c, others


def _chunk_of(ref, kind, chip, half, shard_shape):
    r, n = shard_shape
    hr = r // 2
    if kind == "col":
        rows = pl.ds(0, r) if half is None else pl.ds(half * hr, hr)
        return ref.at[rows, pl.ds(chip * n, n)]
    rows = pl.ds(chip * r, r) if half is None else pl.ds(chip * r + half * hr, hr)
    return ref.at[rows, :]


EFFECT = pltpu.SideEffectType.DATAFLOW_SIDE_EFFECTING


def _start_copies(name, bufs, plan, n, deps):
    nb, nd = len(bufs), len(deps)

    def body(*refs):
        send, recv, token = refs[nb + nd], refs[nb + nd + 1], refs[-1]
        for cp in plan(refs[:nb], send, recv)[0]:
            cp.start()
        token[...] = jnp.zeros_like(token)

    out = pl.pallas_call(
        body, name=name,
        out_shape=(pltpu.SemaphoreType.DMA((n,)), pltpu.SemaphoreType.DMA((n,)),
                   *[pltpu.HBM(b.shape, b.dtype) for b in bufs], SDS((8, 128), F32)),
        in_specs=[HBM] * nb + [ANY] * nd,
        out_specs=(SEM, SEM, *[HBM] * nb, pl.BlockSpec(memory_space=pltpu.VMEM)),
        input_output_aliases={i: 2 + i for i in range(nb)},
        compiler_params=pltpu.CompilerParams(has_side_effects=EFFECT),
    )(*[pltpu.with_memory_space_constraint(b, pltpu.HBM) for b in bufs], *deps)
    return out[0], out[1], list(out[2:2 + nb]), out[-1]


def _wait_copies(name, bufs, send, recv, plan, after):
    nb = len(bufs)

    def body(*refs):
        sends, recvs = plan(refs[:nb], refs[nb], refs[nb + 1])
        for cp in sends:
            cp.wait_send()
        for cp in recvs:
            cp.wait_recv()

    out = pl.pallas_call(
        body, name=name, out_shape=tuple(pltpu.HBM(b.shape, b.dtype) for b in bufs),
        in_specs=[HBM] * nb + [SEM, SEM] + [ANY] * len(after), out_specs=tuple([HBM] * nb),
        input_output_aliases={i: i for i in range(nb)},
        compiler_params=pltpu.CompilerParams(has_side_effects=EFFECT),
    )(*bufs, send, recv, *after)
    return list(out)


def _remote(src, dst, send, recv, i, dev):
    return pltpu.make_async_remote_copy(src_ref=src, dst_ref=dst, send_sem=send.at[i], recv_sem=recv.at[i],
                                        device_id=dev, device_id_type=MESH)


def _plan_gather_ici(kinds, shapes):
    def plan(refs, send, recv):
        x, y, c, others = _place()
        sends, recvs = [], []
        for w, (kind, ss) in enumerate(zip(kinds, shapes)):
            for p, (px, py) in enumerate(others):
                mine = _chunk_of(refs[w], kind, 2 * x + y, c, ss)
                theirs = _chunk_of(refs[w], kind, 2 * px + py, c, ss)
                sends.append(_remote(mine, mine, send, recv, 3 * w + p, (px, py, c)))
                recvs.append(_remote(theirs, theirs, send, recv, 3 * w + p, (px, py, c)))
        return sends, recvs

    return plan, 3 * len(kinds)


def _plan_gather_pass(kinds, shapes):
    def plan(refs, send, recv):
        x, y, c, others = _place()
        sends, recvs = [], []
        for w, (kind, ss) in enumerate(zip(kinds, shapes)):
            for p, (px, py) in enumerate(others):
                got = _chunk_of(refs[w], kind, 2 * px + py, c, ss)
                coming = _chunk_of(refs[w], kind, 2 * px + py, 1 - c, ss)
                sends.append(_remote(got, got, send, recv, 3 * w + p, (x, y, 1 - c)))
                recvs.append(_remote(coming, coming, send, recv, 3 * w + p, (x, y, 1 - c)))
        return sends, recvs

    return plan, 3 * len(kinds)


def _plan_pair(kinds, shapes):
    nw = len(kinds)

    def plan(refs, send, recv):
        x, y, c, _ = _place()
        sends = []
        for w, (kind, ss) in enumerate(zip(kinds, shapes)):
            for k in range(4):
                sends.append(_remote(_chunk_of(refs[w], kind, k, 1 - c, ss), refs[nw + w].at[k], send, recv,
                                     4 * w + k, (x, y, 1 - c)))
        return sends, sends

    return plan, 4 * nw


def _plan_chip(nw):
    def plan(refs, send, recv):
        x, y, c, others = _place()
        sends = []
        for w in range(nw):
            for p, (px, py) in enumerate(others):
                sends.append(_remote(refs[w].at[2 * px + py], refs[nw + w].at[p], send, recv, 3 * w + p, (px, py, c)))
        return sends, sends

    return plan, 3 * nw


def _plan_share(nw):
    def plan(refs, send, recv):
        x, y, c, _ = _place()
        sends = [_remote(refs[w].at[c], refs[w].at[c], send, recv, w, (x, y, 1 - c)) for w in range(nw)]
        recvs = [_remote(refs[w].at[1 - c], refs[w].at[1 - c], send, recv, w, (x, y, 1 - c)) for w in range(nw)]
        return sends, recvs

    return plan, nw


def _pair_add(grad, got, kind, shard_shape, pos, name):
    r, n = shard_shape
    hr = r // 2
    tr, tn = _tile(hr, 256, 16), _tile(n, 1408)
    nr, nn = hr // tr, n // tn
    if kind == "col":
        g_spec = pl.BlockSpec((tr, tn), lambda k, i, j, c: (c[1] * nr + i, k * nn + j))
    else:
        g_spec = pl.BlockSpec((tr, tn), lambda k, i, j, c: ((2 * k + c[1]) * nr + i, j))
    o_spec = pl.BlockSpec((1, tr, tn), lambda k, i, j, c: (k, i, j))

    def body(c_ref, g_ref, r_ref, o32_ref, o16_ref):
        s = g_ref[...] + r_ref[0]
        o32_ref[0] = s
        o16_ref[0] = s.astype(BF16)

    return pl.pallas_call(
        body,
        grid_spec=pltpu.PrefetchScalarGridSpec(num_scalar_prefetch=1, grid=(4, nr, nn), in_specs=[g_spec, o_spec],
                                               out_specs=[o_spec, o_spec]),
        out_shape=[SDS((4, hr, n), F32), SDS((4, hr, n), BF16)],
        compiler_params=_params(("parallel", "parallel", "parallel")), name=name)(pos, grad, got)


def _chip_add(part32, got16, pos, name):
    _, hr, n = part32.shape
    tr, tn = _tile(hr, 256, 16), _tile(n, 1408)
    own = pl.BlockSpec((1, tr, tn), lambda i, j, p: (p[0], i, j))
    oth = pl.BlockSpec((3, tr, tn), lambda i, j, p: (0, i, j))

    def body(p_ref, own_ref, oth_ref, o_ref):
        o_ref[0] = ((own_ref[0] + oth_ref[0].astype(F32)) + oth_ref[1].astype(F32)) + oth_ref[2].astype(F32)

    return pl.pallas_call(
        body,
        grid_spec=pltpu.PrefetchScalarGridSpec(num_scalar_prefetch=1, grid=(hr // tr, n // tn), in_specs=[own, oth],
                                               out_specs=pl.BlockSpec((1, tr, tn), lambda i, j, p: (p[1], i, j))),
        out_shape=SDS((2, hr, n), F32), compiler_params=_params(("parallel", "parallel")), name=name)(pos, part32, got16)


def _adamw_math(w, g, m, v):
    m = ADAM_B1 * m + (1.0 - ADAM_B1) * g
    v = ADAM_B2 * v + (1.0 - ADAM_B2) * (g * g)
    m_hat = m / (1.0 - ADAM_B1 ** ADAM_STEP)
    v_hat = v / (1.0 - ADAM_B2 ** ADAM_STEP)
    return -ADAM_LR * (m_hat / (jnp.sqrt(v_hat) + ADAM_EPS) + ADAM_WD * w), m, v


def _adamw(w, g, m, v, name):
    r, n = w.shape
    tr, tn = _tile(r, 256, 16), _tile(n, 1408)

    def body(w_ref, g_ref, m_ref, v_ref, d_ref, nm_ref, nv_ref):
        d_ref[...], nm_ref[...], nv_ref[...] = _adamw_math(w_ref[...], g_ref[...], m_ref[...], v_ref[...])

    tile = pl.BlockSpec((tr, tn), lambda i, j: (i, j))
    return pl.pallas_call(
        body, grid=(r // tr, n // tn), in_specs=[tile] * 4, out_specs=[tile] * 3, out_shape=[SDS((r, n), F32)] * 3,
        compiler_params=_params(("parallel", "parallel")), name=name)(w, g, m, v)


def _small_allreduce_adamw(g, w, m, v):
    length = g.shape[1]

    def body(g_ref, w_ref, m_ref, v_ref, gs_ref, d_ref, nm_ref, nv_ref, buf, send, recv):
        x, y, c = lax.axis_index("x"), lax.axis_index("y"), lax.axis_index("c")
        me = 4 * x + 2 * y + c
        buf[me] = g_ref[...]
        cps = []
        for d in range(1, 8):
            peer = (x ^ (d >> 2), y ^ ((d >> 1) & 1), c ^ (d & 1))
            cp = pltpu.make_async_remote_copy(src_ref=buf.at[me], dst_ref=buf.at[me], send_sem=send.at[d - 1],
                                              recv_sem=recv.at[d - 1], device_id=peer, device_id_type=MESH)
            cp.start()
            cps.append(cp)
        for cp in cps:
            cp.wait()
        total = buf[0]
        for d in range(1, 8):
            total = total + buf[d]
        gs_ref[...] = total
        d_ref[...], nm_ref[...], nv_ref[...] = _adamw_math(w_ref[...], total, m_ref[...], v_ref[...])

    vm = pl.BlockSpec(memory_space=pltpu.VMEM)
    return pl.pallas_call(
        body, in_specs=[vm] * 4, out_specs=[vm] * 4, out_shape=[SDS((1, length), F32)] * 4,
        scratch_shapes=[pltpu.VMEM((8, 1, length), F32), pltpu.SemaphoreType.DMA((7,)), pltpu.SemaphoreType.DMA((7,))],
        compiler_params=pltpu.CompilerParams(has_side_effects=True), name="small_allreduce_adamw")(g, w, m, v)


def kernel(x, w_in, b_gate, norm_mix, norm_ffn, hgrn_lb_logits, hgrn_out_gain, q_gain, k_gain, rel_bias, w_proj_a, w_proj_b, w_out, w_ffn_in, w_ffn_out, loss_target, m_w_in, m_b_gate, m_norm_mix, m_norm_ffn, m_hgrn_lb_logits, m_hgrn_out_gain, m_q_gain, m_k_gain, m_rel_bias, m_w_proj_a, m_w_proj_b, m_w_out, m_w_ffn_in, m_w_ffn_out, v_w_in, v_b_gate, v_norm_mix, v_norm_ffn, v_hgrn_lb_logits, v_hgrn_out_gain, v_q_gain, v_k_gain, v_rel_bias, v_w_proj_a, v_w_proj_b, v_w_out, v_w_ffn_in, v_w_ffn_out):
    t, d = x.shape[1], x.shape[2]
    d_a = hgrn_out_gain.shape[1]
    h_a = d_a // HEAD
    h_b = rel_bias.shape[1]
    d_b = h_b * HEAD
    x0 = x.reshape(t, d)
    target = loss_target.reshape(t, d)
    pos = jnp.stack([2 * lax.axis_index("x") + lax.axis_index("y"), lax.axis_index("c")]).astype(jnp.int32)

    names = ["w_in", "w_proj_a", "w_proj_b", "w_out", "w_ffn_in", "w_ffn_out"]
    big = dict(zip(names, [w_in[0], w_proj_a[0], w_proj_b[0], w_out[0], w_ffn_in[0], w_ffn_out[0]]))
    big_m = dict(zip(names, [m_w_in[0], m_w_proj_a[0], m_w_proj_b[0], m_w_out[0], m_w_ffn_in[0], m_w_ffn_out[0]]))
    big_v = dict(zip(names, [v_w_in[0], v_w_proj_a[0], v_w_proj_b[0], v_w_out[0], v_w_ffn_in[0], v_w_ffn_out[0]]))
    kind = dict(zip(names, ["col", "col", "col", "row", "col", "row"]))
    shape = {nm: big[nm].shape for nm in names}

    def gather_start(tag, group, deps):
        plan, n = _plan_gather_ici([kind[g] for g in group], [shape[g] for g in group])
        fulls = [_cast_into_full(big[g], kind[g], pos, "cast_" + g) for g in group]
        send, recv, bufs, token = _start_copies("gather_ici_start_" + tag, fulls, plan, n, deps)
        return (tag, group, plan, send, recv, bufs), token

    def gather_pass(state, after):
        tag, group, plan, send, recv, bufs = state
        bufs = _wait_copies("gather_ici_wait_" + tag, bufs, send, recv, plan, after)
        plan, n = _plan_gather_pass([kind[g] for g in group], [shape[g] for g in group])
        send, recv, bufs, token = _start_copies("gather_pass_start_" + tag, bufs, plan, n, ())
        return (tag, group, plan, send, recv, bufs), token

    def gather_done(state, after):
        tag, group, plan, send, recv, bufs = state
        return _wait_copies("gather_pass_wait_" + tag, bufs, send, recv, plan, after)

    def reduce_start(tag, group, grads, deps):
        plan, n = _plan_pair([kind[g] for g in group], [shape[g] for g in group])
        lands = [lax.empty((4, shape[g][0] // 2, shape[g][1]), F32) for g in group]
        send, recv, bufs, token = _start_copies("pair_start_" + tag, list(grads) + lands, plan, n, deps)
        return dict(tag=tag, group=group, plan=plan, send=send, recv=recv, bufs=bufs), token

    def reduce_pair_done(st, after):
        tag, group, nw = st["tag"], st["group"], len(st["group"])
        bufs = _wait_copies("pair_wait_" + tag, st["bufs"], st["send"], st["recv"], st["plan"], after)
        parts = [_pair_add(g, l, kind[nm], shape[nm], pos, "pair_add_" + nm)
                 for g, l, nm in zip(bufs[:nw], bufs[nw:], group)]
        lands = [lax.empty((3, shape[g][0] // 2, shape[g][1]), BF16) for g in group]
        plan, n = _plan_chip(nw)
        send, recv, bufs, token = _start_copies("chip_start_" + tag, [p16 for _, p16 in parts] + lands, plan, n, ())
        return dict(st, plan=plan, send=send, recv=recv, bufs=bufs, p32=[p32 for p32, _ in parts]), token

    def reduce_chip_done(st, after):
        tag, group, nw = st["tag"], st["group"], len(st["group"])
        bufs = _wait_copies("chip_wait_" + tag, st["bufs"], st["send"], st["recv"], st["plan"], after)
        finals = [_chip_add(p32, got, pos, "chip_add_" + nm) for p32, got, nm in zip(st["p32"], bufs[nw:], group)]
        plan, n = _plan_share(nw)
        send, recv, bufs, token = _start_copies("share_start_" + tag, finals, plan, n, ())
        return dict(st, plan=plan, send=send, recv=recv, bufs=bufs), token

    g_big, upd = {}, {}

    def reduce_finish(st, after):
        bufs = _wait_copies("share_wait_" + st["tag"], st["bufs"], st["send"], st["recv"], st["plan"], after)
        for full, nm in zip(bufs, st["group"]):
            g_big[nm] = full.reshape(shape[nm])
            upd[nm] = _adamw(big[nm], g_big[nm], big_m[nm], big_v[nm], "adamw_" + nm)

    ga, token = gather_start("a", ["w_in"], ())
    gb, token = gather_start("b", ["w_proj_a", "w_proj_b", "w_out"], (token,))
    gc, token = gather_start("c", ["w_ffn_in", "w_ffn_out"], (token,))
    h1, r1 = _rmsnorm_fwd(x0, norm_mix, "rmsnorm_mix")
    ga, token = gather_pass(ga, (h1, token))
    (wg_in,) = gather_done(ga, ())
    proj = _matmul(h1, wg_in, name="proj_in")
    y_a, o_pre, states = _hgrn_fwd(proj, hgrn_lb_logits, hgrn_out_gain, h_a)
    gb, token = gather_pass(gb, (y_a,))
    rb = jnp.pad(rel_bias[0], ((0, 0), (0, REL_LANES - N_REL)))
    bias = _relbias_expand(rb).transpose(1, 0, 2)
    col_b = 4 * d_a // HEAD
    y_b = _attn_fwd(proj, q_gain, k_gain, bias, h_b, col_b)
    wg_pa, wg_pb, wg_out = gather_done(gb, (y_b,))
    gc, token = gather_pass(gc, (y_b,))
    pa = _matmul(y_a, wg_pa, name="proj_a", deps=(token,))
    pb = _matmul(y_b, wg_pb, name="proj_b")
    gate_off = 4 * d_a + 3 * d_b
    merged = _merge_fwd(proj, b_gate, pa, pb, gate_off)
    x2 = _matmul(merged, wg_out, res=x0, name="out_proj")
    h2, r2 = _rmsnorm_fwd(x2, norm_ffn, "rmsnorm_ffn")
    wg_fin, wg_fout = gather_done(gc, (h2,))
    gu = _matmul(h2, wg_fin, name="ffn_in")
    act = _swiglu_fwd(gu)
    y = _matmul(act, wg_fout, res=x2, name="ffn_out")
    dy, loss_part = _loss_head(y, target)

    g_fout = _matmul(act, dy, ta=True, name="dw_ffn_out")
    r_fout, token = reduce_start("fout", ["w_ffn_out"], [g_fout], ())
    dact = _matmul(dy, wg_fout, tb=True, name="d_act", deps=(token,))
    r_fout, token = reduce_pair_done(r_fout, (dact,))
    dgu = _swiglu_bwd(dact, gu)
    g_fin = _matmul(h2, dgu, ta=True, name="dw_ffn_in", deps=(token,))
    r_fin, token = reduce_start("fin", ["w_ffn_in"], [g_fin], ())
    dh2 = _matmul(dgu, wg_fin, tb=True, name="d_h2", deps=(token,))
    r_fout, token_a = reduce_chip_done(r_fout, (dh2,))
    r_fin, token_b = reduce_pair_done(r_fin, (dh2,))
    dx2, g_norm_ffn = _rmsnorm_bwd(dh2, x2, r2, norm_ffn, dy, "rmsnorm_ffn_bwd")
    dmerged = _matmul(dx2, wg_out, tb=True, name="d_merged", deps=(token_a, token_b))
    dp_ab, dproj, g_bgate = _merge_bwd(dmerged, proj, b_gate, pa, pb, gate_off)
    g_out = _matmul(merged, dx2, ta=True, name="dw_out")
    g_pa = _matmul(y_a, dp_ab[0], ta=True, name="dw_proj_a")
    g_pb = _matmul(y_b, dp_ab[1], ta=True, name="dw_proj_b")
    r_mid, token = reduce_start("mid", ["w_proj_a", "w_proj_b", "w_out"], [g_pa, g_pb, g_out], ())
    dy_a = _matmul(dp_ab[0], wg_pa, tb=True, name="d_y_a", deps=(token,))
    dy_b = _matmul(dp_ab[1], wg_pb, tb=True, name="d_y_b")
    reduce_finish(r_fout, (dy_b,))
    r_fin, token_a = reduce_chip_done(r_fin, (dy_b,))
    r_mid, token_b = reduce_pair_done(r_mid, (dy_b,))
    dproj, dbias, g_qg, g_kg = _attn_bwd(dproj, proj, q_gain, k_gain, bias, dy_b, h_b, col_b, deps=(token_a, token_b))
    r_mid, token = reduce_chip_done(r_mid, (dbias,))
    reduce_finish(r_fin, (dbias,))
    dproj, g_lb, g_gain = _hgrn_bwd(dproj, proj, o_pre, states, dy_a, hgrn_lb_logits, hgrn_out_gain, h_a, deps=(token,))
    reduce_finish(r_mid, (g_lb,))
    g_rb = _relbias_reduce(dbias.transpose(1, 0, 2))[:, :N_REL]
    g_in = _matmul(h1, dproj, ta=True, name="dw_in")
    r_in, token = reduce_start("in", ["w_in"], [g_in], ())
    dh1 = _matmul(dproj, wg_in, tb=True, name="d_h1", deps=(token,))
    r_in, token = reduce_pair_done(r_in, (dh1,))
    dx, g_norm_mix = _rmsnorm_bwd(dh1, x0, r1, norm_mix, dx2, "rmsnorm_mix_bwd")

    small_w = [b_gate, norm_mix, norm_ffn, hgrn_lb_logits, hgrn_out_gain, q_gain, k_gain, rel_bias]
    small_m = [m_b_gate, m_norm_mix, m_norm_ffn, m_hgrn_lb_logits, m_hgrn_out_gain, m_q_gain, m_k_gain, m_rel_bias]
    small_v = [v_b_gate, v_norm_mix, v_norm_ffn, v_hgrn_lb_logits, v_hgrn_out_gain, v_q_gain, v_k_gain, v_rel_bias]
    small_g = [g_bgate, g_norm_mix, g_norm_ffn, g_lb, g_gain, g_qg, g_kg, g_rb]
    sizes = [w.size for w in small_w]
    length = -(-(sum(sizes) + 1) // 128) * 128

    def pack(parts_):
        flat = jnp.concatenate([p.reshape(1, -1) for p in parts_], axis=1)
        return jnp.pad(flat, ((0, 0), (0, length - flat.shape[1])))

    one = jnp.ones((1, 1), F32)
    packed = _small_allreduce_adamw(pack(small_g + [loss_part]), pack(small_w + [one]), pack(small_m + [one]),
                                    pack(small_v + [one]))

    def unpack(vec):
        out, at = [], 0
        for w, n in zip(small_w, sizes):
            out.append(vec[0, at:at + n].reshape(w.shape))
            at += n
        return out, vec[0, at]

    (sg, loss), (sd, _), (sm, _), (sv, _) = [unpack(p) for p in packed]
    r_in, token = reduce_chip_done(r_in, (packed[0],))
    reduce_finish(r_in, ())

    def ordered(small, bigs):
        bigs = [bigs[nm][None] for nm in names]
        return [bigs[0]] + small + bigs[1:]

    return (loss, dx.reshape(x.shape), *ordered(sg, g_big), *ordered(sd, {nm: upd[nm][0] for nm in names}),
            *ordered(sm, {nm: upd[nm][1] for nm in names}), *ordered(sv, {nm: upd[nm][2] for nm in names}))
```

```python
import functools

import jax
import jax.numpy as jnp
from jax import lax
from jax.experimental import pallas as pl
from jax.experimental.pallas import tpu as pltpu

F32 = jnp.float32
BF16 = jnp.bfloat16
SDS = jax.ShapeDtypeStruct
MESH = pl.DeviceIdType.MESH
HIGHEST = lax.Precision.HIGHEST

CHUNK = 64
SUB = 16
HEAD = 128
N_PAST = 8
BAND = (N_PAST + 1) * CHUNK
PAD = N_PAST * CHUNK
REL_FUTURE = CHUNK - 1
REL_PAST = 2 * CHUNK - 1
N_REL = REL_FUTURE + REL_PAST + 1
REL_LANES = 256
EPS = 1e-6
EXP_CLAMP = 80.0

ADAM_LR = 0.001
ADAM_B1 = 0.9
ADAM_B2 = 0.999
ADAM_EPS = 1e-08
ADAM_WD = 0.01
ADAM_STEP = 10

VMEM_LIMIT = 56 * 1024 * 1024

HBM = pl.BlockSpec(memory_space=pltpu.HBM)
ANY = pl.BlockSpec(memory_space=pl.ANY)
SEM = pl.BlockSpec(memory_space=pltpu.SEMAPHORE)

NT = (((1,), (1,)), ((), ()))
TN = (((0,), (0,)), ((), ()))
NN = (((1,), (0,)), ((), ()))


def _params(sem=None, **kw):
    return pltpu.CompilerParams(dimension_semantics=sem, vmem_limit_bytes=VMEM_LIMIT, **kw)


def _tile(n, pref, unit=128):
    if n <= pref:
        return n
    t = pref - pref % unit
    while n % t:
        t -= unit
    return t


def _sigmoid(x):
    return 1.0 / (1.0 + jnp.exp(-x))


def _dsilu(x, s):
    return s * (1.0 + x * (1.0 - s))


def _bdot(a, b, dims=NN):
    return lax.dot_general(a.astype(BF16), b.astype(BF16), dims, preferred_element_type=F32)


def _split(a):
    hi = a.astype(BF16)
    return hi, (a - hi.astype(F32)).astype(BF16)


def _dot3(a, b, dims):
    dot = lambda u, v: lax.dot_general(u, v, dims, preferred_element_type=F32)
    return dot(a[0], b[1]) + dot(a[1], b[0]) + dot(a[0], b[0])


def _fdot(a, b):
    return lax.dot_general(a, b, NN, precision=HIGHEST, preferred_element_type=F32)


MM_TILE_K = 2816
MM_TILE_N = 512


def _matmul(a, b, *, ta=False, tb=False, res=None, out_dtype=F32, name, deps=()):
    m, k = (a.shape[1], a.shape[0]) if ta else a.shape
    n = b.shape[0] if tb else b.shape[1]
    tk = _tile(k, MM_TILE_K)
    nk = k // tk
    tm, tn = _tile(m, 2048 if nk == 1 else 1024), _tile(n, MM_TILE_N)
    dims = ((((0,) if ta else (1,)), ((1,) if tb else (0,))), ((), ()))

    def body(*refs):
        n_in = 2 + (res is not None)
        a_ref, b_ref = refs[:2]
        r_ref = refs[2] if res is not None else None
        o_ref = refs[n_in + len(deps)]
        part = lax.dot_general(a_ref[...].astype(BF16), b_ref[...].astype(BF16), dims, preferred_element_type=F32)

        def finish(out):
            if r_ref is not None:
                out = out + r_ref[...]
            o_ref[...] = out.astype(o_ref.dtype)

        if nk == 1:
            finish(part)
            return
        acc_ref = refs[-1]
        kk = pl.program_id(2)

        @pl.when(kk == 0)
        def _():
            acc_ref[...] = part

        @pl.when(jnp.logical_and(kk > 0, kk < nk - 1))
        def _():
            acc_ref[...] += part

        @pl.when(kk == nk - 1)
        def _():
            finish(acc_ref[...] + part)

    a_spec = pl.BlockSpec((tk, tm), lambda i, j, l: (l, i)) if ta else pl.BlockSpec((tm, tk), lambda i, j, l: (i, l))
    b_spec = pl.BlockSpec((tn, tk), lambda i, j, l: (j, l)) if tb else pl.BlockSpec((tk, tn), lambda i, j, l: (l, j))
    o_spec = pl.BlockSpec((tm, tn), lambda i, j, l: (i, j))
    in_specs = [a_spec, b_spec] + ([o_spec] if res is not None else []) + [ANY] * len(deps)
    args = (a, b) + ((res,) if res is not None else ()) + tuple(deps)
    return pl.pallas_call(
        body, grid=(m // tm, n // tn, nk), in_specs=in_specs, out_specs=o_spec,
        out_shape=SDS((m, n), out_dtype), scratch_shapes=[pltpu.VMEM((tm, tn), F32)] if nk > 1 else [],
        compiler_params=_params(("parallel", "parallel", "arbitrary")), name=name)(*args)


def _cast_into_full(w, kind, pos, name):
    r, n = w.shape
    tr = _tile(r, 512, 16)
    nr = r // tr
    if kind == "col":
        shape, o_spec = (r, 4 * n), pl.BlockSpec((tr, n), lambda i, p: (i, p[0]))
    else:
        shape, o_spec = (4 * r, n), pl.BlockSpec((tr, n), lambda i, p: (p[0] * nr + i, 0))

    def body(p_ref, w_ref, o_ref):
        o_ref[...] = w_ref[...].astype(BF16)

    return pl.pallas_call(
        body,
        grid_spec=pltpu.PrefetchScalarGridSpec(num_scalar_prefetch=1, grid=(nr,),
                                               in_specs=[pl.BlockSpec((tr, n), lambda i, p: (i, 0))], out_specs=o_spec),
        out_shape=SDS(shape, BF16), compiler_params=_params(("parallel",)), name=name)(pos, w)


def _rmsnorm_fwd(x, gain, name):
    t, d = x.shape
    tm = _tile(t, 256)

    def body(x_ref, g_ref, h_ref, r_ref):
        xv = x_ref[...]
        r = lax.rsqrt(jnp.mean(xv * xv, axis=-1, keepdims=True) + EPS)
        h_ref[...] = (xv * r * g_ref[...]).astype(BF16)
        r_ref[...] = r

    return pl.pallas_call(
        body, grid=(t // tm,),
        in_specs=[pl.BlockSpec((tm, d), lambda i: (i, 0)), pl.BlockSpec((1, d), lambda i: (0, 0))],
        out_specs=[pl.BlockSpec((tm, d), lambda i: (i, 0)), pl.BlockSpec((tm, 1), lambda i: (i, 0))],
        out_shape=[SDS((t, d), BF16), SDS((t, 1), F32)], compiler_params=_params(("parallel",)), name=name)(x, gain)


def _rmsnorm_bwd(dh, x, r, gain, dres, name):
    t, d = x.shape
    tm = _tile(t, 256)

    def body(dh_ref, x_ref, r_ref, g_ref, dres_ref, dx_ref, dxb_ref, dg_ref):
        @pl.when(pl.program_id(0) == 0)
        def _():
            dg_ref[...] = jnp.zeros_like(dg_ref)

        dhv, xv, rv = dh_ref[...], x_ref[...], r_ref[...]
        dg_ref[...] += jnp.sum(dhv * (xv * rv), axis=0, keepdims=True)
        u = dhv * g_ref[...]
        dx = dres_ref[...] + rv * u - xv * (rv * rv * rv) * jnp.mean(u * xv, axis=-1, keepdims=True)
        dx_ref[...] = dx
        dxb_ref[...] = dx.astype(BF16)

    row = pl.BlockSpec((tm, d), lambda i: (i, 0))
    vec = pl.BlockSpec((1, d), lambda i: (0, 0))
    return pl.pallas_call(
        body, grid=(t // tm,), in_specs=[row, row, pl.BlockSpec((tm, 1), lambda i: (i, 0)), vec, row],
        out_specs=[row, row, vec], out_shape=[SDS((t, d), F32), SDS((t, d), BF16), SDS((1, d), F32)],
        compiler_params=_params(("arbitrary",)), name=name)(dh, x, r, gain, dres)


def _merge_fwd(proj, b_gate, pa, pb, off):
    t, d = pa.shape
    tm, tc = _tile(t, 512), _tile(d, 512)
    nj = d // tc
    oa, ob = off // tc, off // tc + nj

    def body(la_ref, lb_ref, ba_ref, bb_ref, pa_ref, pb_ref, o_ref):
        ga = _sigmoid(la_ref[...] + ba_ref[...])
        gb = _sigmoid(lb_ref[...] + bb_ref[...])
        o_ref[...] = (ga * pa_ref[...] + gb * pb_ref[...]).astype(BF16)

    tile = pl.BlockSpec((tm, tc), lambda i, j: (i, j))
    return pl.pallas_call(
        body, grid=(t // tm, nj),
        in_specs=[pl.BlockSpec((tm, tc), lambda i, j: (i, oa + j)), pl.BlockSpec((tm, tc), lambda i, j: (i, ob + j)),
                  pl.BlockSpec((1, tc), lambda i, j: (0, j)), pl.BlockSpec((1, tc), lambda i, j: (0, nj + j)), tile, tile],
        out_specs=tile, out_shape=SDS((t, d), BF16), compiler_params=_params(("parallel", "parallel")),
        name="merge_fwd")(proj, proj, b_gate, b_gate, pa, pb)


def _merge_bwd(dmerged, proj, b_gate, pa, pb, off):
    t, d = pa.shape
    tm, tc = _tile(t, 512), _tile(d, 512)
    nj, ni = d // tc, t // tm
    o0 = off // tc

    def body(dm_ref, l_ref, b_ref, pa_ref, pb_ref, dp_ref, dl_ref, db_ref):
        s, i = pl.program_id(0), pl.program_id(2)
        p = jnp.where(s == 0, pa_ref[...], pb_ref[...])
        g = _sigmoid(l_ref[...] + b_ref[...])
        dm = dm_ref[...]
        dp_ref[0] = (dm * g).astype(BF16)
        dl = dm * p * g * (1.0 - g)
        dl_ref[...] = dl.astype(BF16)

        @pl.when(i == 0)
        def _():
            db_ref[...] = jnp.zeros_like(db_ref)

        db_ref[...] += jnp.sum(dl, axis=0, keepdims=True)

    tile = pl.BlockSpec((tm, tc), lambda s, j, i: (i, j))
    return pl.pallas_call(
        body, grid=(2, nj, ni),
        in_specs=[tile, pl.BlockSpec((tm, tc), lambda s, j, i: (i, o0 + s * nj + j)),
                  pl.BlockSpec((1, tc), lambda s, j, i: (0, s * nj + j)), tile, tile],
        out_specs=[pl.BlockSpec((1, tm, tc), lambda s, j, i: (s, i, j)),
                   pl.BlockSpec((tm, tc), lambda s, j, i: (i, o0 + s * nj + j)),
                   pl.BlockSpec((1, tc), lambda s, j, i: (0, s * nj + j))],
        out_shape=[SDS((2, t, d), BF16), SDS(proj.shape, BF16), SDS((1, 2 * d), F32)],
        compiler_params=_params(("arbitrary", "arbitrary", "arbitrary")),
        name="merge_bwd")(dmerged, proj, b_gate, pa, pb)


def _swiglu_fwd(gu):
    t, f2 = gu.shape
    f = f2 // 2
    tm, tc = _tile(t, 512), _tile(f, 512)
    nj = f // tc

    def body(g_ref, u_ref, o_ref):
        g = g_ref[...]
        o_ref[...] = (g * _sigmoid(g) * u_ref[...]).astype(BF16)

    return pl.pallas_call(
        body, grid=(t // tm, nj),
        in_specs=[pl.BlockSpec((tm, tc), lambda i, j: (i, j)), pl.BlockSpec((tm, tc), lambda i, j: (i, nj + j))],
        out_specs=pl.BlockSpec((tm, tc), lambda i, j: (i, j)), out_shape=SDS((t, f), BF16),
        compiler_params=_params(("parallel", "parallel")), name="swiglu_fwd")(gu, gu)


def _swiglu_bwd(dact, gu):
    t, f2 = gu.shape
    f = f2 // 2
    tm = _tile(t, 128)

    def body(d_ref, gu_ref, o_ref):
        g, u, dv = gu_ref[:, :f], gu_ref[:, f:], d_ref[...]
        sg = _sigmoid(g)
        o_ref[:, :f] = (dv * u * _dsilu(g, sg)).astype(BF16)
        o_ref[:, f:] = (dv * (g * sg)).astype(BF16)

    return pl.pallas_call(
        body, grid=(t // tm,),
        in_specs=[pl.BlockSpec((tm, f), lambda i: (i, 0)), pl.BlockSpec((tm, f2), lambda i: (i, 0))],
        out_specs=pl.BlockSpec((tm, f2), lambda i: (i, 0)), out_shape=SDS((t, f2), BF16),
        compiler_params=_params(("parallel",)), name="swiglu_bwd")(dact, gu)


def _loss_head(y, target):
    t, d = y.shape
    tm = _tile(t, 256)

    def body(y_ref, t_ref, dy_ref, dyb_ref, l_ref):
        @pl.when(pl.program_id(0) == 0)
        def _():
            l_ref[...] = jnp.zeros_like(l_ref)

        e = y_ref[...] - t_ref[...]
        dy = e * (1.0 / d)
        dy_ref[...] = dy
        dyb_ref[...] = dy.astype(BF16)
        l_ref[...] += 0.5 * jnp.sum(jnp.mean(e * e, axis=-1, keepdims=True), axis=0, keepdims=True)

    row = pl.BlockSpec((tm, d), lambda i: (i, 0))
    return pl.pallas_call(
        body, grid=(t // tm,), in_specs=[row, row], out_specs=[row, row, pl.BlockSpec((1, 1), lambda i: (0, 0))],
        out_shape=[SDS((t, d), F32), SDS((t, d), BF16), SDS((1, 1), F32)], compiler_params=_params(("arbitrary",)),
        name="loss_head")(y, target)


def _rel_onehot(qi):
    p = lax.broadcasted_iota(jnp.int32, (REL_LANES, BAND), 1)
    r = lax.broadcasted_iota(jnp.int32, (REL_LANES, BAND), 0)
    idx = jnp.clip(qi + PAD - p, -REL_FUTURE, REL_PAST) + REL_FUTURE
    return (idx == r).astype(F32)


def _relbias_expand(rb):
    h = rb.shape[0]

    def body(rb_ref, o_ref):
        def step(qi, _):
            o_ref[qi] = _fdot(rb_ref[...], _rel_onehot(qi))
            return 0

        lax.fori_loop(0, CHUNK, step, 0)

    return pl.pallas_call(body, out_shape=SDS((CHUNK, h, BAND), F32), compiler_params=_params(),
                          name="relbias_expand")(rb)


def _relbias_reduce(dbias):
    h = dbias.shape[1]

    def body(db_ref, o_ref):
        def step(qi, acc):
            return acc + lax.dot_general(db_ref[qi], _rel_onehot(qi), NT, precision=HIGHEST,
                                         preferred_element_type=F32)

        o_ref[...] = lax.fori_loop(0, CHUNK, step, jnp.zeros((h, REL_LANES), F32))

    return pl.pallas_call(body, out_shape=SDS((h, REL_LANES), F32), compiler_params=_params(),
                          name="relbias_reduce")(dbias)


def _lower_bound(l_ref):
    l0, l1 = l_ref[0:1, :], l_ref[1:2, :]
    m = jnp.maximum(l0, l1)
    e0, e1 = jnp.exp(l0 - m), jnp.exp(l1 - m)
    return e0 / (e0 + e1)


def _tri(lower):
    r = lax.broadcasted_iota(jnp.int32, (CHUNK, CHUNK), 0)
    c = lax.broadcasted_iota(jnp.int32, (CHUNK, CHUNK), 1)
    return r >= c if lower else r <= c


def _hgrn_intra(qs, kk, b_s):
    rows = lax.broadcasted_iota(jnp.int32, (CHUNK, HEAD), 0)
    tr = lax.broadcasted_iota(jnp.int32, (SUB, CHUNK), 0)
    tc = lax.broadcasted_iota(jnp.int32, (SUB, CHUNK), 1)
    b = b_s[...]
    out = []
    for i in range(CHUNK // SUB):
        lo = i * SUB
        ref = jnp.zeros((1, HEAD), F32) if i == 0 else b_s[lo - 1:lo, :]
        eq = jnp.exp(b[lo:lo + SUB] - ref)
        qt = _split(qs[lo:lo + SUB] * eq)
        e = jnp.where(rows < lo + SUB, jnp.exp(jnp.minimum(ref - b, EXP_CLAMP)), 0.0)
        kt = _split(kk * e)
        a = jnp.where(tc <= tr + lo, _dot3(qt, kt, NT), 0.0)
        out.append((eq, qt, e, kt, a))
    return out


def _hgrn_fwd(proj, lb_logits, gain, n_heads):
    t = proj.shape[0]
    nc = t // CHUNK
    da = n_heads * HEAD

    def body(q_ref, f_ref, i_ref, g_ref, l_ref, gain_ref, y_ref, o_ref, st_ref, state, b_s):
        state[...] = jnp.zeros_like(state)
        lb = _lower_bound(l_ref)
        gain_v = gain_ref[...]
        tril = _tri(True).astype(F32)

        def chunk(j, _):
            sl = pl.ds(pl.multiple_of(j * CHUNK, CHUNK), CHUNK)
            fg = lb + (1.0 - lb) * _sigmoid(f_ref[sl, :])
            kk = 1.0 - fg
            qv = q_ref[sl, :]
            qs = qv * _sigmoid(qv)
            vb = i_ref[sl, :].astype(BF16)
            b = _fdot(tril, jnp.log(fg))
            b_s[...] = b
            s_in = state[...]
            st_ref[0, j] = s_in
            o = _bdot(qs * jnp.exp(b), s_in, NT)
            a = jnp.concatenate([blk[4] for blk in _hgrn_intra(qs, kk, b_s)], axis=0)
            o = o + _bdot(a, vb)
            o_ref[sl, :] = o
            bl = b_s[CHUNK - 1:CHUNK, :]
            state[...] = s_in * jnp.exp(bl) + _bdot(vb, kk * jnp.exp(bl - b), TN)
            rr = lax.rsqrt(jnp.mean(o * o, axis=-1, keepdims=True) + EPS)
            gv = g_ref[sl, :]
            y_ref[sl, :] = (o * rr * gain_v * (gv * _sigmoid(gv))).astype(BF16)
            return 0

        lax.fori_loop(0, nc, chunk, 0)

    col = lambda k: pl.BlockSpec((t, HEAD), lambda h: (0, k * n_heads + h))
    vec = pl.BlockSpec((1, HEAD), lambda h: (0, h))
    return pl.pallas_call(
        body, grid=(n_heads,),
        in_specs=[col(0), col(1), col(2), col(3), pl.BlockSpec((2, HEAD), lambda h: (0, h)), vec],
        out_specs=[pl.BlockSpec((t, HEAD), lambda h: (0, h)), pl.BlockSpec((t, HEAD), lambda h: (0, h)),
                   pl.BlockSpec((1, nc, HEAD, HEAD), lambda h: (h, 0, 0, 0))],
        out_shape=[SDS((t, da), BF16), SDS((t, da), F32), SDS((n_heads, nc, HEAD, HEAD), F32)],
        scratch_shapes=[pltpu.VMEM((HEAD, HEAD), F32), pltpu.VMEM((CHUNK, HEAD), F32)],
        compiler_params=_params(("parallel",)), name="hgrn_fwd")(proj, proj, proj, proj, lb_logits, gain)


def _hgrn_bwd(dproj, proj, o_pre, states, dy, lb_logits, gain, n_heads, deps=()):
    t = proj.shape[0]
    nc = t // CHUNK
    da = n_heads * HEAD

    def body(*refs):
        (q_ref, f_ref, i_ref, g_ref, o_ref, st_ref, dy_ref, l_ref, gain_ref,
         dproj_ref, dl_ref, dgain_ref, res, dstate, b_s) = refs[1 + len(deps):]

        @pl.when(pl.program_id(1) == 0)
        def _():
            dstate[...] = jnp.zeros_like(dstate)
            lb = _lower_bound(l_ref)
            gain_v = gain_ref[...]
            tril, triu = _tri(True), _tri(False).astype(F32)
            last = lax.broadcasted_iota(jnp.int32, (CHUNK, HEAD), 0) == CHUNK - 1

            def chunk(jj, carry):
                dlb_acc, dgain_acc = carry
                j = nc - 1 - jj
                sl = pl.ds(pl.multiple_of(j * CHUNK, CHUNK), CHUNK)
                sg = _sigmoid(f_ref[sl, :])
                fg = lb + (1.0 - lb) * sg
                kk = 1.0 - fg
                qv = q_ref[sl, :]
                sq = _sigmoid(qv)
                qs = qv * sq
                vb = i_ref[sl, :].astype(BF16)
                gv = g_ref[sl, :]
                sgg = _sigmoid(gv)
                silg = gv * sgg
                b = _fdot(tril.astype(F32), jnp.log(fg))
                b_s[...] = b
                o = o_ref[sl, :]
                dyv = dy_ref[sl, :]
                rr = lax.rsqrt(jnp.mean(o * o, axis=-1, keepdims=True) + EPS)
                on = o * rr
                dgain_acc = dgain_acc + jnp.sum(dyv * on * silg, axis=0, keepdims=True)
                dg = dyv * on * gain_v * _dsilu(gv, sgg)
                don = dyv * gain_v * silg
                do = (rr * don - o * (rr * rr * rr) * jnp.mean(don * o, axis=-1, keepdims=True)).astype(BF16)
                s_in = st_ref[0, j]
                ds_out = dstate[...]
                eb = jnp.exp(b)
                bl = b_s[CHUNK - 1:CHUNK, :]
                ebl = jnp.exp(bl)
                ekd = jnp.exp(bl - b)
                dq = _bdot(do, s_in) * eb
                da_m = jnp.where(tril, _bdot(do, vb, NT), 0.0)
                a_rows, dq_rows = [], []
                dk = jnp.zeros((CHUNK, HEAD), F32)
                for i, (eq, qt, e, kt, a) in enumerate(_hgrn_intra(qs, kk, b_s)):
                    da_i = _split(da_m[i * SUB:(i + 1) * SUB])
                    a_rows.append(a)
                    dq_rows.append(eq * _dot3(da_i, kt, NN))
                    dk = dk + e * _dot3(da_i, qt, TN)
                dq = dq + jnp.concatenate(dq_rows, axis=0)
                dv = _bdot(jnp.concatenate(a_rows, axis=0), do, TN) + _bdot(kk * ekd, ds_out, NT)
                dk_state = ekd * _bdot(vb, ds_out)
                dk = dk + dk_state
                db = qs * dq - kk * dk
                extra = (jnp.sum(kk * dk_state, axis=0, keepdims=True)
                         + ebl * jnp.sum(s_in * ds_out, axis=0, keepdims=True))
                db = db + jnp.where(last, extra, 0.0)
                dfg = _fdot(triu, db) / fg - dk
                dlb_acc = dlb_acc + jnp.sum(dfg * (1.0 - sg), axis=0, keepdims=True)
                dstate[...] = ds_out * ebl + _bdot(do, qs * eb, TN)
                res[0, sl, :] = dq * _dsilu(qv, sq)
                res[1, sl, :] = dfg * (1.0 - lb) * sg * (1.0 - sg)
                res[2, sl, :] = dv
                res[3, sl, :] = dg
                return dlb_acc, dgain_acc

            zero = jnp.zeros((1, HEAD), F32)
            dlb, dgain = lax.fori_loop(0, nc, chunk, (zero, zero))
            dgain_ref[...] = dgain
            dl0 = dlb * lb * (1.0 - lb)
            dl_ref[0:1, :] = dl0
            dl_ref[1:2, :] = -dl0

        dproj_ref[...] = res[pl.program_id(1)].astype(BF16)

    col = lambda k: pl.BlockSpec((t, HEAD), lambda h, p: (0, k * n_heads + h))
    head = pl.BlockSpec((t, HEAD), lambda h, p: (0, h))
    vec = pl.BlockSpec((1, HEAD), lambda h, p: (0, h))
    return pl.pallas_call(
        body, grid=(n_heads, 4),
        in_specs=[ANY] * (1 + len(deps)) + [col(0), col(1), col(2), col(3), head,
                  pl.BlockSpec((1, nc, HEAD, HEAD), lambda h, p: (h, 0, 0, 0)),
                  head, pl.BlockSpec((2, HEAD), lambda h, p: (0, h)), vec],
        out_specs=[pl.BlockSpec((t, HEAD), lambda h, p: (0, p * n_heads + h)),
                   pl.BlockSpec((2, HEAD), lambda h, p: (0, h)), vec],
        out_shape=[SDS(dproj.shape, BF16), SDS((2, da), F32), SDS((1, da), F32)],
        scratch_shapes=[pltpu.VMEM((4, t, HEAD), F32), pltpu.VMEM((HEAD, HEAD), F32), pltpu.VMEM((CHUNK, HEAD), F32)],
        input_output_aliases={0: 0}, compiler_params=_params(("arbitrary", "arbitrary")),
        name="hgrn_bwd")(dproj, *deps, proj, proj, proj, proj, o_pre, states, dy, lb_logits, gain)


ROWS = 256


def _head_norm(x_ref, gain, dst, dst_off, t):
    def step(i, _):
        sl = pl.ds(pl.multiple_of(i * ROWS, ROWS), ROWS)
        xv = x_ref[sl, :]
        r = lax.rsqrt(jnp.mean(xv * xv, axis=-1, keepdims=True) + EPS)
        dst[pl.ds(pl.multiple_of(dst_off + i * ROWS, ROWS), ROWS), :] = (xv * r * gain).astype(BF16)
        return 0

    lax.fori_loop(0, t // ROWS, step, 0)


def _head_norm_bwd(x_ref, gain, dn_ref, dn_off, out, slot, t):
    def step(i, acc):
        sl = pl.ds(pl.multiple_of(i * ROWS, ROWS), ROWS)
        xv = x_ref[sl, :]
        dn = dn_ref[pl.ds(pl.multiple_of(dn_off + i * ROWS, ROWS), ROWS), :]
        r = lax.rsqrt(jnp.mean(xv * xv, axis=-1, keepdims=True) + EPS)
        u = dn * gain
        out[slot, sl, :] = r * u - xv * (r * r * r) * jnp.mean(u * xv, axis=-1, keepdims=True)
        return acc + jnp.sum(dn * (xv * r), axis=0, keepdims=True)

    return lax.fori_loop(0, t // ROWS, step, jnp.zeros((1, HEAD), F32))


def _attn_probs(qn, kpad, bias_ref, n):
    qc = qn[pl.ds(pl.multiple_of(n * CHUNK, CHUNK), CHUNK), :]
    band = pl.ds(pl.multiple_of(n * CHUNK, CHUNK), BAND)
    s = lax.dot_general(qc, kpad[band, :], NT, preferred_element_type=F32) * (HEAD ** -0.5) + bias_ref[0]
    col = lax.broadcasted_iota(jnp.int32, (CHUNK, BAND), 1)
    s = jnp.where(col >= PAD - n * CHUNK, s, -jnp.inf)
    p = jnp.exp(s - jnp.max(s, axis=-1, keepdims=True))
    return qc, band, p / jnp.sum(p, axis=-1, keepdims=True)


def _attn_fwd(proj, q_gain, k_gain, bias, n_heads, col0):
    t = proj.shape[0]
    nc = t // CHUNK

    def body(q_ref, k_ref, v_ref, qg_ref, kg_ref, bias_ref, y_ref, qn, kpad, vpad):
        kpad[0:PAD, :] = jnp.zeros((PAD, HEAD), BF16)
        vpad[0:PAD, :] = jnp.zeros((PAD, HEAD), BF16)
        _head_norm(q_ref, qg_ref[...], qn, 0, t)
        _head_norm(k_ref, kg_ref[...], kpad, PAD, t)

        def copy_v(i, _):
            vpad[pl.ds(pl.multiple_of(PAD + i * ROWS, ROWS), ROWS), :] = v_ref[
                pl.ds(pl.multiple_of(i * ROWS, ROWS), ROWS), :].astype(BF16)
            return 0

        lax.fori_loop(0, t // ROWS, copy_v, 0)

        def chunk(n, _):
            _, band, p = _attn_probs(qn, kpad, bias_ref, n)
            y_ref[pl.ds(pl.multiple_of(n * CHUNK, CHUNK), CHUNK), :] = _bdot(p, vpad[band, :]).astype(BF16)
            return 0

        lax.fori_loop(0, nc, chunk, 0)

    col = lambda k: pl.BlockSpec((t, HEAD), lambda h: (0, col0 + k * n_heads + h))
    vec = pl.BlockSpec((1, HEAD), lambda h: (0, 0))
    return pl.pallas_call(
        body, grid=(n_heads,),
        in_specs=[col(0), col(1), col(2), vec, vec, pl.BlockSpec((1, CHUNK, BAND), lambda h: (h, 0, 0))],
        out_specs=pl.BlockSpec((t, HEAD), lambda h: (0, h)), out_shape=SDS((t, n_heads * HEAD), BF16),
        scratch_shapes=[pltpu.VMEM((t, HEAD), BF16), pltpu.VMEM((t + PAD, HEAD), BF16), pltpu.VMEM((t + PAD, HEAD), BF16)],
        compiler_params=_params(("parallel",)), name="attn_fwd")(proj, proj, proj, q_gain, k_gain, bias)


def _attn_bwd(dproj, proj, q_gain, k_gain, bias, dy, n_heads, col0, deps=()):
    t = proj.shape[0]
    nc = t // CHUNK

    def body(*refs):
        (q_ref, k_ref, v_ref, qg_ref, kg_ref, bias_ref, dy_ref,
         dproj_ref, dbias_ref, dqg_ref, dkg_ref, qn, kpad, vpad, dqn, dk_acc, dv_acc, res) = refs[1 + len(deps):]
        h, part = pl.program_id(0), pl.program_id(1)

        @pl.when(part == 0)
        def _():
            kpad[0:PAD, :] = jnp.zeros((PAD, HEAD), BF16)
            vpad[0:PAD, :] = jnp.zeros((PAD, HEAD), BF16)
            _head_norm(q_ref, qg_ref[...], qn, 0, t)
            _head_norm(k_ref, kg_ref[...], kpad, PAD, t)

            def prep(i, _):
                sl = pl.ds(pl.multiple_of(PAD + i * ROWS, ROWS), ROWS)
                vpad[sl, :] = v_ref[pl.ds(pl.multiple_of(i * ROWS, ROWS), ROWS), :].astype(BF16)
                return 0

            lax.fori_loop(0, t // ROWS, prep, 0)

            def clear(i, _):
                sl = pl.ds(pl.multiple_of(i * ROWS, ROWS), ROWS)
                dk_acc[sl, :] = jnp.zeros((ROWS, HEAD), F32)
                dv_acc[sl, :] = jnp.zeros((ROWS, HEAD), F32)
                return 0

            lax.fori_loop(0, (t + PAD) // ROWS, clear, 0)
            dbias_ref[0] = jnp.zeros((CHUNK, BAND), F32)

            def chunk(n, _):
                qc, band, p = _attn_probs(qn, kpad, bias_ref, n)
                do = dy_ref[pl.ds(pl.multiple_of(n * CHUNK, CHUNK), CHUNK), :].astype(BF16)
                dp = lax.dot_general(do, vpad[band, :], NT, preferred_element_type=F32)
                ds = p * (dp - jnp.sum(dp * p, axis=-1, keepdims=True))
                dbias_ref[0] += ds
                dss = (ds * (HEAD ** -0.5)).astype(BF16)
                dqn[pl.ds(pl.multiple_of(n * CHUNK, CHUNK), CHUNK), :] = lax.dot_general(
                    dss, kpad[band, :], NN, preferred_element_type=F32)
                dk_acc[band, :] += lax.dot_general(dss, qc, TN, preferred_element_type=F32)
                dv_acc[band, :] += lax.dot_general(p.astype(BF16), do, TN, preferred_element_type=F32)
                return 0

            lax.fori_loop(0, nc, chunk, 0)
            dqg = _head_norm_bwd(q_ref, qg_ref[...], dqn, 0, res, 0, t)
            dkg = _head_norm_bwd(k_ref, kg_ref[...], dk_acc, PAD, res, 1, t)

            def put_v(i, _):
                sl = pl.ds(pl.multiple_of(i * ROWS, ROWS), ROWS)
                res[2, sl, :] = dv_acc[pl.ds(pl.multiple_of(PAD + i * ROWS, ROWS), ROWS), :]
                return 0

            lax.fori_loop(0, t // ROWS, put_v, 0)

            @pl.when(h == 0)
            def _():
                dqg_ref[...] = jnp.zeros_like(dqg_ref)
                dkg_ref[...] = jnp.zeros_like(dkg_ref)

            dqg_ref[...] += dqg
            dkg_ref[...] += dkg

        dproj_ref[...] = res[part].astype(BF16)

    col = lambda k: pl.BlockSpec((t, HEAD), lambda h, p: (0, col0 + k * n_heads + h))
    vec = pl.BlockSpec((1, HEAD), lambda h, p: (0, 0))
    btile = pl.BlockSpec((1, CHUNK, BAND), lambda h, p: (h, 0, 0))
    return pl.pallas_call(
        body, grid=(n_heads, 3),
        in_specs=[ANY] * (1 + len(deps)) + [col(0), col(1), col(2), vec, vec, btile,
                                            pl.BlockSpec((t, HEAD), lambda h, p: (0, h))],
        out_specs=[pl.BlockSpec((t, HEAD), lambda h, p: (0, col0 + p * n_heads + h)), btile, vec, vec],
        out_shape=[SDS(dproj.shape, BF16), SDS((n_heads, CHUNK, BAND), F32), SDS((1, HEAD), F32), SDS((1, HEAD), F32)],
        scratch_shapes=[pltpu.VMEM((t, HEAD), BF16), pltpu.VMEM((t + PAD, HEAD), BF16), pltpu.VMEM((t + PAD, HEAD), BF16),
                        pltpu.VMEM((t, HEAD), F32), pltpu.VMEM((t + PAD, HEAD), F32), pltpu.VMEM((t + PAD, HEAD), F32),
                        pltpu.VMEM((3, t, HEAD), F32)],
        input_output_aliases={0: 0}, compiler_params=_params(("arbitrary", "arbitrary")),
        name="attn_bwd")(dproj, *deps, proj, proj, proj, q_gain, k_gain, bias, dy)


def _place():
    x, y, c = lax.axis_index("x"), lax.axis_index("y"), lax.axis_index("c")
    others = [(1 - x, y), (x, 1 - y), (1 - x, 1 - y)]
    return x, y, c, others


def _chunk_of(ref, kind, chip, half, shard_shape):
    r, n = shard_shape
    hr = r // 2
    if kind == "col":
        rows = pl.ds(0, r) if half is None else pl.ds(half * hr, hr)
        return ref.at[rows, pl.ds(chip * n, n)]
    rows = pl.ds(chip * r, r) if half is None else pl.ds(chip * r + half * hr, hr)
    return ref.at[rows, :]


EFFECT = pltpu.SideEffectType.DATAFLOW_SIDE_EFFECTING


def _start_copies(name, bufs, plan, n, deps):
    nb, nd = len(bufs), len(deps)

    def body(*refs):
        send, recv, token = refs[nb + nd], refs[nb + nd + 1], refs[-1]
        for cp in plan(refs[:nb], send, recv)[0]:
            cp.start()
        token[...] = jnp.zeros_like(token)

    out = pl.pallas_call(
        body, name=name,
        out_shape=(pltpu.SemaphoreType.DMA((n,)), pltpu.SemaphoreType.DMA((n,)),
                   *[pltpu.HBM(b.shape, b.dtype) for b in bufs], SDS((8, 128), F32)),
        in_specs=[HBM] * nb + [ANY] * nd,
        out_specs=(SEM, SEM, *[HBM] * nb, pl.BlockSpec(memory_space=pltpu.VMEM)),
        input_output_aliases={i: 2 + i for i in range(nb)},
        compiler_params=pltpu.CompilerParams(has_side_effects=EFFECT),
    )(*[pltpu.with_memory_space_constraint(b, pltpu.HBM) for b in bufs], *deps)
    return out[0], out[1], list(out[2:2 + nb]), out[-1]


def _wait_copies(name, bufs, send, recv, plan, after):
    nb = len(bufs)

    def body(*refs):
        sends, recvs = plan(refs[:nb], refs[nb], refs[nb + 1])
        for cp in sends:
            cp.wait_send()
        for cp in recvs:
            cp.wait_recv()

    out = pl.pallas_call(
        body, name=name, out_shape=tuple(pltpu.HBM(b.shape, b.dtype) for b in bufs),
        in_specs=[HBM] * nb + [SEM, SEM] + [ANY] * len(after), out_specs=tuple([HBM] * nb),
        input_output_aliases={i: i for i in range(nb)},
        compiler_params=pltpu.CompilerParams(has_side_effects=EFFECT),
    )(*bufs, send, recv, *after)
    return list(out)


def _remote(src, dst, send, recv, i, dev):
    return pltpu.make_async_remote_copy(src_ref=src, dst_ref=dst, send_sem=send.at[i], recv_sem=recv.at[i],
                                        device_id=dev, device_id_type=MESH)


def _plan_gather_ici(kinds, shapes):
    def plan(refs, send, recv):
        x, y, c, others = _place()
        sends, recvs = [], []
        for w, (kind, ss) in enumerate(zip(kinds, shapes)):
            for p, (px, py) in enumerate(others):
                mine = _chunk_of(refs[w], kind, 2 * x + y, c, ss)
                theirs = _chunk_of(refs[w], kind, 2 * px + py, c, ss)
                sends.append(_remote(mine, mine, send, recv, 3 * w + p, (px, py, c)))
                recvs.append(_remote(theirs, theirs, send, recv, 3 * w + p, (px, py, c)))
        return sends, recvs

    return plan, 3 * len(kinds)


def _plan_gather_pass(kinds, shapes):
    def plan(refs, send, recv):
        x, y, c, others = _place()
        sends, recvs = [], []
        for w, (kind, ss) in enumerate(zip(kinds, shapes)):
            for p, (px, py) in enumerate(others):
                got = _chunk_of(refs[w], kind, 2 * px + py, c, ss)
                coming = _chunk_of(refs[w], kind, 2 * px + py, 1 - c, ss)
                sends.append(_remote(got, got, send, recv, 3 * w + p, (x, y, 1 - c)))
                recvs.append(_remote(coming, coming, send, recv, 3 * w + p, (x, y, 1 - c)))
        return sends, recvs

    return plan, 3 * len(kinds)


def _plan_pair(kinds, shapes):
    nw = len(kinds)

    def plan(refs, send, recv):
        x, y, c, _ = _place()
        sends = []
        for w, (kind, ss) in enumerate(zip(kinds, shapes)):
            for k in range(4):
                sends.append(_remote(_chunk_of(refs[w], kind, k, 1 - c, ss), refs[nw + w].at[k], send, recv,
                                     4 * w + k, (x, y, 1 - c)))
        return sends, sends

    return plan, 4 * nw


def _plan_chip(nw):
    def plan(refs, send, recv):
        x, y, c, others = _place()
        sends = []
        for w in range(nw):
            for p, (px, py) in enumerate(others):
                sends.append(_remote(refs[w].at[2 * px + py], refs[nw + w].at[p], send, recv, 3 * w + p, (px, py, c)))
        return sends, sends

    return plan, 3 * nw


def _plan_share(nw):
    def plan(refs, send, recv):
        x, y, c, _ = _place()
        sends = [_remote(refs[w].at[c], refs[w].at[c], send, recv, w, (x, y, 1 - c)) for w in range(nw)]
        recvs = [_remote(refs[w].at[1 - c], refs[w].at[1 - c], send, recv, w, (x, y, 1 - c)) for w in range(nw)]
        return sends, recvs

    return plan, nw


def _pair_add(grad, got, kind, shard_shape, pos, name):
    r, n = shard_shape
    hr = r // 2
    tr, tn = _tile(hr, 256, 16), _tile(n, 1408)
    nr, nn = hr // tr, n // tn
    if kind == "col":
        g_spec = pl.BlockSpec((tr, tn), lambda k, i, j, c: (c[1] * nr + i, k * nn + j))
    else:
        g_spec = pl.BlockSpec((tr, tn), lambda k, i, j, c: ((2 * k + c[1]) * nr + i, j))
    o_spec = pl.BlockSpec((1, tr, tn), lambda k, i, j, c: (k, i, j))

    def body(c_ref, g_ref, r_ref, o32_ref, o16_ref):
        s = g_ref[...] + r_ref[0]
        o32_ref[0] = s
        o16_ref[0] = s.astype(BF16)

    return pl.pallas_call(
        body,
        grid_spec=pltpu.PrefetchScalarGridSpec(num_scalar_prefetch=1, grid=(4, nr, nn), in_specs=[g_spec, o_spec],
                                               out_specs=[o_spec, o_spec]),
        out_shape=[SDS((4, hr, n), F32), SDS((4, hr, n), BF16)],
        compiler_params=_params(("parallel", "parallel", "parallel")), name=name)(pos, grad, got)


def _chip_add(part32, got16, pos, name):
    _, hr, n = part32.shape
    tr, tn = _tile(hr, 256, 16), _tile(n, 1408)
    own = pl.BlockSpec((1, tr, tn), lambda i, j, p: (p[0], i, j))
    oth = pl.BlockSpec((3, tr, tn), lambda i, j, p: (0, i, j))

    def body(p_ref, own_ref, oth_ref, o_ref):
        o_ref[0] = ((own_ref[0] + oth_ref[0].astype(F32)) + oth_ref[1].astype(F32)) + oth_ref[2].astype(F32)

    return pl.pallas_call(
        body,
        grid_spec=pltpu.PrefetchScalarGridSpec(num_scalar_prefetch=1, grid=(hr // tr, n // tn), in_specs=[own, oth],
                                               out_specs=pl.BlockSpec((1, tr, tn), lambda i, j, p: (p[1], i, j))),
        out_shape=SDS((2, hr, n), F32), compiler_params=_params(("parallel", "parallel")), name=name)(pos, part32, got16)


def _adamw_math(w, g, m, v):
    m = ADAM_B1 * m + (1.0 - ADAM_B1) * g
    v = ADAM_B2 * v + (1.0 - ADAM_B2) * (g * g)
    m_hat = m / (1.0 - ADAM_B1 ** ADAM_STEP)
    v_hat = v / (1.0 - ADAM_B2 ** ADAM_STEP)
    return -ADAM_LR * (m_hat / (jnp.sqrt(v_hat) + ADAM_EPS) + ADAM_WD * w), m, v


def _adamw(w, g, m, v, name):
    r, n = w.shape
    tr, tn = _tile(r, 256, 16), _tile(n, 1408)

    def body(w_ref, g_ref, m_ref, v_ref, d_ref, nm_ref, nv_ref):
        d_ref[...], nm_ref[...], nv_ref[...] = _adamw_math(w_ref[...], g_ref[...], m_ref[...], v_ref[...])

    tile = pl.BlockSpec((tr, tn), lambda i, j: (i, j))
    return pl.pallas_call(
        body, grid=(r // tr, n // tn), in_specs=[tile] * 4, out_specs=[tile] * 3, out_shape=[SDS((r, n), F32)] * 3,
        compiler_params=_params(("parallel", "parallel")), name=name)(w, g, m, v)


def _small_allreduce_adamw(g, w, m, v):
    length = g.shape[1]

    def body(g_ref, w_ref, m_ref, v_ref, gs_ref, d_ref, nm_ref, nv_ref, buf, send, recv):
        x, y, c = lax.axis_index("x"), lax.axis_index("y"), lax.axis_index("c")
        me = 4 * x + 2 * y + c
        buf[me] = g_ref[...]
        cps = []
        for d in range(1, 8):
            peer = (x ^ (d >> 2), y ^ ((d >> 1) & 1), c ^ (d & 1))
            cp = pltpu.make_async_remote_copy(src_ref=buf.at[me], dst_ref=buf.at[me], send_sem=send.at[d - 1],
                                              recv_sem=recv.at[d - 1], device_id=peer, device_id_type=MESH)
            cp.start()
            cps.append(cp)
        for cp in cps:
            cp.wait()
        total = buf[0]
        for d in range(1, 8):
            total = total + buf[d]
        gs_ref[...] = total
        d_ref[...], nm_ref[...], nv_ref[...] = _adamw_math(w_ref[...], total, m_ref[...], v_ref[...])

    vm = pl.BlockSpec(memory_space=pltpu.VMEM)
    return pl.pallas_call(
        body, in_specs=[vm] * 4, out_specs=[vm] * 4, out_shape=[SDS((1, length), F32)] * 4,
        scratch_shapes=[pltpu.VMEM((8, 1, length), F32), pltpu.SemaphoreType.DMA((7,)), pltpu.SemaphoreType.DMA((7,))],
        compiler_params=pltpu.CompilerParams(has_side_effects=True), name="small_allreduce_adamw")(g, w, m, v)


def kernel(x, w_in, b_gate, norm_mix, norm_ffn, hgrn_lb_logits, hgrn_out_gain, q_gain, k_gain, rel_bias, w_proj_a, w_proj_b, w_out, w_ffn_in, w_ffn_out, loss_target, m_w_in, m_b_gate, m_norm_mix, m_norm_ffn, m_hgrn_lb_logits, m_hgrn_out_gain, m_q_gain, m_k_gain, m_rel_bias, m_w_proj_a, m_w_proj_b, m_w_out, m_w_ffn_in, m_w_ffn_out, v_w_in, v_b_gate, v_norm_mix, v_norm_ffn, v_hgrn_lb_logits, v_hgrn_out_gain, v_q_gain, v_k_gain, v_rel_bias, v_w_proj_a, v_w_proj_b, v_w_out, v_w_ffn_in, v_w_ffn_out):
    t, d = x.shape[1], x.shape[2]
    d_a = hgrn_out_gain.shape[1]
    h_a = d_a // HEAD
    h_b = rel_bias.shape[1]
    d_b = h_b * HEAD
    x0 = x.reshape(t, d)
    target = loss_target.reshape(t, d)
    pos = jnp.stack([2 * lax.axis_index("x") + lax.axis_index("y"), lax.axis_index("c")]).astype(jnp.int32)

    names = ["w_in", "w_proj_a", "w_proj_b", "w_out", "w_ffn_in", "w_ffn_out"]
    big = dict(zip(names, [w_in[0], w_proj_a[0], w_proj_b[0], w_out[0], w_ffn_in[0], w_ffn_out[0]]))
    big_m = dict(zip(names, [m_w_in[0], m_w_proj_a[0], m_w_proj_b[0], m_w_out[0], m_w_ffn_in[0], m_w_ffn_out[0]]))
    big_v = dict(zip(names, [v_w_in[0], v_w_proj_a[0], v_w_proj_b[0], v_w_out[0], v_w_ffn_in[0], v_w_ffn_out[0]]))
    kind = dict(zip(names, ["col", "col", "col", "row", "col", "row"]))
    shape = {nm: big[nm].shape for nm in names}

    def gather_start(tag, group, deps):
        plan, n = _plan_gather_ici([kind[g] for g in group], [shape[g] for g in group])
        fulls = [_cast_into_full(big[g], kind[g], pos, "cast_" + g) for g in group]
        send, recv, bufs, token = _start_copies("gather_ici_start_" + tag, fulls, plan, n, deps)
        return (tag, group, plan, send, recv, bufs), token

    def gather_pass(state, after):
        tag, group, plan, send, recv, bufs = state
        bufs = _wait_copies("gather_ici_wait_" + tag, bufs, send, recv, plan, after)
        plan, n = _plan_gather_pass([kind[g] for g in group], [shape[g] for g in group])
        send, recv, bufs, token = _start_copies("gather_pass_start_" + tag, bufs, plan, n, ())
        return (tag, group, plan, send, recv, bufs), token

    def gather_done(state, after):
        tag, group, plan, send, recv, bufs = state
        return _wait_copies("gather_pass_wait_" + tag, bufs, send, recv, plan, after)

    def reduce_start(tag, group, grads, deps):
        plan, n = _plan_pair([kind[g] for g in group], [shape[g] for g in group])
        lands = [lax.empty((4, shape[g][0] // 2, shape[g][1]), F32) for g in group]
        send, recv, bufs, token = _start_copies("pair_start_" + tag, list(grads) + lands, plan, n, deps)
        return dict(tag=tag, group=group, plan=plan, send=send, recv=recv, bufs=bufs), token

    def reduce_pair_done(st, after):
        tag, group, nw = st["tag"], st["group"], len(st["group"])
        bufs = _wait_copies("pair_wait_" + tag, st["bufs"], st["send"], st["recv"], st["plan"], after)
        parts = [_pair_add(g, l, kind[nm], shape[nm], pos, "pair_add_" + nm)
                 for g, l, nm in zip(bufs[:nw], bufs[nw:], group)]
        lands = [lax.empty((3, shape[g][0] // 2, shape[g][1]), BF16) for g in group]
        plan, n = _plan_chip(nw)
        send, recv, bufs, token = _start_copies("chip_start_" + tag, [p16 for _, p16 in parts] + lands, plan, n, ())
        return dict(st, plan=plan, send=send, recv=recv, bufs=bufs, p32=[p32 for p32, _ in parts]), token

    def reduce_chip_done(st, after):
        tag, group, nw = st["tag"], st["group"], len(st["group"])
        bufs = _wait_copies("chip_wait_" + tag, st["bufs"], st["send"], st["recv"], st["plan"], after)
        finals = [_chip_add(p32, got, pos, "chip_add_" + nm) for p32, got, nm in zip(st["p32"], bufs[nw:], group)]
        plan, n = _plan_share(nw)
        send, recv, bufs, token = _start_copies("share_start_" + tag, finals, plan, n, ())
        return dict(st, plan=plan, send=send, recv=recv, bufs=bufs), token

    g_big, upd = {}, {}

    def reduce_finish(st, after):
        bufs = _wait_copies("share_wait_" + st["tag"], st["bufs"], st["send"], st["recv"], st["plan"], after)
        for full, nm in zip(bufs, st["group"]):
            g_big[nm] = full.reshape(shape[nm])
            upd[nm] = _adamw(big[nm], g_big[nm], big_m[nm], big_v[nm], "adamw_" + nm)

    ga, token = gather_start("a", ["w_in"], ())
    gb, token = gather_start("b", ["w_proj_a", "w_proj_b", "w_out"], (token,))
    gc, token = gather_start("c", ["w_ffn_in", "w_ffn_out"], (token,))
    h1, r1 = _rmsnorm_fwd(x0, norm_mix, "rmsnorm_mix")
    ga, token = gather_pass(ga, (h1, token))
    (wg_in,) = gather_done(ga, ())
    proj = _matmul(h1, wg_in, name="proj_in")
    y_a, o_pre, states = _hgrn_fwd(proj, hgrn_lb_logits, hgrn_out_gain, h_a)
    gb, token = gather_pass(gb, (y_a,))
    rb = jnp.pad(rel_bias[0], ((0, 0), (0, REL_LANES - N_REL)))
    bias = _relbias_expand(rb).transpose(1, 0, 2)
    col_b = 4 * d_a // HEAD
    y_b = _attn_fwd(proj, q_gain, k_gain, bias, h_b, col_b)
    wg_pa, wg_pb, wg_out = gather_done(gb, (y_b,))
    gc, token = gather_pass(gc, (y_b,))
    pa = _matmul(y_a, wg_pa, name="proj_a", deps=(token,))
    pb = _matmul(y_b, wg_pb, name="proj_b")
    gate_off = 4 * d_a + 3 * d_b
    merged = _merge_fwd(proj, b_gate, pa, pb, gate_off)
    x2 = _matmul(merged, wg_out, res=x0, name="out_proj")
    h2, r2 = _rmsnorm_fwd(x2, norm_ffn, "rmsnorm_ffn")
    wg_fin, wg_fout = gather_done(gc, (h2,))
    gu = _matmul(h2, wg_fin, name="ffn_in")
    act = _swiglu_fwd(gu)
    y = _matmul(act, wg_fout, res=x2, name="ffn_out")
    dy, dy16, loss_part = _loss_head(y, target)

    g_fout = _matmul(act, dy16, ta=True, name="dw_ffn_out")
    r_fout, token = reduce_start("fout", ["w_ffn_out"], [g_fout], ())
    dact = _matmul(dy16, wg_fout, tb=True, name="d_act", deps=(token,))
    r_fout, token = reduce_pair_done(r_fout, (dact,))
    dgu = _swiglu_bwd(dact, gu)
    g_fin = _matmul(h2, dgu, ta=True, name="dw_ffn_in", deps=(token,))
    r_fin, token = reduce_start("fin", ["w_ffn_in"], [g_fin], ())
    dh2 = _matmul(dgu, wg_fin, tb=True, name="d_h2", deps=(token,))
    r_fout, token_a = reduce_chip_done(r_fout, (dh2,))
    r_fin, token_b = reduce_pair_done(r_fin, (dh2,))
    dx2, dx2_16, g_norm_ffn = _rmsnorm_bwd(dh2, x2, r2, norm_ffn, dy, "rmsnorm_ffn_bwd")
    dmerged = _matmul(dx2_16, wg_out, tb=True, name="d_merged", deps=(token_a, token_b))
    dp_ab, dproj, g_bgate = _merge_bwd(dmerged, proj, b_gate, pa, pb, gate_off)
    g_out = _matmul(merged, dx2_16, ta=True, name="dw_out")
    g_pa = _matmul(y_a, dp_ab[0], ta=True, name="dw_proj_a")
    g_pb = _matmul(y_b, dp_ab[1], ta=True, name="dw_proj_b")
    r_mid, token = reduce_start("mid", ["w_proj_a", "w_proj_b", "w_out"], [g_pa, g_pb, g_out], ())
    dy_a = _matmul(dp_ab[0], wg_pa, tb=True, name="d_y_a", deps=(token,))
    dy_b = _matmul(dp_ab[1], wg_pb, tb=True, name="d_y_b")
    reduce_finish(r_fout, (dy_b,))
    r_fin, token_a = reduce_chip_done(r_fin, (dy_b,))
    r_mid, token_b = reduce_pair_done(r_mid, (dy_b,))
    dproj, dbias, g_qg, g_kg = _attn_bwd(dproj, proj, q_gain, k_gain, bias, dy_b, h_b, col_b, deps=(token_a, token_b))
    r_mid, token = reduce_chip_done(r_mid, (dbias,))
    dproj, g_lb, g_gain = _hgrn_bwd(dproj, proj, o_pre, states, dy_a, hgrn_lb_logits, hgrn_out_gain, h_a, deps=(token,))
    g_in = _matmul(h1, dproj, ta=True, name="dw_in")
    r_in, token = reduce_start("in", ["w_in"], [g_in], ())
    g_rb = _relbias_reduce(dbias.transpose(1, 0, 2))[:, :N_REL]
    reduce_finish(r_fin, (token,))
    reduce_finish(r_mid, (token,))
    r_in, token = reduce_pair_done(r_in, (g_rb, upd["w_ffn_in"][0], upd["w_out"][0], upd["w_proj_a"][0], upd["w_proj_b"][0]))
    dh1 = _matmul(dproj, wg_in, tb=True, name="d_h1", deps=(token,))
    dx, _, g_norm_mix = _rmsnorm_bwd(dh1, x0, r1, norm_mix, dx2, "rmsnorm_mix_bwd")

    small_w = [b_gate, norm_mix, norm_ffn, hgrn_lb_logits, hgrn_out_gain, q_gain, k_gain, rel_bias]
    small_m = [m_b_gate, m_norm_mix, m_norm_ffn, m_hgrn_lb_logits, m_hgrn_out_gain, m_q_gain, m_k_gain, m_rel_bias]
    small_v = [v_b_gate, v_norm_mix, v_norm_ffn, v_hgrn_lb_logits, v_hgrn_out_gain, v_q_gain, v_k_gain, v_rel_bias]
    small_g = [g_bgate, g_norm_mix, g_norm_ffn, g_lb, g_gain, g_qg, g_kg, g_rb]
    sizes = [w.size for w in small_w]
    length = -(-(sum(sizes) + 1) // 128) * 128

    def pack(parts_):
        flat = jnp.concatenate([p.reshape(1, -1) for p in parts_], axis=1)
        return jnp.pad(flat, ((0, 0), (0, length - flat.shape[1])))

    one = jnp.ones((1, 1), F32)
    packed = _small_allreduce_adamw(pack(small_g + [loss_part]), pack(small_w + [one]), pack(small_m + [one]),
                                    pack(small_v + [one]))

    def unpack(vec):
        out, at = [], 0
        for w, n in zip(small_w, sizes):
            out.append(vec[0, at:at + n].reshape(w.shape))
            at += n
        return out, vec[0, at]

    (sg, loss), (sd, _), (sm, _), (sv, _) = [unpack(p) for p in packed]
    r_in, token = reduce_chip_done(r_in, (packed[0],))
    reduce_finish(r_in, ())

    def ordered(small, bigs):
        bigs = [bigs[nm][None] for nm in names]
        return [bigs[0]] + small + bigs[1:]

    return (loss, dx.reshape(x.shape), *ordered(sg, g_big), *ordered(sd, {nm: upd[nm][0] for nm in names}),
            *ordered(sm, {nm: upd[nm][1] for nm in names}), *ordered(sv, {nm: upd[nm][2] for nm in names}))
```

```python
import functools

import jax
import jax.numpy as jnp
from jax import lax
from jax.experimental import pallas as pl
from jax.experimental.pallas import tpu as pltpu

F32 = jnp.float32
BF16 = jnp.bfloat16
SDS = jax.ShapeDtypeStruct
MESH = pl.DeviceIdType.MESH
HIGHEST = lax.Precision.HIGHEST

CHUNK = 64
SUB = 16
HEAD = 128
N_PAST = 8
BAND = (N_PAST + 1) * CHUNK
PAD = N_PAST * CHUNK
REL_FUTURE = CHUNK - 1
REL_PAST = 2 * CHUNK - 1
N_REL = REL_FUTURE + REL_PAST + 1
REL_LANES = 256
EPS = 1e-6
MIX_HEADS = 2
MIX_UNROLL = 4
MIX_UNROLL_BWD = 2
ATT_UNROLL = 8
ATT_UNROLL_BWD = 4
EXP_CLAMP = 80.0

ADAM_LR = 0.001
ADAM_B1 = 0.9
ADAM_B2 = 0.999
ADAM_EPS = 1e-08
ADAM_WD = 0.01
ADAM_STEP = 10

VMEM_LIMIT = 56 * 1024 * 1024

HBM = pl.BlockSpec(memory_space=pltpu.HBM)
ANY = pl.BlockSpec(memory_space=pl.ANY)
SEM = pl.BlockSpec(memory_space=pltpu.SEMAPHORE)

NT = (((1,), (1,)), ((), ()))
TN = (((0,), (0,)), ((), ()))
NN = (((1,), (0,)), ((), ()))


def _params(sem=None, **kw):
    return pltpu.CompilerParams(dimension_semantics=sem, vmem_limit_bytes=VMEM_LIMIT, **kw)


def _tile(n, pref, unit=128):
    if n <= pref:
        return n
    t = pref - pref % unit
    while n % t:
        t -= unit
    return t


def _loop(n, unroll, step, init):
    assert n % unroll == 0, (n, unroll)

    def several(i, carry):
        for u in range(unroll):
            carry = step(i * unroll + u, carry)
        return carry

    return lax.fori_loop(0, n // unroll, several, init)


def _sigmoid(x):
    return 1.0 / (1.0 + jnp.exp(-x))


def _dsilu(x, s):
    return s * (1.0 + x * (1.0 - s))


def _bdot(a, b, dims=NN):
    return lax.dot_general(a.astype(BF16), b.astype(BF16), dims, preferred_element_type=F32)


def _split(a):
    hi = a.astype(BF16)
    return hi, (a - hi.astype(F32)).astype(BF16)


def _dot3(a, b, dims):
    dot = lambda u, v: lax.dot_general(u, v, dims, preferred_element_type=F32)
    return dot(a[0], b[1]) + dot(a[1], b[0]) + dot(a[0], b[0])


def _fdot(a, b):
    return lax.dot_general(a, b, NN, precision=HIGHEST, preferred_element_type=F32)


MM_TILE_K = 5632
MM_TILE_N = 512


def _matmul(a, b, *, ta=False, tb=False, res=None, out_dtype=F32, name, deps=()):
    m, k = (a.shape[1], a.shape[0]) if ta else a.shape
    n = b.shape[0] if tb else b.shape[1]
    tk = _tile(k, MM_TILE_K)
    nk = k // tk
    tm, tn = _tile(m, 2048 if tk <= MM_TILE_K // 2 else 1024), _tile(n, MM_TILE_N)
    dims = ((((0,) if ta else (1,)), ((1,) if tb else (0,))), ((), ()))

    def body(*refs):
        n_in = 2 + (res is not None)
        a_ref, b_ref = refs[:2]
        r_ref = refs[2] if res is not None else None
        o_ref = refs[n_in + len(deps)]
        part = lax.dot_general(a_ref[...].astype(BF16), b_ref[...].astype(BF16), dims, preferred_element_type=F32)

        def finish(out):
            if r_ref is not None:
                out = out + r_ref[...]
            o_ref[...] = out.astype(o_ref.dtype)

        if nk == 1:
            finish(part)
            return
        acc_ref = refs[-1]
        kk = pl.program_id(2)

        @pl.when(kk == 0)
        def _():
            acc_ref[...] = part

        @pl.when(jnp.logical_and(kk > 0, kk < nk - 1))
        def _():
            acc_ref[...] += part

        @pl.when(kk == nk - 1)
        def _():
            finish(acc_ref[...] + part)

    a_spec = pl.BlockSpec((tk, tm), lambda i, j, l: (l, i)) if ta else pl.BlockSpec((tm, tk), lambda i, j, l: (i, l))
    b_spec = pl.BlockSpec((tn, tk), lambda i, j, l: (j, l)) if tb else pl.BlockSpec((tk, tn), lambda i, j, l: (l, j))
    o_spec = pl.BlockSpec((tm, tn), lambda i, j, l: (i, j))
    in_specs = [a_spec, b_spec] + ([o_spec] if res is not None else []) + [ANY] * len(deps)
    args = (a, b) + ((res,) if res is not None else ()) + tuple(deps)
    return pl.pallas_call(
        body, grid=(m // tm, n // tn, nk), in_specs=in_specs, out_specs=o_spec,
        out_shape=SDS((m, n), out_dtype), scratch_shapes=[pltpu.VMEM((tm, tn), F32)] if nk > 1 else [],
        compiler_params=_params(("parallel", "parallel", "arbitrary")), name=name)(*args)


def _cast_into_full(w, kind, pos, name):
    r, n = w.shape
    tr = _tile(r, 512, 16)
    nr = r // tr
    if kind == "col":
        shape, o_spec = (r, 4 * n), pl.BlockSpec((tr, n), lambda i, p: (i, p[0]))
    else:
        shape, o_spec = (4 * r, n), pl.BlockSpec((tr, n), lambda i, p: (p[0] * nr + i, 0))

    def body(p_ref, w_ref, o_ref):
        o_ref[...] = w_ref[...].astype(BF16)

    return pl.pallas_call(
        body,
        grid_spec=pltpu.PrefetchScalarGridSpec(num_scalar_prefetch=1, grid=(nr,),
                                               in_specs=[pl.BlockSpec((tr, n), lambda i, p: (i, 0))], out_specs=o_spec),
        out_shape=SDS(shape, BF16), compiler_params=_params(("parallel",)), name=name)(pos, w)


def _rmsnorm_fwd(x, gain, name):
    t, d = x.shape
    tm = _tile(t, 256)

    def body(x_ref, g_ref, h_ref, r_ref):
        xv = x_ref[...]
        r = lax.rsqrt(jnp.mean(xv * xv, axis=-1, keepdims=True) + EPS)
        h_ref[...] = (xv * r * g_ref[...]).astype(BF16)
        r_ref[...] = r

    return pl.pallas_call(
        body, grid=(t // tm,),
        in_specs=[pl.BlockSpec((tm, d), lambda i: (i, 0)), pl.BlockSpec((1, d), lambda i: (0, 0))],
        out_specs=[pl.BlockSpec((tm, d), lambda i: (i, 0)), pl.BlockSpec((tm, 1), lambda i: (i, 0))],
        out_shape=[SDS((t, d), BF16), SDS((t, 1), F32)], compiler_params=_params(("parallel",)), name=name)(x, gain)


def _rmsnorm_bwd(dh, x, r, gain, dres, name):
    t, d = x.shape
    tm = _tile(t, 256)

    def body(dh_ref, x_ref, r_ref, g_ref, dres_ref, dx_ref, dxb_ref, dg_ref):
        @pl.when(pl.program_id(0) == 0)
        def _():
            dg_ref[...] = jnp.zeros_like(dg_ref)

        dhv, xv, rv = dh_ref[...], x_ref[...], r_ref[...]
        dg_ref[...] += jnp.sum(dhv * (xv * rv), axis=0, keepdims=True)
        u = dhv * g_ref[...]
        dx = dres_ref[...] + rv * u - xv * (rv * rv * rv) * jnp.mean(u * xv, axis=-1, keepdims=True)
        dx_ref[...] = dx
        dxb_ref[...] = dx.astype(BF16)

    row = pl.BlockSpec((tm, d), lambda i: (i, 0))
    vec = pl.BlockSpec((1, d), lambda i: (0, 0))
    return pl.pallas_call(
        body, grid=(t // tm,), in_specs=[row, row, pl.BlockSpec((tm, 1), lambda i: (i, 0)), vec, row],
        out_specs=[row, row, vec], out_shape=[SDS((t, d), F32), SDS((t, d), BF16), SDS((1, d), F32)],
        compiler_params=_params(("arbitrary",)), name=name)(dh, x, r, gain, dres)


def _merge_fwd(proj, b_gate, pa, pb, off):
    t, d = pa.shape
    tm, tc = _tile(t, 512), _tile(d, 512)
    nj = d // tc
    oa, ob = off // tc, off // tc + nj

    def body(la_ref, lb_ref, ba_ref, bb_ref, pa_ref, pb_ref, o_ref):
        ga = _sigmoid(la_ref[...] + ba_ref[...])
        gb = _sigmoid(lb_ref[...] + bb_ref[...])
        o_ref[...] = (ga * pa_ref[...] + gb * pb_ref[...]).astype(BF16)

    tile = pl.BlockSpec((tm, tc), lambda i, j: (i, j))
    return pl.pallas_call(
        body, grid=(t // tm, nj),
        in_specs=[pl.BlockSpec((tm, tc), lambda i, j: (i, oa + j)), pl.BlockSpec((tm, tc), lambda i, j: (i, ob + j)),
                  pl.BlockSpec((1, tc), lambda i, j: (0, j)), pl.BlockSpec((1, tc), lambda i, j: (0, nj + j)), tile, tile],
        out_specs=tile, out_shape=SDS((t, d), BF16), compiler_params=_params(("parallel", "parallel")),
        name="merge_fwd")(proj, proj, b_gate, b_gate, pa, pb)


def _merge_bwd(dmerged, proj, b_gate, pa, pb, off):
    t, d = pa.shape
    tm, tc = _tile(t, 512), _tile(d, 512)
    nj, ni = d // tc, t // tm
    o0 = off // tc

    def body(dm_ref, l_ref, b_ref, pa_ref, pb_ref, dp_ref, dl_ref, db_ref):
        s, i = pl.program_id(0), pl.program_id(2)
        p = jnp.where(s == 0, pa_ref[...], pb_ref[...])
        g = _sigmoid(l_ref[...] + b_ref[...])
        dm = dm_ref[...]
        dp_ref[0] = (dm * g).astype(BF16)
        dl = dm * p * g * (1.0 - g)
        dl_ref[...] = dl.astype(BF16)

        @pl.when(i == 0)
        def _():
            db_ref[...] = jnp.zeros_like(db_ref)

        db_ref[...] += jnp.sum(dl, axis=0, keepdims=True)

    tile = pl.BlockSpec((tm, tc), lambda s, j, i: (i, j))
    return pl.pallas_call(
        body, grid=(2, nj, ni),
        in_specs=[tile, pl.BlockSpec((tm, tc), lambda s, j, i: (i, o0 + s * nj + j)),
                  pl.BlockSpec((1, tc), lambda s, j, i: (0, s * nj + j)), tile, tile],
        out_specs=[pl.BlockSpec((1, tm, tc), lambda s, j, i: (s, i, j)),
                   pl.BlockSpec((tm, tc), lambda s, j, i: (i, o0 + s * nj + j)),
                   pl.BlockSpec((1, tc), lambda s, j, i: (0, s * nj + j))],
        out_shape=[SDS((2, t, d), BF16), SDS(proj.shape, BF16), SDS((1, 2 * d), F32)],
        compiler_params=_params(("arbitrary", "arbitrary", "arbitrary")),
        name="merge_bwd")(dmerged, proj, b_gate, pa, pb)


def _swiglu_fwd(gu):
    t, f2 = gu.shape
    f = f2 // 2
    tm, tc = _tile(t, 512), _tile(f, 512)
    nj = f // tc

    def body(g_ref, u_ref, o_ref):
        g = g_ref[...]
        o_ref[...] = (g * _sigmoid(g) * u_ref[...]).astype(BF16)

    return pl.pallas_call(
        body, grid=(t // tm, nj),
        in_specs=[pl.BlockSpec((tm, tc), lambda i, j: (i, j)), pl.BlockSpec((tm, tc), lambda i, j: (i, nj + j))],
        out_specs=pl.BlockSpec((tm, tc), lambda i, j: (i, j)), out_shape=SDS((t, f), BF16),
        compiler_params=_params(("parallel", "parallel")), name="swiglu_fwd")(gu, gu)


def _swiglu_bwd(dact, gu):
    t, f2 = gu.shape
    f = f2 // 2
    tm = _tile(t, 128)

    def body(d_ref, gu_ref, o_ref):
        g, u, dv = gu_ref[:, :f], gu_ref[:, f:], d_ref[...]
        sg = _sigmoid(g)
        o_ref[:, :f] = (dv * u * _dsilu(g, sg)).astype(BF16)
        o_ref[:, f:] = (dv * (g * sg)).astype(BF16)

    return pl.pallas_call(
        body, grid=(t // tm,),
        in_specs=[pl.BlockSpec((tm, f), lambda i: (i, 0)), pl.BlockSpec((tm, f2), lambda i: (i, 0))],
        out_specs=pl.BlockSpec((tm, f2), lambda i: (i, 0)), out_shape=SDS((t, f2), BF16),
        compiler_params=_params(("parallel",)), name="swiglu_bwd")(dact, gu)


def _loss_head(y, target):
    t, d = y.shape
    tm = _tile(t, 256)

    def body(y_ref, t_ref, dy_ref, dyb_ref, l_ref):
        @pl.when(pl.program_id(0) == 0)
        def _():
            l_ref[...] = jnp.zeros_like(l_ref)

        e = y_ref[...] - t_ref[...]
        dy = e * (1.0 / d)
        dy_ref[...] = dy
        dyb_ref[...] = dy.astype(BF16)
        l_ref[...] += 0.5 * jnp.sum(jnp.mean(e * e, axis=-1, keepdims=True), axis=0, keepdims=True)

    row = pl.BlockSpec((tm, d), lambda i: (i, 0))
    return pl.pallas_call(
        body, grid=(t // tm,), in_specs=[row, row], out_specs=[row, row, pl.BlockSpec((1, 1), lambda i: (0, 0))],
        out_shape=[SDS((t, d), F32), SDS((t, d), BF16), SDS((1, 1), F32)], compiler_params=_params(("arbitrary",)),
        name="loss_head")(y, target)


def _rel_onehot(qi):
    p = lax.broadcasted_iota(jnp.int32, (REL_LANES, BAND), 1)
    r = lax.broadcasted_iota(jnp.int32, (REL_LANES, BAND), 0)
    idx = jnp.clip(qi + PAD - p, -REL_FUTURE, REL_PAST) + REL_FUTURE
    return (idx == r).astype(F32)


def _relbias_expand(rb):
    h = rb.shape[0]

    def body(rb_ref, o_ref):
        def step(qi, _):
            o_ref[qi] = _fdot(rb_ref[...], _rel_onehot(qi))
            return 0

        lax.fori_loop(0, CHUNK, step, 0)

    return pl.pallas_call(body, out_shape=SDS((CHUNK, h, BAND), F32), compiler_params=_params(),
                          name="relbias_expand")(rb)


def _relbias_reduce(dbias):
    h = dbias.shape[1]

    def body(db_ref, o_ref):
        def step(qi, acc):
            return acc + lax.dot_general(db_ref[qi], _rel_onehot(qi), NT, precision=HIGHEST,
                                         preferred_element_type=F32)

        o_ref[...] = lax.fori_loop(0, CHUNK, step, jnp.zeros((h, REL_LANES), F32))

    return pl.pallas_call(body, out_shape=SDS((h, REL_LANES), F32), compiler_params=_params(),
                          name="relbias_reduce")(dbias)


def _lower_bound(l_ref):
    l0, l1 = l_ref[0:1, :], l_ref[1:2, :]
    m = jnp.maximum(l0, l1)
    e0, e1 = jnp.exp(l0 - m), jnp.exp(l1 - m)
    return e0 / (e0 + e1)


def _tri(lower):
    r = lax.broadcasted_iota(jnp.int32, (CHUNK, CHUNK), 0)
    c = lax.broadcasted_iota(jnp.int32, (CHUNK, CHUNK), 1)
    return r >= c if lower else r <= c


def _hgrn_intra(qs, kk, b_s):
    rows = lax.broadcasted_iota(jnp.int32, (CHUNK, HEAD), 0)
    tr = lax.broadcasted_iota(jnp.int32, (SUB, CHUNK), 0)
    tc = lax.broadcasted_iota(jnp.int32, (SUB, CHUNK), 1)
    b = b_s[...]
    out = []
    for i in range(CHUNK // SUB):
        lo = i * SUB
        ref = jnp.zeros((1, HEAD), F32) if i == 0 else b_s[lo - 1:lo, :]
        eq = jnp.exp(b[lo:lo + SUB] - ref)
        qt = _split(qs[lo:lo + SUB] * eq)
        e = jnp.where(rows < lo + SUB, jnp.exp(jnp.minimum(ref - b, EXP_CLAMP)), 0.0)
        kt = _split(kk * e)
        a = jnp.where(tc <= tr + lo, _dot3(qt, kt, NT), 0.0)
        out.append((eq, qt, e, kt, a))
    return out


def _hgrn_fwd(proj, lb_logits, gain, n_heads):
    t = proj.shape[0]
    nc = t // CHUNK
    da = n_heads * HEAD
    hp = MIX_HEADS
    wide = hp * HEAD

    def body(q_ref, f_ref, i_ref, g_ref, l_ref, gain_ref, y_ref, o_ref, st_ref, state, b_s):
        state[...] = jnp.zeros_like(state)
        lb_all = _lower_bound(l_ref)
        tril = _tri(True).astype(F32)

        def one_head(hh, j):
            sl = pl.ds(pl.multiple_of(j * CHUNK, CHUNK), CHUNK)
            cols = slice(hh * HEAD, (hh + 1) * HEAD)
            lb = lb_all[:, cols]
            fg = lb + (1.0 - lb) * _sigmoid(f_ref[sl, cols])
            kk = 1.0 - fg
            qv = q_ref[sl, cols]
            qs = qv * _sigmoid(qv)
            vb = i_ref[sl, cols].astype(BF16)
            b = _fdot(tril, jnp.log(fg))
            b_s[hh] = b
            s_in = state[hh]
            st_ref[hh, j] = s_in
            o = _bdot(qs * jnp.exp(b), s_in, NT)
            a = jnp.concatenate([blk[4] for blk in _hgrn_intra(qs, kk, b_s.at[hh])], axis=0)
            o = o + _bdot(a, vb)
            o_ref[sl, cols] = o
            bl = b_s[hh, CHUNK - 1:CHUNK, :]
            state[hh] = s_in * jnp.exp(bl) + _bdot(vb, kk * jnp.exp(bl - b), TN)
            rr = lax.rsqrt(jnp.mean(o * o, axis=-1, keepdims=True) + EPS)
            gv = g_ref[sl, cols]
            y_ref[sl, cols] = (o * rr * gain_ref[:, cols] * (gv * _sigmoid(gv))).astype(BF16)

        def chunk(j, _):
            for hh in range(hp):
                one_head(hh, j)
            return 0

        _loop(nc, MIX_UNROLL, chunk, 0)

    col = lambda k: pl.BlockSpec((t, wide), lambda h: (0, k * (n_heads // hp) + h))
    vec = pl.BlockSpec((1, wide), lambda h: (0, h))
    return pl.pallas_call(
        body, grid=(n_heads // hp,),
        in_specs=[col(0), col(1), col(2), col(3), pl.BlockSpec((2, wide), lambda h: (0, h)), vec],
        out_specs=[pl.BlockSpec((t, wide), lambda h: (0, h)), pl.BlockSpec((t, wide), lambda h: (0, h)),
                   pl.BlockSpec((hp, nc, HEAD, HEAD), lambda h: (h, 0, 0, 0))],
        out_shape=[SDS((t, da), BF16), SDS((t, da), F32), SDS((n_heads, nc, HEAD, HEAD), F32)],
        scratch_shapes=[pltpu.VMEM((hp, HEAD, HEAD), F32), pltpu.VMEM((hp, CHUNK, HEAD), F32)],
        compiler_params=_params(("parallel",)), name="hgrn_fwd")(proj, proj, proj, proj, lb_logits, gain)


def _hgrn_bwd(dproj, proj, o_pre, states, dy, lb_logits, gain, n_heads, deps=()):
    t = proj.shape[0]
    nc = t // CHUNK
    da = n_heads * HEAD
    hp = MIX_HEADS
    wide = hp * HEAD

    def body(*refs):
        (q_ref, f_ref, i_ref, g_ref, o_ref, st_ref, dy_ref, l_ref, gain_ref,
         dproj_ref, dl_ref, dgain_ref, res, dstate, b_s) = refs[1 + len(deps):]

        @pl.when(pl.program_id(1) == 0)
        def _():
            dstate[...] = jnp.zeros_like(dstate)
            lb_all = _lower_bound(l_ref)
            tril, triu = _tri(True), _tri(False).astype(F32)
            last = lax.broadcasted_iota(jnp.int32, (CHUNK, HEAD), 0) == CHUNK - 1

            def one_head(hh, j, carry):
                dlb_acc, dgain_acc = carry
                cols = slice(hh * HEAD, (hh + 1) * HEAD)
                lb, gain_v = lb_all[:, cols], gain_ref[:, cols]
                sl = pl.ds(pl.multiple_of(j * CHUNK, CHUNK), CHUNK)
                sg = _sigmoid(f_ref[sl, cols])
                fg = lb + (1.0 - lb) * sg
                kk = 1.0 - fg
                qv = q_ref[sl, cols]
                sq = _sigmoid(qv)
                qs = qv * sq
                vb = i_ref[sl, cols].astype(BF16)
                gv = g_ref[sl, cols]
                sgg = _sigmoid(gv)
                silg = gv * sgg
                b = _fdot(tril.astype(F32), jnp.log(fg))
                b_s[hh] = b
                o = o_ref[sl, cols]
                dyv = dy_ref[sl, cols]
                rr = lax.rsqrt(jnp.mean(o * o, axis=-1, keepdims=True) + EPS)
                on = o * rr
                dgain_acc = dgain_acc + jnp.sum(dyv * on * silg, axis=0, keepdims=True)
                dg = dyv * on * gain_v * _dsilu(gv, sgg)
                don = dyv * gain_v * silg
                do = (rr * don - o * (rr * rr * rr) * jnp.mean(don * o, axis=-1, keepdims=True)).astype(BF16)
                s_in = st_ref[hh, j]
                ds_out = dstate[hh]
                eb = jnp.exp(b)
                bl = b_s[hh, CHUNK - 1:CHUNK, :]
                ebl = jnp.exp(bl)
                ekd = jnp.exp(bl - b)
                dq = _bdot(do, s_in) * eb
                da_m = jnp.where(tril, _bdot(do, vb, NT), 0.0)
                a_rows, dq_rows = [], []
                dk = jnp.zeros((CHUNK, HEAD), F32)
                for i, (eq, qt, e, kt, a) in enumerate(_hgrn_intra(qs, kk, b_s.at[hh])):
                    da_i = _split(da_m[i * SUB:(i + 1) * SUB])
                    a_rows.append(a)
                    dq_rows.append(eq * _dot3(da_i, kt, NN))
                    dk = dk + e * _dot3(da_i, qt, TN)
                dq = dq + jnp.concatenate(dq_rows, axis=0)
                dv = _bdot(jnp.concatenate(a_rows, axis=0), do, TN) + _bdot(kk * ekd, ds_out, NT)
                dk_state = ekd * _bdot(vb, ds_out)
                dk = dk + dk_state
                db = qs * dq - kk * dk
                extra = (jnp.sum(kk * dk_state, axis=0, keepdims=True)
                         + ebl * jnp.sum(s_in * ds_out, axis=0, keepdims=True))
                db = db + jnp.where(last, extra, 0.0)
                dfg = _fdot(triu, db) / fg - dk
                dlb_acc = dlb_acc + jnp.sum(dfg * (1.0 - sg), axis=0, keepdims=True)
                dstate[hh] = ds_out * ebl + _bdot(do, qs * eb, TN)
                res[0, sl, cols] = (dq * _dsilu(qv, sq)).astype(BF16)
                res[1, sl, cols] = (dfg * (1.0 - lb) * sg * (1.0 - sg)).astype(BF16)
                res[2, sl, cols] = dv.astype(BF16)
                res[3, sl, cols] = dg.astype(BF16)
                return dlb_acc, dgain_acc

            def chunk(jj, carry):
                return tuple(one_head(hh, nc - 1 - jj, carry[hh]) for hh in range(hp))

            zero = jnp.zeros((1, HEAD), F32)
            sums = _loop(nc, MIX_UNROLL_BWD, chunk, ((zero, zero),) * hp)
            for hh, (dlb, dgain) in enumerate(sums):
                cols = slice(hh * HEAD, (hh + 1) * HEAD)
                lb = lb_all[:, cols]
                dgain_ref[:, cols] = dgain
                dl0 = dlb * lb * (1.0 - lb)
                dl_ref[0:1, cols] = dl0
                dl_ref[1:2, cols] = -dl0

        dproj_ref[...] = res[pl.program_id(1)]

    ng = n_heads // hp
    col = lambda k: pl.BlockSpec((t, wide), lambda h, p: (0, k * ng + h))
    head = pl.BlockSpec((t, wide), lambda h, p: (0, h))
    vec = pl.BlockSpec((1, wide), lambda h, p: (0, h))
    return pl.pallas_call(
        body, grid=(ng, 4),
        in_specs=[ANY] * (1 + len(deps)) + [col(0), col(1), col(2), col(3), head,
                  pl.BlockSpec((hp, nc, HEAD, HEAD), lambda h, p: (h, 0, 0, 0)),
                  head, pl.BlockSpec((2, wide), lambda h, p: (0, h)), vec],
        out_specs=[pl.BlockSpec((t, wide), lambda h, p: (0, p * ng + h)),
                   pl.BlockSpec((2, wide), lambda h, p: (0, h)), vec],
        out_shape=[SDS(dproj.shape, BF16), SDS((2, da), F32), SDS((1, da), F32)],
        scratch_shapes=[pltpu.VMEM((4, t, wide), BF16), pltpu.VMEM((hp, HEAD, HEAD), F32),
                        pltpu.VMEM((hp, CHUNK, HEAD), F32)],
        input_output_aliases={0: 0}, compiler_params=_params(("arbitrary", "arbitrary")),
        name="hgrn_bwd")(dproj, *deps, proj, proj, proj, proj, o_pre, states, dy, lb_logits, gain)


ROWS = 256


def _head_norm(x_ref, gain, dst, dst_off, t):
    def step(i, _):
        sl = pl.ds(pl.multiple_of(i * ROWS, ROWS), ROWS)
        xv = x_ref[sl, :]
        r = lax.rsqrt(jnp.mean(xv * xv, axis=-1, keepdims=True) + EPS)
        dst[pl.ds(pl.multiple_of(dst_off + i * ROWS, ROWS), ROWS), :] = (xv * r * gain).astype(BF16)
        return 0

    lax.fori_loop(0, t // ROWS, step, 0)


def _head_norm_bwd(x_ref, gain, dn_ref, dn_off, out, slot, t):
    def step(i, acc):
        sl = pl.ds(pl.multiple_of(i * ROWS, ROWS), ROWS)
        xv = x_ref[sl, :]
        dn = dn_ref[pl.ds(pl.multiple_of(dn_off + i * ROWS, ROWS), ROWS), :]
        r = lax.rsqrt(jnp.mean(xv * xv, axis=-1, keepdims=True) + EPS)
        u = dn * gain
        out[slot, sl, :] = r * u - xv * (r * r * r) * jnp.mean(u * xv, axis=-1, keepdims=True)
        return acc + jnp.sum(dn * (xv * r), axis=0, keepdims=True)

    return lax.fori_loop(0, t // ROWS, step, jnp.zeros((1, HEAD), F32))


def _attn_probs(qn, kpad, bias_ref, n):
    qc = qn[pl.ds(pl.multiple_of(n * CHUNK, CHUNK), CHUNK), :]
    band = pl.ds(pl.multiple_of(n * CHUNK, CHUNK), BAND)
    s = lax.dot_general(qc, kpad[band, :], NT, preferred_element_type=F32) * (HEAD ** -0.5) + bias_ref[0]
    col = lax.broadcasted_iota(jnp.int32, (CHUNK, BAND), 1)
    s = jnp.where(col >= PAD - n * CHUNK, s, -jnp.inf)
    p = jnp.exp(s - jnp.max(s, axis=-1, keepdims=True))
    return qc, band, p / jnp.sum(p, axis=-1, keepdims=True)


def _attn_fwd(proj, q_gain, k_gain, bias, n_heads, col0):
    t = proj.shape[0]
    nc = t // CHUNK

    def body(q_ref, k_ref, v_ref, qg_ref, kg_ref, bias_ref, y_ref, qn, kpad, vpad):
        kpad[0:PAD, :] = jnp.zeros((PAD, HEAD), BF16)
        vpad[0:PAD, :] = jnp.zeros((PAD, HEAD), BF16)
        _head_norm(q_ref, qg_ref[...], qn, 0, t)
        _head_norm(k_ref, kg_ref[...], kpad, PAD, t)

        def copy_v(i, _):
            vpad[pl.ds(pl.multiple_of(PAD + i * ROWS, ROWS), ROWS), :] = v_ref[
                pl.ds(pl.multiple_of(i * ROWS, ROWS), ROWS), :].astype(BF16)
            return 0

        lax.fori_loop(0, t // ROWS, copy_v, 0)

        def chunks(i, _):
            outs = []
            for u in range(ATT_UNROLL):
                n = i * ATT_UNROLL + u
                _, band, p = _attn_probs(qn, kpad, bias_ref, n)
                outs.append((n, _bdot(p, vpad[band, :]).astype(BF16)))
            for n, o in outs:
                y_ref[pl.ds(pl.multiple_of(n * CHUNK, CHUNK), CHUNK), :] = o
            return 0

        assert nc % ATT_UNROLL == 0, (nc, ATT_UNROLL)
        lax.fori_loop(0, nc // ATT_UNROLL, chunks, 0)

    col = lambda k: pl.BlockSpec((t, HEAD), lambda h: (0, col0 + k * n_heads + h))
    vec = pl.BlockSpec((1, HEAD), lambda h: (0, 0))
    return pl.pallas_call(
        body, grid=(n_heads,),
        in_specs=[col(0), col(1), col(2), vec, vec, pl.BlockSpec((1, CHUNK, BAND), lambda h: (h, 0, 0))],
        out_specs=pl.BlockSpec((t, HEAD), lambda h: (0, h)), out_shape=SDS((t, n_heads * HEAD), BF16),
        scratch_shapes=[pltpu.VMEM((t, HEAD), BF16), pltpu.VMEM((t + PAD, HEAD), BF16), pltpu.VMEM((t + PAD, HEAD), BF16)],
        compiler_params=_params(("parallel",)), name="attn_fwd")(proj, proj, proj, q_gain, k_gain, bias)


def _attn_bwd(dproj, proj, q_gain, k_gain, bias, dy, n_heads, col0, deps=()):
    t = proj.shape[0]
    nc = t // CHUNK

    def body(*refs):
        (q_ref, k_ref, v_ref, qg_ref, kg_ref, bias_ref, dy_ref,
         dproj_ref, dbias_ref, dqg_ref, dkg_ref, qn, kpad, vpad, dqn, dk_acc, dv_acc, res) = refs[1 + len(deps):]
        h, part = pl.program_id(0), pl.program_id(1)

        @pl.when(part == 0)
        def _():
            kpad[0:PAD, :] = jnp.zeros((PAD, HEAD), BF16)
            vpad[0:PAD, :] = jnp.zeros((PAD, HEAD), BF16)
            _head_norm(q_ref, qg_ref[...], qn, 0, t)
            _head_norm(k_ref, kg_ref[...], kpad, PAD, t)

            def prep(i, _):
                sl = pl.ds(pl.multiple_of(PAD + i * ROWS, ROWS), ROWS)
                vpad[sl, :] = v_ref[pl.ds(pl.multiple_of(i * ROWS, ROWS), ROWS), :].astype(BF16)
                return 0

            lax.fori_loop(0, t // ROWS, prep, 0)

            def clear(i, _):
                sl = pl.ds(pl.multiple_of(i * ROWS, ROWS), ROWS)
                dk_acc[sl, :] = jnp.zeros((ROWS, HEAD), F32)
                dv_acc[sl, :] = jnp.zeros((ROWS, HEAD), F32)
                return 0

            lax.fori_loop(0, (t + PAD) // ROWS, clear, 0)
            dbias_ref[0] = jnp.zeros((CHUNK, BAND), F32)

            def chunk(n, _):
                qc, band, p = _attn_probs(qn, kpad, bias_ref, n)
                do = dy_ref[pl.ds(pl.multiple_of(n * CHUNK, CHUNK), CHUNK), :].astype(BF16)
                dp = lax.dot_general(do, vpad[band, :], NT, preferred_element_type=F32)
                ds = p * (dp - jnp.sum(dp * p, axis=-1, keepdims=True))
                dbias_ref[0] += ds
                dss = (ds * (HEAD ** -0.5)).astype(BF16)
                dqn[pl.ds(pl.multiple_of(n * CHUNK, CHUNK), CHUNK), :] = lax.dot_general(
                    dss, kpad[band, :], NN, preferred_element_type=F32)
                dk_acc[band, :] += lax.dot_general(dss, qc, TN, preferred_element_type=F32)
                dv_acc[band, :] += lax.dot_general(p.astype(BF16), do, TN, preferred_element_type=F32)
                return 0

            _loop(nc, ATT_UNROLL_BWD, chunk, 0)
            dqg = _head_norm_bwd(q_ref, qg_ref[...], dqn, 0, res, 0, t)
            dkg = _head_norm_bwd(k_ref, kg_ref[...], dk_acc, PAD, res, 1, t)

            def put_v(i, _):
                sl = pl.ds(pl.multiple_of(i * ROWS, ROWS), ROWS)
                res[2, sl, :] = dv_acc[pl.ds(pl.multiple_of(PAD + i * ROWS, ROWS), ROWS), :]
                return 0

            lax.fori_loop(0, t // ROWS, put_v, 0)

            @pl.when(h == 0)
            def _():
                dqg_ref[...] = jnp.zeros_like(dqg_ref)
                dkg_ref[...] = jnp.zeros_like(dkg_ref)

            dqg_ref[...] += dqg
            dkg_ref[...] += dkg

        dproj_ref[...] = res[part].astype(BF16)

    col = lambda k: pl.BlockSpec((t, HEAD), lambda h, p: (0, col0 + k * n_heads + h))
    vec = pl.BlockSpec((1, HEAD), lambda h, p: (0, 0))
    btile = pl.BlockSpec((1, CHUNK, BAND), lambda h, p: (h, 0, 0))
    return pl.pallas_call(
        body, grid=(n_heads, 3),
        in_specs=[ANY] * (1 + len(deps)) + [col(0), col(1), col(2), vec, vec, btile,
                                            pl.BlockSpec((t, HEAD), lambda h, p: (0, h))],
        out_specs=[pl.BlockSpec((t, HEAD), lambda h, p: (0, col0 + p * n_heads + h)), btile, vec, vec],
        out_shape=[SDS(dproj.shape, BF16), SDS((n_heads, CHUNK, BAND), F32), SDS((1, HEAD), F32), SDS((1, HEAD), F32)],
        scratch_shapes=[pltpu.VMEM((t, HEAD), BF16), pltpu.VMEM((t + PAD, HEAD), BF16), pltpu.VMEM((t + PAD, HEAD), BF16),
                        pltpu.VMEM((t, HEAD), F32), pltpu.VMEM((t + PAD, HEAD), F32), pltpu.VMEM((t + PAD, HEAD), F32),
                        pltpu.VMEM((3, t, HEAD), F32)],
        input_output_aliases={0: 0}, compiler_params=_params(("arbitrary", "arbitrary")),
        name="attn_bwd")(dproj, *deps, proj, proj, proj, q_gain, k_gain, bias, dy)


def _place():
    x, y, c = lax.axis_index("x"), lax.axis_index("y"), lax.axis_index("c")
    others = [(1 - x, y), (x, 1 - y), (1 - x, 1 - y)]
    return x, y, c, others


def _chunk_of(ref, kind, chip, half, shard_shape):
    r, n = shard_shape
    hr = r // 2
    if kind == "col":
        rows = pl.ds(0, r) if half is None else pl.ds(half * hr, hr)
        return ref.at[rows, pl.ds(chip * n, n)]
    rows = pl.ds(chip * r, r) if half is None else pl.ds(chip * r + half * hr, hr)
    return ref.at[rows, :]


EFFECT = pltpu.SideEffectType.DATAFLOW_SIDE_EFFECTING


def _start_copies(name, bufs, plan, n, deps):
    nb, nd = len(bufs), len(deps)

    def body(*refs):
        send, recv, token = refs[nb + nd], refs[nb + nd + 1], refs[-1]
        for cp in plan(refs[:nb], send, recv)[0]:
            cp.start()
        token[...] = jnp.zeros_like(token)

    out = pl.pallas_call(
        body, name=name,
        out_shape=(pltpu.SemaphoreType.DMA((n,)), pltpu.SemaphoreType.DMA((n,)),
                   *[pltpu.HBM(b.shape, b.dtype) for b in bufs], SDS((8, 128), F32)),
        in_specs=[HBM] * nb + [ANY] * nd,
        out_specs=(SEM, SEM, *[HBM] * nb, pl.BlockSpec(memory_space=pltpu.VMEM)),
        input_output_aliases={i: 2 + i for i in range(nb)},
        compiler_params=pltpu.CompilerParams(has_side_effects=EFFECT),
    )(*[pltpu.with_memory_space_constraint(b, pltpu.HBM) for b in bufs], *deps)
    return out[0], out[1], list(out[2:2 + nb]), out[-1]


def _wait_copies(name, bufs, send, recv, plan, after):
    nb = len(bufs)

    def body(*refs):
        sends, recvs = plan(refs[:nb], refs[nb], refs[nb + 1])
        for cp in sends:
            cp.wait_send()
        for cp in recvs:
            cp.wait_recv()

    out = pl.pallas_call(
        body, name=name, out_shape=tuple(pltpu.HBM(b.shape, b.dtype) for b in bufs),
        in_specs=[HBM] * nb + [SEM, SEM] + [ANY] * len(after), out_specs=tuple([HBM] * nb),
        input_output_aliases={i: i for i in range(nb)},
        compiler_params=pltpu.CompilerParams(has_side_effects=EFFECT),
    )(*bufs, send, recv, *after)
    return list(out)


def _remote(src, dst, send, recv, i, dev):
    return pltpu.make_async_remote_copy(src_ref=src, dst_ref=dst, send_sem=send.at[i], recv_sem=recv.at[i],
                                        device_id=dev, device_id_type=MESH)


def _plan_gather_ici(kinds, shapes):
    def plan(refs, send, recv):
        x, y, c, others = _place()
        sends, recvs = [], []
        for w, (kind, ss) in enumerate(zip(kinds, shapes)):
            for p, (px, py) in enumerate(others):
                mine = _chunk_of(refs[w], kind, 2 * x + y, c, ss)
                theirs = _chunk_of(refs[w], kind, 2 * px + py, c, ss)
                sends.append(_remote(mine, mine, send, recv, 3 * w + p, (px, py, c)))
                recvs.append(_remote(theirs, theirs, send, recv, 3 * w + p, (px, py, c)))
        return sends, recvs

    return plan, 3 * len(kinds)


def _plan_gather_pass(kinds, shapes):
    def plan(refs, send, recv):
        x, y, c, others = _place()
        sends, recvs = [], []
        for w, (kind, ss) in enumerate(zip(kinds, shapes)):
            for p, (px, py) in enumerate(others):
                got = _chunk_of(refs[w], kind, 2 * px + py, c, ss)
                coming = _chunk_of(refs[w], kind, 2 * px + py, 1 - c, ss)
                sends.append(_remote(got, got, send, recv, 3 * w + p, (x, y, 1 - c)))
                recvs.append(_remote(coming, coming, send, recv, 3 * w + p, (x, y, 1 - c)))
        return sends, recvs

    return plan, 3 * len(kinds)


def _plan_pair(kinds, shapes):
    nw = len(kinds)

    def plan(refs, send, recv):
        x, y, c, _ = _place()
        sends = []
        for w, (kind, ss) in enumerate(zip(kinds, shapes)):
            for k in range(4):
                sends.append(_remote(_chunk_of(refs[w], kind, k, 1 - c, ss), refs[nw + w].at[k], send, recv,
                                     4 * w + k, (x, y, 1 - c)))
        return sends, sends

    return plan, 4 * nw


def _plan_chip(nw):
    def plan(refs, send, recv):
        x, y, c, others = _place()
        sends = []
        for w in range(nw):
            for p, (px, py) in enumerate(others):
                sends.append(_remote(refs[w].at[2 * px + py], refs[nw + w].at[p], send, recv, 3 * w + p, (px, py, c)))
        return sends, sends

    return plan, 3 * nw


def _plan_share(nw):
    def plan(refs, send, recv):
        x, y, c, _ = _place()
        sends = [_remote(refs[w].at[c], refs[w].at[c], send, recv, w, (x, y, 1 - c)) for w in range(nw)]
        recvs = [_remote(refs[w].at[1 - c], refs[w].at[1 - c], send, recv, w, (x, y, 1 - c)) for w in range(nw)]
        return sends, recvs

    return plan, nw


def _pair_add(grad, got, kind, shard_shape, pos, name):
    r, n = shard_shape
    hr = r // 2
    tr, tn = _tile(hr, 256, 16), _tile(n, 1408)
    nr, nn = hr // tr, n // tn
    if kind == "col":
        g_spec = pl.BlockSpec((tr, tn), lambda k, i, j, c: (c[1] * nr + i, k * nn + j))
    else:
        g_spec = pl.BlockSpec((tr, tn), lambda k, i, j, c: ((2 * k + c[1]) * nr + i, j))
    o_spec = pl.BlockSpec((1, tr, tn), lambda k, i, j, c: (k, i, j))

    def body(c_ref, g_ref, r_ref, o32_ref, o16_ref):
        s = g_ref[...] + r_ref[0]
        o32_ref[0] = s
        o16_ref[0] = s.astype(BF16)

    return pl.pallas_call(
        body,
        grid_spec=pltpu.PrefetchScalarGridSpec(num_scalar_prefetch=1, grid=(4, nr, nn), in_specs=[g_spec, o_spec],
                                               out_specs=[o_spec, o_spec]),
        out_shape=[SDS((4, hr, n), F32), SDS((4, hr, n), BF16)],
        compiler_params=_params(("parallel", "parallel", "parallel")), name=name)(pos, grad, got)


def _chip_add(part32, got16, pos, name):
    _, hr, n = part32.shape
    tr, tn = _tile(hr, 256, 16), _tile(n, 1408)
    own = pl.BlockSpec((1, tr, tn), lambda i, j, p: (p[0], i, j))
    oth = pl.BlockSpec((3, tr, tn), lambda i, j, p: (0, i, j))

    def body(p_ref, own_ref, oth_ref, o_ref):
        o_ref[0] = ((own_ref[0] + oth_ref[0].astype(F32)) + oth_ref[1].astype(F32)) + oth_ref[2].astype(F32)

    return pl.pallas_call(
        body,
        grid_spec=pltpu.PrefetchScalarGridSpec(num_scalar_prefetch=1, grid=(hr // tr, n // tn), in_specs=[own, oth],
                                               out_specs=pl.BlockSpec((1, tr, tn), lambda i, j, p: (p[1], i, j))),
        out_shape=SDS((2, hr, n), F32), compiler_params=_params(("parallel", "parallel")), name=name)(pos, part32, got16)


def _adamw_math(w, g, m, v):
    m = ADAM_B1 * m + (1.0 - ADAM_B1) * g
    v = ADAM_B2 * v + (1.0 - ADAM_B2) * (g * g)
    m_hat = m / (1.0 - ADAM_B1 ** ADAM_STEP)
    v_hat = v / (1.0 - ADAM_B2 ** ADAM_STEP)
    return -ADAM_LR * (m_hat / (jnp.sqrt(v_hat) + ADAM_EPS) + ADAM_WD * w), m, v


def _adamw(w, g, m, v, name):
    r, n = w.shape
    tr, tn = _tile(r, 256, 16), _tile(n, 1408)

    def body(w_ref, g_ref, m_ref, v_ref, d_ref, nm_ref, nv_ref):
        d_ref[...], nm_ref[...], nv_ref[...] = _adamw_math(w_ref[...], g_ref[...], m_ref[...], v_ref[...])

    tile = pl.BlockSpec((tr, tn), lambda i, j: (i, j))
    return pl.pallas_call(
        body, grid=(r // tr, n // tn), in_specs=[tile] * 4, out_specs=[tile] * 3, out_shape=[SDS((r, n), F32)] * 3,
        compiler_params=_params(("parallel", "parallel")), name=name)(w, g, m, v)


def _small_allreduce_adamw(g, w, m, v, deps=()):
    length = g.shape[1]

    def body(*refs):
        g_ref, w_ref, m_ref, v_ref = refs[:4]
        gs_ref, d_ref, nm_ref, nv_ref, buf, send, recv = refs[4 + len(deps):]
        x, y, c = lax.axis_index("x"), lax.axis_index("y"), lax.axis_index("c")
        me = 4 * x + 2 * y + c
        buf[me] = g_ref[...]
        cps = []
        for d in range(1, 8):
            peer = (x ^ (d >> 2), y ^ ((d >> 1) & 1), c ^ (d & 1))
            cp = pltpu.make_async_remote_copy(src_ref=buf.at[me], dst_ref=buf.at[me], send_sem=send.at[d - 1],
                                              recv_sem=recv.at[d - 1], device_id=peer, device_id_type=MESH)
            cp.start()
            cps.append(cp)
        for cp in cps:
            cp.wait()
        total = buf[0]
        for d in range(1, 8):
            total = total + buf[d]
        gs_ref[...] = total
        d_ref[...], nm_ref[...], nv_ref[...] = _adamw_math(w_ref[...], total, m_ref[...], v_ref[...])

    vm = pl.BlockSpec(memory_space=pltpu.VMEM)
    return pl.pallas_call(
        body, in_specs=[vm] * 4 + [ANY] * len(deps), out_specs=[vm] * 4, out_shape=[SDS((1, length), F32)] * 4,
        scratch_shapes=[pltpu.VMEM((8, 1, length), F32), pltpu.SemaphoreType.DMA((7,)), pltpu.SemaphoreType.DMA((7,))],
        compiler_params=pltpu.CompilerParams(has_side_effects=True), name="small_allreduce_adamw")(g, w, m, v, *deps)


def kernel(x, w_in, b_gate, norm_mix, norm_ffn, hgrn_lb_logits, hgrn_out_gain, q_gain, k_gain, rel_bias, w_proj_a, w_proj_b, w_out, w_ffn_in, w_ffn_out, loss_target, m_w_in, m_b_gate, m_norm_mix, m_norm_ffn, m_hgrn_lb_logits, m_hgrn_out_gain, m_q_gain, m_k_gain, m_rel_bias, m_w_proj_a, m_w_proj_b, m_w_out, m_w_ffn_in, m_w_ffn_out, v_w_in, v_b_gate, v_norm_mix, v_norm_ffn, v_hgrn_lb_logits, v_hgrn_out_gain, v_q_gain, v_k_gain, v_rel_bias, v_w_proj_a, v_w_proj_b, v_w_out, v_w_ffn_in, v_w_ffn_out):
    t, d = x.shape[1], x.shape[2]
    d_a = hgrn_out_gain.shape[1]
    h_a = d_a // HEAD
    h_b = rel_bias.shape[1]
    d_b = h_b * HEAD
    x0 = x.reshape(t, d)
    target = loss_target.reshape(t, d)
    pos = jnp.stack([2 * lax.axis_index("x") + lax.axis_index("y"), lax.axis_index("c")]).astype(jnp.int32)

    names = ["w_in", "w_proj_a", "w_proj_b", "w_out", "w_ffn_in", "w_ffn_out"]
    big = dict(zip(names, [w_in[0], w_proj_a[0], w_proj_b[0], w_out[0], w_ffn_in[0], w_ffn_out[0]]))
    big_m = dict(zip(names, [m_w_in[0], m_w_proj_a[0], m_w_proj_b[0], m_w_out[0], m_w_ffn_in[0], m_w_ffn_out[0]]))
    big_v = dict(zip(names, [v_w_in[0], v_w_proj_a[0], v_w_proj_b[0], v_w_out[0], v_w_ffn_in[0], v_w_ffn_out[0]]))
    kind = dict(zip(names, ["col", "col", "col", "row", "col", "row"]))
    shape = {nm: big[nm].shape for nm in names}

    def gather_start(tag, group, deps):
        plan, n = _plan_gather_ici([kind[g] for g in group], [shape[g] for g in group])
        fulls = [_cast_into_full(big[g], kind[g], pos, "cast_" + g) for g in group]
        send, recv, bufs, token = _start_copies("gather_ici_start_" + tag, fulls, plan, n, deps)
        return (tag, group, plan, send, recv, bufs), token

    def gather_pass(state, after):
        tag, group, plan, send, recv, bufs = state
        bufs = _wait_copies("gather_ici_wait_" + tag, bufs, send, recv, plan, after)
        plan, n = _plan_gather_pass([kind[g] for g in group], [shape[g] for g in group])
        send, recv, bufs, token = _start_copies("gather_pass_start_" + tag, bufs, plan, n, ())
        return (tag, group, plan, send, recv, bufs), token

    def gather_done(state, after):
        tag, group, plan, send, recv, bufs = state
        return _wait_copies("gather_pass_wait_" + tag, bufs, send, recv, plan, after)

    def reduce_start(tag, group, grads, deps):
        plan, n = _plan_pair([kind[g] for g in group], [shape[g] for g in group])
        lands = [lax.empty((4, shape[g][0] // 2, shape[g][1]), F32) for g in group]
        send, recv, bufs, token = _start_copies("pair_start_" + tag, list(grads) + lands, plan, n, deps)
        return dict(tag=tag, group=group, plan=plan, send=send, recv=recv, bufs=bufs), token

    def reduce_pair_done(st, after):
        tag, group, nw = st["tag"], st["group"], len(st["group"])
        bufs = _wait_copies("pair_wait_" + tag, st["bufs"], st["send"], st["recv"], st["plan"], after)
        parts = [_pair_add(g, l, kind[nm], shape[nm], pos, "pair_add_" + nm)
                 for g, l, nm in zip(bufs[:nw], bufs[nw:], group)]
        lands = [lax.empty((3, shape[g][0] // 2, shape[g][1]), BF16) for g in group]
        plan, n = _plan_chip(nw)
        send, recv, bufs, token = _start_copies("chip_start_" + tag, [p16 for _, p16 in parts] + lands, plan, n, ())
        return dict(st, plan=plan, send=send, recv=recv, bufs=bufs, p32=[p32 for p32, _ in parts]), token

    def reduce_chip_done(st, after):
        tag, group, nw = st["tag"], st["group"], len(st["group"])
        bufs = _wait_copies("chip_wait_" + tag, st["bufs"], st["send"], st["recv"], st["plan"], after)
        finals = [_chip_add(p32, got, pos, "chip_add_" + nm) for p32, got, nm in zip(st["p32"], bufs[nw:], group)]
        plan, n = _plan_share(nw)
        send, recv, bufs, token = _start_copies("share_start_" + tag, finals, plan, n, ())
        return dict(st, plan=plan, send=send, recv=recv, bufs=bufs), token

    g_big, upd = {}, {}

    def reduce_finish(st, after):
        bufs = _wait_copies("share_wait_" + st["tag"], st["bufs"], st["send"], st["recv"], st["plan"], after)
        for full, nm in zip(bufs, st["group"]):
            g_big[nm] = full.reshape(shape[nm])
            upd[nm] = _adamw(big[nm], g_big[nm], big_m[nm], big_v[nm], "adamw_" + nm)

    ga, token = gather_start("a", ["w_in"], ())
    gb, token = gather_start("b", ["w_proj_a", "w_proj_b", "w_out"], (token,))
    gc, token = gather_start("c", ["w_ffn_in", "w_ffn_out"], (token,))
    h1, r1 = _rmsnorm_fwd(x0, norm_mix, "rmsnorm_mix")
    ga, token = gather_pass(ga, (h1, token))
    (wg_in,) = gather_done(ga, ())
    proj = _matmul(h1, wg_in, name="proj_in")
    y_a, o_pre, states = _hgrn_fwd(proj, hgrn_lb_logits, hgrn_out_gain, h_a)
    gb, token = gather_pass(gb, (y_a,))
    rb = jnp.pad(rel_bias[0], ((0, 0), (0, REL_LANES - N_REL)))
    bias = _relbias_expand(rb).transpose(1, 0, 2)
    col_b = 4 * d_a // HEAD
    y_b = _attn_fwd(proj, q_gain, k_gain, bias, h_b, col_b)
    wg_pa, wg_pb, wg_out = gather_done(gb, (y_b,))
    gc, token = gather_pass(gc, (y_b,))
    pa = _matmul(y_a, wg_pa, name="proj_a", deps=(token,))
    pb = _matmul(y_b, wg_pb, name="proj_b")
    gate_off = 4 * d_a + 3 * d_b
    merged = _merge_fwd(proj, b_gate, pa, pb, gate_off)
    x2 = _matmul(merged, wg_out, res=x0, name="out_proj")
    h2, r2 = _rmsnorm_fwd(x2, norm_ffn, "rmsnorm_ffn")
    wg_fin, wg_fout = gather_done(gc, (h2,))
    gu = _matmul(h2, wg_fin, name="ffn_in")
    act = _swiglu_fwd(gu)
    y = _matmul(act, wg_fout, res=x2, name="ffn_out")
    dy, dy16, loss_part = _loss_head(y, target)

    g_fout = _matmul(act, dy16, ta=True, name="dw_ffn_out")
    r_fout, token = reduce_start("fout", ["w_ffn_out"], [g_fout], ())
    dact = _matmul(dy16, wg_fout, tb=True, name="d_act", deps=(token,))
    r_fout, token = reduce_pair_done(r_fout, (dact,))
    dgu = _swiglu_bwd(dact, gu)
    g_fin = _matmul(h2, dgu, ta=True, name="dw_ffn_in", deps=(token,))
    r_fin, token = reduce_start("fin", ["w_ffn_in"], [g_fin], ())
    dh2 = _matmul(dgu, wg_fin, tb=True, name="d_h2", deps=(token,))
    r_fout, token_a = reduce_chip_done(r_fout, (dh2,))
    r_fin, token_b = reduce_pair_done(r_fin, (dh2,))
    dx2, dx2_16, g_norm_ffn = _rmsnorm_bwd(dh2, x2, r2, norm_ffn, dy, "rmsnorm_ffn_bwd")
    dmerged = _matmul(dx2_16, wg_out, tb=True, name="d_merged", deps=(token_a, token_b))
    dp_ab, dproj, g_bgate = _merge_bwd(dmerged, proj, b_gate, pa, pb, gate_off)
    g_out = _matmul(merged, dx2_16, ta=True, name="dw_out")
    g_pa = _matmul(y_a, dp_ab[0], ta=True, name="dw_proj_a")
    g_pb = _matmul(y_b, dp_ab[1], ta=True, name="dw_proj_b")
    r_mid, token = reduce_start("mid", ["w_proj_a", "w_proj_b", "w_out"], [g_pa, g_pb, g_out], ())
    dy_a = _matmul(dp_ab[0], wg_pa, tb=True, name="d_y_a", deps=(token,))
    dy_b = _matmul(dp_ab[1], wg_pb, tb=True, name="d_y_b")
    reduce_finish(r_fout, (dy_b,))
    r_fin, token_a = reduce_chip_done(r_fin, (dy_b,))
    r_mid, token_b = reduce_pair_done(r_mid, (dy_b,))
    dproj, dbias, g_qg, g_kg = _attn_bwd(dproj, proj, q_gain, k_gain, bias, dy_b, h_b, col_b, deps=(token_a, token_b))
    r_mid, token = reduce_chip_done(r_mid, (dbias,))
    dproj, g_lb, g_gain = _hgrn_bwd(dproj, proj, o_pre, states, dy_a, hgrn_lb_logits, hgrn_out_gain, h_a, deps=(token,))
    g_in = _matmul(h1, dproj, ta=True, name="dw_in")
    r_in, token = reduce_start("in", ["w_in"], [g_in], ())
    g_rb = _relbias_reduce(dbias.transpose(1, 0, 2))[:, :N_REL]
    reduce_finish(r_mid, (token,))
    r_in, token = reduce_pair_done(r_in, (g_rb, upd["w_out"][0], upd["w_proj_a"][0], upd["w_proj_b"][0]))
    dh1 = _matmul(dproj, wg_in, tb=True, name="d_h1", deps=(token,))
    dx, _, g_norm_mix = _rmsnorm_bwd(dh1, x0, r1, norm_mix, dx2, "rmsnorm_mix_bwd")
    reduce_finish(r_fin, (dx,))
    r_in, token = reduce_chip_done(r_in, (upd["w_ffn_in"][0], upd["w_ffn_out"][0]))

    small_w = [b_gate, norm_mix, norm_ffn, hgrn_lb_logits, hgrn_out_gain, q_gain, k_gain, rel_bias]
    small_m = [m_b_gate, m_norm_mix, m_norm_ffn, m_hgrn_lb_logits, m_hgrn_out_gain, m_q_gain, m_k_gain, m_rel_bias]
    small_v = [v_b_gate, v_norm_mix, v_norm_ffn, v_hgrn_lb_logits, v_hgrn_out_gain, v_q_gain, v_k_gain, v_rel_bias]
    small_g = [g_bgate, g_norm_mix, g_norm_ffn, g_lb, g_gain, g_qg, g_kg, g_rb]
    sizes = [w.size for w in small_w]
    length = -(-(sum(sizes) + 1) // 128) * 128

    def pack(parts_):
        flat = jnp.concatenate([p.reshape(1, -1) for p in parts_], axis=1)
        return jnp.pad(flat, ((0, 0), (0, length - flat.shape[1])))

    one = jnp.ones((1, 1), F32)
    packed = _small_allreduce_adamw(pack(small_g + [loss_part]), pack(small_w + [one]), pack(small_m + [one]),
                                    pack(small_v + [one]), deps=(token,))

    def unpack(vec):
        out, at = [], 0
        for w, n in zip(small_w, sizes):
            out.append(vec[0, at:at + n].reshape(w.shape))
            at += n
        return out, vec[0, at]

    (sg, loss), (sd, _), (sm, _), (sv, _) = [unpack(p) for p in packed]
    reduce_finish(r_in, (packed[0],))

    def ordered(small, bigs):
        bigs = [bigs[nm][None] for nm in names]
        return [bigs[0]] + small + bigs[1:]

    return (loss, dx.reshape(x.shape), *ordered(sg, g_big), *ordered(sd, {nm: upd[nm][0] for nm in names}),
            *ordered(sm, {nm: upd[nm][1] for nm in names}), *ordered(sv, {nm: upd[nm][2] for nm in names}))
```

```python
import functools

import jax
import jax.numpy as jnp
from jax import lax
from jax.experimental import pallas as pl
from jax.experimental.pallas import tpu as pltpu

F32 = jnp.float32
BF16 = jnp.bfloat16
SDS = jax.ShapeDtypeStruct
MESH = pl.DeviceIdType.MESH
HIGHEST = lax.Precision.HIGHEST

CHUNK = 64
SUB = 16
HEAD = 128
N_PAST = 8
BAND = (N_PAST + 1) * CHUNK
PAD = N_PAST * CHUNK
REL_FUTURE = CHUNK - 1
REL_PAST = 2 * CHUNK - 1
N_REL = REL_FUTURE + REL_PAST + 1
REL_LANES = 256
EPS = 1e-6
MIX_HEADS = 2
MIX_UNROLL = 4
MIX_UNROLL_BWD = 4
ATT_UNROLL = 8
ATT_UNROLL_BWD = 4
EXP_CLAMP = 80.0

ADAM_LR = 0.001
ADAM_B1 = 0.9
ADAM_B2 = 0.999
ADAM_EPS = 1e-08
ADAM_WD = 0.01
ADAM_STEP = 10

VMEM_LIMIT = 56 * 1024 * 1024

HBM = pl.BlockSpec(memory_space=pltpu.HBM)
ANY = pl.BlockSpec(memory_space=pl.ANY)
SEM = pl.BlockSpec(memory_space=pltpu.SEMAPHORE)

NT = (((1,), (1,)), ((), ()))
TN = (((0,), (0,)), ((), ()))
NN = (((1,), (0,)), ((), ()))


def _params(sem=None, **kw):
    return pltpu.CompilerParams(dimension_semantics=sem, vmem_limit_bytes=VMEM_LIMIT, **kw)


def _tile(n, pref, unit=128):
    if n <= pref:
        return n
    t = pref - pref % unit
    while n % t:
        t -= unit
    return t


def _loop(n, unroll, step, init):
    assert n % unroll == 0, (n, unroll)

    def several(i, carry):
        for u in range(unroll):
            carry = step(i * unroll + u, carry)
        return carry

    return lax.fori_loop(0, n // unroll, several, init)


def _sigmoid(x):
    return 1.0 / (1.0 + jnp.exp(-x))


def _dsilu(x, s):
    return s * (1.0 + x * (1.0 - s))


def _bdot(a, b, dims=NN):
    return lax.dot_general(a.astype(BF16), b.astype(BF16), dims, preferred_element_type=F32)


def _split(a):
    hi = a.astype(BF16)
    return hi, (a - hi.astype(F32)).astype(BF16)


def _dot3(a, b, dims):
    dot = lambda u, v: lax.dot_general(u, v, dims, preferred_element_type=F32)
    return dot(a[0], b[1]) + dot(a[1], b[0]) + dot(a[0], b[0])


def _fdot(a, b):
    return lax.dot_general(a, b, NN, precision=HIGHEST, preferred_element_type=F32)


MM_TILE_K = 5632
MM_TILE_N = 512


def _matmul(a, b, *, ta=False, tb=False, res=None, out_dtype=F32, name, deps=()):
    m, k = (a.shape[1], a.shape[0]) if ta else a.shape
    n = b.shape[0] if tb else b.shape[1]
    tk = _tile(k, MM_TILE_K)
    nk = k // tk
    tm, tn = _tile(m, 2048 if tk <= MM_TILE_K // 2 else 1024), _tile(n, MM_TILE_N)
    dims = ((((0,) if ta else (1,)), ((1,) if tb else (0,))), ((), ()))

    def body(*refs):
        n_in = 2 + (res is not None)
        a_ref, b_ref = refs[:2]
        r_ref = refs[2] if res is not None else None
        o_ref = refs[n_in + len(deps)]
        part = lax.dot_general(a_ref[...].astype(BF16), b_ref[...].astype(BF16), dims, preferred_element_type=F32)

        def finish(out):
            if r_ref is not None:
                out = out + r_ref[...]
            o_ref[...] = out.astype(o_ref.dtype)

        if nk == 1:
            finish(part)
            return
        acc_ref = refs[-1]
        kk = pl.program_id(2)

        @pl.when(kk == 0)
        def _():
            acc_ref[...] = part

        @pl.when(jnp.logical_and(kk > 0, kk < nk - 1))
        def _():
            acc_ref[...] += part

        @pl.when(kk == nk - 1)
        def _():
            finish(acc_ref[...] + part)

    a_spec = pl.BlockSpec((tk, tm), lambda i, j, l: (l, i)) if ta else pl.BlockSpec((tm, tk), lambda i, j, l: (i, l))
    b_spec = pl.BlockSpec((tn, tk), lambda i, j, l: (j, l)) if tb else pl.BlockSpec((tk, tn), lambda i, j, l: (l, j))
    o_spec = pl.BlockSpec((tm, tn), lambda i, j, l: (i, j))
    in_specs = [a_spec, b_spec] + ([o_spec] if res is not None else []) + [ANY] * len(deps)
    args = (a, b) + ((res,) if res is not None else ()) + tuple(deps)
    return pl.pallas_call(
        body, grid=(m // tm, n // tn, nk), in_specs=in_specs, out_specs=o_spec,
        out_shape=SDS((m, n), out_dtype), scratch_shapes=[pltpu.VMEM((tm, tn), F32)] if nk > 1 else [],
        compiler_params=_params(("parallel", "parallel", "arbitrary")), name=name)(*args)


def _cast_into_full(w, kind, pos, name):
    r, n = w.shape
    tr = _tile(r, 512, 16)
    nr = r // tr
    if kind == "col":
        shape, o_spec = (r, 4 * n), pl.BlockSpec((tr, n), lambda i, p: (i, p[0]))
    else:
        shape, o_spec = (4 * r, n), pl.BlockSpec((tr, n), lambda i, p: (p[0] * nr + i, 0))

    def body(p_ref, w_ref, o_ref):
        o_ref[...] = w_ref[...].astype(BF16)

    return pl.pallas_call(
        body,
        grid_spec=pltpu.PrefetchScalarGridSpec(num_scalar_prefetch=1, grid=(nr,),
                                               in_specs=[pl.BlockSpec((tr, n), lambda i, p: (i, 0))], out_specs=o_spec),
        out_shape=SDS(shape, BF16), compiler_params=_params(("parallel",)), name=name)(pos, w)


def _rmsnorm_fwd(x, gain, name):
    t, d = x.shape
    tm = _tile(t, 256)

    def body(x_ref, g_ref, h_ref, r_ref):
        xv = x_ref[...]
        r = lax.rsqrt(jnp.mean(xv * xv, axis=-1, keepdims=True) + EPS)
        h_ref[...] = (xv * r * g_ref[...]).astype(BF16)
        r_ref[...] = r

    return pl.pallas_call(
        body, grid=(t // tm,),
        in_specs=[pl.BlockSpec((tm, d), lambda i: (i, 0)), pl.BlockSpec((1, d), lambda i: (0, 0))],
        out_specs=[pl.BlockSpec((tm, d), lambda i: (i, 0)), pl.BlockSpec((tm, 1), lambda i: (i, 0))],
        out_shape=[SDS((t, d), BF16), SDS((t, 1), F32)], compiler_params=_params(("parallel",)), name=name)(x, gain)


def _rmsnorm_bwd(dh, x, r, gain, dres, name):
    t, d = x.shape
    tm = _tile(t, 256)

    def body(dh_ref, x_ref, r_ref, g_ref, dres_ref, dx_ref, dxb_ref, dg_ref):
        @pl.when(pl.program_id(0) == 0)
        def _():
            dg_ref[...] = jnp.zeros_like(dg_ref)

        dhv, xv, rv = dh_ref[...], x_ref[...], r_ref[...]
        dg_ref[...] += jnp.sum(dhv * (xv * rv), axis=0, keepdims=True)
        u = dhv * g_ref[...]
        dx = dres_ref[...] + rv * u - xv * (rv * rv * rv) * jnp.mean(u * xv, axis=-1, keepdims=True)
        dx_ref[...] = dx
        dxb_ref[...] = dx.astype(BF16)

    row = pl.BlockSpec((tm, d), lambda i: (i, 0))
    vec = pl.BlockSpec((1, d), lambda i: (0, 0))
    return pl.pallas_call(
        body, grid=(t // tm,), in_specs=[row, row, pl.BlockSpec((tm, 1), lambda i: (i, 0)), vec, row],
        out_specs=[row, row, vec], out_shape=[SDS((t, d), F32), SDS((t, d), BF16), SDS((1, d), F32)],
        compiler_params=_params(("arbitrary",)), name=name)(dh, x, r, gain, dres)


def _merge_fwd(proj, b_gate, pa, pb, off):
    t, d = pa.shape
    tm, tc = _tile(t, 512), _tile(d, 512)
    nj = d // tc
    oa, ob = off // tc, off // tc + nj

    def body(la_ref, lb_ref, ba_ref, bb_ref, pa_ref, pb_ref, o_ref):
        ga = _sigmoid(la_ref[...] + ba_ref[...])
        gb = _sigmoid(lb_ref[...] + bb_ref[...])
        o_ref[...] = (ga * pa_ref[...] + gb * pb_ref[...]).astype(BF16)

    tile = pl.BlockSpec((tm, tc), lambda i, j: (i, j))
    return pl.pallas_call(
        body, grid=(t // tm, nj),
        in_specs=[pl.BlockSpec((tm, tc), lambda i, j: (i, oa + j)), pl.BlockSpec((tm, tc), lambda i, j: (i, ob + j)),
                  pl.BlockSpec((1, tc), lambda i, j: (0, j)), pl.BlockSpec((1, tc), lambda i, j: (0, nj + j)), tile, tile],
        out_specs=tile, out_shape=SDS((t, d), BF16), compiler_params=_params(("parallel", "parallel")),
        name="merge_fwd")(proj, proj, b_gate, b_gate, pa, pb)


def _merge_bwd(dmerged, proj, b_gate, pa, pb, off):
    t, d = pa.shape
    tm, tc = _tile(t, 512), _tile(d, 512)
    nj, ni = d // tc, t // tm
    o0 = off // tc

    def body(dm_ref, l_ref, b_ref, pa_ref, pb_ref, dp_ref, dl_ref, db_ref):
        s, i = pl.program_id(0), pl.program_id(2)
        p = jnp.where(s == 0, pa_ref[...], pb_ref[...])
        g = _sigmoid(l_ref[...] + b_ref[...])
        dm = dm_ref[...]
        dp_ref[0] = (dm * g).astype(BF16)
        dl = dm * p * g * (1.0 - g)
        dl_ref[...] = dl.astype(BF16)

        @pl.when(i == 0)
        def _():
            db_ref[...] = jnp.zeros_like(db_ref)

        db_ref[...] += jnp.sum(dl, axis=0, keepdims=True)

    tile = pl.BlockSpec((tm, tc), lambda s, j, i: (i, j))
    return pl.pallas_call(
        body, grid=(2, nj, ni),
        in_specs=[tile, pl.BlockSpec((tm, tc), lambda s, j, i: (i, o0 + s * nj + j)),
                  pl.BlockSpec((1, tc), lambda s, j, i: (0, s * nj + j)), tile, tile],
        out_specs=[pl.BlockSpec((1, tm, tc), lambda s, j, i: (s, i, j)),
                   pl.BlockSpec((tm, tc), lambda s, j, i: (i, o0 + s * nj + j)),
                   pl.BlockSpec((1, tc), lambda s, j, i: (0, s * nj + j))],
        out_shape=[SDS((2, t, d), BF16), SDS(proj.shape, BF16), SDS((1, 2 * d), F32)],
        compiler_params=_params(("arbitrary", "arbitrary", "arbitrary")),
        name="merge_bwd")(dmerged, proj, b_gate, pa, pb)


def _swiglu_fwd(gu):
    t, f2 = gu.shape
    f = f2 // 2
    tm, tc = _tile(t, 512), _tile(f, 512)
    nj = f // tc

    def body(g_ref, u_ref, o_ref):
        g = g_ref[...]
        o_ref[...] = (g * _sigmoid(g) * u_ref[...]).astype(BF16)

    return pl.pallas_call(
        body, grid=(t // tm, nj),
        in_specs=[pl.BlockSpec((tm, tc), lambda i, j: (i, j)), pl.BlockSpec((tm, tc), lambda i, j: (i, nj + j))],
        out_specs=pl.BlockSpec((tm, tc), lambda i, j: (i, j)), out_shape=SDS((t, f), BF16),
        compiler_params=_params(("parallel", "parallel")), name="swiglu_fwd")(gu, gu)


def _swiglu_bwd(dact, gu):
    t, f2 = gu.shape
    f = f2 // 2
    tm = _tile(t, 128)

    def body(d_ref, gu_ref, o_ref):
        g, u, dv = gu_ref[:, :f], gu_ref[:, f:], d_ref[...]
        sg = _sigmoid(g)
        o_ref[:, :f] = (dv * u * _dsilu(g, sg)).astype(BF16)
        o_ref[:, f:] = (dv * (g * sg)).astype(BF16)

    return pl.pallas_call(
        body, grid=(t // tm,),
        in_specs=[pl.BlockSpec((tm, f), lambda i: (i, 0)), pl.BlockSpec((tm, f2), lambda i: (i, 0))],
        out_specs=pl.BlockSpec((tm, f2), lambda i: (i, 0)), out_shape=SDS((t, f2), BF16),
        compiler_params=_params(("parallel",)), name="swiglu_bwd")(dact, gu)


def _loss_head(y, target):
    t, d = y.shape
    tm = _tile(t, 256)

    def body(y_ref, t_ref, dy_ref, dyb_ref, l_ref):
        @pl.when(pl.program_id(0) == 0)
        def _():
            l_ref[...] = jnp.zeros_like(l_ref)

        e = y_ref[...] - t_ref[...]
        dy = e * (1.0 / d)
        dy_ref[...] = dy
        dyb_ref[...] = dy.astype(BF16)
        l_ref[...] += 0.5 * jnp.sum(jnp.mean(e * e, axis=-1, keepdims=True), axis=0, keepdims=True)

    row = pl.BlockSpec((tm, d), lambda i: (i, 0))
    return pl.pallas_call(
        body, grid=(t // tm,), in_specs=[row, row], out_specs=[row, row, pl.BlockSpec((1, 1), lambda i: (0, 0))],
        out_shape=[SDS((t, d), F32), SDS((t, d), BF16), SDS((1, 1), F32)], compiler_params=_params(("arbitrary",)),
        name="loss_head")(y, target)


def _rel_onehot(qi):
    p = lax.broadcasted_iota(jnp.int32, (REL_LANES, BAND), 1)
    r = lax.broadcasted_iota(jnp.int32, (REL_LANES, BAND), 0)
    idx = jnp.clip(qi + PAD - p, -REL_FUTURE, REL_PAST) + REL_FUTURE
    return (idx == r).astype(F32)


def _relbias_expand(rb):
    h = rb.shape[0]

    def body(rb_ref, o_ref):
        def step(qi, _):
            o_ref[qi] = _fdot(rb_ref[...], _rel_onehot(qi))
            return 0

        lax.fori_loop(0, CHUNK, step, 0)

    return pl.pallas_call(body, out_shape=SDS((CHUNK, h, BAND), F32), compiler_params=_params(),
                          name="relbias_expand")(rb)


def _relbias_reduce(dbias):
    h = dbias.shape[1]

    def body(db_ref, o_ref):
        def step(qi, acc):
            return acc + lax.dot_general(db_ref[qi], _rel_onehot(qi), NT, precision=HIGHEST,
                                         preferred_element_type=F32)

        o_ref[...] = lax.fori_loop(0, CHUNK, step, jnp.zeros((h, REL_LANES), F32))

    return pl.pallas_call(body, out_shape=SDS((h, REL_LANES), F32), compiler_params=_params(),
                          name="relbias_reduce")(dbias)


def _lower_bound(l_ref):
    l0, l1 = l_ref[0:1, :], l_ref[1:2, :]
    m = jnp.maximum(l0, l1)
    e0, e1 = jnp.exp(l0 - m), jnp.exp(l1 - m)
    return e0 / (e0 + e1)


def _tri(lower):
    r = lax.broadcasted_iota(jnp.int32, (CHUNK, CHUNK), 0)
    c = lax.broadcasted_iota(jnp.int32, (CHUNK, CHUNK), 1)
    return r >= c if lower else r <= c


def _hgrn_intra(qs, kk, b_s):
    rows = lax.broadcasted_iota(jnp.int32, (CHUNK, HEAD), 0)
    b = b_s[...]
    out = []
    for i in range(CHUNK // SUB):
        lo = i * SUB
        ref = jnp.zeros((1, HEAD), F32) if i == 0 else b_s[lo - 1:lo, :]
        eq = jnp.exp(b[lo:lo + SUB] - ref)
        qt = _split(qs[lo:lo + SUB] * eq)
        e = jnp.where(rows < lo + SUB, jnp.exp(jnp.minimum(ref - b, EXP_CLAMP)), 0.0)
        kt = _split(kk * e)
        out.append((eq, qt, e, kt))
    return out


def _hgrn_scores(blocks):
    tr = lax.broadcasted_iota(jnp.int32, (SUB, CHUNK), 0)
    tc = lax.broadcasted_iota(jnp.int32, (SUB, CHUNK), 1)
    return jnp.concatenate([jnp.where(tc <= tr + i * SUB, _dot3(qt, kt, NT), 0.0)
                            for i, (_, qt, _, kt) in enumerate(blocks)], axis=0)


def _hgrn_fwd(proj, lb_logits, gain, n_heads):
    t = proj.shape[0]
    nc = t // CHUNK
    da = n_heads * HEAD
    hp = MIX_HEADS
    wide = hp * HEAD

    def body(q_ref, f_ref, i_ref, g_ref, l_ref, gain_ref, y_ref, o_ref, st_ref, state, b_s):
        state[...] = jnp.zeros_like(state)
        lb_all = _lower_bound(l_ref)
        tril = _tri(True).astype(F32)

        def chunks(i, _):
            dot = functools.partial(lax.dot_general, preferred_element_type=F32)
            items = []
            for u in range(MIX_UNROLL):
                for hh in range(hp):
                    j = i * MIX_UNROLL + u
                    sl = pl.ds(pl.multiple_of(j * CHUNK, CHUNK), CHUNK)
                    cols = slice(hh * HEAD, (hh + 1) * HEAD)
                    lb = lb_all[:, cols]
                    fg = lb + (1.0 - lb) * _sigmoid(f_ref[sl, cols])
                    qv = q_ref[sl, cols]
                    gv = g_ref[sl, cols]
                    items.append(dict(hh=hh, j=j, sl=sl, cols=cols, lf=jnp.log(fg), kk=1.0 - fg, qs=qv * _sigmoid(qv),
                                      vb=i_ref[sl, cols].astype(BF16), gate=gv * _sigmoid(gv)))
            for it in items:
                it["b"] = _fdot(tril, it["lf"])
            for slot, it in enumerate(items):
                b = it["b"]
                b_s[slot] = b
                it["blocks"] = _hgrn_intra(it["qs"], it["kk"], b_s.at[slot])
                it["ebl"] = jnp.exp(b_s[slot, CHUNK - 1:CHUNK, :])
                it["qe"] = (it["qs"] * jnp.exp(b)).astype(BF16)
                it["kd"] = (it["kk"] * jnp.exp(b_s[slot, CHUNK - 1:CHUNK, :] - b)).astype(BF16)
            for it in items:
                it["a"] = _hgrn_scores(it["blocks"]).astype(BF16)
            for it in items:
                it["kv"] = dot(it["vb"], it["kd"], TN)
                it["o"] = dot(it["a"], it["vb"], NN)
            s_now = [state[hh] for hh in range(hp)]
            for it in items:
                it["s_in"] = s_now[it["hh"]]
                s_now[it["hh"]] = it["s_in"] * it["ebl"] + it["kv"]
            for hh in range(hp):
                state[hh] = s_now[hh]
            for it in items:
                it["o"] = it["o"] + dot(it["qe"], it["s_in"].astype(BF16), NT)
            for it in items:
                o, sl, cols = it["o"], it["sl"], it["cols"]
                st_ref[it["hh"], it["j"]] = it["s_in"]
                o_ref[sl, cols] = o
                rr = lax.rsqrt(jnp.mean(o * o, axis=-1, keepdims=True) + EPS)
                y_ref[sl, cols] = (o * rr * gain_ref[:, cols] * it["gate"]).astype(BF16)
            return 0

        assert nc % MIX_UNROLL == 0, (nc, MIX_UNROLL)
        lax.fori_loop(0, nc // MIX_UNROLL, chunks, 0)

    col = lambda k: pl.BlockSpec((t, wide), lambda h: (0, k * (n_heads // hp) + h))
    vec = pl.BlockSpec((1, wide), lambda h: (0, h))
    return pl.pallas_call(
        body, grid=(n_heads // hp,),
        in_specs=[col(0), col(1), col(2), col(3), pl.BlockSpec((2, wide), lambda h: (0, h)), vec],
        out_specs=[pl.BlockSpec((t, wide), lambda h: (0, h)), pl.BlockSpec((t, wide), lambda h: (0, h)),
                   pl.BlockSpec((hp, nc, HEAD, HEAD), lambda h: (h, 0, 0, 0))],
        out_shape=[SDS((t, da), BF16), SDS((t, da), F32), SDS((n_heads, nc, HEAD, HEAD), F32)],
        scratch_shapes=[pltpu.VMEM((hp, HEAD, HEAD), F32), pltpu.VMEM((hp * MIX_UNROLL, CHUNK, HEAD), F32)],
        compiler_params=_params(("parallel",)), name="hgrn_fwd")(proj, proj, proj, proj, lb_logits, gain)


def _hgrn_bwd(dproj, proj, o_pre, states, dy, lb_logits, gain, n_heads, deps=()):
    t = proj.shape[0]
    nc = t // CHUNK
    da = n_heads * HEAD
    hp = MIX_HEADS
    wide = hp * HEAD

    def body(*refs):
        (q_ref, f_ref, i_ref, g_ref, o_ref, st_ref, dy_ref, l_ref, gain_ref,
         dproj_ref, dl_ref, dgain_ref, res, dstate, b_s) = refs[1 + len(deps):]

        @pl.when(pl.program_id(1) == 0)
        def _():
            dstate[...] = jnp.zeros_like(dstate)
            lb_all = _lower_bound(l_ref)
            tril_m, tril, triu = _tri(True), _tri(True).astype(F32), _tri(False).astype(F32)
            last = lax.broadcasted_iota(jnp.int32, (CHUNK, HEAD), 0) == CHUNK - 1

            def chunks(i, carry):
                dot = functools.partial(lax.dot_general, preferred_element_type=F32)
                items = []
                for u in range(MIX_UNROLL_BWD):
                    for hh in range(hp):
                        j = nc - 1 - (i * MIX_UNROLL_BWD + u)
                        sl = pl.ds(pl.multiple_of(j * CHUNK, CHUNK), CHUNK)
                        cols = slice(hh * HEAD, (hh + 1) * HEAD)
                        lb, gain_v = lb_all[:, cols], gain_ref[:, cols]
                        sg = _sigmoid(f_ref[sl, cols])
                        fg = lb + (1.0 - lb) * sg
                        qv = q_ref[sl, cols]
                        sq = _sigmoid(qv)
                        gv = g_ref[sl, cols]
                        sgg = _sigmoid(gv)
                        silg = gv * sgg
                        o = o_ref[sl, cols]
                        dyv = dy_ref[sl, cols]
                        rr = lax.rsqrt(jnp.mean(o * o, axis=-1, keepdims=True) + EPS)
                        on = o * rr
                        don = dyv * gain_v * silg
                        do = (rr * don - o * (rr * rr * rr) * jnp.mean(don * o, axis=-1, keepdims=True)).astype(BF16)
                        items.append(dict(
                            hh=hh, j=j, sl=sl, cols=cols, lb=lb, sg=sg, fg=fg, kk=1.0 - fg, qv=qv, sq=sq, qs=qv * sq,
                            vb=i_ref[sl, cols].astype(BF16), do=do, dg=dyv * on * gain_v * _dsilu(gv, sgg),
                            dgain=jnp.sum(dyv * on * silg, axis=0, keepdims=True)))
                for it in items:
                    it["b"] = _fdot(tril, jnp.log(it["fg"]))
                for slot, it in enumerate(items):
                    b = it["b"]
                    b_s[slot] = b
                    it["blocks"] = _hgrn_intra(it["qs"], it["kk"], b_s.at[slot])
                    bl = b_s[slot, CHUNK - 1:CHUNK, :]
                    it["eb"], it["ebl"], it["ekd"] = jnp.exp(b), jnp.exp(bl), jnp.exp(bl - b)
                    it["s_in"] = st_ref[it["hh"], it["j"]]
                for it in items:
                    it["a"] = _hgrn_scores(it["blocks"]).astype(BF16)
                    it["da"] = jnp.where(tril_m, dot(it["do"], it["vb"], NT), 0.0)
                for it in items:
                    dq_rows = []
                    dk = jnp.zeros((CHUNK, HEAD), F32)
                    for blk, (eq, qt, e, kt) in enumerate(it["blocks"]):
                        da_i = _split(it["da"][blk * SUB:(blk + 1) * SUB])
                        dq_rows.append(eq * _dot3(da_i, kt, NN))
                        dk = dk + e * _dot3(da_i, qt, TN)
                    it["dq"] = jnp.concatenate(dq_rows, axis=0) + dot(it["do"], it["s_in"].astype(BF16), NN) * it["eb"]
                    it["dk"] = dk
                    it["dv"] = dot(it["a"], it["do"], TN)
                    it["g"] = dot(it["do"], (it["qs"] * it["eb"]).astype(BF16), TN)
                ds_now = [dstate[hh] for hh in range(hp)]
                for it in items:
                    it["ds_out"] = ds_now[it["hh"]]
                    ds_now[it["hh"]] = it["ds_out"] * it["ebl"] + it["g"]
                for hh in range(hp):
                    dstate[hh] = ds_now[hh]
                for it in items:
                    dsb = it["ds_out"].astype(BF16)
                    it["dv"] = it["dv"] + dot((it["kk"] * it["ekd"]).astype(BF16), dsb, NT)
                    it["dk_state"] = it["ekd"] * dot(it["vb"], dsb, NN)
                for it in items:
                    kk, dk_state = it["kk"], it["dk_state"]
                    it["dk"] = it["dk"] + dk_state
                    extra = (jnp.sum(kk * dk_state, axis=0, keepdims=True)
                             + it["ebl"] * jnp.sum(it["s_in"] * it["ds_out"], axis=0, keepdims=True))
                    it["db"] = it["qs"] * it["dq"] - kk * it["dk"] + jnp.where(last, extra, 0.0)
                for it in items:
                    it["dlf"] = _fdot(triu, it["db"])
                carry = list(carry)
                for it in items:
                    hh, sl, cols, sg, lb = it["hh"], it["sl"], it["cols"], it["sg"], it["lb"]
                    dfg = it["dlf"] / it["fg"] - it["dk"]
                    dlb_acc, dgain_acc = carry[hh]
                    carry[hh] = (dlb_acc + jnp.sum(dfg * (1.0 - sg), axis=0, keepdims=True), dgain_acc + it["dgain"])
                    res[0, sl, cols] = (it["dq"] * _dsilu(it["qv"], it["sq"])).astype(BF16)
                    res[1, sl, cols] = (dfg * (1.0 - lb) * sg * (1.0 - sg)).astype(BF16)
                    res[2, sl, cols] = it["dv"].astype(BF16)
                    res[3, sl, cols] = it["dg"].astype(BF16)
                return tuple(carry)

            assert nc % MIX_UNROLL_BWD == 0, (nc, MIX_UNROLL_BWD)
            zero = jnp.zeros((1, HEAD), F32)
            sums = lax.fori_loop(0, nc // MIX_UNROLL_BWD, chunks, ((zero, zero),) * hp)
            for hh, (dlb, dgain) in enumerate(sums):
                cols = slice(hh * HEAD, (hh + 1) * HEAD)
                lb = lb_all[:, cols]
                dgain_ref[:, cols] = dgain
                dl0 = dlb * lb * (1.0 - lb)
                dl_ref[0:1, cols] = dl0
                dl_ref[1:2, cols] = -dl0

        dproj_ref[...] = res[pl.program_id(1)]

    ng = n_heads // hp
    col = lambda k: pl.BlockSpec((t, wide), lambda h, p: (0, k * ng + h))
    head = pl.BlockSpec((t, wide), lambda h, p: (0, h))
    vec = pl.BlockSpec((1, wide), lambda h, p: (0, h))
    return pl.pallas_call(
        body, grid=(ng, 4),
        in_specs=[ANY] * (1 + len(deps)) + [col(0), col(1), col(2), col(3), head,
                  pl.BlockSpec((hp, nc, HEAD, HEAD), lambda h, p: (h, 0, 0, 0)),
                  head, pl.BlockSpec((2, wide), lambda h, p: (0, h)), vec],
        out_specs=[pl.BlockSpec((t, wide), lambda h, p: (0, p * ng + h)),
                   pl.BlockSpec((2, wide), lambda h, p: (0, h)), vec],
        out_shape=[SDS(dproj.shape, BF16), SDS((2, da), F32), SDS((1, da), F32)],
        scratch_shapes=[pltpu.VMEM((4, t, wide), BF16), pltpu.VMEM((hp, HEAD, HEAD), F32),
                        pltpu.VMEM((hp * MIX_UNROLL_BWD, CHUNK, HEAD), F32)],
        input_output_aliases={0: 0}, compiler_params=_params(("arbitrary", "arbitrary")),
        name="hgrn_bwd")(dproj, *deps, proj, proj, proj, proj, o_pre, states, dy, lb_logits, gain)


ROWS = 256


def _head_norm(x_ref, gain, dst, dst_off, t):
    def step(i, _):
        sl = pl.ds(pl.multiple_of(i * ROWS, ROWS), ROWS)
        xv = x_ref[sl, :]
        r = lax.rsqrt(jnp.mean(xv * xv, axis=-1, keepdims=True) + EPS)
        dst[pl.ds(pl.multiple_of(dst_off + i * ROWS, ROWS), ROWS), :] = (xv * r * gain).astype(BF16)
        return 0

    lax.fori_loop(0, t // ROWS, step, 0)


def _head_norm_bwd(x_ref, gain, dn_ref, dn_off, out, slot, t):
    def step(i, acc):
        sl = pl.ds(pl.multiple_of(i * ROWS, ROWS), ROWS)
        xv = x_ref[sl, :]
        dn = dn_ref[pl.ds(pl.multiple_of(dn_off + i * ROWS, ROWS), ROWS), :]
        r = lax.rsqrt(jnp.mean(xv * xv, axis=-1, keepdims=True) + EPS)
        u = dn * gain
        out[slot, sl, :] = r * u - xv * (r * r * r) * jnp.mean(u * xv, axis=-1, keepdims=True)
        return acc + jnp.sum(dn * (xv * r), axis=0, keepdims=True)

    return lax.fori_loop(0, t // ROWS, step, jnp.zeros((1, HEAD), F32))


def _attn_scores(qn, kpad, n):
    qc = qn[pl.ds(pl.multiple_of(n * CHUNK, CHUNK), CHUNK), :]
    band = pl.ds(pl.multiple_of(n * CHUNK, CHUNK), BAND)
    return qc, band, lax.dot_general(qc, kpad[band, :], NT, preferred_element_type=F32)


def _attn_softmax(raw, bias_ref, n):
    s = raw * (HEAD ** -0.5) + bias_ref[0]
    col = lax.broadcasted_iota(jnp.int32, (CHUNK, BAND), 1)
    s = jnp.where(col >= PAD - n * CHUNK, s, -jnp.inf)
    p = jnp.exp(s - jnp.max(s, axis=-1, keepdims=True))
    return p / jnp.sum(p, axis=-1, keepdims=True)


def _attn_fwd(proj, q_gain, k_gain, bias, n_heads, col0):
    t = proj.shape[0]
    nc = t // CHUNK

    def body(q_ref, k_ref, v_ref, qg_ref, kg_ref, bias_ref, y_ref, qn, kpad, vpad):
        kpad[0:PAD, :] = jnp.zeros((PAD, HEAD), BF16)
        vpad[0:PAD, :] = jnp.zeros((PAD, HEAD), BF16)
        _head_norm(q_ref, qg_ref[...], qn, 0, t)
        _head_norm(k_ref, kg_ref[...], kpad, PAD, t)

        def copy_v(i, _):
            vpad[pl.ds(pl.multiple_of(PAD + i * ROWS, ROWS), ROWS), :] = v_ref[
                pl.ds(pl.multiple_of(i * ROWS, ROWS), ROWS), :].astype(BF16)
            return 0

        lax.fori_loop(0, t // ROWS, copy_v, 0)

        def chunks(i, _):
            ns = [i * ATT_UNROLL + u for u in range(ATT_UNROLL)]
            scored = [_attn_scores(qn, kpad, n) for n in ns]
            probs = [_attn_softmax(raw, bias_ref, n).astype(BF16) for n, (_, _, raw) in zip(ns, scored)]
            outs = [lax.dot_general(p, vpad[band, :], NN, preferred_element_type=F32).astype(BF16)
                    for p, (_, band, _) in zip(probs, scored)]
            for n, o in zip(ns, outs):
                y_ref[pl.ds(pl.multiple_of(n * CHUNK, CHUNK), CHUNK), :] = o
            return 0

        assert nc % ATT_UNROLL == 0, (nc, ATT_UNROLL)
        lax.fori_loop(0, nc // ATT_UNROLL, chunks, 0)

    col = lambda k: pl.BlockSpec((t, HEAD), lambda h: (0, col0 + k * n_heads + h))
    vec = pl.BlockSpec((1, HEAD), lambda h: (0, 0))
    return pl.pallas_call(
        body, grid=(n_heads,),
        in_specs=[col(0), col(1), col(2), vec, vec, pl.BlockSpec((1, CHUNK, BAND), lambda h: (h, 0, 0))],
        out_specs=pl.BlockSpec((t, HEAD), lambda h: (0, h)), out_shape=SDS((t, n_heads * HEAD), BF16),
        scratch_shapes=[pltpu.VMEM((t, HEAD), BF16), pltpu.VMEM((t + PAD, HEAD), BF16), pltpu.VMEM((t + PAD, HEAD), BF16)],
        compiler_params=_params(("parallel",)), name="attn_fwd")(proj, proj, proj, q_gain, k_gain, bias)


def _attn_bwd(dproj, proj, q_gain, k_gain, bias, dy, n_heads, col0, deps=()):
    t = proj.shape[0]
    nc = t // CHUNK

    def body(*refs):
        (q_ref, k_ref, v_ref, qg_ref, kg_ref, bias_ref, dy_ref,
         dproj_ref, dbias_ref, dqg_ref, dkg_ref, qn, kpad, vpad, dqn, dk_acc, dv_acc, res) = refs[1 + len(deps):]
        h, part = pl.program_id(0), pl.program_id(1)

        @pl.when(part == 0)
        def _():
            kpad[0:PAD, :] = jnp.zeros((PAD, HEAD), BF16)
            vpad[0:PAD, :] = jnp.zeros((PAD, HEAD), BF16)
            _head_norm(q_ref, qg_ref[...], qn, 0, t)
            _head_norm(k_ref, kg_ref[...], kpad, PAD, t)

            def prep(i, _):
                sl = pl.ds(pl.multiple_of(PAD + i * ROWS, ROWS), ROWS)
                vpad[sl, :] = v_ref[pl.ds(pl.multiple_of(i * ROWS, ROWS), ROWS), :].astype(BF16)
                return 0

            lax.fori_loop(0, t // ROWS, prep, 0)

            def clear(i, _):
                sl = pl.ds(pl.multiple_of(i * ROWS, ROWS), ROWS)
                dk_acc[sl, :] = jnp.zeros((ROWS, HEAD), F32)
                dv_acc[sl, :] = jnp.zeros((ROWS, HEAD), F32)
                return 0

            lax.fori_loop(0, (t + PAD) // ROWS, clear, 0)
            dbias_ref[0] = jnp.zeros((CHUNK, BAND), F32)

            def chunks(i, _):
                dot = functools.partial(lax.dot_general, preferred_element_type=F32)
                ns = [i * ATT_UNROLL_BWD + u for u in range(ATT_UNROLL_BWD)]
                scored = [_attn_scores(qn, kpad, n) for n in ns]
                dos = [dy_ref[pl.ds(pl.multiple_of(n * CHUNK, CHUNK), CHUNK), :].astype(BF16) for n in ns]
                dps = [dot(do, vpad[band, :], NT) for do, (_, band, _) in zip(dos, scored)]
                ps, dss = [], []
                for n, (_, _, raw), dp in zip(ns, scored, dps):
                    p = _attn_softmax(raw, bias_ref, n)
                    ds = p * (dp - jnp.sum(dp * p, axis=-1, keepdims=True))
                    dbias_ref[0] += ds
                    ps.append(p.astype(BF16))
                    dss.append((ds * (HEAD ** -0.5)).astype(BF16))
                dqs = [dot(d, kpad[band, :], NN) for d, (_, band, _) in zip(dss, scored)]
                dks = [dot(d, qc, TN) for d, (qc, _, _) in zip(dss, scored)]
                dvs = [dot(p, do, TN) for p, do in zip(ps, dos)]
                for n, (_, band, _), dq, dk, dv in zip(ns, scored, dqs, dks, dvs):
                    dqn[pl.ds(pl.multiple_of(n * CHUNK, CHUNK), CHUNK), :] = dq
                    dk_acc[band, :] += dk
                    dv_acc[band, :] += dv
                return 0

            assert nc % ATT_UNROLL_BWD == 0, (nc, ATT_UNROLL_BWD)
            lax.fori_loop(0, nc // ATT_UNROLL_BWD, chunks, 0)
            dqg = _head_norm_bwd(q_ref, qg_ref[...], dqn, 0, res, 0, t)
            dkg = _head_norm_bwd(k_ref, kg_ref[...], dk_acc, PAD, res, 1, t)

            def put_v(i, _):
                sl = pl.ds(pl.multiple_of(i * ROWS, ROWS), ROWS)
                res[2, sl, :] = dv_acc[pl.ds(pl.multiple_of(PAD + i * ROWS, ROWS), ROWS), :]
                return 0

            lax.fori_loop(0, t // ROWS, put_v, 0)

            @pl.when(h == 0)
            def _():
                dqg_ref[...] = jnp.zeros_like(dqg_ref)
                dkg_ref[...] = jnp.zeros_like(dkg_ref)

            dqg_ref[...] += dqg
            dkg_ref[...] += dkg

        dproj_ref[...] = res[part].astype(BF16)

    col = lambda k: pl.BlockSpec((t, HEAD), lambda h, p: (0, col0 + k * n_heads + h))
    vec = pl.BlockSpec((1, HEAD), lambda h, p: (0, 0))
    btile = pl.BlockSpec((1, CHUNK, BAND), lambda h, p: (h, 0, 0))
    return pl.pallas_call(
        body, grid=(n_heads, 3),
        in_specs=[ANY] * (1 + len(deps)) + [col(0), col(1), col(2), vec, vec, btile,
                                            pl.BlockSpec((t, HEAD), lambda h, p: (0, h))],
        out_specs=[pl.BlockSpec((t, HEAD), lambda h, p: (0, col0 + p * n_heads + h)), btile, vec, vec],
        out_shape=[SDS(dproj.shape, BF16), SDS((n_heads, CHUNK, BAND), F32), SDS((1, HEAD), F32), SDS((1, HEAD), F32)],
        scratch_shapes=[pltpu.VMEM((t, HEAD), BF16), pltpu.VMEM((t + PAD, HEAD), BF16), pltpu.VMEM((t + PAD, HEAD), BF16),
                        pltpu.VMEM((t, HEAD), F32), pltpu.VMEM((t + PAD, HEAD), F32), pltpu.VMEM((t + PAD, HEAD), F32),
                        pltpu.VMEM((3, t, HEAD), F32)],
        input_output_aliases={0: 0}, compiler_params=_params(("arbitrary", "arbitrary")),
        name="attn_bwd")(dproj, *deps, proj, proj, proj, q_gain, k_gain, bias, dy)


def _place():
    x, y, c = lax.axis_index("x"), lax.axis_index("y"), lax.axis_index("c")
    others = [(1 - x, y), (x, 1 - y), (1 - x, 1 - y)]
    return x, y, c, others


def _chunk_of(ref, kind, chip, half, shard_shape):
    r, n = shard_shape
    hr = r // 2
    if kind == "col":
        rows = pl.ds(0, r) if half is None else pl.ds(half * hr, hr)
        return ref.at[rows, pl.ds(chip * n, n)]
    rows = pl.ds(chip * r, r) if half is None else pl.ds(chip * r + half * hr, hr)
    return ref.at[rows, :]


EFFECT = pltpu.SideEffectType.DATAFLOW_SIDE_EFFECTING


def _start_copies(name, bufs, plan, n, deps):
    nb, nd = len(bufs), len(deps)

    def body(*refs):
        send, recv, token = refs[nb + nd], refs[nb + nd + 1], refs[-1]
        for cp in plan(refs[:nb], send, recv)[0]:
            cp.start()
        token[...] = jnp.zeros_like(token)

    out = pl.pallas_call(
        body, name=name,
        out_shape=(pltpu.SemaphoreType.DMA((n,)), pltpu.SemaphoreType.DMA((n,)),
                   *[pltpu.HBM(b.shape, b.dtype) for b in bufs], SDS((8, 128), F32)),
        in_specs=[HBM] * nb + [ANY] * nd,
        out_specs=(SEM, SEM, *[HBM] * nb, pl.BlockSpec(memory_space=pltpu.VMEM)),
        input_output_aliases={i: 2 + i for i in range(nb)},
        compiler_params=pltpu.CompilerParams(has_side_effects=EFFECT),
    )(*[pltpu.with_memory_space_constraint(b, pltpu.HBM) for b in bufs], *deps)
    return out[0], out[1], list(out[2:2 + nb]), out[-1]


def _wait_copies(name, bufs, send, recv, plan, after):
    nb = len(bufs)

    def body(*refs):
        sends, recvs = plan(refs[:nb], refs[nb], refs[nb + 1])
        for cp in sends:
            cp.wait_send()
        for cp in recvs:
            cp.wait_recv()

    out = pl.pallas_call(
        body, name=name, out_shape=tuple(pltpu.HBM(b.shape, b.dtype) for b in bufs),
        in_specs=[HBM] * nb + [SEM, SEM] + [ANY] * len(after), out_specs=tuple([HBM] * nb),
        input_output_aliases={i: i for i in range(nb)},
        compiler_params=pltpu.CompilerParams(has_side_effects=EFFECT),
    )(*bufs, send, recv, *after)
    return list(out)


def _remote(src, dst, send, recv, i, dev):
    return pltpu.make_async_remote_copy(src_ref=src, dst_ref=dst, send_sem=send.at[i], recv_sem=recv.at[i],
                                        device_id=dev, device_id_type=MESH)


def _plan_gather_ici(kinds, shapes):
    def plan(refs, send, recv):
        x, y, c, others = _place()
        sends, recvs = [], []
        for w, (kind, ss) in enumerate(zip(kinds, shapes)):
            for p, (px, py) in enumerate(others):
                mine = _chunk_of(refs[w], kind, 2 * x + y, c, ss)
                theirs = _chunk_of(refs[w], kind, 2 * px + py, c, ss)
                sends.append(_remote(mine, mine, send, recv, 3 * w + p, (px, py, c)))
                recvs.append(_remote(theirs, theirs, send, recv, 3 * w + p, (px, py, c)))
        return sends, recvs

    return plan, 3 * len(kinds)


def _plan_gather_pass(kinds, shapes):
    def plan(refs, send, recv):
        x, y, c, others = _place()
        sends, recvs = [], []
        for w, (kind, ss) in enumerate(zip(kinds, shapes)):
            for p, (px, py) in enumerate(others):
                got = _chunk_of(refs[w], kind, 2 * px + py, c, ss)
                coming = _chunk_of(refs[w], kind, 2 * px + py, 1 - c, ss)
                sends.append(_remote(got, got, send, recv, 3 * w + p, (x, y, 1 - c)))
                recvs.append(_remote(coming, coming, send, recv, 3 * w + p, (x, y, 1 - c)))
        return sends, recvs

    return plan, 3 * len(kinds)


def _plan_pair(kinds, shapes):
    nw = len(kinds)

    def plan(refs, send, recv):
        x, y, c, _ = _place()
        sends = []
        for w, (kind, ss) in enumerate(zip(kinds, shapes)):
            for k in range(4):
                sends.append(_remote(_chunk_of(refs[w], kind, k, 1 - c, ss), refs[nw + w].at[k], send, recv,
                                     4 * w + k, (x, y, 1 - c)))
        return sends, sends

    return plan, 4 * nw


def _plan_chip(nw):
    def plan(refs, send, recv):
        x, y, c, others = _place()
        sends = []
        for w in range(nw):
            for p, (px, py) in enumerate(others):
                sends.append(_remote(refs[w].at[2 * px + py], refs[nw + w].at[p], send, recv, 3 * w + p, (px, py, c)))
        return sends, sends

    return plan, 3 * nw


def _plan_share(nw):
    def plan(refs, send, recv):
        x, y, c, _ = _place()
        sends = [_remote(refs[w].at[c], refs[w].at[c], send, recv, w, (x, y, 1 - c)) for w in range(nw)]
        recvs = [_remote(refs[w].at[1 - c], refs[w].at[1 - c], send, recv, w, (x, y, 1 - c)) for w in range(nw)]
        return sends, recvs

    return plan, nw


def _pair_add(grad, got, kind, shard_shape, pos, name):
    r, n = shard_shape
    hr = r // 2
    tr, tn = _tile(hr, 256, 16), _tile(n, 1408)
    nr, nn = hr // tr, n // tn
    if kind == "col":
        g_spec = pl.BlockSpec((tr, tn), lambda k, i, j, c: (c[1] * nr + i, k * nn + j))
    else:
        g_spec = pl.BlockSpec((tr, tn), lambda k, i, j, c: ((2 * k + c[1]) * nr + i, j))
    o_spec = pl.BlockSpec((1, tr, tn), lambda k, i, j, c: (k, i, j))

    def body(c_ref, g_ref, r_ref, o32_ref, o16_ref):
        s = g_ref[...] + r_ref[0]
        o32_ref[0] = s
        o16_ref[0] = s.astype(BF16)

    return pl.pallas_call(
        body,
        grid_spec=pltpu.PrefetchScalarGridSpec(num_scalar_prefetch=1, grid=(4, nr, nn), in_specs=[g_spec, o_spec],
                                               out_specs=[o_spec, o_spec]),
        out_shape=[SDS((4, hr, n), F32), SDS((4, hr, n), BF16)],
        compiler_params=_params(("parallel", "parallel", "parallel")), name=name)(pos, grad, got)


def _chip_add(part32, got16, pos, name):
    _, hr, n = part32.shape
    tr, tn = _tile(hr, 256, 16), _tile(n, 1408)
    own = pl.BlockSpec((1, tr, tn), lambda i, j, p: (p[0], i, j))
    oth = pl.BlockSpec((3, tr, tn), lambda i, j, p: (0, i, j))

    def body(p_ref, own_ref, oth_ref, o_ref):
        o_ref[0] = ((own_ref[0] + oth_ref[0].astype(F32)) + oth_ref[1].astype(F32)) + oth_ref[2].astype(F32)

    return pl.pallas_call(
        body,
        grid_spec=pltpu.PrefetchScalarGridSpec(num_scalar_prefetch=1, grid=(hr // tr, n // tn), in_specs=[own, oth],
                                               out_specs=pl.BlockSpec((1, tr, tn), lambda i, j, p: (p[1], i, j))),
        out_shape=SDS((2, hr, n), F32), compiler_params=_params(("parallel", "parallel")), name=name)(pos, part32, got16)


def _adamw_math(w, g, m, v):
    m = ADAM_B1 * m + (1.0 - ADAM_B1) * g
    v = ADAM_B2 * v + (1.0 - ADAM_B2) * (g * g)
    m_hat = m / (1.0 - ADAM_B1 ** ADAM_STEP)
    v_hat = v / (1.0 - ADAM_B2 ** ADAM_STEP)
    return -ADAM_LR * (m_hat / (jnp.sqrt(v_hat) + ADAM_EPS) + ADAM_WD * w), m, v


def _adamw(w, g, m, v, name):
    r, n = w.shape
    tr, tn = _tile(r, 256, 16), _tile(n, 1408)

    def body(w_ref, g_ref, m_ref, v_ref, d_ref, nm_ref, nv_ref):
        d_ref[...], nm_ref[...], nv_ref[...] = _adamw_math(w_ref[...], g_ref[...], m_ref[...], v_ref[...])

    tile = pl.BlockSpec((tr, tn), lambda i, j: (i, j))
    return pl.pallas_call(
        body, grid=(r // tr, n // tn), in_specs=[tile] * 4, out_specs=[tile] * 3, out_shape=[SDS((r, n), F32)] * 3,
        compiler_params=_params(("parallel", "parallel")), name=name)(w, g, m, v)


def _small_allreduce_adamw(g, w, m, v, deps=()):
    length = g.shape[1]

    def body(*refs):
        g_ref, w_ref, m_ref, v_ref = refs[:4]
        gs_ref, d_ref, nm_ref, nv_ref, buf, send, recv = refs[4 + len(deps):]
        x, y, c = lax.axis_index("x"), lax.axis_index("y"), lax.axis_index("c")
        me = 4 * x + 2 * y + c
        buf[me] = g_ref[...]
        cps = []
        for d in range(1, 8):
            peer = (x ^ (d >> 2), y ^ ((d >> 1) & 1), c ^ (d & 1))
            cp = pltpu.make_async_remote_copy(src_ref=buf.at[me], dst_ref=buf.at[me], send_sem=send.at[d - 1],
                                              recv_sem=recv.at[d - 1], device_id=peer, device_id_type=MESH)
            cp.start()
            cps.append(cp)
        for cp in cps:
            cp.wait()
        total = buf[0]
        for d in range(1, 8):
            total = total + buf[d]
        gs_ref[...] = total
        d_ref[...], nm_ref[...], nv_ref[...] = _adamw_math(w_ref[...], total, m_ref[...], v_ref[...])

    vm = pl.BlockSpec(memory_space=pltpu.VMEM)
    return pl.pallas_call(
        body, in_specs=[vm] * 4 + [ANY] * len(deps), out_specs=[vm] * 4, out_shape=[SDS((1, length), F32)] * 4,
        scratch_shapes=[pltpu.VMEM((8, 1, length), F32), pltpu.SemaphoreType.DMA((7,)), pltpu.SemaphoreType.DMA((7,))],
        compiler_params=pltpu.CompilerParams(has_side_effects=True), name="small_allreduce_adamw")(g, w, m, v, *deps)


def kernel(x, w_in, b_gate, norm_mix, norm_ffn, hgrn_lb_logits, hgrn_out_gain, q_gain, k_gain, rel_bias, w_proj_a, w_proj_b, w_out, w_ffn_in, w_ffn_out, loss_target, m_w_in, m_b_gate, m_norm_mix, m_norm_ffn, m_hgrn_lb_logits, m_hgrn_out_gain, m_q_gain, m_k_gain, m_rel_bias, m_w_proj_a, m_w_proj_b, m_w_out, m_w_ffn_in, m_w_ffn_out, v_w_in, v_b_gate, v_norm_mix, v_norm_ffn, v_hgrn_lb_logits, v_hgrn_out_gain, v_q_gain, v_k_gain, v_rel_bias, v_w_proj_a, v_w_proj_b, v_w_out, v_w_ffn_in, v_w_ffn_out):
    t, d = x.shape[1], x.shape[2]
    d_a = hgrn_out_gain.shape[1]
    h_a = d_a // HEAD
    h_b = rel_bias.shape[1]
    d_b = h_b * HEAD
    x0 = x.reshape(t, d)
    target = loss_target.reshape(t, d)
    pos = jnp.stack([2 * lax.axis_index("x") + lax.axis_index("y"), lax.axis_index("c")]).astype(jnp.int32)

    names = ["w_in", "w_proj_a", "w_proj_b", "w_out", "w_ffn_in", "w_ffn_out"]
    big = dict(zip(names, [w_in[0], w_proj_a[0], w_proj_b[0], w_out[0], w_ffn_in[0], w_ffn_out[0]]))
    big_m = dict(zip(names, [m_w_in[0], m_w_proj_a[0], m_w_proj_b[0], m_w_out[0], m_w_ffn_in[0], m_w_ffn_out[0]]))
    big_v = dict(zip(names, [v_w_in[0], v_w_proj_a[0], v_w_proj_b[0], v_w_out[0], v_w_ffn_in[0], v_w_ffn_out[0]]))
    kind = dict(zip(names, ["col", "col", "col", "row", "col", "row"]))
    shape = {nm: big[nm].shape for nm in names}

    def gather_start(tag, group, deps):
        plan, n = _plan_gather_ici([kind[g] for g in group], [shape[g] for g in group])
        fulls = [_cast_into_full(big[g], kind[g], pos, "cast_" + g) for g in group]
        send, recv, bufs, token = _start_copies("gather_ici_start_" + tag, fulls, plan, n, deps)
        return (tag, group, plan, send, recv, bufs), token

    def gather_pass(state, after):
        tag, group, plan, send, recv, bufs = state
        bufs = _wait_copies("gather_ici_wait_" + tag, bufs, send, recv, plan, after)
        plan, n = _plan_gather_pass([kind[g] for g in group], [shape[g] for g in group])
        send, recv, bufs, token = _start_copies("gather_pass_start_" + tag, bufs, plan, n, ())
        return (tag, group, plan, send, recv, bufs), token

    def gather_done(state, after):
        tag, group, plan, send, recv, bufs = state
        return _wait_copies("gather_pass_wait_" + tag, bufs, send, recv, plan, after)

    def reduce_start(tag, group, grads, deps):
        plan, n = _plan_pair([kind[g] for g in group], [shape[g] for g in group])
        lands = [lax.empty((4, shape[g][0] // 2, shape[g][1]), F32) for g in group]
        send, recv, bufs, token = _start_copies("pair_start_" + tag, list(grads) + lands, plan, n, deps)
        return dict(tag=tag, group=group, plan=plan, send=send, recv=recv, bufs=bufs), token

    def reduce_pair_done(st, after):
        tag, group, nw = st["tag"], st["group"], len(st["group"])
        bufs = _wait_copies("pair_wait_" + tag, st["bufs"], st["send"], st["recv"], st["plan"], after)
        parts = [_pair_add(g, l, kind[nm], shape[nm], pos, "pair_add_" + nm)
                 for g, l, nm in zip(bufs[:nw], bufs[nw:], group)]
        lands = [lax.empty((3, shape[g][0] // 2, shape[g][1]), BF16) for g in group]
        plan, n = _plan_chip(nw)
        send, recv, bufs, token = _start_copies("chip_start_" + tag, [p16 for _, p16 in parts] + lands, plan, n, ())
        return dict(st, plan=plan, send=send, recv=recv, bufs=bufs, p32=[p32 for p32, _ in parts]), token

    def reduce_chip_done(st, after):
        tag, group, nw = st["tag"], st["group"], len(st["group"])
        bufs = _wait_copies("chip_wait_" + tag, st["bufs"], st["send"], st["recv"], st["plan"], after)
        finals = [_chip_add(p32, got, pos, "chip_add_" + nm) for p32, got, nm in zip(st["p32"], bufs[nw:], group)]
        plan, n = _plan_share(nw)
        send, recv, bufs, token = _start_copies("share_start_" + tag, finals, plan, n, ())
        return dict(st, plan=plan, send=send, recv=recv, bufs=bufs), token

    g_big, upd = {}, {}

    def reduce_finish(st, after):
        bufs = _wait_copies("share_wait_" + st["tag"], st["bufs"], st["send"], st["recv"], st["plan"], after)
        for full, nm in zip(bufs, st["group"]):
            g_big[nm] = full.reshape(shape[nm])
            upd[nm] = _adamw(big[nm], g_big[nm], big_m[nm], big_v[nm], "adamw_" + nm)

    ga, token = gather_start("a", ["w_in"], ())
    gb, token = gather_start("b", ["w_proj_a", "w_proj_b", "w_out"], (token,))
    gc, token = gather_start("c", ["w_ffn_in"], (token,))
    gd, token = gather_start("d", ["w_ffn_out"], (token,))
    h1, r1 = _rmsnorm_fwd(x0, norm_mix, "rmsnorm_mix")
    rb = jnp.pad(rel_bias[0], ((0, 0), (0, REL_LANES - N_REL)))
    bias = _relbias_expand(rb).transpose(1, 0, 2)
    ga, token = gather_pass(ga, (h1, bias, token))
    (wg_in,) = gather_done(ga, ())
    proj = _matmul(h1, wg_in, name="proj_in")
    y_a, o_pre, states = _hgrn_fwd(proj, hgrn_lb_logits, hgrn_out_gain, h_a)
    gb, token = gather_pass(gb, (y_a,))
    col_b = 4 * d_a // HEAD
    y_b = _attn_fwd(proj, q_gain, k_gain, bias, h_b, col_b)
    wg_pa, wg_pb, wg_out = gather_done(gb, (y_b,))
    pa = _matmul(y_a, wg_pa, name="proj_a", deps=(token,))
    pb = _matmul(y_b, wg_pb, name="proj_b")
    gate_off = 4 * d_a + 3 * d_b
    merged = _merge_fwd(proj, b_gate, pa, pb, gate_off)
    gc, token = gather_pass(gc, (merged,))
    x2 = _matmul(merged, wg_out, res=x0, name="out_proj", deps=(token,))
    h2, r2 = _rmsnorm_fwd(x2, norm_ffn, "rmsnorm_ffn")
    (wg_fin,) = gather_done(gc, (h2,))
    gd, token = gather_pass(gd, (h2,))
    gu = _matmul(h2, wg_fin, name="ffn_in", deps=(token,))
    act = _swiglu_fwd(gu)
    (wg_fout,) = gather_done(gd, (act,))
    y = _matmul(act, wg_fout, res=x2, name="ffn_out")
    dy, dy16, loss_part = _loss_head(y, target)

    g_fout = _matmul(act, dy16, ta=True, name="dw_ffn_out")
    r_fout, token = reduce_start("fout", ["w_ffn_out"], [g_fout], ())
    dact = _matmul(dy16, wg_fout, tb=True, name="d_act", deps=(token,))
    r_fout, token = reduce_pair_done(r_fout, (dact,))
    dgu = _swiglu_bwd(dact, gu)
    g_fin = _matmul(h2, dgu, ta=True, name="dw_ffn_in", deps=(token,))
    r_fin, token = reduce_start("fin", ["w_ffn_in"], [g_fin], ())
    dh2 = _matmul(dgu, wg_fin, tb=True, name="d_h2", deps=(token,))
    r_fout, token_a = reduce_chip_done(r_fout, (dh2,))
    r_fin, token_b = reduce_pair_done(r_fin, (dh2,))
    dx2, dx2_16, g_norm_ffn = _rmsnorm_bwd(dh2, x2, r2, norm_ffn, dy, "rmsnorm_ffn_bwd")
    dmerged = _matmul(dx2_16, wg_out, tb=True, name="d_merged", deps=(token_a, token_b))
    dp_ab, dproj, g_bgate = _merge_bwd(dmerged, proj, b_gate, pa, pb, gate_off)
    g_out = _matmul(merged, dx2_16, ta=True, name="dw_out")
    g_pa = _matmul(y_a, dp_ab[0], ta=True, name="dw_proj_a")
    g_pb = _matmul(y_b, dp_ab[1], ta=True, name="dw_proj_b")
    r_mid, token = reduce_start("mid", ["w_proj_a", "w_proj_b", "w_out"], [g_pa, g_pb, g_out], ())
    dy_a = _matmul(dp_ab[0], wg_pa, tb=True, name="d_y_a", deps=(token,))
    dy_b = _matmul(dp_ab[1], wg_pb, tb=True, name="d_y_b")
    reduce_finish(r_fout, (dy_b,))
    r_fin, token_a = reduce_chip_done(r_fin, (dy_b,))
    r_mid, token_b = reduce_pair_done(r_mid, (dy_b,))
    dproj, dbias, g_qg, g_kg = _attn_bwd(dproj, proj, q_gain, k_gain, bias, dy_b, h_b, col_b, deps=(token_a, token_b))
    r_mid, token = reduce_chip_done(r_mid, (dbias,))
    dproj, g_lb, g_gain = _hgrn_bwd(dproj, proj, o_pre, states, dy_a, hgrn_lb_logits, hgrn_out_gain, h_a, deps=(token,))
    g_in = _matmul(h1, dproj, ta=True, name="dw_in")
    r_in, token = reduce_start("in", ["w_in"], [g_in], ())
    g_rb = _relbias_reduce(dbias.transpose(1, 0, 2))[:, :N_REL]
    reduce_finish(r_mid, (token,))
    r_in, token = reduce_pair_done(r_in, (g_rb, upd["w_out"][0], upd["w_proj_a"][0], upd["w_proj_b"][0]))
    dh1 = _matmul(dproj, wg_in, tb=True, name="d_h1", deps=(token,))
    dx, _, g_norm_mix = _rmsnorm_bwd(dh1, x0, r1, norm_mix, dx2, "rmsnorm_mix_bwd")
    reduce_finish(r_fin, (dx,))
    r_in, token = reduce_chip_done(r_in, (upd["w_ffn_in"][0], upd["w_ffn_out"][0]))

    small_w = [b_gate, norm_mix, norm_ffn, hgrn_lb_logits, hgrn_out_gain, q_gain, k_gain, rel_bias]
    small_m = [m_b_gate, m_norm_mix, m_norm_ffn, m_hgrn_lb_logits, m_hgrn_out_gain, m_q_gain, m_k_gain, m_rel_bias]
    small_v = [v_b_gate, v_norm_mix, v_norm_ffn, v_hgrn_lb_logits, v_hgrn_out_gain, v_q_gain, v_k_gain, v_rel_bias]
    small_g = [g_bgate, g_norm_mix, g_norm_ffn, g_lb, g_gain, g_qg, g_kg, g_rb]
    sizes = [w.size for w in small_w]
    length = -(-(sum(sizes) + 1) // 128) * 128

    def pack(parts_):
        flat = jnp.concatenate([p.reshape(1, -1) for p in parts_], axis=1)
        return jnp.pad(flat, ((0, 0), (0, length - flat.shape[1])))

    one = jnp.ones((1, 1), F32)
    packed = _small_allreduce_adamw(pack(small_g + [loss_part]), pack(small_w + [one]), pack(small_m + [one]),
                                    pack(small_v + [one]), deps=(token,))

    def unpack(vec):
        out, at = [], 0
        for w, n in zip(small_w, sizes):
            out.append(vec[0, at:at + n].reshape(w.shape))
            at += n
        return out, vec[0, at]

    (sg, loss), (sd, _), (sm, _), (sv, _) = [unpack(p) for p in packed]
    reduce_finish(r_in, (packed[0],))

    def ordered(small, bigs):
        bigs = [bigs[nm][None] for nm in names]
        return [bigs[0]] + small + bigs[1:]

    return (loss, dx.reshape(x.shape), *ordered(sg, g_big), *ordered(sd, {nm: upd[nm][0] for nm in names}),
            *ordered(sm, {nm: upd[nm][1] for nm in names}), *ordered(sv, {nm: upd[nm][2] for nm in names}))
```

```python
import functools

import jax
import jax.numpy as jnp
from jax import lax
from jax.experimental import pallas as pl
from jax.experimental.pallas import tpu as pltpu

F32 = jnp.float32
BF16 = jnp.bfloat16
SDS = jax.ShapeDtypeStruct
MESH = pl.DeviceIdType.MESH
HIGHEST = lax.Precision.HIGHEST

CHUNK = 64
SUB = 16
HEAD = 128
N_PAST = 8
BAND = (N_PAST + 1) * CHUNK
PAD = N_PAST * CHUNK
REL_FUTURE = CHUNK - 1
REL_PAST = 2 * CHUNK - 1
N_REL = REL_FUTURE + REL_PAST + 1
REL_LANES = 256
EPS = 1e-6
MIX_HEADS = 2
MIX_UNROLL = 4
MIX_UNROLL_BWD = 4
ATT_UNROLL = 8
ATT_UNROLL_BWD = 4
EXP_CLAMP = 80.0

ADAM_LR = 0.001
ADAM_B1 = 0.9
ADAM_B2 = 0.999
ADAM_EPS = 1e-08
ADAM_WD = 0.01
ADAM_STEP = 10

VMEM_LIMIT = 56 * 1024 * 1024

HBM = pl.BlockSpec(memory_space=pltpu.HBM)
ANY = pl.BlockSpec(memory_space=pl.ANY)
SEM = pl.BlockSpec(memory_space=pltpu.SEMAPHORE)

NT = (((1,), (1,)), ((), ()))
TN = (((0,), (0,)), ((), ()))
NN = (((1,), (0,)), ((), ()))


def _params(sem=None, **kw):
    return pltpu.CompilerParams(dimension_semantics=sem, vmem_limit_bytes=VMEM_LIMIT, **kw)


def _tile(n, pref, unit=128):
    if n <= pref:
        return n
    t = pref - pref % unit
    while n % t:
        t -= unit
    return t


def _loop(n, unroll, step, init):
    assert n % unroll == 0, (n, unroll)

    def several(i, carry):
        for u in range(unroll):
            carry = step(i * unroll + u, carry)
        return carry

    return lax.fori_loop(0, n // unroll, several, init)


def _sigmoid(x):
    return 1.0 / (1.0 + jnp.exp(-x))


def _dsilu(x, s):
    return s * (1.0 + x * (1.0 - s))


def _bdot(a, b, dims=NN):
    return lax.dot_general(a.astype(BF16), b.astype(BF16), dims, preferred_element_type=F32)


def _split(a):
    hi = a.astype(BF16)
    return hi, (a - hi.astype(F32)).astype(BF16)


def _dot3(a, b, dims):
    dot = lambda u, v: lax.dot_general(u, v, dims, preferred_element_type=F32)
    return dot(a[0], b[1]) + dot(a[1], b[0]) + dot(a[0], b[0])


def _fdot(a, b):
    return lax.dot_general(a, b, NN, precision=HIGHEST, preferred_element_type=F32)


MM_TILE_K = 5632
MM_TILE_N = 512


def _matmul(a, b, *, ta=False, tb=False, res=None, out_dtype=F32, name, deps=()):
    m, k = (a.shape[1], a.shape[0]) if ta else a.shape
    n = b.shape[0] if tb else b.shape[1]
    tk = _tile(k, MM_TILE_K)
    nk = k // tk
    tm, tn = _tile(m, 2048 if tk <= MM_TILE_K // 2 else 1024), _tile(n, MM_TILE_N)
    dims = ((((0,) if ta else (1,)), ((1,) if tb else (0,))), ((), ()))

    def body(*refs):
        n_in = 2 + (res is not None)
        a_ref, b_ref = refs[:2]
        r_ref = refs[2] if res is not None else None
        o_ref = refs[n_in + len(deps)]
        part = lax.dot_general(a_ref[...].astype(BF16), b_ref[...].astype(BF16), dims, preferred_element_type=F32)

        def finish(out):
            if r_ref is not None:
                out = out + r_ref[...]
            o_ref[...] = out.astype(o_ref.dtype)

        if nk == 1:
            finish(part)
            return
        acc_ref = refs[-1]
        kk = pl.program_id(2)

        @pl.when(kk == 0)
        def _():
            acc_ref[...] = part

        @pl.when(jnp.logical_and(kk > 0, kk < nk - 1))
        def _():
            acc_ref[...] += part

        @pl.when(kk == nk - 1)
        def _():
            finish(acc_ref[...] + part)

    a_spec = pl.BlockSpec((tk, tm), lambda i, j, l: (l, i)) if ta else pl.BlockSpec((tm, tk), lambda i, j, l: (i, l))
    b_spec = pl.BlockSpec((tn, tk), lambda i, j, l: (j, l)) if tb else pl.BlockSpec((tk, tn), lambda i, j, l: (l, j))
    o_spec = pl.BlockSpec((tm, tn), lambda i, j, l: (i, j))
    in_specs = [a_spec, b_spec] + ([o_spec] if res is not None else []) + [ANY] * len(deps)
    args = (a, b) + ((res,) if res is not None else ()) + tuple(deps)
    return pl.pallas_call(
        body, grid=(m // tm, n // tn, nk), in_specs=in_specs, out_specs=o_spec,
        out_shape=SDS((m, n), out_dtype), scratch_shapes=[pltpu.VMEM((tm, tn), F32)] if nk > 1 else [],
        compiler_params=_params(("parallel", "parallel", "arbitrary")), name=name)(*args)


def _cast_into_full(w, kind, pos, name):
    r, n = w.shape
    tr = _tile(r, 512, 16)
    nr = r // tr
    if kind == "col":
        shape, o_spec = (r, 4 * n), pl.BlockSpec((tr, n), lambda i, p: (i, p[0]))
    else:
        shape, o_spec = (4 * r, n), pl.BlockSpec((tr, n), lambda i, p: (p[0] * nr + i, 0))

    def body(p_ref, w_ref, o_ref):
        o_ref[...] = w_ref[...].astype(BF16)

    return pl.pallas_call(
        body,
        grid_spec=pltpu.PrefetchScalarGridSpec(num_scalar_prefetch=1, grid=(nr,),
                                               in_specs=[pl.BlockSpec((tr, n), lambda i, p: (i, 0))], out_specs=o_spec),
        out_shape=SDS(shape, BF16), compiler_params=_params(("parallel",)), name=name)(pos, w)


def _rmsnorm_fwd(x, gain, name):
    t, d = x.shape
    tm = _tile(t, 256)

    def body(x_ref, g_ref, h_ref, r_ref):
        xv = x_ref[...]
        r = lax.rsqrt(jnp.mean(xv * xv, axis=-1, keepdims=True) + EPS)
        h_ref[...] = (xv * r * g_ref[...]).astype(BF16)
        r_ref[...] = r

    return pl.pallas_call(
        body, grid=(t // tm,),
        in_specs=[pl.BlockSpec((tm, d), lambda i: (i, 0)), pl.BlockSpec((1, d), lambda i: (0, 0))],
        out_specs=[pl.BlockSpec((tm, d), lambda i: (i, 0)), pl.BlockSpec((tm, 1), lambda i: (i, 0))],
        out_shape=[SDS((t, d), BF16), SDS((t, 1), F32)], compiler_params=_params(("parallel",)), name=name)(x, gain)


def _rmsnorm_bwd(dh, x, r, gain, dres, name):
    t, d = x.shape
    tm = _tile(t, 256)

    def body(dh_ref, x_ref, r_ref, g_ref, dres_ref, dx_ref, dxb_ref, dg_ref):
        @pl.when(pl.program_id(0) == 0)
        def _():
            dg_ref[...] = jnp.zeros_like(dg_ref)

        dhv, xv, rv = dh_ref[...], x_ref[...], r_ref[...]
        dg_ref[...] += jnp.sum(dhv * (xv * rv), axis=0, keepdims=True)
        u = dhv * g_ref[...]
        dx = dres_ref[...] + rv * u - xv * (rv * rv * rv) * jnp.mean(u * xv, axis=-1, keepdims=True)
        dx_ref[...] = dx
        dxb_ref[...] = dx.astype(BF16)

    row = pl.BlockSpec((tm, d), lambda i: (i, 0))
    vec = pl.BlockSpec((1, d), lambda i: (0, 0))
    return pl.pallas_call(
        body, grid=(t // tm,), in_specs=[row, row, pl.BlockSpec((tm, 1), lambda i: (i, 0)), vec, row],
        out_specs=[row, row, vec], out_shape=[SDS((t, d), F32), SDS((t, d), BF16), SDS((1, d), F32)],
        compiler_params=_params(("arbitrary",)), name=name)(dh, x, r, gain, dres)


def _merge_fwd(proj, b_gate, pa, pb, off):
    t, d = pa.shape
    tm, tc = _tile(t, 512), _tile(d, 512)
    nj = d // tc
    oa, ob = off // tc, off // tc + nj

    def body(la_ref, lb_ref, ba_ref, bb_ref, pa_ref, pb_ref, o_ref):
        ga = _sigmoid(la_ref[...] + ba_ref[...])
        gb = _sigmoid(lb_ref[...] + bb_ref[...])
        o_ref[...] = (ga * pa_ref[...] + gb * pb_ref[...]).astype(BF16)

    tile = pl.BlockSpec((tm, tc), lambda i, j: (i, j))
    return pl.pallas_call(
        body, grid=(t // tm, nj),
        in_specs=[pl.BlockSpec((tm, tc), lambda i, j: (i, oa + j)), pl.BlockSpec((tm, tc), lambda i, j: (i, ob + j)),
                  pl.BlockSpec((1, tc), lambda i, j: (0, j)), pl.BlockSpec((1, tc), lambda i, j: (0, nj + j)), tile, tile],
        out_specs=tile, out_shape=SDS((t, d), BF16), compiler_params=_params(("parallel", "parallel")),
        name="merge_fwd")(proj, proj, b_gate, b_gate, pa, pb)


def _merge_bwd(dmerged, proj, b_gate, pa, pb, off):
    t, d = pa.shape
    tm, tc = _tile(t, 512), _tile(d, 512)
    nj, ni = d // tc, t // tm
    o0 = off // tc

    def body(dm_ref, l_ref, b_ref, pa_ref, pb_ref, dp_ref, dl_ref, db_ref):
        s, i = pl.program_id(0), pl.program_id(2)
        p = jnp.where(s == 0, pa_ref[...], pb_ref[...])
        g = _sigmoid(l_ref[...] + b_ref[...])
        dm = dm_ref[...]
        dp_ref[0] = (dm * g).astype(BF16)
        dl = dm * p * g * (1.0 - g)
        dl_ref[...] = dl.astype(BF16)

        @pl.when(i == 0)
        def _():
            db_ref[...] = jnp.zeros_like(db_ref)

        db_ref[...] += jnp.sum(dl, axis=0, keepdims=True)

    tile = pl.BlockSpec((tm, tc), lambda s, j, i: (i, j))
    return pl.pallas_call(
        body, grid=(2, nj, ni),
        in_specs=[tile, pl.BlockSpec((tm, tc), lambda s, j, i: (i, o0 + s * nj + j)),
                  pl.BlockSpec((1, tc), lambda s, j, i: (0, s * nj + j)), tile, tile],
        out_specs=[pl.BlockSpec((1, tm, tc), lambda s, j, i: (s, i, j)),
                   pl.BlockSpec((tm, tc), lambda s, j, i: (i, o0 + s * nj + j)),
                   pl.BlockSpec((1, tc), lambda s, j, i: (0, s * nj + j))],
        out_shape=[SDS((2, t, d), BF16), SDS(proj.shape, BF16), SDS((1, 2 * d), F32)],
        compiler_params=_params(("arbitrary", "arbitrary", "arbitrary")),
        name="merge_bwd")(dmerged, proj, b_gate, pa, pb)


def _ffn_in_swiglu(h, w, deps=()):
    t, d = h.shape
    f = w.shape[1] // 2
    tm, tn = _tile(t, 2048), _tile(f, MM_TILE_N)
    nj = f // tn

    def body(h_ref, wg_ref, wu_ref, *rest):
        g_ref, u_ref, a_ref = rest[len(deps):]
        hv = h_ref[...]
        g = jnp.dot(hv, wg_ref[...], preferred_element_type=F32)
        u = jnp.dot(hv, wu_ref[...], preferred_element_type=F32)
        g_ref[...] = g
        u_ref[...] = u
        a_ref[...] = (g * _sigmoid(g) * u).astype(BF16)

    tile = pl.BlockSpec((tm, tn), lambda i, j: (i, j))
    return pl.pallas_call(
        body, grid=(t // tm, nj),
        in_specs=[pl.BlockSpec((tm, d), lambda i, j: (i, 0)), pl.BlockSpec((d, tn), lambda i, j: (0, j)),
                  pl.BlockSpec((d, tn), lambda i, j: (0, nj + j))] + [ANY] * len(deps),
        out_specs=[tile, tile, tile], out_shape=[SDS((t, f), F32), SDS((t, f), F32), SDS((t, f), BF16)],
        compiler_params=_params(("parallel", "parallel")), name="ffn_in_swiglu")(h, w, w, *deps)


def _swiglu_bwd(dact, gate, up):
    t, f = gate.shape
    tm = _tile(t, 128)

    def body(d_ref, g_ref, u_ref, o_ref):
        g, dv = g_ref[...], d_ref[...]
        sg = _sigmoid(g)
        o_ref[:, :f] = (dv * u_ref[...] * _dsilu(g, sg)).astype(BF16)
        o_ref[:, f:] = (dv * (g * sg)).astype(BF16)

    row = pl.BlockSpec((tm, f), lambda i: (i, 0))
    return pl.pallas_call(
        body, grid=(t // tm,), in_specs=[row, row, row],
        out_specs=pl.BlockSpec((tm, 2 * f), lambda i: (i, 0)), out_shape=SDS((t, 2 * f), BF16),
        compiler_params=_params(("parallel",)), name="swiglu_bwd")(dact, gate, up)


def _loss_head(y, target):
    t, d = y.shape
    tm = _tile(t, 256)

    def body(y_ref, t_ref, dy_ref, dyb_ref, l_ref):
        @pl.when(pl.program_id(0) == 0)
        def _():
            l_ref[...] = jnp.zeros_like(l_ref)

        e = y_ref[...] - t_ref[...]
        dy = e * (1.0 / d)
        dy_ref[...] = dy
        dyb_ref[...] = dy.astype(BF16)
        l_ref[...] += 0.5 * jnp.sum(jnp.mean(e * e, axis=-1, keepdims=True), axis=0, keepdims=True)

    row = pl.BlockSpec((tm, d), lambda i: (i, 0))
    return pl.pallas_call(
        body, grid=(t // tm,), in_specs=[row, row], out_specs=[row, row, pl.BlockSpec((1, 1), lambda i: (0, 0))],
        out_shape=[SDS((t, d), F32), SDS((t, d), BF16), SDS((1, 1), F32)], compiler_params=_params(("arbitrary",)),
        name="loss_head")(y, target)


def _rel_onehot(qi):
    p = lax.broadcasted_iota(jnp.int32, (REL_LANES, BAND), 1)
    r = lax.broadcasted_iota(jnp.int32, (REL_LANES, BAND), 0)
    idx = jnp.clip(qi + PAD - p, -REL_FUTURE, REL_PAST) + REL_FUTURE
    return (idx == r).astype(F32)


def _relbias_expand(rb):
    h = rb.shape[0]

    def body(rb_ref, o_ref):
        def step(qi, _):
            o_ref[qi] = _fdot(rb_ref[...], _rel_onehot(qi))
            return 0

        lax.fori_loop(0, CHUNK, step, 0)

    return pl.pallas_call(body, out_shape=SDS((CHUNK, h, BAND), F32), compiler_params=_params(),
                          name="relbias_expand")(rb)


def _relbias_reduce(dbias):
    h = dbias.shape[1]

    def body(db_ref, o_ref):
        def step(qi, acc):
            return acc + lax.dot_general(db_ref[qi], _rel_onehot(qi), NT, precision=HIGHEST,
                                         preferred_element_type=F32)

        o_ref[...] = lax.fori_loop(0, CHUNK, step, jnp.zeros((h, REL_LANES), F32))

    return pl.pallas_call(body, out_shape=SDS((h, REL_LANES), F32), compiler_params=_params(),
                          name="relbias_reduce")(dbias)


def _lower_bound(l_ref):
    l0, l1 = l_ref[0:1, :], l_ref[1:2, :]
    m = jnp.maximum(l0, l1)
    e0, e1 = jnp.exp(l0 - m), jnp.exp(l1 - m)
    return e0 / (e0 + e1)


def _tri(lower):
    r = lax.broadcasted_iota(jnp.int32, (CHUNK, CHUNK), 0)
    c = lax.broadcasted_iota(jnp.int32, (CHUNK, CHUNK), 1)
    return r >= c if lower else r <= c


def _hgrn_intra(qs, kk, b_s):
    rows = lax.broadcasted_iota(jnp.int32, (CHUNK, HEAD), 0)
    b = b_s[...]
    out = []
    for i in range(CHUNK // SUB):
        lo = i * SUB
        ref = jnp.zeros((1, HEAD), F32) if i == 0 else b_s[lo - 1:lo, :]
        eq = jnp.exp(b[lo:lo + SUB] - ref)
        qt = _split(qs[lo:lo + SUB] * eq)
        e = jnp.where(rows < lo + SUB, jnp.exp(jnp.minimum(ref - b, EXP_CLAMP)), 0.0)
        kt = _split(kk * e)
        out.append((eq, qt, e, kt))
    return out


def _hgrn_scores(blocks):
    tr = lax.broadcasted_iota(jnp.int32, (SUB, CHUNK), 0)
    tc = lax.broadcasted_iota(jnp.int32, (SUB, CHUNK), 1)
    return jnp.concatenate([jnp.where(tc <= tr + i * SUB, _dot3(qt, kt, NT), 0.0)
                            for i, (_, qt, _, kt) in enumerate(blocks)], axis=0)


def _hgrn_fwd(proj, lb_logits, gain, n_heads):
    t = proj.shape[0]
    nc = t // CHUNK
    da = n_heads * HEAD
    hp = MIX_HEADS
    wide = hp * HEAD

    def body(q_ref, f_ref, i_ref, g_ref, l_ref, gain_ref, y_ref, o_ref, st_ref, state, b_s):
        state[...] = jnp.zeros_like(state)
        lb_all = _lower_bound(l_ref)
        tril = _tri(True).astype(F32)

        def chunks(i, _):
            dot = functools.partial(lax.dot_general, preferred_element_type=F32)
            items = []
            for u in range(MIX_UNROLL):
                for hh in range(hp):
                    j = i * MIX_UNROLL + u
                    sl = pl.ds(pl.multiple_of(j * CHUNK, CHUNK), CHUNK)
                    cols = slice(hh * HEAD, (hh + 1) * HEAD)
                    lb = lb_all[:, cols]
                    fg = lb + (1.0 - lb) * _sigmoid(f_ref[sl, cols])
                    qv = q_ref[sl, cols]
                    gv = g_ref[sl, cols]
                    items.append(dict(hh=hh, j=j, sl=sl, cols=cols, lf=jnp.log(fg), kk=1.0 - fg, qs=qv * _sigmoid(qv),
                                      vb=i_ref[sl, cols].astype(BF16), gate=gv * _sigmoid(gv)))
            for it in items:
                it["b"] = _fdot(tril, it["lf"])
            for slot, it in enumerate(items):
                b = it["b"]
                b_s[slot] = b
                it["blocks"] = _hgrn_intra(it["qs"], it["kk"], b_s.at[slot])
                it["ebl"] = jnp.exp(b_s[slot, CHUNK - 1:CHUNK, :])
                it["qe"] = (it["qs"] * jnp.exp(b)).astype(BF16)
                it["kd"] = (it["kk"] * jnp.exp(b_s[slot, CHUNK - 1:CHUNK, :] - b)).astype(BF16)
            for it in items:
                it["a"] = _hgrn_scores(it["blocks"]).astype(BF16)
            for it in items:
                it["kv"] = dot(it["vb"], it["kd"], TN)
                it["o"] = dot(it["a"], it["vb"], NN)
            s_now = [state[hh] for hh in range(hp)]
            for it in items:
                it["s_in"] = s_now[it["hh"]]
                s_now[it["hh"]] = it["s_in"] * it["ebl"] + it["kv"]
            for hh in range(hp):
                state[hh] = s_now[hh]
            for it in items:
                it["o"] = it["o"] + dot(it["qe"], it["s_in"].astype(BF16), NT)
            for it in items:
                o, sl, cols = it["o"], it["sl"], it["cols"]
                st_ref[it["hh"], it["j"]] = it["s_in"]
                o_ref[sl, cols] = o
                rr = lax.rsqrt(jnp.mean(o * o, axis=-1, keepdims=True) + EPS)
                y_ref[sl, cols] = (o * rr * gain_ref[:, cols] * it["gate"]).astype(BF16)
            return 0

        assert nc % MIX_UNROLL == 0, (nc, MIX_UNROLL)
        lax.fori_loop(0, nc // MIX_UNROLL, chunks, 0)

    col = lambda k: pl.BlockSpec((t, wide), lambda h: (0, k * (n_heads // hp) + h))
    vec = pl.BlockSpec((1, wide), lambda h: (0, h))
    return pl.pallas_call(
        body, grid=(n_heads // hp,),
        in_specs=[col(0), col(1), col(2), col(3), pl.BlockSpec((2, wide), lambda h: (0, h)), vec],
        out_specs=[pl.BlockSpec((t, wide), lambda h: (0, h)), pl.BlockSpec((t, wide), lambda h: (0, h)),
                   pl.BlockSpec((hp, nc, HEAD, HEAD), lambda h: (h, 0, 0, 0))],
        out_shape=[SDS((t, da), BF16), SDS((t, da), F32), SDS((n_heads, nc, HEAD, HEAD), F32)],
        scratch_shapes=[pltpu.VMEM((hp, HEAD, HEAD), F32), pltpu.VMEM((hp * MIX_UNROLL, CHUNK, HEAD), F32)],
        compiler_params=_params(("parallel",)), name="hgrn_fwd")(proj, proj, proj, proj, lb_logits, gain)


def _hgrn_bwd(dproj, proj, o_pre, states, dy, lb_logits, gain, n_heads, deps=()):
    t = proj.shape[0]
    nc = t // CHUNK
    da = n_heads * HEAD
    hp = MIX_HEADS
    wide = hp * HEAD

    def body(*refs):
        (q_ref, f_ref, i_ref, g_ref, o_ref, st_ref, dy_ref, l_ref, gain_ref,
         dproj_ref, dl_ref, dgain_ref, res, dstate, b_s) = refs[1 + len(deps):]

        @pl.when(pl.program_id(1) == 0)
        def _():
            dstate[...] = jnp.zeros_like(dstate)
            lb_all = _lower_bound(l_ref)
            tril_m, tril, triu = _tri(True), _tri(True).astype(F32), _tri(False).astype(F32)
            last = lax.broadcasted_iota(jnp.int32, (CHUNK, HEAD), 0) == CHUNK - 1

            def chunks(i, carry):
                dot = functools.partial(lax.dot_general, preferred_element_type=F32)
                items = []
                for u in range(MIX_UNROLL_BWD):
                    for hh in range(hp):
                        j = nc - 1 - (i * MIX_UNROLL_BWD + u)
                        sl = pl.ds(pl.multiple_of(j * CHUNK, CHUNK), CHUNK)
                        cols = slice(hh * HEAD, (hh + 1) * HEAD)
                        lb, gain_v = lb_all[:, cols], gain_ref[:, cols]
                        sg = _sigmoid(f_ref[sl, cols])
                        fg = lb + (1.0 - lb) * sg
                        qv = q_ref[sl, cols]
                        sq = _sigmoid(qv)
                        gv = g_ref[sl, cols]
                        sgg = _sigmoid(gv)
                        silg = gv * sgg
                        o = o_ref[sl, cols]
                        dyv = dy_ref[sl, cols]
                        rr = lax.rsqrt(jnp.mean(o * o, axis=-1, keepdims=True) + EPS)
                        on = o * rr
                        don = dyv * gain_v * silg
                        do = (rr * don - o * (rr * rr * rr) * jnp.mean(don * o, axis=-1, keepdims=True)).astype(BF16)
                        items.append(dict(
                            hh=hh, j=j, sl=sl, cols=cols, lb=lb, sg=sg, fg=fg, kk=1.0 - fg, qv=qv, sq=sq, qs=qv * sq,
                            vb=i_ref[sl, cols].astype(BF16), do=do, dg=dyv * on * gain_v * _dsilu(gv, sgg),
                            dgain=jnp.sum(dyv * on * silg, axis=0, keepdims=True)))
                for it in items:
                    it["b"] = _fdot(tril, jnp.log(it["fg"]))
                for slot, it in enumerate(items):
                    b = it["b"]
                    b_s[slot] = b
                    it["blocks"] = _hgrn_intra(it["qs"], it["kk"], b_s.at[slot])
                    bl = b_s[slot, CHUNK - 1:CHUNK, :]
                    it["eb"], it["ebl"], it["ekd"] = jnp.exp(b), jnp.exp(bl), jnp.exp(bl - b)
                    it["s_in"] = st_ref[it["hh"], it["j"]]
                for it in items:
                    it["a"] = _hgrn_scores(it["blocks"]).astype(BF16)
                    it["da"] = jnp.where(tril_m, dot(it["do"], it["vb"], NT), 0.0)
                for it in items:
                    dq_rows = []
                    dk = jnp.zeros((CHUNK, HEAD), F32)
                    for blk, (eq, qt, e, kt) in enumerate(it["blocks"]):
                        da_i = _split(it["da"][blk * SUB:(blk + 1) * SUB])
                        dq_rows.append(eq * _dot3(da_i, kt, NN))
                        dk = dk + e * _dot3(da_i, qt, TN)
                    it["dq"] = jnp.concatenate(dq_rows, axis=0) + dot(it["do"], it["s_in"].astype(BF16), NN) * it["eb"]
                    it["dk"] = dk
                    it["dv"] = dot(it["a"], it["do"], TN)
                    it["g"] = dot(it["do"], (it["qs"] * it["eb"]).astype(BF16), TN)
                ds_now = [dstate[hh] for hh in range(hp)]
                for it in items:
                    it["ds_out"] = ds_now[it["hh"]]
                    ds_now[it["hh"]] = it["ds_out"] * it["ebl"] + it["g"]
                for hh in range(hp):
                    dstate[hh] = ds_now[hh]
                for it in items:
                    dsb = it["ds_out"].astype(BF16)
                    it["dv"] = it["dv"] + dot((it["kk"] * it["ekd"]).astype(BF16), dsb, NT)
                    it["dk_state"] = it["ekd"] * dot(it["vb"], dsb, NN)
                for it in items:
                    kk, dk_state = it["kk"], it["dk_state"]
                    it["dk"] = it["dk"] + dk_state
                    extra = (jnp.sum(kk * dk_state, axis=0, keepdims=True)
                             + it["ebl"] * jnp.sum(it["s_in"] * it["ds_out"], axis=0, keepdims=True))
                    it["db"] = it["qs"] * it["dq"] - kk * it["dk"] + jnp.where(last, extra, 0.0)
                for it in items:
                    it["dlf"] = _fdot(triu, it["db"])
                carry = list(carry)
                for it in items:
                    hh, sl, cols, sg, lb = it["hh"], it["sl"], it["cols"], it["sg"], it["lb"]
                    dfg = it["dlf"] / it["fg"] - it["dk"]
                    dlb_acc, dgain_acc = carry[hh]
                    carry[hh] = (dlb_acc + jnp.sum(dfg * (1.0 - sg), axis=0, keepdims=True), dgain_acc + it["dgain"])
                    res[0, sl, cols] = (it["dq"] * _dsilu(it["qv"], it["sq"])).astype(BF16)
                    res[1, sl, cols] = (dfg * (1.0 - lb) * sg * (1.0 - sg)).astype(BF16)
                    res[2, sl, cols] = it["dv"].astype(BF16)
                    res[3, sl, cols] = it["dg"].astype(BF16)
                return tuple(carry)

            assert nc % MIX_UNROLL_BWD == 0, (nc, MIX_UNROLL_BWD)
            zero = jnp.zeros((1, HEAD), F32)
            sums = lax.fori_loop(0, nc // MIX_UNROLL_BWD, chunks, ((zero, zero),) * hp)
            for hh, (dlb, dgain) in enumerate(sums):
                cols = slice(hh * HEAD, (hh + 1) * HEAD)
                lb = lb_all[:, cols]
                dgain_ref[:, cols] = dgain
                dl0 = dlb * lb * (1.0 - lb)
                dl_ref[0:1, cols] = dl0
                dl_ref[1:2, cols] = -dl0

        dproj_ref[...] = res[pl.program_id(1)]

    ng = n_heads // hp
    col = lambda k: pl.BlockSpec((t, wide), lambda h, p: (0, k * ng + h))
    head = pl.BlockSpec((t, wide), lambda h, p: (0, h))
    vec = pl.BlockSpec((1, wide), lambda h, p: (0, h))
    return pl.pallas_call(
        body, grid=(ng, 4),
        in_specs=[ANY] * (1 + len(deps)) + [col(0), col(1), col(2), col(3), head,
                  pl.BlockSpec((hp, nc, HEAD, HEAD), lambda h, p: (h, 0, 0, 0)),
                  head, pl.BlockSpec((2, wide), lambda h, p: (0, h)), vec],
        out_specs=[pl.BlockSpec((t, wide), lambda h, p: (0, p * ng + h)),
                   pl.BlockSpec((2, wide), lambda h, p: (0, h)), vec],
        out_shape=[SDS(dproj.shape, BF16), SDS((2, da), F32), SDS((1, da), F32)],
        scratch_shapes=[pltpu.VMEM((4, t, wide), BF16), pltpu.VMEM((hp, HEAD, HEAD), F32),
                        pltpu.VMEM((hp * MIX_UNROLL_BWD, CHUNK, HEAD), F32)],
        input_output_aliases={0: 0}, compiler_params=_params(("arbitrary", "arbitrary")),
        name="hgrn_bwd")(dproj, *deps, proj, proj, proj, proj, o_pre, states, dy, lb_logits, gain)


ROWS = 256


def _head_norm(x_ref, gain, dst, dst_off, t):
    def step(i, _):
        sl = pl.ds(pl.multiple_of(i * ROWS, ROWS), ROWS)
        xv = x_ref[sl, :]
        r = lax.rsqrt(jnp.mean(xv * xv, axis=-1, keepdims=True) + EPS)
        dst[pl.ds(pl.multiple_of(dst_off + i * ROWS, ROWS), ROWS), :] = (xv * r * gain).astype(BF16)
        return 0

    lax.fori_loop(0, t // ROWS, step, 0)


def _head_norm_bwd(x_ref, gain, dn_ref, dn_off, out, slot, t):
    def step(i, acc):
        sl = pl.ds(pl.multiple_of(i * ROWS, ROWS), ROWS)
        xv = x_ref[sl, :]
        dn = dn_ref[pl.ds(pl.multiple_of(dn_off + i * ROWS, ROWS), ROWS), :]
        r = lax.rsqrt(jnp.mean(xv * xv, axis=-1, keepdims=True) + EPS)
        u = dn * gain
        out[slot, sl, :] = r * u - xv * (r * r * r) * jnp.mean(u * xv, axis=-1, keepdims=True)
        return acc + jnp.sum(dn * (xv * r), axis=0, keepdims=True)

    return lax.fori_loop(0, t // ROWS, step, jnp.zeros((1, HEAD), F32))


def _attn_scores(qn, kpad, n):
    qc = qn[pl.ds(pl.multiple_of(n * CHUNK, CHUNK), CHUNK), :]
    band = pl.ds(pl.multiple_of(n * CHUNK, CHUNK), BAND)
    return qc, band, lax.dot_general(qc, kpad[band, :], NT, preferred_element_type=F32)


def _attn_softmax(raw, bias_ref, n):
    s = raw * (HEAD ** -0.5) + bias_ref[0]
    col = lax.broadcasted_iota(jnp.int32, (CHUNK, BAND), 1)
    s = jnp.where(col >= PAD - n * CHUNK, s, -jnp.inf)
    p = jnp.exp(s - jnp.max(s, axis=-1, keepdims=True))
    return p / jnp.sum(p, axis=-1, keepdims=True)


def _attn_fwd(proj, q_gain, k_gain, bias, n_heads, col0):
    t = proj.shape[0]
    nc = t // CHUNK

    def body(q_ref, k_ref, v_ref, qg_ref, kg_ref, bias_ref, y_ref, qn, kpad, vpad):
        kpad[0:PAD, :] = jnp.zeros((PAD, HEAD), BF16)
        vpad[0:PAD, :] = jnp.zeros((PAD, HEAD), BF16)
        _head_norm(q_ref, qg_ref[...], qn, 0, t)
        _head_norm(k_ref, kg_ref[...], kpad, PAD, t)

        def copy_v(i, _):
            vpad[pl.ds(pl.multiple_of(PAD + i * ROWS, ROWS), ROWS), :] = v_ref[
                pl.ds(pl.multiple_of(i * ROWS, ROWS), ROWS), :].astype(BF16)
            return 0

        lax.fori_loop(0, t // ROWS, copy_v, 0)

        def chunks(i, _):
            ns = [i * ATT_UNROLL + u for u in range(ATT_UNROLL)]
            scored = [_attn_scores(qn, kpad, n) for n in ns]
            probs = [_attn_softmax(raw, bias_ref, n).astype(BF16) for n, (_, _, raw) in zip(ns, scored)]
            outs = [lax.dot_general(p, vpad[band, :], NN, preferred_element_type=F32).astype(BF16)
                    for p, (_, band, _) in zip(probs, scored)]
            for n, o in zip(ns, outs):
                y_ref[pl.ds(pl.multiple_of(n * CHUNK, CHUNK), CHUNK), :] = o
            return 0

        assert nc % ATT_UNROLL == 0, (nc, ATT_UNROLL)
        lax.fori_loop(0, nc // ATT_UNROLL, chunks, 0)

    col = lambda k: pl.BlockSpec((t, HEAD), lambda h: (0, col0 + k * n_heads + h))
    vec = pl.BlockSpec((1, HEAD), lambda h: (0, 0))
    return pl.pallas_call(
        body, grid=(n_heads,),
        in_specs=[col(0), col(1), col(2), vec, vec, pl.BlockSpec((1, CHUNK, BAND), lambda h: (h, 0, 0))],
        out_specs=pl.BlockSpec((t, HEAD), lambda h: (0, h)), out_shape=SDS((t, n_heads * HEAD), BF16),
        scratch_shapes=[pltpu.VMEM((t, HEAD), BF16), pltpu.VMEM((t + PAD, HEAD), BF16), pltpu.VMEM((t + PAD, HEAD), BF16)],
        compiler_params=_params(("parallel",)), name="attn_fwd")(proj, proj, proj, q_gain, k_gain, bias)


def _attn_bwd(dproj, proj, q_gain, k_gain, bias, dy, n_heads, col0, deps=()):
    t = proj.shape[0]
    nc = t // CHUNK

    def body(*refs):
        (q_ref, k_ref, v_ref, qg_ref, kg_ref, bias_ref, dy_ref,
         dproj_ref, dbias_ref, dqg_ref, dkg_ref, qn, kpad, vpad, dqn, dk_acc, dv_acc, res) = refs[1 + len(deps):]
        h, part = pl.program_id(0), pl.program_id(1)

        @pl.when(part == 0)
        def _():
            kpad[0:PAD, :] = jnp.zeros((PAD, HEAD), BF16)
            vpad[0:PAD, :] = jnp.zeros((PAD, HEAD), BF16)
            _head_norm(q_ref, qg_ref[...], qn, 0, t)
            _head_norm(k_ref, kg_ref[...], kpad, PAD, t)

            def prep(i, _):
                sl = pl.ds(pl.multiple_of(PAD + i * ROWS, ROWS), ROWS)
                vpad[sl, :] = v_ref[pl.ds(pl.multiple_of(i * ROWS, ROWS), ROWS), :].astype(BF16)
                return 0

            lax.fori_loop(0, t // ROWS, prep, 0)

            def clear(i, _):
                sl = pl.ds(pl.multiple_of(i * ROWS, ROWS), ROWS)
                dk_acc[sl, :] = jnp.zeros((ROWS, HEAD), F32)
                dv_acc[sl, :] = jnp.zeros((ROWS, HEAD), F32)
                return 0

            lax.fori_loop(0, (t + PAD) // ROWS, clear, 0)
            dbias_ref[0] = jnp.zeros((CHUNK, BAND), F32)

            def chunks(i, _):
                dot = functools.partial(lax.dot_general, preferred_element_type=F32)
                ns = [i * ATT_UNROLL_BWD + u for u in range(ATT_UNROLL_BWD)]
                scored = [_attn_scores(qn, kpad, n) for n in ns]
                dos = [dy_ref[pl.ds(pl.multiple_of(n * CHUNK, CHUNK), CHUNK), :].astype(BF16) for n in ns]
                dps = [dot(do, vpad[band, :], NT) for do, (_, band, _) in zip(dos, scored)]
                ps, dss = [], []
                for n, (_, _, raw), dp in zip(ns, scored, dps):
                    p = _attn_softmax(raw, bias_ref, n)
                    ds = p * (dp - jnp.sum(dp * p, axis=-1, keepdims=True))
                    dbias_ref[0] += ds
                    ps.append(p.astype(BF16))
                    dss.append((ds * (HEAD ** -0.5)).astype(BF16))
                dqs = [dot(d, kpad[band, :], NN) for d, (_, band, _) in zip(dss, scored)]
                dks = [dot(d, qc, TN) for d, (qc, _, _) in zip(dss, scored)]
                dvs = [dot(p, do, TN) for p, do in zip(ps, dos)]
                for n, (_, band, _), dq, dk, dv in zip(ns, scored, dqs, dks, dvs):
                    dqn[pl.ds(pl.multiple_of(n * CHUNK, CHUNK), CHUNK), :] = dq
                    dk_acc[band, :] += dk
                    dv_acc[band, :] += dv
                return 0

            assert nc % ATT_UNROLL_BWD == 0, (nc, ATT_UNROLL_BWD)
            lax.fori_loop(0, nc // ATT_UNROLL_BWD, chunks, 0)
            dqg = _head_norm_bwd(q_ref, qg_ref[...], dqn, 0, res, 0, t)
            dkg = _head_norm_bwd(k_ref, kg_ref[...], dk_acc, PAD, res, 1, t)

            def put_v(i, _):
                sl = pl.ds(pl.multiple_of(i * ROWS, ROWS), ROWS)
                res[2, sl, :] = dv_acc[pl.ds(pl.multiple_of(PAD + i * ROWS, ROWS), ROWS), :]
                return 0

            lax.fori_loop(0, t // ROWS, put_v, 0)

            @pl.when(h == 0)
            def _():
                dqg_ref[...] = jnp.zeros_like(dqg_ref)
                dkg_ref[...] = jnp.zeros_like(dkg_ref)

            dqg_ref[...] += dqg
            dkg_ref[...] += dkg

        dproj_ref[...] = res[part].astype(BF16)

    col = lambda k: pl.BlockSpec((t, HEAD), lambda h, p: (0, col0 + k * n_heads + h))
    vec = pl.BlockSpec((1, HEAD), lambda h, p: (0, 0))
    btile = pl.BlockSpec((1, CHUNK, BAND), lambda h, p: (h, 0, 0))
    return pl.pallas_call(
        body, grid=(n_heads, 3),
        in_specs=[ANY] * (1 + len(deps)) + [col(0), col(1), col(2), vec, vec, btile,
                                            pl.BlockSpec((t, HEAD), lambda h, p: (0, h))],
        out_specs=[pl.BlockSpec((t, HEAD), lambda h, p: (0, col0 + p * n_heads + h)), btile, vec, vec],
        out_shape=[SDS(dproj.shape, BF16), SDS((n_heads, CHUNK, BAND), F32), SDS((1, HEAD), F32), SDS((1, HEAD), F32)],
        scratch_shapes=[pltpu.VMEM((t, HEAD), BF16), pltpu.VMEM((t + PAD, HEAD), BF16), pltpu.VMEM((t + PAD, HEAD), BF16),
                        pltpu.VMEM((t, HEAD), F32), pltpu.VMEM((t + PAD, HEAD), F32), pltpu.VMEM((t + PAD, HEAD), F32),
                        pltpu.VMEM((3, t, HEAD), F32)],
        input_output_aliases={0: 0}, compiler_params=_params(("arbitrary", "arbitrary")),
        name="attn_bwd")(dproj, *deps, proj, proj, proj, q_gain, k_gain, bias, dy)


def _place():
    x, y, c = lax.axis_index("x"), lax.axis_index("y"), lax.axis_index("c")
    others = [(1 - x, y), (x, 1 - y), (1 - x, 1 - y)]
    return x, y, c, others


def _chunk_of(ref, kind, chip, half, shard_shape):
    r, n = shard_shape
    hr = r // 2
    if kind == "col":
        rows = pl.ds(0, r) if half is None else pl.ds(half * hr, hr)
        return ref.at[rows, pl.ds(chip * n, n)]
    rows = pl.ds(chip * r, r) if half is None else pl.ds(chip * r + half * hr, hr)
    return ref.at[rows, :]


EFFECT = pltpu.SideEffectType.DATAFLOW_SIDE_EFFECTING


def _start_copies(name, bufs, plan, n, deps):
    nb, nd = len(bufs), len(deps)

    def body(*refs):
        send, recv, token = refs[nb + nd], refs[nb + nd + 1], refs[-1]
        for cp in plan(refs[:nb], send, recv)[0]:
            cp.start()
        token[...] = jnp.zeros_like(token)

    out = pl.pallas_call(
        body, name=name,
        out_shape=(pltpu.SemaphoreType.DMA((n,)), pltpu.SemaphoreType.DMA((n,)),
                   *[pltpu.HBM(b.shape, b.dtype) for b in bufs], SDS((8, 128), F32)),
        in_specs=[HBM] * nb + [ANY] * nd,
        out_specs=(SEM, SEM, *[HBM] * nb, pl.BlockSpec(memory_space=pltpu.VMEM)),
        input_output_aliases={i: 2 + i for i in range(nb)},
        compiler_params=pltpu.CompilerParams(has_side_effects=EFFECT),
    )(*[pltpu.with_memory_space_constraint(b, pltpu.HBM) for b in bufs], *deps)
    return out[0], out[1], list(out[2:2 + nb]), out[-1]


def _wait_copies(name, bufs, send, recv, plan, after):
    nb = len(bufs)

    def body(*refs):
        sends, recvs = plan(refs[:nb], refs[nb], refs[nb + 1])
        for cp in sends:
            cp.wait_send()
        for cp in recvs:
            cp.wait_recv()

    out = pl.pallas_call(
        body, name=name, out_shape=tuple(pltpu.HBM(b.shape, b.dtype) for b in bufs),
        in_specs=[HBM] * nb + [SEM, SEM] + [ANY] * len(after), out_specs=tuple([HBM] * nb),
        input_output_aliases={i: i for i in range(nb)},
        compiler_params=pltpu.CompilerParams(has_side_effects=EFFECT),
    )(*bufs, send, recv, *after)
    return list(out)


def _remote(src, dst, send, recv, i, dev):
    return pltpu.make_async_remote_copy(src_ref=src, dst_ref=dst, send_sem=send.at[i], recv_sem=recv.at[i],
                                        device_id=dev, device_id_type=MESH)


def _plan_gather_ici(kinds, shapes):
    def plan(refs, send, recv):
        x, y, c, others = _place()
        sends, recvs = [], []
        for w, (kind, ss) in enumerate(zip(kinds, shapes)):
            for p, (px, py) in enumerate(others):
                mine = _chunk_of(refs[w], kind, 2 * x + y, c, ss)
                theirs = _chunk_of(refs[w], kind, 2 * px + py, c, ss)
                sends.append(_remote(mine, mine, send, recv, 3 * w + p, (px, py, c)))
                recvs.append(_remote(theirs, theirs, send, recv, 3 * w + p, (px, py, c)))
        return sends, recvs

    return plan, 3 * len(kinds)


def _plan_gather_pass(kinds, shapes):
    def plan(refs, send, recv):
        x, y, c, others = _place()
        sends, recvs = [], []
        for w, (kind, ss) in enumerate(zip(kinds, shapes)):
            for p, (px, py) in enumerate(others):
                got = _chunk_of(refs[w], kind, 2 * px + py, c, ss)
                coming = _chunk_of(refs[w], kind, 2 * px + py, 1 - c, ss)
                sends.append(_remote(got, got, send, recv, 3 * w + p, (x, y, 1 - c)))
                recvs.append(_remote(coming, coming, send, recv, 3 * w + p, (x, y, 1 - c)))
        return sends, recvs

    return plan, 3 * len(kinds)


def _plan_pair(kinds, shapes):
    nw = len(kinds)

    def plan(refs, send, recv):
        x, y, c, _ = _place()
        sends = []
        for w, (kind, ss) in enumerate(zip(kinds, shapes)):
            for k in range(4):
                sends.append(_remote(_chunk_of(refs[w], kind, k, 1 - c, ss), refs[nw + w].at[k], send, recv,
                                     4 * w + k, (x, y, 1 - c)))
        return sends, sends

    return plan, 4 * nw


def _plan_chip(nw):
    def plan(refs, send, recv):
        x, y, c, others = _place()
        sends = []
        for w in range(nw):
            for p, (px, py) in enumerate(others):
                sends.append(_remote(refs[w].at[2 * px + py], refs[nw + w].at[p], send, recv, 3 * w + p, (px, py, c)))
        return sends, sends

    return plan, 3 * nw


def _plan_share(nw):
    def plan(refs, send, recv):
        x, y, c, _ = _place()
        sends = [_remote(refs[w].at[c], refs[w].at[c], send, recv, w, (x, y, 1 - c)) for w in range(nw)]
        recvs = [_remote(refs[w].at[1 - c], refs[w].at[1 - c], send, recv, w, (x, y, 1 - c)) for w in range(nw)]
        return sends, recvs

    return plan, nw


def _pair_add(grad, got, kind, shard_shape, pos, name):
    r, n = shard_shape
    hr = r // 2
    tr, tn = _tile(hr, 256, 16), _tile(n, 1408)
    nr, nn = hr // tr, n // tn
    if kind == "col":
        g_spec = pl.BlockSpec((tr, tn), lambda k, i, j, c: (c[1] * nr + i, k * nn + j))
    else:
        g_spec = pl.BlockSpec((tr, tn), lambda k, i, j, c: ((2 * k + c[1]) * nr + i, j))
    o_spec = pl.BlockSpec((1, tr, tn), lambda k, i, j, c: (k, i, j))

    def body(c_ref, g_ref, r_ref, o32_ref, o16_ref):
        s = g_ref[...] + r_ref[0]
        o32_ref[0] = s
        o16_ref[0] = s.astype(BF16)

    return pl.pallas_call(
        body,
        grid_spec=pltpu.PrefetchScalarGridSpec(num_scalar_prefetch=1, grid=(4, nr, nn), in_specs=[g_spec, o_spec],
                                               out_specs=[o_spec, o_spec]),
        out_shape=[SDS((4, hr, n), F32), SDS((4, hr, n), BF16)],
        compiler_params=_params(("parallel", "parallel", "parallel")), name=name)(pos, grad, got)


def _chip_add(part32, got16, pos, name):
    _, hr, n = part32.shape
    tr, tn = _tile(hr, 256, 16), _tile(n, 1408)
    own = pl.BlockSpec((1, tr, tn), lambda i, j, p: (p[0], i, j))
    oth = pl.BlockSpec((3, tr, tn), lambda i, j, p: (0, i, j))

    def body(p_ref, own_ref, oth_ref, o_ref):
        o_ref[0] = ((own_ref[0] + oth_ref[0].astype(F32)) + oth_ref[1].astype(F32)) + oth_ref[2].astype(F32)

    return pl.pallas_call(
        body,
        grid_spec=pltpu.PrefetchScalarGridSpec(num_scalar_prefetch=1, grid=(hr // tr, n // tn), in_specs=[own, oth],
                                               out_specs=pl.BlockSpec((1, tr, tn), lambda i, j, p: (p[1], i, j))),
        out_shape=SDS((2, hr, n), F32), compiler_params=_params(("parallel", "parallel")), name=name)(pos, part32, got16)


def _adamw_math(w, g, m, v):
    m = ADAM_B1 * m + (1.0 - ADAM_B1) * g
    v = ADAM_B2 * v + (1.0 - ADAM_B2) * (g * g)
    m_hat = m / (1.0 - ADAM_B1 ** ADAM_STEP)
    v_hat = v / (1.0 - ADAM_B2 ** ADAM_STEP)
    return -ADAM_LR * (m_hat / (jnp.sqrt(v_hat) + ADAM_EPS) + ADAM_WD * w), m, v


def _adamw(w, g, m, v, name):
    r, n = w.shape
    tr, tn = _tile(r, 256, 16), _tile(n, 1408)

    def body(w_ref, g_ref, m_ref, v_ref, d_ref, nm_ref, nv_ref, go_ref):
        gv = g_ref[...]
        d_ref[...], nm_ref[...], nv_ref[...] = _adamw_math(w_ref[...], gv, m_ref[...], v_ref[...])
        go_ref[...] = gv

    tile = pl.BlockSpec((tr, tn), lambda i, j: (i, j))
    return pl.pallas_call(
        body, grid=(r // tr, n // tn), in_specs=[tile] * 4, out_specs=[tile] * 4, out_shape=[SDS((r, n), F32)] * 4,
        compiler_params=_params(("parallel", "parallel")), name=name)(w, g, m, v)


def _small_allreduce_adamw(g, w, m, v, deps=()):
    length = g.shape[1]

    def body(*refs):
        g_ref, w_ref, m_ref, v_ref = refs[:4]
        gs_ref, d_ref, nm_ref, nv_ref, buf, send, recv = refs[4 + len(deps):]
        x, y, c = lax.axis_index("x"), lax.axis_index("y"), lax.axis_index("c")
        me = 4 * x + 2 * y + c
        buf[me] = g_ref[...]
        cps = []
        for d in range(1, 8):
            peer = (x ^ (d >> 2), y ^ ((d >> 1) & 1), c ^ (d & 1))
            cp = pltpu.make_async_remote_copy(src_ref=buf.at[me], dst_ref=buf.at[me], send_sem=send.at[d - 1],
                                              recv_sem=recv.at[d - 1], device_id=peer, device_id_type=MESH)
            cp.start()
            cps.append(cp)
        for cp in cps:
            cp.wait()
        total = buf[0]
        for d in range(1, 8):
            total = total + buf[d]
        gs_ref[...] = total
        d_ref[...], nm_ref[...], nv_ref[...] = _adamw_math(w_ref[...], total, m_ref[...], v_ref[...])

    vm = pl.BlockSpec(memory_space=pltpu.VMEM)
    return pl.pallas_call(
        body, in_specs=[vm] * 4 + [ANY] * len(deps), out_specs=[vm] * 4, out_shape=[SDS((1, length), F32)] * 4,
        scratch_shapes=[pltpu.VMEM((8, 1, length), F32), pltpu.SemaphoreType.DMA((7,)), pltpu.SemaphoreType.DMA((7,))],
        compiler_params=pltpu.CompilerParams(has_side_effects=True), name="small_allreduce_adamw")(g, w, m, v, *deps)


def kernel(x, w_in, b_gate, norm_mix, norm_ffn, hgrn_lb_logits, hgrn_out_gain, q_gain, k_gain, rel_bias, w_proj_a, w_proj_b, w_out, w_ffn_in, w_ffn_out, loss_target, m_w_in, m_b_gate, m_norm_mix, m_norm_ffn, m_hgrn_lb_logits, m_hgrn_out_gain, m_q_gain, m_k_gain, m_rel_bias, m_w_proj_a, m_w_proj_b, m_w_out, m_w_ffn_in, m_w_ffn_out, v_w_in, v_b_gate, v_norm_mix, v_norm_ffn, v_hgrn_lb_logits, v_hgrn_out_gain, v_q_gain, v_k_gain, v_rel_bias, v_w_proj_a, v_w_proj_b, v_w_out, v_w_ffn_in, v_w_ffn_out):
    t, d = x.shape[1], x.shape[2]
    d_a = hgrn_out_gain.shape[1]
    h_a = d_a // HEAD
    h_b = rel_bias.shape[1]
    d_b = h_b * HEAD
    x0 = x.reshape(t, d)
    target = loss_target.reshape(t, d)
    pos = jnp.stack([2 * lax.axis_index("x") + lax.axis_index("y"), lax.axis_index("c")]).astype(jnp.int32)

    names = ["w_in", "w_proj_a", "w_proj_b", "w_out", "w_ffn_in", "w_ffn_out"]
    big = dict(zip(names, [w_in[0], w_proj_a[0], w_proj_b[0], w_out[0], w_ffn_in[0], w_ffn_out[0]]))
    big_m = dict(zip(names, [m_w_in[0], m_w_proj_a[0], m_w_proj_b[0], m_w_out[0], m_w_ffn_in[0], m_w_ffn_out[0]]))
    big_v = dict(zip(names, [v_w_in[0], v_w_proj_a[0], v_w_proj_b[0], v_w_out[0], v_w_ffn_in[0], v_w_ffn_out[0]]))
    kind = dict(zip(names, ["col", "col", "col", "row", "col", "row"]))
    shape = {nm: big[nm].shape for nm in names}

    def gather_start(tag, group, deps):
        plan, n = _plan_gather_ici([kind[g] for g in group], [shape[g] for g in group])
        fulls = [_cast_into_full(big[g], kind[g], pos, "cast_" + g) for g in group]
        send, recv, bufs, token = _start_copies("gather_ici_start_" + tag, fulls, plan, n, deps)
        return (tag, group, plan, send, recv, bufs), token

    def gather_pass(state, after):
        tag, group, plan, send, recv, bufs = state
        bufs = _wait_copies("gather_ici_wait_" + tag, bufs, send, recv, plan, after)
        plan, n = _plan_gather_pass([kind[g] for g in group], [shape[g] for g in group])
        send, recv, bufs, token = _start_copies("gather_pass_start_" + tag, bufs, plan, n, ())
        return (tag, group, plan, send, recv, bufs), token

    def gather_done(state, after):
        tag, group, plan, send, recv, bufs = state
        return _wait_copies("gather_pass_wait_" + tag, bufs, send, recv, plan, after)

    def reduce_start(tag, group, grads, deps):
        plan, n = _plan_pair([kind[g] for g in group], [shape[g] for g in group])
        lands = [lax.empty((4, shape[g][0] // 2, shape[g][1]), F32) for g in group]
        send, recv, bufs, token = _start_copies("pair_start_" + tag, list(grads) + lands, plan, n, deps)
        return dict(tag=tag, group=group, plan=plan, send=send, recv=recv, bufs=bufs), token

    def reduce_pair_done(st, after):
        tag, group, nw = st["tag"], st["group"], len(st["group"])
        bufs = _wait_copies("pair_wait_" + tag, st["bufs"], st["send"], st["recv"], st["plan"], after)
        parts = [_pair_add(g, l, kind[nm], shape[nm], pos, "pair_add_" + nm)
                 for g, l, nm in zip(bufs[:nw], bufs[nw:], group)]
        lands = [lax.empty((3, shape[g][0] // 2, shape[g][1]), BF16) for g in group]
        plan, n = _plan_chip(nw)
        send, recv, bufs, token = _start_copies("chip_start_" + tag, [p16 for _, p16 in parts] + lands, plan, n, ())
        return dict(st, plan=plan, send=send, recv=recv, bufs=bufs, p32=[p32 for p32, _ in parts]), token

    def reduce_chip_done(st, after):
        tag, group, nw = st["tag"], st["group"], len(st["group"])
        bufs = _wait_copies("chip_wait_" + tag, st["bufs"], st["send"], st["recv"], st["plan"], after)
        finals = [_chip_add(p32, got, pos, "chip_add_" + nm) for p32, got, nm in zip(st["p32"], bufs[nw:], group)]
        plan, n = _plan_share(nw)
        send, recv, bufs, token = _start_copies("share_start_" + tag, finals, plan, n, ())
        return dict(st, plan=plan, send=send, recv=recv, bufs=bufs), token

    g_big, upd = {}, {}

    def reduce_finish(st, after):
        bufs = _wait_copies("share_wait_" + st["tag"], st["bufs"], st["send"], st["recv"], st["plan"], after)
        for full, nm in zip(bufs, st["group"]):
            upd[nm] = _adamw(big[nm], full.reshape(shape[nm]), big_m[nm], big_v[nm], "adamw_" + nm)
            g_big[nm] = upd[nm][3]

    ga, token = gather_start("a", ["w_in"], ())
    gb, token = gather_start("b", ["w_proj_a", "w_proj_b", "w_out"], (token,))
    gc, token = gather_start("c", ["w_ffn_in"], (token,))
    gd, token = gather_start("d", ["w_ffn_out"], (token,))
    h1, r1 = _rmsnorm_fwd(x0, norm_mix, "rmsnorm_mix")
    rb = jnp.pad(rel_bias[0], ((0, 0), (0, REL_LANES - N_REL)))
    bias = _relbias_expand(rb).transpose(1, 0, 2)
    ga, token = gather_pass(ga, (h1, bias, token))
    (wg_in,) = gather_done(ga, ())
    proj = _matmul(h1, wg_in, name="proj_in")
    y_a, o_pre, states = _hgrn_fwd(proj, hgrn_lb_logits, hgrn_out_gain, h_a)
    gb, token = gather_pass(gb, (y_a,))
    col_b = 4 * d_a // HEAD
    y_b = _attn_fwd(proj, q_gain, k_gain, bias, h_b, col_b)
    wg_pa, wg_pb, wg_out = gather_done(gb, (y_b,))
    pa = _matmul(y_a, wg_pa, name="proj_a", deps=(token,))
    pb = _matmul(y_b, wg_pb, name="proj_b")
    gate_off = 4 * d_a + 3 * d_b
    merged = _merge_fwd(proj, b_gate, pa, pb, gate_off)
    x2 = _matmul(merged, wg_out, res=x0, name="out_proj")
    gc, token = gather_pass(gc, (x2,))
    h2, r2 = _rmsnorm_fwd(x2, norm_ffn, "rmsnorm_ffn")
    (wg_fin,) = gather_done(gc, (h2,))
    ff_gate, ff_up, act = _ffn_in_swiglu(h2, wg_fin, deps=(token,))
    gd, token = gather_pass(gd, (act,))
    (wg_fout,) = gather_done(gd, ())
    y = _matmul(act, wg_fout, res=x2, name="ffn_out")
    dy, dy16, loss_part = _loss_head(y, target)

    g_fout = _matmul(act, dy16, ta=True, name="dw_ffn_out")
    r_fout, token = reduce_start("fout", ["w_ffn_out"], [g_fout], ())
    dact = _matmul(dy16, wg_fout, tb=True, name="d_act", deps=(token,))
    r_fout, token = reduce_pair_done(r_fout, (dact,))
    dgu = _swiglu_bwd(dact, ff_gate, ff_up)
    g_fin = _matmul(h2, dgu, ta=True, name="dw_ffn_in", deps=(token,))
    r_fin, token = reduce_start("fin", ["w_ffn_in"], [g_fin], ())
    dh2 = _matmul(dgu, wg_fin, tb=True, name="d_h2", deps=(token,))
    r_fout, token_a = reduce_chip_done(r_fout, (dh2,))
    r_fin, token_b = reduce_pair_done(r_fin, (dh2,))
    dx2, dx2_16, g_norm_ffn = _rmsnorm_bwd(dh2, x2, r2, norm_ffn, dy, "rmsnorm_ffn_bwd")
    dmerged = _matmul(dx2_16, wg_out, tb=True, name="d_merged", deps=(token_a, token_b))
    dp_ab, dproj, g_bgate = _merge_bwd(dmerged, proj, b_gate, pa, pb, gate_off)
    g_out = _matmul(merged, dx2_16, ta=True, name="dw_out")
    g_pa = _matmul(y_a, dp_ab[0], ta=True, name="dw_proj_a")
    g_pb = _matmul(y_b, dp_ab[1], ta=True, name="dw_proj_b")
    r_mid, token = reduce_start("mid", ["w_proj_a", "w_proj_b", "w_out"], [g_pa, g_pb, g_out], ())
    dy_a = _matmul(dp_ab[0], wg_pa, tb=True, name="d_y_a", deps=(token,))
    dy_b = _matmul(dp_ab[1], wg_pb, tb=True, name="d_y_b")
    r_fin, token_a = reduce_chip_done(r_fin, (dy_b,))
    r_mid, token_b = reduce_pair_done(r_mid, (dy_b,))
    dproj, dbias, g_qg, g_kg = _attn_bwd(dproj, proj, q_gain, k_gain, bias, dy_b, h_b, col_b, deps=(token_a, token_b))
    r_mid, token = reduce_chip_done(r_mid, (dbias,))
    dproj, g_lb, g_gain = _hgrn_bwd(dproj, proj, o_pre, states, dy_a, hgrn_lb_logits, hgrn_out_gain, h_a, deps=(token,))
    g_in = _matmul(h1, dproj, ta=True, name="dw_in")
    r_in, token = reduce_start("in", ["w_in"], [g_in], ())
    g_rb = _relbias_reduce(dbias.transpose(1, 0, 2))[:, :N_REL]
    reduce_finish(r_mid, (token,))
    r_in, token = reduce_pair_done(r_in, (g_rb, upd["w_out"][0], upd["w_proj_a"][0], upd["w_proj_b"][0]))
    dh1 = _matmul(dproj, wg_in, tb=True, name="d_h1", deps=(token,))
    dx, _, g_norm_mix = _rmsnorm_bwd(dh1, x0, r1, norm_mix, dx2, "rmsnorm_mix_bwd")
    reduce_finish(r_fin, (dx,))
    reduce_finish(r_fout, (dx,))
    r_in, token = reduce_chip_done(r_in, (upd["w_ffn_in"][0], upd["w_ffn_out"][0]))

    small_w = [b_gate, norm_mix, norm_ffn, hgrn_lb_logits, hgrn_out_gain, q_gain, k_gain, rel_bias]
    small_m = [m_b_gate, m_norm_mix, m_norm_ffn, m_hgrn_lb_logits, m_hgrn_out_gain, m_q_gain, m_k_gain, m_rel_bias]
    small_v = [v_b_gate, v_norm_mix, v_norm_ffn, v_hgrn_lb_logits, v_hgrn_out_gain, v_q_gain, v_k_gain, v_rel_bias]
    small_g = [g_bgate, g_norm_mix, g_norm_ffn, g_lb, g_gain, g_qg, g_kg, g_rb]
    sizes = [w.size for w in small_w]
    length = -(-(sum(sizes) + 1) // 128) * 128

    def pack(parts_):
        flat = jnp.concatenate([p.reshape(1, -1) for p in parts_], axis=1)
        return jnp.pad(flat, ((0, 0), (0, length - flat.shape[1])))

    one = jnp.ones((1, 1), F32)
    packed = _small_allreduce_adamw(pack(small_g + [loss_part]), pack(small_w + [one]), pack(small_m + [one]),
                                    pack(small_v + [one]), deps=(token,))

    def unpack(vec):
        out, at = [], 0
        for w, n in zip(small_w, sizes):
            out.append(vec[0, at:at + n].reshape(w.shape))
            at += n
        return out, vec[0, at]

    (sg, loss), (sd, _), (sm, _), (sv, _) = [unpack(p) for p in packed]
    reduce_finish(r_in, (packed[0],))

    def ordered(small, bigs):
        bigs = [bigs[nm][None] for nm in names]
        return [bigs[0]] + small + bigs[1:]

    return (loss, dx.reshape(x.shape), *ordered(sg, g_big), *ordered(sd, {nm: upd[nm][0] for nm in names}),
            *ordered(sm, {nm: upd[nm][1] for nm in names}), *ordered(sv, {nm: upd[nm][2] for nm in names}))
```

```python
import functools

import jax
import jax.numpy as jnp
from jax import lax
from jax.experimental import pallas as pl
from jax.experimental.pallas import tpu as pltpu

F32 = jnp.float32
BF16 = jnp.bfloat16
SDS = jax.ShapeDtypeStruct
MESH = pl.DeviceIdType.MESH
HIGHEST = lax.Precision.HIGHEST

CHUNK = 64
SUB = 16
HEAD = 128
N_PAST = 8
BAND = (N_PAST + 1) * CHUNK
PAD = N_PAST * CHUNK
REL_FUTURE = CHUNK - 1
REL_PAST = 2 * CHUNK - 1
N_REL = REL_FUTURE + REL_PAST + 1
REL_LANES = 256
EPS = 1e-6
MIX_HEADS = 2
MIX_UNROLL = 4
MIX_UNROLL_BWD = 4
ATT_UNROLL = 8
ATT_UNROLL_BWD = 4
EXP_CLAMP = 80.0

ADAM_LR = 0.001
ADAM_B1 = 0.9
ADAM_B2 = 0.999
ADAM_EPS = 1e-08
ADAM_WD = 0.01
ADAM_STEP = 10

VMEM_LIMIT = 56 * 1024 * 1024

HBM = pl.BlockSpec(memory_space=pltpu.HBM)
ANY = pl.BlockSpec(memory_space=pl.ANY)
SEM = pl.BlockSpec(memory_space=pltpu.SEMAPHORE)

NT = (((1,), (1,)), ((), ()))
TN = (((0,), (0,)), ((), ()))
NN = (((1,), (0,)), ((), ()))


def _params(sem=None, **kw):
    return pltpu.CompilerParams(dimension_semantics=sem, vmem_limit_bytes=VMEM_LIMIT, **kw)


def _tile(n, pref, unit=128):
    if n <= pref:
        return n
    t = pref - pref % unit
    while n % t:
        t -= unit
    return t


def _loop(n, unroll, step, init):
    assert n % unroll == 0, (n, unroll)

    def several(i, carry):
        for u in range(unroll):
            carry = step(i * unroll + u, carry)
        return carry

    return lax.fori_loop(0, n // unroll, several, init)


def _sigmoid(x):
    return 1.0 / (1.0 + jnp.exp(-x))


def _dsilu(x, s):
    return s * (1.0 + x * (1.0 - s))


def _bdot(a, b, dims=NN):
    return lax.dot_general(a.astype(BF16), b.astype(BF16), dims, preferred_element_type=F32)


def _split(a):
    hi = a.astype(BF16)
    return hi, (a - hi.astype(F32)).astype(BF16)


def _dot3(a, b, dims):
    dot = lambda u, v: lax.dot_general(u, v, dims, preferred_element_type=F32)
    return dot(a[0], b[1]) + dot(a[1], b[0]) + dot(a[0], b[0])


def _fdot(a, b):
    return lax.dot_general(a, b, NN, precision=HIGHEST, preferred_element_type=F32)


MM_TILE_K = 5632
MM_TILE_N = 512


def _matmul(a, b, *, ta=False, tb=False, res=None, out_dtype=F32, name, deps=()):
    m, k = (a.shape[1], a.shape[0]) if ta else a.shape
    n = b.shape[0] if tb else b.shape[1]
    tk = _tile(k, MM_TILE_K)
    nk = k // tk
    tm, tn = _tile(m, 2048 if tk <= MM_TILE_K // 2 else 1024), _tile(n, MM_TILE_N)
    dims = ((((0,) if ta else (1,)), ((1,) if tb else (0,))), ((), ()))

    def body(*refs):
        n_in = 2 + (res is not None)
        a_ref, b_ref = refs[:2]
        r_ref = refs[2] if res is not None else None
        o_ref = refs[n_in + len(deps)]
        part = lax.dot_general(a_ref[...].astype(BF16), b_ref[...].astype(BF16), dims, preferred_element_type=F32)

        def finish(out):
            if r_ref is not None:
                out = out + r_ref[...]
            o_ref[...] = out.astype(o_ref.dtype)

        if nk == 1:
            finish(part)
            return
        acc_ref = refs[-1]
        kk = pl.program_id(2)

        @pl.when(kk == 0)
        def _():
            acc_ref[...] = part

        @pl.when(jnp.logical_and(kk > 0, kk < nk - 1))
        def _():
            acc_ref[...] += part

        @pl.when(kk == nk - 1)
        def _():
            finish(acc_ref[...] + part)

    a_spec = pl.BlockSpec((tk, tm), lambda i, j, l: (l, i)) if ta else pl.BlockSpec((tm, tk), lambda i, j, l: (i, l))
    b_spec = pl.BlockSpec((tn, tk), lambda i, j, l: (j, l)) if tb else pl.BlockSpec((tk, tn), lambda i, j, l: (l, j))
    o_spec = pl.BlockSpec((tm, tn), lambda i, j, l: (i, j))
    in_specs = [a_spec, b_spec] + ([o_spec] if res is not None else []) + [ANY] * len(deps)
    args = (a, b) + ((res,) if res is not None else ()) + tuple(deps)
    return pl.pallas_call(
        body, grid=(m // tm, n // tn, nk), in_specs=in_specs, out_specs=o_spec,
        out_shape=SDS((m, n), out_dtype), scratch_shapes=[pltpu.VMEM((tm, tn), F32)] if nk > 1 else [],
        compiler_params=_params(("parallel", "parallel", "arbitrary")), name=name)(*args)


def _cast_into_full(w, kind, pos, name):
    r, n = w.shape
    tr = _tile(r, 512, 16)
    nr = r // tr
    if kind == "col":
        shape, o_spec = (r, 4 * n), pl.BlockSpec((tr, n), lambda i, p: (i, p[0]))
    else:
        shape, o_spec = (4 * r, n), pl.BlockSpec((tr, n), lambda i, p: (p[0] * nr + i, 0))

    def body(p_ref, w_ref, o_ref):
        o_ref[...] = w_ref[...].astype(BF16)

    return pl.pallas_call(
        body,
        grid_spec=pltpu.PrefetchScalarGridSpec(num_scalar_prefetch=1, grid=(nr,),
                                               in_specs=[pl.BlockSpec((tr, n), lambda i, p: (i, 0))], out_specs=o_spec),
        out_shape=SDS(shape, BF16), compiler_params=_params(("parallel",)), name=name)(pos, w)


def _rmsnorm_fwd(x, gain, name):
    t, d = x.shape
    tm = _tile(t, 256)

    def body(x_ref, g_ref, h_ref, r_ref):
        xv = x_ref[...]
        r = lax.rsqrt(jnp.mean(xv * xv, axis=-1, keepdims=True) + EPS)
        h_ref[...] = (xv * r * g_ref[...]).astype(BF16)
        r_ref[...] = r

    return pl.pallas_call(
        body, grid=(t // tm,),
        in_specs=[pl.BlockSpec((tm, d), lambda i: (i, 0)), pl.BlockSpec((1, d), lambda i: (0, 0))],
        out_specs=[pl.BlockSpec((tm, d), lambda i: (i, 0)), pl.BlockSpec((tm, 1), lambda i: (i, 0))],
        out_shape=[SDS((t, d), BF16), SDS((t, 1), F32)], compiler_params=_params(("parallel",)), name=name)(x, gain)


def _rmsnorm_bwd(dh, x, r, gain, dres, name):
    t, d = x.shape
    tm = _tile(t, 256)

    def body(dh_ref, x_ref, r_ref, g_ref, dres_ref, dx_ref, dxb_ref, dg_ref):
        @pl.when(pl.program_id(0) == 0)
        def _():
            dg_ref[...] = jnp.zeros_like(dg_ref)

        dhv, xv, rv = dh_ref[...], x_ref[...], r_ref[...]
        dg_ref[...] += jnp.sum(dhv * (xv * rv), axis=0, keepdims=True)
        u = dhv * g_ref[...]
        dx = dres_ref[...] + rv * u - xv * (rv * rv * rv) * jnp.mean(u * xv, axis=-1, keepdims=True)
        dx_ref[...] = dx
        dxb_ref[...] = dx.astype(BF16)

    row = pl.BlockSpec((tm, d), lambda i: (i, 0))
    vec = pl.BlockSpec((1, d), lambda i: (0, 0))
    return pl.pallas_call(
        body, grid=(t // tm,), in_specs=[row, row, pl.BlockSpec((tm, 1), lambda i: (i, 0)), vec, row],
        out_specs=[row, row, vec], out_shape=[SDS((t, d), F32), SDS((t, d), BF16), SDS((1, d), F32)],
        compiler_params=_params(("arbitrary",)), name=name)(dh, x, r, gain, dres)


def _merge_fwd(proj, b_gate, pa, pb, off):
    t, d = pa.shape
    tm, tc = _tile(t, 512), _tile(d, 512)
    nj = d // tc
    oa, ob = off // tc, off // tc + nj

    def body(la_ref, lb_ref, ba_ref, bb_ref, pa_ref, pb_ref, o_ref):
        ga = _sigmoid(la_ref[...] + ba_ref[...])
        gb = _sigmoid(lb_ref[...] + bb_ref[...])
        o_ref[...] = (ga * pa_ref[...] + gb * pb_ref[...]).astype(BF16)

    tile = pl.BlockSpec((tm, tc), lambda i, j: (i, j))
    return pl.pallas_call(
        body, grid=(t // tm, nj),
        in_specs=[pl.BlockSpec((tm, tc), lambda i, j: (i, oa + j)), pl.BlockSpec((tm, tc), lambda i, j: (i, ob + j)),
                  pl.BlockSpec((1, tc), lambda i, j: (0, j)), pl.BlockSpec((1, tc), lambda i, j: (0, nj + j)), tile, tile],
        out_specs=tile, out_shape=SDS((t, d), BF16), compiler_params=_params(("parallel", "parallel")),
        name="merge_fwd")(proj, proj, b_gate, b_gate, pa, pb)


def _merge_bwd(dmerged, proj, b_gate, pa, pb, off):
    t, d = pa.shape
    tm, tc = _tile(t, 512), _tile(d, 512)
    nj, ni = d // tc, t // tm
    o0 = off // tc

    def body(dm_ref, l_ref, b_ref, pa_ref, pb_ref, dp_ref, dl_ref, db_ref):
        s, i = pl.program_id(0), pl.program_id(2)
        p = jnp.where(s == 0, pa_ref[...], pb_ref[...])
        g = _sigmoid(l_ref[...] + b_ref[...])
        dm = dm_ref[...]
        dp_ref[0] = (dm * g).astype(BF16)
        dl = dm * p * g * (1.0 - g)
        dl_ref[...] = dl.astype(BF16)

        @pl.when(i == 0)
        def _():
            db_ref[...] = jnp.zeros_like(db_ref)

        db_ref[...] += jnp.sum(dl, axis=0, keepdims=True)

    tile = pl.BlockSpec((tm, tc), lambda s, j, i: (i, j))
    return pl.pallas_call(
        body, grid=(2, nj, ni),
        in_specs=[tile, pl.BlockSpec((tm, tc), lambda s, j, i: (i, o0 + s * nj + j)),
                  pl.BlockSpec((1, tc), lambda s, j, i: (0, s * nj + j)), tile, tile],
        out_specs=[pl.BlockSpec((1, tm, tc), lambda s, j, i: (s, i, j)),
                   pl.BlockSpec((tm, tc), lambda s, j, i: (i, o0 + s * nj + j)),
                   pl.BlockSpec((1, tc), lambda s, j, i: (0, s * nj + j))],
        out_shape=[SDS((2, t, d), BF16), SDS(proj.shape, BF16), SDS((1, 2 * d), F32)],
        compiler_params=_params(("arbitrary", "arbitrary", "arbitrary")),
        name="merge_bwd")(dmerged, proj, b_gate, pa, pb)


def _ffn_in_swiglu(h, w, deps=()):
    t, d = h.shape
    f = w.shape[1] // 2
    tm, tn = _tile(t, 2048), _tile(f, MM_TILE_N)
    nj = f // tn

    def body(h_ref, wg_ref, wu_ref, *rest):
        g_ref, u_ref, a_ref = rest[len(deps):]
        hv = h_ref[...]
        g = jnp.dot(hv, wg_ref[...], preferred_element_type=F32)
        u = jnp.dot(hv, wu_ref[...], preferred_element_type=F32)
        g_ref[...] = g
        u_ref[...] = u
        a_ref[...] = (g * _sigmoid(g) * u).astype(BF16)

    tile = pl.BlockSpec((tm, tn), lambda i, j: (i, j))
    return pl.pallas_call(
        body, grid=(t // tm, nj),
        in_specs=[pl.BlockSpec((tm, d), lambda i, j: (i, 0)), pl.BlockSpec((d, tn), lambda i, j: (0, j)),
                  pl.BlockSpec((d, tn), lambda i, j: (0, nj + j))] + [ANY] * len(deps),
        out_specs=[tile, tile, tile], out_shape=[SDS((t, f), F32), SDS((t, f), F32), SDS((t, f), BF16)],
        compiler_params=_params(("parallel", "parallel")), name="ffn_in_swiglu")(h, w, w, *deps)


def _swiglu_bwd(dact, gate, up):
    t, f = gate.shape
    tm = _tile(t, 128)

    def body(d_ref, g_ref, u_ref, o_ref):
        g, dv = g_ref[...], d_ref[...]
        sg = _sigmoid(g)
        o_ref[:, :f] = (dv * u_ref[...] * _dsilu(g, sg)).astype(BF16)
        o_ref[:, f:] = (dv * (g * sg)).astype(BF16)

    row = pl.BlockSpec((tm, f), lambda i: (i, 0))
    return pl.pallas_call(
        body, grid=(t // tm,), in_specs=[row, row, row],
        out_specs=pl.BlockSpec((tm, 2 * f), lambda i: (i, 0)), out_shape=SDS((t, 2 * f), BF16),
        compiler_params=_params(("parallel",)), name="swiglu_bwd")(dact, gate, up)


def _loss_head(y, target):
    t, d = y.shape
    tm = _tile(t, 256)

    def body(y_ref, t_ref, dy_ref, dyb_ref, l_ref):
        @pl.when(pl.program_id(0) == 0)
        def _():
            l_ref[...] = jnp.zeros_like(l_ref)

        e = y_ref[...] - t_ref[...]
        dy = e * (1.0 / d)
        dy_ref[...] = dy
        dyb_ref[...] = dy.astype(BF16)
        l_ref[...] += 0.5 * jnp.sum(jnp.mean(e * e, axis=-1, keepdims=True), axis=0, keepdims=True)

    row = pl.BlockSpec((tm, d), lambda i: (i, 0))
    return pl.pallas_call(
        body, grid=(t // tm,), in_specs=[row, row], out_specs=[row, row, pl.BlockSpec((1, 1), lambda i: (0, 0))],
        out_shape=[SDS((t, d), F32), SDS((t, d), BF16), SDS((1, 1), F32)], compiler_params=_params(("arbitrary",)),
        name="loss_head")(y, target)


def _rel_onehot(qi):
    p = lax.broadcasted_iota(jnp.int32, (REL_LANES, BAND), 1)
    r = lax.broadcasted_iota(jnp.int32, (REL_LANES, BAND), 0)
    idx = jnp.clip(qi + PAD - p, -REL_FUTURE, REL_PAST) + REL_FUTURE
    return (idx == r).astype(F32)


def _relbias_expand(rb):
    h = rb.shape[0]

    def body(rb_ref, o_ref):
        def step(qi, _):
            o_ref[qi] = _fdot(rb_ref[...], _rel_onehot(qi))
            return 0

        lax.fori_loop(0, CHUNK, step, 0)

    return pl.pallas_call(body, out_shape=SDS((CHUNK, h, BAND), F32), compiler_params=_params(),
                          name="relbias_expand")(rb)


def _relbias_reduce(dbias):
    h = dbias.shape[1]

    def body(db_ref, o_ref):
        def step(qi, acc):
            return acc + lax.dot_general(db_ref[qi], _rel_onehot(qi), NT, precision=HIGHEST,
                                         preferred_element_type=F32)

        o_ref[...] = lax.fori_loop(0, CHUNK, step, jnp.zeros((h, REL_LANES), F32))

    return pl.pallas_call(body, out_shape=SDS((h, REL_LANES), F32), compiler_params=_params(),
                          name="relbias_reduce")(dbias)


def _lower_bound(l_ref):
    l0, l1 = l_ref[0:1, :], l_ref[1:2, :]
    m = jnp.maximum(l0, l1)
    e0, e1 = jnp.exp(l0 - m), jnp.exp(l1 - m)
    return e0 / (e0 + e1)


def _tri(lower):
    r = lax.broadcasted_iota(jnp.int32, (CHUNK, CHUNK), 0)
    c = lax.broadcasted_iota(jnp.int32, (CHUNK, CHUNK), 1)
    return r >= c if lower else r <= c


def _hgrn_intra(qs, kk, b_s):
    rows = lax.broadcasted_iota(jnp.int32, (CHUNK, HEAD), 0)
    b = b_s[...]
    out = []
    for i in range(CHUNK // SUB):
        lo = i * SUB
        ref = jnp.zeros((1, HEAD), F32) if i == 0 else b_s[lo - 1:lo, :]
        eq = jnp.exp(b[lo:lo + SUB] - ref)
        qt = _split(qs[lo:lo + SUB] * eq)
        e = jnp.where(rows < lo + SUB, jnp.exp(jnp.minimum(ref - b, EXP_CLAMP)), 0.0)
        kt = _split(kk * e)
        out.append((eq, qt, e, kt))
    return out


def _hgrn_scores(blocks):
    tr = lax.broadcasted_iota(jnp.int32, (SUB, CHUNK), 0)
    tc = lax.broadcasted_iota(jnp.int32, (SUB, CHUNK), 1)
    return jnp.concatenate([jnp.where(tc <= tr + i * SUB, _dot3(qt, kt, NT), 0.0)
                            for i, (_, qt, _, kt) in enumerate(blocks)], axis=0)


def _hgrn_fwd(proj, lb_logits, gain, n_heads):
    t = proj.shape[0]
    nc = t // CHUNK
    da = n_heads * HEAD
    hp = MIX_HEADS
    wide = hp * HEAD

    def body(q_ref, f_ref, i_ref, g_ref, l_ref, gain_ref, y_ref, o_ref, st_ref, state, b_s):
        state[...] = jnp.zeros_like(state)
        lb_all = _lower_bound(l_ref)
        tril = _tri(True).astype(F32)

        def chunks(i, _):
            dot = functools.partial(lax.dot_general, preferred_element_type=F32)
            items = []
            for u in range(MIX_UNROLL):
                for hh in range(hp):
                    j = i * MIX_UNROLL + u
                    sl = pl.ds(pl.multiple_of(j * CHUNK, CHUNK), CHUNK)
                    cols = slice(hh * HEAD, (hh + 1) * HEAD)
                    lb = lb_all[:, cols]
                    fg = lb + (1.0 - lb) * _sigmoid(f_ref[sl, cols])
                    qv = q_ref[sl, cols]
                    gv = g_ref[sl, cols]
                    items.append(dict(hh=hh, j=j, sl=sl, cols=cols, lf=jnp.log(fg), kk=1.0 - fg, qs=qv * _sigmoid(qv),
                                      vb=i_ref[sl, cols].astype(BF16), gate=gv * _sigmoid(gv)))
            for it in items:
                it["b"] = _fdot(tril, it["lf"])
            for slot, it in enumerate(items):
                b = it["b"]
                b_s[slot] = b
                it["blocks"] = _hgrn_intra(it["qs"], it["kk"], b_s.at[slot])
                it["ebl"] = jnp.exp(b_s[slot, CHUNK - 1:CHUNK, :])
                it["qe"] = (it["qs"] * jnp.exp(b)).astype(BF16)
                it["kd"] = (it["kk"] * jnp.exp(b_s[slot, CHUNK - 1:CHUNK, :] - b)).astype(BF16)
            for it in items:
                it["a"] = _hgrn_scores(it["blocks"]).astype(BF16)
            for it in items:
                it["kv"] = dot(it["vb"], it["kd"], TN)
                it["o"] = dot(it["a"], it["vb"], NN)
            s_now = [state[hh] for hh in range(hp)]
            for it in items:
                it["s_in"] = s_now[it["hh"]]
                s_now[it["hh"]] = it["s_in"] * it["ebl"] + it["kv"]
            for hh in range(hp):
                state[hh] = s_now[hh]
            for it in items:
                it["o"] = it["o"] + dot(it["qe"], it["s_in"].astype(BF16), NT)
            for it in items:
                o, sl, cols = it["o"], it["sl"], it["cols"]
                st_ref[it["hh"], it["j"]] = it["s_in"]
                o_ref[sl, cols] = o
                rr = lax.rsqrt(jnp.mean(o * o, axis=-1, keepdims=True) + EPS)
                y_ref[sl, cols] = (o * rr * gain_ref[:, cols] * it["gate"]).astype(BF16)
            return 0

        assert nc % MIX_UNROLL == 0, (nc, MIX_UNROLL)
        lax.fori_loop(0, nc // MIX_UNROLL, chunks, 0)

    col = lambda k: pl.BlockSpec((t, wide), lambda h: (0, k * (n_heads // hp) + h))
    vec = pl.BlockSpec((1, wide), lambda h: (0, h))
    return pl.pallas_call(
        body, grid=(n_heads // hp,),
        in_specs=[col(0), col(1), col(2), col(3), pl.BlockSpec((2, wide), lambda h: (0, h)), vec],
        out_specs=[pl.BlockSpec((t, wide), lambda h: (0, h)), pl.BlockSpec((t, wide), lambda h: (0, h)),
                   pl.BlockSpec((hp, nc, HEAD, HEAD), lambda h: (h, 0, 0, 0))],
        out_shape=[SDS((t, da), BF16), SDS((t, da), F32), SDS((n_heads, nc, HEAD, HEAD), F32)],
        scratch_shapes=[pltpu.VMEM((hp, HEAD, HEAD), F32), pltpu.VMEM((hp * MIX_UNROLL, CHUNK, HEAD), F32)],
        compiler_params=_params(("parallel",)), name="hgrn_fwd")(proj, proj, proj, proj, lb_logits, gain)


def _hgrn_bwd(dproj, proj, o_pre, states, dy, lb_logits, gain, n_heads, deps=()):
    t = proj.shape[0]
    nc = t // CHUNK
    da = n_heads * HEAD
    hp = MIX_HEADS
    wide = hp * HEAD

    def body(*refs):
        (q_ref, f_ref, i_ref, g_ref, o_ref, st_ref, dy_ref, l_ref, gain_ref,
         dproj_ref, dl_ref, dgain_ref, res, dstate, b_s) = refs[1 + len(deps):]

        @pl.when(pl.program_id(1) == 0)
        def _():
            dstate[...] = jnp.zeros_like(dstate)
            lb_all = _lower_bound(l_ref)
            tril_m, tril, triu = _tri(True), _tri(True).astype(F32), _tri(False).astype(F32)
            last = lax.broadcasted_iota(jnp.int32, (CHUNK, HEAD), 0) == CHUNK - 1

            def chunks(i, carry):
                dot = functools.partial(lax.dot_general, preferred_element_type=F32)
                items = []
                for u in range(MIX_UNROLL_BWD):
                    for hh in range(hp):
                        j = nc - 1 - (i * MIX_UNROLL_BWD + u)
                        sl = pl.ds(pl.multiple_of(j * CHUNK, CHUNK), CHUNK)
                        cols = slice(hh * HEAD, (hh + 1) * HEAD)
                        lb, gain_v = lb_all[:, cols], gain_ref[:, cols]
                        sg = _sigmoid(f_ref[sl, cols])
                        fg = lb + (1.0 - lb) * sg
                        qv = q_ref[sl, cols]
                        sq = _sigmoid(qv)
                        gv = g_ref[sl, cols]
                        sgg = _sigmoid(gv)
                        silg = gv * sgg
                        o = o_ref[sl, cols]
                        dyv = dy_ref[sl, cols]
                        rr = lax.rsqrt(jnp.mean(o * o, axis=-1, keepdims=True) + EPS)
                        on = o * rr
                        don = dyv * gain_v * silg
                        do = (rr * don - o * (rr * rr * rr) * jnp.mean(don * o, axis=-1, keepdims=True)).astype(BF16)
                        items.append(dict(
                            hh=hh, j=j, sl=sl, cols=cols, lb=lb, sg=sg, fg=fg, kk=1.0 - fg, qv=qv, sq=sq, qs=qv * sq,
                            vb=i_ref[sl, cols].astype(BF16), do=do, dg=dyv * on * gain_v * _dsilu(gv, sgg),
                            dgain=jnp.sum(dyv * on * silg, axis=0, keepdims=True)))
                for it in items:
                    it["b"] = _fdot(tril, jnp.log(it["fg"]))
                for slot, it in enumerate(items):
                    b = it["b"]
                    b_s[slot] = b
                    it["blocks"] = _hgrn_intra(it["qs"], it["kk"], b_s.at[slot])
                    bl = b_s[slot, CHUNK - 1:CHUNK, :]
                    it["eb"], it["ebl"], it["ekd"] = jnp.exp(b), jnp.exp(bl), jnp.exp(bl - b)
                    it["s_in"] = st_ref[it["hh"], it["j"]]
                for it in items:
                    it["a"] = _hgrn_scores(it["blocks"]).astype(BF16)
                    it["da"] = jnp.where(tril_m, dot(it["do"], it["vb"], NT), 0.0)
                for it in items:
                    dq_rows = []
                    dk = jnp.zeros((CHUNK, HEAD), F32)
                    for blk, (eq, qt, e, kt) in enumerate(it["blocks"]):
                        da_i = _split(it["da"][blk * SUB:(blk + 1) * SUB])
                        dq_rows.append(eq * _dot3(da_i, kt, NN))
                        dk = dk + e * _dot3(da_i, qt, TN)
                    it["dq"] = jnp.concatenate(dq_rows, axis=0) + dot(it["do"], it["s_in"].astype(BF16), NN) * it["eb"]
                    it["dk"] = dk
                    it["dv"] = dot(it["a"], it["do"], TN)
                    it["g"] = dot(it["do"], (it["qs"] * it["eb"]).astype(BF16), TN)
                ds_now = [dstate[hh] for hh in range(hp)]
                for it in items:
                    it["ds_out"] = ds_now[it["hh"]]
                    ds_now[it["hh"]] = it["ds_out"] * it["ebl"] + it["g"]
                for hh in range(hp):
                    dstate[hh] = ds_now[hh]
                for it in items:
                    dsb = it["ds_out"].astype(BF16)
                    it["dv"] = it["dv"] + dot((it["kk"] * it["ekd"]).astype(BF16), dsb, NT)
                    it["dk_state"] = it["ekd"] * dot(it["vb"], dsb, NN)
                for it in items:
                    kk, dk_state = it["kk"], it["dk_state"]
                    it["dk"] = it["dk"] + dk_state
                    extra = (jnp.sum(kk * dk_state, axis=0, keepdims=True)
                             + it["ebl"] * jnp.sum(it["s_in"] * it["ds_out"], axis=0, keepdims=True))
                    it["db"] = it["qs"] * it["dq"] - kk * it["dk"] + jnp.where(last, extra, 0.0)
                for it in items:
                    it["dlf"] = _fdot(triu, it["db"])
                carry = list(carry)
                for it in items:
                    hh, sl, cols, sg, lb = it["hh"], it["sl"], it["cols"], it["sg"], it["lb"]
                    dfg = it["dlf"] / it["fg"] - it["dk"]
                    dlb_acc, dgain_acc = carry[hh]
                    carry[hh] = (dlb_acc + jnp.sum(dfg * (1.0 - sg), axis=0, keepdims=True), dgain_acc + it["dgain"])
                    res[0, sl, cols] = (it["dq"] * _dsilu(it["qv"], it["sq"])).astype(BF16)
                    res[1, sl, cols] = (dfg * (1.0 - lb) * sg * (1.0 - sg)).astype(BF16)
                    res[2, sl, cols] = it["dv"].astype(BF16)
                    res[3, sl, cols] = it["dg"].astype(BF16)
                return tuple(carry)

            assert nc % MIX_UNROLL_BWD == 0, (nc, MIX_UNROLL_BWD)
            zero = jnp.zeros((1, HEAD), F32)
            sums = lax.fori_loop(0, nc // MIX_UNROLL_BWD, chunks, ((zero, zero),) * hp)
            for hh, (dlb, dgain) in enumerate(sums):
                cols = slice(hh * HEAD, (hh + 1) * HEAD)
                lb = lb_all[:, cols]
                dgain_ref[:, cols] = dgain
                dl0 = dlb * lb * (1.0 - lb)
                dl_ref[0:1, cols] = dl0
                dl_ref[1:2, cols] = -dl0

        dproj_ref[...] = res[pl.program_id(1)]

    ng = n_heads // hp
    col = lambda k: pl.BlockSpec((t, wide), lambda h, p: (0, k * ng + h))
    head = pl.BlockSpec((t, wide), lambda h, p: (0, h))
    vec = pl.BlockSpec((1, wide), lambda h, p: (0, h))
    return pl.pallas_call(
        body, grid=(ng, 4),
        in_specs=[ANY] * (1 + len(deps)) + [col(0), col(1), col(2), col(3), head,
                  pl.BlockSpec((hp, nc, HEAD, HEAD), lambda h, p: (h, 0, 0, 0)),
                  head, pl.BlockSpec((2, wide), lambda h, p: (0, h)), vec],
        out_specs=[pl.BlockSpec((t, wide), lambda h, p: (0, p * ng + h)),
                   pl.BlockSpec((2, wide), lambda h, p: (0, h)), vec],
        out_shape=[SDS(dproj.shape, BF16), SDS((2, da), F32), SDS((1, da), F32)],
        scratch_shapes=[pltpu.VMEM((4, t, wide), BF16), pltpu.VMEM((hp, HEAD, HEAD), F32),
                        pltpu.VMEM((hp * MIX_UNROLL_BWD, CHUNK, HEAD), F32)],
        input_output_aliases={0: 0}, compiler_params=_params(("arbitrary", "arbitrary")),
        name="hgrn_bwd")(dproj, *deps, proj, proj, proj, proj, o_pre, states, dy, lb_logits, gain)


ROWS = 256


def _head_norm(x_ref, gain, dst, dst_off, t):
    def step(i, _):
        sl = pl.ds(pl.multiple_of(i * ROWS, ROWS), ROWS)
        xv = x_ref[sl, :]
        r = lax.rsqrt(jnp.mean(xv * xv, axis=-1, keepdims=True) + EPS)
        dst[pl.ds(pl.multiple_of(dst_off + i * ROWS, ROWS), ROWS), :] = (xv * r * gain).astype(BF16)
        return 0

    lax.fori_loop(0, t // ROWS, step, 0)


def _head_norm_bwd(x_ref, gain, dn_ref, dn_off, out, slot, t):
    def step(i, acc):
        sl = pl.ds(pl.multiple_of(i * ROWS, ROWS), ROWS)
        xv = x_ref[sl, :]
        dn = dn_ref[pl.ds(pl.multiple_of(dn_off + i * ROWS, ROWS), ROWS), :]
        r = lax.rsqrt(jnp.mean(xv * xv, axis=-1, keepdims=True) + EPS)
        u = dn * gain
        out[slot, sl, :] = r * u - xv * (r * r * r) * jnp.mean(u * xv, axis=-1, keepdims=True)
        return acc + jnp.sum(dn * (xv * r), axis=0, keepdims=True)

    return lax.fori_loop(0, t // ROWS, step, jnp.zeros((1, HEAD), F32))


def _attn_scores(qn, kpad, n):
    qc = qn[pl.ds(pl.multiple_of(n * CHUNK, CHUNK), CHUNK), :]
    band = pl.ds(pl.multiple_of(n * CHUNK, CHUNK), BAND)
    return qc, band, lax.dot_general(qc, kpad[band, :], NT, preferred_element_type=F32)


def _attn_softmax(raw, bias_ref, n):
    s = raw * (HEAD ** -0.5) + bias_ref[0]
    col = lax.broadcasted_iota(jnp.int32, (CHUNK, BAND), 1)
    s = jnp.where(col >= PAD - n * CHUNK, s, -jnp.inf)
    p = jnp.exp(s - jnp.max(s, axis=-1, keepdims=True))
    return p / jnp.sum(p, axis=-1, keepdims=True)


def _attn_fwd(proj, q_gain, k_gain, bias, n_heads, col0):
    t = proj.shape[0]
    nc = t // CHUNK

    def body(q_ref, k_ref, v_ref, qg_ref, kg_ref, bias_ref, y_ref, qn, kpad, vpad):
        kpad[0:PAD, :] = jnp.zeros((PAD, HEAD), BF16)
        vpad[0:PAD, :] = jnp.zeros((PAD, HEAD), BF16)
        _head_norm(q_ref, qg_ref[...], qn, 0, t)
        _head_norm(k_ref, kg_ref[...], kpad, PAD, t)

        def copy_v(i, _):
            vpad[pl.ds(pl.multiple_of(PAD + i * ROWS, ROWS), ROWS), :] = v_ref[
                pl.ds(pl.multiple_of(i * ROWS, ROWS), ROWS), :].astype(BF16)
            return 0

        lax.fori_loop(0, t // ROWS, copy_v, 0)

        def chunks(i, _):
            ns = [i * ATT_UNROLL + u for u in range(ATT_UNROLL)]
            scored = [_attn_scores(qn, kpad, n) for n in ns]
            probs = [_attn_softmax(raw, bias_ref, n).astype(BF16) for n, (_, _, raw) in zip(ns, scored)]
            outs = [lax.dot_general(p, vpad[band, :], NN, preferred_element_type=F32).astype(BF16)
                    for p, (_, band, _) in zip(probs, scored)]
            for n, o in zip(ns, outs):
                y_ref[pl.ds(pl.multiple_of(n * CHUNK, CHUNK), CHUNK), :] = o
            return 0

        assert nc % ATT_UNROLL == 0, (nc, ATT_UNROLL)
        lax.fori_loop(0, nc // ATT_UNROLL, chunks, 0)

    col = lambda k: pl.BlockSpec((t, HEAD), lambda h: (0, col0 + k * n_heads + h))
    vec = pl.BlockSpec((1, HEAD), lambda h: (0, 0))
    return pl.pallas_call(
        body, grid=(n_heads,),
        in_specs=[col(0), col(1), col(2), vec, vec, pl.BlockSpec((1, CHUNK, BAND), lambda h: (h, 0, 0))],
        out_specs=pl.BlockSpec((t, HEAD), lambda h: (0, h)), out_shape=SDS((t, n_heads * HEAD), BF16),
        scratch_shapes=[pltpu.VMEM((t, HEAD), BF16), pltpu.VMEM((t + PAD, HEAD), BF16), pltpu.VMEM((t + PAD, HEAD), BF16)],
        compiler_params=_params(("parallel",)), name="attn_fwd")(proj, proj, proj, q_gain, k_gain, bias)


def _attn_bwd(dproj, proj, q_gain, k_gain, bias, dy, n_heads, col0, deps=()):
    t = proj.shape[0]
    nc = t // CHUNK

    def body(*refs):
        (q_ref, k_ref, v_ref, qg_ref, kg_ref, bias_ref, dy_ref,
         dproj_ref, dbias_ref, dqg_ref, dkg_ref, qn, kpad, vpad, dqn, dk_acc, dv_acc, res) = refs[1 + len(deps):]
        h, part = pl.program_id(0), pl.program_id(1)

        @pl.when(part == 0)
        def _():
            kpad[0:PAD, :] = jnp.zeros((PAD, HEAD), BF16)
            vpad[0:PAD, :] = jnp.zeros((PAD, HEAD), BF16)
            _head_norm(q_ref, qg_ref[...], qn, 0, t)
            _head_norm(k_ref, kg_ref[...], kpad, PAD, t)

            def prep(i, _):
                sl = pl.ds(pl.multiple_of(PAD + i * ROWS, ROWS), ROWS)
                vpad[sl, :] = v_ref[pl.ds(pl.multiple_of(i * ROWS, ROWS), ROWS), :].astype(BF16)
                return 0

            lax.fori_loop(0, t // ROWS, prep, 0)

            def clear(i, _):
                sl = pl.ds(pl.multiple_of(i * ROWS, ROWS), ROWS)
                dk_acc[sl, :] = jnp.zeros((ROWS, HEAD), F32)
                dv_acc[sl, :] = jnp.zeros((ROWS, HEAD), F32)
                return 0

            lax.fori_loop(0, (t + PAD) // ROWS, clear, 0)
            dbias_ref[0] = jnp.zeros((CHUNK, BAND), F32)

            def chunks(i, _):
                dot = functools.partial(lax.dot_general, preferred_element_type=F32)
                ns = [i * ATT_UNROLL_BWD + u for u in range(ATT_UNROLL_BWD)]
                scored = [_attn_scores(qn, kpad, n) for n in ns]
                dos = [dy_ref[pl.ds(pl.multiple_of(n * CHUNK, CHUNK), CHUNK), :].astype(BF16) for n in ns]
                dps = [dot(do, vpad[band, :], NT) for do, (_, band, _) in zip(dos, scored)]
                ps, dss = [], []
                for n, (_, _, raw), dp in zip(ns, scored, dps):
                    p = _attn_softmax(raw, bias_ref, n)
                    ds = p * (dp - jnp.sum(dp * p, axis=-1, keepdims=True))
                    dbias_ref[0] += ds
                    ps.append(p.astype(BF16))
                    dss.append((ds * (HEAD ** -0.5)).astype(BF16))
                dqs = [dot(d, kpad[band, :], NN) for d, (_, band, _) in zip(dss, scored)]
                dks = [dot(d, qc, TN) for d, (qc, _, _) in zip(dss, scored)]
                dvs = [dot(p, do, TN) for p, do in zip(ps, dos)]
                for n, (_, band, _), dq, dk, dv in zip(ns, scored, dqs, dks, dvs):
                    dqn[pl.ds(pl.multiple_of(n * CHUNK, CHUNK), CHUNK), :] = dq
                    dk_acc[band, :] += dk
                    dv_acc[band, :] += dv
                return 0

            assert nc % ATT_UNROLL_BWD == 0, (nc, ATT_UNROLL_BWD)
            lax.fori_loop(0, nc // ATT_UNROLL_BWD, chunks, 0)
            dqg = _head_norm_bwd(q_ref, qg_ref[...], dqn, 0, res, 0, t)
            dkg = _head_norm_bwd(k_ref, kg_ref[...], dk_acc, PAD, res, 1, t)

            def put_v(i, _):
                sl = pl.ds(pl.multiple_of(i * ROWS, ROWS), ROWS)
                res[2, sl, :] = dv_acc[pl.ds(pl.multiple_of(PAD + i * ROWS, ROWS), ROWS), :]
                return 0

            lax.fori_loop(0, t // ROWS, put_v, 0)

            @pl.when(h == 0)
            def _():
                dqg_ref[...] = jnp.zeros_like(dqg_ref)
                dkg_ref[...] = jnp.zeros_like(dkg_ref)

            dqg_ref[...] += dqg
            dkg_ref[...] += dkg

        dproj_ref[...] = res[part].astype(BF16)

    col = lambda k: pl.BlockSpec((t, HEAD), lambda h, p: (0, col0 + k * n_heads + h))
    vec = pl.BlockSpec((1, HEAD), lambda h, p: (0, 0))
    btile = pl.BlockSpec((1, CHUNK, BAND), lambda h, p: (h, 0, 0))
    return pl.pallas_call(
        body, grid=(n_heads, 3),
        in_specs=[ANY] * (1 + len(deps)) + [col(0), col(1), col(2), vec, vec, btile,
                                            pl.BlockSpec((t, HEAD), lambda h, p: (0, h))],
        out_specs=[pl.BlockSpec((t, HEAD), lambda h, p: (0, col0 + p * n_heads + h)), btile, vec, vec],
        out_shape=[SDS(dproj.shape, BF16), SDS((n_heads, CHUNK, BAND), F32), SDS((1, HEAD), F32), SDS((1, HEAD), F32)],
        scratch_shapes=[pltpu.VMEM((t, HEAD), BF16), pltpu.VMEM((t + PAD, HEAD), BF16), pltpu.VMEM((t + PAD, HEAD), BF16),
                        pltpu.VMEM((t, HEAD), F32), pltpu.VMEM((t + PAD, HEAD), F32), pltpu.VMEM((t + PAD, HEAD), F32),
                        pltpu.VMEM((3, t, HEAD), F32)],
        input_output_aliases={0: 0}, compiler_params=_params(("arbitrary", "arbitrary")),
        name="attn_bwd")(dproj, *deps, proj, proj, proj, q_gain, k_gain, bias, dy)


def _place():
    x, y, c = lax.axis_index("x"), lax.axis_index("y"), lax.axis_index("c")
    others = [(1 - x, y), (x, 1 - y), (1 - x, 1 - y)]
    return x, y, c, others


def _half_shape(kind, shard_shape):
    r, n = shard_shape
    return (r // 2, n) if kind == "col" else (r, n // 2)


def _chunk_of(ref, kind, chip, half, shard_shape):
    r, n = shard_shape
    if kind == "col":
        return ref.at[pl.ds(half * (r // 2), r // 2), pl.ds(chip * n, n)]
    return ref.at[pl.ds(chip * r, r), pl.ds(half * (n // 2), n // 2)]


def _half_of(ref, kind, half):
    r, n = ref.shape
    if kind == "col":
        return ref.at[pl.ds(half * (r // 2), r // 2), :]
    return ref.at[:, pl.ds(half * (n // 2), n // 2)]


EFFECT = pltpu.SideEffectType.DATAFLOW_SIDE_EFFECTING


def _start_copies(name, bufs, plan, n, deps):
    nb, nd = len(bufs), len(deps)

    def body(*refs):
        send, recv, token = refs[nb + nd], refs[nb + nd + 1], refs[-1]
        for cp in plan(refs[:nb], send, recv)[0]:
            cp.start()
        token[...] = jnp.zeros_like(token)

    out = pl.pallas_call(
        body, name=name,
        out_shape=(pltpu.SemaphoreType.DMA((n,)), pltpu.SemaphoreType.DMA((n,)),
                   *[pltpu.HBM(b.shape, b.dtype) for b in bufs], SDS((8, 128), F32)),
        in_specs=[HBM] * nb + [ANY] * nd,
        out_specs=(SEM, SEM, *[HBM] * nb, pl.BlockSpec(memory_space=pltpu.VMEM)),
        input_output_aliases={i: 2 + i for i in range(nb)},
        compiler_params=pltpu.CompilerParams(has_side_effects=EFFECT),
    )(*[pltpu.with_memory_space_constraint(b, pltpu.HBM) for b in bufs], *deps)
    return out[0], out[1], list(out[2:2 + nb]), out[-1]


def _wait_copies(name, bufs, send, recv, plan, after):
    nb = len(bufs)

    def body(*refs):
        sends, recvs = plan(refs[:nb], refs[nb], refs[nb + 1])
        for cp in sends:
            cp.wait_send()
        for cp in recvs:
            cp.wait_recv()

    out = pl.pallas_call(
        body, name=name, out_shape=tuple(pltpu.HBM(b.shape, b.dtype) for b in bufs),
        in_specs=[HBM] * nb + [SEM, SEM] + [ANY] * len(after), out_specs=tuple([HBM] * nb),
        input_output_aliases={i: i for i in range(nb)},
        compiler_params=pltpu.CompilerParams(has_side_effects=EFFECT),
    )(*bufs, send, recv, *after)
    return list(out)


def _remote(src, dst, send, recv, i, dev):
    return pltpu.make_async_remote_copy(src_ref=src, dst_ref=dst, send_sem=send.at[i], recv_sem=recv.at[i],
                                        device_id=dev, device_id_type=MESH)


def _plan_gather_ici(kinds, shapes):
    def plan(refs, send, recv):
        x, y, c, others = _place()
        sends, recvs = [], []
        for w, (kind, ss) in enumerate(zip(kinds, shapes)):
            for p, (px, py) in enumerate(others):
                mine = _chunk_of(refs[w], kind, 2 * x + y, c, ss)
                theirs = _chunk_of(refs[w], kind, 2 * px + py, c, ss)
                sends.append(_remote(mine, mine, send, recv, 3 * w + p, (px, py, c)))
                recvs.append(_remote(theirs, theirs, send, recv, 3 * w + p, (px, py, c)))
        return sends, recvs

    return plan, 3 * len(kinds)


def _plan_gather_pass(kinds, shapes):
    def plan(refs, send, recv):
        x, y, c, others = _place()
        sends, recvs = [], []
        for w, (kind, ss) in enumerate(zip(kinds, shapes)):
            for p, (px, py) in enumerate(others):
                got = _chunk_of(refs[w], kind, 2 * px + py, c, ss)
                coming = _chunk_of(refs[w], kind, 2 * px + py, 1 - c, ss)
                sends.append(_remote(got, got, send, recv, 3 * w + p, (x, y, 1 - c)))
                recvs.append(_remote(coming, coming, send, recv, 3 * w + p, (x, y, 1 - c)))
        return sends, recvs

    return plan, 3 * len(kinds)


def _plan_pair(nw):
    def plan(refs, send, recv):
        x, y, c, _ = _place()
        sends = [_remote(refs[w], refs[nw + w], send, recv, w, (x, y, 1 - c)) for w in range(nw)]
        return sends, sends

    return plan, nw


def _plan_chip(nw):
    def plan(refs, send, recv):
        x, y, c, others = _place()
        sends = []
        for w in range(nw):
            for p, (px, py) in enumerate(others):
                sends.append(_remote(refs[w].at[2 * px + py], refs[nw + w].at[p], send, recv, 3 * w + p, (px, py, c)))
        return sends, sends

    return plan, 3 * nw


def _plan_share(kinds):
    def plan(refs, send, recv):
        x, y, c, _ = _place()
        sends, recvs = [], []
        for w, kind in enumerate(kinds):
            mine, theirs = _half_of(refs[w], kind, c), _half_of(refs[w], kind, 1 - c)
            sends.append(_remote(mine, mine, send, recv, w, (x, y, 1 - c)))
            recvs.append(_remote(theirs, theirs, send, recv, w, (x, y, 1 - c)))
        return sends, recvs

    return plan, len(kinds)


def _dw_half(x, dy, kind, shard_shape, pos, got, name, deps=()):
    r, n = shard_shape
    h0, h1 = _half_shape(kind, shard_shape)
    tm = 512 if h0 % 512 == 0 else h0
    tn = _tile(h1, 1408 if tm <= 512 else 512)
    ni, nj = h0 // tm, h1 // tn
    t = x.shape[0]
    flip = got is None

    def half_of(p):
        return 1 - p[1] if flip else p[1]

    if kind == "col":
        a_spec = pl.BlockSpec((t, tm), lambda k, i, j, p: (0, half_of(p) * ni + i))
        b_spec = pl.BlockSpec((t, tn), lambda k, i, j, p: (0, k * nj + j))
    else:
        a_spec = pl.BlockSpec((t, tm), lambda k, i, j, p: (0, k * ni + i))
        b_spec = pl.BlockSpec((t, tn), lambda k, i, j, p: (0, half_of(p) * nj + j))
    o_spec = pl.BlockSpec((1, tm, tn), lambda k, i, j, p: (k, i, j))

    def body(p_ref, a_ref, b_ref, *rest):
        rest = rest[len(deps):]
        part = lax.dot_general(a_ref[...], b_ref[...], TN, preferred_element_type=F32)
        if got is None:
            rest[0][0] = part
        else:
            total = part + rest[0][0]
            rest[1][0] = total
            rest[2][0] = total.astype(BF16)

    if got is None:
        in_specs, args = [a_spec, b_spec] + [ANY] * len(deps), (pos, x, dy, *deps)
        out_specs, out_shape = o_spec, SDS((4, h0, h1), F32)
    else:
        in_specs, args = [a_spec, b_spec] + [ANY] * len(deps) + [o_spec], (pos, x, dy, *deps, got)
        out_specs, out_shape = [o_spec, o_spec], [SDS((4, h0, h1), F32), SDS((4, h0, h1), BF16)]
    return pl.pallas_call(
        body,
        grid_spec=pltpu.PrefetchScalarGridSpec(num_scalar_prefetch=1, grid=(4, ni, nj), in_specs=in_specs,
                                               out_specs=out_specs),
        out_shape=out_shape, compiler_params=_params(("parallel", "parallel", "parallel")), name=name)(*args)


def _chip_add(part32, got16, kind, shard_shape, pos, name):
    _, h0, h1 = part32.shape
    tr, tn = _tile(h0, 256, 16), _tile(h1, 1408)
    ni, nj = h0 // tr, h1 // tn
    own = pl.BlockSpec((1, tr, tn), lambda i, j, p: (p[0], i, j))
    oth = pl.BlockSpec((3, tr, tn), lambda i, j, p: (0, i, j))
    if kind == "col":
        out = pl.BlockSpec((tr, tn), lambda i, j, p: (p[1] * ni + i, j))
    else:
        out = pl.BlockSpec((tr, tn), lambda i, j, p: (i, p[1] * nj + j))

    def body(p_ref, own_ref, oth_ref, o_ref):
        o_ref[...] = ((own_ref[0] + oth_ref[0].astype(F32)) + oth_ref[1].astype(F32)) + oth_ref[2].astype(F32)

    return pl.pallas_call(
        body,
        grid_spec=pltpu.PrefetchScalarGridSpec(num_scalar_prefetch=1, grid=(ni, nj), in_specs=[own, oth], out_specs=out),
        out_shape=SDS(shard_shape, F32), compiler_params=_params(("parallel", "parallel")), name=name)(pos, part32, got16)


def _adamw_math(w, g, m, v):
    m = ADAM_B1 * m + (1.0 - ADAM_B1) * g
    v = ADAM_B2 * v + (1.0 - ADAM_B2) * (g * g)
    m_hat = m / (1.0 - ADAM_B1 ** ADAM_STEP)
    v_hat = v / (1.0 - ADAM_B2 ** ADAM_STEP)
    return -ADAM_LR * (m_hat / (jnp.sqrt(v_hat) + ADAM_EPS) + ADAM_WD * w), m, v


def _adamw(w, g, m, v, name):
    r, n = w.shape
    tr, tn = _tile(r, 256, 16), _tile(n, 1408)

    def body(w_ref, g_ref, m_ref, v_ref, d_ref, nm_ref, nv_ref, go_ref):
        gv = g_ref[...]
        d_ref[...], nm_ref[...], nv_ref[...] = _adamw_math(w_ref[...], gv, m_ref[...], v_ref[...])
        go_ref[...] = gv

    tile = pl.BlockSpec((tr, tn), lambda i, j: (i, j))
    return pl.pallas_call(
        body, grid=(r // tr, n // tn), in_specs=[tile] * 4, out_specs=[tile] * 4, out_shape=[SDS((r, n), F32)] * 4,
        compiler_params=_params(("parallel", "parallel")), name=name)(w, g, m, v)


def _small_allreduce_adamw(g, w, m, v, deps=()):
    length = g.shape[1]

    def body(*refs):
        g_ref, w_ref, m_ref, v_ref = refs[:4]
        gs_ref, d_ref, nm_ref, nv_ref, buf, send, recv = refs[4 + len(deps):]
        x, y, c = lax.axis_index("x"), lax.axis_index("y"), lax.axis_index("c")
        me = 4 * x + 2 * y + c
        buf[me] = g_ref[...]
        cps = []
        for d in range(1, 8):
            peer = (x ^ (d >> 2), y ^ ((d >> 1) & 1), c ^ (d & 1))
            cp = pltpu.make_async_remote_copy(src_ref=buf.at[me], dst_ref=buf.at[me], send_sem=send.at[d - 1],
                                              recv_sem=recv.at[d - 1], device_id=peer, device_id_type=MESH)
            cp.start()
            cps.append(cp)
        for cp in cps:
            cp.wait()
        total = buf[0]
        for d in range(1, 8):
            total = total + buf[d]
        gs_ref[...] = total
        d_ref[...], nm_ref[...], nv_ref[...] = _adamw_math(w_ref[...], total, m_ref[...], v_ref[...])

    vm = pl.BlockSpec(memory_space=pltpu.VMEM)
    return pl.pallas_call(
        body, in_specs=[vm] * 4 + [ANY] * len(deps), out_specs=[vm] * 4, out_shape=[SDS((1, length), F32)] * 4,
        scratch_shapes=[pltpu.VMEM((8, 1, length), F32), pltpu.SemaphoreType.DMA((7,)), pltpu.SemaphoreType.DMA((7,))],
        compiler_params=pltpu.CompilerParams(has_side_effects=True), name="small_allreduce_adamw")(g, w, m, v, *deps)


def kernel(x, w_in, b_gate, norm_mix, norm_ffn, hgrn_lb_logits, hgrn_out_gain, q_gain, k_gain, rel_bias, w_proj_a, w_proj_b, w_out, w_ffn_in, w_ffn_out, loss_target, m_w_in, m_b_gate, m_norm_mix, m_norm_ffn, m_hgrn_lb_logits, m_hgrn_out_gain, m_q_gain, m_k_gain, m_rel_bias, m_w_proj_a, m_w_proj_b, m_w_out, m_w_ffn_in, m_w_ffn_out, v_w_in, v_b_gate, v_norm_mix, v_norm_ffn, v_hgrn_lb_logits, v_hgrn_out_gain, v_q_gain, v_k_gain, v_rel_bias, v_w_proj_a, v_w_proj_b, v_w_out, v_w_ffn_in, v_w_ffn_out):
    t, d = x.shape[1], x.shape[2]
    d_a = hgrn_out_gain.shape[1]
    h_a = d_a // HEAD
    h_b = rel_bias.shape[1]
    d_b = h_b * HEAD
    x0 = x.reshape(t, d)
    target = loss_target.reshape(t, d)
    pos = jnp.stack([2 * lax.axis_index("x") + lax.axis_index("y"), lax.axis_index("c")]).astype(jnp.int32)

    names = ["w_in", "w_proj_a", "w_proj_b", "w_out", "w_ffn_in", "w_ffn_out"]
    big = dict(zip(names, [w_in[0], w_proj_a[0], w_proj_b[0], w_out[0], w_ffn_in[0], w_ffn_out[0]]))
    big_m = dict(zip(names, [m_w_in[0], m_w_proj_a[0], m_w_proj_b[0], m_w_out[0], m_w_ffn_in[0], m_w_ffn_out[0]]))
    big_v = dict(zip(names, [v_w_in[0], v_w_proj_a[0], v_w_proj_b[0], v_w_out[0], v_w_ffn_in[0], v_w_ffn_out[0]]))
    kind = dict(zip(names, ["col", "col", "col", "row", "col", "row"]))
    shape = {nm: big[nm].shape for nm in names}

    def gather_start(tag, group, deps):
        plan, n = _plan_gather_ici([kind[g] for g in group], [shape[g] for g in group])
        fulls = [_cast_into_full(big[g], kind[g], pos, "cast_" + g) for g in group]
        send, recv, bufs, token = _start_copies("gather_ici_start_" + tag, fulls, plan, n, deps)
        return (tag, group, plan, send, recv, bufs), token

    def gather_pass(state, after):
        tag, group, plan, send, recv, bufs = state
        bufs = _wait_copies("gather_ici_wait_" + tag, bufs, send, recv, plan, after)
        plan, n = _plan_gather_pass([kind[g] for g in group], [shape[g] for g in group])
        send, recv, bufs, token = _start_copies("gather_pass_start_" + tag, bufs, plan, n, ())
        return (tag, group, plan, send, recv, bufs), token

    def gather_done(state, after):
        tag, group, plan, send, recv, bufs = state
        return _wait_copies("gather_pass_wait_" + tag, bufs, send, recv, plan, after)

    def reduce_start(tag, group, xs, dys):
        firsts = [_dw_half(xv, dyv, kind[g], shape[g], pos, None, "dw_first_" + g) for g, xv, dyv in zip(group, xs, dys)]
        lands = [lax.empty(fh.shape, F32) for fh in firsts]
        plan, n = _plan_pair(len(group))
        send, recv, bufs, token = _start_copies("pair_start_" + tag, firsts + lands, plan, n, ())
        return dict(tag=tag, group=group, xs=xs, dys=dys, plan=plan, send=send, recv=recv, bufs=bufs), token

    def reduce_pair_done(st, after):
        tag, group, nw = st["tag"], st["group"], len(st["group"])
        bufs = _wait_copies("pair_wait_" + tag, st["bufs"], st["send"], st["recv"], st["plan"], after)
        parts = [_dw_half(xv, dyv, kind[g], shape[g], pos, got, "dw_second_" + g)
                 for g, xv, dyv, got in zip(group, st["xs"], st["dys"], bufs[nw:])]
        lands = [lax.empty((3,) + p16.shape[1:], BF16) for _, p16 in parts]
        plan, n = _plan_chip(nw)
        send, recv, bufs, token = _start_copies("chip_start_" + tag, [p16 for _, p16 in parts] + lands, plan, n, ())
        return dict(st, plan=plan, send=send, recv=recv, bufs=bufs, p32=[p32 for p32, _ in parts]), token

    def reduce_chip_done(st, after):
        tag, group, nw = st["tag"], st["group"], len(st["group"])
        bufs = _wait_copies("chip_wait_" + tag, st["bufs"], st["send"], st["recv"], st["plan"], after)
        finals = [_chip_add(p32, got, kind[nm], shape[nm], pos, "chip_add_" + nm)
                  for p32, got, nm in zip(st["p32"], bufs[nw:], group)]
        plan, n = _plan_share([kind[g] for g in group])
        send, recv, bufs, token = _start_copies("share_start_" + tag, finals, plan, n, ())
        return dict(st, plan=plan, send=send, recv=recv, bufs=bufs), token

    g_big, upd = {}, {}

    def reduce_finish(st, after):
        bufs = _wait_copies("share_wait_" + st["tag"], st["bufs"], st["send"], st["recv"], st["plan"], after)
        for full, nm in zip(bufs, st["group"]):
            upd[nm] = _adamw(big[nm], full, big_m[nm], big_v[nm], "adamw_" + nm)
            g_big[nm] = upd[nm][3]

    ga, token = gather_start("a", ["w_in"], ())
    gb, token = gather_start("b", ["w_proj_a", "w_proj_b", "w_out"], (token,))
    gc, token = gather_start("c", ["w_ffn_in"], (token,))
    gd, token = gather_start("d", ["w_ffn_out"], (token,))
    h1, r1 = _rmsnorm_fwd(x0, norm_mix, "rmsnorm_mix")
    rb = jnp.pad(rel_bias[0], ((0, 0), (0, REL_LANES - N_REL)))
    bias = _relbias_expand(rb).transpose(1, 0, 2)
    ga, token = gather_pass(ga, (h1, bias, token))
    (wg_in,) = gather_done(ga, ())
    proj = _matmul(h1, wg_in, name="proj_in")
    y_a, o_pre, states = _hgrn_fwd(proj, hgrn_lb_logits, hgrn_out_gain, h_a)
    gb, token = gather_pass(gb, (y_a,))
    col_b = 4 * d_a // HEAD
    y_b = _attn_fwd(proj, q_gain, k_gain, bias, h_b, col_b)
    wg_pa, wg_pb, wg_out = gather_done(gb, (y_b,))
    pa = _matmul(y_a, wg_pa, name="proj_a", deps=(token,))
    pb = _matmul(y_b, wg_pb, name="proj_b")
    gate_off = 4 * d_a + 3 * d_b
    merged = _merge_fwd(proj, b_gate, pa, pb, gate_off)
    x2 = _matmul(merged, wg_out, res=x0, name="out_proj")
    gc, token = gather_pass(gc, (x2,))
    h2, r2 = _rmsnorm_fwd(x2, norm_ffn, "rmsnorm_ffn")
    (wg_fin,) = gather_done(gc, (h2,))
    ff_gate, ff_up, act = _ffn_in_swiglu(h2, wg_fin, deps=(token,))
    gd, token = gather_pass(gd, (act,))
    (wg_fout,) = gather_done(gd, ())
    y = _matmul(act, wg_fout, res=x2, name="ffn_out")
    dy, dy16, loss_part = _loss_head(y, target)

    r_fout, token = reduce_start("fout", ["w_ffn_out"], [act], [dy16])
    dact = _matmul(dy16, wg_fout, tb=True, name="d_act", deps=(token,))
    r_fout, token = reduce_pair_done(r_fout, (dact,))
    dgu = _swiglu_bwd(dact, ff_gate, ff_up)
    r_fin, token_b = reduce_start("fin", ["w_ffn_in"], [h2], [dgu])
    dh2 = _matmul(dgu, wg_fin, tb=True, name="d_h2", deps=(token, token_b))
    r_fout, token_a = reduce_chip_done(r_fout, (dh2,))
    r_fin, token_b = reduce_pair_done(r_fin, (dh2,))
    dx2, dx2_16, g_norm_ffn = _rmsnorm_bwd(dh2, x2, r2, norm_ffn, dy, "rmsnorm_ffn_bwd")
    dmerged = _matmul(dx2_16, wg_out, tb=True, name="d_merged", deps=(token_a, token_b))
    dp_ab, dproj, g_bgate = _merge_bwd(dmerged, proj, b_gate, pa, pb, gate_off)
    r_mid, token = reduce_start("mid", ["w_proj_a", "w_proj_b", "w_out"], [y_a, y_b, merged],
                                [dp_ab[0], dp_ab[1], dx2_16])
    dy_a = _matmul(dp_ab[0], wg_pa, tb=True, name="d_y_a", deps=(token,))
    dy_b = _matmul(dp_ab[1], wg_pb, tb=True, name="d_y_b")
    r_fin, token_a = reduce_chip_done(r_fin, (dy_b,))
    r_mid, token_b = reduce_pair_done(r_mid, (dy_b,))
    dproj, dbias, g_qg, g_kg = _attn_bwd(dproj, proj, q_gain, k_gain, bias, dy_b, h_b, col_b, deps=(token_a, token_b))
    r_mid, token = reduce_chip_done(r_mid, (dbias,))
    dproj, g_lb, g_gain = _hgrn_bwd(dproj, proj, o_pre, states, dy_a, hgrn_lb_logits, hgrn_out_gain, h_a, deps=(token,))
    r_in, token = reduce_start("in", ["w_in"], [h1], [dproj])
    dh1 = _matmul(dproj, wg_in, tb=True, name="d_h1", deps=(token,))
    r_in, token = reduce_pair_done(r_in, (dh1,))
    dx, _, g_norm_mix = _rmsnorm_bwd(dh1, x0, r1, norm_mix, dx2, "rmsnorm_mix_bwd")
    g_rb = _relbias_reduce(dbias.transpose(1, 0, 2))[:, :N_REL]
    reduce_finish(r_mid, (dx, token))
    reduce_finish(r_fin, (dx, token))
    reduce_finish(r_fout, (dx, token))
    r_in, token = reduce_chip_done(r_in, (g_rb, upd["w_ffn_in"][0], upd["w_ffn_out"][0], upd["w_out"][0]))

    small_w = [b_gate, norm_mix, norm_ffn, hgrn_lb_logits, hgrn_out_gain, q_gain, k_gain, rel_bias]
    small_m = [m_b_gate, m_norm_mix, m_norm_ffn, m_hgrn_lb_logits, m_hgrn_out_gain, m_q_gain, m_k_gain, m_rel_bias]
    small_v = [v_b_gate, v_norm_mix, v_norm_ffn, v_hgrn_lb_logits, v_hgrn_out_gain, v_q_gain, v_k_gain, v_rel_bias]
    small_g = [g_bgate, g_norm_mix, g_norm_ffn, g_lb, g_gain, g_qg, g_kg, g_rb]
    sizes = [w.size for w in small_w]
    length = -(-(sum(sizes) + 1) // 128) * 128

    def pack(parts_):
        flat = jnp.concatenate([p.reshape(1, -1) for p in parts_], axis=1)
        return jnp.pad(flat, ((0, 0), (0, length - flat.shape[1])))

    one = jnp.ones((1, 1), F32)
    packed = _small_allreduce_adamw(pack(small_g + [loss_part]), pack(small_w + [one]), pack(small_m + [one]),
                                    pack(small_v + [one]), deps=(token,))

    def unpack(vec):
        out, at = [], 0
        for w, n in zip(small_w, sizes):
            out.append(vec[0, at:at + n].reshape(w.shape))
            at += n
        return out, vec[0, at]

    (sg, loss), (sd, _), (sm, _), (sv, _) = [unpack(p) for p in packed]
    reduce_finish(r_in, (packed[0],))

    def ordered(small, bigs):
        bigs = [bigs[nm][None] for nm in names]
        return [bigs[0]] + small + bigs[1:]

    return (loss, dx.reshape(x.shape), *ordered(sg, g_big), *ordered(sd, {nm: upd[nm][0] for nm in names}),
            *ordered(sm, {nm: upd[nm][1] for nm in names}), *ordered(sv, {nm: upd[nm][2] for nm in names}))
```

```python
import functools

import jax
import jax.numpy as jnp
from jax import lax
from jax.experimental import pallas as pl
from jax.experimental.pallas import tpu as pltpu

F32 = jnp.float32
BF16 = jnp.bfloat16
SDS = jax.ShapeDtypeStruct
MESH = pl.DeviceIdType.MESH
HIGHEST = lax.Precision.HIGHEST

CHUNK = 64
SUB = 16
HEAD = 128
N_PAST = 8
BAND = (N_PAST + 1) * CHUNK
PAD = N_PAST * CHUNK
REL_FUTURE = CHUNK - 1
REL_PAST = 2 * CHUNK - 1
N_REL = REL_FUTURE + REL_PAST + 1
REL_LANES = 256
EPS = 1e-6
MIX_HEADS = 2
MIX_UNROLL = 4
MIX_UNROLL_BWD = 4
ATT_UNROLL = 8
ATT_UNROLL_BWD = 4
EXP_CLAMP = 80.0

ADAM_LR = 0.001
ADAM_B1 = 0.9
ADAM_B2 = 0.999
ADAM_EPS = 1e-08
ADAM_WD = 0.01
ADAM_STEP = 10

VMEM_LIMIT = 56 * 1024 * 1024

HBM = pl.BlockSpec(memory_space=pltpu.HBM)
ANY = pl.BlockSpec(memory_space=pl.ANY)
SEM = pl.BlockSpec(memory_space=pltpu.SEMAPHORE)

NT = (((1,), (1,)), ((), ()))
TN = (((0,), (0,)), ((), ()))
NN = (((1,), (0,)), ((), ()))


def _params(sem=None, **kw):
    return pltpu.CompilerParams(dimension_semantics=sem, vmem_limit_bytes=VMEM_LIMIT, **kw)


def _tile(n, pref, unit=128):
    if n <= pref:
        return n
    t = pref - pref % unit
    while n % t:
        t -= unit
    return t


def _loop(n, unroll, step, init):
    assert n % unroll == 0, (n, unroll)

    def several(i, carry):
        for u in range(unroll):
            carry = step(i * unroll + u, carry)
        return carry

    return lax.fori_loop(0, n // unroll, several, init)


def _sigmoid(x):
    return 1.0 / (1.0 + jnp.exp(-x))


def _dsilu(x, s):
    return s * (1.0 + x * (1.0 - s))


def _bdot(a, b, dims=NN):
    return lax.dot_general(a.astype(BF16), b.astype(BF16), dims, preferred_element_type=F32)


def _split(a):
    hi = a.astype(BF16)
    return hi, (a - hi.astype(F32)).astype(BF16)


def _dot3(a, b, dims):
    dot = lambda u, v: lax.dot_general(u, v, dims, preferred_element_type=F32)
    return dot(a[0], b[1]) + dot(a[1], b[0]) + dot(a[0], b[0])


def _fdot(a, b):
    return lax.dot_general(a, b, NN, precision=HIGHEST, preferred_element_type=F32)


MM_TILE_K = 5632
MM_TILE_N = 512
MM_RESIDENT_OUT = 16 * 1024 * 1024
MM_LONG_TILE_K = 704


def _matmul_long_k(a, b, out_dtype, name, deps):
    m, k = a.shape
    n = b.shape[0]
    tk = _tile(k, MM_LONG_TILE_K)

    def body(a_ref, b_ref, *rest):
        o_ref = rest[len(deps)]
        part = lax.dot_general(a_ref[...].astype(BF16), b_ref[...].astype(BF16), NT, preferred_element_type=F32)

        @pl.when(pl.program_id(0) == 0)
        def _():
            o_ref[...] = part.astype(o_ref.dtype)

        @pl.when(pl.program_id(0) > 0)
        def _():
            o_ref[...] += part.astype(o_ref.dtype)

    assert out_dtype == F32
    return pl.pallas_call(
        body, grid=(k // tk,),
        in_specs=[pl.BlockSpec((m, tk), lambda l: (0, l)), pl.BlockSpec((n, tk), lambda l: (0, l))] + [ANY] * len(deps),
        out_specs=pl.BlockSpec((m, n), lambda l: (0, 0)), out_shape=SDS((m, n), out_dtype),
        compiler_params=_params(("arbitrary",)), name=name)(a, b, *deps)


def _matmul(a, b, *, ta=False, tb=False, res=None, out_dtype=F32, name, deps=()):
    m, k = (a.shape[1], a.shape[0]) if ta else a.shape
    n = b.shape[0] if tb else b.shape[1]
    if tb and not ta and res is None and k > MM_TILE_K and m * n * 4 <= MM_RESIDENT_OUT:
        return _matmul_long_k(a, b, out_dtype, name, deps)
    tk = _tile(k, MM_TILE_K)
    nk = k // tk
    tm, tn = _tile(m, 2048 if tk <= MM_TILE_K // 2 else 1024), _tile(n, MM_TILE_N)
    dims = ((((0,) if ta else (1,)), ((1,) if tb else (0,))), ((), ()))

    def body(*refs):
        n_in = 2 + (res is not None)
        a_ref, b_ref = refs[:2]
        r_ref = refs[2] if res is not None else None
        o_ref = refs[n_in + len(deps)]
        part = lax.dot_general(a_ref[...].astype(BF16), b_ref[...].astype(BF16), dims, preferred_element_type=F32)

        def finish(out):
            if r_ref is not None:
                out = out + r_ref[...]
            o_ref[...] = out.astype(o_ref.dtype)

        if nk == 1:
            finish(part)
            return
        acc_ref = refs[-1]
        kk = pl.program_id(2)

        @pl.when(kk == 0)
        def _():
            acc_ref[...] = part

        @pl.when(jnp.logical_and(kk > 0, kk < nk - 1))
        def _():
            acc_ref[...] += part

        @pl.when(kk == nk - 1)
        def _():
            finish(acc_ref[...] + part)

    a_spec = pl.BlockSpec((tk, tm), lambda i, j, l: (l, i)) if ta else pl.BlockSpec((tm, tk), lambda i, j, l: (i, l))
    b_spec = pl.BlockSpec((tn, tk), lambda i, j, l: (j, l)) if tb else pl.BlockSpec((tk, tn), lambda i, j, l: (l, j))
    o_spec = pl.BlockSpec((tm, tn), lambda i, j, l: (i, j))
    in_specs = [a_spec, b_spec] + ([o_spec] if res is not None else []) + [ANY] * len(deps)
    args = (a, b) + ((res,) if res is not None else ()) + tuple(deps)
    return pl.pallas_call(
        body, grid=(m // tm, n // tn, nk), in_specs=in_specs, out_specs=o_spec,
        out_shape=SDS((m, n), out_dtype), scratch_shapes=[pltpu.VMEM((tm, tn), F32)] if nk > 1 else [],
        compiler_params=_params(("parallel", "parallel", "arbitrary")), name=name)(*args)


def _cast_into_full(w, kind, pos, name):
    r, n = w.shape
    tr = _tile(r, 512, 16)
    nr = r // tr
    if kind == "col":
        shape, o_spec = (r, 4 * n), pl.BlockSpec((tr, n), lambda i, p: (i, p[0]))
    else:
        shape, o_spec = (4 * r, n), pl.BlockSpec((tr, n), lambda i, p: (p[0] * nr + i, 0))

    def body(p_ref, w_ref, o_ref):
        o_ref[...] = w_ref[...].astype(BF16)

    return pl.pallas_call(
        body,
        grid_spec=pltpu.PrefetchScalarGridSpec(num_scalar_prefetch=1, grid=(nr,),
                                               in_specs=[pl.BlockSpec((tr, n), lambda i, p: (i, 0))], out_specs=o_spec),
        out_shape=SDS(shape, BF16), compiler_params=_params(("parallel",)), name=name)(pos, w)


def _rmsnorm_fwd(x, gain, name):
    t, d = x.shape
    tm = _tile(t, 256)

    def body(x_ref, g_ref, h_ref, r_ref):
        xv = x_ref[...]
        r = lax.rsqrt(jnp.mean(xv * xv, axis=-1, keepdims=True) + EPS)
        h_ref[...] = (xv * r * g_ref[...]).astype(BF16)
        r_ref[...] = r

    return pl.pallas_call(
        body, grid=(t // tm,),
        in_specs=[pl.BlockSpec((tm, d), lambda i: (i, 0)), pl.BlockSpec((1, d), lambda i: (0, 0))],
        out_specs=[pl.BlockSpec((tm, d), lambda i: (i, 0)), pl.BlockSpec((tm, 1), lambda i: (i, 0))],
        out_shape=[SDS((t, d), BF16), SDS((t, 1), F32)], compiler_params=_params(("parallel",)), name=name)(x, gain)


def _rmsnorm_bwd(dh, x, r, gain, dres, name):
    t, d = x.shape
    tm = _tile(t, 256)

    def body(dh_ref, x_ref, r_ref, g_ref, dres_ref, dx_ref, dxb_ref, dg_ref):
        @pl.when(pl.program_id(0) == 0)
        def _():
            dg_ref[...] = jnp.zeros_like(dg_ref)

        dhv, xv, rv = dh_ref[...], x_ref[...], r_ref[...]
        dg_ref[...] += jnp.sum(dhv * (xv * rv), axis=0, keepdims=True)
        u = dhv * g_ref[...]
        dx = dres_ref[...] + rv * u - xv * (rv * rv * rv) * jnp.mean(u * xv, axis=-1, keepdims=True)
        dx_ref[...] = dx
        dxb_ref[...] = dx.astype(BF16)

    row = pl.BlockSpec((tm, d), lambda i: (i, 0))
    vec = pl.BlockSpec((1, d), lambda i: (0, 0))
    return pl.pallas_call(
        body, grid=(t // tm,), in_specs=[row, row, pl.BlockSpec((tm, 1), lambda i: (i, 0)), vec, row],
        out_specs=[row, row, vec], out_shape=[SDS((t, d), F32), SDS((t, d), BF16), SDS((1, d), F32)],
        compiler_params=_params(("arbitrary",)), name=name)(dh, x, r, gain, dres)


def _merge_fwd(proj, b_gate, pa, pb, off):
    t, d = pa.shape
    tm, tc = _tile(t, 512), _tile(d, 512)
    nj = d // tc
    oa, ob = off // tc, off // tc + nj

    def body(la_ref, lb_ref, ba_ref, bb_ref, pa_ref, pb_ref, o_ref):
        ga = _sigmoid(la_ref[...] + ba_ref[...])
        gb = _sigmoid(lb_ref[...] + bb_ref[...])
        o_ref[...] = (ga * pa_ref[...] + gb * pb_ref[...]).astype(BF16)

    tile = pl.BlockSpec((tm, tc), lambda i, j: (i, j))
    return pl.pallas_call(
        body, grid=(t // tm, nj),
        in_specs=[pl.BlockSpec((tm, tc), lambda i, j: (i, oa + j)), pl.BlockSpec((tm, tc), lambda i, j: (i, ob + j)),
                  pl.BlockSpec((1, tc), lambda i, j: (0, j)), pl.BlockSpec((1, tc), lambda i, j: (0, nj + j)), tile, tile],
        out_specs=tile, out_shape=SDS((t, d), BF16), compiler_params=_params(("parallel", "parallel")),
        name="merge_fwd")(proj, proj, b_gate, b_gate, pa, pb)


def _merge_bwd(dmerged, proj, b_gate, pa, pb, off):
    t, d = pa.shape
    tm, tc = _tile(t, 512), _tile(d, 512)
    nj, ni = d // tc, t // tm
    o0 = off // tc

    def body(dm_ref, l_ref, b_ref, pa_ref, pb_ref, dp_ref, dl_ref, db_ref):
        s, i = pl.program_id(0), pl.program_id(2)
        p = jnp.where(s == 0, pa_ref[...], pb_ref[...])
        g = _sigmoid(l_ref[...] + b_ref[...])
        dm = dm_ref[...]
        dp_ref[0] = (dm * g).astype(BF16)
        dl = dm * p * g * (1.0 - g)
        dl_ref[...] = dl.astype(BF16)

        @pl.when(i == 0)
        def _():
            db_ref[...] = jnp.zeros_like(db_ref)

        db_ref[...] += jnp.sum(dl, axis=0, keepdims=True)

    tile = pl.BlockSpec((tm, tc), lambda s, j, i: (i, j))
    return pl.pallas_call(
        body, grid=(2, nj, ni),
        in_specs=[tile, pl.BlockSpec((tm, tc), lambda s, j, i: (i, o0 + s * nj + j)),
                  pl.BlockSpec((1, tc), lambda s, j, i: (0, s * nj + j)), tile, tile],
        out_specs=[pl.BlockSpec((1, tm, tc), lambda s, j, i: (s, i, j)),
                   pl.BlockSpec((tm, tc), lambda s, j, i: (i, o0 + s * nj + j)),
                   pl.BlockSpec((1, tc), lambda s, j, i: (0, s * nj + j))],
        out_shape=[SDS((2, t, d), BF16), SDS(proj.shape, BF16), SDS((1, 2 * d), F32)],
        compiler_params=_params(("arbitrary", "arbitrary", "arbitrary")),
        name="merge_bwd")(dmerged, proj, b_gate, pa, pb)


def _ffn_in_swiglu(h, w, deps=()):
    t, d = h.shape
    f = w.shape[1] // 2
    tm, tn = _tile(t, 2048), _tile(f, MM_TILE_N)
    nj = f // tn

    def body(h_ref, wg_ref, wu_ref, *rest):
        g_ref, u_ref, a_ref = rest[len(deps):]
        hv = h_ref[...]
        g = jnp.dot(hv, wg_ref[...], preferred_element_type=F32)
        u = jnp.dot(hv, wu_ref[...], preferred_element_type=F32)
        g_ref[...] = g
        u_ref[...] = u
        a_ref[...] = (g * _sigmoid(g) * u).astype(BF16)

    tile = pl.BlockSpec((tm, tn), lambda i, j: (i, j))
    return pl.pallas_call(
        body, grid=(t // tm, nj),
        in_specs=[pl.BlockSpec((tm, d), lambda i, j: (i, 0)), pl.BlockSpec((d, tn), lambda i, j: (0, j)),
                  pl.BlockSpec((d, tn), lambda i, j: (0, nj + j))] + [ANY] * len(deps),
        out_specs=[tile, tile, tile], out_shape=[SDS((t, f), F32), SDS((t, f), F32), SDS((t, f), BF16)],
        compiler_params=_params(("parallel", "parallel")), name="ffn_in_swiglu")(h, w, w, *deps)


def _swiglu_bwd(dact, gate, up):
    t, f = gate.shape
    tm = _tile(t, 128)

    def body(d_ref, g_ref, u_ref, o_ref):
        g, dv = g_ref[...], d_ref[...]
        sg = _sigmoid(g)
        o_ref[:, :f] = (dv * u_ref[...] * _dsilu(g, sg)).astype(BF16)
        o_ref[:, f:] = (dv * (g * sg)).astype(BF16)

    row = pl.BlockSpec((tm, f), lambda i: (i, 0))
    return pl.pallas_call(
        body, grid=(t // tm,), in_specs=[row, row, row],
        out_specs=pl.BlockSpec((tm, 2 * f), lambda i: (i, 0)), out_shape=SDS((t, 2 * f), BF16),
        compiler_params=_params(("parallel",)), name="swiglu_bwd")(dact, gate, up)


def _ffn_out_loss(act, w, x_res, target):
    t, d = x_res.shape
    k = act.shape[1]
    tm, tn = _tile(t, 1024), _tile(d, MM_TILE_N)

    def body(a_ref, w_ref, r_ref, t_ref, dy_ref, dyb_ref, l_ref):
        @pl.when(jnp.logical_and(pl.program_id(0) == 0, pl.program_id(1) == 0))
        def _():
            l_ref[...] = jnp.zeros_like(l_ref)

        y = jnp.dot(a_ref[...], w_ref[...], preferred_element_type=F32) + r_ref[...]
        e = y - t_ref[...]
        dy = e * (1.0 / d)
        dy_ref[...] = dy
        dyb_ref[...] = dy.astype(BF16)
        l_ref[...] += (0.5 / d) * jnp.sum(jnp.sum(e * e, axis=-1, keepdims=True), axis=0, keepdims=True)

    tile = pl.BlockSpec((tm, tn), lambda i, j: (i, j))
    return pl.pallas_call(
        body, grid=(t // tm, d // tn),
        in_specs=[pl.BlockSpec((tm, k), lambda i, j: (i, 0)), pl.BlockSpec((k, tn), lambda i, j: (0, j)), tile, tile],
        out_specs=[tile, tile, pl.BlockSpec((1, 1), lambda i, j: (0, 0))],
        out_shape=[SDS((t, d), F32), SDS((t, d), BF16), SDS((1, 1), F32)],
        compiler_params=_params(("arbitrary", "arbitrary")), name="ffn_out_loss")(act, w, x_res, target)


def _rel_onehot(qi):
    p = lax.broadcasted_iota(jnp.int32, (REL_LANES, BAND), 1)
    r = lax.broadcasted_iota(jnp.int32, (REL_LANES, BAND), 0)
    idx = jnp.clip(qi + PAD - p, -REL_FUTURE, REL_PAST) + REL_FUTURE
    return (idx == r).astype(F32)


def _relbias_expand(rb):
    h = rb.shape[0]

    def body(rb_ref, o_ref):
        def step(qi, _):
            o_ref[qi] = _fdot(rb_ref[...], _rel_onehot(qi))
            return 0

        lax.fori_loop(0, CHUNK, step, 0)

    return pl.pallas_call(body, out_shape=SDS((CHUNK, h, BAND), F32), compiler_params=_params(),
                          name="relbias_expand")(rb)


def _relbias_reduce(dbias):
    h = dbias.shape[1]

    def body(db_ref, o_ref):
        def step(qi, acc):
            return acc + lax.dot_general(db_ref[qi], _rel_onehot(qi), NT, precision=HIGHEST,
                                         preferred_element_type=F32)

        o_ref[...] = lax.fori_loop(0, CHUNK, step, jnp.zeros((h, REL_LANES), F32))

    return pl.pallas_call(body, out_shape=SDS((h, REL_LANES), F32), compiler_params=_params(),
                          name="relbias_reduce")(dbias)


def _lower_bound(l_ref):
    l0, l1 = l_ref[0:1, :], l_ref[1:2, :]
    m = jnp.maximum(l0, l1)
    e0, e1 = jnp.exp(l0 - m), jnp.exp(l1 - m)
    return e0 / (e0 + e1)


def _tri(lower):
    r = lax.broadcasted_iota(jnp.int32, (CHUNK, CHUNK), 0)
    c = lax.broadcasted_iota(jnp.int32, (CHUNK, CHUNK), 1)
    return r >= c if lower else r <= c


def _hgrn_intra(qs, kk, b_s):
    rows = lax.broadcasted_iota(jnp.int32, (CHUNK, HEAD), 0)
    b = b_s[...]
    out = []
    for i in range(CHUNK // SUB):
        lo = i * SUB
        ref = jnp.zeros((1, HEAD), F32) if i == 0 else b_s[lo - 1:lo, :]
        eq = jnp.exp(b[lo:lo + SUB] - ref)
        qt = _split(qs[lo:lo + SUB] * eq)
        e = jnp.where(rows < lo + SUB, jnp.exp(jnp.minimum(ref - b, EXP_CLAMP)), 0.0)
        kt = _split(kk * e)
        out.append((eq, qt, e, kt))
    return out


def _hgrn_scores(blocks):
    tr = lax.broadcasted_iota(jnp.int32, (SUB, CHUNK), 0)
    tc = lax.broadcasted_iota(jnp.int32, (SUB, CHUNK), 1)
    return jnp.concatenate([jnp.where(tc <= tr + i * SUB, _dot3(qt, kt, NT), 0.0)
                            for i, (_, qt, _, kt) in enumerate(blocks)], axis=0)


def _hgrn_fwd(proj, lb_logits, gain, n_heads):
    t = proj.shape[0]
    nc = t // CHUNK
    da = n_heads * HEAD
    hp = MIX_HEADS
    wide = hp * HEAD

    def body(q_ref, f_ref, i_ref, g_ref, l_ref, gain_ref, y_ref, o_ref, st_ref, state, b_s):
        state[...] = jnp.zeros_like(state)
        lb_all = _lower_bound(l_ref)
        tril = _tri(True).astype(F32)

        def chunks(i, _):
            dot = functools.partial(lax.dot_general, preferred_element_type=F32)
            items = []
            for u in range(MIX_UNROLL):
                for hh in range(hp):
                    j = i * MIX_UNROLL + u
                    sl = pl.ds(pl.multiple_of(j * CHUNK, CHUNK), CHUNK)
                    cols = slice(hh * HEAD, (hh + 1) * HEAD)
                    lb = lb_all[:, cols]
                    fg = lb + (1.0 - lb) * _sigmoid(f_ref[sl, cols])
                    qv = q_ref[sl, cols]
                    gv = g_ref[sl, cols]
                    items.append(dict(hh=hh, j=j, sl=sl, cols=cols, lf=jnp.log(fg), kk=1.0 - fg, qs=qv * _sigmoid(qv),
                                      vb=i_ref[sl, cols].astype(BF16), gate=gv * _sigmoid(gv)))
            for it in items:
                it["b"] = _fdot(tril, it["lf"])
            for slot, it in enumerate(items):
                b = it["b"]
                b_s[slot] = b
                it["blocks"] = _hgrn_intra(it["qs"], it["kk"], b_s.at[slot])
                it["ebl"] = jnp.exp(b_s[slot, CHUNK - 1:CHUNK, :])
                it["qe"] = (it["qs"] * jnp.exp(b)).astype(BF16)
                it["kd"] = (it["kk"] * jnp.exp(b_s[slot, CHUNK - 1:CHUNK, :] - b)).astype(BF16)
            for it in items:
                it["a"] = _hgrn_scores(it["blocks"]).astype(BF16)
            for it in items:
                it["kv"] = dot(it["vb"], it["kd"], TN)
                it["o"] = dot(it["a"], it["vb"], NN)
            s_now = [state[hh] for hh in range(hp)]
            for it in items:
                it["s_in"] = s_now[it["hh"]]
                s_now[it["hh"]] = it["s_in"] * it["ebl"] + it["kv"]
            for hh in range(hp):
                state[hh] = s_now[hh]
            for it in items:
                it["o"] = it["o"] + dot(it["qe"], it["s_in"].astype(BF16), NT)
            for it in items:
                o, sl, cols = it["o"], it["sl"], it["cols"]
                st_ref[it["hh"], it["j"]] = it["s_in"]
                o_ref[sl, cols] = o
                rr = lax.rsqrt(jnp.mean(o * o, axis=-1, keepdims=True) + EPS)
                y_ref[sl, cols] = (o * rr * gain_ref[:, cols] * it["gate"]).astype(BF16)
            return 0

        assert nc % MIX_UNROLL == 0, (nc, MIX_UNROLL)
        lax.fori_loop(0, nc // MIX_UNROLL, chunks, 0)

    col = lambda k: pl.BlockSpec((t, wide), lambda h: (0, k * (n_heads // hp) + h))
    vec = pl.BlockSpec((1, wide), lambda h: (0, h))
    return pl.pallas_call(
        body, grid=(n_heads // hp,),
        in_specs=[col(0), col(1), col(2), col(3), pl.BlockSpec((2, wide), lambda h: (0, h)), vec],
        out_specs=[pl.BlockSpec((t, wide), lambda h: (0, h)), pl.BlockSpec((t, wide), lambda h: (0, h)),
                   pl.BlockSpec((hp, nc, HEAD, HEAD), lambda h: (h, 0, 0, 0))],
        out_shape=[SDS((t, da), BF16), SDS((t, da), F32), SDS((n_heads, nc, HEAD, HEAD), F32)],
        scratch_shapes=[pltpu.VMEM((hp, HEAD, HEAD), F32), pltpu.VMEM((hp * MIX_UNROLL, CHUNK, HEAD), F32)],
        compiler_params=_params(("parallel",)), name="hgrn_fwd")(proj, proj, proj, proj, lb_logits, gain)


def _hgrn_bwd(dproj, proj, o_pre, states, dy, lb_logits, gain, n_heads, deps=()):
    t = proj.shape[0]
    nc = t // CHUNK
    da = n_heads * HEAD
    hp = MIX_HEADS
    wide = hp * HEAD

    def body(*refs):
        (q_ref, f_ref, i_ref, g_ref, o_ref, st_ref, dy_ref, l_ref, gain_ref,
         dproj_ref, dl_ref, dgain_ref, res, dstate, b_s) = refs[1 + len(deps):]

        @pl.when(pl.program_id(1) == 0)
        def _():
            dstate[...] = jnp.zeros_like(dstate)
            lb_all = _lower_bound(l_ref)
            tril_m, tril, triu = _tri(True), _tri(True).astype(F32), _tri(False).astype(F32)
            last = lax.broadcasted_iota(jnp.int32, (CHUNK, HEAD), 0) == CHUNK - 1

            def chunks(i, carry):
                dot = functools.partial(lax.dot_general, preferred_element_type=F32)
                items = []
                for u in range(MIX_UNROLL_BWD):
                    for hh in range(hp):
                        j = nc - 1 - (i * MIX_UNROLL_BWD + u)
                        sl = pl.ds(pl.multiple_of(j * CHUNK, CHUNK), CHUNK)
                        cols = slice(hh * HEAD, (hh + 1) * HEAD)
                        lb, gain_v = lb_all[:, cols], gain_ref[:, cols]
                        sg = _sigmoid(f_ref[sl, cols])
                        fg = lb + (1.0 - lb) * sg
                        qv = q_ref[sl, cols]
                        sq = _sigmoid(qv)
                        gv = g_ref[sl, cols]
                        sgg = _sigmoid(gv)
                        silg = gv * sgg
                        o = o_ref[sl, cols]
                        dyv = dy_ref[sl, cols]
                        rr = lax.rsqrt(jnp.mean(o * o, axis=-1, keepdims=True) + EPS)
                        on = o * rr
                        don = dyv * gain_v * silg
                        do = (rr * don - o * (rr * rr * rr) * jnp.mean(don * o, axis=-1, keepdims=True)).astype(BF16)
                        items.append(dict(
                            hh=hh, j=j, sl=sl, cols=cols, lb=lb, sg=sg, fg=fg, kk=1.0 - fg, qv=qv, sq=sq, qs=qv * sq,
                            vb=i_ref[sl, cols].astype(BF16), do=do, dg=dyv * on * gain_v * _dsilu(gv, sgg),
                            dgain=jnp.sum(dyv * on * silg, axis=0, keepdims=True)))
                for it in items:
                    it["b"] = _fdot(tril, jnp.log(it["fg"]))
                for slot, it in enumerate(items):
                    b = it["b"]
                    b_s[slot] = b
                    it["blocks"] = _hgrn_intra(it["qs"], it["kk"], b_s.at[slot])
                    bl = b_s[slot, CHUNK - 1:CHUNK, :]
                    it["eb"], it["ebl"], it["ekd"] = jnp.exp(b), jnp.exp(bl), jnp.exp(bl - b)
                    it["s_in"] = st_ref[it["hh"], it["j"]]
                for it in items:
                    it["a"] = _hgrn_scores(it["blocks"]).astype(BF16)
                    it["da"] = jnp.where(tril_m, dot(it["do"], it["vb"], NT), 0.0)
                for it in items:
                    dq_rows = []
                    dk = jnp.zeros((CHUNK, HEAD), F32)
                    for blk, (eq, qt, e, kt) in enumerate(it["blocks"]):
                        da_i = _split(it["da"][blk * SUB:(blk + 1) * SUB])
                        dq_rows.append(eq * _dot3(da_i, kt, NN))
                        dk = dk + e * _dot3(da_i, qt, TN)
                    it["dq"] = jnp.concatenate(dq_rows, axis=0) + dot(it["do"], it["s_in"].astype(BF16), NN) * it["eb"]
                    it["dk"] = dk
                    it["dv"] = dot(it["a"], it["do"], TN)
                    it["g"] = dot(it["do"], (it["qs"] * it["eb"]).astype(BF16), TN)
                ds_now = [dstate[hh] for hh in range(hp)]
                for it in items:
                    it["ds_out"] = ds_now[it["hh"]]
                    ds_now[it["hh"]] = it["ds_out"] * it["ebl"] + it["g"]
                for hh in range(hp):
                    dstate[hh] = ds_now[hh]
                for it in items:
                    dsb = it["ds_out"].astype(BF16)
                    it["dv"] = it["dv"] + dot((it["kk"] * it["ekd"]).astype(BF16), dsb, NT)
                    it["dk_state"] = it["ekd"] * dot(it["vb"], dsb, NN)
                for it in items:
                    kk, dk_state = it["kk"], it["dk_state"]
                    it["dk"] = it["dk"] + dk_state
                    extra = (jnp.sum(kk * dk_state, axis=0, keepdims=True)
                             + it["ebl"] * jnp.sum(it["s_in"] * it["ds_out"], axis=0, keepdims=True))
                    it["db"] = it["qs"] * it["dq"] - kk * it["dk"] + jnp.where(last, extra, 0.0)
                for it in items:
                    it["dlf"] = _fdot(triu, it["db"])
                carry = list(carry)
                for it in items:
                    hh, sl, cols, sg, lb = it["hh"], it["sl"], it["cols"], it["sg"], it["lb"]
                    dfg = it["dlf"] / it["fg"] - it["dk"]
                    dlb_acc, dgain_acc = carry[hh]
                    carry[hh] = (dlb_acc + jnp.sum(dfg * (1.0 - sg), axis=0, keepdims=True), dgain_acc + it["dgain"])
                    res[0, sl, cols] = (it["dq"] * _dsilu(it["qv"], it["sq"])).astype(BF16)
                    res[1, sl, cols] = (dfg * (1.0 - lb) * sg * (1.0 - sg)).astype(BF16)
                    res[2, sl, cols] = it["dv"].astype(BF16)
                    res[3, sl, cols] = it["dg"].astype(BF16)
                return tuple(carry)

            assert nc % MIX_UNROLL_BWD == 0, (nc, MIX_UNROLL_BWD)
            zero = jnp.zeros((1, HEAD), F32)
            sums = lax.fori_loop(0, nc // MIX_UNROLL_BWD, chunks, ((zero, zero),) * hp)
            for hh, (dlb, dgain) in enumerate(sums):
                cols = slice(hh * HEAD, (hh + 1) * HEAD)
                lb = lb_all[:, cols]
                dgain_ref[:, cols] = dgain
                dl0 = dlb * lb * (1.0 - lb)
                dl_ref[0:1, cols] = dl0
                dl_ref[1:2, cols] = -dl0

        dproj_ref[...] = res[pl.program_id(1)]

    ng = n_heads // hp
    col = lambda k: pl.BlockSpec((t, wide), lambda h, p: (0, k * ng + h))
    head = pl.BlockSpec((t, wide), lambda h, p: (0, h))
    vec = pl.BlockSpec((1, wide), lambda h, p: (0, h))
    return pl.pallas_call(
        body, grid=(ng, 4),
        in_specs=[ANY] * (1 + len(deps)) + [col(0), col(1), col(2), col(3), head,
                  pl.BlockSpec((hp, nc, HEAD, HEAD), lambda h, p: (h, 0, 0, 0)),
                  head, pl.BlockSpec((2, wide), lambda h, p: (0, h)), vec],
        out_specs=[pl.BlockSpec((t, wide), lambda h, p: (0, p * ng + h)),
                   pl.BlockSpec((2, wide), lambda h, p: (0, h)), vec],
        out_shape=[SDS(dproj.shape, BF16), SDS((2, da), F32), SDS((1, da), F32)],
        scratch_shapes=[pltpu.VMEM((4, t, wide), BF16), pltpu.VMEM((hp, HEAD, HEAD), F32),
                        pltpu.VMEM((hp * MIX_UNROLL_BWD, CHUNK, HEAD), F32)],
        input_output_aliases={0: 0}, compiler_params=_params(("arbitrary", "arbitrary")),
        name="hgrn_bwd")(dproj, *deps, proj, proj, proj, proj, o_pre, states, dy, lb_logits, gain)


ROWS = 256


def _head_norm(x_ref, gain, dst, dst_off, t):
    def step(i, _):
        sl = pl.ds(pl.multiple_of(i * ROWS, ROWS), ROWS)
        xv = x_ref[sl, :]
        r = lax.rsqrt(jnp.mean(xv * xv, axis=-1, keepdims=True) + EPS)
        dst[pl.ds(pl.multiple_of(dst_off + i * ROWS, ROWS), ROWS), :] = (xv * r * gain).astype(BF16)
        return 0

    lax.fori_loop(0, t // ROWS, step, 0)


def _head_norm_bwd(x_ref, gain, dn_ref, dn_off, out, slot, t):
    def step(i, acc):
        sl = pl.ds(pl.multiple_of(i * ROWS, ROWS), ROWS)
        xv = x_ref[sl, :]
        dn = dn_ref[pl.ds(pl.multiple_of(dn_off + i * ROWS, ROWS), ROWS), :]
        r = lax.rsqrt(jnp.mean(xv * xv, axis=-1, keepdims=True) + EPS)
        u = dn * gain
        out[slot, sl, :] = r * u - xv * (r * r * r) * jnp.mean(u * xv, axis=-1, keepdims=True)
        return acc + jnp.sum(dn * (xv * r), axis=0, keepdims=True)

    return lax.fori_loop(0, t // ROWS, step, jnp.zeros((1, HEAD), F32))


def _attn_scores(qn, kpad, n):
    qc = qn[pl.ds(pl.multiple_of(n * CHUNK, CHUNK), CHUNK), :]
    band = pl.ds(pl.multiple_of(n * CHUNK, CHUNK), BAND)
    return qc, band, lax.dot_general(qc, kpad[band, :], NT, preferred_element_type=F32)


def _attn_softmax(raw, bias_ref, n):
    s = raw * (HEAD ** -0.5) + bias_ref[0]
    col = lax.broadcasted_iota(jnp.int32, (CHUNK, BAND), 1)
    s = jnp.where(col >= PAD - n * CHUNK, s, -jnp.inf)
    p = jnp.exp(s - jnp.max(s, axis=-1, keepdims=True))
    return p / jnp.sum(p, axis=-1, keepdims=True)


def _attn_fwd(proj, q_gain, k_gain, bias, n_heads, col0):
    t = proj.shape[0]
    nc = t // CHUNK

    def body(q_ref, k_ref, v_ref, qg_ref, kg_ref, bias_ref, y_ref, qn, kpad, vpad):
        kpad[0:PAD, :] = jnp.zeros((PAD, HEAD), BF16)
        vpad[0:PAD, :] = jnp.zeros((PAD, HEAD), BF16)
        _head_norm(q_ref, qg_ref[...], qn, 0, t)
        _head_norm(k_ref, kg_ref[...], kpad, PAD, t)

        def copy_v(i, _):
            vpad[pl.ds(pl.multiple_of(PAD + i * ROWS, ROWS), ROWS), :] = v_ref[
                pl.ds(pl.multiple_of(i * ROWS, ROWS), ROWS), :].astype(BF16)
            return 0

        lax.fori_loop(0, t // ROWS, copy_v, 0)

        def chunks(i, _):
            ns = [i * ATT_UNROLL + u for u in range(ATT_UNROLL)]
            scored = [_attn_scores(qn, kpad, n) for n in ns]
            probs = [_attn_softmax(raw, bias_ref, n).astype(BF16) for n, (_, _, raw) in zip(ns, scored)]
            outs = [lax.dot_general(p, vpad[band, :], NN, preferred_element_type=F32).astype(BF16)
                    for p, (_, band, _) in zip(probs, scored)]
            for n, o in zip(ns, outs):
                y_ref[pl.ds(pl.multiple_of(n * CHUNK, CHUNK), CHUNK), :] = o
            return 0

        assert nc % ATT_UNROLL == 0, (nc, ATT_UNROLL)
        lax.fori_loop(0, nc // ATT_UNROLL, chunks, 0)

    col = lambda k: pl.BlockSpec((t, HEAD), lambda h: (0, col0 + k * n_heads + h))
    vec = pl.BlockSpec((1, HEAD), lambda h: (0, 0))
    return pl.pallas_call(
        body, grid=(n_heads,),
        in_specs=[col(0), col(1), col(2), vec, vec, pl.BlockSpec((1, CHUNK, BAND), lambda h: (h, 0, 0))],
        out_specs=pl.BlockSpec((t, HEAD), lambda h: (0, h)), out_shape=SDS((t, n_heads * HEAD), BF16),
        scratch_shapes=[pltpu.VMEM((t, HEAD), BF16), pltpu.VMEM((t + PAD, HEAD), BF16), pltpu.VMEM((t + PAD, HEAD), BF16)],
        compiler_params=_params(("parallel",)), name="attn_fwd")(proj, proj, proj, q_gain, k_gain, bias)


def _attn_bwd(dproj, proj, q_gain, k_gain, bias, dy, n_heads, col0, deps=()):
    t = proj.shape[0]
    nc = t // CHUNK

    def body(*refs):
        (q_ref, k_ref, v_ref, qg_ref, kg_ref, bias_ref, dy_ref,
         dproj_ref, dbias_ref, dqg_ref, dkg_ref, qn, kpad, vpad, dqn, dk_acc, dv_acc, res) = refs[1 + len(deps):]
        h, part = pl.program_id(0), pl.program_id(1)

        @pl.when(part == 0)
        def _():
            kpad[0:PAD, :] = jnp.zeros((PAD, HEAD), BF16)
            vpad[0:PAD, :] = jnp.zeros((PAD, HEAD), BF16)
            _head_norm(q_ref, qg_ref[...], qn, 0, t)
            _head_norm(k_ref, kg_ref[...], kpad, PAD, t)

            def prep(i, _):
                sl = pl.ds(pl.multiple_of(PAD + i * ROWS, ROWS), ROWS)
                vpad[sl, :] = v_ref[pl.ds(pl.multiple_of(i * ROWS, ROWS), ROWS), :].astype(BF16)
                return 0

            lax.fori_loop(0, t // ROWS, prep, 0)

            def clear(i, _):
                sl = pl.ds(pl.multiple_of(i * ROWS, ROWS), ROWS)
                dk_acc[sl, :] = jnp.zeros((ROWS, HEAD), F32)
                dv_acc[sl, :] = jnp.zeros((ROWS, HEAD), F32)
                return 0

            lax.fori_loop(0, (t + PAD) // ROWS, clear, 0)
            dbias_ref[0] = jnp.zeros((CHUNK, BAND), F32)

            def chunks(i, _):
                dot = functools.partial(lax.dot_general, preferred_element_type=F32)
                ns = [i * ATT_UNROLL_BWD + u for u in range(ATT_UNROLL_BWD)]
                scored = [_attn_scores(qn, kpad, n) for n in ns]
                dos = [dy_ref[pl.ds(pl.multiple_of(n * CHUNK, CHUNK), CHUNK), :].astype(BF16) for n in ns]
                dps = [dot(do, vpad[band, :], NT) for do, (_, band, _) in zip(dos, scored)]
                ps, dss = [], []
                for n, (_, _, raw), dp in zip(ns, scored, dps):
                    p = _attn_softmax(raw, bias_ref, n)
                    ds = p * (dp - jnp.sum(dp * p, axis=-1, keepdims=True))
                    dbias_ref[0] += ds
                    ps.append(p.astype(BF16))
                    dss.append((ds * (HEAD ** -0.5)).astype(BF16))
                dqs = [dot(d, kpad[band, :], NN) for d, (_, band, _) in zip(dss, scored)]
                dks = [dot(d, qc, TN) for d, (qc, _, _) in zip(dss, scored)]
                dvs = [dot(p, do, TN) for p, do in zip(ps, dos)]
                for n, (_, band, _), dq, dk, dv in zip(ns, scored, dqs, dks, dvs):
                    dqn[pl.ds(pl.multiple_of(n * CHUNK, CHUNK), CHUNK), :] = dq
                    dk_acc[band, :] += dk
                    dv_acc[band, :] += dv
                return 0

            assert nc % ATT_UNROLL_BWD == 0, (nc, ATT_UNROLL_BWD)
            lax.fori_loop(0, nc // ATT_UNROLL_BWD, chunks, 0)
            dqg = _head_norm_bwd(q_ref, qg_ref[...], dqn, 0, res, 0, t)
            dkg = _head_norm_bwd(k_ref, kg_ref[...], dk_acc, PAD, res, 1, t)

            def put_v(i, _):
                sl = pl.ds(pl.multiple_of(i * ROWS, ROWS), ROWS)
                res[2, sl, :] = dv_acc[pl.ds(pl.multiple_of(PAD + i * ROWS, ROWS), ROWS), :]
                return 0

            lax.fori_loop(0, t // ROWS, put_v, 0)

            @pl.when(h == 0)
            def _():
                dqg_ref[...] = jnp.zeros_like(dqg_ref)
                dkg_ref[...] = jnp.zeros_like(dkg_ref)

            dqg_ref[...] += dqg
            dkg_ref[...] += dkg

        dproj_ref[...] = res[part].astype(BF16)

    col = lambda k: pl.BlockSpec((t, HEAD), lambda h, p: (0, col0 + k * n_heads + h))
    vec = pl.BlockSpec((1, HEAD), lambda h, p: (0, 0))
    btile = pl.BlockSpec((1, CHUNK, BAND), lambda h, p: (h, 0, 0))
    return pl.pallas_call(
        body, grid=(n_heads, 3),
        in_specs=[ANY] * (1 + len(deps)) + [col(0), col(1), col(2), vec, vec, btile,
                                            pl.BlockSpec((t, HEAD), lambda h, p: (0, h))],
        out_specs=[pl.BlockSpec((t, HEAD), lambda h, p: (0, col0 + p * n_heads + h)), btile, vec, vec],
        out_shape=[SDS(dproj.shape, BF16), SDS((n_heads, CHUNK, BAND), F32), SDS((1, HEAD), F32), SDS((1, HEAD), F32)],
        scratch_shapes=[pltpu.VMEM((t, HEAD), BF16), pltpu.VMEM((t + PAD, HEAD), BF16), pltpu.VMEM((t + PAD, HEAD), BF16),
                        pltpu.VMEM((t, HEAD), F32), pltpu.VMEM((t + PAD, HEAD), F32), pltpu.VMEM((t + PAD, HEAD), F32),
                        pltpu.VMEM((3, t, HEAD), F32)],
        input_output_aliases={0: 0}, compiler_params=_params(("arbitrary", "arbitrary")),
        name="attn_bwd")(dproj, *deps, proj, proj, proj, q_gain, k_gain, bias, dy)


def _place():
    x, y, c = lax.axis_index("x"), lax.axis_index("y"), lax.axis_index("c")
    others = [(1 - x, y), (x, 1 - y), (1 - x, 1 - y)]
    return x, y, c, others


def _chunk_of(ref, kind, chip, half, shard_shape):
    r, n = shard_shape
    hr = r // 2
    if kind == "col":
        rows = pl.ds(0, r) if half is None else pl.ds(half * hr, hr)
        return ref.at[rows, pl.ds(chip * n, n)]
    rows = pl.ds(chip * r, r) if half is None else pl.ds(chip * r + half * hr, hr)
    return ref.at[rows, :]


EFFECT = pltpu.SideEffectType.DATAFLOW_SIDE_EFFECTING


def _start_copies(name, bufs, plan, n, deps):
    nb, nd = len(bufs), len(deps)

    def body(*refs):
        send, recv, token = refs[nb + nd], refs[nb + nd + 1], refs[-1]
        for cp in plan(refs[:nb], send, recv)[0]:
            cp.start()
        token[...] = jnp.zeros_like(token)

    out = pl.pallas_call(
        body, name=name,
        out_shape=(pltpu.SemaphoreType.DMA((n,)), pltpu.SemaphoreType.DMA((n,)),
                   *[pltpu.HBM(b.shape, b.dtype) for b in bufs], SDS((8, 128), F32)),
        in_specs=[HBM] * nb + [ANY] * nd,
        out_specs=(SEM, SEM, *[HBM] * nb, pl.BlockSpec(memory_space=pltpu.VMEM)),
        input_output_aliases={i: 2 + i for i in range(nb)},
        compiler_params=pltpu.CompilerParams(has_side_effects=EFFECT),
    )(*[pltpu.with_memory_space_constraint(b, pltpu.HBM) for b in bufs], *deps)
    return out[0], out[1], list(out[2:2 + nb]), out[-1]


def _wait_copies(name, bufs, send, recv, plan, after):
    nb = len(bufs)

    def body(*refs):
        sends, recvs = plan(refs[:nb], refs[nb], refs[nb + 1])
        for cp in sends:
            cp.wait_send()
        for cp in recvs:
            cp.wait_recv()

    out = pl.pallas_call(
        body, name=name, out_shape=tuple(pltpu.HBM(b.shape, b.dtype) for b in bufs),
        in_specs=[HBM] * nb + [SEM, SEM] + [ANY] * len(after), out_specs=tuple([HBM] * nb),
        input_output_aliases={i: i for i in range(nb)},
        compiler_params=pltpu.CompilerParams(has_side_effects=EFFECT),
    )(*bufs, send, recv, *after)
    return list(out)


def _remote(src, dst, send, recv, i, dev):
    return pltpu.make_async_remote_copy(src_ref=src, dst_ref=dst, send_sem=send.at[i], recv_sem=recv.at[i],
                                        device_id=dev, device_id_type=MESH)


def _plan_gather_ici(kinds, shapes):
    def plan(refs, send, recv):
        x, y, c, others = _place()
        sends, recvs = [], []
        for w, (kind, ss) in enumerate(zip(kinds, shapes)):
            for p, (px, py) in enumerate(others):
                mine = _chunk_of(refs[w], kind, 2 * x + y, c, ss)
                theirs = _chunk_of(refs[w], kind, 2 * px + py, c, ss)
                sends.append(_remote(mine, mine, send, recv, 3 * w + p, (px, py, c)))
                recvs.append(_remote(theirs, theirs, send, recv, 3 * w + p, (px, py, c)))
        return sends, recvs

    return plan, 3 * len(kinds)


def _plan_gather_pass(kinds, shapes):
    def plan(refs, send, recv):
        x, y, c, others = _place()
        sends, recvs = [], []
        for w, (kind, ss) in enumerate(zip(kinds, shapes)):
            for p, (px, py) in enumerate(others):
                got = _chunk_of(refs[w], kind, 2 * px + py, c, ss)
                coming = _chunk_of(refs[w], kind, 2 * px + py, 1 - c, ss)
                sends.append(_remote(got, got, send, recv, 3 * w + p, (x, y, 1 - c)))
                recvs.append(_remote(coming, coming, send, recv, 3 * w + p, (x, y, 1 - c)))
        return sends, recvs

    return plan, 3 * len(kinds)


def _plan_pair(kinds, shapes):
    nw = len(kinds)

    def plan(refs, send, recv):
        x, y, c, _ = _place()
        sends = []
        for w, (kind, ss) in enumerate(zip(kinds, shapes)):
            for k in range(4):
                sends.append(_remote(_chunk_of(refs[w], kind, k, 1 - c, ss), refs[nw + w].at[k], send, recv,
                                     4 * w + k, (x, y, 1 - c)))
        return sends, sends

    return plan, 4 * nw


def _plan_chip(nw):
    def plan(refs, send, recv):
        x, y, c, others = _place()
        sends = []
        for w in range(nw):
            for p, (px, py) in enumerate(others):
                sends.append(_remote(refs[w].at[p], refs[nw + w].at[p], send, recv, 3 * w + p, (px, py, c)))
        return sends, sends

    return plan, 3 * nw


def _plan_share(nw):
    def plan(refs, send, recv):
        x, y, c, _ = _place()
        sends = [_remote(refs[w].at[c], refs[w].at[c], send, recv, w, (x, y, 1 - c)) for w in range(nw)]
        recvs = [_remote(refs[w].at[1 - c], refs[w].at[1 - c], send, recv, w, (x, y, 1 - c)) for w in range(nw)]
        return sends, recvs

    return plan, nw


def _grad_half_spec(kind, tr, tn, nr, nn, chunk):
    if kind == "col":
        return pl.BlockSpec((tr, tn), lambda *a: (a[-1][1] * nr + a[-3], chunk(*a) * nn + a[-2]))
    return pl.BlockSpec((tr, tn), lambda *a: ((2 * chunk(*a) + a[-1][1]) * nr + a[-3], a[-2]))


def _pair_add(grad, got, kind, shard_shape, pos, name):
    r, n = shard_shape
    hr = r // 2
    tr, tn = _tile(hr, 256, 16), _tile(n, 1408)
    nr, nn = hr // tr, n // tn
    g_spec = _grad_half_spec(kind, tr, tn, nr, nn, lambda p, i, j, pos_: pos_[2 + p])
    r_spec = pl.BlockSpec((1, tr, tn), lambda p, i, j, pos_: (pos_[2 + p], i, j))
    o_spec = pl.BlockSpec((1, tr, tn), lambda p, i, j, pos_: (p, i, j))

    def body(pos_ref, g_ref, r_ref, o_ref):
        o_ref[0] = (g_ref[...] + r_ref[0]).astype(BF16)

    return pl.pallas_call(
        body,
        grid_spec=pltpu.PrefetchScalarGridSpec(num_scalar_prefetch=1, grid=(3, nr, nn), in_specs=[g_spec, r_spec],
                                               out_specs=o_spec),
        out_shape=SDS((3, hr, n), BF16),
        compiler_params=_params(("parallel", "parallel", "parallel")), name=name)(pos, grad, got)


def _chip_add(grad, got, got16, kind, shard_shape, pos, name):
    r, n = shard_shape
    hr = r // 2
    tr, tn = _tile(hr, 256, 16), _tile(n, 1408)
    nr, nn = hr // tr, n // tn
    g_spec = _grad_half_spec(kind, tr, tn, nr, nn, lambda i, j, pos_: pos_[0])
    r_spec = pl.BlockSpec((1, tr, tn), lambda i, j, pos_: (pos_[0], i, j))
    oth = pl.BlockSpec((3, tr, tn), lambda i, j, pos_: (0, i, j))

    def body(pos_ref, g_ref, r_ref, oth_ref, o_ref):
        own = g_ref[...] + r_ref[0]
        o_ref[0] = ((own + oth_ref[0].astype(F32)) + oth_ref[1].astype(F32)) + oth_ref[2].astype(F32)

    return pl.pallas_call(
        body,
        grid_spec=pltpu.PrefetchScalarGridSpec(num_scalar_prefetch=1, grid=(nr, nn), in_specs=[g_spec, r_spec, oth],
                                               out_specs=pl.BlockSpec((1, tr, tn), lambda i, j, pos_: (pos_[1], i, j))),
        out_shape=SDS((2, hr, n), F32), compiler_params=_params(("parallel", "parallel")),
        name=name)(pos, grad, got, got16)


def _adamw_math(w, g, m, v):
    m = ADAM_B1 * m + (1.0 - ADAM_B1) * g
    v = ADAM_B2 * v + (1.0 - ADAM_B2) * (g * g)
    m_hat = m / (1.0 - ADAM_B1 ** ADAM_STEP)
    v_hat = v / (1.0 - ADAM_B2 ** ADAM_STEP)
    return -ADAM_LR * (m_hat / (jnp.sqrt(v_hat) + ADAM_EPS) + ADAM_WD * w), m, v


def _adamw(w, g, m, v, name):
    r, n = w.shape
    tr, tn = _tile(r, 256, 16), _tile(n, 1408)

    def body(w_ref, g_ref, m_ref, v_ref, d_ref, nm_ref, nv_ref, go_ref):
        gv = g_ref[...]
        d_ref[...], nm_ref[...], nv_ref[...] = _adamw_math(w_ref[...], gv, m_ref[...], v_ref[...])
        go_ref[...] = gv

    tile = pl.BlockSpec((tr, tn), lambda i, j: (i, j))
    return pl.pallas_call(
        body, grid=(r // tr, n // tn), in_specs=[tile] * 4, out_specs=[tile] * 4, out_shape=[SDS((r, n), F32)] * 4,
        compiler_params=_params(("parallel", "parallel")), name=name)(w, g, m, v)


def _small_allreduce_adamw(g, w, m, v, deps=()):
    length = g.shape[1]

    def body(*refs):
        g_ref, w_ref, m_ref, v_ref = refs[:4]
        gs_ref, d_ref, nm_ref, nv_ref, buf, send, recv = refs[4 + len(deps):]
        x, y, c = lax.axis_index("x"), lax.axis_index("y"), lax.axis_index("c")
        me = 4 * x + 2 * y + c
        buf[me] = g_ref[...]
        cps = []
        for d in range(1, 8):
            peer = (x ^ (d >> 2), y ^ ((d >> 1) & 1), c ^ (d & 1))
            cp = pltpu.make_async_remote_copy(src_ref=buf.at[me], dst_ref=buf.at[me], send_sem=send.at[d - 1],
                                              recv_sem=recv.at[d - 1], device_id=peer, device_id_type=MESH)
            cp.start()
            cps.append(cp)
        for cp in cps:
            cp.wait()
        total = buf[0]
        for d in range(1, 8):
            total = total + buf[d]
        gs_ref[...] = total
        d_ref[...], nm_ref[...], nv_ref[...] = _adamw_math(w_ref[...], total, m_ref[...], v_ref[...])

    vm = pl.BlockSpec(memory_space=pltpu.VMEM)
    return pl.pallas_call(
        body, in_specs=[vm] * 4 + [ANY] * len(deps), out_specs=[vm] * 4, out_shape=[SDS((1, length), F32)] * 4,
        scratch_shapes=[pltpu.VMEM((8, 1, length), F32), pltpu.SemaphoreType.DMA((7,)), pltpu.SemaphoreType.DMA((7,))],
        compiler_params=pltpu.CompilerParams(has_side_effects=True), name="small_allreduce_adamw")(g, w, m, v, *deps)


def kernel(x, w_in, b_gate, norm_mix, norm_ffn, hgrn_lb_logits, hgrn_out_gain, q_gain, k_gain, rel_bias, w_proj_a, w_proj_b, w_out, w_ffn_in, w_ffn_out, loss_target, m_w_in, m_b_gate, m_norm_mix, m_norm_ffn, m_hgrn_lb_logits, m_hgrn_out_gain, m_q_gain, m_k_gain, m_rel_bias, m_w_proj_a, m_w_proj_b, m_w_out, m_w_ffn_in, m_w_ffn_out, v_w_in, v_b_gate, v_norm_mix, v_norm_ffn, v_hgrn_lb_logits, v_hgrn_out_gain, v_q_gain, v_k_gain, v_rel_bias, v_w_proj_a, v_w_proj_b, v_w_out, v_w_ffn_in, v_w_ffn_out):
    t, d = x.shape[1], x.shape[2]
    d_a = hgrn_out_gain.shape[1]
    h_a = d_a // HEAD
    h_b = rel_bias.shape[1]
    d_b = h_b * HEAD
    x0 = x.reshape(t, d)
    target = loss_target.reshape(t, d)
    ax, ay = lax.axis_index("x"), lax.axis_index("y")
    pos = jnp.stack([2 * ax + ay, lax.axis_index("c"), 2 * (1 - ax) + ay, 2 * ax + 1 - ay,
                     2 * (1 - ax) + 1 - ay]).astype(jnp.int32)

    names = ["w_in", "w_proj_a", "w_proj_b", "w_out", "w_ffn_in", "w_ffn_out"]
    big = dict(zip(names, [w_in[0], w_proj_a[0], w_proj_b[0], w_out[0], w_ffn_in[0], w_ffn_out[0]]))
    big_m = dict(zip(names, [m_w_in[0], m_w_proj_a[0], m_w_proj_b[0], m_w_out[0], m_w_ffn_in[0], m_w_ffn_out[0]]))
    big_v = dict(zip(names, [v_w_in[0], v_w_proj_a[0], v_w_proj_b[0], v_w_out[0], v_w_ffn_in[0], v_w_ffn_out[0]]))
    kind = dict(zip(names, ["col", "col", "col", "row", "col", "row"]))
    shape = {nm: big[nm].shape for nm in names}

    def gather_start(tag, group, deps):
        plan, n = _plan_gather_ici([kind[g] for g in group], [shape[g] for g in group])
        fulls = [_cast_into_full(big[g], kind[g], pos, "cast_" + g) for g in group]
        send, recv, bufs, token = _start_copies("gather_ici_start_" + tag, fulls, plan, n, deps)
        return (tag, group, plan, send, recv, bufs), token

    def gather_pass(state, after):
        tag, group, plan, send, recv, bufs = state
        bufs = _wait_copies("gather_ici_wait_" + tag, bufs, send, recv, plan, after)
        plan, n = _plan_gather_pass([kind[g] for g in group], [shape[g] for g in group])
        send, recv, bufs, token = _start_copies("gather_pass_start_" + tag, bufs, plan, n, ())
        return (tag, group, plan, send, recv, bufs), token

    def gather_done(state, after):
        tag, group, plan, send, recv, bufs = state
        return _wait_copies("gather_pass_wait_" + tag, bufs, send, recv, plan, after)

    def reduce_start(tag, group, grads, deps):
        plan, n = _plan_pair([kind[g] for g in group], [shape[g] for g in group])
        lands = [lax.empty((4, shape[g][0] // 2, shape[g][1]), F32) for g in group]
        send, recv, bufs, token = _start_copies("pair_start_" + tag, list(grads) + lands, plan, n, deps)
        return dict(tag=tag, group=group, plan=plan, send=send, recv=recv, bufs=bufs), token

    def reduce_pair_done(st, after):
        tag, group, nw = st["tag"], st["group"], len(st["group"])
        bufs = _wait_copies("pair_wait_" + tag, st["bufs"], st["send"], st["recv"], st["plan"], after)
        grads, gots = bufs[:nw], bufs[nw:]
        parts = [_pair_add(g, l, kind[nm], shape[nm], pos, "pair_add_" + nm) for g, l, nm in zip(grads, gots, group)]
        lands = [lax.empty((3, shape[g][0] // 2, shape[g][1]), BF16) for g in group]
        plan, n = _plan_chip(nw)
        send, recv, bufs, token = _start_copies("chip_start_" + tag, parts + lands, plan, n, ())
        return dict(st, plan=plan, send=send, recv=recv, bufs=bufs, grads=grads, gots=gots), token

    def reduce_chip_done(st, after):
        tag, group, nw = st["tag"], st["group"], len(st["group"])
        bufs = _wait_copies("chip_wait_" + tag, st["bufs"], st["send"], st["recv"], st["plan"], after)
        finals = [_chip_add(g, l, got16, kind[nm], shape[nm], pos, "chip_add_" + nm)
                  for g, l, got16, nm in zip(st["grads"], st["gots"], bufs[nw:], group)]
        plan, n = _plan_share(nw)
        send, recv, bufs, token = _start_copies("share_start_" + tag, finals, plan, n, ())
        return dict(st, plan=plan, send=send, recv=recv, bufs=bufs), token

    g_big, upd = {}, {}

    def reduce_finish(st, after):
        bufs = _wait_copies("share_wait_" + st["tag"], st["bufs"], st["send"], st["recv"], st["plan"], after)
        for full, nm in zip(bufs, st["group"]):
            upd[nm] = _adamw(big[nm], full.reshape(shape[nm]), big_m[nm], big_v[nm], "adamw_" + nm)
            g_big[nm] = upd[nm][3]

    ga, token = gather_start("a", ["w_in"], ())
    gb, token = gather_start("b", ["w_proj_a", "w_proj_b", "w_out"], (token,))
    gc, token = gather_start("c", ["w_ffn_in"], (token,))
    gd, token = gather_start("d", ["w_ffn_out"], (token,))
    h1, r1 = _rmsnorm_fwd(x0, norm_mix, "rmsnorm_mix")
    rb = jnp.pad(rel_bias[0], ((0, 0), (0, REL_LANES - N_REL)))
    bias = _relbias_expand(rb).transpose(1, 0, 2)
    ga, token = gather_pass(ga, (h1, bias, token))
    (wg_in,) = gather_done(ga, ())
    proj = _matmul(h1, wg_in, name="proj_in")
    y_a, o_pre, states = _hgrn_fwd(proj, hgrn_lb_logits, hgrn_out_gain, h_a)
    gb, token = gather_pass(gb, (y_a,))
    col_b = 4 * d_a // HEAD
    y_b = _attn_fwd(proj, q_gain, k_gain, bias, h_b, col_b)
    wg_pa, wg_pb, wg_out = gather_done(gb, (y_b,))
    pa = _matmul(y_a, wg_pa, name="proj_a", deps=(token,))
    pb = _matmul(y_b, wg_pb, name="proj_b")
    gate_off = 4 * d_a + 3 * d_b
    merged = _merge_fwd(proj, b_gate, pa, pb, gate_off)
    x2 = _matmul(merged, wg_out, res=x0, name="out_proj")
    gc, token = gather_pass(gc, (x2,))
    h2, r2 = _rmsnorm_fwd(x2, norm_ffn, "rmsnorm_ffn")
    (wg_fin,) = gather_done(gc, (h2,))
    ff_gate, ff_up, act = _ffn_in_swiglu(h2, wg_fin, deps=(token,))
    gd, token = gather_pass(gd, (act,))
    (wg_fout,) = gather_done(gd, ())
    dy, dy16, loss_part = _ffn_out_loss(act, wg_fout, x2, target)

    g_fout = _matmul(act, dy16, ta=True, name="dw_ffn_out")
    r_fout, token = reduce_start("fout", ["w_ffn_out"], [g_fout], ())
    dact = _matmul(dy16, wg_fout, tb=True, name="d_act", deps=(token,))
    r_fout, token = reduce_pair_done(r_fout, (dact,))
    dgu = _swiglu_bwd(dact, ff_gate, ff_up)
    g_fin = _matmul(h2, dgu, ta=True, name="dw_ffn_in", deps=(token,))
    r_fin, token = reduce_start("fin", ["w_ffn_in"], [g_fin], ())
    dh2 = _matmul(dgu, wg_fin, tb=True, name="d_h2", deps=(token,))
    r_fout, token_a = reduce_chip_done(r_fout, (dh2,))
    r_fin, token_b = reduce_pair_done(r_fin, (dh2,))
    dx2, dx2_16, g_norm_ffn = _rmsnorm_bwd(dh2, x2, r2, norm_ffn, dy, "rmsnorm_ffn_bwd")
    dmerged = _matmul(dx2_16, wg_out, tb=True, name="d_merged", deps=(token_a, token_b))
    dp_ab, dproj, g_bgate = _merge_bwd(dmerged, proj, b_gate, pa, pb, gate_off)
    g_out = _matmul(merged, dx2_16, ta=True, name="dw_out")
    g_pa = _matmul(y_a, dp_ab[0], ta=True, name="dw_proj_a")
    g_pb = _matmul(y_b, dp_ab[1], ta=True, name="dw_proj_b")
    r_mid, token = reduce_start("mid", ["w_proj_a", "w_proj_b", "w_out"], [g_pa, g_pb, g_out], ())
    dy_a = _matmul(dp_ab[0], wg_pa, tb=True, name="d_y_a", deps=(token,))
    dy_b = _matmul(dp_ab[1], wg_pb, tb=True, name="d_y_b")
    r_fin, token_a = reduce_chip_done(r_fin, (dy_b,))
    r_mid, token_b = reduce_pair_done(r_mid, (dy_b,))
    dproj, dbias, g_qg, g_kg = _attn_bwd(dproj, proj, q_gain, k_gain, bias, dy_b, h_b, col_b, deps=(token_a, token_b))
    r_mid, token = reduce_chip_done(r_mid, (dbias,))
    dproj, g_lb, g_gain = _hgrn_bwd(dproj, proj, o_pre, states, dy_a, hgrn_lb_logits, hgrn_out_gain, h_a, deps=(token,))
    g_in = _matmul(h1, dproj, ta=True, name="dw_in")
    r_in, token = reduce_start("in", ["w_in"], [g_in], ())
    g_rb = _relbias_reduce(dbias.transpose(1, 0, 2))[:, :N_REL]
    reduce_finish(r_mid, (token,))
    r_in, token = reduce_pair_done(r_in, (g_rb, upd["w_out"][0], upd["w_proj_a"][0], upd["w_proj_b"][0]))
    dh1 = _matmul(dproj, wg_in, tb=True, name="d_h1", deps=(token,))
    dx, _, g_norm_mix = _rmsnorm_bwd(dh1, x0, r1, norm_mix, dx2, "rmsnorm_mix_bwd")
    reduce_finish(r_fin, (dx,))
    reduce_finish(r_fout, (dx,))
    r_in, token = reduce_chip_done(r_in, (upd["w_ffn_in"][0], upd["w_ffn_out"][0]))

    small_w = [b_gate, norm_mix, norm_ffn, hgrn_lb_logits, hgrn_out_gain, q_gain, k_gain, rel_bias]
    small_m = [m_b_gate, m_norm_mix, m_norm_ffn, m_hgrn_lb_logits, m_hgrn_out_gain, m_q_gain, m_k_gain, m_rel_bias]
    small_v = [v_b_gate, v_norm_mix, v_norm_ffn, v_hgrn_lb_logits, v_hgrn_out_gain, v_q_gain, v_k_gain, v_rel_bias]
    small_g = [g_bgate, g_norm_mix, g_norm_ffn, g_lb, g_gain, g_qg, g_kg, g_rb]
    sizes = [w.size for w in small_w]
    length = -(-(sum(sizes) + 1) // 128) * 128

    def pack(parts_):
        flat = jnp.concatenate([p.reshape(1, -1) for p in parts_], axis=1)
        return jnp.pad(flat, ((0, 0), (0, length - flat.shape[1])))

    one = jnp.ones((1, 1), F32)
    packed = _small_allreduce_adamw(pack(small_g + [loss_part]), pack(small_w + [one]), pack(small_m + [one]),
                                    pack(small_v + [one]), deps=(token,))

    def unpack(vec):
        out, at = [], 0
        for w, n in zip(small_w, sizes):
            out.append(vec[0, at:at + n].reshape(w.shape))
            at += n
        return out, vec[0, at]

    (sg, loss), (sd, _), (sm, _), (sv, _) = [unpack(p) for p in packed]
    reduce_finish(r_in, (packed[0],))

    def ordered(small, bigs):
        bigs = [bigs[nm][None] for nm in names]
        return [bigs[0]] + small + bigs[1:]

    return (loss, dx.reshape(x.shape), *ordered(sg, g_big), *ordered(sd, {nm: upd[nm][0] for nm in names}),
            *ordered(sm, {nm: upd[nm][1] for nm in names}), *ordered(sv, {nm: upd[nm][2] for nm in names}))
```

```python
import functools

import jax
import jax.numpy as jnp
from jax import lax
from jax.experimental import pallas as pl
from jax.experimental.pallas import tpu as pltpu

F32 = jnp.float32
BF16 = jnp.bfloat16
SDS = jax.ShapeDtypeStruct
MESH = pl.DeviceIdType.MESH
HIGHEST = lax.Precision.HIGHEST

CHUNK = 64
SUB = 16
HEAD = 128
N_PAST = 8
BAND = (N_PAST + 1) * CHUNK
PAD = N_PAST * CHUNK
REL_FUTURE = CHUNK - 1
REL_PAST = 2 * CHUNK - 1
N_REL = REL_FUTURE + REL_PAST + 1
REL_LANES = 256
EPS = 1e-6
MIX_HEADS = 2
MIX_UNROLL = 4
MIX_UNROLL_BWD = 4
ATT_UNROLL = 8
ATT_UNROLL_BWD = 4
EXP_CLAMP = 80.0

ADAM_LR = 0.001
ADAM_B1 = 0.9
ADAM_B2 = 0.999
ADAM_EPS = 1e-08
ADAM_WD = 0.01
ADAM_STEP = 10

VMEM_LIMIT = 56 * 1024 * 1024

HBM = pl.BlockSpec(memory_space=pltpu.HBM)
ANY = pl.BlockSpec(memory_space=pl.ANY)
SEM = pl.BlockSpec(memory_space=pltpu.SEMAPHORE)

NT = (((1,), (1,)), ((), ()))
TN = (((0,), (0,)), ((), ()))
NN = (((1,), (0,)), ((), ()))


def _params(sem=None, **kw):
    return pltpu.CompilerParams(dimension_semantics=sem, vmem_limit_bytes=VMEM_LIMIT, **kw)


def _tile(n, pref, unit=128):
    if n <= pref:
        return n
    t = pref - pref % unit
    while n % t:
        t -= unit
    return t


def _loop(n, unroll, step, init):
    assert n % unroll == 0, (n, unroll)

    def several(i, carry):
        for u in range(unroll):
            carry = step(i * unroll + u, carry)
        return carry

    return lax.fori_loop(0, n // unroll, several, init)


def _sigmoid(x):
    return 1.0 / (1.0 + jnp.exp(-x))


def _dsilu(x, s):
    return s * (1.0 + x * (1.0 - s))


def _bdot(a, b, dims=NN):
    return lax.dot_general(a.astype(BF16), b.astype(BF16), dims, preferred_element_type=F32)


def _split(a):
    hi = a.astype(BF16)
    return hi, (a - hi.astype(F32)).astype(BF16)


def _dot3(a, b, dims):
    dot = lambda u, v: lax.dot_general(u, v, dims, preferred_element_type=F32)
    return dot(a[0], b[1]) + dot(a[1], b[0]) + dot(a[0], b[0])


def _fdot(a, b):
    return lax.dot_general(a, b, NN, precision=HIGHEST, preferred_element_type=F32)


MM_TILE_K = 5632
MM_TILE_N = 512


def _matmul_chunks(h, w, which, prev, pos, name, own_shard=False, deps=()):
    t, d = h.shape
    nc_ = w.shape[1] if own_shard else w.shape[1] // 4
    tm, tn = _tile(t, 1024), _tile(nc_, 1408)
    nn = nc_ // tn

    def chunk(q, p):
        sel = p[which[0]]
        for i in range(1, len(which)):
            sel = jnp.where(q == i, p[which[i]], sel)
        return sel

    def body(p_ref, h_ref, w_ref, *rest):
        rest[-1][...] = jnp.dot(h_ref[...], w_ref[...].astype(BF16), preferred_element_type=F32)

    if own_shard:
        w_spec = pl.BlockSpec((d, tn), lambda q, i, j, p: (0, j))
    else:
        w_spec = pl.BlockSpec((d, tn), lambda q, i, j, p: (0, chunk(q, p) * nn + j))
    n_extra = len(deps) + (prev is not None)
    return pl.pallas_call(
        body,
        grid_spec=pltpu.PrefetchScalarGridSpec(
            num_scalar_prefetch=1, grid=(len(which), t // tm, nn),
            in_specs=[pl.BlockSpec((tm, d), lambda q, i, j, p: (i, 0)), w_spec] + [ANY] * n_extra,
            out_specs=pl.BlockSpec((tm, tn), lambda q, i, j, p: (i, chunk(q, p) * nn + j))),
        out_shape=SDS((t, 4 * nc_), F32), input_output_aliases={3 + len(deps): 0} if prev is not None else {},
        compiler_params=_params(("arbitrary", "arbitrary", "arbitrary")),
        name=name)(pos, h, w, *deps, *(() if prev is None else (prev,)))


def _matmul(a, b, *, ta=False, tb=False, res=None, out_dtype=F32, name, deps=()):
    m, k = (a.shape[1], a.shape[0]) if ta else a.shape
    n = b.shape[0] if tb else b.shape[1]
    tk = _tile(k, MM_TILE_K)
    nk = k // tk
    tm, tn = _tile(m, 2048 if tk <= MM_TILE_K // 2 else 1024), _tile(n, MM_TILE_N)
    dims = ((((0,) if ta else (1,)), ((1,) if tb else (0,))), ((), ()))

    def body(*refs):
        n_in = 2 + (res is not None)
        a_ref, b_ref = refs[:2]
        r_ref = refs[2] if res is not None else None
        o_ref = refs[n_in + len(deps)]
        part = lax.dot_general(a_ref[...].astype(BF16), b_ref[...].astype(BF16), dims, preferred_element_type=F32)

        def finish(out):
            if r_ref is not None:
                out = out + r_ref[...]
            o_ref[...] = out.astype(o_ref.dtype)

        if nk == 1:
            finish(part)
            return
        acc_ref = refs[-1]
        kk = pl.program_id(2)

        @pl.when(kk == 0)
        def _():
            acc_ref[...] = part

        @pl.when(jnp.logical_and(kk > 0, kk < nk - 1))
        def _():
            acc_ref[...] += part

        @pl.when(kk == nk - 1)
        def _():
            finish(acc_ref[...] + part)

    a_spec = pl.BlockSpec((tk, tm), lambda i, j, l: (l, i)) if ta else pl.BlockSpec((tm, tk), lambda i, j, l: (i, l))
    b_spec = pl.BlockSpec((tn, tk), lambda i, j, l: (j, l)) if tb else pl.BlockSpec((tk, tn), lambda i, j, l: (l, j))
    o_spec = pl.BlockSpec((tm, tn), lambda i, j, l: (i, j))
    in_specs = [a_spec, b_spec] + ([o_spec] if res is not None else []) + [ANY] * len(deps)
    args = (a, b) + ((res,) if res is not None else ()) + tuple(deps)
    return pl.pallas_call(
        body, grid=(m // tm, n // tn, nk), in_specs=in_specs, out_specs=o_spec,
        out_shape=SDS((m, n), out_dtype), scratch_shapes=[pltpu.VMEM((tm, tn), F32)] if nk > 1 else [],
        compiler_params=_params(("parallel", "parallel", "arbitrary")), name=name)(*args)


def _cast_into_full(w, kind, pos, name):
    r, n = w.shape
    tr = _tile(r, 512, 16)
    nr = r // tr
    if kind == "col":
        shape, o_spec = (r, 4 * n), pl.BlockSpec((tr, n), lambda i, p: (i, p[0]))
    else:
        shape, o_spec = (4 * r, n), pl.BlockSpec((tr, n), lambda i, p: (p[0] * nr + i, 0))

    def body(p_ref, w_ref, o_ref):
        o_ref[...] = w_ref[...].astype(BF16)

    return pl.pallas_call(
        body,
        grid_spec=pltpu.PrefetchScalarGridSpec(num_scalar_prefetch=1, grid=(nr,),
                                               in_specs=[pl.BlockSpec((tr, n), lambda i, p: (i, 0))], out_specs=o_spec),
        out_shape=SDS(shape, BF16), compiler_params=_params(("parallel",)), name=name)(pos, w)


def _rmsnorm_fwd(x, gain, name):
    t, d = x.shape
    tm = _tile(t, 256)

    def body(x_ref, g_ref, h_ref, r_ref):
        xv = x_ref[...]
        r = lax.rsqrt(jnp.mean(xv * xv, axis=-1, keepdims=True) + EPS)
        h_ref[...] = (xv * r * g_ref[...]).astype(BF16)
        r_ref[...] = r

    return pl.pallas_call(
        body, grid=(t // tm,),
        in_specs=[pl.BlockSpec((tm, d), lambda i: (i, 0)), pl.BlockSpec((1, d), lambda i: (0, 0))],
        out_specs=[pl.BlockSpec((tm, d), lambda i: (i, 0)), pl.BlockSpec((tm, 1), lambda i: (i, 0))],
        out_shape=[SDS((t, d), BF16), SDS((t, 1), F32)], compiler_params=_params(("parallel",)), name=name)(x, gain)


def _rmsnorm_bwd(dh, x, r, gain, dres, name):
    t, d = x.shape
    tm = _tile(t, 256)

    def body(dh_ref, x_ref, r_ref, g_ref, dres_ref, dx_ref, dxb_ref, dg_ref):
        @pl.when(pl.program_id(0) == 0)
        def _():
            dg_ref[...] = jnp.zeros_like(dg_ref)

        dhv, xv, rv = dh_ref[...], x_ref[...], r_ref[...]
        dg_ref[...] += jnp.sum(dhv * (xv * rv), axis=0, keepdims=True)
        u = dhv * g_ref[...]
        dx = dres_ref[...] + rv * u - xv * (rv * rv * rv) * jnp.mean(u * xv, axis=-1, keepdims=True)
        dx_ref[...] = dx
        dxb_ref[...] = dx.astype(BF16)

    row = pl.BlockSpec((tm, d), lambda i: (i, 0))
    vec = pl.BlockSpec((1, d), lambda i: (0, 0))
    return pl.pallas_call(
        body, grid=(t // tm,), in_specs=[row, row, pl.BlockSpec((tm, 1), lambda i: (i, 0)), vec, row],
        out_specs=[row, row, vec], out_shape=[SDS((t, d), F32), SDS((t, d), BF16), SDS((1, d), F32)],
        compiler_params=_params(("arbitrary",)), name=name)(dh, x, r, gain, dres)


def _merge_fwd(proj, b_gate, pa, pb, off):
    t, d = pa.shape
    tm, tc = _tile(t, 512), _tile(d, 512)
    nj = d // tc
    oa, ob = off // tc, off // tc + nj

    def body(la_ref, lb_ref, ba_ref, bb_ref, pa_ref, pb_ref, o_ref):
        ga = _sigmoid(la_ref[...] + ba_ref[...])
        gb = _sigmoid(lb_ref[...] + bb_ref[...])
        o_ref[...] = (ga * pa_ref[...] + gb * pb_ref[...]).astype(BF16)

    tile = pl.BlockSpec((tm, tc), lambda i, j: (i, j))
    return pl.pallas_call(
        body, grid=(t // tm, nj),
        in_specs=[pl.BlockSpec((tm, tc), lambda i, j: (i, oa + j)), pl.BlockSpec((tm, tc), lambda i, j: (i, ob + j)),
                  pl.BlockSpec((1, tc), lambda i, j: (0, j)), pl.BlockSpec((1, tc), lambda i, j: (0, nj + j)), tile, tile],
        out_specs=tile, out_shape=SDS((t, d), BF16), compiler_params=_params(("parallel", "parallel")),
        name="merge_fwd")(proj, proj, b_gate, b_gate, pa, pb)


def _merge_bwd(dmerged, proj, b_gate, pa, pb, off):
    t, d = pa.shape
    tm, tc = _tile(t, 512), _tile(d, 512)
    nj, ni = d // tc, t // tm
    o0 = off // tc

    def body(dm_ref, l_ref, b_ref, pa_ref, pb_ref, dp_ref, dl_ref, db_ref):
        s, i = pl.program_id(0), pl.program_id(2)
        p = jnp.where(s == 0, pa_ref[...], pb_ref[...])
        g = _sigmoid(l_ref[...] + b_ref[...])
        dm = dm_ref[...]
        dp_ref[0] = (dm * g).astype(BF16)
        dl = dm * p * g * (1.0 - g)
        dl_ref[...] = dl.astype(BF16)

        @pl.when(i == 0)
        def _():
            db_ref[...] = jnp.zeros_like(db_ref)

        db_ref[...] += jnp.sum(dl, axis=0, keepdims=True)

    tile = pl.BlockSpec((tm, tc), lambda s, j, i: (i, j))
    return pl.pallas_call(
        body, grid=(2, nj, ni),
        in_specs=[tile, pl.BlockSpec((tm, tc), lambda s, j, i: (i, o0 + s * nj + j)),
                  pl.BlockSpec((1, tc), lambda s, j, i: (0, s * nj + j)), tile, tile],
        out_specs=[pl.BlockSpec((1, tm, tc), lambda s, j, i: (s, i, j)),
                   pl.BlockSpec((tm, tc), lambda s, j, i: (i, o0 + s * nj + j)),
                   pl.BlockSpec((1, tc), lambda s, j, i: (0, s * nj + j))],
        out_shape=[SDS((2, t, d), BF16), SDS(proj.shape, BF16), SDS((1, 2 * d), F32)],
        compiler_params=_params(("arbitrary", "arbitrary", "arbitrary")),
        name="merge_bwd")(dmerged, proj, b_gate, pa, pb)


def _ffn_in_swiglu(h, w, deps=()):
    t, d = h.shape
    f = w.shape[1] // 2
    tm, tn = _tile(t, 2048), _tile(f, MM_TILE_N)
    nj = f // tn

    def body(h_ref, wg_ref, wu_ref, *rest):
        g_ref, u_ref, a_ref = rest[len(deps):]
        hv = h_ref[...]
        g = jnp.dot(hv, wg_ref[...], preferred_element_type=F32)
        u = jnp.dot(hv, wu_ref[...], preferred_element_type=F32)
        g_ref[...] = g
        u_ref[...] = u
        a_ref[...] = (g * _sigmoid(g) * u).astype(BF16)

    tile = pl.BlockSpec((tm, tn), lambda i, j: (i, j))
    return pl.pallas_call(
        body, grid=(t // tm, nj),
        in_specs=[pl.BlockSpec((tm, d), lambda i, j: (i, 0)), pl.BlockSpec((d, tn), lambda i, j: (0, j)),
                  pl.BlockSpec((d, tn), lambda i, j: (0, nj + j))] + [ANY] * len(deps),
        out_specs=[tile, tile, tile], out_shape=[SDS((t, f), F32), SDS((t, f), F32), SDS((t, f), BF16)],
        compiler_params=_params(("parallel", "parallel")), name="ffn_in_swiglu")(h, w, w, *deps)


def _swiglu_bwd(dact, gate, up):
    t, f = gate.shape
    tm = _tile(t, 128)

    def body(d_ref, g_ref, u_ref, o_ref):
        g, dv = g_ref[...], d_ref[...]
        sg = _sigmoid(g)
        o_ref[:, :f] = (dv * u_ref[...] * _dsilu(g, sg)).astype(BF16)
        o_ref[:, f:] = (dv * (g * sg)).astype(BF16)

    row = pl.BlockSpec((tm, f), lambda i: (i, 0))
    return pl.pallas_call(
        body, grid=(t // tm,), in_specs=[row, row, row],
        out_specs=pl.BlockSpec((tm, 2 * f), lambda i: (i, 0)), out_shape=SDS((t, 2 * f), BF16),
        compiler_params=_params(("parallel",)), name="swiglu_bwd")(dact, gate, up)


def _ffn_out_loss(act, w, x_res, target):
    t, d = x_res.shape
    k = act.shape[1]
    tm, tn = _tile(t, 1024), _tile(d, MM_TILE_N)

    def body(a_ref, w_ref, r_ref, t_ref, dy_ref, dyb_ref, l_ref):
        @pl.when(jnp.logical_and(pl.program_id(0) == 0, pl.program_id(1) == 0))
        def _():
            l_ref[...] = jnp.zeros_like(l_ref)

        y = jnp.dot(a_ref[...], w_ref[...], preferred_element_type=F32) + r_ref[...]
        e = y - t_ref[...]
        dy = e * (1.0 / d)
        dy_ref[...] = dy
        dyb_ref[...] = dy.astype(BF16)
        l_ref[...] += (0.5 / d) * jnp.sum(jnp.sum(e * e, axis=-1, keepdims=True), axis=0, keepdims=True)

    tile = pl.BlockSpec((tm, tn), lambda i, j: (i, j))
    return pl.pallas_call(
        body, grid=(t // tm, d // tn),
        in_specs=[pl.BlockSpec((tm, k), lambda i, j: (i, 0)), pl.BlockSpec((k, tn), lambda i, j: (0, j)), tile, tile],
        out_specs=[tile, tile, pl.BlockSpec((1, 1), lambda i, j: (0, 0))],
        out_shape=[SDS((t, d), F32), SDS((t, d), BF16), SDS((1, 1), F32)],
        compiler_params=_params(("arbitrary", "arbitrary")), name="ffn_out_loss")(act, w, x_res, target)


def _rel_onehot(qi):
    p = lax.broadcasted_iota(jnp.int32, (REL_LANES, BAND), 1)
    r = lax.broadcasted_iota(jnp.int32, (REL_LANES, BAND), 0)
    idx = jnp.clip(qi + PAD - p, -REL_FUTURE, REL_PAST) + REL_FUTURE
    return (idx == r).astype(F32)


def _relbias_expand(rb):
    h = rb.shape[0]

    def body(rb_ref, o_ref):
        def step(qi, _):
            o_ref[qi] = _fdot(rb_ref[...], _rel_onehot(qi))
            return 0

        lax.fori_loop(0, CHUNK, step, 0)

    return pl.pallas_call(body, out_shape=SDS((CHUNK, h, BAND), F32), compiler_params=_params(),
                          name="relbias_expand")(rb)


def _relbias_reduce(dbias):
    h = dbias.shape[1]

    def body(db_ref, o_ref):
        def step(qi, acc):
            return acc + lax.dot_general(db_ref[qi], _rel_onehot(qi), NT, precision=HIGHEST,
                                         preferred_element_type=F32)

        o_ref[...] = lax.fori_loop(0, CHUNK, step, jnp.zeros((h, REL_LANES), F32))

    return pl.pallas_call(body, out_shape=SDS((h, REL_LANES), F32), compiler_params=_params(),
                          name="relbias_reduce")(dbias)


def _lower_bound(l_ref):
    l0, l1 = l_ref[0:1, :], l_ref[1:2, :]
    m = jnp.maximum(l0, l1)
    e0, e1 = jnp.exp(l0 - m), jnp.exp(l1 - m)
    return e0 / (e0 + e1)


def _tri(lower):
    r = lax.broadcasted_iota(jnp.int32, (CHUNK, CHUNK), 0)
    c = lax.broadcasted_iota(jnp.int32, (CHUNK, CHUNK), 1)
    return r >= c if lower else r <= c


def _hgrn_intra(qs, kk, b_s):
    rows = lax.broadcasted_iota(jnp.int32, (CHUNK, HEAD), 0)
    b = b_s[...]
    out = []
    for i in range(CHUNK // SUB):
        lo = i * SUB
        ref = jnp.zeros((1, HEAD), F32) if i == 0 else b_s[lo - 1:lo, :]
        eq = jnp.exp(b[lo:lo + SUB] - ref)
        qt = _split(qs[lo:lo + SUB] * eq)
        e = jnp.where(rows < lo + SUB, jnp.exp(jnp.minimum(ref - b, EXP_CLAMP)), 0.0)
        kt = _split(kk * e)
        out.append((eq, qt, e, kt))
    return out


def _hgrn_scores(blocks):
    tr = lax.broadcasted_iota(jnp.int32, (SUB, CHUNK), 0)
    tc = lax.broadcasted_iota(jnp.int32, (SUB, CHUNK), 1)
    return jnp.concatenate([jnp.where(tc <= tr + i * SUB, _dot3(qt, kt, NT), 0.0)
                            for i, (_, qt, _, kt) in enumerate(blocks)], axis=0)


def _hgrn_fwd(proj, lb_logits, gain, n_heads):
    t = proj.shape[0]
    nc = t // CHUNK
    da = n_heads * HEAD
    hp = MIX_HEADS
    wide = hp * HEAD

    def body(q_ref, f_ref, i_ref, g_ref, l_ref, gain_ref, y_ref, o_ref, st_ref, state, b_s):
        state[...] = jnp.zeros_like(state)
        lb_all = _lower_bound(l_ref)
        tril = _tri(True).astype(F32)

        def chunks(i, _):
            dot = functools.partial(lax.dot_general, preferred_element_type=F32)
            items = []
            for u in range(MIX_UNROLL):
                for hh in range(hp):
                    j = i * MIX_UNROLL + u
                    sl = pl.ds(pl.multiple_of(j * CHUNK, CHUNK), CHUNK)
                    cols = slice(hh * HEAD, (hh + 1) * HEAD)
                    lb = lb_all[:, cols]
                    fg = lb + (1.0 - lb) * _sigmoid(f_ref[sl, cols])
                    qv = q_ref[sl, cols]
                    gv = g_ref[sl, cols]
                    items.append(dict(hh=hh, j=j, sl=sl, cols=cols, lf=jnp.log(fg), kk=1.0 - fg, qs=qv * _sigmoid(qv),
                                      vb=i_ref[sl, cols].astype(BF16), gate=gv * _sigmoid(gv)))
            for it in items:
                it["b"] = _fdot(tril, it["lf"])
            for slot, it in enumerate(items):
                b = it["b"]
                b_s[slot] = b
                it["blocks"] = _hgrn_intra(it["qs"], it["kk"], b_s.at[slot])
                it["ebl"] = jnp.exp(b_s[slot, CHUNK - 1:CHUNK, :])
                it["qe"] = (it["qs"] * jnp.exp(b)).astype(BF16)
                it["kd"] = (it["kk"] * jnp.exp(b_s[slot, CHUNK - 1:CHUNK, :] - b)).astype(BF16)
            for it in items:
                it["a"] = _hgrn_scores(it["blocks"]).astype(BF16)
            for it in items:
                it["kv"] = dot(it["vb"], it["kd"], TN)
                it["o"] = dot(it["a"], it["vb"], NN)
            s_now = [state[hh] for hh in range(hp)]
            for it in items:
                it["s_in"] = s_now[it["hh"]]
                s_now[it["hh"]] = it["s_in"] * it["ebl"] + it["kv"]
            for hh in range(hp):
                state[hh] = s_now[hh]
            for it in items:
                it["o"] = it["o"] + dot(it["qe"], it["s_in"].astype(BF16), NT)
            for it in items:
                o, sl, cols = it["o"], it["sl"], it["cols"]
                st_ref[it["hh"], it["j"]] = it["s_in"]
                o_ref[sl, cols] = o
                rr = lax.rsqrt(jnp.mean(o * o, axis=-1, keepdims=True) + EPS)
                y_ref[sl, cols] = (o * rr * gain_ref[:, cols] * it["gate"]).astype(BF16)
            return 0

        assert nc % MIX_UNROLL == 0, (nc, MIX_UNROLL)
        lax.fori_loop(0, nc // MIX_UNROLL, chunks, 0)

    col = lambda k: pl.BlockSpec((t, wide), lambda h: (0, k * (n_heads // hp) + h))
    vec = pl.BlockSpec((1, wide), lambda h: (0, h))
    return pl.pallas_call(
        body, grid=(n_heads // hp,),
        in_specs=[col(0), col(1), col(2), col(3), pl.BlockSpec((2, wide), lambda h: (0, h)), vec],
        out_specs=[pl.BlockSpec((t, wide), lambda h: (0, h)), pl.BlockSpec((t, wide), lambda h: (0, h)),
                   pl.BlockSpec((hp, nc, HEAD, HEAD), lambda h: (h, 0, 0, 0))],
        out_shape=[SDS((t, da), BF16), SDS((t, da), F32), SDS((n_heads, nc, HEAD, HEAD), F32)],
        scratch_shapes=[pltpu.VMEM((hp, HEAD, HEAD), F32), pltpu.VMEM((hp * MIX_UNROLL, CHUNK, HEAD), F32)],
        compiler_params=_params(("parallel",)), name="hgrn_fwd")(proj, proj, proj, proj, lb_logits, gain)


def _hgrn_bwd(dproj, proj, o_pre, states, dy, lb_logits, gain, n_heads, deps=()):
    t = proj.shape[0]
    nc = t // CHUNK
    da = n_heads * HEAD
    hp = MIX_HEADS
    wide = hp * HEAD

    def body(*refs):
        (q_ref, f_ref, i_ref, g_ref, o_ref, st_ref, dy_ref, l_ref, gain_ref,
         dproj_ref, dl_ref, dgain_ref, res, dstate, b_s) = refs[1 + len(deps):]

        @pl.when(pl.program_id(1) == 0)
        def _():
            dstate[...] = jnp.zeros_like(dstate)
            lb_all = _lower_bound(l_ref)
            tril_m, tril, triu = _tri(True), _tri(True).astype(F32), _tri(False).astype(F32)
            last = lax.broadcasted_iota(jnp.int32, (CHUNK, HEAD), 0) == CHUNK - 1

            def chunks(i, carry):
                dot = functools.partial(lax.dot_general, preferred_element_type=F32)
                items = []
                for u in range(MIX_UNROLL_BWD):
                    for hh in range(hp):
                        j = nc - 1 - (i * MIX_UNROLL_BWD + u)
                        sl = pl.ds(pl.multiple_of(j * CHUNK, CHUNK), CHUNK)
                        cols = slice(hh * HEAD, (hh + 1) * HEAD)
                        lb, gain_v = lb_all[:, cols], gain_ref[:, cols]
                        sg = _sigmoid(f_ref[sl, cols])
                        fg = lb + (1.0 - lb) * sg
                        qv = q_ref[sl, cols]
                        sq = _sigmoid(qv)
                        gv = g_ref[sl, cols]
                        sgg = _sigmoid(gv)
                        silg = gv * sgg
                        o = o_ref[sl, cols]
                        dyv = dy_ref[sl, cols]
                        rr = lax.rsqrt(jnp.mean(o * o, axis=-1, keepdims=True) + EPS)
                        on = o * rr
                        don = dyv * gain_v * silg
                        do = (rr * don - o * (rr * rr * rr) * jnp.mean(don * o, axis=-1, keepdims=True)).astype(BF16)
                        items.append(dict(
                            hh=hh, j=j, sl=sl, cols=cols, lb=lb, sg=sg, fg=fg, kk=1.0 - fg, qv=qv, sq=sq, qs=qv * sq,
                            vb=i_ref[sl, cols].astype(BF16), do=do, dg=dyv * on * gain_v * _dsilu(gv, sgg),
                            dgain=jnp.sum(dyv * on * silg, axis=0, keepdims=True)))
                for it in items:
                    it["b"] = _fdot(tril, jnp.log(it["fg"]))
                for slot, it in enumerate(items):
                    b = it["b"]
                    b_s[slot] = b
                    it["blocks"] = _hgrn_intra(it["qs"], it["kk"], b_s.at[slot])
                    bl = b_s[slot, CHUNK - 1:CHUNK, :]
                    it["eb"], it["ebl"], it["ekd"] = jnp.exp(b), jnp.exp(bl), jnp.exp(bl - b)
                    it["s_in"] = st_ref[it["hh"], it["j"]]
                for it in items:
                    it["a"] = _hgrn_scores(it["blocks"]).astype(BF16)
                    it["da"] = jnp.where(tril_m, dot(it["do"], it["vb"], NT), 0.0)
                for it in items:
                    dq_rows = []
                    dk = jnp.zeros((CHUNK, HEAD), F32)
                    for blk, (eq, qt, e, kt) in enumerate(it["blocks"]):
                        da_i = _split(it["da"][blk * SUB:(blk + 1) * SUB])
                        dq_rows.append(eq * _dot3(da_i, kt, NN))
                        dk = dk + e * _dot3(da_i, qt, TN)
                    it["dq"] = jnp.concatenate(dq_rows, axis=0) + dot(it["do"], it["s_in"].astype(BF16), NN) * it["eb"]
                    it["dk"] = dk
                    it["dv"] = dot(it["a"], it["do"], TN)
                    it["g"] = dot(it["do"], (it["qs"] * it["eb"]).astype(BF16), TN)
                ds_now = [dstate[hh] for hh in range(hp)]
                for it in items:
                    it["ds_out"] = ds_now[it["hh"]]
                    ds_now[it["hh"]] = it["ds_out"] * it["ebl"] + it["g"]
                for hh in range(hp):
                    dstate[hh] = ds_now[hh]
                for it in items:
                    dsb = it["ds_out"].astype(BF16)
                    it["dv"] = it["dv"] + dot((it["kk"] * it["ekd"]).astype(BF16), dsb, NT)
                    it["dk_state"] = it["ekd"] * dot(it["vb"], dsb, NN)
                for it in items:
                    kk, dk_state = it["kk"], it["dk_state"]
                    it["dk"] = it["dk"] + dk_state
                    extra = (jnp.sum(kk * dk_state, axis=0, keepdims=True)
                             + it["ebl"] * jnp.sum(it["s_in"] * it["ds_out"], axis=0, keepdims=True))
                    it["db"] = it["qs"] * it["dq"] - kk * it["dk"] + jnp.where(last, extra, 0.0)
                for it in items:
                    it["dlf"] = _fdot(triu, it["db"])
                carry = list(carry)
                for it in items:
                    hh, sl, cols, sg, lb = it["hh"], it["sl"], it["cols"], it["sg"], it["lb"]
                    dfg = it["dlf"] / it["fg"] - it["dk"]
                    dlb_acc, dgain_acc = carry[hh]
                    carry[hh] = (dlb_acc + jnp.sum(dfg * (1.0 - sg), axis=0, keepdims=True), dgain_acc + it["dgain"])
                    res[0, sl, cols] = (it["dq"] * _dsilu(it["qv"], it["sq"])).astype(BF16)
                    res[1, sl, cols] = (dfg * (1.0 - lb) * sg * (1.0 - sg)).astype(BF16)
                    res[2, sl, cols] = it["dv"].astype(BF16)
                    res[3, sl, cols] = it["dg"].astype(BF16)
                return tuple(carry)

            assert nc % MIX_UNROLL_BWD == 0, (nc, MIX_UNROLL_BWD)
            zero = jnp.zeros((1, HEAD), F32)
            sums = lax.fori_loop(0, nc // MIX_UNROLL_BWD, chunks, ((zero, zero),) * hp)
            for hh, (dlb, dgain) in enumerate(sums):
                cols = slice(hh * HEAD, (hh + 1) * HEAD)
                lb = lb_all[:, cols]
                dgain_ref[:, cols] = dgain
                dl0 = dlb * lb * (1.0 - lb)
                dl_ref[0:1, cols] = dl0
                dl_ref[1:2, cols] = -dl0

        dproj_ref[...] = res[pl.program_id(1)]

    ng = n_heads // hp
    col = lambda k: pl.BlockSpec((t, wide), lambda h, p: (0, k * ng + h))
    head = pl.BlockSpec((t, wide), lambda h, p: (0, h))
    vec = pl.BlockSpec((1, wide), lambda h, p: (0, h))
    return pl.pallas_call(
        body, grid=(ng, 4),
        in_specs=[ANY] * (1 + len(deps)) + [col(0), col(1), col(2), col(3), head,
                  pl.BlockSpec((hp, nc, HEAD, HEAD), lambda h, p: (h, 0, 0, 0)),
                  head, pl.BlockSpec((2, wide), lambda h, p: (0, h)), vec],
        out_specs=[pl.BlockSpec((t, wide), lambda h, p: (0, p * ng + h)),
                   pl.BlockSpec((2, wide), lambda h, p: (0, h)), vec],
        out_shape=[SDS(dproj.shape, BF16), SDS((2, da), F32), SDS((1, da), F32)],
        scratch_shapes=[pltpu.VMEM((4, t, wide), BF16), pltpu.VMEM((hp, HEAD, HEAD), F32),
                        pltpu.VMEM((hp * MIX_UNROLL_BWD, CHUNK, HEAD), F32)],
        input_output_aliases={0: 0}, compiler_params=_params(("arbitrary", "arbitrary")),
        name="hgrn_bwd")(dproj, *deps, proj, proj, proj, proj, o_pre, states, dy, lb_logits, gain)


ROWS = 256


def _head_norm(x_ref, gain, dst, dst_off, t):
    def step(i, _):
        sl = pl.ds(pl.multiple_of(i * ROWS, ROWS), ROWS)
        xv = x_ref[sl, :]
        r = lax.rsqrt(jnp.mean(xv * xv, axis=-1, keepdims=True) + EPS)
        dst[pl.ds(pl.multiple_of(dst_off + i * ROWS, ROWS), ROWS), :] = (xv * r * gain).astype(BF16)
        return 0

    lax.fori_loop(0, t // ROWS, step, 0)


def _head_norm_bwd(x_ref, gain, dn_ref, dn_off, out, slot, t):
    def step(i, acc):
        sl = pl.ds(pl.multiple_of(i * ROWS, ROWS), ROWS)
        xv = x_ref[sl, :]
        dn = dn_ref[pl.ds(pl.multiple_of(dn_off + i * ROWS, ROWS), ROWS), :]
        r = lax.rsqrt(jnp.mean(xv * xv, axis=-1, keepdims=True) + EPS)
        u = dn * gain
        out[slot, sl, :] = r * u - xv * (r * r * r) * jnp.mean(u * xv, axis=-1, keepdims=True)
        return acc + jnp.sum(dn * (xv * r), axis=0, keepdims=True)

    return lax.fori_loop(0, t // ROWS, step, jnp.zeros((1, HEAD), F32))


def _attn_scores(qn, kpad, n):
    qc = qn[pl.ds(pl.multiple_of(n * CHUNK, CHUNK), CHUNK), :]
    band = pl.ds(pl.multiple_of(n * CHUNK, CHUNK), BAND)
    return qc, band, lax.dot_general(qc, kpad[band, :], NT, preferred_element_type=F32)


def _attn_softmax(raw, bias_ref, n):
    s = raw * (HEAD ** -0.5) + bias_ref[0]
    col = lax.broadcasted_iota(jnp.int32, (CHUNK, BAND), 1)
    s = jnp.where(col >= PAD - n * CHUNK, s, -jnp.inf)
    p = jnp.exp(s - jnp.max(s, axis=-1, keepdims=True))
    return p / jnp.sum(p, axis=-1, keepdims=True)


def _attn_fwd(proj, q_gain, k_gain, bias, n_heads, col0):
    t = proj.shape[0]
    nc = t // CHUNK

    def body(q_ref, k_ref, v_ref, qg_ref, kg_ref, bias_ref, y_ref, qn, kpad, vpad):
        kpad[0:PAD, :] = jnp.zeros((PAD, HEAD), BF16)
        vpad[0:PAD, :] = jnp.zeros((PAD, HEAD), BF16)
        _head_norm(q_ref, qg_ref[...], qn, 0, t)
        _head_norm(k_ref, kg_ref[...], kpad, PAD, t)

        def copy_v(i, _):
            vpad[pl.ds(pl.multiple_of(PAD + i * ROWS, ROWS), ROWS), :] = v_ref[
                pl.ds(pl.multiple_of(i * ROWS, ROWS), ROWS), :].astype(BF16)
            return 0

        lax.fori_loop(0, t // ROWS, copy_v, 0)

        def chunks(i, _):
            ns = [i * ATT_UNROLL + u for u in range(ATT_UNROLL)]
            scored = [_attn_scores(qn, kpad, n) for n in ns]
            probs = [_attn_softmax(raw, bias_ref, n).astype(BF16) for n, (_, _, raw) in zip(ns, scored)]
            outs = [lax.dot_general(p, vpad[band, :], NN, preferred_element_type=F32).astype(BF16)
                    for p, (_, band, _) in zip(probs, scored)]
            for n, o in zip(ns, outs):
                y_ref[pl.ds(pl.multiple_of(n * CHUNK, CHUNK), CHUNK), :] = o
            return 0

        assert nc % ATT_UNROLL == 0, (nc, ATT_UNROLL)
        lax.fori_loop(0, nc // ATT_UNROLL, chunks, 0)

    col = lambda k: pl.BlockSpec((t, HEAD), lambda h: (0, col0 + k * n_heads + h))
    vec = pl.BlockSpec((1, HEAD), lambda h: (0, 0))
    return pl.pallas_call(
        body, grid=(n_heads,),
        in_specs=[col(0), col(1), col(2), vec, vec, pl.BlockSpec((1, CHUNK, BAND), lambda h: (h, 0, 0))],
        out_specs=pl.BlockSpec((t, HEAD), lambda h: (0, h)), out_shape=SDS((t, n_heads * HEAD), BF16),
        scratch_shapes=[pltpu.VMEM((t, HEAD), BF16), pltpu.VMEM((t + PAD, HEAD), BF16), pltpu.VMEM((t + PAD, HEAD), BF16)],
        compiler_params=_params(("parallel",)), name="attn_fwd")(proj, proj, proj, q_gain, k_gain, bias)


def _attn_bwd(dproj, proj, q_gain, k_gain, bias, dy, n_heads, col0, deps=()):
    t = proj.shape[0]
    nc = t // CHUNK

    def body(*refs):
        (q_ref, k_ref, v_ref, qg_ref, kg_ref, bias_ref, dy_ref,
         dproj_ref, dbias_ref, dqg_ref, dkg_ref, qn, kpad, vpad, dqn, dk_acc, dv_acc, res) = refs[1 + len(deps):]
        h, part = pl.program_id(0), pl.program_id(1)

        @pl.when(part == 0)
        def _():
            kpad[0:PAD, :] = jnp.zeros((PAD, HEAD), BF16)
            vpad[0:PAD, :] = jnp.zeros((PAD, HEAD), BF16)
            _head_norm(q_ref, qg_ref[...], qn, 0, t)
            _head_norm(k_ref, kg_ref[...], kpad, PAD, t)

            def prep(i, _):
                sl = pl.ds(pl.multiple_of(PAD + i * ROWS, ROWS), ROWS)
                vpad[sl, :] = v_ref[pl.ds(pl.multiple_of(i * ROWS, ROWS), ROWS), :].astype(BF16)
                return 0

            lax.fori_loop(0, t // ROWS, prep, 0)

            def clear(i, _):
                sl = pl.ds(pl.multiple_of(i * ROWS, ROWS), ROWS)
                dk_acc[sl, :] = jnp.zeros((ROWS, HEAD), F32)
                dv_acc[sl, :] = jnp.zeros((ROWS, HEAD), F32)
                return 0

            lax.fori_loop(0, (t + PAD) // ROWS, clear, 0)
            dbias_ref[0] = jnp.zeros((CHUNK, BAND), F32)

            def chunks(i, _):
                dot = functools.partial(lax.dot_general, preferred_element_type=F32)
                ns = [i * ATT_UNROLL_BWD + u for u in range(ATT_UNROLL_BWD)]
                scored = [_attn_scores(qn, kpad, n) for n in ns]
                dos = [dy_ref[pl.ds(pl.multiple_of(n * CHUNK, CHUNK), CHUNK), :].astype(BF16) for n in ns]
                dps = [dot(do, vpad[band, :], NT) for do, (_, band, _) in zip(dos, scored)]
                ps, dss = [], []
                for n, (_, _, raw), dp in zip(ns, scored, dps):
                    p = _attn_softmax(raw, bias_ref, n)
                    ds = p * (dp - jnp.sum(dp * p, axis=-1, keepdims=True))
                    dbias_ref[0] += ds
                    ps.append(p.astype(BF16))
                    dss.append((ds * (HEAD ** -0.5)).astype(BF16))
                dqs = [dot(d, kpad[band, :], NN) for d, (_, band, _) in zip(dss, scored)]
                dks = [dot(d, qc, TN) for d, (qc, _, _) in zip(dss, scored)]
                dvs = [dot(p, do, TN) for p, do in zip(ps, dos)]
                for n, (_, band, _), dq, dk, dv in zip(ns, scored, dqs, dks, dvs):
                    dqn[pl.ds(pl.multiple_of(n * CHUNK, CHUNK), CHUNK), :] = dq
                    dk_acc[band, :] += dk
                    dv_acc[band, :] += dv
                return 0

            assert nc % ATT_UNROLL_BWD == 0, (nc, ATT_UNROLL_BWD)
            lax.fori_loop(0, nc // ATT_UNROLL_BWD, chunks, 0)
            dqg = _head_norm_bwd(q_ref, qg_ref[...], dqn, 0, res, 0, t)
            dkg = _head_norm_bwd(k_ref, kg_ref[...], dk_acc, PAD, res, 1, t)

            def put_v(i, _):
                sl = pl.ds(pl.multiple_of(i * ROWS, ROWS), ROWS)
                res[2, sl, :] = dv_acc[pl.ds(pl.multiple_of(PAD + i * ROWS, ROWS), ROWS), :]
                return 0

            lax.fori_loop(0, t // ROWS, put_v, 0)

            @pl.when(h == 0)
            def _():
                dqg_ref[...] = jnp.zeros_like(dqg_ref)
                dkg_ref[...] = jnp.zeros_like(dkg_ref)

            dqg_ref[...] += dqg
            dkg_ref[...] += dkg

        dproj_ref[...] = res[part].astype(BF16)

    col = lambda k: pl.BlockSpec((t, HEAD), lambda h, p: (0, col0 + k * n_heads + h))
    vec = pl.BlockSpec((1, HEAD), lambda h, p: (0, 0))
    btile = pl.BlockSpec((1, CHUNK, BAND), lambda h, p: (h, 0, 0))
    return pl.pallas_call(
        body, grid=(n_heads, 3),
        in_specs=[ANY] * (1 + len(deps)) + [col(0), col(1), col(2), vec, vec, btile,
                                            pl.BlockSpec((t, HEAD), lambda h, p: (0, h))],
        out_specs=[pl.BlockSpec((t, HEAD), lambda h, p: (0, col0 + p * n_heads + h)), btile, vec, vec],
        out_shape=[SDS(dproj.shape, BF16), SDS((n_heads, CHUNK, BAND), F32), SDS((1, HEAD), F32), SDS((1, HEAD), F32)],
        scratch_shapes=[pltpu.VMEM((t, HEAD), BF16), pltpu.VMEM((t + PAD, HEAD), BF16), pltpu.VMEM((t + PAD, HEAD), BF16),
                        pltpu.VMEM((t, HEAD), F32), pltpu.VMEM((t + PAD, HEAD), F32), pltpu.VMEM((t + PAD, HEAD), F32),
                        pltpu.VMEM((3, t, HEAD), F32)],
        input_output_aliases={0: 0}, compiler_params=_params(("arbitrary", "arbitrary")),
        name="attn_bwd")(dproj, *deps, proj, proj, proj, q_gain, k_gain, bias, dy)


def _place():
    x, y, c = lax.axis_index("x"), lax.axis_index("y"), lax.axis_index("c")
    others = [(1 - x, y), (x, 1 - y), (1 - x, 1 - y)]
    return x, y, c, others


def _chunk_of(ref, kind, chip, half, shard_shape):
    r, n = shard_shape
    hr = r // 2
    if kind == "col":
        rows = pl.ds(0, r) if half is None else pl.ds(half * hr, hr)
        return ref.at[rows, pl.ds(chip * n, n)]
    rows = pl.ds(chip * r, r) if half is None else pl.ds(chip * r + half * hr, hr)
    return ref.at[rows, :]


EFFECT = pltpu.SideEffectType.DATAFLOW_SIDE_EFFECTING


def _start_copies(name, bufs, plan, n, deps):
    nb, nd = len(bufs), len(deps)

    def body(*refs):
        send, recv, token = refs[nb + nd], refs[nb + nd + 1], refs[-1]
        for cp in plan(refs[:nb], send, recv)[0]:
            cp.start()
        token[...] = jnp.zeros_like(token)

    out = pl.pallas_call(
        body, name=name,
        out_shape=(pltpu.SemaphoreType.DMA((n,)), pltpu.SemaphoreType.DMA((n,)),
                   *[pltpu.HBM(b.shape, b.dtype) for b in bufs], SDS((8, 128), F32)),
        in_specs=[HBM] * nb + [ANY] * nd,
        out_specs=(SEM, SEM, *[HBM] * nb, pl.BlockSpec(memory_space=pltpu.VMEM)),
        input_output_aliases={i: 2 + i for i in range(nb)},
        compiler_params=pltpu.CompilerParams(has_side_effects=EFFECT),
    )(*[pltpu.with_memory_space_constraint(b, pltpu.HBM) for b in bufs], *deps)
    return out[0], out[1], list(out[2:2 + nb]), out[-1]


def _wait_copies(name, bufs, send, recv, plan, after):
    nb = len(bufs)

    def body(*refs):
        sends, recvs = plan(refs[:nb], refs[nb], refs[nb + 1])
        for cp in sends:
            cp.wait_send()
        for cp in recvs:
            cp.wait_recv()

    out = pl.pallas_call(
        body, name=name, out_shape=tuple(pltpu.HBM(b.shape, b.dtype) for b in bufs),
        in_specs=[HBM] * nb + [SEM, SEM] + [ANY] * len(after), out_specs=tuple([HBM] * nb),
        input_output_aliases={i: i for i in range(nb)},
        compiler_params=pltpu.CompilerParams(has_side_effects=EFFECT),
    )(*bufs, send, recv, *after)
    return list(out)


def _remote(src, dst, send, recv, i, dev):
    return pltpu.make_async_remote_copy(src_ref=src, dst_ref=dst, send_sem=send.at[i], recv_sem=recv.at[i],
                                        device_id=dev, device_id_type=MESH)


ALL_RELATIONS = (0, 1, 2)


def _plan_gather_ici(kinds, shapes, rels=ALL_RELATIONS):
    def plan(refs, send, recv):
        x, y, c, others = _place()
        sends, recvs = [], []
        for w, (kind, ss) in enumerate(zip(kinds, shapes)):
            for p in rels:
                px, py = others[p]
                mine = _chunk_of(refs[w], kind, 2 * x + y, c, ss)
                theirs = _chunk_of(refs[w], kind, 2 * px + py, c, ss)
                sends.append(_remote(mine, mine, send, recv, 3 * w + p, (px, py, c)))
                recvs.append(_remote(theirs, theirs, send, recv, 3 * w + p, (px, py, c)))
        return sends, recvs

    return plan, 3 * len(kinds)


def _plan_gather_pass(kinds, shapes, rels=ALL_RELATIONS):
    def plan(refs, send, recv):
        x, y, c, others = _place()
        sends, recvs = [], []
        for w, (kind, ss) in enumerate(zip(kinds, shapes)):
            for i, p in enumerate(rels):
                px, py = others[p]
                got = _chunk_of(refs[w], kind, 2 * px + py, c, ss)
                coming = _chunk_of(refs[w], kind, 2 * px + py, 1 - c, ss)
                sends.append(_remote(got, got, send, recv, len(rels) * w + i, (x, y, 1 - c)))
                recvs.append(_remote(coming, coming, send, recv, len(rels) * w + i, (x, y, 1 - c)))
        return sends, recvs

    return plan, len(rels) * len(kinds)


def _plan_pair(kinds, shapes):
    nw = len(kinds)

    def plan(refs, send, recv):
        x, y, c, _ = _place()
        sends = []
        for w, (kind, ss) in enumerate(zip(kinds, shapes)):
            for k in range(4):
                sends.append(_remote(_chunk_of(refs[w], kind, k, 1 - c, ss), refs[nw + w].at[k], send, recv,
                                     4 * w + k, (x, y, 1 - c)))
        return sends, sends

    return plan, 4 * nw


def _plan_chip(nw):
    def plan(refs, send, recv):
        x, y, c, others = _place()
        sends = []
        for w in range(nw):
            for p, (px, py) in enumerate(others):
                sends.append(_remote(refs[w].at[p], refs[nw + w].at[p], send, recv, 3 * w + p, (px, py, c)))
        return sends, sends

    return plan, 3 * nw


def _plan_share(nw):
    def plan(refs, send, recv):
        x, y, c, _ = _place()
        sends = [_remote(refs[w].at[c], refs[w].at[c], send, recv, w, (x, y, 1 - c)) for w in range(nw)]
        recvs = [_remote(refs[w].at[1 - c], refs[w].at[1 - c], send, recv, w, (x, y, 1 - c)) for w in range(nw)]
        return sends, recvs

    return plan, nw


def _grad_half_spec(kind, tr, tn, nr, nn, chunk):
    if kind == "col":
        return pl.BlockSpec((tr, tn), lambda *a: (a[-1][1] * nr + a[-3], chunk(*a) * nn + a[-2]))
    return pl.BlockSpec((tr, tn), lambda *a: ((2 * chunk(*a) + a[-1][1]) * nr + a[-3], a[-2]))


def _pair_add(grad, got, kind, shard_shape, pos, name):
    r, n = shard_shape
    hr = r // 2
    tr, tn = _tile(hr, 256, 16), _tile(n, 1408)
    nr, nn = hr // tr, n // tn
    g_spec = _grad_half_spec(kind, tr, tn, nr, nn, lambda p, i, j, pos_: pos_[2 + p])
    r_spec = pl.BlockSpec((1, tr, tn), lambda p, i, j, pos_: (pos_[2 + p], i, j))
    o_spec = pl.BlockSpec((1, tr, tn), lambda p, i, j, pos_: (p, i, j))

    def body(pos_ref, g_ref, r_ref, o_ref):
        o_ref[0] = (g_ref[...] + r_ref[0]).astype(BF16)

    return pl.pallas_call(
        body,
        grid_spec=pltpu.PrefetchScalarGridSpec(num_scalar_prefetch=1, grid=(3, nr, nn), in_specs=[g_spec, r_spec],
                                               out_specs=o_spec),
        out_shape=SDS((3, hr, n), BF16),
        compiler_params=_params(("parallel", "parallel", "parallel")), name=name)(pos, grad, got)


def _chip_add(grad, got, got16, kind, shard_shape, pos, name):
    r, n = shard_shape
    hr = r // 2
    tr, tn = _tile(hr, 256, 16), _tile(n, 1408)
    nr, nn = hr // tr, n // tn
    g_spec = _grad_half_spec(kind, tr, tn, nr, nn, lambda i, j, pos_: pos_[0])
    r_spec = pl.BlockSpec((1, tr, tn), lambda i, j, pos_: (pos_[0], i, j))
    oth = pl.BlockSpec((3, tr, tn), lambda i, j, pos_: (0, i, j))

    def body(pos_ref, g_ref, r_ref, oth_ref, o_ref):
        own = g_ref[...] + r_ref[0]
        o_ref[0] = ((own + oth_ref[0].astype(F32)) + oth_ref[1].astype(F32)) + oth_ref[2].astype(F32)

    return pl.pallas_call(
        body,
        grid_spec=pltpu.PrefetchScalarGridSpec(num_scalar_prefetch=1, grid=(nr, nn), in_specs=[g_spec, r_spec, oth],
                                               out_specs=pl.BlockSpec((1, tr, tn), lambda i, j, pos_: (pos_[1], i, j))),
        out_shape=SDS((2, hr, n), F32), compiler_params=_params(("parallel", "parallel")),
        name=name)(pos, grad, got, got16)


def _adamw_math(w, g, m, v):
    m = ADAM_B1 * m + (1.0 - ADAM_B1) * g
    v = ADAM_B2 * v + (1.0 - ADAM_B2) * (g * g)
    m_hat = m / (1.0 - ADAM_B1 ** ADAM_STEP)
    v_hat = v / (1.0 - ADAM_B2 ** ADAM_STEP)
    return -ADAM_LR * (m_hat / (jnp.sqrt(v_hat) + ADAM_EPS) + ADAM_WD * w), m, v


def _adamw(w, g, m, v, name):
    r, n = w.shape
    tr, tn = _tile(r, 256, 16), _tile(n, 1408)

    def body(w_ref, g_ref, m_ref, v_ref, d_ref, nm_ref, nv_ref, go_ref):
        gv = g_ref[...]
        d_ref[...], nm_ref[...], nv_ref[...] = _adamw_math(w_ref[...], gv, m_ref[...], v_ref[...])
        go_ref[...] = gv

    tile = pl.BlockSpec((tr, tn), lambda i, j: (i, j))
    return pl.pallas_call(
        body, grid=(r // tr, n // tn), in_specs=[tile] * 4, out_specs=[tile] * 4, out_shape=[SDS((r, n), F32)] * 4,
        compiler_params=_params(("parallel", "parallel")), name=name)(w, g, m, v)


def _small_allreduce_adamw(g, w, m, v, deps=()):
    length = g.shape[1]

    def body(*refs):
        g_ref, w_ref, m_ref, v_ref = refs[:4]
        gs_ref, d_ref, nm_ref, nv_ref, buf, send, recv = refs[4 + len(deps):]
        x, y, c = lax.axis_index("x"), lax.axis_index("y"), lax.axis_index("c")
        me = 4 * x + 2 * y + c
        buf[me] = g_ref[...]
        cps = []
        for d in range(1, 8):
            peer = (x ^ (d >> 2), y ^ ((d >> 1) & 1), c ^ (d & 1))
            cp = pltpu.make_async_remote_copy(src_ref=buf.at[me], dst_ref=buf.at[me], send_sem=send.at[d - 1],
                                              recv_sem=recv.at[d - 1], device_id=peer, device_id_type=MESH)
            cp.start()
            cps.append(cp)
        for cp in cps:
            cp.wait()
        total = buf[0]
        for d in range(1, 8):
            total = total + buf[d]
        gs_ref[...] = total
        d_ref[...], nm_ref[...], nv_ref[...] = _adamw_math(w_ref[...], total, m_ref[...], v_ref[...])

    vm = pl.BlockSpec(memory_space=pltpu.VMEM)
    return pl.pallas_call(
        body, in_specs=[vm] * 4 + [ANY] * len(deps), out_specs=[vm] * 4, out_shape=[SDS((1, length), F32)] * 4,
        scratch_shapes=[pltpu.VMEM((8, 1, length), F32), pltpu.SemaphoreType.DMA((7,)), pltpu.SemaphoreType.DMA((7,))],
        compiler_params=pltpu.CompilerParams(has_side_effects=True), name="small_allreduce_adamw")(g, w, m, v, *deps)


def kernel(x, w_in, b_gate, norm_mix, norm_ffn, hgrn_lb_logits, hgrn_out_gain, q_gain, k_gain, rel_bias, w_proj_a, w_proj_b, w_out, w_ffn_in, w_ffn_out, loss_target, m_w_in, m_b_gate, m_norm_mix, m_norm_ffn, m_hgrn_lb_logits, m_hgrn_out_gain, m_q_gain, m_k_gain, m_rel_bias, m_w_proj_a, m_w_proj_b, m_w_out, m_w_ffn_in, m_w_ffn_out, v_w_in, v_b_gate, v_norm_mix, v_norm_ffn, v_hgrn_lb_logits, v_hgrn_out_gain, v_q_gain, v_k_gain, v_rel_bias, v_w_proj_a, v_w_proj_b, v_w_out, v_w_ffn_in, v_w_ffn_out):
    t, d = x.shape[1], x.shape[2]
    d_a = hgrn_out_gain.shape[1]
    h_a = d_a // HEAD
    h_b = rel_bias.shape[1]
    d_b = h_b * HEAD
    x0 = x.reshape(t, d)
    target = loss_target.reshape(t, d)
    ax, ay = lax.axis_index("x"), lax.axis_index("y")
    pos = jnp.stack([2 * ax + ay, lax.axis_index("c"), 2 * (1 - ax) + ay, 2 * ax + 1 - ay,
                     2 * (1 - ax) + 1 - ay]).astype(jnp.int32)

    names = ["w_in", "w_proj_a", "w_proj_b", "w_out", "w_ffn_in", "w_ffn_out"]
    big = dict(zip(names, [w_in[0], w_proj_a[0], w_proj_b[0], w_out[0], w_ffn_in[0], w_ffn_out[0]]))
    big_m = dict(zip(names, [m_w_in[0], m_w_proj_a[0], m_w_proj_b[0], m_w_out[0], m_w_ffn_in[0], m_w_ffn_out[0]]))
    big_v = dict(zip(names, [v_w_in[0], v_w_proj_a[0], v_w_proj_b[0], v_w_out[0], v_w_ffn_in[0], v_w_ffn_out[0]]))
    kind = dict(zip(names, ["col", "col", "col", "row", "col", "row"]))
    shape = {nm: big[nm].shape for nm in names}

    def gather_start(tag, group, deps):
        plan, n = _plan_gather_ici([kind[g] for g in group], [shape[g] for g in group])
        fulls = [_cast_into_full(big[g], kind[g], pos, "cast_" + g) for g in group]
        send, recv, bufs, token = _start_copies("gather_ici_start_" + tag, fulls, plan, n, deps)
        return (tag, group, plan, send, recv, bufs), token

    def gather_pass(state, after, rels=ALL_RELATIONS, part=""):
        tag, group, _, send, recv, bufs = state
        kinds_, shapes_ = [kind[g] for g in group], [shape[g] for g in group]
        bufs = _wait_copies("gather_ici_wait_" + tag + part, bufs, send, recv,
                            _plan_gather_ici(kinds_, shapes_, rels)[0], after)
        plan, n = _plan_gather_pass(kinds_, shapes_, rels)
        send2, recv2, bufs, token = _start_copies("gather_pass_start_" + tag + part, bufs, plan, n, ())
        return (tag + part, group, plan, send2, recv2, bufs), token

    def gather_done(state, after):
        tag, group, plan, send, recv, bufs = state
        return _wait_copies("gather_pass_wait_" + tag, bufs, send, recv, plan, after)

    def reduce_start(tag, group, grads, deps):
        plan, n = _plan_pair([kind[g] for g in group], [shape[g] for g in group])
        lands = [lax.empty((4, shape[g][0] // 2, shape[g][1]), F32) for g in group]
        send, recv, bufs, token = _start_copies("pair_start_" + tag, list(grads) + lands, plan, n, deps)
        return dict(tag=tag, group=group, plan=plan, send=send, recv=recv, bufs=bufs), token

    def reduce_pair_done(st, after):
        tag, group, nw = st["tag"], st["group"], len(st["group"])
        bufs = _wait_copies("pair_wait_" + tag, st["bufs"], st["send"], st["recv"], st["plan"], after)
        grads, gots = bufs[:nw], bufs[nw:]
        parts = [_pair_add(g, l, kind[nm], shape[nm], pos, "pair_add_" + nm) for g, l, nm in zip(grads, gots, group)]
        lands = [lax.empty((3, shape[g][0] // 2, shape[g][1]), BF16) for g in group]
        plan, n = _plan_chip(nw)
        send, recv, bufs, token = _start_copies("chip_start_" + tag, parts + lands, plan, n, ())
        return dict(st, plan=plan, send=send, recv=recv, bufs=bufs, grads=grads, gots=gots), token

    def reduce_chip_done(st, after):
        tag, group, nw = st["tag"], st["group"], len(st["group"])
        bufs = _wait_copies("chip_wait_" + tag, st["bufs"], st["send"], st["recv"], st["plan"], after)
        finals = [_chip_add(g, l, got16, kind[nm], shape[nm], pos, "chip_add_" + nm)
                  for g, l, got16, nm in zip(st["grads"], st["gots"], bufs[nw:], group)]
        plan, n = _plan_share(nw)
        send, recv, bufs, token = _start_copies("share_start_" + tag, finals, plan, n, ())
        return dict(st, plan=plan, send=send, recv=recv, bufs=bufs), token

    g_big, upd = {}, {}

    def reduce_finish(st, after):
        bufs = _wait_copies("share_wait_" + st["tag"], st["bufs"], st["send"], st["recv"], st["plan"], after)
        for full, nm in zip(bufs, st["group"]):
            upd[nm] = _adamw(big[nm], full.reshape(shape[nm]), big_m[nm], big_v[nm], "adamw_" + nm)
            g_big[nm] = upd[nm][3]

    ga, token = gather_start("a", ["w_in"], ())
    gb, token = gather_start("b", ["w_proj_a", "w_proj_b", "w_out"], (token,))
    gc, token = gather_start("c", ["w_ffn_in"], (token,))
    gd, token = gather_start("d", ["w_ffn_out"], (token,))
    h1, r1 = _rmsnorm_fwd(x0, norm_mix, "rmsnorm_mix")
    rb = jnp.pad(rel_bias[0], ((0, 0), (0, REL_LANES - N_REL)))
    bias = _relbias_expand(rb).transpose(1, 0, 2)
    proj = _matmul_chunks(h1, big["w_in"], (0,), None, pos, "proj_in_own", own_shard=True)
    ici_a = ga
    ga, token = gather_pass(ici_a, (h1, bias, proj, token), rels=(0, 1), part="_near")
    (wg_in,) = gather_done(ga, ())
    proj = _matmul_chunks(h1, wg_in, (2, 3), proj, pos, "proj_in_near")
    ga, token = gather_pass(ici_a[:5] + ([wg_in],), (proj,), rels=(2,), part="_far")
    (wg_in,) = gather_done(ga, ())
    proj = _matmul_chunks(h1, wg_in, (4,), proj, pos, "proj_in_far")
    y_a, o_pre, states = _hgrn_fwd(proj, hgrn_lb_logits, hgrn_out_gain, h_a)
    gb, token = gather_pass(gb, (y_a,))
    col_b = 4 * d_a // HEAD
    y_b = _attn_fwd(proj, q_gain, k_gain, bias, h_b, col_b)
    wg_pa, wg_pb, wg_out = gather_done(gb, (y_b,))
    pa = _matmul(y_a, wg_pa, name="proj_a", deps=(token,))
    pb = _matmul(y_b, wg_pb, name="proj_b")
    gate_off = 4 * d_a + 3 * d_b
    merged = _merge_fwd(proj, b_gate, pa, pb, gate_off)
    x2 = _matmul(merged, wg_out, res=x0, name="out_proj")
    gc, token = gather_pass(gc, (x2,))
    h2, r2 = _rmsnorm_fwd(x2, norm_ffn, "rmsnorm_ffn")
    (wg_fin,) = gather_done(gc, (h2,))
    ff_gate, ff_up, act = _ffn_in_swiglu(h2, wg_fin, deps=(token,))
    gd, token = gather_pass(gd, (act,))
    (wg_fout,) = gather_done(gd, ())
    dy, dy16, loss_part = _ffn_out_loss(act, wg_fout, x2, target)

    g_fout = _matmul(act, dy16, ta=True, name="dw_ffn_out")
    r_fout, token = reduce_start("fout", ["w_ffn_out"], [g_fout], ())
    dact = _matmul(dy16, wg_fout, tb=True, name="d_act", deps=(token,))
    r_fout, token = reduce_pair_done(r_fout, (dact,))
    dgu = _swiglu_bwd(dact, ff_gate, ff_up)
    g_fin = _matmul(h2, dgu, ta=True, name="dw_ffn_in", deps=(token,))
    r_fin, token = reduce_start("fin", ["w_ffn_in"], [g_fin], ())
    dh2 = _matmul(dgu, wg_fin, tb=True, name="d_h2", deps=(token,))
    r_fout, token_a = reduce_chip_done(r_fout, (dh2,))
    r_fin, token_b = reduce_pair_done(r_fin, (dh2,))
    dx2, dx2_16, g_norm_ffn = _rmsnorm_bwd(dh2, x2, r2, norm_ffn, dy, "rmsnorm_ffn_bwd")
    dmerged = _matmul(dx2_16, wg_out, tb=True, name="d_merged", deps=(token_a, token_b))
    dp_ab, dproj, g_bgate = _merge_bwd(dmerged, proj, b_gate, pa, pb, gate_off)
    g_out = _matmul(merged, dx2_16, ta=True, name="dw_out")
    g_pa = _matmul(y_a, dp_ab[0], ta=True, name="dw_proj_a")
    g_pb = _matmul(y_b, dp_ab[1], ta=True, name="dw_proj_b")
    r_mid, token = reduce_start("mid", ["w_proj_a", "w_proj_b", "w_out"], [g_pa, g_pb, g_out], ())
    dy_a = _matmul(dp_ab[0], wg_pa, tb=True, name="d_y_a", deps=(token,))
    dy_b = _matmul(dp_ab[1], wg_pb, tb=True, name="d_y_b")
    r_fin, token_a = reduce_chip_done(r_fin, (dy_b,))
    r_mid, token_b = reduce_pair_done(r_mid, (dy_b,))
    dproj, dbias, g_qg, g_kg = _attn_bwd(dproj, proj, q_gain, k_gain, bias, dy_b, h_b, col_b, deps=(token_a, token_b))
    r_mid, token = reduce_chip_done(r_mid, (dbias,))
    dproj, g_lb, g_gain = _hgrn_bwd(dproj, proj, o_pre, states, dy_a, hgrn_lb_logits, hgrn_out_gain, h_a, deps=(token,))
    g_in = _matmul(h1, dproj, ta=True, name="dw_in")
    r_in, token = reduce_start("in", ["w_in"], [g_in], ())
    g_rb = _relbias_reduce(dbias.transpose(1, 0, 2))[:, :N_REL]
    reduce_finish(r_mid, (token,))
    r_in, token = reduce_pair_done(r_in, (g_rb, upd["w_out"][0], upd["w_proj_a"][0], upd["w_proj_b"][0]))
    dh1 = _matmul(dproj, wg_in, tb=True, name="d_h1", deps=(token,))
    dx, _, g_norm_mix = _rmsnorm_bwd(dh1, x0, r1, norm_mix, dx2, "rmsnorm_mix_bwd")
    reduce_finish(r_fin, (dx,))
    reduce_finish(r_fout, (dx,))
    r_in, token = reduce_chip_done(r_in, (upd["w_ffn_in"][0], upd["w_ffn_out"][0]))

    small_w = [b_gate, norm_mix, norm_ffn, hgrn_lb_logits, hgrn_out_gain, q_gain, k_gain, rel_bias]
    small_m = [m_b_gate, m_norm_mix, m_norm_ffn, m_hgrn_lb_logits, m_hgrn_out_gain, m_q_gain, m_k_gain, m_rel_bias]
    small_v = [v_b_gate, v_norm_mix, v_norm_ffn, v_hgrn_lb_logits, v_hgrn_out_gain, v_q_gain, v_k_gain, v_rel_bias]
    small_g = [g_bgate, g_norm_mix, g_norm_ffn, g_lb, g_gain, g_qg, g_kg, g_rb]
    sizes = [w.size for w in small_w]
    length = -(-(sum(sizes) + 1) // 128) * 128

    def pack(parts_):
        flat = jnp.concatenate([p.reshape(1, -1) for p in parts_], axis=1)
        return jnp.pad(flat, ((0, 0), (0, length - flat.shape[1])))

    one = jnp.ones((1, 1), F32)
    packed = _small_allreduce_adamw(pack(small_g + [loss_part]), pack(small_w + [one]), pack(small_m + [one]),
                                    pack(small_v + [one]), deps=(token,))

    def unpack(vec):
        out, at = [], 0
        for w, n in zip(small_w, sizes):
            out.append(vec[0, at:at + n].reshape(w.shape))
            at += n
        return out, vec[0, at]

    (sg, loss), (sd, _), (sm, _), (sv, _) = [unpack(p) for p in packed]
    reduce_finish(r_in, (packed[0],))

    def ordered(small, bigs):
        bigs = [bigs[nm][None] for nm in names]
        return [bigs[0]] + small + bigs[1:]

    return (loss, dx.reshape(x.shape), *ordered(sg, g_big), *ordered(sd, {nm: upd[nm][0] for nm in names}),
            *ordered(sm, {nm: upd[nm][1] for nm in names}), *ordered(sv, {nm: upd[nm][2] for nm in names}))
```

```python
import functools

import jax
import jax.numpy as jnp
from jax import lax
from jax.experimental import pallas as pl
from jax.experimental.pallas import tpu as pltpu

F32 = jnp.float32
BF16 = jnp.bfloat16
SDS = jax.ShapeDtypeStruct
MESH = pl.DeviceIdType.MESH
HIGHEST = lax.Precision.HIGHEST

CHUNK = 64
SUB = 16
HEAD = 128
N_PAST = 8
BAND = (N_PAST + 1) * CHUNK
PAD = N_PAST * CHUNK
REL_FUTURE = CHUNK - 1
REL_PAST = 2 * CHUNK - 1
N_REL = REL_FUTURE + REL_PAST + 1
REL_LANES = 256
EPS = 1e-6
MIX_HEADS = 2
MIX_UNROLL = 4
MIX_UNROLL_BWD = 4
ATT_UNROLL = 8
ATT_UNROLL_BWD = 4
EXP_CLAMP = 80.0

ADAM_LR = 0.001
ADAM_B1 = 0.9
ADAM_B2 = 0.999
ADAM_EPS = 1e-08
ADAM_WD = 0.01
ADAM_STEP = 10

VMEM_LIMIT = 56 * 1024 * 1024

HBM = pl.BlockSpec(memory_space=pltpu.HBM)
ANY = pl.BlockSpec(memory_space=pl.ANY)
SEM = pl.BlockSpec(memory_space=pltpu.SEMAPHORE)

NT = (((1,), (1,)), ((), ()))
TN = (((0,), (0,)), ((), ()))
NN = (((1,), (0,)), ((), ()))


def _params(sem=None, **kw):
    return pltpu.CompilerParams(dimension_semantics=sem, vmem_limit_bytes=VMEM_LIMIT, **kw)


def _tile(n, pref, unit=128):
    if n <= pref:
        return n
    t = pref - pref % unit
    while n % t:
        t -= unit
    return t


def _loop(n, unroll, step, init):
    assert n % unroll == 0, (n, unroll)

    def several(i, carry):
        for u in range(unroll):
            carry = step(i * unroll + u, carry)
        return carry

    return lax.fori_loop(0, n // unroll, several, init)


def _sigmoid(x):
    return 1.0 / (1.0 + jnp.exp(-x))


def _dsilu(x, s):
    return s * (1.0 + x * (1.0 - s))


def _bdot(a, b, dims=NN):
    return lax.dot_general(a.astype(BF16), b.astype(BF16), dims, preferred_element_type=F32)


def _split(a):
    hi = a.astype(BF16)
    return hi, (a - hi.astype(F32)).astype(BF16)


def _dot3(a, b, dims):
    dot = lambda u, v: lax.dot_general(u, v, dims, preferred_element_type=F32)
    return dot(a[0], b[1]) + dot(a[1], b[0]) + dot(a[0], b[0])


def _fdot(a, b):
    return lax.dot_general(a, b, NN, precision=HIGHEST, preferred_element_type=F32)


MM_TILE_K = 5632
MM_TILE_N = 512


def _matmul_chunks(h, w, which, prev, pos, name, own_shard=False, deps=()):
    t, d = h.shape
    nc_ = w.shape[1] if own_shard else w.shape[1] // 4
    tm, tn = _tile(t, 1024), _tile(nc_, 1408)
    nn = nc_ // tn

    def chunk(q, p):
        sel = p[which[0]]
        for i in range(1, len(which)):
            sel = jnp.where(q == i, p[which[i]], sel)
        return sel

    def body(p_ref, h_ref, w_ref, *rest):
        rest[-1][...] = jnp.dot(h_ref[...], w_ref[...].astype(BF16), preferred_element_type=F32)

    if own_shard:
        w_spec = pl.BlockSpec((d, tn), lambda q, i, j, p: (0, j))
    else:
        w_spec = pl.BlockSpec((d, tn), lambda q, i, j, p: (0, chunk(q, p) * nn + j))
    n_extra = len(deps) + (prev is not None)
    return pl.pallas_call(
        body,
        grid_spec=pltpu.PrefetchScalarGridSpec(
            num_scalar_prefetch=1, grid=(len(which), t // tm, nn),
            in_specs=[pl.BlockSpec((tm, d), lambda q, i, j, p: (i, 0)), w_spec] + [ANY] * n_extra,
            out_specs=pl.BlockSpec((tm, tn), lambda q, i, j, p: (i, chunk(q, p) * nn + j))),
        out_shape=SDS((t, 4 * nc_), F32), input_output_aliases={3 + len(deps): 0} if prev is not None else {},
        compiler_params=_params(("arbitrary", "arbitrary", "arbitrary")),
        name=name)(pos, h, w, *deps, *(() if prev is None else (prev,)))


def _matmul(a, b, *, ta=False, tb=False, res=None, out_dtype=F32, name, deps=()):
    m, k = (a.shape[1], a.shape[0]) if ta else a.shape
    n = b.shape[0] if tb else b.shape[1]
    if k > MM_TILE_K:
        tk, tm, tn = _tile(k, MM_TILE_K // 2), _tile(m, 1024), _tile(n, 1024)
    else:
        tk = k
        tm, tn = _tile(m, 2048 if tk <= MM_TILE_K // 2 else 1024), _tile(n, MM_TILE_N)
    nk = k // tk
    dims = ((((0,) if ta else (1,)), ((1,) if tb else (0,))), ((), ()))

    def body(*refs):
        n_in = 2 + (res is not None)
        a_ref, b_ref = refs[:2]
        r_ref = refs[2] if res is not None else None
        o_ref = refs[n_in + len(deps)]
        part = lax.dot_general(a_ref[...].astype(BF16), b_ref[...].astype(BF16), dims, preferred_element_type=F32)

        def finish(out):
            if r_ref is not None:
                out = out + r_ref[...]
            o_ref[...] = out.astype(o_ref.dtype)

        if nk == 1:
            finish(part)
            return
        acc_ref = refs[-1]
        kk = pl.program_id(2)

        @pl.when(kk == 0)
        def _():
            acc_ref[...] = part

        @pl.when(jnp.logical_and(kk > 0, kk < nk - 1))
        def _():
            acc_ref[...] += part

        @pl.when(kk == nk - 1)
        def _():
            finish(acc_ref[...] + part)

    a_spec = pl.BlockSpec((tk, tm), lambda i, j, l: (l, i)) if ta else pl.BlockSpec((tm, tk), lambda i, j, l: (i, l))
    b_spec = pl.BlockSpec((tn, tk), lambda i, j, l: (j, l)) if tb else pl.BlockSpec((tk, tn), lambda i, j, l: (l, j))
    o_spec = pl.BlockSpec((tm, tn), lambda i, j, l: (i, j))
    in_specs = [a_spec, b_spec] + ([o_spec] if res is not None else []) + [ANY] * len(deps)
    args = (a, b) + ((res,) if res is not None else ()) + tuple(deps)
    return pl.pallas_call(
        body, grid=(m // tm, n // tn, nk), in_specs=in_specs, out_specs=o_spec,
        out_shape=SDS((m, n), out_dtype), scratch_shapes=[pltpu.VMEM((tm, tn), F32)] if nk > 1 else [],
        compiler_params=_params(("parallel", "parallel", "arbitrary")), name=name)(*args)


def _cast_into_full(w, kind, pos, name):
    r, n = w.shape
    tr = _tile(r, 512, 16)
    nr = r // tr
    if kind == "col":
        shape, o_spec = (r, 4 * n), pl.BlockSpec((tr, n), lambda i, p: (i, p[0]))
    else:
        shape, o_spec = (4 * r, n), pl.BlockSpec((tr, n), lambda i, p: (p[0] * nr + i, 0))

    def body(p_ref, w_ref, o_ref):
        o_ref[...] = w_ref[...].astype(BF16)

    return pl.pallas_call(
        body,
        grid_spec=pltpu.PrefetchScalarGridSpec(num_scalar_prefetch=1, grid=(nr,),
                                               in_specs=[pl.BlockSpec((tr, n), lambda i, p: (i, 0))], out_specs=o_spec),
        out_shape=SDS(shape, BF16), compiler_params=_params(("parallel",)), name=name)(pos, w)


def _rmsnorm_fwd(x, gain, name):
    t, d = x.shape
    tm = _tile(t, 256)

    def body(x_ref, g_ref, h_ref, r_ref):
        xv = x_ref[...]
        r = lax.rsqrt(jnp.mean(xv * xv, axis=-1, keepdims=True) + EPS)
        h_ref[...] = (xv * r * g_ref[...]).astype(BF16)
        r_ref[...] = r

    return pl.pallas_call(
        body, grid=(t // tm,),
        in_specs=[pl.BlockSpec((tm, d), lambda i: (i, 0)), pl.BlockSpec((1, d), lambda i: (0, 0))],
        out_specs=[pl.BlockSpec((tm, d), lambda i: (i, 0)), pl.BlockSpec((tm, 1), lambda i: (i, 0))],
        out_shape=[SDS((t, d), BF16), SDS((t, 1), F32)], compiler_params=_params(("parallel",)), name=name)(x, gain)


def _rmsnorm_bwd(dh, x, r, gain, dres, name, deps=()):
    t, d = x.shape
    tm = _tile(t, 256)

    def body(dh_ref, x_ref, r_ref, g_ref, dres_ref, *rest):
        dx_ref, dxb_ref, dg_ref = rest[len(deps):]

        @pl.when(pl.program_id(0) == 0)
        def _():
            dg_ref[...] = jnp.zeros_like(dg_ref)

        dhv, xv, rv = dh_ref[...], x_ref[...], r_ref[...]
        dg_ref[...] += jnp.sum(dhv * (xv * rv), axis=0, keepdims=True)
        u = dhv * g_ref[...]
        dx = dres_ref[...] + rv * u - xv * (rv * rv * rv) * jnp.mean(u * xv, axis=-1, keepdims=True)
        dx_ref[...] = dx
        dxb_ref[...] = dx.astype(BF16)

    row = pl.BlockSpec((tm, d), lambda i: (i, 0))
    vec = pl.BlockSpec((1, d), lambda i: (0, 0))
    return pl.pallas_call(
        body, grid=(t // tm,),
        in_specs=[row, row, pl.BlockSpec((tm, 1), lambda i: (i, 0)), vec, row] + [ANY] * len(deps),
        out_specs=[row, row, vec], out_shape=[SDS((t, d), F32), SDS((t, d), BF16), SDS((1, d), F32)],
        compiler_params=_params(("arbitrary",)), name=name)(dh, x, r, gain, dres, *deps)


def _merge_fwd(proj, b_gate, pa, pb, off):
    t, d = pa.shape
    tm, tc = _tile(t, 512), _tile(d, 512)
    nj = d // tc
    oa, ob = off // tc, off // tc + nj

    def body(la_ref, lb_ref, ba_ref, bb_ref, pa_ref, pb_ref, o_ref):
        ga = _sigmoid(la_ref[...] + ba_ref[...])
        gb = _sigmoid(lb_ref[...] + bb_ref[...])
        o_ref[...] = (ga * pa_ref[...] + gb * pb_ref[...]).astype(BF16)

    tile = pl.BlockSpec((tm, tc), lambda i, j: (i, j))
    return pl.pallas_call(
        body, grid=(t // tm, nj),
        in_specs=[pl.BlockSpec((tm, tc), lambda i, j: (i, oa + j)), pl.BlockSpec((tm, tc), lambda i, j: (i, ob + j)),
                  pl.BlockSpec((1, tc), lambda i, j: (0, j)), pl.BlockSpec((1, tc), lambda i, j: (0, nj + j)), tile, tile],
        out_specs=tile, out_shape=SDS((t, d), BF16), compiler_params=_params(("parallel", "parallel")),
        name="merge_fwd")(proj, proj, b_gate, b_gate, pa, pb)


def _merge_bwd(dmerged, proj, b_gate, pa, pb, off):
    t, d = pa.shape
    tm, tc = _tile(t, 512), _tile(d, 512)
    nj, ni = d // tc, t // tm
    o0 = off // tc

    def body(dm_ref, l_ref, b_ref, pa_ref, pb_ref, dp_ref, dl_ref, db_ref):
        s, i = pl.program_id(0), pl.program_id(2)
        p = jnp.where(s == 0, pa_ref[...], pb_ref[...])
        g = _sigmoid(l_ref[...] + b_ref[...])
        dm = dm_ref[...]
        dp_ref[0] = (dm * g).astype(BF16)
        dl = dm * p * g * (1.0 - g)
        dl_ref[...] = dl.astype(BF16)

        @pl.when(i == 0)
        def _():
            db_ref[...] = jnp.zeros_like(db_ref)

        db_ref[...] += jnp.sum(dl, axis=0, keepdims=True)

    tile = pl.BlockSpec((tm, tc), lambda s, j, i: (i, j))
    return pl.pallas_call(
        body, grid=(2, nj, ni),
        in_specs=[tile, pl.BlockSpec((tm, tc), lambda s, j, i: (i, o0 + s * nj + j)),
                  pl.BlockSpec((1, tc), lambda s, j, i: (0, s * nj + j)), tile, tile],
        out_specs=[pl.BlockSpec((1, tm, tc), lambda s, j, i: (s, i, j)),
                   pl.BlockSpec((tm, tc), lambda s, j, i: (i, o0 + s * nj + j)),
                   pl.BlockSpec((1, tc), lambda s, j, i: (0, s * nj + j))],
        out_shape=[SDS((2, t, d), BF16), SDS(proj.shape, BF16), SDS((1, 2 * d), F32)],
        compiler_params=_params(("arbitrary", "arbitrary", "arbitrary")),
        name="merge_bwd")(dmerged, proj, b_gate, pa, pb)


def _ffn_in_swiglu(h, w, deps=()):
    t, d = h.shape
    f = w.shape[1] // 2
    tm, tn = _tile(t, 2048), _tile(f, MM_TILE_N)
    nj = f // tn

    def body(h_ref, wg_ref, wu_ref, *rest):
        g_ref, u_ref, a_ref = rest[len(deps):]
        hv = h_ref[...]
        g = jnp.dot(hv, wg_ref[...], preferred_element_type=F32)
        u = jnp.dot(hv, wu_ref[...], preferred_element_type=F32)
        g_ref[...] = g
        u_ref[...] = u
        a_ref[...] = (g * _sigmoid(g) * u).astype(BF16)

    tile = pl.BlockSpec((tm, tn), lambda i, j: (i, j))
    return pl.pallas_call(
        body, grid=(t // tm, nj),
        in_specs=[pl.BlockSpec((tm, d), lambda i, j: (i, 0)), pl.BlockSpec((d, tn), lambda i, j: (0, j)),
                  pl.BlockSpec((d, tn), lambda i, j: (0, nj + j))] + [ANY] * len(deps),
        out_specs=[tile, tile, tile], out_shape=[SDS((t, f), F32), SDS((t, f), F32), SDS((t, f), BF16)],
        compiler_params=_params(("parallel", "parallel")), name="ffn_in_swiglu")(h, w, w, *deps)


def _swiglu_bwd(dact, gate, up):
    t, f = gate.shape
    tm = _tile(t, 128)

    def body(d_ref, g_ref, u_ref, o_ref):
        g, dv = g_ref[...], d_ref[...]
        sg = _sigmoid(g)
        o_ref[:, :f] = (dv * u_ref[...] * _dsilu(g, sg)).astype(BF16)
        o_ref[:, f:] = (dv * (g * sg)).astype(BF16)

    row = pl.BlockSpec((tm, f), lambda i: (i, 0))
    return pl.pallas_call(
        body, grid=(t // tm,), in_specs=[row, row, row],
        out_specs=pl.BlockSpec((tm, 2 * f), lambda i: (i, 0)), out_shape=SDS((t, 2 * f), BF16),
        compiler_params=_params(("parallel",)), name="swiglu_bwd")(dact, gate, up)


def _ffn_out_loss(act, w, x_res, target):
    t, d = x_res.shape
    k = act.shape[1]
    tm, tn = _tile(t, 1024), _tile(d, MM_TILE_N)

    def body(a_ref, w_ref, r_ref, t_ref, dy_ref, dyb_ref, l_ref):
        @pl.when(jnp.logical_and(pl.program_id(0) == 0, pl.program_id(1) == 0))
        def _():
            l_ref[...] = jnp.zeros_like(l_ref)

        y = jnp.dot(a_ref[...], w_ref[...], preferred_element_type=F32) + r_ref[...]
        e = y - t_ref[...]
        dy = e * (1.0 / d)
        dy_ref[...] = dy
        dyb_ref[...] = dy.astype(BF16)
        l_ref[...] += (0.5 / d) * jnp.sum(jnp.sum(e * e, axis=-1, keepdims=True), axis=0, keepdims=True)

    tile = pl.BlockSpec((tm, tn), lambda i, j: (i, j))
    return pl.pallas_call(
        body, grid=(t // tm, d // tn),
        in_specs=[pl.BlockSpec((tm, k), lambda i, j: (i, 0)), pl.BlockSpec((k, tn), lambda i, j: (0, j)), tile, tile],
        out_specs=[tile, tile, pl.BlockSpec((1, 1), lambda i, j: (0, 0))],
        out_shape=[SDS((t, d), F32), SDS((t, d), BF16), SDS((1, 1), F32)],
        compiler_params=_params(("arbitrary", "arbitrary")), name="ffn_out_loss")(act, w, x_res, target)


def _rel_onehot(qi):
    p = lax.broadcasted_iota(jnp.int32, (REL_LANES, BAND), 1)
    r = lax.broadcasted_iota(jnp.int32, (REL_LANES, BAND), 0)
    idx = jnp.clip(qi + PAD - p, -REL_FUTURE, REL_PAST) + REL_FUTURE
    return (idx == r).astype(F32)


def _relbias_expand(rb):
    h = rb.shape[0]

    def body(rb_ref, o_ref):
        def step(qi, _):
            o_ref[qi] = _fdot(rb_ref[...], _rel_onehot(qi))
            return 0

        lax.fori_loop(0, CHUNK, step, 0)

    return pl.pallas_call(body, out_shape=SDS((CHUNK, h, BAND), F32), compiler_params=_params(),
                          name="relbias_expand")(rb)


def _relbias_reduce(dbias):
    h = dbias.shape[1]

    rows_per_pass = 4

    def body(db_ref, o_ref):
        def step(i, acc):
            parts = []
            for u in range(rows_per_pass):
                qi = i * rows_per_pass + u
                xv = db_ref[qi]
                hi = xv.astype(BF16)
                rest = xv - hi.astype(F32)
                mid = rest.astype(BF16)
                low = (rest - mid.astype(F32)).astype(BF16)
                parts.append(lax.dot_general(jnp.concatenate([hi, mid, low], axis=0), _rel_onehot(qi).astype(BF16), NT,
                                             preferred_element_type=F32))
            for part in parts:
                acc = acc + (part[0:h] + part[h:2 * h] + part[2 * h:3 * h])
            return acc

        o_ref[...] = lax.fori_loop(0, CHUNK // rows_per_pass, step, jnp.zeros((h, REL_LANES), F32))

    return pl.pallas_call(body, out_shape=SDS((h, REL_LANES), F32), compiler_params=_params(),
                          name="relbias_reduce")(dbias)


def _lower_bound(l_ref):
    l0, l1 = l_ref[0:1, :], l_ref[1:2, :]
    m = jnp.maximum(l0, l1)
    e0, e1 = jnp.exp(l0 - m), jnp.exp(l1 - m)
    return e0 / (e0 + e1)


def _tri(lower):
    r = lax.broadcasted_iota(jnp.int32, (CHUNK, CHUNK), 0)
    c = lax.broadcasted_iota(jnp.int32, (CHUNK, CHUNK), 1)
    return r >= c if lower else r <= c


def _hgrn_intra(qs, kk, b_s):
    rows = lax.broadcasted_iota(jnp.int32, (CHUNK, HEAD), 0)
    b = b_s[...]
    out = []
    for i in range(CHUNK // SUB):
        lo = i * SUB
        ref = jnp.zeros((1, HEAD), F32) if i == 0 else b_s[lo - 1:lo, :]
        eq = jnp.exp(b[lo:lo + SUB] - ref)
        qt = _split(qs[lo:lo + SUB] * eq)
        e = jnp.where(rows < lo + SUB, jnp.exp(jnp.minimum(ref - b, EXP_CLAMP)), 0.0)
        kt = _split(kk * e)
        out.append((eq, qt, e, kt))
    return out


def _hgrn_scores(blocks):
    tr = lax.broadcasted_iota(jnp.int32, (SUB, CHUNK), 0)
    tc = lax.broadcasted_iota(jnp.int32, (SUB, CHUNK), 1)
    return jnp.concatenate([jnp.where(tc <= tr + i * SUB, _dot3(qt, kt, NT), 0.0)
                            for i, (_, qt, _, kt) in enumerate(blocks)], axis=0)


def _hgrn_fwd(proj, lb_logits, gain, n_heads):
    t = proj.shape[0]
    nc = t // CHUNK
    da = n_heads * HEAD
    hp = MIX_HEADS
    wide = hp * HEAD

    def body(q_ref, f_ref, i_ref, g_ref, l_ref, gain_ref, y_ref, o_ref, st_ref, state, b_s):
        state[...] = jnp.zeros_like(state)
        lb_all = _lower_bound(l_ref)
        tril = _tri(True).astype(F32)

        def chunks(i, _):
            dot = functools.partial(lax.dot_general, preferred_element_type=F32)
            items = []
            for u in range(MIX_UNROLL):
                for hh in range(hp):
                    j = i * MIX_UNROLL + u
                    sl = pl.ds(pl.multiple_of(j * CHUNK, CHUNK), CHUNK)
                    cols = slice(hh * HEAD, (hh + 1) * HEAD)
                    lb = lb_all[:, cols]
                    fg = lb + (1.0 - lb) * _sigmoid(f_ref[sl, cols])
                    qv = q_ref[sl, cols]
                    gv = g_ref[sl, cols]
                    items.append(dict(hh=hh, j=j, sl=sl, cols=cols, lf=jnp.log(fg), kk=1.0 - fg, qs=qv * _sigmoid(qv),
                                      vb=i_ref[sl, cols].astype(BF16), gate=gv * _sigmoid(gv)))
            for it in items:
                it["b"] = _fdot(tril, it["lf"])
            for slot, it in enumerate(items):
                b = it["b"]
                b_s[slot] = b
                it["blocks"] = _hgrn_intra(it["qs"], it["kk"], b_s.at[slot])
                it["ebl"] = jnp.exp(b_s[slot, CHUNK - 1:CHUNK, :])
                it["qe"] = (it["qs"] * jnp.exp(b)).astype(BF16)
                it["kd"] = (it["kk"] * jnp.exp(b_s[slot, CHUNK - 1:CHUNK, :] - b)).astype(BF16)
            for it in items:
                it["a"] = _hgrn_scores(it["blocks"]).astype(BF16)
            for it in items:
                it["kv"] = dot(it["vb"], it["kd"], TN)
                it["o"] = dot(it["a"], it["vb"], NN)
            s_now = [state[hh] for hh in range(hp)]
            for it in items:
                it["s_in"] = s_now[it["hh"]]
                s_now[it["hh"]] = it["s_in"] * it["ebl"] + it["kv"]
            for hh in range(hp):
                state[hh] = s_now[hh]
            for it in items:
                it["o"] = it["o"] + dot(it["qe"], it["s_in"].astype(BF16), NT)
            for it in items:
                o, sl, cols = it["o"], it["sl"], it["cols"]
                st_ref[it["hh"], it["j"]] = it["s_in"]
                o_ref[sl, cols] = o
                rr = lax.rsqrt(jnp.mean(o * o, axis=-1, keepdims=True) + EPS)
                y_ref[sl, cols] = (o * rr * gain_ref[:, cols] * it["gate"]).astype(BF16)
            return 0

        assert nc % MIX_UNROLL == 0, (nc, MIX_UNROLL)
        lax.fori_loop(0, nc // MIX_UNROLL, chunks, 0)

    col = lambda k: pl.BlockSpec((t, wide), lambda h: (0, k * (n_heads // hp) + h))
    vec = pl.BlockSpec((1, wide), lambda h: (0, h))
    return pl.pallas_call(
        body, grid=(n_heads // hp,),
        in_specs=[col(0), col(1), col(2), col(3), pl.BlockSpec((2, wide), lambda h: (0, h)), vec],
        out_specs=[pl.BlockSpec((t, wide), lambda h: (0, h)), pl.BlockSpec((t, wide), lambda h: (0, h)),
                   pl.BlockSpec((hp, nc, HEAD, HEAD), lambda h: (h, 0, 0, 0))],
        out_shape=[SDS((t, da), BF16), SDS((t, da), F32), SDS((n_heads, nc, HEAD, HEAD), F32)],
        scratch_shapes=[pltpu.VMEM((hp, HEAD, HEAD), F32), pltpu.VMEM((hp * MIX_UNROLL, CHUNK, HEAD), F32)],
        compiler_params=_params(("parallel",)), name="hgrn_fwd")(proj, proj, proj, proj, lb_logits, gain)


def _hgrn_bwd(dproj, proj, o_pre, states, dy, lb_logits, gain, n_heads, deps=()):
    t = proj.shape[0]
    nc = t // CHUNK
    da = n_heads * HEAD
    hp = MIX_HEADS
    wide = hp * HEAD

    def body(*refs):
        (q_ref, f_ref, i_ref, g_ref, o_ref, st_ref, dy_ref, l_ref, gain_ref,
         dproj_ref, dl_ref, dgain_ref, res, dstate, b_s) = refs[1 + len(deps):]

        @pl.when(pl.program_id(1) == 0)
        def _():
            dstate[...] = jnp.zeros_like(dstate)
            lb_all = _lower_bound(l_ref)
            tril_m, tril, triu = _tri(True), _tri(True).astype(F32), _tri(False).astype(F32)
            last = lax.broadcasted_iota(jnp.int32, (CHUNK, HEAD), 0) == CHUNK - 1

            def chunks(i, carry):
                dot = functools.partial(lax.dot_general, preferred_element_type=F32)
                items = []
                for u in range(MIX_UNROLL_BWD):
                    for hh in range(hp):
                        j = nc - 1 - (i * MIX_UNROLL_BWD + u)
                        sl = pl.ds(pl.multiple_of(j * CHUNK, CHUNK), CHUNK)
                        cols = slice(hh * HEAD, (hh + 1) * HEAD)
                        lb, gain_v = lb_all[:, cols], gain_ref[:, cols]
                        sg = _sigmoid(f_ref[sl, cols])
                        fg = lb + (1.0 - lb) * sg
                        qv = q_ref[sl, cols]
                        sq = _sigmoid(qv)
                        gv = g_ref[sl, cols]
                        sgg = _sigmoid(gv)
                        silg = gv * sgg
                        o = o_ref[sl, cols]
                        dyv = dy_ref[sl, cols]
                        rr = lax.rsqrt(jnp.mean(o * o, axis=-1, keepdims=True) + EPS)
                        on = o * rr
                        don = dyv * gain_v * silg
                        do = (rr * don - o * (rr * rr * rr) * jnp.mean(don * o, axis=-1, keepdims=True)).astype(BF16)
                        items.append(dict(
                            hh=hh, j=j, sl=sl, cols=cols, lb=lb, sg=sg, fg=fg, kk=1.0 - fg, qv=qv, sq=sq, qs=qv * sq,
                            vb=i_ref[sl, cols].astype(BF16), do=do, dg=dyv * on * gain_v * _dsilu(gv, sgg),
                            dgain=jnp.sum(dyv * on * silg, axis=0, keepdims=True)))
                for it in items:
                    it["b"] = _fdot(tril, jnp.log(it["fg"]))
                for slot, it in enumerate(items):
                    b = it["b"]
                    b_s[slot] = b
                    it["blocks"] = _hgrn_intra(it["qs"], it["kk"], b_s.at[slot])
                    bl = b_s[slot, CHUNK - 1:CHUNK, :]
                    it["eb"], it["ebl"], it["ekd"] = jnp.exp(b), jnp.exp(bl), jnp.exp(bl - b)
                    it["s_in"] = st_ref[it["hh"], it["j"]]
                for it in items:
                    it["a"] = _hgrn_scores(it["blocks"]).astype(BF16)
                    it["da"] = jnp.where(tril_m, dot(it["do"], it["vb"], NT), 0.0)
                for it in items:
                    dq_rows = []
                    dk = jnp.zeros((CHUNK, HEAD), F32)
                    for blk, (eq, qt, e, kt) in enumerate(it["blocks"]):
                        da_i = _split(it["da"][blk * SUB:(blk + 1) * SUB])
                        dq_rows.append(eq * _dot3(da_i, kt, NN))
                        dk = dk + e * _dot3(da_i, qt, TN)
                    it["dq"] = jnp.concatenate(dq_rows, axis=0) + dot(it["do"], it["s_in"].astype(BF16), NN) * it["eb"]
                    it["dk"] = dk
                    it["dv"] = dot(it["a"], it["do"], TN)
                    it["g"] = dot(it["do"], (it["qs"] * it["eb"]).astype(BF16), TN)
                ds_now = [dstate[hh] for hh in range(hp)]
                for it in items:
                    it["ds_out"] = ds_now[it["hh"]]
                    ds_now[it["hh"]] = it["ds_out"] * it["ebl"] + it["g"]
                for hh in range(hp):
                    dstate[hh] = ds_now[hh]
                for it in items:
                    dsb = it["ds_out"].astype(BF16)
                    it["dv"] = it["dv"] + dot((it["kk"] * it["ekd"]).astype(BF16), dsb, NT)
                    it["dk_state"] = it["ekd"] * dot(it["vb"], dsb, NN)
                for it in items:
                    kk, dk_state = it["kk"], it["dk_state"]
                    it["dk"] = it["dk"] + dk_state
                    extra = (jnp.sum(kk * dk_state, axis=0, keepdims=True)
                             + it["ebl"] * jnp.sum(it["s_in"] * it["ds_out"], axis=0, keepdims=True))
                    it["db"] = it["qs"] * it["dq"] - kk * it["dk"] + jnp.where(last, extra, 0.0)
                for it in items:
                    it["dlf"] = _fdot(triu, it["db"])
                carry = list(carry)
                for it in items:
                    hh, sl, cols, sg, lb = it["hh"], it["sl"], it["cols"], it["sg"], it["lb"]
                    dfg = it["dlf"] / it["fg"] - it["dk"]
                    dlb_acc, dgain_acc = carry[hh]
                    carry[hh] = (dlb_acc + jnp.sum(dfg * (1.0 - sg), axis=0, keepdims=True), dgain_acc + it["dgain"])
                    res[0, sl, cols] = (it["dq"] * _dsilu(it["qv"], it["sq"])).astype(BF16)
                    res[1, sl, cols] = (dfg * (1.0 - lb) * sg * (1.0 - sg)).astype(BF16)
                    res[2, sl, cols] = it["dv"].astype(BF16)
                    res[3, sl, cols] = it["dg"].astype(BF16)
                return tuple(carry)

            assert nc % MIX_UNROLL_BWD == 0, (nc, MIX_UNROLL_BWD)
            zero = jnp.zeros((1, HEAD), F32)
            sums = lax.fori_loop(0, nc // MIX_UNROLL_BWD, chunks, ((zero, zero),) * hp)
            for hh, (dlb, dgain) in enumerate(sums):
                cols = slice(hh * HEAD, (hh + 1) * HEAD)
                lb = lb_all[:, cols]
                dgain_ref[:, cols] = dgain
                dl0 = dlb * lb * (1.0 - lb)
                dl_ref[0:1, cols] = dl0
                dl_ref[1:2, cols] = -dl0

        dproj_ref[...] = res[pl.program_id(1)]

    ng = n_heads // hp
    col = lambda k: pl.BlockSpec((t, wide), lambda h, p: (0, k * ng + h))
    head = pl.BlockSpec((t, wide), lambda h, p: (0, h))
    vec = pl.BlockSpec((1, wide), lambda h, p: (0, h))
    return pl.pallas_call(
        body, grid=(ng, 4),
        in_specs=[ANY] * (1 + len(deps)) + [col(0), col(1), col(2), col(3), head,
                  pl.BlockSpec((hp, nc, HEAD, HEAD), lambda h, p: (h, 0, 0, 0)),
                  head, pl.BlockSpec((2, wide), lambda h, p: (0, h)), vec],
        out_specs=[pl.BlockSpec((t, wide), lambda h, p: (0, p * ng + h)),
                   pl.BlockSpec((2, wide), lambda h, p: (0, h)), vec],
        out_shape=[SDS(dproj.shape, BF16), SDS((2, da), F32), SDS((1, da), F32)],
        scratch_shapes=[pltpu.VMEM((4, t, wide), BF16), pltpu.VMEM((hp, HEAD, HEAD), F32),
                        pltpu.VMEM((hp * MIX_UNROLL_BWD, CHUNK, HEAD), F32)],
        input_output_aliases={0: 0}, compiler_params=_params(("arbitrary", "arbitrary")),
        name="hgrn_bwd")(dproj, *deps, proj, proj, proj, proj, o_pre, states, dy, lb_logits, gain)


ROWS = 256


def _head_norm(x_ref, gain, dst, dst_off, t):
    def step(i, _):
        sl = pl.ds(pl.multiple_of(i * ROWS, ROWS), ROWS)
        xv = x_ref[sl, :]
        r = lax.rsqrt(jnp.mean(xv * xv, axis=-1, keepdims=True) + EPS)
        dst[pl.ds(pl.multiple_of(dst_off + i * ROWS, ROWS), ROWS), :] = (xv * r * gain).astype(BF16)
        return 0

    lax.fori_loop(0, t // ROWS, step, 0)


def _head_norm_bwd(x_ref, gain, dn_ref, dn_off, out, slot, t):
    def step(i, acc):
        sl = pl.ds(pl.multiple_of(i * ROWS, ROWS), ROWS)
        xv = x_ref[sl, :]
        dn = dn_ref[pl.ds(pl.multiple_of(dn_off + i * ROWS, ROWS), ROWS), :]
        r = lax.rsqrt(jnp.mean(xv * xv, axis=-1, keepdims=True) + EPS)
        u = dn * gain
        out[slot, sl, :] = r * u - xv * (r * r * r) * jnp.mean(u * xv, axis=-1, keepdims=True)
        return acc + jnp.sum(dn * (xv * r), axis=0, keepdims=True)

    return lax.fori_loop(0, t // ROWS, step, jnp.zeros((1, HEAD), F32))


def _attn_scores(qn, kpad, n):
    qc = qn[pl.ds(pl.multiple_of(n * CHUNK, CHUNK), CHUNK), :]
    band = pl.ds(pl.multiple_of(n * CHUNK, CHUNK), BAND)
    return qc, band, lax.dot_general(qc, kpad[band, :], NT, preferred_element_type=F32)


def _attn_softmax(raw, bias_ref, n):
    s = raw * (HEAD ** -0.5) + bias_ref[0]
    col = lax.broadcasted_iota(jnp.int32, (CHUNK, BAND), 1)
    s = jnp.where(col >= PAD - n * CHUNK, s, -jnp.inf)
    p = jnp.exp(s - jnp.max(s, axis=-1, keepdims=True))
    return p / jnp.sum(p, axis=-1, keepdims=True)


def _attn_fwd(proj, q_gain, k_gain, bias, n_heads, col0):
    t = proj.shape[0]
    nc = t // CHUNK

    def body(q_ref, k_ref, v_ref, qg_ref, kg_ref, bias_ref, y_ref, qn, kpad, vpad):
        kpad[0:PAD, :] = jnp.zeros((PAD, HEAD), BF16)
        vpad[0:PAD, :] = jnp.zeros((PAD, HEAD), BF16)
        _head_norm(q_ref, qg_ref[...], qn, 0, t)
        _head_norm(k_ref, kg_ref[...], kpad, PAD, t)

        def copy_v(i, _):
            vpad[pl.ds(pl.multiple_of(PAD + i * ROWS, ROWS), ROWS), :] = v_ref[
                pl.ds(pl.multiple_of(i * ROWS, ROWS), ROWS), :].astype(BF16)
            return 0

        lax.fori_loop(0, t // ROWS, copy_v, 0)

        def chunks(i, _):
            ns = [i * ATT_UNROLL + u for u in range(ATT_UNROLL)]
            scored = [_attn_scores(qn, kpad, n) for n in ns]
            probs = [_attn_softmax(raw, bias_ref, n).astype(BF16) for n, (_, _, raw) in zip(ns, scored)]
            outs = [lax.dot_general(p, vpad[band, :], NN, preferred_element_type=F32).astype(BF16)
                    for p, (_, band, _) in zip(probs, scored)]
            for n, o in zip(ns, outs):
                y_ref[pl.ds(pl.multiple_of(n * CHUNK, CHUNK), CHUNK), :] = o
            return 0

        assert nc % ATT_UNROLL == 0, (nc, ATT_UNROLL)
        lax.fori_loop(0, nc // ATT_UNROLL, chunks, 0)

    col = lambda k: pl.BlockSpec((t, HEAD), lambda h: (0, col0 + k * n_heads + h))
    vec = pl.BlockSpec((1, HEAD), lambda h: (0, 0))
    return pl.pallas_call(
        body, grid=(n_heads,),
        in_specs=[col(0), col(1), col(2), vec, vec, pl.BlockSpec((1, CHUNK, BAND), lambda h: (h, 0, 0))],
        out_specs=pl.BlockSpec((t, HEAD), lambda h: (0, h)), out_shape=SDS((t, n_heads * HEAD), BF16),
        scratch_shapes=[pltpu.VMEM((t, HEAD), BF16), pltpu.VMEM((t + PAD, HEAD), BF16), pltpu.VMEM((t + PAD, HEAD), BF16)],
        compiler_params=_params(("parallel",)), name="attn_fwd")(proj, proj, proj, q_gain, k_gain, bias)


def _attn_bwd(dproj, proj, q_gain, k_gain, bias, dy, n_heads, col0, deps=()):
    t = proj.shape[0]
    nc = t // CHUNK

    def body(*refs):
        (q_ref, k_ref, v_ref, qg_ref, kg_ref, bias_ref, dy_ref,
         dproj_ref, dbias_ref, dqg_ref, dkg_ref, qn, kpad, vpad, dqn, dk_acc, dv_acc, res) = refs[1 + len(deps):]
        h, part = pl.program_id(0), pl.program_id(1)

        @pl.when(part == 0)
        def _():
            kpad[0:PAD, :] = jnp.zeros((PAD, HEAD), BF16)
            vpad[0:PAD, :] = jnp.zeros((PAD, HEAD), BF16)
            _head_norm(q_ref, qg_ref[...], qn, 0, t)
            _head_norm(k_ref, kg_ref[...], kpad, PAD, t)

            def prep(i, _):
                sl = pl.ds(pl.multiple_of(PAD + i * ROWS, ROWS), ROWS)
                vpad[sl, :] = v_ref[pl.ds(pl.multiple_of(i * ROWS, ROWS), ROWS), :].astype(BF16)
                return 0

            lax.fori_loop(0, t // ROWS, prep, 0)

            def clear(i, _):
                sl = pl.ds(pl.multiple_of(i * ROWS, ROWS), ROWS)
                dk_acc[sl, :] = jnp.zeros((ROWS, HEAD), F32)
                dv_acc[sl, :] = jnp.zeros((ROWS, HEAD), F32)
                return 0

            lax.fori_loop(0, (t + PAD) // ROWS, clear, 0)
            dbias_ref[0] = jnp.zeros((CHUNK, BAND), F32)

            def chunks(i, _):
                dot = functools.partial(lax.dot_general, preferred_element_type=F32)
                ns = [i * ATT_UNROLL_BWD + u for u in range(ATT_UNROLL_BWD)]
                scored = [_attn_scores(qn, kpad, n) for n in ns]
                dos = [dy_ref[pl.ds(pl.multiple_of(n * CHUNK, CHUNK), CHUNK), :].astype(BF16) for n in ns]
                dps = [dot(do, vpad[band, :], NT) for do, (_, band, _) in zip(dos, scored)]
                ps, dss = [], []
                for n, (_, _, raw), dp in zip(ns, scored, dps):
                    p = _attn_softmax(raw, bias_ref, n)
                    ds = p * (dp - jnp.sum(dp * p, axis=-1, keepdims=True))
                    dbias_ref[0] += ds
                    ps.append(p.astype(BF16))
                    dss.append((ds * (HEAD ** -0.5)).astype(BF16))
                dqs = [dot(d, kpad[band, :], NN) for d, (_, band, _) in zip(dss, scored)]
                dks = [dot(d, qc, TN) for d, (qc, _, _) in zip(dss, scored)]
                dvs = [dot(p, do, TN) for p, do in zip(ps, dos)]
                for n, (_, band, _), dq, dk, dv in zip(ns, scored, dqs, dks, dvs):
                    dqn[pl.ds(pl.multiple_of(n * CHUNK, CHUNK), CHUNK), :] = dq
                    dk_acc[band, :] += dk
                    dv_acc[band, :] += dv
                return 0

            assert nc % ATT_UNROLL_BWD == 0, (nc, ATT_UNROLL_BWD)
            lax.fori_loop(0, nc // ATT_UNROLL_BWD, chunks, 0)
            dqg = _head_norm_bwd(q_ref, qg_ref[...], dqn, 0, res, 0, t)
            dkg = _head_norm_bwd(k_ref, kg_ref[...], dk_acc, PAD, res, 1, t)

            def put_v(i, _):
                sl = pl.ds(pl.multiple_of(i * ROWS, ROWS), ROWS)
                res[2, sl, :] = dv_acc[pl.ds(pl.multiple_of(PAD + i * ROWS, ROWS), ROWS), :]
                return 0

            lax.fori_loop(0, t // ROWS, put_v, 0)

            @pl.when(h == 0)
            def _():
                dqg_ref[...] = jnp.zeros_like(dqg_ref)
                dkg_ref[...] = jnp.zeros_like(dkg_ref)

            dqg_ref[...] += dqg
            dkg_ref[...] += dkg

        dproj_ref[...] = res[part].astype(BF16)

    col = lambda k: pl.BlockSpec((t, HEAD), lambda h, p: (0, col0 + k * n_heads + h))
    vec = pl.BlockSpec((1, HEAD), lambda h, p: (0, 0))
    btile = pl.BlockSpec((1, CHUNK, BAND), lambda h, p: (h, 0, 0))
    return pl.pallas_call(
        body, grid=(n_heads, 3),
        in_specs=[ANY] * (1 + len(deps)) + [col(0), col(1), col(2), vec, vec, btile,
                                            pl.BlockSpec((t, HEAD), lambda h, p: (0, h))],
        out_specs=[pl.BlockSpec((t, HEAD), lambda h, p: (0, col0 + p * n_heads + h)), btile, vec, vec],
        out_shape=[SDS(dproj.shape, BF16), SDS((n_heads, CHUNK, BAND), F32), SDS((1, HEAD), F32), SDS((1, HEAD), F32)],
        scratch_shapes=[pltpu.VMEM((t, HEAD), BF16), pltpu.VMEM((t + PAD, HEAD), BF16), pltpu.VMEM((t + PAD, HEAD), BF16),
                        pltpu.VMEM((t, HEAD), F32), pltpu.VMEM((t + PAD, HEAD), F32), pltpu.VMEM((t + PAD, HEAD), F32),
                        pltpu.VMEM((3, t, HEAD), F32)],
        input_output_aliases={0: 0}, compiler_params=_params(("arbitrary", "arbitrary")),
        name="attn_bwd")(dproj, *deps, proj, proj, proj, q_gain, k_gain, bias, dy)


def _place():
    x, y, c = lax.axis_index("x"), lax.axis_index("y"), lax.axis_index("c")
    others = [(1 - x, y), (x, 1 - y), (1 - x, 1 - y)]
    return x, y, c, others


def _chunk_of(ref, kind, chip, half, shard_shape):
    r, n = shard_shape
    hr = r // 2
    if kind == "col":
        rows = pl.ds(0, r) if half is None else pl.ds(half * hr, hr)
        return ref.at[rows, pl.ds(chip * n, n)]
    rows = pl.ds(chip * r, r) if half is None else pl.ds(chip * r + half * hr, hr)
    return ref.at[rows, :]


EFFECT = pltpu.SideEffectType.DATAFLOW_SIDE_EFFECTING


def _start_copies(name, bufs, plan, n, deps):
    nb, nd = len(bufs), len(deps)

    def body(*refs):
        send, recv, token = refs[nb + nd], refs[nb + nd + 1], refs[-1]
        for cp in plan(refs[:nb], send, recv)[0]:
            cp.start()
        token[...] = jnp.zeros_like(token)

    out = pl.pallas_call(
        body, name=name,
        out_shape=(pltpu.SemaphoreType.DMA((n,)), pltpu.SemaphoreType.DMA((n,)),
                   *[pltpu.HBM(b.shape, b.dtype) for b in bufs], SDS((8, 128), F32)),
        in_specs=[HBM] * nb + [ANY] * nd,
        out_specs=(SEM, SEM, *[HBM] * nb, pl.BlockSpec(memory_space=pltpu.VMEM)),
        input_output_aliases={i: 2 + i for i in range(nb)},
        compiler_params=pltpu.CompilerParams(has_side_effects=EFFECT),
    )(*[pltpu.with_memory_space_constraint(b, pltpu.HBM) for b in bufs], *deps)
    return out[0], out[1], list(out[2:2 + nb]), out[-1]


def _wait_copies(name, bufs, send, recv, plan, after):
    nb = len(bufs)

    def body(*refs):
        sends, recvs = plan(refs[:nb], refs[nb], refs[nb + 1])
        for cp in sends:
            cp.wait_send()
        for cp in recvs:
            cp.wait_recv()

    out = pl.pallas_call(
        body, name=name, out_shape=tuple(pltpu.HBM(b.shape, b.dtype) for b in bufs),
        in_specs=[HBM] * nb + [SEM, SEM] + [ANY] * len(after), out_specs=tuple([HBM] * nb),
        input_output_aliases={i: i for i in range(nb)},
        compiler_params=pltpu.CompilerParams(has_side_effects=EFFECT),
    )(*bufs, send, recv, *after)
    return list(out)


def _remote(src, dst, send, recv, i, dev):
    return pltpu.make_async_remote_copy(src_ref=src, dst_ref=dst, send_sem=send.at[i], recv_sem=recv.at[i],
                                        device_id=dev, device_id_type=MESH)


ALL_RELATIONS = (0, 1, 2)


def _plan_gather_ici(kinds, shapes, rels=ALL_RELATIONS):
    def plan(refs, send, recv):
        x, y, c, others = _place()
        sends, recvs = [], []
        for w, (kind, ss) in enumerate(zip(kinds, shapes)):
            for p in rels:
                px, py = others[p]
                mine = _chunk_of(refs[w], kind, 2 * x + y, c, ss)
                theirs = _chunk_of(refs[w], kind, 2 * px + py, c, ss)
                sends.append(_remote(mine, mine, send, recv, 3 * w + p, (px, py, c)))
                recvs.append(_remote(theirs, theirs, send, recv, 3 * w + p, (px, py, c)))
        return sends, recvs

    return plan, 3 * len(kinds)


def _plan_gather_pass(kinds, shapes, rels=ALL_RELATIONS):
    def plan(refs, send, recv):
        x, y, c, others = _place()
        sends, recvs = [], []
        for w, (kind, ss) in enumerate(zip(kinds, shapes)):
            for i, p in enumerate(rels):
                px, py = others[p]
                got = _chunk_of(refs[w], kind, 2 * px + py, c, ss)
                coming = _chunk_of(refs[w], kind, 2 * px + py, 1 - c, ss)
                sends.append(_remote(got, got, send, recv, len(rels) * w + i, (x, y, 1 - c)))
                recvs.append(_remote(coming, coming, send, recv, len(rels) * w + i, (x, y, 1 - c)))
        return sends, recvs

    return plan, len(rels) * len(kinds)


def _plan_pair(kinds, shapes):
    nw = len(kinds)

    def plan(refs, send, recv):
        x, y, c, _ = _place()
        sends = []
        for w, (kind, ss) in enumerate(zip(kinds, shapes)):
            for k in range(4):
                sends.append(_remote(_chunk_of(refs[w], kind, k, 1 - c, ss), refs[nw + w].at[k], send, recv,
                                     4 * w + k, (x, y, 1 - c)))
        return sends, sends

    return plan, 4 * nw


def _plan_chip(nw):
    def plan(refs, send, recv):
        x, y, c, others = _place()
        sends = []
        for w in range(nw):
            for p, (px, py) in enumerate(others):
                sends.append(_remote(refs[w].at[p], refs[nw + w].at[p], send, recv, 3 * w + p, (px, py, c)))
        return sends, sends

    return plan, 3 * nw


def _plan_share(nw):
    def plan(refs, send, recv):
        x, y, c, _ = _place()
        sends = [_remote(refs[w].at[c], refs[w].at[c], send, recv, w, (x, y, 1 - c)) for w in range(nw)]
        recvs = [_remote(refs[w].at[1 - c], refs[w].at[1 - c], send, recv, w, (x, y, 1 - c)) for w in range(nw)]
        return sends, recvs

    return plan, nw


def _grad_half_spec(kind, tr, tn, nr, nn, chunk):
    if kind == "col":
        return pl.BlockSpec((tr, tn), lambda *a: (a[-1][1] * nr + a[-3], chunk(*a) * nn + a[-2]))
    return pl.BlockSpec((tr, tn), lambda *a: ((2 * chunk(*a) + a[-1][1]) * nr + a[-3], a[-2]))


def _pair_add(grad, got, kind, shard_shape, pos, name):
    r, n = shard_shape
    hr = r // 2
    tr, tn = _tile(hr, 256, 16), _tile(n, 1408)
    nr, nn = hr // tr, n // tn
    g_spec = _grad_half_spec(kind, tr, tn, nr, nn, lambda p, i, j, pos_: pos_[2 + p])
    r_spec = pl.BlockSpec((1, tr, tn), lambda p, i, j, pos_: (pos_[2 + p], i, j))
    o_spec = pl.BlockSpec((1, tr, tn), lambda p, i, j, pos_: (p, i, j))

    def body(pos_ref, g_ref, r_ref, o_ref):
        o_ref[0] = (g_ref[...] + r_ref[0]).astype(BF16)

    return pl.pallas_call(
        body,
        grid_spec=pltpu.PrefetchScalarGridSpec(num_scalar_prefetch=1, grid=(3, nr, nn), in_specs=[g_spec, r_spec],
                                               out_specs=o_spec),
        out_shape=SDS((3, hr, n), BF16),
        compiler_params=_params(("parallel", "parallel", "parallel")), name=name)(pos, grad, got)


def _chip_add(grad, got, got16, kind, shard_shape, pos, name):
    r, n = shard_shape
    hr = r // 2
    tr, tn = _tile(hr, 256, 16), _tile(n, 1408)
    nr, nn = hr // tr, n // tn
    g_spec = _grad_half_spec(kind, tr, tn, nr, nn, lambda i, j, pos_: pos_[0])
    r_spec = pl.BlockSpec((1, tr, tn), lambda i, j, pos_: (pos_[0], i, j))
    oth = pl.BlockSpec((3, tr, tn), lambda i, j, pos_: (0, i, j))

    def body(pos_ref, g_ref, r_ref, oth_ref, o_ref):
        own = g_ref[...] + r_ref[0]
        o_ref[0] = ((own + oth_ref[0].astype(F32)) + oth_ref[1].astype(F32)) + oth_ref[2].astype(F32)

    return pl.pallas_call(
        body,
        grid_spec=pltpu.PrefetchScalarGridSpec(num_scalar_prefetch=1, grid=(nr, nn), in_specs=[g_spec, r_spec, oth],
                                               out_specs=pl.BlockSpec((1, tr, tn), lambda i, j, pos_: (pos_[1], i, j))),
        out_shape=SDS((2, hr, n), F32), compiler_params=_params(("parallel", "parallel")),
        name=name)(pos, grad, got, got16)


def _adamw_math(w, g, m, v):
    m = ADAM_B1 * m + (1.0 - ADAM_B1) * g
    v = ADAM_B2 * v + (1.0 - ADAM_B2) * (g * g)
    m_hat = m / (1.0 - ADAM_B1 ** ADAM_STEP)
    v_hat = v / (1.0 - ADAM_B2 ** ADAM_STEP)
    return -ADAM_LR * (m_hat / (jnp.sqrt(v_hat) + ADAM_EPS) + ADAM_WD * w), m, v


def _adamw(w, g, m, v, name):
    r, n = w.shape
    tr, tn = _tile(r, 256, 16), _tile(n, 1408)

    def body(w_ref, g_ref, m_ref, v_ref, d_ref, nm_ref, nv_ref, go_ref):
        gv = g_ref[...]
        d_ref[...], nm_ref[...], nv_ref[...] = _adamw_math(w_ref[...], gv, m_ref[...], v_ref[...])
        go_ref[...] = gv

    tile = pl.BlockSpec((tr, tn), lambda i, j: (i, j))
    return pl.pallas_call(
        body, grid=(r // tr, n // tn), in_specs=[tile] * 4, out_specs=[tile] * 4, out_shape=[SDS((r, n), F32)] * 4,
        compiler_params=_params(("parallel", "parallel")), name=name)(w, g, m, v)


def _small_allreduce_adamw(g, w, m, v, deps=()):
    length = g.shape[1]

    def body(*refs):
        g_ref, w_ref, m_ref, v_ref = refs[:4]
        gs_ref, d_ref, nm_ref, nv_ref, buf, send, recv = refs[4 + len(deps):]
        x, y, c = lax.axis_index("x"), lax.axis_index("y"), lax.axis_index("c")
        me = 4 * x + 2 * y + c
        buf[me] = g_ref[...]
        cps = []
        for d in range(1, 8):
            peer = (x ^ (d >> 2), y ^ ((d >> 1) & 1), c ^ (d & 1))
            cp = pltpu.make_async_remote_copy(src_ref=buf.at[me], dst_ref=buf.at[me], send_sem=send.at[d - 1],
                                              recv_sem=recv.at[d - 1], device_id=peer, device_id_type=MESH)
            cp.start()
            cps.append(cp)
        for cp in cps:
            cp.wait()
        total = buf[0]
        for d in range(1, 8):
            total = total + buf[d]
        gs_ref[...] = total
        d_ref[...], nm_ref[...], nv_ref[...] = _adamw_math(w_ref[...], total, m_ref[...], v_ref[...])

    vm = pl.BlockSpec(memory_space=pltpu.VMEM)
    return pl.pallas_call(
        body, in_specs=[vm] * 4 + [ANY] * len(deps), out_specs=[vm] * 4, out_shape=[SDS((1, length), F32)] * 4,
        scratch_shapes=[pltpu.VMEM((8, 1, length), F32), pltpu.SemaphoreType.DMA((7,)), pltpu.SemaphoreType.DMA((7,))],
        compiler_params=pltpu.CompilerParams(has_side_effects=True), name="small_allreduce_adamw")(g, w, m, v, *deps)


def kernel(x, w_in, b_gate, norm_mix, norm_ffn, hgrn_lb_logits, hgrn_out_gain, q_gain, k_gain, rel_bias, w_proj_a, w_proj_b, w_out, w_ffn_in, w_ffn_out, loss_target, m_w_in, m_b_gate, m_norm_mix, m_norm_ffn, m_hgrn_lb_logits, m_hgrn_out_gain, m_q_gain, m_k_gain, m_rel_bias, m_w_proj_a, m_w_proj_b, m_w_out, m_w_ffn_in, m_w_ffn_out, v_w_in, v_b_gate, v_norm_mix, v_norm_ffn, v_hgrn_lb_logits, v_hgrn_out_gain, v_q_gain, v_k_gain, v_rel_bias, v_w_proj_a, v_w_proj_b, v_w_out, v_w_ffn_in, v_w_ffn_out):
    t, d = x.shape[1], x.shape[2]
    d_a = hgrn_out_gain.shape[1]
    h_a = d_a // HEAD
    h_b = rel_bias.shape[1]
    d_b = h_b * HEAD
    x0 = x.reshape(t, d)
    target = loss_target.reshape(t, d)
    ax, ay = lax.axis_index("x"), lax.axis_index("y")
    pos = jnp.stack([2 * ax + ay, lax.axis_index("c"), 2 * (1 - ax) + ay, 2 * ax + 1 - ay,
                     2 * (1 - ax) + 1 - ay]).astype(jnp.int32)

    names = ["w_in", "w_proj_a", "w_proj_b", "w_out", "w_ffn_in", "w_ffn_out"]
    big = dict(zip(names, [w_in[0], w_proj_a[0], w_proj_b[0], w_out[0], w_ffn_in[0], w_ffn_out[0]]))
    big_m = dict(zip(names, [m_w_in[0], m_w_proj_a[0], m_w_proj_b[0], m_w_out[0], m_w_ffn_in[0], m_w_ffn_out[0]]))
    big_v = dict(zip(names, [v_w_in[0], v_w_proj_a[0], v_w_proj_b[0], v_w_out[0], v_w_ffn_in[0], v_w_ffn_out[0]]))
    kind = dict(zip(names, ["col", "col", "col", "row", "col", "row"]))
    shape = {nm: big[nm].shape for nm in names}

    def gather_start(tag, group, deps):
        plan, n = _plan_gather_ici([kind[g] for g in group], [shape[g] for g in group])
        fulls = [_cast_into_full(big[g], kind[g], pos, "cast_" + g) for g in group]
        send, recv, bufs, token = _start_copies("gather_ici_start_" + tag, fulls, plan, n, deps)
        return (tag, group, plan, send, recv, bufs), token

    def gather_pass(state, after, rels=ALL_RELATIONS, part=""):
        tag, group, _, send, recv, bufs = state
        kinds_, shapes_ = [kind[g] for g in group], [shape[g] for g in group]
        bufs = _wait_copies("gather_ici_wait_" + tag + part, bufs, send, recv,
                            _plan_gather_ici(kinds_, shapes_, rels)[0], after)
        plan, n = _plan_gather_pass(kinds_, shapes_, rels)
        send2, recv2, bufs, token = _start_copies("gather_pass_start_" + tag + part, bufs, plan, n, ())
        return (tag + part, group, plan, send2, recv2, bufs), token

    def gather_done(state, after):
        tag, group, plan, send, recv, bufs = state
        return _wait_copies("gather_pass_wait_" + tag, bufs, send, recv, plan, after)

    def reduce_start(tag, group, grads, deps):
        plan, n = _plan_pair([kind[g] for g in group], [shape[g] for g in group])
        lands = [lax.empty((4, shape[g][0] // 2, shape[g][1]), F32) for g in group]
        send, recv, bufs, token = _start_copies("pair_start_" + tag, list(grads) + lands, plan, n, deps)
        return dict(tag=tag, group=group, plan=plan, send=send, recv=recv, bufs=bufs), token

    def reduce_pair_done(st, after):
        tag, group, nw = st["tag"], st["group"], len(st["group"])
        bufs = _wait_copies("pair_wait_" + tag, st["bufs"], st["send"], st["recv"], st["plan"], after)
        grads, gots = bufs[:nw], bufs[nw:]
        parts = [_pair_add(g, l, kind[nm], shape[nm], pos, "pair_add_" + nm) for g, l, nm in zip(grads, gots, group)]
        lands = [lax.empty((3, shape[g][0] // 2, shape[g][1]), BF16) for g in group]
        plan, n = _plan_chip(nw)
        send, recv, bufs, token = _start_copies("chip_start_" + tag, parts + lands, plan, n, ())
        return dict(st, plan=plan, send=send, recv=recv, bufs=bufs, grads=grads, gots=gots), token

    def reduce_chip_done(st, after):
        tag, group, nw = st["tag"], st["group"], len(st["group"])
        bufs = _wait_copies("chip_wait_" + tag, st["bufs"], st["send"], st["recv"], st["plan"], after)
        finals = [_chip_add(g, l, got16, kind[nm], shape[nm], pos, "chip_add_" + nm)
                  for g, l, got16, nm in zip(st["grads"], st["gots"], bufs[nw:], group)]
        plan, n = _plan_share(nw)
        send, recv, bufs, token = _start_copies("share_start_" + tag, finals, plan, n, ())
        return dict(st, plan=plan, send=send, recv=recv, bufs=bufs), token

    g_big, upd = {}, {}

    def reduce_finish(st, after):
        bufs = _wait_copies("share_wait_" + st["tag"], st["bufs"], st["send"], st["recv"], st["plan"], after)
        for full, nm in zip(bufs, st["group"]):
            upd[nm] = _adamw(big[nm], full.reshape(shape[nm]), big_m[nm], big_v[nm], "adamw_" + nm)
            g_big[nm] = upd[nm][3]

    ga, token = gather_start("a", ["w_in"], ())
    gb, token = gather_start("b", ["w_proj_a", "w_proj_b", "w_out"], (token,))
    gc, token = gather_start("c", ["w_ffn_in"], (token,))
    gd, token = gather_start("d", ["w_ffn_out"], (token,))
    h1, r1 = _rmsnorm_fwd(x0, norm_mix, "rmsnorm_mix")
    rb = jnp.pad(rel_bias[0], ((0, 0), (0, REL_LANES - N_REL)))
    bias = _relbias_expand(rb).transpose(1, 0, 2)
    proj = _matmul_chunks(h1, big["w_in"], (0,), None, pos, "proj_in_own", own_shard=True)
    ici_a = ga
    ga, token = gather_pass(ici_a, (h1, bias, proj, token), rels=(0, 1), part="_near")
    (wg_in,) = gather_done(ga, ())
    proj = _matmul_chunks(h1, wg_in, (2, 3), proj, pos, "proj_in_near")
    ga, token = gather_pass(ici_a[:5] + ([wg_in],), (proj,), rels=(2,), part="_far")
    (wg_in,) = gather_done(ga, ())
    proj = _matmul_chunks(h1, wg_in, (4,), proj, pos, "proj_in_far")
    y_a, o_pre, states = _hgrn_fwd(proj, hgrn_lb_logits, hgrn_out_gain, h_a)
    gb, token = gather_pass(gb, (y_a,))
    col_b = 4 * d_a // HEAD
    y_b = _attn_fwd(proj, q_gain, k_gain, bias, h_b, col_b)
    wg_pa, wg_pb, wg_out = gather_done(gb, (y_b,))
    pa = _matmul(y_a, wg_pa, name="proj_a", deps=(token,))
    pb = _matmul(y_b, wg_pb, name="proj_b")
    gate_off = 4 * d_a + 3 * d_b
    merged = _merge_fwd(proj, b_gate, pa, pb, gate_off)
    x2 = _matmul(merged, wg_out, res=x0, name="out_proj")
    gc, token = gather_pass(gc, (x2,))
    h2, r2 = _rmsnorm_fwd(x2, norm_ffn, "rmsnorm_ffn")
    (wg_fin,) = gather_done(gc, (h2,))
    ff_gate, ff_up, act = _ffn_in_swiglu(h2, wg_fin, deps=(token,))
    gd, token = gather_pass(gd, (act,))
    (wg_fout,) = gather_done(gd, ())
    dy, dy16, loss_part = _ffn_out_loss(act, wg_fout, x2, target)

    g_fout = _matmul(act, dy16, ta=True, name="dw_ffn_out")
    r_fout, token = reduce_start("fout", ["w_ffn_out"], [g_fout], ())
    dact = _matmul(dy16, wg_fout, tb=True, name="d_act", deps=(token,))
    r_fout, token = reduce_pair_done(r_fout, (dact,))
    dgu = _swiglu_bwd(dact, ff_gate, ff_up)
    g_fin = _matmul(h2, dgu, ta=True, name="dw_ffn_in", deps=(token,))
    r_fin, token = reduce_start("fin", ["w_ffn_in"], [g_fin], ())
    dh2 = _matmul(dgu, wg_fin, tb=True, name="d_h2", deps=(token,))
    r_fout, token_a = reduce_chip_done(r_fout, (dh2,))
    r_fin, token_b = reduce_pair_done(r_fin, (dh2,))
    dx2, dx2_16, g_norm_ffn = _rmsnorm_bwd(dh2, x2, r2, norm_ffn, dy, "rmsnorm_ffn_bwd", deps=(token_a, token_b))
    dmerged = _matmul(dx2_16, wg_out, tb=True, name="d_merged")
    dp_ab, dproj, g_bgate = _merge_bwd(dmerged, proj, b_gate, pa, pb, gate_off)
    g_out = _matmul(merged, dx2_16, ta=True, name="dw_out")
    g_pa = _matmul(y_a, dp_ab[0], ta=True, name="dw_proj_a")
    g_pb = _matmul(y_b, dp_ab[1], ta=True, name="dw_proj_b")
    r_mid, token = reduce_start("mid", ["w_proj_a", "w_proj_b", "w_out"], [g_pa, g_pb, g_out], ())
    dy_a = _matmul(dp_ab[0], wg_pa, tb=True, name="d_y_a", deps=(token,))
    dy_b = _matmul(dp_ab[1], wg_pb, tb=True, name="d_y_b")
    r_fin, token_a = reduce_chip_done(r_fin, (dy_b,))
    r_mid, token_b = reduce_pair_done(r_mid, (dy_b,))
    dproj, dbias, g_qg, g_kg = _attn_bwd(dproj, proj, q_gain, k_gain, bias, dy_b, h_b, col_b, deps=(token_a, token_b))
    r_mid, token = reduce_chip_done(r_mid, (dbias,))
    dproj, g_lb, g_gain = _hgrn_bwd(dproj, proj, o_pre, states, dy_a, hgrn_lb_logits, hgrn_out_gain, h_a, deps=(token,))
    g_in = _matmul(h1, dproj, ta=True, name="dw_in")
    r_in, token = reduce_start("in", ["w_in"], [g_in], ())
    g_rb = _relbias_reduce(dbias.transpose(1, 0, 2))[:, :N_REL]
    reduce_finish(r_mid, (token,))
    r_in, token = reduce_pair_done(r_in, (g_rb, upd["w_out"][0], upd["w_proj_a"][0], upd["w_proj_b"][0]))
    dh1 = _matmul(dproj, wg_in, tb=True, name="d_h1", deps=(token,))
    dx, _, g_norm_mix = _rmsnorm_bwd(dh1, x0, r1, norm_mix, dx2, "rmsnorm_mix_bwd")
    reduce_finish(r_fin, (dx,))
    reduce_finish(r_fout, (dx,))
    r_in, token = reduce_chip_done(r_in, (upd["w_ffn_in"][0], upd["w_ffn_out"][0]))

    small_w = [b_gate, norm_mix, norm_ffn, hgrn_lb_logits, hgrn_out_gain, q_gain, k_gain, rel_bias]
    small_m = [m_b_gate, m_norm_mix, m_norm_ffn, m_hgrn_lb_logits, m_hgrn_out_gain, m_q_gain, m_k_gain, m_rel_bias]
    small_v = [v_b_gate, v_norm_mix, v_norm_ffn, v_hgrn_lb_logits, v_hgrn_out_gain, v_q_gain, v_k_gain, v_rel_bias]
    small_g = [g_bgate, g_norm_mix, g_norm_ffn, g_lb, g_gain, g_qg, g_kg, g_rb]
    sizes = [w.size for w in small_w]
    length = -(-(sum(sizes) + 1) // 128) * 128

    def pack(parts_):
        flat = jnp.concatenate([p.reshape(1, -1) for p in parts_], axis=1)
        return jnp.pad(flat, ((0, 0), (0, length - flat.shape[1])))

    one = jnp.ones((1, 1), F32)
    packed = _small_allreduce_adamw(pack(small_g + [loss_part]), pack(small_w + [one]), pack(small_m + [one]),
                                    pack(small_v + [one]), deps=(token,))

    def unpack(vec):
        out, at = [], 0
        for w, n in zip(small_w, sizes):
            out.append(vec[0, at:at + n].reshape(w.shape))
            at += n
        return out, vec[0, at]

    (sg, loss), (sd, _), (sm, _), (sv, _) = [unpack(p) for p in packed]
    reduce_finish(r_in, (packed[0],))

    def ordered(small, bigs):
        bigs = [bigs[nm][None] for nm in names]
        return [bigs[0]] + small + bigs[1:]

    return (loss, dx.reshape(x.shape), *ordered(sg, g_big), *ordered(sd, {nm: upd[nm][0] for nm in names}),
            *ordered(sm, {nm: upd[nm][1] for nm in names}), *ordered(sv, {nm: upd[nm][2] for nm in names}))
```

```python
import functools

import jax
import jax.numpy as jnp
from jax import lax
from jax.experimental import pallas as pl
from jax.experimental.pallas import tpu as pltpu

F32 = jnp.float32
BF16 = jnp.bfloat16
SDS = jax.ShapeDtypeStruct
MESH = pl.DeviceIdType.MESH
HIGHEST = lax.Precision.HIGHEST

CHUNK = 64
SUB = 16
HEAD = 128
N_PAST = 8
BAND = (N_PAST + 1) * CHUNK
PAD = N_PAST * CHUNK
REL_FUTURE = CHUNK - 1
REL_PAST = 2 * CHUNK - 1
N_REL = REL_FUTURE + REL_PAST + 1
REL_LANES = 256
EPS = 1e-6
MIX_HEADS = 2
MIX_UNROLL = 4
MIX_UNROLL_BWD = 4
ATT_UNROLL = 8
ATT_UNROLL_BWD = 4
EXP_CLAMP = 80.0

ADAM_LR = 0.001
ADAM_B1 = 0.9
ADAM_B2 = 0.999
ADAM_EPS = 1e-08
ADAM_WD = 0.01
ADAM_STEP = 10

VMEM_LIMIT = 56 * 1024 * 1024

HBM = pl.BlockSpec(memory_space=pltpu.HBM)
ANY = pl.BlockSpec(memory_space=pl.ANY)
SEM = pl.BlockSpec(memory_space=pltpu.SEMAPHORE)

NT = (((1,), (1,)), ((), ()))
TN = (((0,), (0,)), ((), ()))
NN = (((1,), (0,)), ((), ()))


def _params(sem=None, **kw):
    return pltpu.CompilerParams(dimension_semantics=sem, vmem_limit_bytes=VMEM_LIMIT, **kw)


def _tile(n, pref, unit=128):
    if n <= pref:
        return n
    t = pref - pref % unit
    while n % t:
        t -= unit
    return t


def _loop(n, unroll, step, init):
    assert n % unroll == 0, (n, unroll)

    def several(i, carry):
        for u in range(unroll):
            carry = step(i * unroll + u, carry)
        return carry

    return lax.fori_loop(0, n // unroll, several, init)


def _sigmoid(x):
    return 1.0 / (1.0 + jnp.exp(-x))


def _dsilu(x, s):
    return s * (1.0 + x * (1.0 - s))


def _bdot(a, b, dims=NN):
    return lax.dot_general(a.astype(BF16), b.astype(BF16), dims, preferred_element_type=F32)


def _split(a):
    hi = a.astype(BF16)
    return hi, (a - hi.astype(F32)).astype(BF16)


def _dot3(a, b, dims):
    dot = lambda u, v: lax.dot_general(u, v, dims, preferred_element_type=F32)
    return dot(a[0], b[1]) + dot(a[1], b[0]) + dot(a[0], b[0])


def _fdot(a, b):
    return lax.dot_general(a, b, NN, precision=HIGHEST, preferred_element_type=F32)


MM_TILE_K = 5632
MM_TILE_N = 512


def _matmul_chunks(h, w, which, prev, pos, name, own_shard=False, deps=()):
    t, d = h.shape
    nc_ = w.shape[1] if own_shard else w.shape[1] // 4
    tm, tn = _tile(t, 1024), _tile(nc_, 1408)
    nn = nc_ // tn

    def chunk(q, p):
        sel = p[which[0]]
        for i in range(1, len(which)):
            sel = jnp.where(q == i, p[which[i]], sel)
        return sel

    def body(p_ref, h_ref, w_ref, *rest):
        rest[-1][...] = jnp.dot(h_ref[...], w_ref[...].astype(BF16), preferred_element_type=F32)

    if own_shard:
        w_spec = pl.BlockSpec((d, tn), lambda q, i, j, p: (0, j))
    else:
        w_spec = pl.BlockSpec((d, tn), lambda q, i, j, p: (0, chunk(q, p) * nn + j))
    n_extra = len(deps) + (prev is not None)
    return pl.pallas_call(
        body,
        grid_spec=pltpu.PrefetchScalarGridSpec(
            num_scalar_prefetch=1, grid=(len(which), t // tm, nn),
            in_specs=[pl.BlockSpec((tm, d), lambda q, i, j, p: (i, 0)), w_spec] + [ANY] * n_extra,
            out_specs=pl.BlockSpec((tm, tn), lambda q, i, j, p: (i, chunk(q, p) * nn + j))),
        out_shape=SDS((t, 4 * nc_), F32), input_output_aliases={3 + len(deps): 0} if prev is not None else {},
        compiler_params=_params(("arbitrary", "arbitrary", "arbitrary")),
        name=name)(pos, h, w, *deps, *(() if prev is None else (prev,)))


def _matmul(a, b, *, ta=False, tb=False, res=None, out_dtype=F32, name, deps=()):
    m, k = (a.shape[1], a.shape[0]) if ta else a.shape
    n = b.shape[0] if tb else b.shape[1]
    if k > MM_TILE_K:
        tk, tm, tn = _tile(k, MM_TILE_K // 2), _tile(m, 1024), _tile(n, 1024)
    else:
        tk = k
        tm, tn = _tile(m, 2048 if tk <= MM_TILE_K // 2 else 1024), _tile(n, MM_TILE_N)
    nk = k // tk
    dims = ((((0,) if ta else (1,)), ((1,) if tb else (0,))), ((), ()))

    def body(*refs):
        n_in = 2 + (res is not None)
        a_ref, b_ref = refs[:2]
        r_ref = refs[2] if res is not None else None
        o_ref = refs[n_in + len(deps)]
        part = lax.dot_general(a_ref[...].astype(BF16), b_ref[...].astype(BF16), dims, preferred_element_type=F32)

        def finish(out):
            if r_ref is not None:
                out = out + r_ref[...]
            o_ref[...] = out.astype(o_ref.dtype)

        if nk == 1:
            finish(part)
            return
        acc_ref = refs[-1]
        kk = pl.program_id(2)

        @pl.when(kk == 0)
        def _():
            acc_ref[...] = part

        @pl.when(jnp.logical_and(kk > 0, kk < nk - 1))
        def _():
            acc_ref[...] += part

        @pl.when(kk == nk - 1)
        def _():
            finish(acc_ref[...] + part)

    a_spec = pl.BlockSpec((tk, tm), lambda i, j, l: (l, i)) if ta else pl.BlockSpec((tm, tk), lambda i, j, l: (i, l))
    b_spec = pl.BlockSpec((tn, tk), lambda i, j, l: (j, l)) if tb else pl.BlockSpec((tk, tn), lambda i, j, l: (l, j))
    o_spec = pl.BlockSpec((tm, tn), lambda i, j, l: (i, j))
    in_specs = [a_spec, b_spec] + ([o_spec] if res is not None else []) + [ANY] * len(deps)
    args = (a, b) + ((res,) if res is not None else ()) + tuple(deps)
    return pl.pallas_call(
        body, grid=(m // tm, n // tn, nk), in_specs=in_specs, out_specs=o_spec,
        out_shape=SDS((m, n), out_dtype), scratch_shapes=[pltpu.VMEM((tm, tn), F32)] if nk > 1 else [],
        compiler_params=_params(("parallel", "parallel", "arbitrary")), name=name)(*args)


def _cast_into_full(w, kind, pos, name):
    r, n = w.shape
    tr = _tile(r, 512, 16)
    nr = r // tr
    if kind == "col":
        shape, o_spec = (r, 4 * n), pl.BlockSpec((tr, n), lambda i, p: (i, p[0]))
    else:
        shape, o_spec = (4 * r, n), pl.BlockSpec((tr, n), lambda i, p: (p[0] * nr + i, 0))

    def body(p_ref, w_ref, o_ref):
        o_ref[...] = w_ref[...].astype(BF16)

    return pl.pallas_call(
        body,
        grid_spec=pltpu.PrefetchScalarGridSpec(num_scalar_prefetch=1, grid=(nr,),
                                               in_specs=[pl.BlockSpec((tr, n), lambda i, p: (i, 0))], out_specs=o_spec),
        out_shape=SDS(shape, BF16), compiler_params=_params(("parallel",)), name=name)(pos, w)


def _rmsnorm_fwd(x, gain, name):
    t, d = x.shape
    tm = _tile(t, 256)

    def body(x_ref, g_ref, h_ref, r_ref):
        xv = x_ref[...]
        r = lax.rsqrt(jnp.mean(xv * xv, axis=-1, keepdims=True) + EPS)
        h_ref[...] = (xv * r * g_ref[...]).astype(BF16)
        r_ref[...] = r

    return pl.pallas_call(
        body, grid=(t // tm,),
        in_specs=[pl.BlockSpec((tm, d), lambda i: (i, 0)), pl.BlockSpec((1, d), lambda i: (0, 0))],
        out_specs=[pl.BlockSpec((tm, d), lambda i: (i, 0)), pl.BlockSpec((tm, 1), lambda i: (i, 0))],
        out_shape=[SDS((t, d), BF16), SDS((t, 1), F32)], compiler_params=_params(("parallel",)), name=name)(x, gain)


def _rmsnorm_bwd(dh, x, r, gain, dres, name, deps=()):
    t, d = x.shape
    tm = _tile(t, 256)

    def body(dh_ref, x_ref, r_ref, g_ref, dres_ref, *rest):
        dx_ref, dxb_ref, dg_ref = rest[len(deps):]

        @pl.when(pl.program_id(0) == 0)
        def _():
            dg_ref[...] = jnp.zeros_like(dg_ref)

        dhv, xv, rv = dh_ref[...], x_ref[...], r_ref[...]
        dg_ref[...] += jnp.sum(dhv * (xv * rv), axis=0, keepdims=True)
        u = dhv * g_ref[...]
        dx = dres_ref[...] + rv * u - xv * (rv * rv * rv) * jnp.mean(u * xv, axis=-1, keepdims=True)
        dx_ref[...] = dx
        dxb_ref[...] = dx.astype(BF16)

    row = pl.BlockSpec((tm, d), lambda i: (i, 0))
    vec = pl.BlockSpec((1, d), lambda i: (0, 0))
    return pl.pallas_call(
        body, grid=(t // tm,),
        in_specs=[row, row, pl.BlockSpec((tm, 1), lambda i: (i, 0)), vec, row] + [ANY] * len(deps),
        out_specs=[row, row, vec], out_shape=[SDS((t, d), F32), SDS((t, d), BF16), SDS((1, d), F32)],
        compiler_params=_params(("arbitrary",)), name=name)(dh, x, r, gain, dres, *deps)


def _merge_fwd(proj, b_gate, pa, pb, off):
    t, d = pa.shape
    tm, tc = _tile(t, 512), _tile(d, 512)
    nj = d // tc
    oa, ob = off // tc, off // tc + nj

    def body(la_ref, lb_ref, ba_ref, bb_ref, pa_ref, pb_ref, o_ref):
        ga = _sigmoid(la_ref[...] + ba_ref[...])
        gb = _sigmoid(lb_ref[...] + bb_ref[...])
        o_ref[...] = (ga * pa_ref[...] + gb * pb_ref[...]).astype(BF16)

    tile = pl.BlockSpec((tm, tc), lambda i, j: (i, j))
    return pl.pallas_call(
        body, grid=(t // tm, nj),
        in_specs=[pl.BlockSpec((tm, tc), lambda i, j: (i, oa + j)), pl.BlockSpec((tm, tc), lambda i, j: (i, ob + j)),
                  pl.BlockSpec((1, tc), lambda i, j: (0, j)), pl.BlockSpec((1, tc), lambda i, j: (0, nj + j)), tile, tile],
        out_specs=tile, out_shape=SDS((t, d), BF16), compiler_params=_params(("parallel", "parallel")),
        name="merge_fwd")(proj, proj, b_gate, b_gate, pa, pb)


def _merge_bwd(dmerged, proj, b_gate, pa, pb, off):
    t, d = pa.shape
    tm, tc = _tile(t, 512), _tile(d, 512)
    nj, ni = d // tc, t // tm
    o0 = off // tc

    def body(dm_ref, l_ref, b_ref, pa_ref, pb_ref, dp_ref, dl_ref, db_ref):
        s, i = pl.program_id(0), pl.program_id(2)
        p = jnp.where(s == 0, pa_ref[...], pb_ref[...])
        g = _sigmoid(l_ref[...] + b_ref[...])
        dm = dm_ref[...]
        dp_ref[0] = (dm * g).astype(BF16)
        dl = dm * p * g * (1.0 - g)
        dl_ref[...] = dl.astype(BF16)

        @pl.when(i == 0)
        def _():
            db_ref[...] = jnp.zeros_like(db_ref)

        db_ref[...] += jnp.sum(dl, axis=0, keepdims=True)

    tile = pl.BlockSpec((tm, tc), lambda s, j, i: (i, j))
    return pl.pallas_call(
        body, grid=(2, nj, ni),
        in_specs=[tile, pl.BlockSpec((tm, tc), lambda s, j, i: (i, o0 + s * nj + j)),
                  pl.BlockSpec((1, tc), lambda s, j, i: (0, s * nj + j)), tile, tile],
        out_specs=[pl.BlockSpec((1, tm, tc), lambda s, j, i: (s, i, j)),
                   pl.BlockSpec((tm, tc), lambda s, j, i: (i, o0 + s * nj + j)),
                   pl.BlockSpec((1, tc), lambda s, j, i: (0, s * nj + j))],
        out_shape=[SDS((2, t, d), BF16), SDS(proj.shape, BF16), SDS((1, 2 * d), F32)],
        compiler_params=_params(("arbitrary", "arbitrary", "arbitrary")),
        name="merge_bwd")(dmerged, proj, b_gate, pa, pb)


def _ffn_in_swiglu(h, w, deps=()):
    t, d = h.shape
    f = w.shape[1] // 2
    tm, tn = _tile(t, 2048), _tile(f, MM_TILE_N)
    nj = f // tn

    def body(h_ref, wg_ref, wu_ref, *rest):
        g_ref, u_ref, a_ref = rest[len(deps):]
        hv = h_ref[...]
        g = jnp.dot(hv, wg_ref[...], preferred_element_type=F32)
        u = jnp.dot(hv, wu_ref[...], preferred_element_type=F32)
        g_ref[...] = g
        u_ref[...] = u
        a_ref[...] = (g * _sigmoid(g) * u).astype(BF16)

    tile = pl.BlockSpec((tm, tn), lambda i, j: (i, j))
    return pl.pallas_call(
        body, grid=(t // tm, nj),
        in_specs=[pl.BlockSpec((tm, d), lambda i, j: (i, 0)), pl.BlockSpec((d, tn), lambda i, j: (0, j)),
                  pl.BlockSpec((d, tn), lambda i, j: (0, nj + j))] + [ANY] * len(deps),
        out_specs=[tile, tile, tile], out_shape=[SDS((t, f), F32), SDS((t, f), F32), SDS((t, f), BF16)],
        compiler_params=_params(("parallel", "parallel")), name="ffn_in_swiglu")(h, w, w, *deps)


def _swiglu_bwd(dact, gate, up):
    t, f = gate.shape
    tm = _tile(t, 128)

    def body(d_ref, g_ref, u_ref, o_ref):
        g, dv = g_ref[...], d_ref[...]
        sg = _sigmoid(g)
        o_ref[:, :f] = (dv * u_ref[...] * _dsilu(g, sg)).astype(BF16)
        o_ref[:, f:] = (dv * (g * sg)).astype(BF16)

    row = pl.BlockSpec((tm, f), lambda i: (i, 0))
    return pl.pallas_call(
        body, grid=(t // tm,), in_specs=[row, row, row],
        out_specs=pl.BlockSpec((tm, 2 * f), lambda i: (i, 0)), out_shape=SDS((t, 2 * f), BF16),
        compiler_params=_params(("parallel",)), name="swiglu_bwd")(dact, gate, up)


def _ffn_out_loss(act, w, x_res, target):
    t, d = x_res.shape
    k = act.shape[1]
    tm, tn = _tile(t, 1024), _tile(d, MM_TILE_N)

    def body(a_ref, w_ref, r_ref, t_ref, dy_ref, dyb_ref, l_ref):
        @pl.when(jnp.logical_and(pl.program_id(0) == 0, pl.program_id(1) == 0))
        def _():
            l_ref[...] = jnp.zeros_like(l_ref)

        y = jnp.dot(a_ref[...], w_ref[...], preferred_element_type=F32) + r_ref[...]
        e = y - t_ref[...]
        dy = e * (1.0 / d)
        dy_ref[...] = dy
        dyb_ref[...] = dy.astype(BF16)
        l_ref[...] += (0.5 / d) * jnp.sum(jnp.sum(e * e, axis=-1, keepdims=True), axis=0, keepdims=True)

    tile = pl.BlockSpec((tm, tn), lambda i, j: (i, j))
    return pl.pallas_call(
        body, grid=(t // tm, d // tn),
        in_specs=[pl.BlockSpec((tm, k), lambda i, j: (i, 0)), pl.BlockSpec((k, tn), lambda i, j: (0, j)), tile, tile],
        out_specs=[tile, tile, pl.BlockSpec((1, 1), lambda i, j: (0, 0))],
        out_shape=[SDS((t, d), F32), SDS((t, d), BF16), SDS((1, 1), F32)],
        compiler_params=_params(("arbitrary", "arbitrary")), name="ffn_out_loss")(act, w, x_res, target)


def _rel_onehot(qi):
    p = lax.broadcasted_iota(jnp.int32, (REL_LANES, BAND), 1)
    r = lax.broadcasted_iota(jnp.int32, (REL_LANES, BAND), 0)
    idx = jnp.clip(qi + PAD - p, -REL_FUTURE, REL_PAST) + REL_FUTURE
    return (idx == r).astype(F32)


def _relbias_expand(rb):
    h = rb.shape[0]

    def body(rb_ref, o_ref):
        def step(qi, _):
            o_ref[qi] = _fdot(rb_ref[...], _rel_onehot(qi))
            return 0

        lax.fori_loop(0, CHUNK, step, 0)

    return pl.pallas_call(body, out_shape=SDS((CHUNK, h, BAND), F32), compiler_params=_params(),
                          name="relbias_expand")(rb)


def _relbias_reduce(dbias):
    h = dbias.shape[1]

    rows_per_pass = 4

    def body(db_ref, o_ref):
        def step(i, acc):
            parts = []
            for u in range(rows_per_pass):
                qi = i * rows_per_pass + u
                xv = db_ref[qi]
                hi = xv.astype(BF16)
                rest = xv - hi.astype(F32)
                mid = rest.astype(BF16)
                low = (rest - mid.astype(F32)).astype(BF16)
                parts.append(lax.dot_general(jnp.concatenate([hi, mid, low], axis=0), _rel_onehot(qi).astype(BF16), NT,
                                             preferred_element_type=F32))
            for part in parts:
                acc = acc + (part[0:h] + part[h:2 * h] + part[2 * h:3 * h])
            return acc

        o_ref[...] = lax.fori_loop(0, CHUNK // rows_per_pass, step, jnp.zeros((h, REL_LANES), F32))

    return pl.pallas_call(body, out_shape=SDS((h, REL_LANES), F32), compiler_params=_params(),
                          name="relbias_reduce")(dbias)


def _lower_bound(l_ref):
    l0, l1 = l_ref[0:1, :], l_ref[1:2, :]
    m = jnp.maximum(l0, l1)
    e0, e1 = jnp.exp(l0 - m), jnp.exp(l1 - m)
    return e0 / (e0 + e1)


def _tri(lower):
    r = lax.broadcasted_iota(jnp.int32, (CHUNK, CHUNK), 0)
    c = lax.broadcasted_iota(jnp.int32, (CHUNK, CHUNK), 1)
    return r >= c if lower else r <= c


def _hgrn_intra(qs, kk, b_s):
    rows = lax.broadcasted_iota(jnp.int32, (CHUNK, HEAD), 0)
    b = b_s[...]
    out = []
    for i in range(CHUNK // SUB):
        lo = i * SUB
        ref = jnp.zeros((1, HEAD), F32) if i == 0 else b_s[lo - 1:lo, :]
        eq = jnp.exp(b[lo:lo + SUB] - ref)
        qt = _split(qs[lo:lo + SUB] * eq)
        e = jnp.where(rows < lo + SUB, jnp.exp(jnp.minimum(ref - b, EXP_CLAMP)), 0.0)
        kt = _split(kk * e)
        out.append((eq, qt, e, kt))
    return out


def _hgrn_scores(blocks):
    tr = lax.broadcasted_iota(jnp.int32, (SUB, CHUNK), 0)
    tc = lax.broadcasted_iota(jnp.int32, (SUB, CHUNK), 1)
    return jnp.concatenate([jnp.where(tc <= tr + i * SUB, _dot3(qt, kt, NT), 0.0)
                            for i, (_, qt, _, kt) in enumerate(blocks)], axis=0)


def _hgrn_fwd(proj, lb_logits, gain, n_heads):
    t = proj.shape[0]
    nc = t // CHUNK
    da = n_heads * HEAD
    hp = MIX_HEADS
    wide = hp * HEAD

    def body(q_ref, f_ref, i_ref, g_ref, l_ref, gain_ref, y_ref, o_ref, st_ref, state, b_s):
        state[...] = jnp.zeros_like(state)
        lb_all = _lower_bound(l_ref)
        tril = _tri(True).astype(F32)

        def chunks(i, _):
            dot = functools.partial(lax.dot_general, preferred_element_type=F32)
            items = []
            for u in range(MIX_UNROLL):
                for hh in range(hp):
                    j = i * MIX_UNROLL + u
                    sl = pl.ds(pl.multiple_of(j * CHUNK, CHUNK), CHUNK)
                    cols = slice(hh * HEAD, (hh + 1) * HEAD)
                    lb = lb_all[:, cols]
                    fg = lb + (1.0 - lb) * _sigmoid(f_ref[sl, cols])
                    qv = q_ref[sl, cols]
                    gv = g_ref[sl, cols]
                    items.append(dict(hh=hh, j=j, sl=sl, cols=cols, lf=jnp.log(fg), kk=1.0 - fg, qs=qv * _sigmoid(qv),
                                      vb=i_ref[sl, cols].astype(BF16), gate=gv * _sigmoid(gv)))
            for it in items:
                it["b"] = _fdot(tril, it["lf"])
            for slot, it in enumerate(items):
                b = it["b"]
                b_s[slot] = b
                it["blocks"] = _hgrn_intra(it["qs"], it["kk"], b_s.at[slot])
                it["ebl"] = jnp.exp(b_s[slot, CHUNK - 1:CHUNK, :])
                it["qe"] = (it["qs"] * jnp.exp(b)).astype(BF16)
                it["kd"] = (it["kk"] * jnp.exp(b_s[slot, CHUNK - 1:CHUNK, :] - b)).astype(BF16)
            for it in items:
                it["a"] = _hgrn_scores(it["blocks"]).astype(BF16)
            for it in items:
                it["kv"] = dot(it["vb"], it["kd"], TN)
                it["o"] = dot(it["a"], it["vb"], NN)
            s_now = [state[hh] for hh in range(hp)]
            for it in items:
                it["s_in"] = s_now[it["hh"]]
                s_now[it["hh"]] = it["s_in"] * it["ebl"] + it["kv"]
            for hh in range(hp):
                state[hh] = s_now[hh]
            for it in items:
                it["o"] = it["o"] + dot(it["qe"], it["s_in"].astype(BF16), NT)
            for it in items:
                o, sl, cols = it["o"], it["sl"], it["cols"]
                st_ref[it["hh"], it["j"]] = it["s_in"]
                o_ref[sl, cols] = o
                rr = lax.rsqrt(jnp.mean(o * o, axis=-1, keepdims=True) + EPS)
                y_ref[sl, cols] = (o * rr * gain_ref[:, cols] * it["gate"]).astype(BF16)
            return 0

        assert nc % MIX_UNROLL == 0, (nc, MIX_UNROLL)
        lax.fori_loop(0, nc // MIX_UNROLL, chunks, 0)

    col = lambda k: pl.BlockSpec((t, wide), lambda h: (0, k * (n_heads // hp) + h))
    vec = pl.BlockSpec((1, wide), lambda h: (0, h))
    return pl.pallas_call(
        body, grid=(n_heads // hp,),
        in_specs=[col(0), col(1), col(2), col(3), pl.BlockSpec((2, wide), lambda h: (0, h)), vec],
        out_specs=[pl.BlockSpec((t, wide), lambda h: (0, h)), pl.BlockSpec((t, wide), lambda h: (0, h)),
                   pl.BlockSpec((hp, nc, HEAD, HEAD), lambda h: (h, 0, 0, 0))],
        out_shape=[SDS((t, da), BF16), SDS((t, da), F32), SDS((n_heads, nc, HEAD, HEAD), F32)],
        scratch_shapes=[pltpu.VMEM((hp, HEAD, HEAD), F32), pltpu.VMEM((hp * MIX_UNROLL, CHUNK, HEAD), F32)],
        compiler_params=_params(("parallel",)), name="hgrn_fwd")(proj, proj, proj, proj, lb_logits, gain)


def _hgrn_bwd(dproj, proj, o_pre, states, dy, lb_logits, gain, n_heads, deps=()):
    t = proj.shape[0]
    nc = t // CHUNK
    da = n_heads * HEAD
    hp = MIX_HEADS
    wide = hp * HEAD

    def body(*refs):
        (q_ref, f_ref, i_ref, g_ref, o_ref, st_ref, dy_ref, l_ref, gain_ref,
         dproj_ref, dl_ref, dgain_ref, res, dstate, b_s) = refs[1 + len(deps):]

        @pl.when(pl.program_id(1) == 0)
        def _():
            dstate[...] = jnp.zeros_like(dstate)
            lb_all = _lower_bound(l_ref)
            tril_m, tril, triu = _tri(True), _tri(True).astype(F32), _tri(False).astype(F32)
            last = lax.broadcasted_iota(jnp.int32, (CHUNK, HEAD), 0) == CHUNK - 1

            def chunks(i, carry):
                dot = functools.partial(lax.dot_general, preferred_element_type=F32)
                items = []
                for u in range(MIX_UNROLL_BWD):
                    for hh in range(hp):
                        j = nc - 1 - (i * MIX_UNROLL_BWD + u)
                        sl = pl.ds(pl.multiple_of(j * CHUNK, CHUNK), CHUNK)
                        cols = slice(hh * HEAD, (hh + 1) * HEAD)
                        lb, gain_v = lb_all[:, cols], gain_ref[:, cols]
                        sg = _sigmoid(f_ref[sl, cols])
                        fg = lb + (1.0 - lb) * sg
                        qv = q_ref[sl, cols]
                        sq = _sigmoid(qv)
                        gv = g_ref[sl, cols]
                        sgg = _sigmoid(gv)
                        silg = gv * sgg
                        o = o_ref[sl, cols]
                        dyv = dy_ref[sl, cols]
                        rr = lax.rsqrt(jnp.mean(o * o, axis=-1, keepdims=True) + EPS)
                        on = o * rr
                        don = dyv * gain_v * silg
                        do = (rr * don - o * (rr * rr * rr) * jnp.mean(don * o, axis=-1, keepdims=True)).astype(BF16)
                        items.append(dict(
                            hh=hh, j=j, sl=sl, cols=cols, lb=lb, sg=sg, fg=fg, kk=1.0 - fg, qv=qv, sq=sq, qs=qv * sq,
                            vb=i_ref[sl, cols].astype(BF16), do=do, dg=dyv * on * gain_v * _dsilu(gv, sgg),
                            dgain=jnp.sum(dyv * on * silg, axis=0, keepdims=True)))
                for it in items:
                    it["b"] = _fdot(tril, jnp.log(it["fg"]))
                for slot, it in enumerate(items):
                    b = it["b"]
                    b_s[slot] = b
                    it["blocks"] = _hgrn_intra(it["qs"], it["kk"], b_s.at[slot])
                    bl = b_s[slot, CHUNK - 1:CHUNK, :]
                    it["eb"], it["ebl"], it["ekd"] = jnp.exp(b), jnp.exp(bl), jnp.exp(bl - b)
                    it["s_in"] = st_ref[it["hh"], it["j"]]
                for it in items:
                    it["a"] = _hgrn_scores(it["blocks"]).astype(BF16)
                    it["da"] = jnp.where(tril_m, dot(it["do"], it["vb"], NT), 0.0)
                for it in items:
                    dq_rows = []
                    dk = jnp.zeros((CHUNK, HEAD), F32)
                    for blk, (eq, qt, e, kt) in enumerate(it["blocks"]):
                        da_i = _split(it["da"][blk * SUB:(blk + 1) * SUB])
                        dq_rows.append(eq * _dot3(da_i, kt, NN))
                        dk = dk + e * _dot3(da_i, qt, TN)
                    it["dq"] = jnp.concatenate(dq_rows, axis=0) + dot(it["do"], it["s_in"].astype(BF16), NN) * it["eb"]
                    it["dk"] = dk
                    it["dv"] = dot(it["a"], it["do"], TN)
                    it["g"] = dot(it["do"], (it["qs"] * it["eb"]).astype(BF16), TN)
                ds_now = [dstate[hh] for hh in range(hp)]
                for it in items:
                    it["ds_out"] = ds_now[it["hh"]]
                    ds_now[it["hh"]] = it["ds_out"] * it["ebl"] + it["g"]
                for hh in range(hp):
                    dstate[hh] = ds_now[hh]
                for it in items:
                    dsb = it["ds_out"].astype(BF16)
                    it["dv"] = it["dv"] + dot((it["kk"] * it["ekd"]).astype(BF16), dsb, NT)
                    it["dk_state"] = it["ekd"] * dot(it["vb"], dsb, NN)
                for it in items:
                    kk, dk_state = it["kk"], it["dk_state"]
                    it["dk"] = it["dk"] + dk_state
                    extra = (jnp.sum(kk * dk_state, axis=0, keepdims=True)
                             + it["ebl"] * jnp.sum(it["s_in"] * it["ds_out"], axis=0, keepdims=True))
                    it["db"] = it["qs"] * it["dq"] - kk * it["dk"] + jnp.where(last, extra, 0.0)
                for it in items:
                    it["dlf"] = _fdot(triu, it["db"])
                carry = list(carry)
                for it in items:
                    hh, sl, cols, sg, lb = it["hh"], it["sl"], it["cols"], it["sg"], it["lb"]
                    dfg = it["dlf"] / it["fg"] - it["dk"]
                    dlb_acc, dgain_acc = carry[hh]
                    carry[hh] = (dlb_acc + jnp.sum(dfg * (1.0 - sg), axis=0, keepdims=True), dgain_acc + it["dgain"])
                    res[0, sl, cols] = (it["dq"] * _dsilu(it["qv"], it["sq"])).astype(BF16)
                    res[1, sl, cols] = (dfg * (1.0 - lb) * sg * (1.0 - sg)).astype(BF16)
                    res[2, sl, cols] = it["dv"].astype(BF16)
                    res[3, sl, cols] = it["dg"].astype(BF16)
                return tuple(carry)

            assert nc % MIX_UNROLL_BWD == 0, (nc, MIX_UNROLL_BWD)
            zero = jnp.zeros((1, HEAD), F32)
            sums = lax.fori_loop(0, nc // MIX_UNROLL_BWD, chunks, ((zero, zero),) * hp)
            for hh, (dlb, dgain) in enumerate(sums):
                cols = slice(hh * HEAD, (hh + 1) * HEAD)
                lb = lb_all[:, cols]
                dgain_ref[:, cols] = dgain
                dl0 = dlb * lb * (1.0 - lb)
                dl_ref[0:1, cols] = dl0
                dl_ref[1:2, cols] = -dl0

        dproj_ref[...] = res[pl.program_id(1)]

    ng = n_heads // hp
    col = lambda k: pl.BlockSpec((t, wide), lambda h, p: (0, k * ng + h))
    head = pl.BlockSpec((t, wide), lambda h, p: (0, h))
    vec = pl.BlockSpec((1, wide), lambda h, p: (0, h))
    return pl.pallas_call(
        body, grid=(ng, 4),
        in_specs=[ANY] * (1 + len(deps)) + [col(0), col(1), col(2), col(3), head,
                  pl.BlockSpec((hp, nc, HEAD, HEAD), lambda h, p: (h, 0, 0, 0)),
                  head, pl.BlockSpec((2, wide), lambda h, p: (0, h)), vec],
        out_specs=[pl.BlockSpec((t, wide), lambda h, p: (0, p * ng + h)),
                   pl.BlockSpec((2, wide), lambda h, p: (0, h)), vec],
        out_shape=[SDS(dproj.shape, BF16), SDS((2, da), F32), SDS((1, da), F32)],
        scratch_shapes=[pltpu.VMEM((4, t, wide), BF16), pltpu.VMEM((hp, HEAD, HEAD), F32),
                        pltpu.VMEM((hp * MIX_UNROLL_BWD, CHUNK, HEAD), F32)],
        input_output_aliases={0: 0}, compiler_params=_params(("arbitrary", "arbitrary")),
        name="hgrn_bwd")(dproj, *deps, proj, proj, proj, proj, o_pre, states, dy, lb_logits, gain)


ROWS = 256


def _head_norm(x_ref, gain, dst, dst_off, t):
    def step(i, _):
        sl = pl.ds(pl.multiple_of(i * ROWS, ROWS), ROWS)
        xv = x_ref[sl, :]
        r = lax.rsqrt(jnp.mean(xv * xv, axis=-1, keepdims=True) + EPS)
        dst[pl.ds(pl.multiple_of(dst_off + i * ROWS, ROWS), ROWS), :] = (xv * r * gain).astype(BF16)
        return 0

    lax.fori_loop(0, t // ROWS, step, 0)


def _head_norm_bwd(x_ref, gain, dn_ref, dn_off, out, slot, t):
    def step(i, acc):
        sl = pl.ds(pl.multiple_of(i * ROWS, ROWS), ROWS)
        xv = x_ref[sl, :]
        dn = dn_ref[pl.ds(pl.multiple_of(dn_off + i * ROWS, ROWS), ROWS), :]
        r = lax.rsqrt(jnp.mean(xv * xv, axis=-1, keepdims=True) + EPS)
        u = dn * gain
        out[slot, sl, :] = r * u - xv * (r * r * r) * jnp.mean(u * xv, axis=-1, keepdims=True)
        return acc + jnp.sum(dn * (xv * r), axis=0, keepdims=True)

    return lax.fori_loop(0, t // ROWS, step, jnp.zeros((1, HEAD), F32))


def _attn_scores(qn, kpad, n):
    qc = qn[pl.ds(pl.multiple_of(n * CHUNK, CHUNK), CHUNK), :]
    band = pl.ds(pl.multiple_of(n * CHUNK, CHUNK), BAND)
    return qc, band, lax.dot_general(qc, kpad[band, :], NT, preferred_element_type=F32)


def _attn_softmax(raw, bias_ref, n):
    s = raw * (HEAD ** -0.5) + bias_ref[0]
    col = lax.broadcasted_iota(jnp.int32, (CHUNK, BAND), 1)
    s = jnp.where(col >= PAD - n * CHUNK, s, -jnp.inf)
    p = jnp.exp(s - jnp.max(s, axis=-1, keepdims=True))
    return p / jnp.sum(p, axis=-1, keepdims=True)


def _attn_fwd(proj, q_gain, k_gain, bias, n_heads, col0):
    t = proj.shape[0]
    nc = t // CHUNK

    def body(q_ref, k_ref, v_ref, qg_ref, kg_ref, bias_ref, y_ref, qn, kpad, vpad):
        kpad[0:PAD, :] = jnp.zeros((PAD, HEAD), BF16)
        vpad[0:PAD, :] = jnp.zeros((PAD, HEAD), BF16)
        _head_norm(q_ref, qg_ref[...], qn, 0, t)
        _head_norm(k_ref, kg_ref[...], kpad, PAD, t)

        def copy_v(i, _):
            vpad[pl.ds(pl.multiple_of(PAD + i * ROWS, ROWS), ROWS), :] = v_ref[
                pl.ds(pl.multiple_of(i * ROWS, ROWS), ROWS), :].astype(BF16)
            return 0

        lax.fori_loop(0, t // ROWS, copy_v, 0)

        def chunks(i, _):
            ns = [i * ATT_UNROLL + u for u in range(ATT_UNROLL)]
            scored = [_attn_scores(qn, kpad, n) for n in ns]
            probs = [_attn_softmax(raw, bias_ref, n).astype(BF16) for n, (_, _, raw) in zip(ns, scored)]
            outs = [lax.dot_general(p, vpad[band, :], NN, preferred_element_type=F32).astype(BF16)
                    for p, (_, band, _) in zip(probs, scored)]
            for n, o in zip(ns, outs):
                y_ref[pl.ds(pl.multiple_of(n * CHUNK, CHUNK), CHUNK), :] = o
            return 0

        assert nc % ATT_UNROLL == 0, (nc, ATT_UNROLL)
        lax.fori_loop(0, nc // ATT_UNROLL, chunks, 0)

    col = lambda k: pl.BlockSpec((t, HEAD), lambda h: (0, col0 + k * n_heads + h))
    vec = pl.BlockSpec((1, HEAD), lambda h: (0, 0))
    return pl.pallas_call(
        body, grid=(n_heads,),
        in_specs=[col(0), col(1), col(2), vec, vec, pl.BlockSpec((1, CHUNK, BAND), lambda h: (h, 0, 0))],
        out_specs=pl.BlockSpec((t, HEAD), lambda h: (0, h)), out_shape=SDS((t, n_heads * HEAD), BF16),
        scratch_shapes=[pltpu.VMEM((t, HEAD), BF16), pltpu.VMEM((t + PAD, HEAD), BF16), pltpu.VMEM((t + PAD, HEAD), BF16)],
        compiler_params=_params(("parallel",)), name="attn_fwd")(proj, proj, proj, q_gain, k_gain, bias)


def _attn_bwd(dproj, proj, q_gain, k_gain, bias, dy, n_heads, col0, deps=()):
    t = proj.shape[0]
    nc = t // CHUNK

    def body(*refs):
        (q_ref, k_ref, v_ref, qg_ref, kg_ref, bias_ref, dy_ref,
         dproj_ref, dbias_ref, dqg_ref, dkg_ref, qn, kpad, vpad, dqn, dk_acc, dv_acc, res) = refs[1 + len(deps):]
        h, part = pl.program_id(0), pl.program_id(1)

        @pl.when(part == 0)
        def _():
            kpad[0:PAD, :] = jnp.zeros((PAD, HEAD), BF16)
            vpad[0:PAD, :] = jnp.zeros((PAD, HEAD), BF16)
            _head_norm(q_ref, qg_ref[...], qn, 0, t)
            _head_norm(k_ref, kg_ref[...], kpad, PAD, t)

            def prep(i, _):
                sl = pl.ds(pl.multiple_of(PAD + i * ROWS, ROWS), ROWS)
                vpad[sl, :] = v_ref[pl.ds(pl.multiple_of(i * ROWS, ROWS), ROWS), :].astype(BF16)
                return 0

            lax.fori_loop(0, t // ROWS, prep, 0)

            def clear(i, _):
                sl = pl.ds(pl.multiple_of(i * ROWS, ROWS), ROWS)
                dk_acc[sl, :] = jnp.zeros((ROWS, HEAD), F32)
                dv_acc[sl, :] = jnp.zeros((ROWS, HEAD), F32)
                return 0

            lax.fori_loop(0, (t + PAD) // ROWS, clear, 0)
            dbias_ref[0] = jnp.zeros((CHUNK, BAND), F32)

            def chunks(i, _):
                dot = functools.partial(lax.dot_general, preferred_element_type=F32)
                ns = [i * ATT_UNROLL_BWD + u for u in range(ATT_UNROLL_BWD)]
                scored = [_attn_scores(qn, kpad, n) for n in ns]
                dos = [dy_ref[pl.ds(pl.multiple_of(n * CHUNK, CHUNK), CHUNK), :].astype(BF16) for n in ns]
                dps = [dot(do, vpad[band, :], NT) for do, (_, band, _) in zip(dos, scored)]
                ps, dss = [], []
                for n, (_, _, raw), dp in zip(ns, scored, dps):
                    p = _attn_softmax(raw, bias_ref, n)
                    ds = p * (dp - jnp.sum(dp * p, axis=-1, keepdims=True))
                    dbias_ref[0] += ds
                    ps.append(p.astype(BF16))
                    dss.append((ds * (HEAD ** -0.5)).astype(BF16))
                dqs = [dot(d, kpad[band, :], NN) for d, (_, band, _) in zip(dss, scored)]
                dks = [dot(d, qc, TN) for d, (qc, _, _) in zip(dss, scored)]
                dvs = [dot(p, do, TN) for p, do in zip(ps, dos)]
                for n, (_, band, _), dq, dk, dv in zip(ns, scored, dqs, dks, dvs):
                    dqn[pl.ds(pl.multiple_of(n * CHUNK, CHUNK), CHUNK), :] = dq
                    dk_acc[band, :] += dk
                    dv_acc[band, :] += dv
                return 0

            assert nc % ATT_UNROLL_BWD == 0, (nc, ATT_UNROLL_BWD)
            lax.fori_loop(0, nc // ATT_UNROLL_BWD, chunks, 0)
            dqg = _head_norm_bwd(q_ref, qg_ref[...], dqn, 0, res, 0, t)
            dkg = _head_norm_bwd(k_ref, kg_ref[...], dk_acc, PAD, res, 1, t)

            def put_v(i, _):
                sl = pl.ds(pl.multiple_of(i * ROWS, ROWS), ROWS)
                res[2, sl, :] = dv_acc[pl.ds(pl.multiple_of(PAD + i * ROWS, ROWS), ROWS), :]
                return 0

            lax.fori_loop(0, t // ROWS, put_v, 0)

            @pl.when(h == 0)
            def _():
                dqg_ref[...] = jnp.zeros_like(dqg_ref)
                dkg_ref[...] = jnp.zeros_like(dkg_ref)

            dqg_ref[...] += dqg
            dkg_ref[...] += dkg

        dproj_ref[...] = res[part].astype(BF16)

    col = lambda k: pl.BlockSpec((t, HEAD), lambda h, p: (0, col0 + k * n_heads + h))
    vec = pl.BlockSpec((1, HEAD), lambda h, p: (0, 0))
    btile = pl.BlockSpec((1, CHUNK, BAND), lambda h, p: (h, 0, 0))
    return pl.pallas_call(
        body, grid=(n_heads, 3),
        in_specs=[ANY] * (1 + len(deps)) + [col(0), col(1), col(2), vec, vec, btile,
                                            pl.BlockSpec((t, HEAD), lambda h, p: (0, h))],
        out_specs=[pl.BlockSpec((t, HEAD), lambda h, p: (0, col0 + p * n_heads + h)), btile, vec, vec],
        out_shape=[SDS(dproj.shape, BF16), SDS((n_heads, CHUNK, BAND), F32), SDS((1, HEAD), F32), SDS((1, HEAD), F32)],
        scratch_shapes=[pltpu.VMEM((t, HEAD), BF16), pltpu.VMEM((t + PAD, HEAD), BF16), pltpu.VMEM((t + PAD, HEAD), BF16),
                        pltpu.VMEM((t, HEAD), F32), pltpu.VMEM((t + PAD, HEAD), F32), pltpu.VMEM((t + PAD, HEAD), F32),
                        pltpu.VMEM((3, t, HEAD), F32)],
        input_output_aliases={0: 0}, compiler_params=_params(("arbitrary", "arbitrary")),
        name="attn_bwd")(dproj, *deps, proj, proj, proj, q_gain, k_gain, bias, dy)


def _place():
    x, y, c = lax.axis_index("x"), lax.axis_index("y"), lax.axis_index("c")
    others = [(1 - x, y), (x, 1 - y), (1 - x, 1 - y)]
    return x, y, c, others


def _chunk_of(ref, kind, chip, half, shard_shape):
    r, n = shard_shape
    hr = r // 2
    if kind == "col":
        rows = pl.ds(0, r) if half is None else pl.ds(half * hr, hr)
        return ref.at[rows, pl.ds(chip * n, n)]
    rows = pl.ds(chip * r, r) if half is None else pl.ds(chip * r + half * hr, hr)
    return ref.at[rows, :]


EFFECT = pltpu.SideEffectType.DATAFLOW_SIDE_EFFECTING


def _start_copies(name, bufs, plan, n, deps):
    nb, nd = len(bufs), len(deps)

    def body(*refs):
        send, recv, token = refs[nb + nd], refs[nb + nd + 1], refs[-1]
        for cp in plan(refs[:nb], send, recv)[0]:
            cp.start()
        token[...] = jnp.zeros_like(token)

    out = pl.pallas_call(
        body, name=name,
        out_shape=(pltpu.SemaphoreType.DMA((n,)), pltpu.SemaphoreType.DMA((n,)),
                   *[pltpu.HBM(b.shape, b.dtype) for b in bufs], SDS((8, 128), F32)),
        in_specs=[HBM] * nb + [ANY] * nd,
        out_specs=(SEM, SEM, *[HBM] * nb, pl.BlockSpec(memory_space=pltpu.VMEM)),
        input_output_aliases={i: 2 + i for i in range(nb)},
        compiler_params=pltpu.CompilerParams(has_side_effects=EFFECT),
    )(*[pltpu.with_memory_space_constraint(b, pltpu.HBM) for b in bufs], *deps)
    return out[0], out[1], list(out[2:2 + nb]), out[-1]


def _wait_copies(name, bufs, send, recv, plan, after):
    nb = len(bufs)

    def body(*refs):
        sends, recvs = plan(refs[:nb], refs[nb], refs[nb + 1])
        for cp in sends:
            cp.wait_send()
        for cp in recvs:
            cp.wait_recv()

    out = pl.pallas_call(
        body, name=name, out_shape=tuple(pltpu.HBM(b.shape, b.dtype) for b in bufs),
        in_specs=[HBM] * nb + [SEM, SEM] + [ANY] * len(after), out_specs=tuple([HBM] * nb),
        input_output_aliases={i: i for i in range(nb)},
        compiler_params=pltpu.CompilerParams(has_side_effects=EFFECT),
    )(*bufs, send, recv, *after)
    return list(out)


def _remote(src, dst, send, recv, i, dev):
    return pltpu.make_async_remote_copy(src_ref=src, dst_ref=dst, send_sem=send.at[i], recv_sem=recv.at[i],
                                        device_id=dev, device_id_type=MESH)


ALL_RELATIONS = (0, 1, 2)


def _plan_gather_ici(kinds, shapes, rels=ALL_RELATIONS):
    def plan(refs, send, recv):
        x, y, c, others = _place()
        sends, recvs = [], []
        for w, (kind, ss) in enumerate(zip(kinds, shapes)):
            for p in rels:
                px, py = others[p]
                mine = _chunk_of(refs[w], kind, 2 * x + y, c, ss)
                theirs = _chunk_of(refs[w], kind, 2 * px + py, c, ss)
                sends.append(_remote(mine, mine, send, recv, 3 * w + p, (px, py, c)))
                recvs.append(_remote(theirs, theirs, send, recv, 3 * w + p, (px, py, c)))
        return sends, recvs

    return plan, 3 * len(kinds)


def _plan_gather_pass(kinds, shapes, rels=ALL_RELATIONS):
    def plan(refs, send, recv):
        x, y, c, others = _place()
        sends, recvs = [], []
        for w, (kind, ss) in enumerate(zip(kinds, shapes)):
            for i, p in enumerate(rels):
                px, py = others[p]
                got = _chunk_of(refs[w], kind, 2 * px + py, c, ss)
                coming = _chunk_of(refs[w], kind, 2 * px + py, 1 - c, ss)
                sends.append(_remote(got, got, send, recv, len(rels) * w + i, (x, y, 1 - c)))
                recvs.append(_remote(coming, coming, send, recv, len(rels) * w + i, (x, y, 1 - c)))
        return sends, recvs

    return plan, len(rels) * len(kinds)


def _plan_pair(kinds, shapes):
    nw = len(kinds)

    def plan(refs, send, recv):
        x, y, c, _ = _place()
        sends = []
        for w, (kind, ss) in enumerate(zip(kinds, shapes)):
            for k in range(4):
                sends.append(_remote(_chunk_of(refs[w], kind, k, 1 - c, ss), refs[nw + w].at[k], send, recv,
                                     4 * w + k, (x, y, 1 - c)))
        return sends, sends

    return plan, 4 * nw


def _plan_chip(nw):
    def plan(refs, send, recv):
        x, y, c, others = _place()
        sends = []
        for w in range(nw):
            for p, (px, py) in enumerate(others):
                sends.append(_remote(refs[w].at[p], refs[nw + w].at[p], send, recv, 3 * w + p, (px, py, c)))
        return sends, sends

    return plan, 3 * nw


def _plan_share(slabs):
    def plan(refs, send, recv):
        x, y, c, _ = _place()
        sends, recvs, i = [], [], 0
        for w, ns in enumerate(slabs):
            for s in range(ns):
                sends.append(_remote(refs[w].at[s, c], refs[w].at[s, c], send, recv, i, (x, y, 1 - c)))
                recvs.append(_remote(refs[w].at[s, 1 - c], refs[w].at[s, 1 - c], send, recv, i, (x, y, 1 - c)))
                i += 1
        return sends, recvs

    return plan, sum(slabs)


def _grad_half_spec(kind, tr, tn, nr, nn, chunk):
    if kind == "col":
        return pl.BlockSpec((tr, tn), lambda *a: (a[-1][1] * nr + a[-3], chunk(*a) * nn + a[-2]))
    return pl.BlockSpec((tr, tn), lambda *a: ((2 * chunk(*a) + a[-1][1]) * nr + a[-3], a[-2]))


def _pair_add(grad, got, kind, shard_shape, pos, name):
    r, n = shard_shape
    hr = r // 2
    tr, tn = _tile(hr, 256, 16), _tile(n, 1408)
    nr, nn = hr // tr, n // tn
    g_spec = _grad_half_spec(kind, tr, tn, nr, nn, lambda p, i, j, pos_: pos_[2 + p])
    r_spec = pl.BlockSpec((1, tr, tn), lambda p, i, j, pos_: (pos_[2 + p], i, j))
    o_spec = pl.BlockSpec((1, tr, tn), lambda p, i, j, pos_: (p, i, j))

    def body(pos_ref, g_ref, r_ref, o_ref):
        o_ref[0] = (g_ref[...] + r_ref[0]).astype(BF16)

    return pl.pallas_call(
        body,
        grid_spec=pltpu.PrefetchScalarGridSpec(num_scalar_prefetch=1, grid=(3, nr, nn), in_specs=[g_spec, r_spec],
                                               out_specs=o_spec),
        out_shape=SDS((3, hr, n), BF16),
        compiler_params=_params(("parallel", "parallel", "parallel")), name=name)(pos, grad, got)


def _chip_add(grad, got, got16, kind, shard_shape, pos, name, slab=0, slabs=1, prev=None):
    r, n = shard_shape
    hr = r // 2
    tr, tn = _tile(hr, 256, 16), _tile(n, 1408)
    nr, nn = hr // tr, n // tn
    g_spec = _grad_half_spec(kind, tr, tn, nr, nn, lambda i, j, pos_: pos_[0])
    r_spec = pl.BlockSpec((1, tr, tn), lambda i, j, pos_: (pos_[0], i, j))
    oth = pl.BlockSpec((3, tr, tn), lambda i, j, pos_: (0, i, j))

    def body(pos_ref, g_ref, r_ref, oth_ref, *rest):
        own = g_ref[...] + r_ref[0]
        rest[-1][0, 0] = ((own + oth_ref[0].astype(F32)) + oth_ref[1].astype(F32)) + oth_ref[2].astype(F32)

    return pl.pallas_call(
        body,
        grid_spec=pltpu.PrefetchScalarGridSpec(
            num_scalar_prefetch=1, grid=(nr, nn), in_specs=[g_spec, r_spec, oth] + [ANY] * (prev is not None),
            out_specs=pl.BlockSpec((1, 1, tr, tn), lambda i, j, pos_: (slab, pos_[1], i, j))),
        out_shape=SDS((slabs, 2, hr, n), F32), input_output_aliases={4: 0} if prev is not None else {},
        compiler_params=_params(("parallel", "parallel")),
        name=name)(pos, grad, got, got16, *(() if prev is None else (prev,)))


def _adamw_math(w, g, m, v):
    m = ADAM_B1 * m + (1.0 - ADAM_B1) * g
    v = ADAM_B2 * v + (1.0 - ADAM_B2) * (g * g)
    m_hat = m / (1.0 - ADAM_B1 ** ADAM_STEP)
    v_hat = v / (1.0 - ADAM_B2 ** ADAM_STEP)
    return -ADAM_LR * (m_hat / (jnp.sqrt(v_hat) + ADAM_EPS) + ADAM_WD * w), m, v


def _adamw(w, g, m, v, name):
    r, n = w.shape
    tr, tn = _tile(r, 256, 16), _tile(n, 1408)

    def body(w_ref, g_ref, m_ref, v_ref, d_ref, nm_ref, nv_ref, go_ref):
        gv = g_ref[...]
        d_ref[...], nm_ref[...], nv_ref[...] = _adamw_math(w_ref[...], gv, m_ref[...], v_ref[...])
        go_ref[...] = gv

    tile = pl.BlockSpec((tr, tn), lambda i, j: (i, j))
    return pl.pallas_call(
        body, grid=(r // tr, n // tn), in_specs=[tile] * 4, out_specs=[tile] * 4, out_shape=[SDS((r, n), F32)] * 4,
        compiler_params=_params(("parallel", "parallel")), name=name)(w, g, m, v)


def _small_allreduce_adamw(g, w, m, v, deps=()):
    length = g.shape[1]

    def body(*refs):
        g_ref, w_ref, m_ref, v_ref = refs[:4]
        gs_ref, d_ref, nm_ref, nv_ref, buf, send, recv = refs[4 + len(deps):]
        x, y, c = lax.axis_index("x"), lax.axis_index("y"), lax.axis_index("c")
        me = 4 * x + 2 * y + c
        buf[me] = g_ref[...]
        cps = []
        for d in range(1, 8):
            peer = (x ^ (d >> 2), y ^ ((d >> 1) & 1), c ^ (d & 1))
            cp = pltpu.make_async_remote_copy(src_ref=buf.at[me], dst_ref=buf.at[me], send_sem=send.at[d - 1],
                                              recv_sem=recv.at[d - 1], device_id=peer, device_id_type=MESH)
            cp.start()
            cps.append(cp)
        for cp in cps:
            cp.wait()
        total = buf[0]
        for d in range(1, 8):
            total = total + buf[d]
        gs_ref[...] = total
        d_ref[...], nm_ref[...], nv_ref[...] = _adamw_math(w_ref[...], total, m_ref[...], v_ref[...])

    vm = pl.BlockSpec(memory_space=pltpu.VMEM)
    return pl.pallas_call(
        body, in_specs=[vm] * 4 + [ANY] * len(deps), out_specs=[vm] * 4, out_shape=[SDS((1, length), F32)] * 4,
        scratch_shapes=[pltpu.VMEM((8, 1, length), F32), pltpu.SemaphoreType.DMA((7,)), pltpu.SemaphoreType.DMA((7,))],
        compiler_params=pltpu.CompilerParams(has_side_effects=True), name="small_allreduce_adamw")(g, w, m, v, *deps)


def kernel(x, w_in, b_gate, norm_mix, norm_ffn, hgrn_lb_logits, hgrn_out_gain, q_gain, k_gain, rel_bias, w_proj_a, w_proj_b, w_out, w_ffn_in, w_ffn_out, loss_target, m_w_in, m_b_gate, m_norm_mix, m_norm_ffn, m_hgrn_lb_logits, m_hgrn_out_gain, m_q_gain, m_k_gain, m_rel_bias, m_w_proj_a, m_w_proj_b, m_w_out, m_w_ffn_in, m_w_ffn_out, v_w_in, v_b_gate, v_norm_mix, v_norm_ffn, v_hgrn_lb_logits, v_hgrn_out_gain, v_q_gain, v_k_gain, v_rel_bias, v_w_proj_a, v_w_proj_b, v_w_out, v_w_ffn_in, v_w_ffn_out):
    t, d = x.shape[1], x.shape[2]
    d_a = hgrn_out_gain.shape[1]
    h_a = d_a // HEAD
    h_b = rel_bias.shape[1]
    d_b = h_b * HEAD
    x0 = x.reshape(t, d)
    target = loss_target.reshape(t, d)
    ax, ay = lax.axis_index("x"), lax.axis_index("y")
    pos = jnp.stack([2 * ax + ay, lax.axis_index("c"), 2 * (1 - ax) + ay, 2 * ax + 1 - ay,
                     2 * (1 - ax) + 1 - ay]).astype(jnp.int32)

    names = ["w_in", "w_proj_a", "w_proj_b", "w_out", "w_ffn_in", "w_ffn_out"]
    big = dict(zip(names, [w_in[0], w_proj_a[0], w_proj_b[0], w_out[0], w_ffn_in[0], w_ffn_out[0]]))
    big_m = dict(zip(names, [m_w_in[0], m_w_proj_a[0], m_w_proj_b[0], m_w_out[0], m_w_ffn_in[0], m_w_ffn_out[0]]))
    big_v = dict(zip(names, [v_w_in[0], v_w_proj_a[0], v_w_proj_b[0], v_w_out[0], v_w_ffn_in[0], v_w_ffn_out[0]]))
    kind = dict(zip(names, ["col", "col", "col", "row", "col", "row"]))
    shape = {nm: big[nm].shape for nm in names}

    def gather_start(tag, group, deps):
        plan, n = _plan_gather_ici([kind[g] for g in group], [shape[g] for g in group])
        fulls = [_cast_into_full(big[g], kind[g], pos, "cast_" + g) for g in group]
        send, recv, bufs, token = _start_copies("gather_ici_start_" + tag, fulls, plan, n, deps)
        return (tag, group, plan, send, recv, bufs), token

    def gather_pass(state, after, rels=ALL_RELATIONS, part=""):
        tag, group, _, send, recv, bufs = state
        kinds_, shapes_ = [kind[g] for g in group], [shape[g] for g in group]
        bufs = _wait_copies("gather_ici_wait_" + tag + part, bufs, send, recv,
                            _plan_gather_ici(kinds_, shapes_, rels)[0], after)
        plan, n = _plan_gather_pass(kinds_, shapes_, rels)
        send2, recv2, bufs, token = _start_copies("gather_pass_start_" + tag + part, bufs, plan, n, ())
        return (tag + part, group, plan, send2, recv2, bufs), token

    def gather_done(state, after):
        tag, group, plan, send, recv, bufs = state
        return _wait_copies("gather_pass_wait_" + tag, bufs, send, recv, plan, after)

    def reduce_start(tag, group, grads, deps):
        plan, n = _plan_pair([kind[g] for g in group], [shape[g] for g in group])
        lands = [lax.empty((4, shape[g][0] // 2, shape[g][1]), F32) for g in group]
        send, recv, bufs, token = _start_copies("pair_start_" + tag, list(grads) + lands, plan, n, deps)
        return dict(tag=tag, group=group, plan=plan, send=send, recv=recv, bufs=bufs), token

    def reduce_pair_done(st, after):
        tag, group, nw = st["tag"], st["group"], len(st["group"])
        bufs = _wait_copies("pair_wait_" + tag, st["bufs"], st["send"], st["recv"], st["plan"], after)
        grads, gots = bufs[:nw], bufs[nw:]
        parts = [_pair_add(g, l, kind[nm], shape[nm], pos, "pair_add_" + nm) for g, l, nm in zip(grads, gots, group)]
        lands = [lax.empty((3, shape[g][0] // 2, shape[g][1]), BF16) for g in group]
        plan, n = _plan_chip(nw)
        send, recv, bufs, token = _start_copies("chip_start_" + tag, parts + lands, plan, n, ())
        return dict(st, plan=plan, send=send, recv=recv, bufs=bufs, grads=grads, gots=gots), token

    def reduce_chip_wait(st, after, slab=0, slabs=1, prev=None):
        tag, group, nw = st["tag"], st["group"], len(st["group"])
        bufs = _wait_copies("chip_wait_" + tag, st["bufs"], st["send"], st["recv"], st["plan"], after)
        return [_chip_add(g, l, got16, kind[nm], shape[nm], pos, "chip_add_" + nm, slab, slabs, prev)
                for g, l, got16, nm in zip(st["grads"], st["gots"], bufs[nw:], group)]

    def reduce_share(tag, group, finals, slabs=1):
        plan, n = _plan_share([slabs] * len(finals))
        send, recv, bufs, token = _start_copies("share_start_" + tag, finals, plan, n, ())
        return dict(tag=tag, group=group, plan=plan, send=send, recv=recv, bufs=bufs), token

    def reduce_chip_done(st, after):
        return reduce_share(st["tag"], st["group"], reduce_chip_wait(st, after))

    g_big, upd = {}, {}

    def reduce_finish(st, after):
        bufs = _wait_copies("share_wait_" + st["tag"], st["bufs"], st["send"], st["recv"], st["plan"], after)
        for full, nm in zip(bufs, st["group"]):
            upd[nm] = _adamw(big[nm], full.reshape(shape[nm]), big_m[nm], big_v[nm], "adamw_" + nm)
            g_big[nm] = upd[nm][3]

    ga, token = gather_start("a", ["w_in"], ())
    gb, token = gather_start("b", ["w_proj_a", "w_proj_b", "w_out"], (token,))
    gc, token = gather_start("c", ["w_ffn_in"], (token,))
    gd, token = gather_start("d", ["w_ffn_out"], (token,))
    h1, r1 = _rmsnorm_fwd(x0, norm_mix, "rmsnorm_mix")
    rb = jnp.pad(rel_bias[0], ((0, 0), (0, REL_LANES - N_REL)))
    bias = _relbias_expand(rb).transpose(1, 0, 2)
    proj = _matmul_chunks(h1, big["w_in"], (0,), None, pos, "proj_in_own", own_shard=True)
    ici_a = ga
    ga, token = gather_pass(ici_a, (h1, bias, proj, token), rels=(0, 1), part="_near")
    (wg_in,) = gather_done(ga, ())
    proj = _matmul_chunks(h1, wg_in, (2, 3), proj, pos, "proj_in_near")
    ga, token = gather_pass(ici_a[:5] + ([wg_in],), (proj,), rels=(2,), part="_far")
    (wg_in,) = gather_done(ga, ())
    proj = _matmul_chunks(h1, wg_in, (4,), proj, pos, "proj_in_far")
    y_a, o_pre, states = _hgrn_fwd(proj, hgrn_lb_logits, hgrn_out_gain, h_a)
    gb, token = gather_pass(gb, (y_a,))
    col_b = 4 * d_a // HEAD
    y_b = _attn_fwd(proj, q_gain, k_gain, bias, h_b, col_b)
    wg_pa, wg_pb, wg_out = gather_done(gb, (y_b,))
    pa = _matmul(y_a, wg_pa, name="proj_a", deps=(token,))
    pb = _matmul(y_b, wg_pb, name="proj_b")
    gate_off = 4 * d_a + 3 * d_b
    merged = _merge_fwd(proj, b_gate, pa, pb, gate_off)
    x2 = _matmul(merged, wg_out, res=x0, name="out_proj")
    gc, token = gather_pass(gc, (x2,))
    h2, r2 = _rmsnorm_fwd(x2, norm_ffn, "rmsnorm_ffn")
    (wg_fin,) = gather_done(gc, (h2,))
    ff_gate, ff_up, act = _ffn_in_swiglu(h2, wg_fin, deps=(token,))
    gd, token = gather_pass(gd, (act,))
    (wg_fout,) = gather_done(gd, ())
    dy, dy16, loss_part = _ffn_out_loss(act, wg_fout, x2, target)

    g_fout = _matmul(act, dy16, ta=True, name="dw_ffn_out")
    r_fout, token = reduce_start("fout", ["w_ffn_out"], [g_fout], ())
    dact = _matmul(dy16, wg_fout, tb=True, name="d_act", deps=(token,))
    r_fout, token = reduce_pair_done(r_fout, (dact,))
    dgu = _swiglu_bwd(dact, ff_gate, ff_up)
    g_fin = _matmul(h2, dgu, ta=True, name="dw_ffn_in", deps=(token,))
    r_fin, token = reduce_start("fin", ["w_ffn_in"], [g_fin], ())
    dh2 = _matmul(dgu, wg_fin, tb=True, name="d_h2", deps=(token,))
    r_fout, token_a = reduce_chip_done(r_fout, (dh2,))
    r_fin, token_b = reduce_pair_done(r_fin, (dh2,))
    dx2, dx2_16, g_norm_ffn = _rmsnorm_bwd(dh2, x2, r2, norm_ffn, dy, "rmsnorm_ffn_bwd", deps=(token_a, token_b))
    dmerged = _matmul(dx2_16, wg_out, tb=True, name="d_merged")
    dp_ab, dproj, g_bgate = _merge_bwd(dmerged, proj, b_gate, pa, pb, gate_off)
    g_out = _matmul(merged, dx2_16, ta=True, name="dw_out")
    g_pa = _matmul(y_a, dp_ab[0], ta=True, name="dw_proj_a")
    g_pb = _matmul(y_b, dp_ab[1], ta=True, name="dw_proj_b")
    r_mid, token = reduce_start("mid", ["w_proj_a", "w_proj_b", "w_out"], [g_pa, g_pb, g_out], ())
    dy_a = _matmul(dp_ab[0], wg_pa, tb=True, name="d_y_a", deps=(token,))
    dy_b = _matmul(dp_ab[1], wg_pb, tb=True, name="d_y_b")
    r_fin, token_a = reduce_chip_done(r_fin, (dy_b,))
    r_mid, token_b = reduce_pair_done(r_mid, (dy_b,))
    dproj, dbias, g_qg, g_kg = _attn_bwd(dproj, proj, q_gain, k_gain, bias, dy_b, h_b, col_b, deps=(token_a, token_b))
    r_mid, token = reduce_chip_done(r_mid, (dbias,))
    dproj, g_lb, g_gain = _hgrn_bwd(dproj, proj, o_pre, states, dy_a, hgrn_lb_logits, hgrn_out_gain, h_a, deps=(token,))
    half_d = d // 2
    for slab, rows in (("w_in_top", slice(0, half_d)), ("w_in_bot", slice(half_d, d))):
        kind[slab], shape[slab] = "col", (half_d, shape["w_in"][1])
    g_top = _matmul(h1[:, :half_d], dproj, ta=True, name="dw_in_top")
    r_top, token = reduce_start("in_top", ["w_in_top"], [g_top], ())
    g_bot = _matmul(h1[:, half_d:], dproj, ta=True, name="dw_in_bot", deps=(token,))
    r_bot, token = reduce_start("in_bot", ["w_in_bot"], [g_bot], ())
    r_top, token_a = reduce_pair_done(r_top, (g_bot, token))
    g_rb = _relbias_reduce(dbias.transpose(1, 0, 2))[:, :N_REL]
    reduce_finish(r_mid, (token_a,))
    r_bot, token = reduce_pair_done(r_bot, (g_rb, upd["w_out"][0], upd["w_proj_a"][0], upd["w_proj_b"][0]))
    dh1 = _matmul(dproj, wg_in, tb=True, name="d_h1", deps=(token,))
    dx, _, g_norm_mix = _rmsnorm_bwd(dh1, x0, r1, norm_mix, dx2, "rmsnorm_mix_bwd")
    reduce_finish(r_fin, (dx,))
    reduce_finish(r_fout, (dx,))
    (final_in,) = reduce_chip_wait(r_top, (upd["w_ffn_in"][0], upd["w_ffn_out"][0]), 0, 2)
    (final_in,) = reduce_chip_wait(r_bot, (final_in,), 1, 2, final_in)
    r_in, token = reduce_share("in", ["w_in"], [final_in], 2)

    small_w = [b_gate, norm_mix, norm_ffn, hgrn_lb_logits, hgrn_out_gain, q_gain, k_gain, rel_bias]
    small_m = [m_b_gate, m_norm_mix, m_norm_ffn, m_hgrn_lb_logits, m_hgrn_out_gain, m_q_gain, m_k_gain, m_rel_bias]
    small_v = [v_b_gate, v_norm_mix, v_norm_ffn, v_hgrn_lb_logits, v_hgrn_out_gain, v_q_gain, v_k_gain, v_rel_bias]
    small_g = [g_bgate, g_norm_mix, g_norm_ffn, g_lb, g_gain, g_qg, g_kg, g_rb]
    sizes = [w.size for w in small_w]
    length = -(-(sum(sizes) + 1) // 128) * 128

    def pack(parts_):
        flat = jnp.concatenate([p.reshape(1, -1) for p in parts_], axis=1)
        return jnp.pad(flat, ((0, 0), (0, length - flat.shape[1])))

    one = jnp.ones((1, 1), F32)
    packed = _small_allreduce_adamw(pack(small_g + [loss_part]), pack(small_w + [one]), pack(small_m + [one]),
                                    pack(small_v + [one]), deps=(token,))

    def unpack(vec):
        out, at = [], 0
        for w, n in zip(small_w, sizes):
            out.append(vec[0, at:at + n].reshape(w.shape))
            at += n
        return out, vec[0, at]

    (sg, loss), (sd, _), (sm, _), (sv, _) = [unpack(p) for p in packed]
    reduce_finish(r_in, (packed[0],))

    def ordered(small, bigs):
        bigs = [bigs[nm][None] for nm in names]
        return [bigs[0]] + small + bigs[1:]

    return (loss, dx.reshape(x.shape), *ordered(sg, g_big), *ordered(sd, {nm: upd[nm][0] for nm in names}),
            *ordered(sm, {nm: upd[nm][1] for nm in names}), *ordered(sv, {nm: upd[nm][2] for nm in names}))
```

```python
import functools

import jax
import jax.numpy as jnp
from jax import lax
from jax.experimental import pallas as pl
from jax.experimental.pallas import tpu as pltpu

F32 = jnp.float32
BF16 = jnp.bfloat16
SDS = jax.ShapeDtypeStruct
MESH = pl.DeviceIdType.MESH
HIGHEST = lax.Precision.HIGHEST

CHUNK = 64
SUB = 16
HEAD = 128
N_PAST = 8
BAND = (N_PAST + 1) * CHUNK
PAD = N_PAST * CHUNK
REL_FUTURE = CHUNK - 1
REL_PAST = 2 * CHUNK - 1
N_REL = REL_FUTURE + REL_PAST + 1
REL_LANES = 256
EPS = 1e-6
MIX_HEADS = 2
MIX_UNROLL = 4
MIX_UNROLL_BWD = 4
ATT_UNROLL = 8
ATT_UNROLL_BWD = 4
EXP_CLAMP = 80.0

ADAM_LR = 0.001
ADAM_B1 = 0.9
ADAM_B2 = 0.999
ADAM_EPS = 1e-08
ADAM_WD = 0.01
ADAM_STEP = 10

VMEM_LIMIT = 56 * 1024 * 1024

HBM = pl.BlockSpec(memory_space=pltpu.HBM)
ANY = pl.BlockSpec(memory_space=pl.ANY)
SEM = pl.BlockSpec(memory_space=pltpu.SEMAPHORE)

NT = (((1,), (1,)), ((), ()))
TN = (((0,), (0,)), ((), ()))
NN = (((1,), (0,)), ((), ()))


def _params(sem=None, **kw):
    return pltpu.CompilerParams(dimension_semantics=sem, vmem_limit_bytes=VMEM_LIMIT, **kw)


def _tile(n, pref, unit=128):
    if n <= pref:
        return n
    t = pref - pref % unit
    while n % t:
        t -= unit
    return t


def _loop(n, unroll, step, init):
    assert n % unroll == 0, (n, unroll)

    def several(i, carry):
        for u in range(unroll):
            carry = step(i * unroll + u, carry)
        return carry

    return lax.fori_loop(0, n // unroll, several, init)


def _sigmoid(x):
    return 1.0 / (1.0 + jnp.exp(-x))


def _dsilu(x, s):
    return s * (1.0 + x * (1.0 - s))


def _bdot(a, b, dims=NN):
    return lax.dot_general(a.astype(BF16), b.astype(BF16), dims, preferred_element_type=F32)


def _split(a):
    hi = a.astype(BF16)
    return hi, (a - hi.astype(F32)).astype(BF16)


def _dot3(a, b, dims):
    dot = lambda u, v: lax.dot_general(u, v, dims, preferred_element_type=F32)
    return dot(a[0], b[1]) + dot(a[1], b[0]) + dot(a[0], b[0])


def _fdot(a, b):
    return lax.dot_general(a, b, NN, precision=HIGHEST, preferred_element_type=F32)


MM_TILE_K = 5632
MM_TILE_N = 512


def _matmul_chunks(h, w, which, prev, pos, name, own_shard=False, deps=()):
    t, d = h.shape
    nc_ = w.shape[1] if own_shard else w.shape[1] // 4
    tm, tn = _tile(t, 1024), _tile(nc_, 1408)
    nn = nc_ // tn

    def chunk(q, p):
        sel = p[which[0]]
        for i in range(1, len(which)):
            sel = jnp.where(q == i, p[which[i]], sel)
        return sel

    def body(p_ref, h_ref, w_ref, *rest):
        rest[-1][...] = jnp.dot(h_ref[...], w_ref[...].astype(BF16), preferred_element_type=F32)

    if own_shard:
        w_spec = pl.BlockSpec((d, tn), lambda q, i, j, p: (0, j))
    else:
        w_spec = pl.BlockSpec((d, tn), lambda q, i, j, p: (0, chunk(q, p) * nn + j))
    n_extra = len(deps) + (prev is not None)
    return pl.pallas_call(
        body,
        grid_spec=pltpu.PrefetchScalarGridSpec(
            num_scalar_prefetch=1, grid=(len(which), t // tm, nn),
            in_specs=[pl.BlockSpec((tm, d), lambda q, i, j, p: (i, 0)), w_spec] + [ANY] * n_extra,
            out_specs=pl.BlockSpec((tm, tn), lambda q, i, j, p: (i, chunk(q, p) * nn + j))),
        out_shape=SDS((t, 4 * nc_), F32), input_output_aliases={3 + len(deps): 0} if prev is not None else {},
        compiler_params=_params(("arbitrary", "arbitrary", "arbitrary")),
        name=name)(pos, h, w, *deps, *(() if prev is None else (prev,)))


def _matmul(a, b, *, ta=False, tb=False, res=None, out_dtype=F32, name, deps=()):
    m, k = (a.shape[1], a.shape[0]) if ta else a.shape
    n = b.shape[0] if tb else b.shape[1]
    if k > MM_TILE_K:
        tk, tm, tn = _tile(k, MM_TILE_K // 2), _tile(m, 1024), _tile(n, 1024)
    else:
        tk = k
        tm, tn = _tile(m, 2048 if tk <= MM_TILE_K // 2 else 1024), _tile(n, MM_TILE_N)
    nk = k // tk
    dims = ((((0,) if ta else (1,)), ((1,) if tb else (0,))), ((), ()))

    def body(*refs):
        n_in = 2 + (res is not None)
        a_ref, b_ref = refs[:2]
        r_ref = refs[2] if res is not None else None
        o_ref = refs[n_in + len(deps)]
        part = lax.dot_general(a_ref[...].astype(BF16), b_ref[...].astype(BF16), dims, preferred_element_type=F32)

        def finish(out):
            if r_ref is not None:
                out = out + r_ref[...]
            o_ref[...] = out.astype(o_ref.dtype)

        if nk == 1:
            finish(part)
            return
        acc_ref = refs[-1]
        kk = pl.program_id(2)

        @pl.when(kk == 0)
        def _():
            acc_ref[...] = part

        @pl.when(jnp.logical_and(kk > 0, kk < nk - 1))
        def _():
            acc_ref[...] += part

        @pl.when(kk == nk - 1)
        def _():
            finish(acc_ref[...] + part)

    a_spec = pl.BlockSpec((tk, tm), lambda i, j, l: (l, i)) if ta else pl.BlockSpec((tm, tk), lambda i, j, l: (i, l))
    b_spec = pl.BlockSpec((tn, tk), lambda i, j, l: (j, l)) if tb else pl.BlockSpec((tk, tn), lambda i, j, l: (l, j))
    o_spec = pl.BlockSpec((tm, tn), lambda i, j, l: (i, j))
    in_specs = [a_spec, b_spec] + ([o_spec] if res is not None else []) + [ANY] * len(deps)
    args = (a, b) + ((res,) if res is not None else ()) + tuple(deps)
    return pl.pallas_call(
        body, grid=(m // tm, n // tn, nk), in_specs=in_specs, out_specs=o_spec,
        out_shape=SDS((m, n), out_dtype), scratch_shapes=[pltpu.VMEM((tm, tn), F32)] if nk > 1 else [],
        compiler_params=_params(("parallel", "parallel", "arbitrary")), name=name)(*args)


def _cast_into_full(w, kind, pos, name):
    r, n = w.shape
    tr = _tile(r, 512, 16)
    nr = r // tr
    if kind == "col":
        shape, o_spec = (r, 4 * n), pl.BlockSpec((tr, n), lambda i, p: (i, p[0]))
    else:
        shape, o_spec = (4 * r, n), pl.BlockSpec((tr, n), lambda i, p: (p[0] * nr + i, 0))

    def body(p_ref, w_ref, o_ref):
        o_ref[...] = w_ref[...].astype(BF16)

    return pl.pallas_call(
        body,
        grid_spec=pltpu.PrefetchScalarGridSpec(num_scalar_prefetch=1, grid=(nr,),
                                               in_specs=[pl.BlockSpec((tr, n), lambda i, p: (i, 0))], out_specs=o_spec),
        out_shape=SDS(shape, BF16), compiler_params=_params(("parallel",)), name=name)(pos, w)


def _rmsnorm_fwd(x, gain, name):
    t, d = x.shape
    tm = _tile(t, 256)

    def body(x_ref, g_ref, h_ref, r_ref):
        xv = x_ref[...]
        r = lax.rsqrt(jnp.mean(xv * xv, axis=-1, keepdims=True) + EPS)
        h_ref[...] = (xv * r * g_ref[...]).astype(BF16)
        r_ref[...] = r

    return pl.pallas_call(
        body, grid=(t // tm,),
        in_specs=[pl.BlockSpec((tm, d), lambda i: (i, 0)), pl.BlockSpec((1, d), lambda i: (0, 0))],
        out_specs=[pl.BlockSpec((tm, d), lambda i: (i, 0)), pl.BlockSpec((tm, 1), lambda i: (i, 0))],
        out_shape=[SDS((t, d), BF16), SDS((t, 1), F32)], compiler_params=_params(("parallel",)), name=name)(x, gain)


def _rmsnorm_bwd(dh, x, r, gain, dres, name, deps=()):
    t, d = x.shape
    tm = _tile(t, 256)

    def body(dh_ref, x_ref, r_ref, g_ref, dres_ref, *rest):
        dx_ref, dxb_ref, dg_ref = rest[len(deps):]

        @pl.when(pl.program_id(0) == 0)
        def _():
            dg_ref[...] = jnp.zeros_like(dg_ref)

        dhv, xv, rv = dh_ref[...], x_ref[...], r_ref[...]
        dg_ref[...] += jnp.sum(dhv * (xv * rv), axis=0, keepdims=True)
        u = dhv * g_ref[...]
        dx = dres_ref[...] + rv * u - xv * (rv * rv * rv) * jnp.mean(u * xv, axis=-1, keepdims=True)
        dx_ref[...] = dx
        dxb_ref[...] = dx.astype(BF16)

    row = pl.BlockSpec((tm, d), lambda i: (i, 0))
    vec = pl.BlockSpec((1, d), lambda i: (0, 0))
    return pl.pallas_call(
        body, grid=(t // tm,),
        in_specs=[row, row, pl.BlockSpec((tm, 1), lambda i: (i, 0)), vec, row] + [ANY] * len(deps),
        out_specs=[row, row, vec], out_shape=[SDS((t, d), F32), SDS((t, d), BF16), SDS((1, d), F32)],
        compiler_params=_params(("arbitrary",)), name=name)(dh, x, r, gain, dres, *deps)


def _merge_fwd(proj, b_gate, pa, pb, off):
    t, d = pa.shape
    tm, tc = _tile(t, 512), _tile(d, 512)
    nj = d // tc
    oa, ob = off // tc, off // tc + nj

    def body(la_ref, lb_ref, ba_ref, bb_ref, pa_ref, pb_ref, o_ref):
        ga = _sigmoid(la_ref[...] + ba_ref[...])
        gb = _sigmoid(lb_ref[...] + bb_ref[...])
        o_ref[...] = (ga * pa_ref[...] + gb * pb_ref[...]).astype(BF16)

    tile = pl.BlockSpec((tm, tc), lambda i, j: (i, j))
    return pl.pallas_call(
        body, grid=(t // tm, nj),
        in_specs=[pl.BlockSpec((tm, tc), lambda i, j: (i, oa + j)), pl.BlockSpec((tm, tc), lambda i, j: (i, ob + j)),
                  pl.BlockSpec((1, tc), lambda i, j: (0, j)), pl.BlockSpec((1, tc), lambda i, j: (0, nj + j)), tile, tile],
        out_specs=tile, out_shape=SDS((t, d), BF16), compiler_params=_params(("parallel", "parallel")),
        name="merge_fwd")(proj, proj, b_gate, b_gate, pa, pb)


def _merge_bwd(dmerged, proj, b_gate, pa, pb, off):
    t, d = pa.shape
    tm, tc = _tile(t, 512), _tile(d, 512)
    nj, ni = d // tc, t // tm
    o0 = off // tc

    def body(dm_ref, l_ref, b_ref, pa_ref, pb_ref, dp_ref, dl_ref, db_ref):
        s, i = pl.program_id(0), pl.program_id(2)
        p = jnp.where(s == 0, pa_ref[...], pb_ref[...])
        g = _sigmoid(l_ref[...] + b_ref[...])
        dm = dm_ref[...]
        dp_ref[0] = (dm * g).astype(BF16)
        dl = dm * p * g * (1.0 - g)
        dl_ref[...] = dl.astype(BF16)

        @pl.when(i == 0)
        def _():
            db_ref[...] = jnp.zeros_like(db_ref)

        db_ref[...] += jnp.sum(dl, axis=0, keepdims=True)

    tile = pl.BlockSpec((tm, tc), lambda s, j, i: (i, j))
    return pl.pallas_call(
        body, grid=(2, nj, ni),
        in_specs=[tile, pl.BlockSpec((tm, tc), lambda s, j, i: (i, o0 + s * nj + j)),
                  pl.BlockSpec((1, tc), lambda s, j, i: (0, s * nj + j)), tile, tile],
        out_specs=[pl.BlockSpec((1, tm, tc), lambda s, j, i: (s, i, j)),
                   pl.BlockSpec((tm, tc), lambda s, j, i: (i, o0 + s * nj + j)),
                   pl.BlockSpec((1, tc), lambda s, j, i: (0, s * nj + j))],
        out_shape=[SDS((2, t, d), BF16), SDS(proj.shape, BF16), SDS((1, 2 * d), F32)],
        compiler_params=_params(("arbitrary", "arbitrary", "arbitrary")),
        name="merge_bwd")(dmerged, proj, b_gate, pa, pb)


def _ffn_in_swiglu(h, w, deps=()):
    t, d = h.shape
    f = w.shape[1] // 2
    tm, tn = _tile(t, 2048), _tile(f, MM_TILE_N)
    nj = f // tn

    def body(h_ref, wg_ref, wu_ref, *rest):
        g_ref, u_ref, a_ref = rest[len(deps):]
        hv = h_ref[...]
        g = jnp.dot(hv, wg_ref[...], preferred_element_type=F32)
        u = jnp.dot(hv, wu_ref[...], preferred_element_type=F32)
        g_ref[...] = g
        u_ref[...] = u
        a_ref[...] = (g * _sigmoid(g) * u).astype(BF16)

    tile = pl.BlockSpec((tm, tn), lambda i, j: (i, j))
    return pl.pallas_call(
        body, grid=(t // tm, nj),
        in_specs=[pl.BlockSpec((tm, d), lambda i, j: (i, 0)), pl.BlockSpec((d, tn), lambda i, j: (0, j)),
                  pl.BlockSpec((d, tn), lambda i, j: (0, nj + j))] + [ANY] * len(deps),
        out_specs=[tile, tile, tile], out_shape=[SDS((t, f), F32), SDS((t, f), F32), SDS((t, f), BF16)],
        compiler_params=_params(("parallel", "parallel")), name="ffn_in_swiglu")(h, w, w, *deps)


def _swiglu_bwd(dact, gate, up):
    t, f = gate.shape
    tm = _tile(t, 128)

    def body(d_ref, g_ref, u_ref, o_ref):
        g, dv = g_ref[...], d_ref[...]
        sg = _sigmoid(g)
        o_ref[:, :f] = (dv * u_ref[...] * _dsilu(g, sg)).astype(BF16)
        o_ref[:, f:] = (dv * (g * sg)).astype(BF16)

    row = pl.BlockSpec((tm, f), lambda i: (i, 0))
    return pl.pallas_call(
        body, grid=(t // tm,), in_specs=[row, row, row],
        out_specs=pl.BlockSpec((tm, 2 * f), lambda i: (i, 0)), out_shape=SDS((t, 2 * f), BF16),
        compiler_params=_params(("parallel",)), name="swiglu_bwd")(dact, gate, up)


def _ffn_out_loss(act, w, x_res, target):
    t, d = x_res.shape
    k = act.shape[1]
    tm, tn = _tile(t, 1024), _tile(d, MM_TILE_N)

    def body(a_ref, w_ref, r_ref, t_ref, dy_ref, dyb_ref, l_ref):
        @pl.when(jnp.logical_and(pl.program_id(0) == 0, pl.program_id(1) == 0))
        def _():
            l_ref[...] = jnp.zeros_like(l_ref)

        y = jnp.dot(a_ref[...], w_ref[...], preferred_element_type=F32) + r_ref[...]
        e = y - t_ref[...]
        dy = e * (1.0 / d)
        dy_ref[...] = dy
        dyb_ref[...] = dy.astype(BF16)
        l_ref[...] += (0.5 / d) * jnp.sum(jnp.sum(e * e, axis=-1, keepdims=True), axis=0, keepdims=True)

    tile = pl.BlockSpec((tm, tn), lambda i, j: (i, j))
    return pl.pallas_call(
        body, grid=(t // tm, d // tn),
        in_specs=[pl.BlockSpec((tm, k), lambda i, j: (i, 0)), pl.BlockSpec((k, tn), lambda i, j: (0, j)), tile, tile],
        out_specs=[tile, tile, pl.BlockSpec((1, 1), lambda i, j: (0, 0))],
        out_shape=[SDS((t, d), F32), SDS((t, d), BF16), SDS((1, 1), F32)],
        compiler_params=_params(("arbitrary", "arbitrary")), name="ffn_out_loss")(act, w, x_res, target)


def _rel_onehot(qi):
    p = lax.broadcasted_iota(jnp.int32, (REL_LANES, BAND), 1)
    r = lax.broadcasted_iota(jnp.int32, (REL_LANES, BAND), 0)
    idx = jnp.clip(qi + PAD - p, -REL_FUTURE, REL_PAST) + REL_FUTURE
    return (idx == r).astype(F32)


def _relbias_expand(rb):
    h = rb.shape[0]

    def body(rb_ref, o_ref):
        def step(qi, _):
            o_ref[qi] = _fdot(rb_ref[...], _rel_onehot(qi))
            return 0

        lax.fori_loop(0, CHUNK, step, 0)

    return pl.pallas_call(body, out_shape=SDS((CHUNK, h, BAND), F32), compiler_params=_params(),
                          name="relbias_expand")(rb)


def _relbias_reduce(dbias):
    h = dbias.shape[1]

    rows_per_pass = 4

    def body(db_ref, o_ref):
        def step(i, acc):
            parts = []
            for u in range(rows_per_pass):
                qi = i * rows_per_pass + u
                xv = db_ref[qi]
                hi = xv.astype(BF16)
                rest = xv - hi.astype(F32)
                mid = rest.astype(BF16)
                low = (rest - mid.astype(F32)).astype(BF16)
                parts.append(lax.dot_general(jnp.concatenate([hi, mid, low], axis=0), _rel_onehot(qi).astype(BF16), NT,
                                             preferred_element_type=F32))
            for part in parts:
                acc = acc + (part[0:h] + part[h:2 * h] + part[2 * h:3 * h])
            return acc

        o_ref[...] = lax.fori_loop(0, CHUNK // rows_per_pass, step, jnp.zeros((h, REL_LANES), F32))

    return pl.pallas_call(body, out_shape=SDS((h, REL_LANES), F32), compiler_params=_params(),
                          name="relbias_reduce")(dbias)


def _lower_bound(l_ref):
    l0, l1 = l_ref[0:1, :], l_ref[1:2, :]
    m = jnp.maximum(l0, l1)
    e0, e1 = jnp.exp(l0 - m), jnp.exp(l1 - m)
    return e0 / (e0 + e1)


def _tri(lower):
    r = lax.broadcasted_iota(jnp.int32, (CHUNK, CHUNK), 0)
    c = lax.broadcasted_iota(jnp.int32, (CHUNK, CHUNK), 1)
    return r >= c if lower else r <= c


def _hgrn_intra(qs, kk, b_s):
    rows = lax.broadcasted_iota(jnp.int32, (CHUNK, HEAD), 0)
    b = b_s[...]
    out = []
    for i in range(CHUNK // SUB):
        lo = i * SUB
        ref = jnp.zeros((1, HEAD), F32) if i == 0 else b_s[lo - 1:lo, :]
        eq = jnp.exp(b[lo:lo + SUB] - ref)
        qt = _split(qs[lo:lo + SUB] * eq)
        e = jnp.where(rows < lo + SUB, jnp.exp(jnp.minimum(ref - b, EXP_CLAMP)), 0.0)
        kt = _split(kk * e)
        out.append((eq, qt, e, kt))
    return out


def _hgrn_scores(blocks):
    tr = lax.broadcasted_iota(jnp.int32, (SUB, CHUNK), 0)
    tc = lax.broadcasted_iota(jnp.int32, (SUB, CHUNK), 1)
    return jnp.concatenate([jnp.where(tc <= tr + i * SUB, _dot3(qt, kt, NT), 0.0)
                            for i, (_, qt, _, kt) in enumerate(blocks)], axis=0)


def _hgrn_fwd(proj, lb_logits, gain, n_heads):
    t = proj.shape[0]
    nc = t // CHUNK
    da = n_heads * HEAD
    hp = MIX_HEADS
    wide = hp * HEAD

    def body(q_ref, f_ref, i_ref, g_ref, l_ref, gain_ref, y_ref, o_ref, st_ref, state, b_s):
        state[...] = jnp.zeros_like(state)
        lb_all = _lower_bound(l_ref)
        tril = _tri(True).astype(F32)

        def chunks(i, _):
            dot = functools.partial(lax.dot_general, preferred_element_type=F32)
            items = []
            for u in range(MIX_UNROLL):
                for hh in range(hp):
                    j = i * MIX_UNROLL + u
                    sl = pl.ds(pl.multiple_of(j * CHUNK, CHUNK), CHUNK)
                    cols = slice(hh * HEAD, (hh + 1) * HEAD)
                    lb = lb_all[:, cols]
                    fg = lb + (1.0 - lb) * _sigmoid(f_ref[sl, cols])
                    qv = q_ref[sl, cols]
                    gv = g_ref[sl, cols]
                    items.append(dict(hh=hh, j=j, sl=sl, cols=cols, lf=jnp.log(fg), kk=1.0 - fg, qs=qv * _sigmoid(qv),
                                      vb=i_ref[sl, cols].astype(BF16), gate=gv * _sigmoid(gv)))
            for it in items:
                it["b"] = _fdot(tril, it["lf"])
            for slot, it in enumerate(items):
                b = it["b"]
                b_s[slot] = b
                it["blocks"] = _hgrn_intra(it["qs"], it["kk"], b_s.at[slot])
                it["ebl"] = jnp.exp(b_s[slot, CHUNK - 1:CHUNK, :])
                it["qe"] = (it["qs"] * jnp.exp(b)).astype(BF16)
                it["kd"] = (it["kk"] * jnp.exp(b_s[slot, CHUNK - 1:CHUNK, :] - b)).astype(BF16)
            for it in items:
                it["a"] = _hgrn_scores(it["blocks"]).astype(BF16)
            for it in items:
                it["kv"] = dot(it["vb"], it["kd"], TN)
                it["o"] = dot(it["a"], it["vb"], NN)
            s_now = [state[hh] for hh in range(hp)]
            for it in items:
                it["s_in"] = s_now[it["hh"]]
                s_now[it["hh"]] = it["s_in"] * it["ebl"] + it["kv"]
            for hh in range(hp):
                state[hh] = s_now[hh]
            for it in items:
                it["o"] = it["o"] + dot(it["qe"], it["s_in"].astype(BF16), NT)
            for it in items:
                o, sl, cols = it["o"], it["sl"], it["cols"]
                st_ref[it["hh"], it["j"]] = it["s_in"]
                o_ref[sl, cols] = o
                rr = lax.rsqrt(jnp.mean(o * o, axis=-1, keepdims=True) + EPS)
                y_ref[sl, cols] = (o * rr * gain_ref[:, cols] * it["gate"]).astype(BF16)
            return 0

        assert nc % MIX_UNROLL == 0, (nc, MIX_UNROLL)
        lax.fori_loop(0, nc // MIX_UNROLL, chunks, 0)

    col = lambda k: pl.BlockSpec((t, wide), lambda h: (0, k * (n_heads // hp) + h))
    vec = pl.BlockSpec((1, wide), lambda h: (0, h))
    return pl.pallas_call(
        body, grid=(n_heads // hp,),
        in_specs=[col(0), col(1), col(2), col(3), pl.BlockSpec((2, wide), lambda h: (0, h)), vec],
        out_specs=[pl.BlockSpec((t, wide), lambda h: (0, h)), pl.BlockSpec((t, wide), lambda h: (0, h)),
                   pl.BlockSpec((hp, nc, HEAD, HEAD), lambda h: (h, 0, 0, 0))],
        out_shape=[SDS((t, da), BF16), SDS((t, da), F32), SDS((n_heads, nc, HEAD, HEAD), F32)],
        scratch_shapes=[pltpu.VMEM((hp, HEAD, HEAD), F32), pltpu.VMEM((hp * MIX_UNROLL, CHUNK, HEAD), F32)],
        compiler_params=_params(("parallel",)), name="hgrn_fwd")(proj, proj, proj, proj, lb_logits, gain)


def _write_column_groups(res, dproj_ref, sems, col0, stride, h, width):
    copies = [pltpu.make_async_copy(
        res.at[p], dproj_ref.at[:, pl.ds(pl.multiple_of((col0 + p * stride + h) * width, HEAD), width)], sems.at[p])
        for p in range(res.shape[0])]
    for cp in copies:
        cp.start()
    for cp in copies:
        cp.wait()


def _hgrn_bwd(dproj, proj, o_pre, states, dy, lb_logits, gain, n_heads, deps=()):
    t = proj.shape[0]
    nc = t // CHUNK
    da = n_heads * HEAD
    hp = MIX_HEADS
    wide = hp * HEAD

    def body(*refs):
        (q_ref, f_ref, i_ref, g_ref, o_ref, st_ref, dy_ref, l_ref, gain_ref,
         dproj_ref, dl_ref, dgain_ref, res, dstate, b_s, out_sems) = refs[1 + len(deps):]

        def compute():
            dstate[...] = jnp.zeros_like(dstate)
            lb_all = _lower_bound(l_ref)
            tril_m, tril, triu = _tri(True), _tri(True).astype(F32), _tri(False).astype(F32)
            last = lax.broadcasted_iota(jnp.int32, (CHUNK, HEAD), 0) == CHUNK - 1

            def chunks(i, carry):
                dot = functools.partial(lax.dot_general, preferred_element_type=F32)
                items = []
                for u in range(MIX_UNROLL_BWD):
                    for hh in range(hp):
                        j = nc - 1 - (i * MIX_UNROLL_BWD + u)
                        sl = pl.ds(pl.multiple_of(j * CHUNK, CHUNK), CHUNK)
                        cols = slice(hh * HEAD, (hh + 1) * HEAD)
                        lb, gain_v = lb_all[:, cols], gain_ref[:, cols]
                        sg = _sigmoid(f_ref[sl, cols])
                        fg = lb + (1.0 - lb) * sg
                        qv = q_ref[sl, cols]
                        sq = _sigmoid(qv)
                        gv = g_ref[sl, cols]
                        sgg = _sigmoid(gv)
                        silg = gv * sgg
                        o = o_ref[sl, cols]
                        dyv = dy_ref[sl, cols]
                        rr = lax.rsqrt(jnp.mean(o * o, axis=-1, keepdims=True) + EPS)
                        on = o * rr
                        don = dyv * gain_v * silg
                        do = (rr * don - o * (rr * rr * rr) * jnp.mean(don * o, axis=-1, keepdims=True)).astype(BF16)
                        items.append(dict(
                            hh=hh, j=j, sl=sl, cols=cols, lb=lb, sg=sg, fg=fg, kk=1.0 - fg, qv=qv, sq=sq, qs=qv * sq,
                            vb=i_ref[sl, cols].astype(BF16), do=do, dg=dyv * on * gain_v * _dsilu(gv, sgg),
                            dgain=jnp.sum(dyv * on * silg, axis=0, keepdims=True)))
                for it in items:
                    it["b"] = _fdot(tril, jnp.log(it["fg"]))
                for slot, it in enumerate(items):
                    b = it["b"]
                    b_s[slot] = b
                    it["blocks"] = _hgrn_intra(it["qs"], it["kk"], b_s.at[slot])
                    bl = b_s[slot, CHUNK - 1:CHUNK, :]
                    it["eb"], it["ebl"], it["ekd"] = jnp.exp(b), jnp.exp(bl), jnp.exp(bl - b)
                    it["s_in"] = st_ref[it["hh"], it["j"]]
                for it in items:
                    it["a"] = _hgrn_scores(it["blocks"]).astype(BF16)
                    it["da"] = jnp.where(tril_m, dot(it["do"], it["vb"], NT), 0.0)
                for it in items:
                    dq_rows = []
                    dk = jnp.zeros((CHUNK, HEAD), F32)
                    for blk, (eq, qt, e, kt) in enumerate(it["blocks"]):
                        da_i = _split(it["da"][blk * SUB:(blk + 1) * SUB])
                        dq_rows.append(eq * _dot3(da_i, kt, NN))
                        dk = dk + e * _dot3(da_i, qt, TN)
                    it["dq"] = jnp.concatenate(dq_rows, axis=0) + dot(it["do"], it["s_in"].astype(BF16), NN) * it["eb"]
                    it["dk"] = dk
                    it["dv"] = dot(it["a"], it["do"], TN)
                    it["g"] = dot(it["do"], (it["qs"] * it["eb"]).astype(BF16), TN)
                ds_now = [dstate[hh] for hh in range(hp)]
                for it in items:
                    it["ds_out"] = ds_now[it["hh"]]
                    ds_now[it["hh"]] = it["ds_out"] * it["ebl"] + it["g"]
                for hh in range(hp):
                    dstate[hh] = ds_now[hh]
                for it in items:
                    dsb = it["ds_out"].astype(BF16)
                    it["dv"] = it["dv"] + dot((it["kk"] * it["ekd"]).astype(BF16), dsb, NT)
                    it["dk_state"] = it["ekd"] * dot(it["vb"], dsb, NN)
                for it in items:
                    kk, dk_state = it["kk"], it["dk_state"]
                    it["dk"] = it["dk"] + dk_state
                    extra = (jnp.sum(kk * dk_state, axis=0, keepdims=True)
                             + it["ebl"] * jnp.sum(it["s_in"] * it["ds_out"], axis=0, keepdims=True))
                    it["db"] = it["qs"] * it["dq"] - kk * it["dk"] + jnp.where(last, extra, 0.0)
                for it in items:
                    it["dlf"] = _fdot(triu, it["db"])
                carry = list(carry)
                for it in items:
                    hh, sl, cols, sg, lb = it["hh"], it["sl"], it["cols"], it["sg"], it["lb"]
                    dfg = it["dlf"] / it["fg"] - it["dk"]
                    dlb_acc, dgain_acc = carry[hh]
                    carry[hh] = (dlb_acc + jnp.sum(dfg * (1.0 - sg), axis=0, keepdims=True), dgain_acc + it["dgain"])
                    res[0, sl, cols] = (it["dq"] * _dsilu(it["qv"], it["sq"])).astype(BF16)
                    res[1, sl, cols] = (dfg * (1.0 - lb) * sg * (1.0 - sg)).astype(BF16)
                    res[2, sl, cols] = it["dv"].astype(BF16)
                    res[3, sl, cols] = it["dg"].astype(BF16)
                return tuple(carry)

            assert nc % MIX_UNROLL_BWD == 0, (nc, MIX_UNROLL_BWD)
            zero = jnp.zeros((1, HEAD), F32)
            sums = lax.fori_loop(0, nc // MIX_UNROLL_BWD, chunks, ((zero, zero),) * hp)
            for hh, (dlb, dgain) in enumerate(sums):
                cols = slice(hh * HEAD, (hh + 1) * HEAD)
                lb = lb_all[:, cols]
                dgain_ref[:, cols] = dgain
                dl0 = dlb * lb * (1.0 - lb)
                dl_ref[0:1, cols] = dl0
                dl_ref[1:2, cols] = -dl0

        compute()
        _write_column_groups(res, dproj_ref, out_sems, 0, ng, pl.program_id(0), wide)

    ng = n_heads // hp
    col = lambda k: pl.BlockSpec((t, wide), lambda h: (0, k * ng + h))
    head = pl.BlockSpec((t, wide), lambda h: (0, h))
    vec = pl.BlockSpec((1, wide), lambda h: (0, h))
    return pl.pallas_call(
        body, grid=(ng,),
        in_specs=[ANY] * (1 + len(deps)) + [col(0), col(1), col(2), col(3), head,
                  pl.BlockSpec((hp, nc, HEAD, HEAD), lambda h: (h, 0, 0, 0)),
                  head, pl.BlockSpec((2, wide), lambda h: (0, h)), vec],
        out_specs=[ANY, pl.BlockSpec((2, wide), lambda h: (0, h)), vec],
        out_shape=[SDS(dproj.shape, BF16), SDS((2, da), F32), SDS((1, da), F32)],
        scratch_shapes=[pltpu.VMEM((4, t, wide), BF16), pltpu.VMEM((hp, HEAD, HEAD), F32),
                        pltpu.VMEM((hp * MIX_UNROLL_BWD, CHUNK, HEAD), F32), pltpu.SemaphoreType.DMA((4,))],
        input_output_aliases={0: 0}, compiler_params=_params(("arbitrary",)),
        name="hgrn_bwd")(dproj, *deps, proj, proj, proj, proj, o_pre, states, dy, lb_logits, gain)


ROWS = 256


def _head_norm(x_ref, gain, dst, dst_off, t):
    def step(i, _):
        sl = pl.ds(pl.multiple_of(i * ROWS, ROWS), ROWS)
        xv = x_ref[sl, :]
        r = lax.rsqrt(jnp.mean(xv * xv, axis=-1, keepdims=True) + EPS)
        dst[pl.ds(pl.multiple_of(dst_off + i * ROWS, ROWS), ROWS), :] = (xv * r * gain).astype(BF16)
        return 0

    lax.fori_loop(0, t // ROWS, step, 0)


def _head_norm_bwd(x_ref, gain, dn_ref, dn_off, out, slot, t):
    def step(i, acc):
        sl = pl.ds(pl.multiple_of(i * ROWS, ROWS), ROWS)
        xv = x_ref[sl, :]
        dn = dn_ref[pl.ds(pl.multiple_of(dn_off + i * ROWS, ROWS), ROWS), :]
        r = lax.rsqrt(jnp.mean(xv * xv, axis=-1, keepdims=True) + EPS)
        u = dn * gain
        out[slot, sl, :] = (r * u - xv * (r * r * r) * jnp.mean(u * xv, axis=-1, keepdims=True)).astype(out.dtype)
        return acc + jnp.sum(dn * (xv * r), axis=0, keepdims=True)

    return lax.fori_loop(0, t // ROWS, step, jnp.zeros((1, HEAD), F32))


def _attn_scores(qn, kpad, n):
    qc = qn[pl.ds(pl.multiple_of(n * CHUNK, CHUNK), CHUNK), :]
    band = pl.ds(pl.multiple_of(n * CHUNK, CHUNK), BAND)
    return qc, band, lax.dot_general(qc, kpad[band, :], NT, preferred_element_type=F32)


def _attn_softmax(raw, bias_ref, n):
    s = raw * (HEAD ** -0.5) + bias_ref[0]
    col = lax.broadcasted_iota(jnp.int32, (CHUNK, BAND), 1)
    s = jnp.where(col >= PAD - n * CHUNK, s, -jnp.inf)
    p = jnp.exp(s - jnp.max(s, axis=-1, keepdims=True))
    return p / jnp.sum(p, axis=-1, keepdims=True)


def _attn_fwd(proj, q_gain, k_gain, bias, n_heads, col0):
    t = proj.shape[0]
    nc = t // CHUNK

    def body(q_ref, k_ref, v_ref, qg_ref, kg_ref, bias_ref, y_ref, qn, kpad, vpad):
        kpad[0:PAD, :] = jnp.zeros((PAD, HEAD), BF16)
        vpad[0:PAD, :] = jnp.zeros((PAD, HEAD), BF16)
        _head_norm(q_ref, qg_ref[...], qn, 0, t)
        _head_norm(k_ref, kg_ref[...], kpad, PAD, t)

        def copy_v(i, _):
            vpad[pl.ds(pl.multiple_of(PAD + i * ROWS, ROWS), ROWS), :] = v_ref[
                pl.ds(pl.multiple_of(i * ROWS, ROWS), ROWS), :].astype(BF16)
            return 0

        lax.fori_loop(0, t // ROWS, copy_v, 0)

        def chunks(i, _):
            ns = [i * ATT_UNROLL + u for u in range(ATT_UNROLL)]
            scored = [_attn_scores(qn, kpad, n) for n in ns]
            probs = [_attn_softmax(raw, bias_ref, n).astype(BF16) for n, (_, _, raw) in zip(ns, scored)]
            outs = [lax.dot_general(p, vpad[band, :], NN, preferred_element_type=F32).astype(BF16)
                    for p, (_, band, _) in zip(probs, scored)]
            for n, o in zip(ns, outs):
                y_ref[pl.ds(pl.multiple_of(n * CHUNK, CHUNK), CHUNK), :] = o
            return 0

        assert nc % ATT_UNROLL == 0, (nc, ATT_UNROLL)
        lax.fori_loop(0, nc // ATT_UNROLL, chunks, 0)

    col = lambda k: pl.BlockSpec((t, HEAD), lambda h: (0, col0 + k * n_heads + h))
    vec = pl.BlockSpec((1, HEAD), lambda h: (0, 0))
    return pl.pallas_call(
        body, grid=(n_heads,),
        in_specs=[col(0), col(1), col(2), vec, vec, pl.BlockSpec((1, CHUNK, BAND), lambda h: (h, 0, 0))],
        out_specs=pl.BlockSpec((t, HEAD), lambda h: (0, h)), out_shape=SDS((t, n_heads * HEAD), BF16),
        scratch_shapes=[pltpu.VMEM((t, HEAD), BF16), pltpu.VMEM((t + PAD, HEAD), BF16), pltpu.VMEM((t + PAD, HEAD), BF16)],
        compiler_params=_params(("parallel",)), name="attn_fwd")(proj, proj, proj, q_gain, k_gain, bias)


def _attn_bwd(dproj, proj, q_gain, k_gain, bias, dy, n_heads, col0, deps=()):
    t = proj.shape[0]
    nc = t // CHUNK

    def body(*refs):
        (q_ref, k_ref, v_ref, qg_ref, kg_ref, bias_ref, dy_ref,
         dproj_ref, dbias_ref, dqg_ref, dkg_ref, qn, kpad, vpad, dqn, dk_acc, dv_acc, res,
         out_sems) = refs[1 + len(deps):]
        h = pl.program_id(0)

        def compute():
            kpad[0:PAD, :] = jnp.zeros((PAD, HEAD), BF16)
            vpad[0:PAD, :] = jnp.zeros((PAD, HEAD), BF16)
            _head_norm(q_ref, qg_ref[...], qn, 0, t)
            _head_norm(k_ref, kg_ref[...], kpad, PAD, t)

            def prep(i, _):
                sl = pl.ds(pl.multiple_of(PAD + i * ROWS, ROWS), ROWS)
                vpad[sl, :] = v_ref[pl.ds(pl.multiple_of(i * ROWS, ROWS), ROWS), :].astype(BF16)
                return 0

            lax.fori_loop(0, t // ROWS, prep, 0)

            def clear(i, _):
                sl = pl.ds(pl.multiple_of(i * ROWS, ROWS), ROWS)
                dk_acc[sl, :] = jnp.zeros((ROWS, HEAD), F32)
                dv_acc[sl, :] = jnp.zeros((ROWS, HEAD), F32)
                return 0

            lax.fori_loop(0, (t + PAD) // ROWS, clear, 0)
            dbias_ref[0] = jnp.zeros((CHUNK, BAND), F32)

            def chunks(i, _):
                dot = functools.partial(lax.dot_general, preferred_element_type=F32)
                ns = [i * ATT_UNROLL_BWD + u for u in range(ATT_UNROLL_BWD)]
                scored = [_attn_scores(qn, kpad, n) for n in ns]
                dos = [dy_ref[pl.ds(pl.multiple_of(n * CHUNK, CHUNK), CHUNK), :].astype(BF16) for n in ns]
                dps = [dot(do, vpad[band, :], NT) for do, (_, band, _) in zip(dos, scored)]
                ps, dss = [], []
                for n, (_, _, raw), dp in zip(ns, scored, dps):
                    p = _attn_softmax(raw, bias_ref, n)
                    ds = p * (dp - jnp.sum(dp * p, axis=-1, keepdims=True))
                    dbias_ref[0] += ds
                    ps.append(p.astype(BF16))
                    dss.append((ds * (HEAD ** -0.5)).astype(BF16))
                dqs = [dot(d, kpad[band, :], NN) for d, (_, band, _) in zip(dss, scored)]
                dks = [dot(d, qc, TN) for d, (qc, _, _) in zip(dss, scored)]
                dvs = [dot(p, do, TN) for p, do in zip(ps, dos)]
                for n, (_, band, _), dq, dk, dv in zip(ns, scored, dqs, dks, dvs):
                    dqn[pl.ds(pl.multiple_of(n * CHUNK, CHUNK), CHUNK), :] = dq
                    dk_acc[band, :] += dk
                    dv_acc[band, :] += dv
                return 0

            assert nc % ATT_UNROLL_BWD == 0, (nc, ATT_UNROLL_BWD)
            lax.fori_loop(0, nc // ATT_UNROLL_BWD, chunks, 0)
            dqg = _head_norm_bwd(q_ref, qg_ref[...], dqn, 0, res, 0, t)
            dkg = _head_norm_bwd(k_ref, kg_ref[...], dk_acc, PAD, res, 1, t)

            def put_v(i, _):
                sl = pl.ds(pl.multiple_of(i * ROWS, ROWS), ROWS)
                res[2, sl, :] = dv_acc[pl.ds(pl.multiple_of(PAD + i * ROWS, ROWS), ROWS), :].astype(BF16)
                return 0

            lax.fori_loop(0, t // ROWS, put_v, 0)

            @pl.when(h == 0)
            def _():
                dqg_ref[...] = jnp.zeros_like(dqg_ref)
                dkg_ref[...] = jnp.zeros_like(dkg_ref)

            dqg_ref[...] += dqg
            dkg_ref[...] += dkg

        compute()
        _write_column_groups(res, dproj_ref, out_sems, col0, n_heads, h, HEAD)

    col = lambda k: pl.BlockSpec((t, HEAD), lambda h: (0, col0 + k * n_heads + h))
    vec = pl.BlockSpec((1, HEAD), lambda h: (0, 0))
    btile = pl.BlockSpec((1, CHUNK, BAND), lambda h: (h, 0, 0))
    return pl.pallas_call(
        body, grid=(n_heads,),
        in_specs=[ANY] * (1 + len(deps)) + [col(0), col(1), col(2), vec, vec, btile,
                                            pl.BlockSpec((t, HEAD), lambda h: (0, h))],
        out_specs=[ANY, btile, vec, vec],
        out_shape=[SDS(dproj.shape, BF16), SDS((n_heads, CHUNK, BAND), F32), SDS((1, HEAD), F32), SDS((1, HEAD), F32)],
        scratch_shapes=[pltpu.VMEM((t, HEAD), BF16), pltpu.VMEM((t + PAD, HEAD), BF16), pltpu.VMEM((t + PAD, HEAD), BF16),
                        pltpu.VMEM((t, HEAD), F32), pltpu.VMEM((t + PAD, HEAD), F32), pltpu.VMEM((t + PAD, HEAD), F32),
                        pltpu.VMEM((3, t, HEAD), BF16), pltpu.SemaphoreType.DMA((3,))],
        input_output_aliases={0: 0}, compiler_params=_params(("arbitrary",)),
        name="attn_bwd")(dproj, *deps, proj, proj, proj, q_gain, k_gain, bias, dy)


def _place():
    x, y, c = lax.axis_index("x"), lax.axis_index("y"), lax.axis_index("c")
    others = [(1 - x, y), (x, 1 - y), (1 - x, 1 - y)]
    return x, y, c, others


def _chunk_of(ref, kind, chip, half, shard_shape):
    r, n = shard_shape
    hr = r // 2
    if kind == "col":
        rows = pl.ds(0, r) if half is None else pl.ds(half * hr, hr)
        return ref.at[rows, pl.ds(chip * n, n)]
    rows = pl.ds(chip * r, r) if half is None else pl.ds(chip * r + half * hr, hr)
    return ref.at[rows, :]


EFFECT = pltpu.SideEffectType.DATAFLOW_SIDE_EFFECTING


def _start_copies(name, bufs, plan, n, deps):
    nb, nd = len(bufs), len(deps)

    def body(*refs):
        send, recv, token = refs[nb + nd], refs[nb + nd + 1], refs[-1]
        for cp in plan(refs[:nb], send, recv)[0]:
            cp.start()
        token[...] = jnp.zeros_like(token)

    out = pl.pallas_call(
        body, name=name,
        out_shape=(pltpu.SemaphoreType.DMA((n,)), pltpu.SemaphoreType.DMA((n,)),
                   *[pltpu.HBM(b.shape, b.dtype) for b in bufs], SDS((8, 128), F32)),
        in_specs=[HBM] * nb + [ANY] * nd,
        out_specs=(SEM, SEM, *[HBM] * nb, pl.BlockSpec(memory_space=pltpu.VMEM)),
        input_output_aliases={i: 2 + i for i in range(nb)},
        compiler_params=pltpu.CompilerParams(has_side_effects=EFFECT),
    )(*[pltpu.with_memory_space_constraint(b, pltpu.HBM) for b in bufs], *deps)
    return out[0], out[1], list(out[2:2 + nb]), out[-1]


def _wait_copies(name, bufs, send, recv, plan, after):
    nb = len(bufs)

    def body(*refs):
        sends, recvs = plan(refs[:nb], refs[nb], refs[nb + 1])
        for cp in sends:
            cp.wait_send()
        for cp in recvs:
            cp.wait_recv()

    out = pl.pallas_call(
        body, name=name, out_shape=tuple(pltpu.HBM(b.shape, b.dtype) for b in bufs),
        in_specs=[HBM] * nb + [SEM, SEM] + [ANY] * len(after), out_specs=tuple([HBM] * nb),
        input_output_aliases={i: i for i in range(nb)},
        compiler_params=pltpu.CompilerParams(has_side_effects=EFFECT),
    )(*bufs, send, recv, *after)
    return list(out)


def _remote(src, dst, send, recv, i, dev):
    return pltpu.make_async_remote_copy(src_ref=src, dst_ref=dst, send_sem=send.at[i], recv_sem=recv.at[i],
                                        device_id=dev, device_id_type=MESH)


ALL_RELATIONS = (0, 1, 2)


def _plan_gather_ici(kinds, shapes, rels=ALL_RELATIONS):
    def plan(refs, send, recv):
        x, y, c, others = _place()
        sends, recvs = [], []
        for w, (kind, ss) in enumerate(zip(kinds, shapes)):
            for p in rels:
                px, py = others[p]
                mine = _chunk_of(refs[w], kind, 2 * x + y, c, ss)
                theirs = _chunk_of(refs[w], kind, 2 * px + py, c, ss)
                sends.append(_remote(mine, mine, send, recv, 3 * w + p, (px, py, c)))
                recvs.append(_remote(theirs, theirs, send, recv, 3 * w + p, (px, py, c)))
        return sends, recvs

    return plan, 3 * len(kinds)


def _plan_gather_pass(kinds, shapes, rels=ALL_RELATIONS):
    def plan(refs, send, recv):
        x, y, c, others = _place()
        sends, recvs = [], []
        for w, (kind, ss) in enumerate(zip(kinds, shapes)):
            for i, p in enumerate(rels):
                px, py = others[p]
                got = _chunk_of(refs[w], kind, 2 * px + py, c, ss)
                coming = _chunk_of(refs[w], kind, 2 * px + py, 1 - c, ss)
                sends.append(_remote(got, got, send, recv, len(rels) * w + i, (x, y, 1 - c)))
                recvs.append(_remote(coming, coming, send, recv, len(rels) * w + i, (x, y, 1 - c)))
        return sends, recvs

    return plan, len(rels) * len(kinds)


def _plan_pair(kinds, shapes):
    nw = len(kinds)

    def plan(refs, send, recv):
        x, y, c, _ = _place()
        sends = []
        for w, (kind, ss) in enumerate(zip(kinds, shapes)):
            for k in range(4):
                sends.append(_remote(_chunk_of(refs[w], kind, k, 1 - c, ss), refs[nw + w].at[k], send, recv,
                                     4 * w + k, (x, y, 1 - c)))
        return sends, sends

    return plan, 4 * nw


def _plan_chip(nw):
    def plan(refs, send, recv):
        x, y, c, others = _place()
        sends = []
        for w in range(nw):
            for p, (px, py) in enumerate(others):
                sends.append(_remote(refs[w].at[p], refs[nw + w].at[p], send, recv, 3 * w + p, (px, py, c)))
        return sends, sends

    return plan, 3 * nw


def _plan_share(slabs):
    def plan(refs, send, recv):
        x, y, c, _ = _place()
        sends, recvs, i = [], [], 0
        for w, ns in enumerate(slabs):
            for s in range(ns):
                sends.append(_remote(refs[w].at[s, c], refs[w].at[s, c], send, recv, i, (x, y, 1 - c)))
                recvs.append(_remote(refs[w].at[s, 1 - c], refs[w].at[s, 1 - c], send, recv, i, (x, y, 1 - c)))
                i += 1
        return sends, recvs

    return plan, sum(slabs)


def _grad_half_spec(kind, tr, tn, nr, nn, chunk):
    if kind == "col":
        return pl.BlockSpec((tr, tn), lambda *a: (a[-1][1] * nr + a[-3], chunk(*a) * nn + a[-2]))
    return pl.BlockSpec((tr, tn), lambda *a: ((2 * chunk(*a) + a[-1][1]) * nr + a[-3], a[-2]))


def _pair_add(grad, got, kind, shard_shape, pos, name):
    r, n = shard_shape
    hr = r // 2
    tr, tn = _tile(hr, 256, 16), _tile(n, 1408)
    nr, nn = hr // tr, n // tn
    g_spec = _grad_half_spec(kind, tr, tn, nr, nn, lambda p, i, j, pos_: pos_[2 + p])
    r_spec = pl.BlockSpec((1, tr, tn), lambda p, i, j, pos_: (pos_[2 + p], i, j))
    o_spec = pl.BlockSpec((1, tr, tn), lambda p, i, j, pos_: (p, i, j))

    def body(pos_ref, g_ref, r_ref, o_ref):
        o_ref[0] = (g_ref[...] + r_ref[0]).astype(BF16)

    return pl.pallas_call(
        body,
        grid_spec=pltpu.PrefetchScalarGridSpec(num_scalar_prefetch=1, grid=(3, nr, nn), in_specs=[g_spec, r_spec],
                                               out_specs=o_spec),
        out_shape=SDS((3, hr, n), BF16),
        compiler_params=_params(("parallel", "parallel", "parallel")), name=name)(pos, grad, got)


def _chip_add(grad, got, got16, kind, shard_shape, pos, name, slab=0, slabs=1, prev=None):
    r, n = shard_shape
    hr = r // 2
    tr, tn = _tile(hr, 256, 16), _tile(n, 1408)
    nr, nn = hr // tr, n // tn
    g_spec = _grad_half_spec(kind, tr, tn, nr, nn, lambda i, j, pos_: pos_[0])
    r_spec = pl.BlockSpec((1, tr, tn), lambda i, j, pos_: (pos_[0], i, j))
    oth = pl.BlockSpec((3, tr, tn), lambda i, j, pos_: (0, i, j))

    def body(pos_ref, g_ref, r_ref, oth_ref, *rest):
        own = g_ref[...] + r_ref[0]
        rest[-1][0, 0] = ((own + oth_ref[0].astype(F32)) + oth_ref[1].astype(F32)) + oth_ref[2].astype(F32)

    return pl.pallas_call(
        body,
        grid_spec=pltpu.PrefetchScalarGridSpec(
            num_scalar_prefetch=1, grid=(nr, nn), in_specs=[g_spec, r_spec, oth] + [ANY] * (prev is not None),
            out_specs=pl.BlockSpec((1, 1, tr, tn), lambda i, j, pos_: (slab, pos_[1], i, j))),
        out_shape=SDS((slabs, 2, hr, n), F32), input_output_aliases={4: 0} if prev is not None else {},
        compiler_params=_params(("parallel", "parallel")),
        name=name)(pos, grad, got, got16, *(() if prev is None else (prev,)))


def _adamw_math(w, g, m, v):
    m = ADAM_B1 * m + (1.0 - ADAM_B1) * g
    v = ADAM_B2 * v + (1.0 - ADAM_B2) * (g * g)
    m_hat = m / (1.0 - ADAM_B1 ** ADAM_STEP)
    v_hat = v / (1.0 - ADAM_B2 ** ADAM_STEP)
    return -ADAM_LR * (m_hat / (jnp.sqrt(v_hat) + ADAM_EPS) + ADAM_WD * w), m, v


def _adamw(w, g, m, v, name):
    r, n = w.shape
    tr, tn = _tile(r, 256, 16), _tile(n, 1408)

    def body(w_ref, g_ref, m_ref, v_ref, d_ref, nm_ref, nv_ref, go_ref):
        gv = g_ref[...]
        d_ref[...], nm_ref[...], nv_ref[...] = _adamw_math(w_ref[...], gv, m_ref[...], v_ref[...])
        go_ref[...] = gv

    tile = pl.BlockSpec((tr, tn), lambda i, j: (i, j))
    return pl.pallas_call(
        body, grid=(r // tr, n // tn), in_specs=[tile] * 4, out_specs=[tile] * 4, out_shape=[SDS((r, n), F32)] * 4,
        compiler_params=_params(("parallel", "parallel")), name=name)(w, g, m, v)


def _small_allreduce_adamw(g, w, m, v, deps=()):
    length = g.shape[1]

    def body(*refs):
        g_ref, w_ref, m_ref, v_ref = refs[:4]
        gs_ref, d_ref, nm_ref, nv_ref, buf, send, recv = refs[4 + len(deps):]
        x, y, c = lax.axis_index("x"), lax.axis_index("y"), lax.axis_index("c")
        me = 4 * x + 2 * y + c
        buf[me] = g_ref[...]
        cps = []
        for d in range(1, 8):
            peer = (x ^ (d >> 2), y ^ ((d >> 1) & 1), c ^ (d & 1))
            cp = pltpu.make_async_remote_copy(src_ref=buf.at[me], dst_ref=buf.at[me], send_sem=send.at[d - 1],
                                              recv_sem=recv.at[d - 1], device_id=peer, device_id_type=MESH)
            cp.start()
            cps.append(cp)
        for cp in cps:
            cp.wait()
        total = buf[0]
        for d in range(1, 8):
            total = total + buf[d]
        gs_ref[...] = total
        d_ref[...], nm_ref[...], nv_ref[...] = _adamw_math(w_ref[...], total, m_ref[...], v_ref[...])

    vm = pl.BlockSpec(memory_space=pltpu.VMEM)
    return pl.pallas_call(
        body, in_specs=[vm] * 4 + [ANY] * len(deps), out_specs=[vm] * 4, out_shape=[SDS((1, length), F32)] * 4,
        scratch_shapes=[pltpu.VMEM((8, 1, length), F32), pltpu.SemaphoreType.DMA((7,)), pltpu.SemaphoreType.DMA((7,))],
        compiler_params=pltpu.CompilerParams(has_side_effects=True), name="small_allreduce_adamw")(g, w, m, v, *deps)


def kernel(x, w_in, b_gate, norm_mix, norm_ffn, hgrn_lb_logits, hgrn_out_gain, q_gain, k_gain, rel_bias, w_proj_a, w_proj_b, w_out, w_ffn_in, w_ffn_out, loss_target, m_w_in, m_b_gate, m_norm_mix, m_norm_ffn, m_hgrn_lb_logits, m_hgrn_out_gain, m_q_gain, m_k_gain, m_rel_bias, m_w_proj_a, m_w_proj_b, m_w_out, m_w_ffn_in, m_w_ffn_out, v_w_in, v_b_gate, v_norm_mix, v_norm_ffn, v_hgrn_lb_logits, v_hgrn_out_gain, v_q_gain, v_k_gain, v_rel_bias, v_w_proj_a, v_w_proj_b, v_w_out, v_w_ffn_in, v_w_ffn_out):
    t, d = x.shape[1], x.shape[2]
    d_a = hgrn_out_gain.shape[1]
    h_a = d_a // HEAD
    h_b = rel_bias.shape[1]
    d_b = h_b * HEAD
    x0 = x.reshape(t, d)
    target = loss_target.reshape(t, d)
    ax, ay = lax.axis_index("x"), lax.axis_index("y")
    pos = jnp.stack([2 * ax + ay, lax.axis_index("c"), 2 * (1 - ax) + ay, 2 * ax + 1 - ay,
                     2 * (1 - ax) + 1 - ay]).astype(jnp.int32)

    names = ["w_in", "w_proj_a", "w_proj_b", "w_out", "w_ffn_in", "w_ffn_out"]
    big = dict(zip(names, [w_in[0], w_proj_a[0], w_proj_b[0], w_out[0], w_ffn_in[0], w_ffn_out[0]]))
    big_m = dict(zip(names, [m_w_in[0], m_w_proj_a[0], m_w_proj_b[0], m_w_out[0], m_w_ffn_in[0], m_w_ffn_out[0]]))
    big_v = dict(zip(names, [v_w_in[0], v_w_proj_a[0], v_w_proj_b[0], v_w_out[0], v_w_ffn_in[0], v_w_ffn_out[0]]))
    kind = dict(zip(names, ["col", "col", "col", "row", "col", "row"]))
    shape = {nm: big[nm].shape for nm in names}

    def gather_start(tag, group, deps):
        plan, n = _plan_gather_ici([kind[g] for g in group], [shape[g] for g in group])
        fulls = [_cast_into_full(big[g], kind[g], pos, "cast_" + g) for g in group]
        send, recv, bufs, token = _start_copies("gather_ici_start_" + tag, fulls, plan, n, deps)
        return (tag, group, plan, send, recv, bufs), token

    def gather_pass(state, after, rels=ALL_RELATIONS, part=""):
        tag, group, _, send, recv, bufs = state
        kinds_, shapes_ = [kind[g] for g in group], [shape[g] for g in group]
        bufs = _wait_copies("gather_ici_wait_" + tag + part, bufs, send, recv,
                            _plan_gather_ici(kinds_, shapes_, rels)[0], after)
        plan, n = _plan_gather_pass(kinds_, shapes_, rels)
        send2, recv2, bufs, token = _start_copies("gather_pass_start_" + tag + part, bufs, plan, n, ())
        return (tag + part, group, plan, send2, recv2, bufs), token

    def gather_done(state, after):
        tag, group, plan, send, recv, bufs = state
        return _wait_copies("gather_pass_wait_" + tag, bufs, send, recv, plan, after)

    def reduce_start(tag, group, grads, deps):
        plan, n = _plan_pair([kind[g] for g in group], [shape[g] for g in group])
        lands = [lax.empty((4, shape[g][0] // 2, shape[g][1]), F32) for g in group]
        send, recv, bufs, token = _start_copies("pair_start_" + tag, list(grads) + lands, plan, n, deps)
        return dict(tag=tag, group=group, plan=plan, send=send, recv=recv, bufs=bufs), token

    def reduce_pair_done(st, after):
        tag, group, nw = st["tag"], st["group"], len(st["group"])
        bufs = _wait_copies("pair_wait_" + tag, st["bufs"], st["send"], st["recv"], st["plan"], after)
        grads, gots = bufs[:nw], bufs[nw:]
        parts = [_pair_add(g, l, kind[nm], shape[nm], pos, "pair_add_" + nm) for g, l, nm in zip(grads, gots, group)]
        lands = [lax.empty((3, shape[g][0] // 2, shape[g][1]), BF16) for g in group]
        plan, n = _plan_chip(nw)
        send, recv, bufs, token = _start_copies("chip_start_" + tag, parts + lands, plan, n, ())
        return dict(st, plan=plan, send=send, recv=recv, bufs=bufs, grads=grads, gots=gots), token

    def reduce_chip_wait(st, after, slab=0, slabs=1, prev=None):
        tag, group, nw = st["tag"], st["group"], len(st["group"])
        bufs = _wait_copies("chip_wait_" + tag, st["bufs"], st["send"], st["recv"], st["plan"], after)
        return [_chip_add(g, l, got16, kind[nm], shape[nm], pos, "chip_add_" + nm, slab, slabs, prev)
                for g, l, got16, nm in zip(st["grads"], st["gots"], bufs[nw:], group)]

    def reduce_share(tag, group, finals, slabs=1):
        plan, n = _plan_share([slabs] * len(finals))
        send, recv, bufs, token = _start_copies("share_start_" + tag, finals, plan, n, ())
        return dict(tag=tag, group=group, plan=plan, send=send, recv=recv, bufs=bufs), token

    def reduce_chip_done(st, after):
        return reduce_share(st["tag"], st["group"], reduce_chip_wait(st, after))

    g_big, upd = {}, {}

    def reduce_finish(st, after):
        bufs = _wait_copies("share_wait_" + st["tag"], st["bufs"], st["send"], st["recv"], st["plan"], after)
        for full, nm in zip(bufs, st["group"]):
            upd[nm] = _adamw(big[nm], full.reshape(shape[nm]), big_m[nm], big_v[nm], "adamw_" + nm)
            g_big[nm] = upd[nm][3]

    ga, token = gather_start("a", ["w_in"], ())
    gb, token = gather_start("b", ["w_proj_a", "w_proj_b", "w_out"], (token,))
    gc, token = gather_start("c", ["w_ffn_in"], (token,))
    gd, token = gather_start("d", ["w_ffn_out"], (token,))
    h1, r1 = _rmsnorm_fwd(x0, norm_mix, "rmsnorm_mix")
    rb = jnp.pad(rel_bias[0], ((0, 0), (0, REL_LANES - N_REL)))
    bias = _relbias_expand(rb).transpose(1, 0, 2)
    proj = _matmul_chunks(h1, big["w_in"], (0,), None, pos, "proj_in_own", own_shard=True)
    ici_a = ga
    ga, token = gather_pass(ici_a, (h1, bias, proj, token), rels=(0, 1), part="_near")
    (wg_in,) = gather_done(ga, ())
    proj = _matmul_chunks(h1, wg_in, (2, 3), proj, pos, "proj_in_near")
    ga, token = gather_pass(ici_a[:5] + ([wg_in],), (proj,), rels=(2,), part="_far")
    (wg_in,) = gather_done(ga, ())
    proj = _matmul_chunks(h1, wg_in, (4,), proj, pos, "proj_in_far")
    y_a, o_pre, states = _hgrn_fwd(proj, hgrn_lb_logits, hgrn_out_gain, h_a)
    gb, token = gather_pass(gb, (y_a,))
    col_b = 4 * d_a // HEAD
    y_b = _attn_fwd(proj, q_gain, k_gain, bias, h_b, col_b)
    wg_pa, wg_pb, wg_out = gather_done(gb, (y_b,))
    pa = _matmul(y_a, wg_pa, name="proj_a", deps=(token,))
    pb = _matmul(y_b, wg_pb, name="proj_b")
    gate_off = 4 * d_a + 3 * d_b
    merged = _merge_fwd(proj, b_gate, pa, pb, gate_off)
    x2 = _matmul(merged, wg_out, res=x0, name="out_proj")
    gc, token = gather_pass(gc, (x2,))
    h2, r2 = _rmsnorm_fwd(x2, norm_ffn, "rmsnorm_ffn")
    (wg_fin,) = gather_done(gc, (h2,))
    ff_gate, ff_up, act = _ffn_in_swiglu(h2, wg_fin, deps=(token,))
    gd, token = gather_pass(gd, (act,))
    (wg_fout,) = gather_done(gd, ())
    dy, dy16, loss_part = _ffn_out_loss(act, wg_fout, x2, target)

    g_fout = _matmul(act, dy16, ta=True, name="dw_ffn_out")
    r_fout, token = reduce_start("fout", ["w_ffn_out"], [g_fout], ())
    dact = _matmul(dy16, wg_fout, tb=True, name="d_act", deps=(token,))
    r_fout, token = reduce_pair_done(r_fout, (dact,))
    dgu = _swiglu_bwd(dact, ff_gate, ff_up)
    g_fin = _matmul(h2, dgu, ta=True, name="dw_ffn_in", deps=(token,))
    r_fin, token = reduce_start("fin", ["w_ffn_in"], [g_fin], ())
    dh2 = _matmul(dgu, wg_fin, tb=True, name="d_h2", deps=(token,))
    r_fout, token_a = reduce_chip_done(r_fout, (dh2,))
    r_fin, token_b = reduce_pair_done(r_fin, (dh2,))
    dx2, dx2_16, g_norm_ffn = _rmsnorm_bwd(dh2, x2, r2, norm_ffn, dy, "rmsnorm_ffn_bwd", deps=(token_a, token_b))
    dmerged = _matmul(dx2_16, wg_out, tb=True, name="d_merged")
    dp_ab, dproj, g_bgate = _merge_bwd(dmerged, proj, b_gate, pa, pb, gate_off)
    g_out = _matmul(merged, dx2_16, ta=True, name="dw_out")
    g_pa = _matmul(y_a, dp_ab[0], ta=True, name="dw_proj_a")
    g_pb = _matmul(y_b, dp_ab[1], ta=True, name="dw_proj_b")
    r_mid, token = reduce_start("mid", ["w_proj_a", "w_proj_b", "w_out"], [g_pa, g_pb, g_out], ())
    dy_a = _matmul(dp_ab[0], wg_pa, tb=True, name="d_y_a", deps=(token,))
    dy_b = _matmul(dp_ab[1], wg_pb, tb=True, name="d_y_b")
    r_fin, token_a = reduce_chip_done(r_fin, (dy_b,))
    r_mid, token_b = reduce_pair_done(r_mid, (dy_b,))
    dproj, dbias, g_qg, g_kg = _attn_bwd(dproj, proj, q_gain, k_gain, bias, dy_b, h_b, col_b, deps=(token_a, token_b))
    r_mid, token = reduce_chip_done(r_mid, (dbias,))
    dproj, g_lb, g_gain = _hgrn_bwd(dproj, proj, o_pre, states, dy_a, hgrn_lb_logits, hgrn_out_gain, h_a, deps=(token,))
    g_in = _matmul(h1, dproj, ta=True, name="dw_in")
    r_in, token = reduce_start("in", ["w_in"], [g_in], ())
    g_rb = _relbias_reduce(dbias.transpose(1, 0, 2))[:, :N_REL]
    reduce_finish(r_mid, (token,))
    reduce_finish(r_fout, (token,))
    r_in, token = reduce_pair_done(r_in, (g_rb, upd["w_out"][0], upd["w_proj_a"][0], upd["w_proj_b"][0],
                                          upd["w_ffn_out"][0]))
    dh1 = _matmul(dproj, wg_in, tb=True, name="d_h1", deps=(token,))
    dx, _, g_norm_mix = _rmsnorm_bwd(dh1, x0, r1, norm_mix, dx2, "rmsnorm_mix_bwd")
    reduce_finish(r_fin, (dx,))
    r_in, token = reduce_chip_done(r_in, (upd["w_ffn_in"][0],))

    small_w = [b_gate, norm_mix, norm_ffn, hgrn_lb_logits, hgrn_out_gain, q_gain, k_gain, rel_bias]
    small_m = [m_b_gate, m_norm_mix, m_norm_ffn, m_hgrn_lb_logits, m_hgrn_out_gain, m_q_gain, m_k_gain, m_rel_bias]
    small_v = [v_b_gate, v_norm_mix, v_norm_ffn, v_hgrn_lb_logits, v_hgrn_out_gain, v_q_gain, v_k_gain, v_rel_bias]
    small_g = [g_bgate, g_norm_mix, g_norm_ffn, g_lb, g_gain, g_qg, g_kg, g_rb]
    sizes = [w.size for w in small_w]
    length = -(-(sum(sizes) + 1) // 128) * 128

    def pack(parts_):
        flat = jnp.concatenate([p.reshape(1, -1) for p in parts_], axis=1)
        return jnp.pad(flat, ((0, 0), (0, length - flat.shape[1])))

    one = jnp.ones((1, 1), F32)
    packed = _small_allreduce_adamw(pack(small_g + [loss_part]), pack(small_w + [one]), pack(small_m + [one]),
                                    pack(small_v + [one]), deps=(token,))

    def unpack(vec):
        out, at = [], 0
        for w, n in zip(small_w, sizes):
            out.append(vec[0, at:at + n].reshape(w.shape))
            at += n
        return out, vec[0, at]

    (sg, loss), (sd, _), (sm, _), (sv, _) = [unpack(p) for p in packed]
    reduce_finish(r_in, (packed[0],))

    def ordered(small, bigs):
        bigs = [bigs[nm][None] for nm in names]
        return [bigs[0]] + small + bigs[1:]

    return (loss, dx.reshape(x.shape), *ordered(sg, g_big), *ordered(sd, {nm: upd[nm][0] for nm in names}),
            *ordered(sm, {nm: upd[nm][1] for nm in names}), *ordered(sv, {nm: upd[nm][2] for nm in names}))
```

```python
import functools

import jax
import jax.numpy as jnp
from jax import lax
from jax.experimental import pallas as pl
from jax.experimental.pallas import tpu as pltpu

F32 = jnp.float32
BF16 = jnp.bfloat16
SDS = jax.ShapeDtypeStruct
MESH = pl.DeviceIdType.MESH
HIGHEST = lax.Precision.HIGHEST

CHUNK = 64
SUB = 16
HEAD = 128
N_PAST = 8
BAND = (N_PAST + 1) * CHUNK
PAD = N_PAST * CHUNK
REL_FUTURE = CHUNK - 1
REL_PAST = 2 * CHUNK - 1
N_REL = REL_FUTURE + REL_PAST + 1
REL_LANES = 256
EPS = 1e-6
MIX_HEADS = 2
MIX_UNROLL = 4
MIX_UNROLL_BWD = 4
ATT_UNROLL = 8
ATT_UNROLL_BWD = 4
EXP_CLAMP = 80.0

ADAM_LR = 0.001
ADAM_B1 = 0.9
ADAM_B2 = 0.999
ADAM_EPS = 1e-08
ADAM_WD = 0.01
ADAM_STEP = 10

VMEM_LIMIT = 56 * 1024 * 1024

HBM = pl.BlockSpec(memory_space=pltpu.HBM)
ANY = pl.BlockSpec(memory_space=pl.ANY)
SEM = pl.BlockSpec(memory_space=pltpu.SEMAPHORE)

NT = (((1,), (1,)), ((), ()))
TN = (((0,), (0,)), ((), ()))
NN = (((1,), (0,)), ((), ()))


def _params(sem=None, **kw):
    return pltpu.CompilerParams(dimension_semantics=sem, vmem_limit_bytes=VMEM_LIMIT, **kw)


def _tile(n, pref, unit=128):
    if n <= pref:
        return n
    t = pref - pref % unit
    while n % t:
        t -= unit
    return t


def _loop(n, unroll, step, init):
    assert n % unroll == 0, (n, unroll)

    def several(i, carry):
        for u in range(unroll):
            carry = step(i * unroll + u, carry)
        return carry

    return lax.fori_loop(0, n // unroll, several, init)


def _sigmoid(x):
    return 1.0 / (1.0 + jnp.exp(-x))


def _dsilu(x, s):
    return s * (1.0 + x * (1.0 - s))


def _bdot(a, b, dims=NN):
    return lax.dot_general(a.astype(BF16), b.astype(BF16), dims, preferred_element_type=F32)


def _split(a):
    hi = a.astype(BF16)
    return hi, (a - hi.astype(F32)).astype(BF16)


def _dot3(a, b, dims):
    dot = lambda u, v: lax.dot_general(u, v, dims, preferred_element_type=F32)
    return dot(a[0], b[1]) + dot(a[1], b[0]) + dot(a[0], b[0])


def _fdot(a, b):
    return lax.dot_general(a, b, NN, precision=HIGHEST, preferred_element_type=F32)


MM_TILE_K = 5632
MM_TILE_N = 512


def _matmul_chunks(h, w, which, prev, pos, name, own_shard=False, deps=()):
    t, d = h.shape
    nc_ = w.shape[1] if own_shard else w.shape[1] // 4
    tm, tn = _tile(t, 1024), _tile(nc_, 1408)
    nn = nc_ // tn

    def chunk(q, p):
        sel = p[which[0]]
        for i in range(1, len(which)):
            sel = jnp.where(q == i, p[which[i]], sel)
        return sel

    def body(p_ref, h_ref, w_ref, *rest):
        rest[-1][...] = jnp.dot(h_ref[...], w_ref[...].astype(BF16), preferred_element_type=F32)

    if own_shard:
        w_spec = pl.BlockSpec((d, tn), lambda q, i, j, p: (0, j))
    else:
        w_spec = pl.BlockSpec((d, tn), lambda q, i, j, p: (0, chunk(q, p) * nn + j))
    n_extra = len(deps) + (prev is not None)
    return pl.pallas_call(
        body,
        grid_spec=pltpu.PrefetchScalarGridSpec(
            num_scalar_prefetch=1, grid=(len(which), t // tm, nn),
            in_specs=[pl.BlockSpec((tm, d), lambda q, i, j, p: (i, 0)), w_spec] + [ANY] * n_extra,
            out_specs=pl.BlockSpec((tm, tn), lambda q, i, j, p: (i, chunk(q, p) * nn + j))),
        out_shape=SDS((t, 4 * nc_), F32), input_output_aliases={3 + len(deps): 0} if prev is not None else {},
        compiler_params=_params(("arbitrary", "arbitrary", "arbitrary")),
        name=name)(pos, h, w, *deps, *(() if prev is None else (prev,)))


def _matmul(a, b, *, ta=False, tb=False, res=None, out_dtype=F32, name, deps=()):
    m, k = (a.shape[1], a.shape[0]) if ta else a.shape
    n = b.shape[0] if tb else b.shape[1]
    if k > MM_TILE_K:
        tk, tm, tn = _tile(k, MM_TILE_K // 2), _tile(m, 1024), _tile(n, 1024)
    else:
        tk = k
        tm, tn = _tile(m, 2048 if tk <= MM_TILE_K // 2 else 1024), _tile(n, MM_TILE_N)
    nk = k // tk
    dims = ((((0,) if ta else (1,)), ((1,) if tb else (0,))), ((), ()))

    def body(*refs):
        n_in = 2 + (res is not None)
        a_ref, b_ref = refs[:2]
        r_ref = refs[2] if res is not None else None
        o_ref = refs[n_in + len(deps)]
        part = lax.dot_general(a_ref[...].astype(BF16), b_ref[...].astype(BF16), dims, preferred_element_type=F32)

        def finish(out):
            if r_ref is not None:
                out = out + r_ref[...]
            o_ref[...] = out.astype(o_ref.dtype)

        if nk == 1:
            finish(part)
            return
        acc_ref = refs[-1]
        kk = pl.program_id(2)

        @pl.when(kk == 0)
        def _():
            acc_ref[...] = part

        @pl.when(jnp.logical_and(kk > 0, kk < nk - 1))
        def _():
            acc_ref[...] += part

        @pl.when(kk == nk - 1)
        def _():
            finish(acc_ref[...] + part)

    a_spec = pl.BlockSpec((tk, tm), lambda i, j, l: (l, i)) if ta else pl.BlockSpec((tm, tk), lambda i, j, l: (i, l))
    b_spec = pl.BlockSpec((tn, tk), lambda i, j, l: (j, l)) if tb else pl.BlockSpec((tk, tn), lambda i, j, l: (l, j))
    o_spec = pl.BlockSpec((tm, tn), lambda i, j, l: (i, j))
    in_specs = [a_spec, b_spec] + ([o_spec] if res is not None else []) + [ANY] * len(deps)
    args = (a, b) + ((res,) if res is not None else ()) + tuple(deps)
    return pl.pallas_call(
        body, grid=(m // tm, n // tn, nk), in_specs=in_specs, out_specs=o_spec,
        out_shape=SDS((m, n), out_dtype), scratch_shapes=[pltpu.VMEM((tm, tn), F32)] if nk > 1 else [],
        compiler_params=_params(("parallel", "parallel", "arbitrary")), name=name)(*args)


def _cast_into_full(w, kind, pos, name):
    r, n = w.shape
    tr = _tile(r, 512, 16)
    nr = r // tr
    if kind == "col":
        shape, o_spec = (r, 4 * n), pl.BlockSpec((tr, n), lambda i, p: (i, p[0]))
    else:
        shape, o_spec = (4 * r, n), pl.BlockSpec((tr, n), lambda i, p: (p[0] * nr + i, 0))

    def body(p_ref, w_ref, o_ref):
        o_ref[...] = w_ref[...].astype(BF16)

    return pl.pallas_call(
        body,
        grid_spec=pltpu.PrefetchScalarGridSpec(num_scalar_prefetch=1, grid=(nr,),
                                               in_specs=[pl.BlockSpec((tr, n), lambda i, p: (i, 0))], out_specs=o_spec),
        out_shape=SDS(shape, BF16), compiler_params=_params(("parallel",)), name=name)(pos, w)


def _rmsnorm_fwd(x, gain, name):
    t, d = x.shape
    tm = _tile(t, 256)

    def body(x_ref, g_ref, h_ref, r_ref):
        xv = x_ref[...]
        r = lax.rsqrt(jnp.mean(xv * xv, axis=-1, keepdims=True) + EPS)
        h_ref[...] = (xv * r * g_ref[...]).astype(BF16)
        r_ref[...] = r

    return pl.pallas_call(
        body, grid=(t // tm,),
        in_specs=[pl.BlockSpec((tm, d), lambda i: (i, 0)), pl.BlockSpec((1, d), lambda i: (0, 0))],
        out_specs=[pl.BlockSpec((tm, d), lambda i: (i, 0)), pl.BlockSpec((tm, 1), lambda i: (i, 0))],
        out_shape=[SDS((t, d), BF16), SDS((t, 1), F32)], compiler_params=_params(("parallel",)), name=name)(x, gain)


def _rmsnorm_bwd(dh, x, r, gain, dres, name, deps=()):
    t, d = x.shape
    tm = _tile(t, 256)

    def body(dh_ref, x_ref, r_ref, g_ref, dres_ref, *rest):
        dx_ref, dxb_ref, dg_ref = rest[len(deps):]

        @pl.when(pl.program_id(0) == 0)
        def _():
            dg_ref[...] = jnp.zeros_like(dg_ref)

        dhv, xv, rv = dh_ref[...], x_ref[...], r_ref[...]
        dg_ref[...] += jnp.sum(dhv * (xv * rv), axis=0, keepdims=True)
        u = dhv * g_ref[...]
        dx = dres_ref[...] + rv * u - xv * (rv * rv * rv) * jnp.mean(u * xv, axis=-1, keepdims=True)
        dx_ref[...] = dx
        dxb_ref[...] = dx.astype(BF16)

    row = pl.BlockSpec((tm, d), lambda i: (i, 0))
    vec = pl.BlockSpec((1, d), lambda i: (0, 0))
    return pl.pallas_call(
        body, grid=(t // tm,),
        in_specs=[row, row, pl.BlockSpec((tm, 1), lambda i: (i, 0)), vec, row] + [ANY] * len(deps),
        out_specs=[row, row, vec], out_shape=[SDS((t, d), F32), SDS((t, d), BF16), SDS((1, d), F32)],
        compiler_params=_params(("arbitrary",)), name=name)(dh, x, r, gain, dres, *deps)


def _merge_fwd(proj, b_gate, pa, pb, off):
    t, d = pa.shape
    tm, tc = _tile(t, 512), _tile(d, 512)
    nj = d // tc
    oa, ob = off // tc, off // tc + nj

    def body(la_ref, lb_ref, ba_ref, bb_ref, pa_ref, pb_ref, o_ref):
        ga = _sigmoid(la_ref[...] + ba_ref[...])
        gb = _sigmoid(lb_ref[...] + bb_ref[...])
        o_ref[...] = (ga * pa_ref[...] + gb * pb_ref[...]).astype(BF16)

    tile = pl.BlockSpec((tm, tc), lambda i, j: (i, j))
    return pl.pallas_call(
        body, grid=(t // tm, nj),
        in_specs=[pl.BlockSpec((tm, tc), lambda i, j: (i, oa + j)), pl.BlockSpec((tm, tc), lambda i, j: (i, ob + j)),
                  pl.BlockSpec((1, tc), lambda i, j: (0, j)), pl.BlockSpec((1, tc), lambda i, j: (0, nj + j)), tile, tile],
        out_specs=tile, out_shape=SDS((t, d), BF16), compiler_params=_params(("parallel", "parallel")),
        name="merge_fwd")(proj, proj, b_gate, b_gate, pa, pb)


def _merge_bwd(dmerged, proj, b_gate, pa, pb, off):
    t, d = pa.shape
    tm, tc = _tile(t, 512), _tile(d, 512)
    nj, ni = d // tc, t // tm
    o0 = off // tc

    def body(dm_ref, l_ref, b_ref, pa_ref, pb_ref, dp_ref, dl_ref, db_ref):
        s, i = pl.program_id(0), pl.program_id(2)
        p = jnp.where(s == 0, pa_ref[...], pb_ref[...])
        g = _sigmoid(l_ref[...] + b_ref[...])
        dm = dm_ref[...]
        dp_ref[0] = (dm * g).astype(BF16)
        dl = dm * p * g * (1.0 - g)
        dl_ref[...] = dl.astype(BF16)

        @pl.when(i == 0)
        def _():
            db_ref[...] = jnp.zeros_like(db_ref)

        db_ref[...] += jnp.sum(dl, axis=0, keepdims=True)

    tile = pl.BlockSpec((tm, tc), lambda s, j, i: (i, j))
    return pl.pallas_call(
        body, grid=(2, nj, ni),
        in_specs=[tile, pl.BlockSpec((tm, tc), lambda s, j, i: (i, o0 + s * nj + j)),
                  pl.BlockSpec((1, tc), lambda s, j, i: (0, s * nj + j)), tile, tile],
        out_specs=[pl.BlockSpec((1, tm, tc), lambda s, j, i: (s, i, j)),
                   pl.BlockSpec((tm, tc), lambda s, j, i: (i, o0 + s * nj + j)),
                   pl.BlockSpec((1, tc), lambda s, j, i: (0, s * nj + j))],
        out_shape=[SDS((2, t, d), BF16), SDS(proj.shape, BF16), SDS((1, 2 * d), F32)],
        compiler_params=_params(("arbitrary", "arbitrary", "arbitrary")),
        name="merge_bwd")(dmerged, proj, b_gate, pa, pb)


def _ffn_in_swiglu(h, w, deps=()):
    t, d = h.shape
    f = w.shape[1] // 2
    tm, tn = _tile(t, 2048), _tile(f, MM_TILE_N)
    nj = f // tn

    def body(h_ref, wg_ref, wu_ref, *rest):
        g_ref, u_ref, a_ref = rest[len(deps):]
        hv = h_ref[...]
        g = jnp.dot(hv, wg_ref[...], preferred_element_type=F32)
        u = jnp.dot(hv, wu_ref[...], preferred_element_type=F32)
        g_ref[...] = g
        u_ref[...] = u
        a_ref[...] = (g * _sigmoid(g) * u).astype(BF16)

    tile = pl.BlockSpec((tm, tn), lambda i, j: (i, j))
    return pl.pallas_call(
        body, grid=(t // tm, nj),
        in_specs=[pl.BlockSpec((tm, d), lambda i, j: (i, 0)), pl.BlockSpec((d, tn), lambda i, j: (0, j)),
                  pl.BlockSpec((d, tn), lambda i, j: (0, nj + j))] + [ANY] * len(deps),
        out_specs=[tile, tile, tile], out_shape=[SDS((t, f), F32), SDS((t, f), F32), SDS((t, f), BF16)],
        compiler_params=_params(("parallel", "parallel")), name="ffn_in_swiglu")(h, w, w, *deps)


def _swiglu_bwd(dact, gate, up):
    t, f = gate.shape
    tm = _tile(t, 128)

    def body(d_ref, g_ref, u_ref, o_ref):
        g, dv = g_ref[...], d_ref[...]
        sg = _sigmoid(g)
        o_ref[:, :f] = (dv * u_ref[...] * _dsilu(g, sg)).astype(BF16)
        o_ref[:, f:] = (dv * (g * sg)).astype(BF16)

    row = pl.BlockSpec((tm, f), lambda i: (i, 0))
    return pl.pallas_call(
        body, grid=(t // tm,), in_specs=[row, row, row],
        out_specs=pl.BlockSpec((tm, 2 * f), lambda i: (i, 0)), out_shape=SDS((t, 2 * f), BF16),
        compiler_params=_params(("parallel",)), name="swiglu_bwd")(dact, gate, up)


def _ffn_out_loss(act, w, x_res, target):
    t, d = x_res.shape
    k = act.shape[1]
    tm, tn = _tile(t, 1024), _tile(d, MM_TILE_N)

    def body(a_ref, w_ref, r_ref, t_ref, dy_ref, dyb_ref, l_ref):
        @pl.when(jnp.logical_and(pl.program_id(0) == 0, pl.program_id(1) == 0))
        def _():
            l_ref[...] = jnp.zeros_like(l_ref)

        y = jnp.dot(a_ref[...], w_ref[...], preferred_element_type=F32) + r_ref[...]
        e = y - t_ref[...]
        dy = e * (1.0 / d)
        dy_ref[...] = dy
        dyb_ref[...] = dy.astype(BF16)
        l_ref[...] += (0.5 / d) * jnp.sum(jnp.sum(e * e, axis=-1, keepdims=True), axis=0, keepdims=True)

    tile = pl.BlockSpec((tm, tn), lambda i, j: (i, j))
    return pl.pallas_call(
        body, grid=(t // tm, d // tn),
        in_specs=[pl.BlockSpec((tm, k), lambda i, j: (i, 0)), pl.BlockSpec((k, tn), lambda i, j: (0, j)), tile, tile],
        out_specs=[tile, tile, pl.BlockSpec((1, 1), lambda i, j: (0, 0))],
        out_shape=[SDS((t, d), F32), SDS((t, d), BF16), SDS((1, 1), F32)],
        compiler_params=_params(("arbitrary", "arbitrary")), name="ffn_out_loss")(act, w, x_res, target)


def _rel_onehot(qi):
    p = lax.broadcasted_iota(jnp.int32, (REL_LANES, BAND), 1)
    r = lax.broadcasted_iota(jnp.int32, (REL_LANES, BAND), 0)
    idx = jnp.clip(qi + PAD - p, -REL_FUTURE, REL_PAST) + REL_FUTURE
    return (idx == r).astype(F32)


def _relbias_expand(rb):
    h = rb.shape[0]

    def body(rb_ref, o_ref):
        def step(qi, _):
            o_ref[qi] = _fdot(rb_ref[...], _rel_onehot(qi))
            return 0

        lax.fori_loop(0, CHUNK, step, 0)

    return pl.pallas_call(body, out_shape=SDS((CHUNK, h, BAND), F32), compiler_params=_params(),
                          name="relbias_expand")(rb)


def _relbias_reduce(dbias):
    h = dbias.shape[1]

    rows_per_pass = 4

    def body(db_ref, o_ref):
        def step(i, acc):
            parts = []
            for u in range(rows_per_pass):
                qi = i * rows_per_pass + u
                xv = db_ref[qi]
                hi = xv.astype(BF16)
                rest = xv - hi.astype(F32)
                mid = rest.astype(BF16)
                low = (rest - mid.astype(F32)).astype(BF16)
                parts.append(lax.dot_general(jnp.concatenate([hi, mid, low], axis=0), _rel_onehot(qi).astype(BF16), NT,
                                             preferred_element_type=F32))
            for part in parts:
                acc = acc + (part[0:h] + part[h:2 * h] + part[2 * h:3 * h])
            return acc

        o_ref[...] = lax.fori_loop(0, CHUNK // rows_per_pass, step, jnp.zeros((h, REL_LANES), F32))

    return pl.pallas_call(body, out_shape=SDS((h, REL_LANES), F32), compiler_params=_params(),
                          name="relbias_reduce")(dbias)


def _lower_bound(l_ref):
    l0, l1 = l_ref[0:1, :], l_ref[1:2, :]
    m = jnp.maximum(l0, l1)
    e0, e1 = jnp.exp(l0 - m), jnp.exp(l1 - m)
    return e0 / (e0 + e1)


def _tri(lower):
    r = lax.broadcasted_iota(jnp.int32, (CHUNK, CHUNK), 0)
    c = lax.broadcasted_iota(jnp.int32, (CHUNK, CHUNK), 1)
    return r >= c if lower else r <= c


def _hgrn_intra(qs, kk, b_s):
    rows = lax.broadcasted_iota(jnp.int32, (CHUNK, HEAD), 0)
    b = b_s[...]
    out = []
    for i in range(CHUNK // SUB):
        lo = i * SUB
        ref = jnp.zeros((1, HEAD), F32) if i == 0 else b_s[lo - 1:lo, :]
        eq = jnp.exp(b[lo:lo + SUB] - ref)
        qt = _split(qs[lo:lo + SUB] * eq)
        e = jnp.where(rows < lo + SUB, jnp.exp(jnp.minimum(ref - b, EXP_CLAMP)), 0.0)
        kt = _split(kk * e)
        out.append((eq, qt, e, kt))
    return out


def _hgrn_scores(blocks):
    tr = lax.broadcasted_iota(jnp.int32, (SUB, CHUNK), 0)
    tc = lax.broadcasted_iota(jnp.int32, (SUB, CHUNK), 1)
    return jnp.concatenate([jnp.where(tc <= tr + i * SUB, _dot3(qt, kt, NT), 0.0)
                            for i, (_, qt, _, kt) in enumerate(blocks)], axis=0)


def _hgrn_fwd(proj, lb_logits, gain, n_heads):
    t = proj.shape[0]
    nc = t // CHUNK
    da = n_heads * HEAD
    hp = MIX_HEADS
    wide = hp * HEAD

    def body(q_ref, f_ref, i_ref, g_ref, l_ref, gain_ref, y_ref, o_ref, st_ref, state, b_s):
        state[...] = jnp.zeros_like(state)
        lb_all = _lower_bound(l_ref)
        tril = _tri(True).astype(F32)

        def chunks(i, _):
            dot = functools.partial(lax.dot_general, preferred_element_type=F32)
            items = []
            for u in range(MIX_UNROLL):
                for hh in range(hp):
                    j = i * MIX_UNROLL + u
                    sl = pl.ds(pl.multiple_of(j * CHUNK, CHUNK), CHUNK)
                    cols = slice(hh * HEAD, (hh + 1) * HEAD)
                    lb = lb_all[:, cols]
                    fg = lb + (1.0 - lb) * _sigmoid(f_ref[sl, cols])
                    qv = q_ref[sl, cols]
                    gv = g_ref[sl, cols]
                    items.append(dict(hh=hh, j=j, sl=sl, cols=cols, lf=jnp.log(fg), kk=1.0 - fg, qs=qv * _sigmoid(qv),
                                      vb=i_ref[sl, cols].astype(BF16), gate=gv * _sigmoid(gv)))
            for it in items:
                it["b"] = _fdot(tril, it["lf"])
            for slot, it in enumerate(items):
                b = it["b"]
                b_s[slot] = b
                it["blocks"] = _hgrn_intra(it["qs"], it["kk"], b_s.at[slot])
                it["ebl"] = jnp.exp(b_s[slot, CHUNK - 1:CHUNK, :])
                it["qe"] = (it["qs"] * jnp.exp(b)).astype(BF16)
                it["kd"] = (it["kk"] * jnp.exp(b_s[slot, CHUNK - 1:CHUNK, :] - b)).astype(BF16)
            for it in items:
                it["a"] = _hgrn_scores(it["blocks"]).astype(BF16)
            for it in items:
                it["kv"] = dot(it["vb"], it["kd"], TN)
                it["o"] = dot(it["a"], it["vb"], NN)
            s_now = [state[hh] for hh in range(hp)]
            for it in items:
                it["s_in"] = s_now[it["hh"]]
                s_now[it["hh"]] = it["s_in"] * it["ebl"] + it["kv"]
            for hh in range(hp):
                state[hh] = s_now[hh]
            for it in items:
                it["o"] = it["o"] + dot(it["qe"], it["s_in"].astype(BF16), NT)
            for it in items:
                o, sl, cols = it["o"], it["sl"], it["cols"]
                st_ref[it["hh"], it["j"]] = it["s_in"]
                o_ref[sl, cols] = o
                rr = lax.rsqrt(jnp.mean(o * o, axis=-1, keepdims=True) + EPS)
                y_ref[sl, cols] = (o * rr * gain_ref[:, cols] * it["gate"]).astype(BF16)
            return 0

        assert nc % MIX_UNROLL == 0, (nc, MIX_UNROLL)
        lax.fori_loop(0, nc // MIX_UNROLL, chunks, 0)

    col = lambda k: pl.BlockSpec((t, wide), lambda h: (0, k * (n_heads // hp) + h))
    vec = pl.BlockSpec((1, wide), lambda h: (0, h))
    return pl.pallas_call(
        body, grid=(n_heads // hp,),
        in_specs=[col(0), col(1), col(2), col(3), pl.BlockSpec((2, wide), lambda h: (0, h)), vec],
        out_specs=[pl.BlockSpec((t, wide), lambda h: (0, h)), pl.BlockSpec((t, wide), lambda h: (0, h)),
                   pl.BlockSpec((hp, nc, HEAD, HEAD), lambda h: (h, 0, 0, 0))],
        out_shape=[SDS((t, da), BF16), SDS((t, da), F32), SDS((n_heads, nc, HEAD, HEAD), F32)],
        scratch_shapes=[pltpu.VMEM((hp, HEAD, HEAD), F32), pltpu.VMEM((hp * MIX_UNROLL, CHUNK, HEAD), F32)],
        compiler_params=_params(("parallel",)), name="hgrn_fwd")(proj, proj, proj, proj, lb_logits, gain)


def _write_column_groups(res, dproj_ref, sems, col0, stride, h, width):
    copies = [pltpu.make_async_copy(
        res.at[p], dproj_ref.at[:, pl.ds(pl.multiple_of((col0 + p * stride + h) * width, HEAD), width)], sems.at[p])
        for p in range(res.shape[0])]
    for cp in copies:
        cp.start()
    for cp in copies:
        cp.wait()


def _hgrn_bwd(dproj, proj, o_pre, states, dy, lb_logits, gain, n_heads, deps=()):
    t = proj.shape[0]
    nc = t // CHUNK
    da = n_heads * HEAD
    hp = MIX_HEADS
    wide = hp * HEAD

    def body(*refs):
        (q_ref, f_ref, i_ref, g_ref, o_ref, st_ref, dy_ref, l_ref, gain_ref,
         dproj_ref, dl_ref, dgain_ref, res, dstate, b_s, out_sems) = refs[1 + len(deps):]

        def compute():
            dstate[...] = jnp.zeros_like(dstate)
            lb_all = _lower_bound(l_ref)
            tril_m, tril, triu = _tri(True), _tri(True).astype(F32), _tri(False).astype(F32)
            last = lax.broadcasted_iota(jnp.int32, (CHUNK, HEAD), 0) == CHUNK - 1

            def chunks(i, carry):
                dot = functools.partial(lax.dot_general, preferred_element_type=F32)
                items = []
                for u in range(MIX_UNROLL_BWD):
                    for hh in range(hp):
                        j = nc - 1 - (i * MIX_UNROLL_BWD + u)
                        sl = pl.ds(pl.multiple_of(j * CHUNK, CHUNK), CHUNK)
                        cols = slice(hh * HEAD, (hh + 1) * HEAD)
                        lb, gain_v = lb_all[:, cols], gain_ref[:, cols]
                        sg = _sigmoid(f_ref[sl, cols])
                        fg = lb + (1.0 - lb) * sg
                        qv = q_ref[sl, cols]
                        sq = _sigmoid(qv)
                        gv = g_ref[sl, cols]
                        sgg = _sigmoid(gv)
                        silg = gv * sgg
                        o = o_ref[sl, cols]
                        dyv = dy_ref[sl, cols]
                        rr = lax.rsqrt(jnp.mean(o * o, axis=-1, keepdims=True) + EPS)
                        on = o * rr
                        don = dyv * gain_v * silg
                        do = (rr * don - o * (rr * rr * rr) * jnp.mean(don * o, axis=-1, keepdims=True)).astype(BF16)
                        items.append(dict(
                            hh=hh, j=j, sl=sl, cols=cols, lb=lb, sg=sg, fg=fg, kk=1.0 - fg, qv=qv, sq=sq, qs=qv * sq,
                            vb=i_ref[sl, cols].astype(BF16), do=do, dg=dyv * on * gain_v * _dsilu(gv, sgg),
                            dgain=jnp.sum(dyv * on * silg, axis=0, keepdims=True)))
                for it in items:
                    it["b"] = _fdot(tril, jnp.log(it["fg"]))
                for slot, it in enumerate(items):
                    b = it["b"]
                    b_s[slot] = b
                    it["blocks"] = _hgrn_intra(it["qs"], it["kk"], b_s.at[slot])
                    bl = b_s[slot, CHUNK - 1:CHUNK, :]
                    it["eb"], it["ebl"], it["ekd"] = jnp.exp(b), jnp.exp(bl), jnp.exp(bl - b)
                    it["s_in"] = st_ref[it["hh"], it["j"]]
                for it in items:
                    it["a"] = _hgrn_scores(it["blocks"]).astype(BF16)
                    it["da"] = jnp.where(tril_m, dot(it["do"], it["vb"], NT), 0.0)
                for it in items:
                    dq_rows = []
                    dk = jnp.zeros((CHUNK, HEAD), F32)
                    for blk, (eq, qt, e, kt) in enumerate(it["blocks"]):
                        da_i = _split(it["da"][blk * SUB:(blk + 1) * SUB])
                        dq_rows.append(eq * _dot3(da_i, kt, NN))
                        dk = dk + e * _dot3(da_i, qt, TN)
                    it["dq"] = jnp.concatenate(dq_rows, axis=0) + dot(it["do"], it["s_in"].astype(BF16), NN) * it["eb"]
                    it["dk"] = dk
                    it["dv"] = dot(it["a"], it["do"], TN)
                    it["g"] = dot(it["do"], (it["qs"] * it["eb"]).astype(BF16), TN)
                ds_now = [dstate[hh] for hh in range(hp)]
                for it in items:
                    it["ds_out"] = ds_now[it["hh"]]
                    ds_now[it["hh"]] = it["ds_out"] * it["ebl"] + it["g"]
                for hh in range(hp):
                    dstate[hh] = ds_now[hh]
                for it in items:
                    dsb = it["ds_out"].astype(BF16)
                    it["dv"] = it["dv"] + dot((it["kk"] * it["ekd"]).astype(BF16), dsb, NT)
                    it["dk_state"] = it["ekd"] * dot(it["vb"], dsb, NN)
                for it in items:
                    kk, dk_state = it["kk"], it["dk_state"]
                    it["dk"] = it["dk"] + dk_state
                    extra = (jnp.sum(kk * dk_state, axis=0, keepdims=True)
                             + it["ebl"] * jnp.sum(it["s_in"] * it["ds_out"], axis=0, keepdims=True))
                    it["db"] = it["qs"] * it["dq"] - kk * it["dk"] + jnp.where(last, extra, 0.0)
                for it in items:
                    it["dlf"] = _fdot(triu, it["db"])
                carry = list(carry)
                for it in items:
                    hh, sl, cols, sg, lb = it["hh"], it["sl"], it["cols"], it["sg"], it["lb"]
                    dfg = it["dlf"] / it["fg"] - it["dk"]
                    dlb_acc, dgain_acc = carry[hh]
                    carry[hh] = (dlb_acc + jnp.sum(dfg * (1.0 - sg), axis=0, keepdims=True), dgain_acc + it["dgain"])
                    res[0, sl, cols] = (it["dq"] * _dsilu(it["qv"], it["sq"])).astype(BF16)
                    res[1, sl, cols] = (dfg * (1.0 - lb) * sg * (1.0 - sg)).astype(BF16)
                    res[2, sl, cols] = it["dv"].astype(BF16)
                    res[3, sl, cols] = it["dg"].astype(BF16)
                return tuple(carry)

            assert nc % MIX_UNROLL_BWD == 0, (nc, MIX_UNROLL_BWD)
            zero = jnp.zeros((1, HEAD), F32)
            sums = lax.fori_loop(0, nc // MIX_UNROLL_BWD, chunks, ((zero, zero),) * hp)
            for hh, (dlb, dgain) in enumerate(sums):
                cols = slice(hh * HEAD, (hh + 1) * HEAD)
                lb = lb_all[:, cols]
                dgain_ref[:, cols] = dgain
                dl0 = dlb * lb * (1.0 - lb)
                dl_ref[0:1, cols] = dl0
                dl_ref[1:2, cols] = -dl0

        compute()
        _write_column_groups(res, dproj_ref, out_sems, 0, ng, pl.program_id(0), wide)

    ng = n_heads // hp
    col = lambda k: pl.BlockSpec((t, wide), lambda h: (0, k * ng + h))
    head = pl.BlockSpec((t, wide), lambda h: (0, h))
    vec = pl.BlockSpec((1, wide), lambda h: (0, h))
    return pl.pallas_call(
        body, grid=(ng,),
        in_specs=[ANY] * (1 + len(deps)) + [col(0), col(1), col(2), col(3), head,
                  pl.BlockSpec((hp, nc, HEAD, HEAD), lambda h: (h, 0, 0, 0)),
                  head, pl.BlockSpec((2, wide), lambda h: (0, h)), vec],
        out_specs=[ANY, pl.BlockSpec((2, wide), lambda h: (0, h)), vec],
        out_shape=[SDS(dproj.shape, BF16), SDS((2, da), F32), SDS((1, da), F32)],
        scratch_shapes=[pltpu.VMEM((4, t, wide), BF16), pltpu.VMEM((hp, HEAD, HEAD), F32),
                        pltpu.VMEM((hp * MIX_UNROLL_BWD, CHUNK, HEAD), F32), pltpu.SemaphoreType.DMA((4,))],
        input_output_aliases={0: 0}, compiler_params=_params(("arbitrary",)),
        name="hgrn_bwd")(dproj, *deps, proj, proj, proj, proj, o_pre, states, dy, lb_logits, gain)


ROWS = 256


def _head_norm(x_ref, gain, dst, dst_off, t):
    def step(i, _):
        sl = pl.ds(pl.multiple_of(i * ROWS, ROWS), ROWS)
        xv = x_ref[sl, :]
        r = lax.rsqrt(jnp.mean(xv * xv, axis=-1, keepdims=True) + EPS)
        dst[pl.ds(pl.multiple_of(dst_off + i * ROWS, ROWS), ROWS), :] = (xv * r * gain).astype(BF16)
        return 0

    lax.fori_loop(0, t // ROWS, step, 0)


def _head_norm_bwd(x_ref, gain, dn_ref, dn_off, out, slot, t):
    def step(i, acc):
        sl = pl.ds(pl.multiple_of(i * ROWS, ROWS), ROWS)
        xv = x_ref[sl, :]
        dn = dn_ref[pl.ds(pl.multiple_of(dn_off + i * ROWS, ROWS), ROWS), :]
        r = lax.rsqrt(jnp.mean(xv * xv, axis=-1, keepdims=True) + EPS)
        u = dn * gain
        out[slot, sl, :] = (r * u - xv * (r * r * r) * jnp.mean(u * xv, axis=-1, keepdims=True)).astype(out.dtype)
        return acc + jnp.sum(dn * (xv * r), axis=0, keepdims=True)

    return lax.fori_loop(0, t // ROWS, step, jnp.zeros((1, HEAD), F32))


def _attn_scores(qn, kpad, n):
    qc = qn[pl.ds(pl.multiple_of(n * CHUNK, CHUNK), CHUNK), :]
    band = pl.ds(pl.multiple_of(n * CHUNK, CHUNK), BAND)
    return qc, band, lax.dot_general(qc, kpad[band, :], NT, preferred_element_type=F32)


def _attn_softmax(raw, bias_ref, n):
    s = raw * (HEAD ** -0.5) + bias_ref[0]
    col = lax.broadcasted_iota(jnp.int32, (CHUNK, BAND), 1)
    s = jnp.where(col >= PAD - n * CHUNK, s, -jnp.inf)
    p = jnp.exp(s - jnp.max(s, axis=-1, keepdims=True))
    return p / jnp.sum(p, axis=-1, keepdims=True)


def _attn_fwd(proj, q_gain, k_gain, bias, n_heads, col0):
    t = proj.shape[0]
    nc = t // CHUNK

    def body(q_ref, k_ref, v_ref, qg_ref, kg_ref, bias_ref, y_ref, qn, kpad, vpad):
        kpad[0:PAD, :] = jnp.zeros((PAD, HEAD), BF16)
        vpad[0:PAD, :] = jnp.zeros((PAD, HEAD), BF16)
        _head_norm(q_ref, qg_ref[...], qn, 0, t)
        _head_norm(k_ref, kg_ref[...], kpad, PAD, t)

        def copy_v(i, _):
            vpad[pl.ds(pl.multiple_of(PAD + i * ROWS, ROWS), ROWS), :] = v_ref[
                pl.ds(pl.multiple_of(i * ROWS, ROWS), ROWS), :].astype(BF16)
            return 0

        lax.fori_loop(0, t // ROWS, copy_v, 0)

        def chunks(i, _):
            ns = [i * ATT_UNROLL + u for u in range(ATT_UNROLL)]
            scored = [_attn_scores(qn, kpad, n) for n in ns]
            probs = [_attn_softmax(raw, bias_ref, n).astype(BF16) for n, (_, _, raw) in zip(ns, scored)]
            outs = [lax.dot_general(p, vpad[band, :], NN, preferred_element_type=F32).astype(BF16)
                    for p, (_, band, _) in zip(probs, scored)]
            for n, o in zip(ns, outs):
                y_ref[pl.ds(pl.multiple_of(n * CHUNK, CHUNK), CHUNK), :] = o
            return 0

        assert nc % ATT_UNROLL == 0, (nc, ATT_UNROLL)
        lax.fori_loop(0, nc // ATT_UNROLL, chunks, 0)

    col = lambda k: pl.BlockSpec((t, HEAD), lambda h: (0, col0 + k * n_heads + h))
    vec = pl.BlockSpec((1, HEAD), lambda h: (0, 0))
    return pl.pallas_call(
        body, grid=(n_heads,),
        in_specs=[col(0), col(1), col(2), vec, vec, pl.BlockSpec((1, CHUNK, BAND), lambda h: (h, 0, 0))],
        out_specs=pl.BlockSpec((t, HEAD), lambda h: (0, h)), out_shape=SDS((t, n_heads * HEAD), BF16),
        scratch_shapes=[pltpu.VMEM((t, HEAD), BF16), pltpu.VMEM((t + PAD, HEAD), BF16), pltpu.VMEM((t + PAD, HEAD), BF16)],
        compiler_params=_params(("parallel",)), name="attn_fwd")(proj, proj, proj, q_gain, k_gain, bias)


def _attn_bwd(dproj, proj, q_gain, k_gain, bias, dy, n_heads, col0, deps=()):
    t = proj.shape[0]
    nc = t // CHUNK

    def body(*refs):
        (q_ref, k_ref, v_ref, qg_ref, kg_ref, bias_ref, dy_ref,
         dproj_ref, dbias_ref, dqg_ref, dkg_ref, qn, kpad, vpad, dqn, dk_acc, dv_acc, res,
         out_sems) = refs[1 + len(deps):]
        h = pl.program_id(0)

        def compute():
            kpad[0:PAD, :] = jnp.zeros((PAD, HEAD), BF16)
            vpad[0:PAD, :] = jnp.zeros((PAD, HEAD), BF16)
            _head_norm(q_ref, qg_ref[...], qn, 0, t)
            _head_norm(k_ref, kg_ref[...], kpad, PAD, t)

            def prep(i, _):
                sl = pl.ds(pl.multiple_of(PAD + i * ROWS, ROWS), ROWS)
                vpad[sl, :] = v_ref[pl.ds(pl.multiple_of(i * ROWS, ROWS), ROWS), :].astype(BF16)
                return 0

            lax.fori_loop(0, t // ROWS, prep, 0)

            def clear(i, _):
                sl = pl.ds(pl.multiple_of(i * ROWS, ROWS), ROWS)
                dk_acc[sl, :] = jnp.zeros((ROWS, HEAD), F32)
                dv_acc[sl, :] = jnp.zeros((ROWS, HEAD), F32)
                return 0

            lax.fori_loop(0, (t + PAD) // ROWS, clear, 0)
            dbias_ref[0] = jnp.zeros((CHUNK, BAND), F32)

            def chunks(i, _):
                dot = functools.partial(lax.dot_general, preferred_element_type=F32)
                ns = [i * ATT_UNROLL_BWD + u for u in range(ATT_UNROLL_BWD)]
                scored = [_attn_scores(qn, kpad, n) for n in ns]
                dos = [dy_ref[pl.ds(pl.multiple_of(n * CHUNK, CHUNK), CHUNK), :].astype(BF16) for n in ns]
                dps = [dot(do, vpad[band, :], NT) for do, (_, band, _) in zip(dos, scored)]
                ps, dss = [], []
                for n, (_, _, raw), dp in zip(ns, scored, dps):
                    p = _attn_softmax(raw, bias_ref, n)
                    ds = p * (dp - jnp.sum(dp * p, axis=-1, keepdims=True))
                    dbias_ref[0] += ds
                    ps.append(p.astype(BF16))
                    dss.append((ds * (HEAD ** -0.5)).astype(BF16))
                dqs = [dot(d, kpad[band, :], NN) for d, (_, band, _) in zip(dss, scored)]
                dks = [dot(d, qc, TN) for d, (qc, _, _) in zip(dss, scored)]
                dvs = [dot(p, do, TN) for p, do in zip(ps, dos)]
                for n, (_, band, _), dq, dk, dv in zip(ns, scored, dqs, dks, dvs):
                    dqn[pl.ds(pl.multiple_of(n * CHUNK, CHUNK), CHUNK), :] = dq
                    dk_acc[band, :] += dk
                    dv_acc[band, :] += dv
                return 0

            assert nc % ATT_UNROLL_BWD == 0, (nc, ATT_UNROLL_BWD)
            lax.fori_loop(0, nc // ATT_UNROLL_BWD, chunks, 0)
            dqg = _head_norm_bwd(q_ref, qg_ref[...], dqn, 0, res, 0, t)
            dkg = _head_norm_bwd(k_ref, kg_ref[...], dk_acc, PAD, res, 1, t)

            def put_v(i, _):
                sl = pl.ds(pl.multiple_of(i * ROWS, ROWS), ROWS)
                res[2, sl, :] = dv_acc[pl.ds(pl.multiple_of(PAD + i * ROWS, ROWS), ROWS), :].astype(BF16)
                return 0

            lax.fori_loop(0, t // ROWS, put_v, 0)

            @pl.when(h == 0)
            def _():
                dqg_ref[...] = jnp.zeros_like(dqg_ref)
                dkg_ref[...] = jnp.zeros_like(dkg_ref)

            dqg_ref[...] += dqg
            dkg_ref[...] += dkg

        compute()
        _write_column_groups(res, dproj_ref, out_sems, col0, n_heads, h, HEAD)

    col = lambda k: pl.BlockSpec((t, HEAD), lambda h: (0, col0 + k * n_heads + h))
    vec = pl.BlockSpec((1, HEAD), lambda h: (0, 0))
    btile = pl.BlockSpec((1, CHUNK, BAND), lambda h: (h, 0, 0))
    return pl.pallas_call(
        body, grid=(n_heads,),
        in_specs=[ANY] * (1 + len(deps)) + [col(0), col(1), col(2), vec, vec, btile,
                                            pl.BlockSpec((t, HEAD), lambda h: (0, h))],
        out_specs=[ANY, btile, vec, vec],
        out_shape=[SDS(dproj.shape, BF16), SDS((n_heads, CHUNK, BAND), F32), SDS((1, HEAD), F32), SDS((1, HEAD), F32)],
        scratch_shapes=[pltpu.VMEM((t, HEAD), BF16), pltpu.VMEM((t + PAD, HEAD), BF16), pltpu.VMEM((t + PAD, HEAD), BF16),
                        pltpu.VMEM((t, HEAD), F32), pltpu.VMEM((t + PAD, HEAD), F32), pltpu.VMEM((t + PAD, HEAD), F32),
                        pltpu.VMEM((3, t, HEAD), BF16), pltpu.SemaphoreType.DMA((3,))],
        input_output_aliases={0: 0}, compiler_params=_params(("arbitrary",)),
        name="attn_bwd")(dproj, *deps, proj, proj, proj, q_gain, k_gain, bias, dy)


def _place():
    x, y, c = lax.axis_index("x"), lax.axis_index("y"), lax.axis_index("c")
    others = [(1 - x, y), (x, 1 - y), (1 - x, 1 - y)]
    return x, y, c, others


def _chunk_of(ref, kind, chip, half, shard_shape):
    r, n = shard_shape
    hr = r // 2
    if kind == "col":
        rows = pl.ds(0, r) if half is None else pl.ds(half * hr, hr)
        return ref.at[rows, pl.ds(chip * n, n)]
    rows = pl.ds(chip * r, r) if half is None else pl.ds(chip * r + half * hr, hr)
    return ref.at[rows, :]


EFFECT = pltpu.SideEffectType.DATAFLOW_SIDE_EFFECTING


def _start_copies(name, bufs, plan, n, deps):
    nb, nd = len(bufs), len(deps)

    def body(*refs):
        send, recv, token = refs[nb + nd], refs[nb + nd + 1], refs[-1]
        for cp in plan(refs[:nb], send, recv)[0]:
            cp.start()
        token[...] = jnp.zeros_like(token)

    out = pl.pallas_call(
        body, name=name,
        out_shape=(pltpu.SemaphoreType.DMA((n,)), pltpu.SemaphoreType.DMA((n,)),
                   *[pltpu.HBM(b.shape, b.dtype) for b in bufs], SDS((8, 128), F32)),
        in_specs=[HBM] * nb + [ANY] * nd,
        out_specs=(SEM, SEM, *[HBM] * nb, pl.BlockSpec(memory_space=pltpu.VMEM)),
        input_output_aliases={i: 2 + i for i in range(nb)},
        compiler_params=pltpu.CompilerParams(has_side_effects=EFFECT),
    )(*[pltpu.with_memory_space_constraint(b, pltpu.HBM) for b in bufs], *deps)
    return out[0], out[1], list(out[2:2 + nb]), out[-1]


def _wait_copies(name, bufs, send, recv, plan, after):
    nb = len(bufs)

    def body(*refs):
        sends, recvs = plan(refs[:nb], refs[nb], refs[nb + 1])
        for cp in sends:
            cp.wait_send()
        for cp in recvs:
            cp.wait_recv()

    out = pl.pallas_call(
        body, name=name, out_shape=tuple(pltpu.HBM(b.shape, b.dtype) for b in bufs),
        in_specs=[HBM] * nb + [SEM, SEM] + [ANY] * len(after), out_specs=tuple([HBM] * nb),
        input_output_aliases={i: i for i in range(nb)},
        compiler_params=pltpu.CompilerParams(has_side_effects=EFFECT),
    )(*bufs, send, recv, *after)
    return list(out)


def _remote(src, dst, send, recv, i, dev):
    return pltpu.make_async_remote_copy(src_ref=src, dst_ref=dst, send_sem=send.at[i], recv_sem=recv.at[i],
                                        device_id=dev, device_id_type=MESH)


ALL_RELATIONS = (0, 1, 2)


def _plan_gather_ici(kinds, shapes, rels=ALL_RELATIONS):
    def plan(refs, send, recv):
        x, y, c, others = _place()
        sends, recvs = [], []
        for w, (kind, ss) in enumerate(zip(kinds, shapes)):
            for p in rels:
                px, py = others[p]
                mine = _chunk_of(refs[w], kind, 2 * x + y, c, ss)
                theirs = _chunk_of(refs[w], kind, 2 * px + py, c, ss)
                sends.append(_remote(mine, mine, send, recv, 3 * w + p, (px, py, c)))
                recvs.append(_remote(theirs, theirs, send, recv, 3 * w + p, (px, py, c)))
        return sends, recvs

    return plan, 3 * len(kinds)


def _plan_gather_pass(kinds, shapes, rels=ALL_RELATIONS):
    def plan(refs, send, recv):
        x, y, c, others = _place()
        sends, recvs = [], []
        for w, (kind, ss) in enumerate(zip(kinds, shapes)):
            for i, p in enumerate(rels):
                px, py = others[p]
                got = _chunk_of(refs[w], kind, 2 * px + py, c, ss)
                coming = _chunk_of(refs[w], kind, 2 * px + py, 1 - c, ss)
                sends.append(_remote(got, got, send, recv, len(rels) * w + i, (x, y, 1 - c)))
                recvs.append(_remote(coming, coming, send, recv, len(rels) * w + i, (x, y, 1 - c)))
        return sends, recvs

    return plan, len(rels) * len(kinds)


def _plan_pair(kinds, shapes):
    nw = len(kinds)

    def plan(refs, send, recv):
        x, y, c, _ = _place()
        sends = []
        for w, (kind, ss) in enumerate(zip(kinds, shapes)):
            for k in range(4):
                sends.append(_remote(_chunk_of(refs[w], kind, k, 1 - c, ss), refs[nw + w].at[k], send, recv,
                                     4 * w + k, (x, y, 1 - c)))
        return sends, sends

    return plan, 4 * nw


def _plan_chip(nw):
    def plan(refs, send, recv):
        x, y, c, others = _place()
        sends = []
        for w in range(nw):
            for p, (px, py) in enumerate(others):
                sends.append(_remote(refs[w].at[p], refs[nw + w].at[p], send, recv, 3 * w + p, (px, py, c)))
        return sends, sends

    return plan, 3 * nw


def _plan_share(slabs):
    def plan(refs, send, recv):
        x, y, c, _ = _place()
        sends, recvs, i = [], [], 0
        for w, ns in enumerate(slabs):
            for s in range(ns):
                sends.append(_remote(refs[w].at[s, c], refs[w].at[s, c], send, recv, i, (x, y, 1 - c)))
                recvs.append(_remote(refs[w].at[s, 1 - c], refs[w].at[s, 1 - c], send, recv, i, (x, y, 1 - c)))
                i += 1
        return sends, recvs

    return plan, sum(slabs)


def _grad_half_spec(kind, tr, tn, nr, nn, chunk):
    if kind == "col":
        return pl.BlockSpec((tr, tn), lambda *a: (a[-1][1] * nr + a[-3], chunk(*a) * nn + a[-2]))
    return pl.BlockSpec((tr, tn), lambda *a: ((2 * chunk(*a) + a[-1][1]) * nr + a[-3], a[-2]))


def _pair_add(grad, got, kind, shard_shape, pos, name):
    r, n = shard_shape
    hr = r // 2
    tr, tn = _tile(hr, 256, 16), _tile(n, 1408)
    nr, nn = hr // tr, n // tn
    g_spec = _grad_half_spec(kind, tr, tn, nr, nn, lambda p, i, j, pos_: pos_[2 + p])
    r_spec = pl.BlockSpec((1, tr, tn), lambda p, i, j, pos_: (pos_[2 + p], i, j))
    o_spec = pl.BlockSpec((1, tr, tn), lambda p, i, j, pos_: (p, i, j))

    def body(pos_ref, g_ref, r_ref, o_ref):
        o_ref[0] = (g_ref[...] + r_ref[0]).astype(BF16)

    return pl.pallas_call(
        body,
        grid_spec=pltpu.PrefetchScalarGridSpec(num_scalar_prefetch=1, grid=(3, nr, nn), in_specs=[g_spec, r_spec],
                                               out_specs=o_spec),
        out_shape=SDS((3, hr, n), BF16),
        compiler_params=_params(("parallel", "parallel", "parallel")), name=name)(pos, grad, got)


def _chip_add(grad, got, got16, kind, shard_shape, pos, name, slab=0, slabs=1, prev=None):
    r, n = shard_shape
    hr = r // 2
    tr, tn = _tile(hr, 256, 16), _tile(n, 1408)
    nr, nn = hr // tr, n // tn
    g_spec = _grad_half_spec(kind, tr, tn, nr, nn, lambda i, j, pos_: pos_[0])
    r_spec = pl.BlockSpec((1, tr, tn), lambda i, j, pos_: (pos_[0], i, j))
    oth = pl.BlockSpec((3, tr, tn), lambda i, j, pos_: (0, i, j))

    def body(pos_ref, g_ref, r_ref, oth_ref, *rest):
        own = g_ref[...] + r_ref[0]
        rest[-1][0, 0] = ((own + oth_ref[0].astype(F32)) + oth_ref[1].astype(F32)) + oth_ref[2].astype(F32)

    return pl.pallas_call(
        body,
        grid_spec=pltpu.PrefetchScalarGridSpec(
            num_scalar_prefetch=1, grid=(nr, nn), in_specs=[g_spec, r_spec, oth] + [ANY] * (prev is not None),
            out_specs=pl.BlockSpec((1, 1, tr, tn), lambda i, j, pos_: (slab, pos_[1], i, j))),
        out_shape=SDS((slabs, 2, hr, n), F32), input_output_aliases={4: 0} if prev is not None else {},
        compiler_params=_params(("parallel", "parallel")),
        name=name)(pos, grad, got, got16, *(() if prev is None else (prev,)))


def _adamw_math(w, g, m, v):
    m = ADAM_B1 * m + (1.0 - ADAM_B1) * g
    v = ADAM_B2 * v + (1.0 - ADAM_B2) * (g * g)
    m_hat = m / (1.0 - ADAM_B1 ** ADAM_STEP)
    v_hat = v / (1.0 - ADAM_B2 ** ADAM_STEP)
    return -ADAM_LR * (m_hat / (jnp.sqrt(v_hat) + ADAM_EPS) + ADAM_WD * w), m, v


def _adamw(w, g, m, v, name):
    r, n = w.shape
    tr, tn = _tile(r, 256, 16), _tile(n, 1408)

    def body(w_ref, g_ref, m_ref, v_ref, d_ref, nm_ref, nv_ref, go_ref):
        gv = g_ref[...]
        d_ref[...], nm_ref[...], nv_ref[...] = _adamw_math(w_ref[...], gv, m_ref[...], v_ref[...])
        go_ref[...] = gv

    tile = pl.BlockSpec((tr, tn), lambda i, j: (i, j))
    return pl.pallas_call(
        body, grid=(r // tr, n // tn), in_specs=[tile] * 4, out_specs=[tile] * 4, out_shape=[SDS((r, n), F32)] * 4,
        compiler_params=_params(("parallel", "parallel")), name=name)(w, g, m, v)


def _small_allreduce_adamw(g, w, m, v, deps=()):
    length = g.shape[1]

    def body(*refs):
        g_ref, w_ref, m_ref, v_ref = refs[:4]
        gs_ref, d_ref, nm_ref, nv_ref, buf, send, recv = refs[4 + len(deps):]
        x, y, c = lax.axis_index("x"), lax.axis_index("y"), lax.axis_index("c")
        me = 4 * x + 2 * y + c
        buf[me] = g_ref[...]
        cps = []
        for d in range(1, 8):
            peer = (x ^ (d >> 2), y ^ ((d >> 1) & 1), c ^ (d & 1))
            cp = pltpu.make_async_remote_copy(src_ref=buf.at[me], dst_ref=buf.at[me], send_sem=send.at[d - 1],
                                              recv_sem=recv.at[d - 1], device_id=peer, device_id_type=MESH)
            cp.start()
            cps.append(cp)
        for cp in cps:
            cp.wait()
        total = buf[0]
        for d in range(1, 8):
            total = total + buf[d]
        gs_ref[...] = total
        d_ref[...], nm_ref[...], nv_ref[...] = _adamw_math(w_ref[...], total, m_ref[...], v_ref[...])

    vm = pl.BlockSpec(memory_space=pltpu.VMEM)
    return pl.pallas_call(
        body, in_specs=[vm] * 4 + [ANY] * len(deps), out_specs=[vm] * 4, out_shape=[SDS((1, length), F32)] * 4,
        scratch_shapes=[pltpu.VMEM((8, 1, length), F32), pltpu.SemaphoreType.DMA((7,)), pltpu.SemaphoreType.DMA((7,))],
        compiler_params=pltpu.CompilerParams(has_side_effects=True), name="small_allreduce_adamw")(g, w, m, v, *deps)


def kernel(x, w_in, b_gate, norm_mix, norm_ffn, hgrn_lb_logits, hgrn_out_gain, q_gain, k_gain, rel_bias, w_proj_a, w_proj_b, w_out, w_ffn_in, w_ffn_out, loss_target, m_w_in, m_b_gate, m_norm_mix, m_norm_ffn, m_hgrn_lb_logits, m_hgrn_out_gain, m_q_gain, m_k_gain, m_rel_bias, m_w_proj_a, m_w_proj_b, m_w_out, m_w_ffn_in, m_w_ffn_out, v_w_in, v_b_gate, v_norm_mix, v_norm_ffn, v_hgrn_lb_logits, v_hgrn_out_gain, v_q_gain, v_k_gain, v_rel_bias, v_w_proj_a, v_w_proj_b, v_w_out, v_w_ffn_in, v_w_ffn_out):
    t, d = x.shape[1], x.shape[2]
    d_a = hgrn_out_gain.shape[1]
    h_a = d_a // HEAD
    h_b = rel_bias.shape[1]
    d_b = h_b * HEAD
    x0 = x.reshape(t, d)
    target = loss_target.reshape(t, d)
    ax, ay = lax.axis_index("x"), lax.axis_index("y")
    pos = jnp.stack([2 * ax + ay, lax.axis_index("c"), 2 * (1 - ax) + ay, 2 * ax + 1 - ay,
                     2 * (1 - ax) + 1 - ay]).astype(jnp.int32)

    names = ["w_in", "w_proj_a", "w_proj_b", "w_out", "w_ffn_in", "w_ffn_out"]
    big = dict(zip(names, [w_in[0], w_proj_a[0], w_proj_b[0], w_out[0], w_ffn_in[0], w_ffn_out[0]]))
    big_m = dict(zip(names, [m_w_in[0], m_w_proj_a[0], m_w_proj_b[0], m_w_out[0], m_w_ffn_in[0], m_w_ffn_out[0]]))
    big_v = dict(zip(names, [v_w_in[0], v_w_proj_a[0], v_w_proj_b[0], v_w_out[0], v_w_ffn_in[0], v_w_ffn_out[0]]))
    kind = dict(zip(names, ["col", "col", "col", "row", "col", "row"]))
    shape = {nm: big[nm].shape for nm in names}

    def cast(group):
        return [_cast_into_full(big[g], kind[g], pos, "cast_" + g) for g in group]

    def gather_start(tag, group, fulls, deps, rels=ALL_RELATIONS):
        plan, n = _plan_gather_ici([kind[g] for g in group], [shape[g] for g in group], rels)
        send, recv, bufs, token = _start_copies("gather_ici_start_" + tag, fulls, plan, n, deps)
        return (tag, group, plan, send, recv, bufs), token

    def gather_pass(state, after, rels=ALL_RELATIONS, part=""):
        tag, group, _, send, recv, bufs = state
        kinds_, shapes_ = [kind[g] for g in group], [shape[g] for g in group]
        bufs = _wait_copies("gather_ici_wait_" + tag + part, bufs, send, recv,
                            _plan_gather_ici(kinds_, shapes_, rels)[0], after)
        plan, n = _plan_gather_pass(kinds_, shapes_, rels)
        send2, recv2, bufs, token = _start_copies("gather_pass_start_" + tag + part, bufs, plan, n, ())
        return (tag + part, group, plan, send2, recv2, bufs), token

    def gather_done(state, after):
        tag, group, plan, send, recv, bufs = state
        return _wait_copies("gather_pass_wait_" + tag, bufs, send, recv, plan, after)

    def reduce_start(tag, group, grads, deps):
        plan, n = _plan_pair([kind[g] for g in group], [shape[g] for g in group])
        lands = [lax.empty((4, shape[g][0] // 2, shape[g][1]), F32) for g in group]
        send, recv, bufs, token = _start_copies("pair_start_" + tag, list(grads) + lands, plan, n, deps)
        return dict(tag=tag, group=group, plan=plan, send=send, recv=recv, bufs=bufs), token

    def reduce_pair_done(st, after):
        tag, group, nw = st["tag"], st["group"], len(st["group"])
        bufs = _wait_copies("pair_wait_" + tag, st["bufs"], st["send"], st["recv"], st["plan"], after)
        grads, gots = bufs[:nw], bufs[nw:]
        parts = [_pair_add(g, l, kind[nm], shape[nm], pos, "pair_add_" + nm) for g, l, nm in zip(grads, gots, group)]
        lands = [lax.empty((3, shape[g][0] // 2, shape[g][1]), BF16) for g in group]
        plan, n = _plan_chip(nw)
        send, recv, bufs, token = _start_copies("chip_start_" + tag, parts + lands, plan, n, ())
        return dict(st, plan=plan, send=send, recv=recv, bufs=bufs, grads=grads, gots=gots), token

    def reduce_chip_wait(st, after, slab=0, slabs=1, prev=None):
        tag, group, nw = st["tag"], st["group"], len(st["group"])
        bufs = _wait_copies("chip_wait_" + tag, st["bufs"], st["send"], st["recv"], st["plan"], after)
        return [_chip_add(g, l, got16, kind[nm], shape[nm], pos, "chip_add_" + nm, slab, slabs, prev)
                for g, l, got16, nm in zip(st["grads"], st["gots"], bufs[nw:], group)]

    def reduce_share(tag, group, finals, slabs=1):
        plan, n = _plan_share([slabs] * len(finals))
        send, recv, bufs, token = _start_copies("share_start_" + tag, finals, plan, n, ())
        return dict(tag=tag, group=group, plan=plan, send=send, recv=recv, bufs=bufs), token

    def reduce_chip_done(st, after):
        return reduce_share(st["tag"], st["group"], reduce_chip_wait(st, after))

    g_big, upd = {}, {}

    def reduce_finish(st, after):
        bufs = _wait_copies("share_wait_" + st["tag"], st["bufs"], st["send"], st["recv"], st["plan"], after)
        for full, nm in zip(bufs, st["group"]):
            upd[nm] = _adamw(big[nm], full.reshape(shape[nm]), big_m[nm], big_v[nm], "adamw_" + nm)
            g_big[nm] = upd[nm][3]

    g_in = ["w_in"]
    near, token = gather_start("a_near", g_in, cast(g_in), (), rels=(0, 1))
    fulls_b, fulls_c, fulls_d = cast(["w_proj_a", "w_proj_b", "w_out"]), cast(["w_ffn_in"]), cast(["w_ffn_out"])
    h1, r1 = _rmsnorm_fwd(x0, norm_mix, "rmsnorm_mix")
    rb = jnp.pad(rel_bias[0], ((0, 0), (0, REL_LANES - N_REL)))
    bias = _relbias_expand(rb).transpose(1, 0, 2)
    proj = _matmul_chunks(h1, big["w_in"], (0,), None, pos, "proj_in_own", own_shard=True)
    bufs = _wait_copies("gather_ici_wait_a_near", near[5], near[3], near[4], near[2],
                        (h1, bias, proj, token, *fulls_b, *fulls_c, *fulls_d))
    far, token = gather_start("a_far", g_in, bufs, (), rels=(2,))
    gb, token = gather_start("b", ["w_proj_a", "w_proj_b", "w_out"], fulls_b, (token,))
    gc, token = gather_start("c", ["w_ffn_in"], fulls_c, (token,))
    gd, token = gather_start("d", ["w_ffn_out"], fulls_d, (token,))
    plan, n = _plan_gather_pass(["col"], [shape["w_in"]], (0, 1))
    send, recv, bufs, _ = _start_copies("gather_pass_start_a_near", far[5], plan, n, (token,))
    bufs = _wait_copies("gather_pass_wait_a_near", bufs, send, recv, plan, ())
    proj = _matmul_chunks(h1, bufs[0], (2, 3), proj, pos, "proj_in_near")
    ga, token = gather_pass(far[:5] + (bufs,), (proj,), rels=(2,))
    (wg_in,) = gather_done(ga, ())
    proj = _matmul_chunks(h1, wg_in, (4,), proj, pos, "proj_in_far")
    y_a, o_pre, states = _hgrn_fwd(proj, hgrn_lb_logits, hgrn_out_gain, h_a)
    gb, token = gather_pass(gb, (y_a,))
    col_b = 4 * d_a // HEAD
    y_b = _attn_fwd(proj, q_gain, k_gain, bias, h_b, col_b)
    wg_pa, wg_pb, wg_out = gather_done(gb, (y_b,))
    pa = _matmul(y_a, wg_pa, name="proj_a", deps=(token,))
    pb = _matmul(y_b, wg_pb, name="proj_b")
    gate_off = 4 * d_a + 3 * d_b
    merged = _merge_fwd(proj, b_gate, pa, pb, gate_off)
    x2 = _matmul(merged, wg_out, res=x0, name="out_proj")
    gc, token = gather_pass(gc, (x2,))
    h2, r2 = _rmsnorm_fwd(x2, norm_ffn, "rmsnorm_ffn")
    (wg_fin,) = gather_done(gc, (h2,))
    ff_gate, ff_up, act = _ffn_in_swiglu(h2, wg_fin, deps=(token,))
    gd, token = gather_pass(gd, (act,))
    (wg_fout,) = gather_done(gd, ())
    dy, dy16, loss_part = _ffn_out_loss(act, wg_fout, x2, target)

    g_fout = _matmul(act, dy16, ta=True, name="dw_ffn_out")
    r_fout, token = reduce_start("fout", ["w_ffn_out"], [g_fout], ())
    dact = _matmul(dy16, wg_fout, tb=True, name="d_act", deps=(token,))
    r_fout, token = reduce_pair_done(r_fout, (dact,))
    dgu = _swiglu_bwd(dact, ff_gate, ff_up)
    g_fin = _matmul(h2, dgu, ta=True, name="dw_ffn_in", deps=(token,))
    r_fin, token = reduce_start("fin", ["w_ffn_in"], [g_fin], ())
    dh2 = _matmul(dgu, wg_fin, tb=True, name="d_h2", deps=(token,))
    r_fout, token_a = reduce_chip_done(r_fout, (dh2,))
    r_fin, token_b = reduce_pair_done(r_fin, (dh2,))
    dx2, dx2_16, g_norm_ffn = _rmsnorm_bwd(dh2, x2, r2, norm_ffn, dy, "rmsnorm_ffn_bwd", deps=(token_a, token_b))
    dmerged = _matmul(dx2_16, wg_out, tb=True, name="d_merged")
    dp_ab, dproj, g_bgate = _merge_bwd(dmerged, proj, b_gate, pa, pb, gate_off)
    g_out = _matmul(merged, dx2_16, ta=True, name="dw_out")
    g_pa = _matmul(y_a, dp_ab[0], ta=True, name="dw_proj_a")
    g_pb = _matmul(y_b, dp_ab[1], ta=True, name="dw_proj_b")
    r_mid, token = reduce_start("mid", ["w_proj_a", "w_proj_b", "w_out"], [g_pa, g_pb, g_out], ())
    dy_a = _matmul(dp_ab[0], wg_pa, tb=True, name="d_y_a", deps=(token,))
    dy_b = _matmul(dp_ab[1], wg_pb, tb=True, name="d_y_b")
    r_fin, token_a = reduce_chip_done(r_fin, (dy_b,))
    r_mid, token_b = reduce_pair_done(r_mid, (dy_b,))
    dproj, dbias, g_qg, g_kg = _attn_bwd(dproj, proj, q_gain, k_gain, bias, dy_b, h_b, col_b, deps=(token_a, token_b))
    r_mid, token = reduce_chip_done(r_mid, (dbias,))
    dproj, g_lb, g_gain = _hgrn_bwd(dproj, proj, o_pre, states, dy_a, hgrn_lb_logits, hgrn_out_gain, h_a, deps=(token,))
    g_in = _matmul(h1, dproj, ta=True, name="dw_in")
    r_in, token = reduce_start("in", ["w_in"], [g_in], ())
    g_rb = _relbias_reduce(dbias.transpose(1, 0, 2))[:, :N_REL]
    reduce_finish(r_mid, (token,))
    reduce_finish(r_fout, (token,))
    r_in, token = reduce_pair_done(r_in, (g_rb, upd["w_out"][0], upd["w_proj_a"][0], upd["w_proj_b"][0],
                                          upd["w_ffn_out"][0]))
    dh1 = _matmul(dproj, wg_in, tb=True, name="d_h1", deps=(token,))
    dx, _, g_norm_mix = _rmsnorm_bwd(dh1, x0, r1, norm_mix, dx2, "rmsnorm_mix_bwd")
    reduce_finish(r_fin, (dx,))
    r_in, token = reduce_chip_done(r_in, (upd["w_ffn_in"][0],))

    small_w = [b_gate, norm_mix, norm_ffn, hgrn_lb_logits, hgrn_out_gain, q_gain, k_gain, rel_bias]
    small_m = [m_b_gate, m_norm_mix, m_norm_ffn, m_hgrn_lb_logits, m_hgrn_out_gain, m_q_gain, m_k_gain, m_rel_bias]
    small_v = [v_b_gate, v_norm_mix, v_norm_ffn, v_hgrn_lb_logits, v_hgrn_out_gain, v_q_gain, v_k_gain, v_rel_bias]
    small_g = [g_bgate, g_norm_mix, g_norm_ffn, g_lb, g_gain, g_qg, g_kg, g_rb]
    sizes = [w.size for w in small_w]
    length = -(-(sum(sizes) + 1) // 128) * 128

    def pack(parts_):
        flat = jnp.concatenate([p.reshape(1, -1) for p in parts_], axis=1)
        return jnp.pad(flat, ((0, 0), (0, length - flat.shape[1])))

    one = jnp.ones((1, 1), F32)
    packed = _small_allreduce_adamw(pack(small_g + [loss_part]), pack(small_w + [one]), pack(small_m + [one]),
                                    pack(small_v + [one]), deps=(token,))

    def unpack(vec):
        out, at = [], 0
        for w, n in zip(small_w, sizes):
            out.append(vec[0, at:at + n].reshape(w.shape))
            at += n
        return out, vec[0, at]

    (sg, loss), (sd, _), (sm, _), (sv, _) = [unpack(p) for p in packed]
    reduce_finish(r_in, (packed[0],))

    def ordered(small, bigs):
        bigs = [bigs[nm][None] for nm in names]
        return [bigs[0]] + small + bigs[1:]

    return (loss, dx.reshape(x.shape), *ordered(sg, g_big), *ordered(sd, {nm: upd[nm][0] for nm in names}),
            *ordered(sm, {nm: upd[nm][1] for nm in names}), *ordered(sv, {nm: upd[nm][2] for nm in names}))
```

```python
import functools

import jax
import jax.numpy as jnp
from jax import lax
from jax.experimental import pallas as pl
from jax.experimental.pallas import tpu as pltpu

F32 = jnp.float32
BF16 = jnp.bfloat16
SDS = jax.ShapeDtypeStruct
MESH = pl.DeviceIdType.MESH
HIGHEST = lax.Precision.HIGHEST

CHUNK = 64
SUB = 16
HEAD = 128
N_PAST = 8
BAND = (N_PAST + 1) * CHUNK
PAD = N_PAST * CHUNK
REL_FUTURE = CHUNK - 1
REL_PAST = 2 * CHUNK - 1
N_REL = REL_FUTURE + REL_PAST + 1
REL_LANES = 256
EPS = 1e-6
MIX_HEADS = 2
MIX_UNROLL = 4
MIX_UNROLL_BWD = 4
ATT_UNROLL = 8
ATT_UNROLL_BWD = 4
EXP_CLAMP = 80.0

ADAM_LR = 0.001
ADAM_B1 = 0.9
ADAM_B2 = 0.999
ADAM_EPS = 1e-08
ADAM_WD = 0.01
ADAM_STEP = 10

VMEM_LIMIT = 56 * 1024 * 1024

HBM = pl.BlockSpec(memory_space=pltpu.HBM)
ANY = pl.BlockSpec(memory_space=pl.ANY)
SEM = pl.BlockSpec(memory_space=pltpu.SEMAPHORE)

NT = (((1,), (1,)), ((), ()))
TN = (((0,), (0,)), ((), ()))
NN = (((1,), (0,)), ((), ()))


def _params(sem=None, **kw):
    return pltpu.CompilerParams(dimension_semantics=sem, vmem_limit_bytes=VMEM_LIMIT, **kw)


def _tile(n, pref, unit=128):
    if n <= pref:
        return n
    t = pref - pref % unit
    while n % t:
        t -= unit
    return t


def _loop(n, unroll, step, init):
    assert n % unroll == 0, (n, unroll)

    def several(i, carry):
        for u in range(unroll):
            carry = step(i * unroll + u, carry)
        return carry

    return lax.fori_loop(0, n // unroll, several, init)


def _sigmoid(x):
    return 1.0 / (1.0 + jnp.exp(-x))


def _dsilu(x, s):
    return s * (1.0 + x * (1.0 - s))


def _bdot(a, b, dims=NN):
    return lax.dot_general(a.astype(BF16), b.astype(BF16), dims, preferred_element_type=F32)


def _split(a):
    hi = a.astype(BF16)
    return hi, (a - hi.astype(F32)).astype(BF16)


def _dot3(a, b, dims):
    dot = lambda u, v: lax.dot_general(u, v, dims, preferred_element_type=F32)
    return dot(a[0], b[1]) + dot(a[1], b[0]) + dot(a[0], b[0])


def _fdot(a, b):
    return lax.dot_general(a, b, NN, precision=HIGHEST, preferred_element_type=F32)


MM_TILE_K = 5632
MM_TILE_N = 512


def _matmul_chunks(h, w, which, prev, pos, name, own_shard=False, deps=()):
    t, d = h.shape
    nc_ = w.shape[1] if own_shard else w.shape[1] // 4
    tm, tn = _tile(t, 1024), _tile(nc_, 1408)
    nn = nc_ // tn

    def chunk(q, p):
        sel = p[which[0]]
        for i in range(1, len(which)):
            sel = jnp.where(q == i, p[which[i]], sel)
        return sel

    def body(p_ref, h_ref, w_ref, *rest):
        rest[-1][...] = jnp.dot(h_ref[...], w_ref[...].astype(BF16), preferred_element_type=F32)

    if own_shard:
        w_spec = pl.BlockSpec((d, tn), lambda q, i, j, p: (0, j))
    else:
        w_spec = pl.BlockSpec((d, tn), lambda q, i, j, p: (0, chunk(q, p) * nn + j))
    n_extra = len(deps) + (prev is not None)
    return pl.pallas_call(
        body,
        grid_spec=pltpu.PrefetchScalarGridSpec(
            num_scalar_prefetch=1, grid=(len(which), t // tm, nn),
            in_specs=[pl.BlockSpec((tm, d), lambda q, i, j, p: (i, 0)), w_spec] + [ANY] * n_extra,
            out_specs=pl.BlockSpec((tm, tn), lambda q, i, j, p: (i, chunk(q, p) * nn + j))),
        out_shape=SDS((t, 4 * nc_), F32), input_output_aliases={3 + len(deps): 0} if prev is not None else {},
        compiler_params=_params(("arbitrary", "arbitrary", "arbitrary")),
        name=name)(pos, h, w, *deps, *(() if prev is None else (prev,)))


def _matmul(a, b, *, ta=False, tb=False, res=None, out_dtype=F32, name, deps=(), a_lead=None, b_lead=None):
    a2, b2 = a.shape[-2:], b.shape[-2:]
    m, k = (a2[1], a2[0]) if ta else a2
    n = b2[0] if tb else b2[1]
    if k > MM_TILE_K:
        tk, tm, tn = _tile(k, MM_TILE_K // 2), _tile(m, 1024), _tile(n, 1024)
    else:
        tk = k
        tm, tn = _tile(m, 2048 if tk <= MM_TILE_K // 2 else 1024), _tile(n, MM_TILE_N)
    nk = k // tk
    dims = ((((0,) if ta else (1,)), ((1,) if tb else (0,))), ((), ()))

    def body(*refs):
        n_in = 2 + (res is not None)
        a_ref, b_ref = refs[:2]
        r_ref = refs[2] if res is not None else None
        o_ref = refs[n_in + len(deps)]
        part = lax.dot_general(a_ref[...].astype(BF16), b_ref[...].astype(BF16), dims, preferred_element_type=F32)

        def finish(out):
            if r_ref is not None:
                out = out + r_ref[...]
            o_ref[...] = out.astype(o_ref.dtype)

        if nk == 1:
            finish(part)
            return
        acc_ref = refs[-1]
        kk = pl.program_id(2)

        @pl.when(kk == 0)
        def _():
            acc_ref[...] = part

        @pl.when(jnp.logical_and(kk > 0, kk < nk - 1))
        def _():
            acc_ref[...] += part

        @pl.when(kk == nk - 1)
        def _():
            finish(acc_ref[...] + part)

    def spec(block, index, lead):
        if lead is None:
            return pl.BlockSpec(block, index)
        return pl.BlockSpec((None,) + block, lambda i, j, l: (lead,) + index(i, j, l))

    a_spec = spec((tk, tm), lambda i, j, l: (l, i), a_lead) if ta else spec((tm, tk), lambda i, j, l: (i, l), a_lead)
    b_spec = spec((tn, tk), lambda i, j, l: (j, l), b_lead) if tb else spec((tk, tn), lambda i, j, l: (l, j), b_lead)
    o_spec = pl.BlockSpec((tm, tn), lambda i, j, l: (i, j))
    in_specs = [a_spec, b_spec] + ([o_spec] if res is not None else []) + [ANY] * len(deps)
    args = (a, b) + ((res,) if res is not None else ()) + tuple(deps)
    return pl.pallas_call(
        body, grid=(m // tm, n // tn, nk), in_specs=in_specs, out_specs=o_spec,
        out_shape=SDS((m, n), out_dtype), scratch_shapes=[pltpu.VMEM((tm, tn), F32)] if nk > 1 else [],
        compiler_params=_params(("parallel", "parallel", "arbitrary")), name=name)(*args)


def _cast_into_full(w, kind, pos, name):
    r, n = w.shape
    tr = _tile(r, 512, 16)
    nr = r // tr
    if kind == "col":
        shape, o_spec = (r, 4 * n), pl.BlockSpec((tr, n), lambda i, p: (i, p[0]))
    else:
        shape, o_spec = (4 * r, n), pl.BlockSpec((tr, n), lambda i, p: (p[0] * nr + i, 0))

    def body(p_ref, w_ref, o_ref):
        o_ref[...] = w_ref[...].astype(BF16)

    return pl.pallas_call(
        body,
        grid_spec=pltpu.PrefetchScalarGridSpec(num_scalar_prefetch=1, grid=(nr,),
                                               in_specs=[pl.BlockSpec((tr, n), lambda i, p: (i, 0))], out_specs=o_spec),
        out_shape=SDS(shape, BF16), compiler_params=_params(("parallel",)), name=name)(pos, w)


def _rmsnorm_fwd(x, gain, name):
    t, d = x.shape
    tm = _tile(t, 256)

    def body(x_ref, g_ref, h_ref, r_ref):
        xv = x_ref[...]
        r = lax.rsqrt(jnp.mean(xv * xv, axis=-1, keepdims=True) + EPS)
        h_ref[...] = (xv * r * g_ref[...]).astype(BF16)
        r_ref[...] = r

    return pl.pallas_call(
        body, grid=(t // tm,),
        in_specs=[pl.BlockSpec((tm, d), lambda i: (i, 0)), pl.BlockSpec((1, d), lambda i: (0, 0))],
        out_specs=[pl.BlockSpec((tm, d), lambda i: (i, 0)), pl.BlockSpec((tm, 1), lambda i: (i, 0))],
        out_shape=[SDS((t, d), BF16), SDS((t, 1), F32)], compiler_params=_params(("parallel",)), name=name)(x, gain)


def _rmsnorm_bwd(dh, x, r, gain, dres, name, deps=()):
    t, d = x.shape
    tm = _tile(t, 256)

    def body(dh_ref, x_ref, r_ref, g_ref, dres_ref, *rest):
        dx_ref, dxb_ref, dg_ref = rest[len(deps):]

        @pl.when(pl.program_id(0) == 0)
        def _():
            dg_ref[...] = jnp.zeros_like(dg_ref)

        dhv, xv, rv = dh_ref[...], x_ref[...], r_ref[...]
        dg_ref[...] += jnp.sum(dhv * (xv * rv), axis=0, keepdims=True)
        u = dhv * g_ref[...]
        dx = dres_ref[...] + rv * u - xv * (rv * rv * rv) * jnp.mean(u * xv, axis=-1, keepdims=True)
        dx_ref[...] = dx
        dxb_ref[...] = dx.astype(BF16)

    row = pl.BlockSpec((tm, d), lambda i: (i, 0))
    vec = pl.BlockSpec((1, d), lambda i: (0, 0))
    return pl.pallas_call(
        body, grid=(t // tm,),
        in_specs=[row, row, pl.BlockSpec((tm, 1), lambda i: (i, 0)), vec, row] + [ANY] * len(deps),
        out_specs=[row, row, vec], out_shape=[SDS((t, d), F32), SDS((t, d), BF16), SDS((1, d), F32)],
        compiler_params=_params(("arbitrary",)), name=name)(dh, x, r, gain, dres, *deps)


def _proj_merge(y_a, y_b, w_a, w_b, proj, b_gate, off, deps=()):
    t, ka = y_a.shape
    kb = y_b.shape[1]
    d = w_a.shape[1]
    tm, tc = _tile(t, 1024), _tile(d, MM_TILE_N)
    nj = d // tc
    oa, ob = off // tc, off // tc + nj

    def body(ya_ref, yb_ref, wa_ref, wb_ref, la_ref, lb_ref, ba_ref, bb_ref, *rest):
        pa_ref, pb_ref, o_ref = rest[len(deps):]
        pa = jnp.dot(ya_ref[...], wa_ref[...], preferred_element_type=F32)
        pb = jnp.dot(yb_ref[...], wb_ref[...], preferred_element_type=F32)
        pa_ref[...] = pa
        pb_ref[...] = pb
        ga = _sigmoid(la_ref[...] + ba_ref[...])
        gb = _sigmoid(lb_ref[...] + bb_ref[...])
        o_ref[...] = (ga * pa + gb * pb).astype(BF16)

    tile = pl.BlockSpec((tm, tc), lambda i, j: (i, j))
    return pl.pallas_call(
        body, grid=(t // tm, nj),
        in_specs=[pl.BlockSpec((tm, ka), lambda i, j: (i, 0)), pl.BlockSpec((tm, kb), lambda i, j: (i, 0)),
                  pl.BlockSpec((ka, tc), lambda i, j: (0, j)), pl.BlockSpec((kb, tc), lambda i, j: (0, j)),
                  pl.BlockSpec((tm, tc), lambda i, j: (i, oa + j)), pl.BlockSpec((tm, tc), lambda i, j: (i, ob + j)),
                  pl.BlockSpec((1, tc), lambda i, j: (0, j)), pl.BlockSpec((1, tc), lambda i, j: (0, nj + j))]
        + [ANY] * len(deps),
        out_specs=[tile, tile, tile], out_shape=[SDS((t, d), F32), SDS((t, d), F32), SDS((t, d), BF16)],
        compiler_params=_params(("parallel", "parallel")),
        name="proj_merge")(y_a, y_b, w_a, w_b, proj, proj, b_gate, b_gate, *deps)


def _merge_bwd(dmerged, proj, b_gate, pa, pb, off):
    t, d = pa.shape
    tm, tc = _tile(t, 512), _tile(d, 512)
    nj, ni = d // tc, t // tm
    o0 = off // tc

    def body(dm_ref, l_ref, b_ref, pa_ref, pb_ref, dp_ref, dl_ref, db_ref):
        s, i = pl.program_id(0), pl.program_id(2)
        p = jnp.where(s == 0, pa_ref[...], pb_ref[...])
        g = _sigmoid(l_ref[...] + b_ref[...])
        dm = dm_ref[...]
        dp_ref[0] = (dm * g).astype(BF16)
        dl = dm * p * g * (1.0 - g)
        dl_ref[...] = dl.astype(BF16)

        @pl.when(i == 0)
        def _():
            db_ref[...] = jnp.zeros_like(db_ref)

        db_ref[...] += jnp.sum(dl, axis=0, keepdims=True)

    tile = pl.BlockSpec((tm, tc), lambda s, j, i: (i, j))
    return pl.pallas_call(
        body, grid=(2, nj, ni),
        in_specs=[tile, pl.BlockSpec((tm, tc), lambda s, j, i: (i, o0 + s * nj + j)),
                  pl.BlockSpec((1, tc), lambda s, j, i: (0, s * nj + j)), tile, tile],
        out_specs=[pl.BlockSpec((1, tm, tc), lambda s, j, i: (s, i, j)),
                   pl.BlockSpec((tm, tc), lambda s, j, i: (i, o0 + s * nj + j)),
                   pl.BlockSpec((1, tc), lambda s, j, i: (0, s * nj + j))],
        out_shape=[SDS((2, t, d), BF16), SDS(proj.shape, BF16), SDS((1, 2 * d), F32)],
        compiler_params=_params(("arbitrary", "arbitrary", "arbitrary")),
        name="merge_bwd")(dmerged, proj, b_gate, pa, pb)


def _ffn_in_swiglu(h, w, deps=()):
    t, d = h.shape
    f = w.shape[1] // 2
    tm, tn = _tile(t, 2048), _tile(f, MM_TILE_N)
    nj = f // tn

    def body(h_ref, wg_ref, wu_ref, *rest):
        g_ref, u_ref, a_ref = rest[len(deps):]
        hv = h_ref[...]
        g = jnp.dot(hv, wg_ref[...], preferred_element_type=F32)
        u = jnp.dot(hv, wu_ref[...], preferred_element_type=F32)
        g_ref[...] = g
        u_ref[...] = u
        a_ref[...] = (g * _sigmoid(g) * u).astype(BF16)

    tile = pl.BlockSpec((tm, tn), lambda i, j: (i, j))
    return pl.pallas_call(
        body, grid=(t // tm, nj),
        in_specs=[pl.BlockSpec((tm, d), lambda i, j: (i, 0)), pl.BlockSpec((d, tn), lambda i, j: (0, j)),
                  pl.BlockSpec((d, tn), lambda i, j: (0, nj + j))] + [ANY] * len(deps),
        out_specs=[tile, tile, tile], out_shape=[SDS((t, f), F32), SDS((t, f), F32), SDS((t, f), BF16)],
        compiler_params=_params(("parallel", "parallel")), name="ffn_in_swiglu")(h, w, w, *deps)


def _swiglu_bwd(dact, gate, up):
    t, f = gate.shape
    tm = _tile(t, 128)

    def body(d_ref, g_ref, u_ref, o_ref):
        g, dv = g_ref[...], d_ref[...]
        sg = _sigmoid(g)
        o_ref[:, :f] = (dv * u_ref[...] * _dsilu(g, sg)).astype(BF16)
        o_ref[:, f:] = (dv * (g * sg)).astype(BF16)

    row = pl.BlockSpec((tm, f), lambda i: (i, 0))
    return pl.pallas_call(
        body, grid=(t // tm,), in_specs=[row, row, row],
        out_specs=pl.BlockSpec((tm, 2 * f), lambda i: (i, 0)), out_shape=SDS((t, 2 * f), BF16),
        compiler_params=_params(("parallel",)), name="swiglu_bwd")(dact, gate, up)


def _ffn_out_loss(act, w, x_res, target):
    t, d = x_res.shape
    k = act.shape[1]
    tm, tn = _tile(t, 1024), _tile(d, MM_TILE_N)

    def body(a_ref, w_ref, r_ref, t_ref, dy_ref, dyb_ref, l_ref):
        @pl.when(jnp.logical_and(pl.program_id(0) == 0, pl.program_id(1) == 0))
        def _():
            l_ref[...] = jnp.zeros_like(l_ref)

        y = jnp.dot(a_ref[...], w_ref[...], preferred_element_type=F32) + r_ref[...]
        e = y - t_ref[...]
        dy = e * (1.0 / d)
        dy_ref[...] = dy
        dyb_ref[...] = dy.astype(BF16)
        l_ref[...] += (0.5 / d) * jnp.sum(jnp.sum(e * e, axis=-1, keepdims=True), axis=0, keepdims=True)

    tile = pl.BlockSpec((tm, tn), lambda i, j: (i, j))
    return pl.pallas_call(
        body, grid=(t // tm, d // tn),
        in_specs=[pl.BlockSpec((tm, k), lambda i, j: (i, 0)), pl.BlockSpec((k, tn), lambda i, j: (0, j)), tile, tile],
        out_specs=[tile, tile, pl.BlockSpec((1, 1), lambda i, j: (0, 0))],
        out_shape=[SDS((t, d), F32), SDS((t, d), BF16), SDS((1, 1), F32)],
        compiler_params=_params(("arbitrary", "arbitrary")), name="ffn_out_loss")(act, w, x_res, target)


def _rel_onehot(qi):
    p = lax.broadcasted_iota(jnp.int32, (REL_LANES, BAND), 1)
    r = lax.broadcasted_iota(jnp.int32, (REL_LANES, BAND), 0)
    idx = jnp.clip(qi + PAD - p, -REL_FUTURE, REL_PAST) + REL_FUTURE
    return (idx == r).astype(F32)


def _relbias_expand(rb):
    h = rb.shape[0]

    def body(rb_ref, o_ref):
        def step(qi, _):
            o_ref[qi] = _fdot(rb_ref[...], _rel_onehot(qi))
            return 0

        lax.fori_loop(0, CHUNK, step, 0)

    return pl.pallas_call(body, out_shape=SDS((CHUNK, h, BAND), F32), compiler_params=_params(),
                          name="relbias_expand")(rb)


def _relbias_reduce(dbias):
    h = dbias.shape[1]

    rows_per_pass = 4

    def body(db_ref, o_ref):
        def step(i, acc):
            parts = []
            for u in range(rows_per_pass):
                qi = i * rows_per_pass + u
                xv = db_ref[qi]
                hi = xv.astype(BF16)
                rest = xv - hi.astype(F32)
                mid = rest.astype(BF16)
                low = (rest - mid.astype(F32)).astype(BF16)
                parts.append(lax.dot_general(jnp.concatenate([hi, mid, low], axis=0), _rel_onehot(qi).astype(BF16), NT,
                                             preferred_element_type=F32))
            for part in parts:
                acc = acc + (part[0:h] + part[h:2 * h] + part[2 * h:3 * h])
            return acc

        o_ref[...] = lax.fori_loop(0, CHUNK // rows_per_pass, step, jnp.zeros((h, REL_LANES), F32))

    return pl.pallas_call(body, out_shape=SDS((h, REL_LANES), F32), compiler_params=_params(),
                          name="relbias_reduce")(dbias)


def _lower_bound(l_ref):
    l0, l1 = l_ref[0:1, :], l_ref[1:2, :]
    m = jnp.maximum(l0, l1)
    e0, e1 = jnp.exp(l0 - m), jnp.exp(l1 - m)
    return e0 / (e0 + e1)


def _tri(lower):
    r = lax.broadcasted_iota(jnp.int32, (CHUNK, CHUNK), 0)
    c = lax.broadcasted_iota(jnp.int32, (CHUNK, CHUNK), 1)
    return r >= c if lower else r <= c


def _hgrn_intra(qs, kk, b_s):
    rows = lax.broadcasted_iota(jnp.int32, (CHUNK, HEAD), 0)
    b = b_s[...]
    out = []
    for i in range(CHUNK // SUB):
        lo = i * SUB
        ref = jnp.zeros((1, HEAD), F32) if i == 0 else b_s[lo - 1:lo, :]
        eq = jnp.exp(b[lo:lo + SUB] - ref)
        qt = _split(qs[lo:lo + SUB] * eq)
        e = jnp.where(rows < lo + SUB, jnp.exp(jnp.minimum(ref - b, EXP_CLAMP)), 0.0)
        kt = _split(kk * e)
        out.append((eq, qt, e, kt))
    return out


def _hgrn_scores(blocks):
    tr = lax.broadcasted_iota(jnp.int32, (SUB, CHUNK), 0)
    tc = lax.broadcasted_iota(jnp.int32, (SUB, CHUNK), 1)
    return jnp.concatenate([jnp.where(tc <= tr + i * SUB, _dot3(qt, kt, NT), 0.0)
                            for i, (_, qt, _, kt) in enumerate(blocks)], axis=0)


def _hgrn_fwd(proj, lb_logits, gain, n_heads):
    t = proj.shape[0]
    nc = t // CHUNK
    da = n_heads * HEAD
    hp = MIX_HEADS
    wide = hp * HEAD

    def body(q_ref, f_ref, i_ref, g_ref, l_ref, gain_ref, y_ref, o_ref, st_ref, state, b_s):
        state[...] = jnp.zeros_like(state)
        lb_all = _lower_bound(l_ref)
        tril = _tri(True).astype(F32)

        def chunks(i, _):
            dot = functools.partial(lax.dot_general, preferred_element_type=F32)
            items = []
            for u in range(MIX_UNROLL):
                for hh in range(hp):
                    j = i * MIX_UNROLL + u
                    sl = pl.ds(pl.multiple_of(j * CHUNK, CHUNK), CHUNK)
                    cols = slice(hh * HEAD, (hh + 1) * HEAD)
                    lb = lb_all[:, cols]
                    fg = lb + (1.0 - lb) * _sigmoid(f_ref[sl, cols])
                    qv = q_ref[sl, cols]
                    gv = g_ref[sl, cols]
                    items.append(dict(hh=hh, j=j, sl=sl, cols=cols, lf=jnp.log(fg), kk=1.0 - fg, qs=qv * _sigmoid(qv),
                                      vb=i_ref[sl, cols].astype(BF16), gate=gv * _sigmoid(gv)))
            for it in items:
                it["b"] = _fdot(tril, it["lf"])
            for slot, it in enumerate(items):
                b = it["b"]
                b_s[slot] = b
                it["blocks"] = _hgrn_intra(it["qs"], it["kk"], b_s.at[slot])
                it["ebl"] = jnp.exp(b_s[slot, CHUNK - 1:CHUNK, :])
                it["qe"] = (it["qs"] * jnp.exp(b)).astype(BF16)
                it["kd"] = (it["kk"] * jnp.exp(b_s[slot, CHUNK - 1:CHUNK, :] - b)).astype(BF16)
            for it in items:
                it["a"] = _hgrn_scores(it["blocks"]).astype(BF16)
            for it in items:
                it["kv"] = dot(it["vb"], it["kd"], TN)
                it["o"] = dot(it["a"], it["vb"], NN)
            s_now = [state[hh] for hh in range(hp)]
            for it in items:
                it["s_in"] = s_now[it["hh"]]
                s_now[it["hh"]] = it["s_in"] * it["ebl"] + it["kv"]
            for hh in range(hp):
                state[hh] = s_now[hh]
            for it in items:
                it["o"] = it["o"] + dot(it["qe"], it["s_in"].astype(BF16), NT)
            for it in items:
                o, sl, cols = it["o"], it["sl"], it["cols"]
                st_ref[it["hh"], it["j"]] = it["s_in"]
                o_ref[sl, cols] = o
                rr = lax.rsqrt(jnp.mean(o * o, axis=-1, keepdims=True) + EPS)
                y_ref[sl, cols] = (o * rr * gain_ref[:, cols] * it["gate"]).astype(BF16)
            return 0

        assert nc % MIX_UNROLL == 0, (nc, MIX_UNROLL)
        lax.fori_loop(0, nc // MIX_UNROLL, chunks, 0)

    col = lambda k: pl.BlockSpec((t, wide), lambda h: (0, k * (n_heads // hp) + h))
    vec = pl.BlockSpec((1, wide), lambda h: (0, h))
    return pl.pallas_call(
        body, grid=(n_heads // hp,),
        in_specs=[col(0), col(1), col(2), col(3), pl.BlockSpec((2, wide), lambda h: (0, h)), vec],
        out_specs=[pl.BlockSpec((t, wide), lambda h: (0, h)), pl.BlockSpec((t, wide), lambda h: (0, h)),
                   pl.BlockSpec((hp, nc, HEAD, HEAD), lambda h: (h, 0, 0, 0))],
        out_shape=[SDS((t, da), BF16), SDS((t, da), F32), SDS((n_heads, nc, HEAD, HEAD), F32)],
        scratch_shapes=[pltpu.VMEM((hp, HEAD, HEAD), F32), pltpu.VMEM((hp * MIX_UNROLL, CHUNK, HEAD), F32)],
        compiler_params=_params(("parallel",)), name="hgrn_fwd")(proj, proj, proj, proj, lb_logits, gain)


def _write_column_groups(res, dproj_ref, sems, col0, stride, h, width):
    copies = [pltpu.make_async_copy(
        res.at[p], dproj_ref.at[:, pl.ds(pl.multiple_of((col0 + p * stride + h) * width, HEAD), width)], sems.at[p])
        for p in range(res.shape[0])]
    for cp in copies:
        cp.start()
    for cp in copies:
        cp.wait()


def _hgrn_bwd(dproj, proj, o_pre, states, dy, lb_logits, gain, n_heads, deps=()):
    t = proj.shape[0]
    nc = t // CHUNK
    da = n_heads * HEAD
    hp = MIX_HEADS
    wide = hp * HEAD

    def body(*refs):
        (q_ref, f_ref, i_ref, g_ref, o_ref, st_ref, dy_ref, l_ref, gain_ref,
         dproj_ref, dl_ref, dgain_ref, res, dstate, b_s, out_sems) = refs[1 + len(deps):]

        def compute():
            dstate[...] = jnp.zeros_like(dstate)
            lb_all = _lower_bound(l_ref)
            tril_m, tril, triu = _tri(True), _tri(True).astype(F32), _tri(False).astype(F32)
            last = lax.broadcasted_iota(jnp.int32, (CHUNK, HEAD), 0) == CHUNK - 1

            def chunks(i, carry):
                dot = functools.partial(lax.dot_general, preferred_element_type=F32)
                items = []
                for u in range(MIX_UNROLL_BWD):
                    for hh in range(hp):
                        j = nc - 1 - (i * MIX_UNROLL_BWD + u)
                        sl = pl.ds(pl.multiple_of(j * CHUNK, CHUNK), CHUNK)
                        cols = slice(hh * HEAD, (hh + 1) * HEAD)
                        lb, gain_v = lb_all[:, cols], gain_ref[:, cols]
                        sg = _sigmoid(f_ref[sl, cols])
                        fg = lb + (1.0 - lb) * sg
                        qv = q_ref[sl, cols]
                        sq = _sigmoid(qv)
                        gv = g_ref[sl, cols]
                        sgg = _sigmoid(gv)
                        silg = gv * sgg
                        o = o_ref[sl, cols]
                        dyv = dy_ref[sl, cols]
                        rr = lax.rsqrt(jnp.mean(o * o, axis=-1, keepdims=True) + EPS)
                        on = o * rr
                        don = dyv * gain_v * silg
                        do = (rr * don - o * (rr * rr * rr) * jnp.mean(don * o, axis=-1, keepdims=True)).astype(BF16)
                        items.append(dict(
                            hh=hh, j=j, sl=sl, cols=cols, lb=lb, sg=sg, fg=fg, kk=1.0 - fg, qv=qv, sq=sq, qs=qv * sq,
                            vb=i_ref[sl, cols].astype(BF16), do=do, dg=dyv * on * gain_v * _dsilu(gv, sgg),
                            dgain=jnp.sum(dyv * on * silg, axis=0, keepdims=True)))
                for it in items:
                    it["b"] = _fdot(tril, jnp.log(it["fg"]))
                for slot, it in enumerate(items):
                    b = it["b"]
                    b_s[slot] = b
                    it["blocks"] = _hgrn_intra(it["qs"], it["kk"], b_s.at[slot])
                    bl = b_s[slot, CHUNK - 1:CHUNK, :]
                    it["eb"], it["ebl"], it["ekd"] = jnp.exp(b), jnp.exp(bl), jnp.exp(bl - b)
                    it["s_in"] = st_ref[it["hh"], it["j"]]
                for it in items:
                    it["a"] = _hgrn_scores(it["blocks"]).astype(BF16)
                    it["da"] = jnp.where(tril_m, dot(it["do"], it["vb"], NT), 0.0)
                for it in items:
                    dq_rows = []
                    dk = jnp.zeros((CHUNK, HEAD), F32)
                    for blk, (eq, qt, e, kt) in enumerate(it["blocks"]):
                        da_i = _split(it["da"][blk * SUB:(blk + 1) * SUB])
                        dq_rows.append(eq * _dot3(da_i, kt, NN))
                        dk = dk + e * _dot3(da_i, qt, TN)
                    it["dq"] = jnp.concatenate(dq_rows, axis=0) + dot(it["do"], it["s_in"].astype(BF16), NN) * it["eb"]
                    it["dk"] = dk
                    it["dv"] = dot(it["a"], it["do"], TN)
                    it["g"] = dot(it["do"], (it["qs"] * it["eb"]).astype(BF16), TN)
                ds_now = [dstate[hh] for hh in range(hp)]
                for it in items:
                    it["ds_out"] = ds_now[it["hh"]]
                    ds_now[it["hh"]] = it["ds_out"] * it["ebl"] + it["g"]
                for hh in range(hp):
                    dstate[hh] = ds_now[hh]
                for it in items:
                    dsb = it["ds_out"].astype(BF16)
                    it["dv"] = it["dv"] + dot((it["kk"] * it["ekd"]).astype(BF16), dsb, NT)
                    it["dk_state"] = it["ekd"] * dot(it["vb"], dsb, NN)
                for it in items:
                    kk, dk_state = it["kk"], it["dk_state"]
                    it["dk"] = it["dk"] + dk_state
                    extra = (jnp.sum(kk * dk_state, axis=0, keepdims=True)
                             + it["ebl"] * jnp.sum(it["s_in"] * it["ds_out"], axis=0, keepdims=True))
                    it["db"] = it["qs"] * it["dq"] - kk * it["dk"] + jnp.where(last, extra, 0.0)
                for it in items:
                    it["dlf"] = _fdot(triu, it["db"])
                carry = list(carry)
                for it in items:
                    hh, sl, cols, sg, lb = it["hh"], it["sl"], it["cols"], it["sg"], it["lb"]
                    dfg = it["dlf"] / it["fg"] - it["dk"]
                    dlb_acc, dgain_acc = carry[hh]
                    carry[hh] = (dlb_acc + jnp.sum(dfg * (1.0 - sg), axis=0, keepdims=True), dgain_acc + it["dgain"])
                    res[0, sl, cols] = (it["dq"] * _dsilu(it["qv"], it["sq"])).astype(BF16)
                    res[1, sl, cols] = (dfg * (1.0 - lb) * sg * (1.0 - sg)).astype(BF16)
                    res[2, sl, cols] = it["dv"].astype(BF16)
                    res[3, sl, cols] = it["dg"].astype(BF16)
                return tuple(carry)

            assert nc % MIX_UNROLL_BWD == 0, (nc, MIX_UNROLL_BWD)
            zero = jnp.zeros((1, HEAD), F32)
            sums = lax.fori_loop(0, nc // MIX_UNROLL_BWD, chunks, ((zero, zero),) * hp)
            for hh, (dlb, dgain) in enumerate(sums):
                cols = slice(hh * HEAD, (hh + 1) * HEAD)
                lb = lb_all[:, cols]
                dgain_ref[:, cols] = dgain
                dl0 = dlb * lb * (1.0 - lb)
                dl_ref[0:1, cols] = dl0
                dl_ref[1:2, cols] = -dl0

        compute()
        _write_column_groups(res, dproj_ref, out_sems, 0, ng, pl.program_id(0), wide)

    ng = n_heads // hp
    col = lambda k: pl.BlockSpec((t, wide), lambda h: (0, k * ng + h))
    head = pl.BlockSpec((t, wide), lambda h: (0, h))
    vec = pl.BlockSpec((1, wide), lambda h: (0, h))
    return pl.pallas_call(
        body, grid=(ng,),
        in_specs=[ANY] * (1 + len(deps)) + [col(0), col(1), col(2), col(3), head,
                  pl.BlockSpec((hp, nc, HEAD, HEAD), lambda h: (h, 0, 0, 0)),
                  head, pl.BlockSpec((2, wide), lambda h: (0, h)), vec],
        out_specs=[ANY, pl.BlockSpec((2, wide), lambda h: (0, h)), vec],
        out_shape=[SDS(dproj.shape, BF16), SDS((2, da), F32), SDS((1, da), F32)],
        scratch_shapes=[pltpu.VMEM((4, t, wide), BF16), pltpu.VMEM((hp, HEAD, HEAD), F32),
                        pltpu.VMEM((hp * MIX_UNROLL_BWD, CHUNK, HEAD), F32), pltpu.SemaphoreType.DMA((4,))],
        input_output_aliases={0: 0}, compiler_params=_params(("arbitrary",)),
        name="hgrn_bwd")(dproj, *deps, proj, proj, proj, proj, o_pre, states, dy, lb_logits, gain)


ROWS = 256


def _head_norm(x_ref, gain, dst, dst_off, t):
    def step(i, _):
        sl = pl.ds(pl.multiple_of(i * ROWS, ROWS), ROWS)
        xv = x_ref[sl, :]
        r = lax.rsqrt(jnp.mean(xv * xv, axis=-1, keepdims=True) + EPS)
        dst[pl.ds(pl.multiple_of(dst_off + i * ROWS, ROWS), ROWS), :] = (xv * r * gain).astype(BF16)
        return 0

    lax.fori_loop(0, t // ROWS, step, 0)


def _head_norm_bwd(x_ref, gain, dn_ref, dn_off, out, slot, t):
    def step(i, acc):
        sl = pl.ds(pl.multiple_of(i * ROWS, ROWS), ROWS)
        xv = x_ref[sl, :]
        dn = dn_ref[pl.ds(pl.multiple_of(dn_off + i * ROWS, ROWS), ROWS), :]
        r = lax.rsqrt(jnp.mean(xv * xv, axis=-1, keepdims=True) + EPS)
        u = dn * gain
        out[slot, sl, :] = (r * u - xv * (r * r * r) * jnp.mean(u * xv, axis=-1, keepdims=True)).astype(out.dtype)
        return acc + jnp.sum(dn * (xv * r), axis=0, keepdims=True)

    return lax.fori_loop(0, t // ROWS, step, jnp.zeros((1, HEAD), F32))


def _attn_scores(qn, kpad, n):
    qc = qn[pl.ds(pl.multiple_of(n * CHUNK, CHUNK), CHUNK), :]
    band = pl.ds(pl.multiple_of(n * CHUNK, CHUNK), BAND)
    return qc, band, lax.dot_general(qc, kpad[band, :], NT, preferred_element_type=F32)


def _attn_softmax(raw, bias_ref, n):
    s = raw * (HEAD ** -0.5) + bias_ref[0]
    col = lax.broadcasted_iota(jnp.int32, (CHUNK, BAND), 1)
    s = jnp.where(col >= PAD - n * CHUNK, s, -jnp.inf)
    p = jnp.exp(s - jnp.max(s, axis=-1, keepdims=True))
    return p / jnp.sum(p, axis=-1, keepdims=True)


def _attn_fwd(proj, q_gain, k_gain, bias, n_heads, col0):
    t = proj.shape[0]
    nc = t // CHUNK

    def body(q_ref, k_ref, v_ref, qg_ref, kg_ref, bias_ref, y_ref, qn, kpad, vpad):
        kpad[0:PAD, :] = jnp.zeros((PAD, HEAD), BF16)
        vpad[0:PAD, :] = jnp.zeros((PAD, HEAD), BF16)
        _head_norm(q_ref, qg_ref[...], qn, 0, t)
        _head_norm(k_ref, kg_ref[...], kpad, PAD, t)

        def copy_v(i, _):
            vpad[pl.ds(pl.multiple_of(PAD + i * ROWS, ROWS), ROWS), :] = v_ref[
                pl.ds(pl.multiple_of(i * ROWS, ROWS), ROWS), :].astype(BF16)
            return 0

        lax.fori_loop(0, t // ROWS, copy_v, 0)

        def chunks(i, _):
            ns = [i * ATT_UNROLL + u for u in range(ATT_UNROLL)]
            scored = [_attn_scores(qn, kpad, n) for n in ns]
            probs = [_attn_softmax(raw, bias_ref, n).astype(BF16) for n, (_, _, raw) in zip(ns, scored)]
            outs = [lax.dot_general(p, vpad[band, :], NN, preferred_element_type=F32).astype(BF16)
                    for p, (_, band, _) in zip(probs, scored)]
            for n, o in zip(ns, outs):
                y_ref[pl.ds(pl.multiple_of(n * CHUNK, CHUNK), CHUNK), :] = o
            return 0

        assert nc % ATT_UNROLL == 0, (nc, ATT_UNROLL)
        lax.fori_loop(0, nc // ATT_UNROLL, chunks, 0)

    col = lambda k: pl.BlockSpec((t, HEAD), lambda h: (0, col0 + k * n_heads + h))
    vec = pl.BlockSpec((1, HEAD), lambda h: (0, 0))
    return pl.pallas_call(
        body, grid=(n_heads,),
        in_specs=[col(0), col(1), col(2), vec, vec, pl.BlockSpec((1, CHUNK, BAND), lambda h: (h, 0, 0))],
        out_specs=pl.BlockSpec((t, HEAD), lambda h: (0, h)), out_shape=SDS((t, n_heads * HEAD), BF16),
        scratch_shapes=[pltpu.VMEM((t, HEAD), BF16), pltpu.VMEM((t + PAD, HEAD), BF16), pltpu.VMEM((t + PAD, HEAD), BF16)],
        compiler_params=_params(("parallel",)), name="attn_fwd")(proj, proj, proj, q_gain, k_gain, bias)


def _attn_bwd(dproj, proj, q_gain, k_gain, bias, dy, n_heads, col0, deps=()):
    t = proj.shape[0]
    nc = t // CHUNK

    def body(*refs):
        (q_ref, k_ref, v_ref, qg_ref, kg_ref, bias_ref, dy_ref,
         dproj_ref, dbias_ref, dqg_ref, dkg_ref, qn, kpad, vpad, dqn, dk_acc, dv_acc, res,
         out_sems) = refs[1 + len(deps):]
        h = pl.program_id(0)

        def compute():
            kpad[0:PAD, :] = jnp.zeros((PAD, HEAD), BF16)
            vpad[0:PAD, :] = jnp.zeros((PAD, HEAD), BF16)
            _head_norm(q_ref, qg_ref[...], qn, 0, t)
            _head_norm(k_ref, kg_ref[...], kpad, PAD, t)

            def prep(i, _):
                sl = pl.ds(pl.multiple_of(PAD + i * ROWS, ROWS), ROWS)
                vpad[sl, :] = v_ref[pl.ds(pl.multiple_of(i * ROWS, ROWS), ROWS), :].astype(BF16)
                return 0

            lax.fori_loop(0, t // ROWS, prep, 0)

            def clear(i, _):
                sl = pl.ds(pl.multiple_of(i * ROWS, ROWS), ROWS)
                dk_acc[sl, :] = jnp.zeros((ROWS, HEAD), F32)
                dv_acc[sl, :] = jnp.zeros((ROWS, HEAD), F32)
                return 0

            lax.fori_loop(0, (t + PAD) // ROWS, clear, 0)
            dbias_ref[0] = jnp.zeros((CHUNK, BAND), F32)

            def chunks(i, _):
                dot = functools.partial(lax.dot_general, preferred_element_type=F32)
                ns = [i * ATT_UNROLL_BWD + u for u in range(ATT_UNROLL_BWD)]
                scored = [_attn_scores(qn, kpad, n) for n in ns]
                dos = [dy_ref[pl.ds(pl.multiple_of(n * CHUNK, CHUNK), CHUNK), :].astype(BF16) for n in ns]
                dps = [dot(do, vpad[band, :], NT) for do, (_, band, _) in zip(dos, scored)]
                ps, dss = [], []
                for n, (_, _, raw), dp in zip(ns, scored, dps):
                    p = _attn_softmax(raw, bias_ref, n)
                    ds = p * (dp - jnp.sum(dp * p, axis=-1, keepdims=True))
                    dbias_ref[0] += ds
                    ps.append(p.astype(BF16))
                    dss.append((ds * (HEAD ** -0.5)).astype(BF16))
                dqs = [dot(d, kpad[band, :], NN) for d, (_, band, _) in zip(dss, scored)]
                dks = [dot(d, qc, TN) for d, (qc, _, _) in zip(dss, scored)]
                dvs = [dot(p, do, TN) for p, do in zip(ps, dos)]
                for n, (_, band, _), dq, dk, dv in zip(ns, scored, dqs, dks, dvs):
                    dqn[pl.ds(pl.multiple_of(n * CHUNK, CHUNK), CHUNK), :] = dq
                    dk_acc[band, :] += dk
                    dv_acc[band, :] += dv
                return 0

            assert nc % ATT_UNROLL_BWD == 0, (nc, ATT_UNROLL_BWD)
            lax.fori_loop(0, nc // ATT_UNROLL_BWD, chunks, 0)
            dqg = _head_norm_bwd(q_ref, qg_ref[...], dqn, 0, res, 0, t)
            dkg = _head_norm_bwd(k_ref, kg_ref[...], dk_acc, PAD, res, 1, t)

            def put_v(i, _):
                sl = pl.ds(pl.multiple_of(i * ROWS, ROWS), ROWS)
                res[2, sl, :] = dv_acc[pl.ds(pl.multiple_of(PAD + i * ROWS, ROWS), ROWS), :].astype(BF16)
                return 0

            lax.fori_loop(0, t // ROWS, put_v, 0)

            @pl.when(h == 0)
            def _():
                dqg_ref[...] = jnp.zeros_like(dqg_ref)
                dkg_ref[...] = jnp.zeros_like(dkg_ref)

            dqg_ref[...] += dqg
            dkg_ref[...] += dkg

        compute()
        _write_column_groups(res, dproj_ref, out_sems, col0, n_heads, h, HEAD)

    col = lambda k: pl.BlockSpec((t, HEAD), lambda h: (0, col0 + k * n_heads + h))
    vec = pl.BlockSpec((1, HEAD), lambda h: (0, 0))
    btile = pl.BlockSpec((1, CHUNK, BAND), lambda h: (h, 0, 0))
    return pl.pallas_call(
        body, grid=(n_heads,),
        in_specs=[ANY] * (1 + len(deps)) + [col(0), col(1), col(2), vec, vec, btile,
                                            pl.BlockSpec((t, HEAD), lambda h: (0, h))],
        out_specs=[ANY, btile, vec, vec],
        out_shape=[SDS(dproj.shape, BF16), SDS((n_heads, CHUNK, BAND), F32), SDS((1, HEAD), F32), SDS((1, HEAD), F32)],
        scratch_shapes=[pltpu.VMEM((t, HEAD), BF16), pltpu.VMEM((t + PAD, HEAD), BF16), pltpu.VMEM((t + PAD, HEAD), BF16),
                        pltpu.VMEM((t, HEAD), F32), pltpu.VMEM((t + PAD, HEAD), F32), pltpu.VMEM((t + PAD, HEAD), F32),
                        pltpu.VMEM((3, t, HEAD), BF16), pltpu.SemaphoreType.DMA((3,))],
        input_output_aliases={0: 0}, compiler_params=_params(("arbitrary",)),
        name="attn_bwd")(dproj, *deps, proj, proj, proj, q_gain, k_gain, bias, dy)


def _place():
    x, y, c = lax.axis_index("x"), lax.axis_index("y"), lax.axis_index("c")
    others = [(1 - x, y), (x, 1 - y), (1 - x, 1 - y)]
    return x, y, c, others


def _chunk_of(ref, kind, chip, half, shard_shape):
    r, n = shard_shape
    hr = r // 2
    if kind == "col":
        rows = pl.ds(0, r) if half is None else pl.ds(half * hr, hr)
        return ref.at[rows, pl.ds(chip * n, n)]
    rows = pl.ds(chip * r, r) if half is None else pl.ds(chip * r + half * hr, hr)
    return ref.at[rows, :]


EFFECT = pltpu.SideEffectType.DATAFLOW_SIDE_EFFECTING


def _start_copies(name, bufs, plan, n, deps):
    nb, nd = len(bufs), len(deps)

    def body(*refs):
        send, recv, token = refs[nb + nd], refs[nb + nd + 1], refs[-1]
        for cp in plan(refs[:nb], send, recv)[0]:
            cp.start()
        token[...] = jnp.zeros_like(token)

    out = pl.pallas_call(
        body, name=name,
        out_shape=(pltpu.SemaphoreType.DMA((n,)), pltpu.SemaphoreType.DMA((n,)),
                   *[pltpu.HBM(b.shape, b.dtype) for b in bufs], SDS((8, 128), F32)),
        in_specs=[HBM] * nb + [ANY] * nd,
        out_specs=(SEM, SEM, *[HBM] * nb, pl.BlockSpec(memory_space=pltpu.VMEM)),
        input_output_aliases={i: 2 + i for i in range(nb)},
        compiler_params=pltpu.CompilerParams(has_side_effects=EFFECT),
    )(*[pltpu.with_memory_space_constraint(b, pltpu.HBM) for b in bufs], *deps)
    return out[0], out[1], list(out[2:2 + nb]), out[-1]


def _wait_copies(name, bufs, send, recv, plan, after):
    nb = len(bufs)

    def body(*refs):
        sends, recvs = plan(refs[:nb], refs[nb], refs[nb + 1])
        for cp in sends:
            cp.wait_send()
        for cp in recvs:
            cp.wait_recv()

    out = pl.pallas_call(
        body, name=name, out_shape=tuple(pltpu.HBM(b.shape, b.dtype) for b in bufs),
        in_specs=[HBM] * nb + [SEM, SEM] + [ANY] * len(after), out_specs=tuple([HBM] * nb),
        input_output_aliases={i: i for i in range(nb)},
        compiler_params=pltpu.CompilerParams(has_side_effects=EFFECT),
    )(*bufs, send, recv, *after)
    return list(out)


def _remote(src, dst, send, recv, i, dev):
    return pltpu.make_async_remote_copy(src_ref=src, dst_ref=dst, send_sem=send.at[i], recv_sem=recv.at[i],
                                        device_id=dev, device_id_type=MESH)


ALL_RELATIONS = (0, 1, 2)


def _plan_gather_ici(kinds, shapes, rels=ALL_RELATIONS):
    def plan(refs, send, recv):
        x, y, c, others = _place()
        sends, recvs = [], []
        for w, (kind, ss) in enumerate(zip(kinds, shapes)):
            for p in rels:
                px, py = others[p]
                mine = _chunk_of(refs[w], kind, 2 * x + y, c, ss)
                theirs = _chunk_of(refs[w], kind, 2 * px + py, c, ss)
                sends.append(_remote(mine, mine, send, recv, 3 * w + p, (px, py, c)))
                recvs.append(_remote(theirs, theirs, send, recv, 3 * w + p, (px, py, c)))
        return sends, recvs

    return plan, 3 * len(kinds)


def _plan_gather_pass(kinds, shapes, rels=ALL_RELATIONS):
    def plan(refs, send, recv):
        x, y, c, others = _place()
        sends, recvs = [], []
        for w, (kind, ss) in enumerate(zip(kinds, shapes)):
            for i, p in enumerate(rels):
                px, py = others[p]
                got = _chunk_of(refs[w], kind, 2 * px + py, c, ss)
                coming = _chunk_of(refs[w], kind, 2 * px + py, 1 - c, ss)
                sends.append(_remote(got, got, send, recv, len(rels) * w + i, (x, y, 1 - c)))
                recvs.append(_remote(coming, coming, send, recv, len(rels) * w + i, (x, y, 1 - c)))
        return sends, recvs

    return plan, len(rels) * len(kinds)


def _plan_pair(kinds, shapes):
    nw = len(kinds)

    def plan(refs, send, recv):
        x, y, c, _ = _place()
        sends = []
        for w, (kind, ss) in enumerate(zip(kinds, shapes)):
            for k in range(4):
                sends.append(_remote(_chunk_of(refs[w], kind, k, 1 - c, ss), refs[nw + w].at[k], send, recv,
                                     4 * w + k, (x, y, 1 - c)))
        return sends, sends

    return plan, 4 * nw


def _plan_chip(nw):
    def plan(refs, send, recv):
        x, y, c, others = _place()
        sends = []
        for w in range(nw):
            for p, (px, py) in enumerate(others):
                sends.append(_remote(refs[w].at[p], refs[nw + w].at[p], send, recv, 3 * w + p, (px, py, c)))
        return sends, sends

    return plan, 3 * nw


def _plan_share(slabs):
    def plan(refs, send, recv):
        x, y, c, _ = _place()
        sends, recvs, i = [], [], 0
        for w, ns in enumerate(slabs):
            for s in range(ns):
                sends.append(_remote(refs[w].at[s, c], refs[w].at[s, c], send, recv, i, (x, y, 1 - c)))
                recvs.append(_remote(refs[w].at[s, 1 - c], refs[w].at[s, 1 - c], send, recv, i, (x, y, 1 - c)))
                i += 1
        return sends, recvs

    return plan, sum(slabs)


def _grad_half_spec(kind, tr, tn, nr, nn, chunk):
    if kind == "col":
        return pl.BlockSpec((tr, tn), lambda *a: (a[-1][1] * nr + a[-3], chunk(*a) * nn + a[-2]))
    return pl.BlockSpec((tr, tn), lambda *a: ((2 * chunk(*a) + a[-1][1]) * nr + a[-3], a[-2]))


def _pair_add(grad, got, kind, shard_shape, pos, name):
    r, n = shard_shape
    hr = r // 2
    tr, tn = _tile(hr, 256, 16), _tile(n, 1408)
    nr, nn = hr // tr, n // tn
    g_spec = _grad_half_spec(kind, tr, tn, nr, nn, lambda p, i, j, pos_: pos_[2 + p])
    r_spec = pl.BlockSpec((1, tr, tn), lambda p, i, j, pos_: (pos_[2 + p], i, j))
    o_spec = pl.BlockSpec((1, tr, tn), lambda p, i, j, pos_: (p, i, j))

    def body(pos_ref, g_ref, r_ref, o_ref):
        o_ref[0] = (g_ref[...] + r_ref[0]).astype(BF16)

    return pl.pallas_call(
        body,
        grid_spec=pltpu.PrefetchScalarGridSpec(num_scalar_prefetch=1, grid=(3, nr, nn), in_specs=[g_spec, r_spec],
                                               out_specs=o_spec),
        out_shape=SDS((3, hr, n), BF16),
        compiler_params=_params(("parallel", "parallel", "parallel")), name=name)(pos, grad, got)


def _chip_add(grad, got, got16, kind, shard_shape, pos, name, slab=0, slabs=1, prev=None):
    r, n = shard_shape
    hr = r // 2
    tr, tn = _tile(hr, 256, 16), _tile(n, 1408)
    nr, nn = hr // tr, n // tn
    g_spec = _grad_half_spec(kind, tr, tn, nr, nn, lambda i, j, pos_: pos_[0])
    r_spec = pl.BlockSpec((1, tr, tn), lambda i, j, pos_: (pos_[0], i, j))
    oth = pl.BlockSpec((3, tr, tn), lambda i, j, pos_: (0, i, j))

    def body(pos_ref, g_ref, r_ref, oth_ref, *rest):
        own = g_ref[...] + r_ref[0]
        rest[-1][0, 0] = ((own + oth_ref[0].astype(F32)) + oth_ref[1].astype(F32)) + oth_ref[2].astype(F32)

    return pl.pallas_call(
        body,
        grid_spec=pltpu.PrefetchScalarGridSpec(
            num_scalar_prefetch=1, grid=(nr, nn), in_specs=[g_spec, r_spec, oth] + [ANY] * (prev is not None),
            out_specs=pl.BlockSpec((1, 1, tr, tn), lambda i, j, pos_: (slab, pos_[1], i, j))),
        out_shape=SDS((slabs, 2, hr, n), F32), input_output_aliases={4: 0} if prev is not None else {},
        compiler_params=_params(("parallel", "parallel")),
        name=name)(pos, grad, got, got16, *(() if prev is None else (prev,)))


def _adamw_math(w, g, m, v):
    m = ADAM_B1 * m + (1.0 - ADAM_B1) * g
    v = ADAM_B2 * v + (1.0 - ADAM_B2) * (g * g)
    m_hat = m / (1.0 - ADAM_B1 ** ADAM_STEP)
    v_hat = v / (1.0 - ADAM_B2 ** ADAM_STEP)
    return -ADAM_LR * (m_hat / (jnp.sqrt(v_hat) + ADAM_EPS) + ADAM_WD * w), m, v


def _adamw(w, g, m, v, name):
    r, n = w.shape
    tr, tn = _tile(r, 256, 16), _tile(n, 1408)

    def body(w_ref, g_ref, m_ref, v_ref, d_ref, nm_ref, nv_ref, go_ref):
        gv = g_ref[...]
        d_ref[...], nm_ref[...], nv_ref[...] = _adamw_math(w_ref[...], gv, m_ref[...], v_ref[...])
        go_ref[...] = gv

    tile = pl.BlockSpec((tr, tn), lambda i, j: (i, j))
    return pl.pallas_call(
        body, grid=(r // tr, n // tn), in_specs=[tile] * 4, out_specs=[tile] * 4, out_shape=[SDS((r, n), F32)] * 4,
        compiler_params=_params(("parallel", "parallel")), name=name)(w, g, m, v)


def _small_allreduce_adamw(g, w, m, v, deps=()):
    length = g.shape[1]

    def body(*refs):
        g_ref, w_ref, m_ref, v_ref = refs[:4]
        gs_ref, d_ref, nm_ref, nv_ref, buf, send, recv = refs[4 + len(deps):]
        x, y, c = lax.axis_index("x"), lax.axis_index("y"), lax.axis_index("c")
        me = 4 * x + 2 * y + c
        buf[me] = g_ref[...]
        cps = []
        for d in range(1, 8):
            peer = (x ^ (d >> 2), y ^ ((d >> 1) & 1), c ^ (d & 1))
            cp = pltpu.make_async_remote_copy(src_ref=buf.at[me], dst_ref=buf.at[me], send_sem=send.at[d - 1],
                                              recv_sem=recv.at[d - 1], device_id=peer, device_id_type=MESH)
            cp.start()
            cps.append(cp)
        for cp in cps:
            cp.wait()
        total = buf[0]
        for d in range(1, 8):
            total = total + buf[d]
        gs_ref[...] = total
        d_ref[...], nm_ref[...], nv_ref[...] = _adamw_math(w_ref[...], total, m_ref[...], v_ref[...])

    vm = pl.BlockSpec(memory_space=pltpu.VMEM)
    return pl.pallas_call(
        body, in_specs=[vm] * 4 + [ANY] * len(deps), out_specs=[vm] * 4, out_shape=[SDS((1, length), F32)] * 4,
        scratch_shapes=[pltpu.VMEM((8, 1, length), F32), pltpu.SemaphoreType.DMA((7,)), pltpu.SemaphoreType.DMA((7,))],
        compiler_params=pltpu.CompilerParams(has_side_effects=True), name="small_allreduce_adamw")(g, w, m, v, *deps)


def kernel(x, w_in, b_gate, norm_mix, norm_ffn, hgrn_lb_logits, hgrn_out_gain, q_gain, k_gain, rel_bias, w_proj_a, w_proj_b, w_out, w_ffn_in, w_ffn_out, loss_target, m_w_in, m_b_gate, m_norm_mix, m_norm_ffn, m_hgrn_lb_logits, m_hgrn_out_gain, m_q_gain, m_k_gain, m_rel_bias, m_w_proj_a, m_w_proj_b, m_w_out, m_w_ffn_in, m_w_ffn_out, v_w_in, v_b_gate, v_norm_mix, v_norm_ffn, v_hgrn_lb_logits, v_hgrn_out_gain, v_q_gain, v_k_gain, v_rel_bias, v_w_proj_a, v_w_proj_b, v_w_out, v_w_ffn_in, v_w_ffn_out):
    t, d = x.shape[1], x.shape[2]
    d_a = hgrn_out_gain.shape[1]
    h_a = d_a // HEAD
    h_b = rel_bias.shape[1]
    d_b = h_b * HEAD
    x0 = x.reshape(t, d)
    target = loss_target.reshape(t, d)
    ax, ay = lax.axis_index("x"), lax.axis_index("y")
    pos = jnp.stack([2 * ax + ay, lax.axis_index("c"), 2 * (1 - ax) + ay, 2 * ax + 1 - ay,
                     2 * (1 - ax) + 1 - ay]).astype(jnp.int32)

    names = ["w_in", "w_proj_a", "w_proj_b", "w_out", "w_ffn_in", "w_ffn_out"]
    big = dict(zip(names, [w_in[0], w_proj_a[0], w_proj_b[0], w_out[0], w_ffn_in[0], w_ffn_out[0]]))
    big_m = dict(zip(names, [m_w_in[0], m_w_proj_a[0], m_w_proj_b[0], m_w_out[0], m_w_ffn_in[0], m_w_ffn_out[0]]))
    big_v = dict(zip(names, [v_w_in[0], v_w_proj_a[0], v_w_proj_b[0], v_w_out[0], v_w_ffn_in[0], v_w_ffn_out[0]]))
    kind = dict(zip(names, ["col", "col", "col", "row", "col", "row"]))
    shape = {nm: big[nm].shape for nm in names}

    def gather_start(tag, group, deps):
        plan, n = _plan_gather_ici([kind[g] for g in group], [shape[g] for g in group])
        fulls = [_cast_into_full(big[g], kind[g], pos, "cast_" + g) for g in group]
        send, recv, bufs, token = _start_copies("gather_ici_start_" + tag, fulls, plan, n, deps)
        return (tag, group, plan, send, recv, bufs), token

    def gather_pass(state, after, rels=ALL_RELATIONS, part=""):
        tag, group, _, send, recv, bufs = state
        kinds_, shapes_ = [kind[g] for g in group], [shape[g] for g in group]
        bufs = _wait_copies("gather_ici_wait_" + tag + part, bufs, send, recv,
                            _plan_gather_ici(kinds_, shapes_, rels)[0], after)
        plan, n = _plan_gather_pass(kinds_, shapes_, rels)
        send2, recv2, bufs, token = _start_copies("gather_pass_start_" + tag + part, bufs, plan, n, ())
        return (tag + part, group, plan, send2, recv2, bufs), token

    def gather_done(state, after):
        tag, group, plan, send, recv, bufs = state
        return _wait_copies("gather_pass_wait_" + tag, bufs, send, recv, plan, after)

    def reduce_start(tag, group, grads, deps):
        plan, n = _plan_pair([kind[g] for g in group], [shape[g] for g in group])
        lands = [lax.empty((4, shape[g][0] // 2, shape[g][1]), F32) for g in group]
        send, recv, bufs, token = _start_copies("pair_start_" + tag, list(grads) + lands, plan, n, deps)
        return dict(tag=tag, group=group, plan=plan, send=send, recv=recv, bufs=bufs), token

    def reduce_pair_done(st, after):
        tag, group, nw = st["tag"], st["group"], len(st["group"])
        bufs = _wait_copies("pair_wait_" + tag, st["bufs"], st["send"], st["recv"], st["plan"], after)
        grads, gots = bufs[:nw], bufs[nw:]
        parts = [_pair_add(g, l, kind[nm], shape[nm], pos, "pair_add_" + nm) for g, l, nm in zip(grads, gots, group)]
        lands = [lax.empty((3, shape[g][0] // 2, shape[g][1]), BF16) for g in group]
        plan, n = _plan_chip(nw)
        send, recv, bufs, token = _start_copies("chip_start_" + tag, parts + lands, plan, n, ())
        return dict(st, plan=plan, send=send, recv=recv, bufs=bufs, grads=grads, gots=gots), token

    def reduce_chip_wait(st, after, slab=0, slabs=1, prev=None):
        tag, group, nw = st["tag"], st["group"], len(st["group"])
        bufs = _wait_copies("chip_wait_" + tag, st["bufs"], st["send"], st["recv"], st["plan"], after)
        return [_chip_add(g, l, got16, kind[nm], shape[nm], pos, "chip_add_" + nm, slab, slabs, prev)
                for g, l, got16, nm in zip(st["grads"], st["gots"], bufs[nw:], group)]

    def reduce_share(tag, group, finals, slabs=1):
        plan, n = _plan_share([slabs] * len(finals))
        send, recv, bufs, token = _start_copies("share_start_" + tag, finals, plan, n, ())
        return dict(tag=tag, group=group, plan=plan, send=send, recv=recv, bufs=bufs), token

    def reduce_chip_done(st, after):
        return reduce_share(st["tag"], st["group"], reduce_chip_wait(st, after))

    g_big, upd = {}, {}

    def reduce_finish(st, after):
        bufs = _wait_copies("share_wait_" + st["tag"], st["bufs"], st["send"], st["recv"], st["plan"], after)
        for full, nm in zip(bufs, st["group"]):
            upd[nm] = _adamw(big[nm], full.reshape(shape[nm]), big_m[nm], big_v[nm], "adamw_" + nm)
            g_big[nm] = upd[nm][3]

    ga, token = gather_start("a", ["w_in"], ())
    gb, token = gather_start("b", ["w_proj_a", "w_proj_b", "w_out"], (token,))
    gc, token = gather_start("c", ["w_ffn_in"], (token,))
    gd, token = gather_start("d", ["w_ffn_out"], (token,))
    h1, r1 = _rmsnorm_fwd(x0, norm_mix, "rmsnorm_mix")
    rb = jnp.pad(rel_bias[0], ((0, 0), (0, REL_LANES - N_REL)))
    bias = _relbias_expand(rb).transpose(1, 0, 2)
    proj = _matmul_chunks(h1, big["w_in"], (0,), None, pos, "proj_in_own", own_shard=True)
    ici_a = ga
    ga, token = gather_pass(ici_a, (h1, bias, proj, token), rels=(0, 1), part="_near")
    (wg_in,) = gather_done(ga, ())
    proj = _matmul_chunks(h1, wg_in, (2, 3), proj, pos, "proj_in_near")
    ga, token = gather_pass(ici_a[:5] + ([wg_in],), (proj,), rels=(2,), part="_far")
    (wg_in,) = gather_done(ga, ())
    proj = _matmul_chunks(h1, wg_in, (4,), proj, pos, "proj_in_far")
    y_a, o_pre, states = _hgrn_fwd(proj, hgrn_lb_logits, hgrn_out_gain, h_a)
    gb, token = gather_pass(gb, (y_a,))
    col_b = 4 * d_a // HEAD
    y_b = _attn_fwd(proj, q_gain, k_gain, bias, h_b, col_b)
    wg_pa, wg_pb, wg_out = gather_done(gb, (y_b,))
    gate_off = 4 * d_a + 3 * d_b
    pa, pb, merged = _proj_merge(y_a, y_b, wg_pa, wg_pb, proj, b_gate, gate_off, deps=(token,))
    x2 = _matmul(merged, wg_out, res=x0, name="out_proj")
    gc, token = gather_pass(gc, (x2,))
    h2, r2 = _rmsnorm_fwd(x2, norm_ffn, "rmsnorm_ffn")
    (wg_fin,) = gather_done(gc, (h2,))
    ff_gate, ff_up, act = _ffn_in_swiglu(h2, wg_fin, deps=(token,))
    gd, token = gather_pass(gd, (act,))
    (wg_fout,) = gather_done(gd, ())
    dy, dy16, loss_part = _ffn_out_loss(act, wg_fout, x2, target)

    g_fout = _matmul(act, dy16, ta=True, name="dw_ffn_out")
    r_fout, token = reduce_start("fout", ["w_ffn_out"], [g_fout], ())
    dact = _matmul(dy16, wg_fout, tb=True, name="d_act", deps=(token,))
    r_fout, token = reduce_pair_done(r_fout, (dact,))
    dgu = _swiglu_bwd(dact, ff_gate, ff_up)
    g_fin = _matmul(h2, dgu, ta=True, name="dw_ffn_in", deps=(token,))
    r_fin, token = reduce_start("fin", ["w_ffn_in"], [g_fin], ())
    dh2 = _matmul(dgu, wg_fin, tb=True, name="d_h2", deps=(token,))
    r_fout, token_a = reduce_chip_done(r_fout, (dh2,))
    r_fin, token_b = reduce_pair_done(r_fin, (dh2,))
    dx2, dx2_16, g_norm_ffn = _rmsnorm_bwd(dh2, x2, r2, norm_ffn, dy, "rmsnorm_ffn_bwd", deps=(token_a, token_b))
    dmerged = _matmul(dx2_16, wg_out, tb=True, name="d_merged")
    dp_ab, dproj, g_bgate = _merge_bwd(dmerged, proj, b_gate, pa, pb, gate_off)
    g_out = _matmul(merged, dx2_16, ta=True, name="dw_out")
    g_pa = _matmul(y_a, dp_ab, ta=True, name="dw_proj_a", b_lead=0)
    g_pb = _matmul(y_b, dp_ab, ta=True, name="dw_proj_b", b_lead=1)
    r_mid, token = reduce_start("mid", ["w_proj_a", "w_proj_b", "w_out"], [g_pa, g_pb, g_out], ())
    dy_a = _matmul(dp_ab, wg_pa, tb=True, name="d_y_a", deps=(token,), a_lead=0)
    dy_b = _matmul(dp_ab, wg_pb, tb=True, name="d_y_b", a_lead=1)
    r_fin, token_a = reduce_chip_done(r_fin, (dy_b,))
    r_mid, token_b = reduce_pair_done(r_mid, (dy_b,))
    dproj, dbias, g_qg, g_kg = _attn_bwd(dproj, proj, q_gain, k_gain, bias, dy_b, h_b, col_b, deps=(token_a, token_b))
    r_mid, token = reduce_chip_done(r_mid, (dbias,))
    dproj, g_lb, g_gain = _hgrn_bwd(dproj, proj, o_pre, states, dy_a, hgrn_lb_logits, hgrn_out_gain, h_a, deps=(token,))
    g_in = _matmul(h1, dproj, ta=True, name="dw_in")
    r_in, token = reduce_start("in", ["w_in"], [g_in], ())
    g_rb = _relbias_reduce(dbias.transpose(1, 0, 2))[:, :N_REL]
    reduce_finish(r_mid, (token,))
    reduce_finish(r_fout, (token,))
    r_in, token = reduce_pair_done(r_in, (g_rb, upd["w_out"][0], upd["w_ffn_out"][0]))
    dh1 = _matmul(dproj, wg_in, tb=True, name="d_h1", deps=(token,))
    dx, _, g_norm_mix = _rmsnorm_bwd(dh1, x0, r1, norm_mix, dx2, "rmsnorm_mix_bwd")
    reduce_finish(r_fin, (dx,))
    r_in, token = reduce_chip_done(r_in, (upd["w_ffn_in"][0], upd["w_proj_a"][0], upd["w_proj_b"][0]))

    small_w = [b_gate, norm_mix, norm_ffn, hgrn_lb_logits, hgrn_out_gain, q_gain, k_gain, rel_bias]
    small_m = [m_b_gate, m_norm_mix, m_norm_ffn, m_hgrn_lb_logits, m_hgrn_out_gain, m_q_gain, m_k_gain, m_rel_bias]
    small_v = [v_b_gate, v_norm_mix, v_norm_ffn, v_hgrn_lb_logits, v_hgrn_out_gain, v_q_gain, v_k_gain, v_rel_bias]
    small_g = [g_bgate, g_norm_mix, g_norm_ffn, g_lb, g_gain, g_qg, g_kg, g_rb]
    sizes = [w.size for w in small_w]
    length = -(-(sum(sizes) + 1) // 128) * 128

    def pack(parts_):
        flat = jnp.concatenate([p.reshape(1, -1) for p in parts_], axis=1)
        return jnp.pad(flat, ((0, 0), (0, length - flat.shape[1])))

    one = jnp.ones((1, 1), F32)
    packed = _small_allreduce_adamw(pack(small_g + [loss_part]), pack(small_w + [one]), pack(small_m + [one]),
                                    pack(small_v + [one]), deps=(token,))

    def unpack(vec):
        out, at = [], 0
        for w, n in zip(small_w, sizes):
            out.append(vec[0, at:at + n].reshape(w.shape))
            at += n
        return out, vec[0, at]

    (sg, loss), (sd, _), (sm, _), (sv, _) = [unpack(p) for p in packed]
    reduce_finish(r_in, (packed[0],))

    def ordered(small, bigs):
        bigs = [bigs[nm][None] for nm in names]
        return [bigs[0]] + small + bigs[1:]

    return (loss, dx.reshape(x.shape), *ordered(sg, g_big), *ordered(sd, {nm: upd[nm][0] for nm in names}),
            *ordered(sm, {nm: upd[nm][1] for nm in names}), *ordered(sv, {nm: upd[nm][2] for nm in names}))
```

```python
import functools

import jax
import jax.numpy as jnp
from jax import lax
from jax.experimental import pallas as pl
from jax.experimental.pallas import tpu as pltpu

F32 = jnp.float32
BF16 = jnp.bfloat16
SDS = jax.ShapeDtypeStruct
MESH = pl.DeviceIdType.MESH
HIGHEST = lax.Precision.HIGHEST

CHUNK = 64
SUB = 16
HEAD = 128
N_PAST = 8
BAND = (N_PAST + 1) * CHUNK
PAD = N_PAST * CHUNK
REL_FUTURE = CHUNK - 1
REL_PAST = 2 * CHUNK - 1
N_REL = REL_FUTURE + REL_PAST + 1
REL_LANES = 256
EPS = 1e-6
MIX_HEADS = 2
MIX_UNROLL = 4
MIX_UNROLL_BWD = 4
ATT_UNROLL = 8
ATT_UNROLL_BWD = 4
EXP_CLAMP = 80.0

ADAM_LR = 0.001
ADAM_B1 = 0.9
ADAM_B2 = 0.999
ADAM_EPS = 1e-08
ADAM_WD = 0.01
ADAM_STEP = 10

VMEM_LIMIT = 56 * 1024 * 1024

HBM = pl.BlockSpec(memory_space=pltpu.HBM)
ANY = pl.BlockSpec(memory_space=pl.ANY)
SEM = pl.BlockSpec(memory_space=pltpu.SEMAPHORE)

NT = (((1,), (1,)), ((), ()))
TN = (((0,), (0,)), ((), ()))
NN = (((1,), (0,)), ((), ()))


def _params(sem=None, **kw):
    return pltpu.CompilerParams(dimension_semantics=sem, vmem_limit_bytes=VMEM_LIMIT, **kw)


def _tile(n, pref, unit=128):
    if n <= pref:
        return n
    t = pref - pref % unit
    while n % t:
        t -= unit
    return t


def _loop(n, unroll, step, init):
    assert n % unroll == 0, (n, unroll)

    def several(i, carry):
        for u in range(unroll):
            carry = step(i * unroll + u, carry)
        return carry

    return lax.fori_loop(0, n // unroll, several, init)


def _sigmoid(x):
    return 1.0 / (1.0 + jnp.exp(-x))


def _dsilu(x, s):
    return s * (1.0 + x * (1.0 - s))


def _bdot(a, b, dims=NN):
    return lax.dot_general(a.astype(BF16), b.astype(BF16), dims, preferred_element_type=F32)


def _split(a):
    hi = a.astype(BF16)
    return hi, (a - hi.astype(F32)).astype(BF16)


def _dot3(a, b, dims):
    dot = lambda u, v: lax.dot_general(u, v, dims, preferred_element_type=F32)
    return dot(a[0], b[1]) + dot(a[1], b[0]) + dot(a[0], b[0])


def _fdot(a, b):
    return lax.dot_general(a, b, NN, precision=HIGHEST, preferred_element_type=F32)


MM_TILE_K = 5632
MM_TILE_N = 512


def _matmul_chunks(h, w, which, prev, pos, name, own_shard=False, deps=()):
    t, d = h.shape
    nc_ = w.shape[1] if own_shard else w.shape[1] // 4
    tm, tn = _tile(t, 1024), _tile(nc_, 1408)
    nn = nc_ // tn

    def chunk(q, p):
        sel = p[which[0]]
        for i in range(1, len(which)):
            sel = jnp.where(q == i, p[which[i]], sel)
        return sel

    def body(p_ref, h_ref, w_ref, *rest):
        rest[-1][...] = jnp.dot(h_ref[...], w_ref[...].astype(BF16), preferred_element_type=F32)

    if own_shard:
        w_spec = pl.BlockSpec((d, tn), lambda q, i, j, p: (0, j))
    else:
        w_spec = pl.BlockSpec((d, tn), lambda q, i, j, p: (0, chunk(q, p) * nn + j))
    n_extra = len(deps) + (prev is not None)
    return pl.pallas_call(
        body,
        grid_spec=pltpu.PrefetchScalarGridSpec(
            num_scalar_prefetch=1, grid=(len(which), t // tm, nn),
            in_specs=[pl.BlockSpec((tm, d), lambda q, i, j, p: (i, 0)), w_spec] + [ANY] * n_extra,
            out_specs=pl.BlockSpec((tm, tn), lambda q, i, j, p: (i, chunk(q, p) * nn + j))),
        out_shape=SDS((t, 4 * nc_), F32), input_output_aliases={3 + len(deps): 0} if prev is not None else {},
        compiler_params=_params(("arbitrary", "arbitrary", "arbitrary")),
        name=name)(pos, h, w, *deps, *(() if prev is None else (prev,)))


def _matmul(a, b, *, ta=False, tb=False, res=None, out_dtype=F32, name, deps=(), a_lead=None, b_lead=None):
    a2, b2 = a.shape[-2:], b.shape[-2:]
    m, k = (a2[1], a2[0]) if ta else a2
    n = b2[0] if tb else b2[1]
    if k > MM_TILE_K:
        tk, tm, tn = _tile(k, MM_TILE_K // 2), _tile(m, 1024), _tile(n, 1024)
    else:
        tk = k
        tm, tn = _tile(m, 2048 if tk <= MM_TILE_K // 2 else 1024), _tile(n, MM_TILE_N)
    nk = k // tk
    dims = ((((0,) if ta else (1,)), ((1,) if tb else (0,))), ((), ()))

    def body(*refs):
        n_in = 2 + (res is not None)
        a_ref, b_ref = refs[:2]
        r_ref = refs[2] if res is not None else None
        o_ref = refs[n_in + len(deps)]
        part = lax.dot_general(a_ref[...].astype(BF16), b_ref[...].astype(BF16), dims, preferred_element_type=F32)

        def finish(out):
            if r_ref is not None:
                out = out + r_ref[...]
            o_ref[...] = out.astype(o_ref.dtype)

        if nk == 1:
            finish(part)
            return
        acc_ref = refs[-1]
        kk = pl.program_id(2)

        @pl.when(kk == 0)
        def _():
            acc_ref[...] = part

        @pl.when(jnp.logical_and(kk > 0, kk < nk - 1))
        def _():
            acc_ref[...] += part

        @pl.when(kk == nk - 1)
        def _():
            finish(acc_ref[...] + part)

    def spec(block, index, lead):
        if lead is None:
            return pl.BlockSpec(block, index)
        return pl.BlockSpec((None,) + block, lambda i, j, l: (lead,) + index(i, j, l))

    a_spec = spec((tk, tm), lambda i, j, l: (l, i), a_lead) if ta else spec((tm, tk), lambda i, j, l: (i, l), a_lead)
    b_spec = spec((tn, tk), lambda i, j, l: (j, l), b_lead) if tb else spec((tk, tn), lambda i, j, l: (l, j), b_lead)
    o_spec = pl.BlockSpec((tm, tn), lambda i, j, l: (i, j))
    in_specs = [a_spec, b_spec] + ([o_spec] if res is not None else []) + [ANY] * len(deps)
    args = (a, b) + ((res,) if res is not None else ()) + tuple(deps)
    return pl.pallas_call(
        body, grid=(m // tm, n // tn, nk), in_specs=in_specs, out_specs=o_spec,
        out_shape=SDS((m, n), out_dtype), scratch_shapes=[pltpu.VMEM((tm, tn), F32)] if nk > 1 else [],
        compiler_params=_params(("parallel", "parallel", "arbitrary")), name=name)(*args)


def _cast_into_full(w, kind, pos, name):
    r, n = w.shape
    tr = _tile(r, 512, 16)
    nr = r // tr
    if kind == "col":
        shape, o_spec = (r, 4 * n), pl.BlockSpec((tr, n), lambda i, p: (i, p[0]))
    else:
        shape, o_spec = (4 * r, n), pl.BlockSpec((tr, n), lambda i, p: (p[0] * nr + i, 0))

    def body(p_ref, w_ref, o_ref):
        o_ref[...] = w_ref[...].astype(BF16)

    return pl.pallas_call(
        body,
        grid_spec=pltpu.PrefetchScalarGridSpec(num_scalar_prefetch=1, grid=(nr,),
                                               in_specs=[pl.BlockSpec((tr, n), lambda i, p: (i, 0))], out_specs=o_spec),
        out_shape=SDS(shape, BF16), compiler_params=_params(("parallel",)), name=name)(pos, w)


def _rmsnorm_fwd(x, gain, name):
    t, d = x.shape
    tm = _tile(t, 256)

    def body(x_ref, g_ref, h_ref, r_ref):
        xv = x_ref[...]
        r = lax.rsqrt(jnp.mean(xv * xv, axis=-1, keepdims=True) + EPS)
        h_ref[...] = (xv * r * g_ref[...]).astype(BF16)
        r_ref[...] = r

    return pl.pallas_call(
        body, grid=(t // tm,),
        in_specs=[pl.BlockSpec((tm, d), lambda i: (i, 0)), pl.BlockSpec((1, d), lambda i: (0, 0))],
        out_specs=[pl.BlockSpec((tm, d), lambda i: (i, 0)), pl.BlockSpec((tm, 1), lambda i: (i, 0))],
        out_shape=[SDS((t, d), BF16), SDS((t, 1), F32)], compiler_params=_params(("parallel",)), name=name)(x, gain)


def _rmsnorm_bwd(dh, x, r, gain, dres, name, deps=()):
    t, d = x.shape
    tm = _tile(t, 256)

    def body(dh_ref, x_ref, r_ref, g_ref, dres_ref, *rest):
        dx_ref, dxb_ref, dg_ref = rest[len(deps):]

        @pl.when(pl.program_id(0) == 0)
        def _():
            dg_ref[...] = jnp.zeros_like(dg_ref)

        dhv, xv, rv = dh_ref[...], x_ref[...], r_ref[...]
        dg_ref[...] += jnp.sum(dhv * (xv * rv), axis=0, keepdims=True)
        u = dhv * g_ref[...]
        dx = dres_ref[...] + rv * u - xv * (rv * rv * rv) * jnp.mean(u * xv, axis=-1, keepdims=True)
        dx_ref[...] = dx
        dxb_ref[...] = dx.astype(BF16)

    row = pl.BlockSpec((tm, d), lambda i: (i, 0))
    vec = pl.BlockSpec((1, d), lambda i: (0, 0))
    return pl.pallas_call(
        body, grid=(t // tm,),
        in_specs=[row, row, pl.BlockSpec((tm, 1), lambda i: (i, 0)), vec, row] + [ANY] * len(deps),
        out_specs=[row, row, vec], out_shape=[SDS((t, d), F32), SDS((t, d), BF16), SDS((1, d), F32)],
        compiler_params=_params(("arbitrary",)), name=name)(dh, x, r, gain, dres, *deps)


def _proj_merge(y_a, y_b, w_a, w_b, proj, b_gate, off, deps=()):
    t, ka = y_a.shape
    kb = y_b.shape[1]
    d = w_a.shape[1]
    tm, tc = _tile(t, 1024), _tile(d, MM_TILE_N)
    nj = d // tc
    oa, ob = off // tc, off // tc + nj

    def body(ya_ref, yb_ref, wa_ref, wb_ref, la_ref, lb_ref, ba_ref, bb_ref, *rest):
        pa_ref, pb_ref, o_ref = rest[len(deps):]
        pa = jnp.dot(ya_ref[...], wa_ref[...], preferred_element_type=F32)
        pb = jnp.dot(yb_ref[...], wb_ref[...], preferred_element_type=F32)
        pa_ref[...] = pa
        pb_ref[...] = pb
        ga = _sigmoid(la_ref[...] + ba_ref[...])
        gb = _sigmoid(lb_ref[...] + bb_ref[...])
        o_ref[...] = (ga * pa + gb * pb).astype(BF16)

    tile = pl.BlockSpec((tm, tc), lambda i, j: (i, j))
    return pl.pallas_call(
        body, grid=(t // tm, nj),
        in_specs=[pl.BlockSpec((tm, ka), lambda i, j: (i, 0)), pl.BlockSpec((tm, kb), lambda i, j: (i, 0)),
                  pl.BlockSpec((ka, tc), lambda i, j: (0, j)), pl.BlockSpec((kb, tc), lambda i, j: (0, j)),
                  pl.BlockSpec((tm, tc), lambda i, j: (i, oa + j)), pl.BlockSpec((tm, tc), lambda i, j: (i, ob + j)),
                  pl.BlockSpec((1, tc), lambda i, j: (0, j)), pl.BlockSpec((1, tc), lambda i, j: (0, nj + j))]
        + [ANY] * len(deps),
        out_specs=[tile, tile, tile], out_shape=[SDS((t, d), F32), SDS((t, d), F32), SDS((t, d), BF16)],
        compiler_params=_params(("parallel", "parallel")),
        name="proj_merge")(y_a, y_b, w_a, w_b, proj, proj, b_gate, b_gate, *deps)


def _merge_bwd(dmerged, proj, b_gate, pa, pb, off):
    t, d = pa.shape
    tm, tc = _tile(t, 512), _tile(d, 512)
    nj, ni = d // tc, t // tm
    o0 = off // tc

    def body(dm_ref, l_ref, b_ref, pa_ref, pb_ref, dp_ref, dl_ref, db_ref):
        s, i = pl.program_id(0), pl.program_id(2)
        p = jnp.where(s == 0, pa_ref[...], pb_ref[...])
        g = _sigmoid(l_ref[...] + b_ref[...])
        dm = dm_ref[...]
        dp_ref[0] = (dm * g).astype(BF16)
        dl = dm * p * g * (1.0 - g)
        dl_ref[...] = dl.astype(BF16)

        @pl.when(i == 0)
        def _():
            db_ref[...] = jnp.zeros_like(db_ref)

        db_ref[...] += jnp.sum(dl, axis=0, keepdims=True)

    tile = pl.BlockSpec((tm, tc), lambda s, j, i: (i, j))
    return pl.pallas_call(
        body, grid=(2, nj, ni),
        in_specs=[tile, pl.BlockSpec((tm, tc), lambda s, j, i: (i, o0 + s * nj + j)),
                  pl.BlockSpec((1, tc), lambda s, j, i: (0, s * nj + j)), tile, tile],
        out_specs=[pl.BlockSpec((1, tm, tc), lambda s, j, i: (s, i, j)),
                   pl.BlockSpec((tm, tc), lambda s, j, i: (i, o0 + s * nj + j)),
                   pl.BlockSpec((1, tc), lambda s, j, i: (0, s * nj + j))],
        out_shape=[SDS((2, t, d), BF16), SDS(proj.shape, BF16), SDS((1, 2 * d), F32)],
        compiler_params=_params(("arbitrary", "arbitrary", "arbitrary")),
        name="merge_bwd")(dmerged, proj, b_gate, pa, pb)


def _ffn_in_swiglu(h, w, deps=()):
    t, d = h.shape
    f = w.shape[1] // 2
    tm, tn = _tile(t, 2048), _tile(f, MM_TILE_N)
    nj = f // tn

    def body(h_ref, wg_ref, wu_ref, *rest):
        g_ref, u_ref, a_ref = rest[len(deps):]
        hv = h_ref[...]
        g = jnp.dot(hv, wg_ref[...], preferred_element_type=F32)
        u = jnp.dot(hv, wu_ref[...], preferred_element_type=F32)
        g_ref[...] = g
        u_ref[...] = u
        a_ref[...] = (g * _sigmoid(g) * u).astype(BF16)

    tile = pl.BlockSpec((tm, tn), lambda i, j: (i, j))
    return pl.pallas_call(
        body, grid=(t // tm, nj),
        in_specs=[pl.BlockSpec((tm, d), lambda i, j: (i, 0)), pl.BlockSpec((d, tn), lambda i, j: (0, j)),
                  pl.BlockSpec((d, tn), lambda i, j: (0, nj + j))] + [ANY] * len(deps),
        out_specs=[tile, tile, tile], out_shape=[SDS((t, f), F32), SDS((t, f), F32), SDS((t, f), BF16)],
        compiler_params=_params(("parallel", "parallel")), name="ffn_in_swiglu")(h, w, w, *deps)


def _d_act_swiglu(dy, w, gate, up, deps=()):
    t, k = dy.shape
    f = w.shape[0]
    tm, tn = _tile(t, 1024), _tile(f, MM_TILE_N)
    ni, nj = t // tm, f // tn

    def body(dy_ref, w_ref, g_ref, u_ref, *rest):
        out_ref, stage, sems = rest[len(deps):]
        i, j = pl.program_id(0), pl.program_id(1)
        step = i * nj + j
        slot = step % 2

        def copies(s, ii, jj):
            rows = pl.ds(pl.multiple_of(ii * tm, tm), tm)
            return [pltpu.make_async_copy(
                stage.at[s, half], out_ref.at[rows, pl.ds(pl.multiple_of(half * f + jj * tn, 128), tn)],
                sems.at[s, half]) for half in range(2)]

        @pl.when(step >= 2)
        def _():
            for cp in copies(slot, 0, 0):
                cp.wait()

        dact = lax.dot_general(dy_ref[...], w_ref[...], NT, preferred_element_type=F32)
        g = g_ref[...]
        sg = _sigmoid(g)
        stage[slot, 0] = (dact * u_ref[...] * _dsilu(g, sg)).astype(BF16)
        stage[slot, 1] = (dact * (g * sg)).astype(BF16)
        for cp in copies(slot, i, j):
            cp.start()

        @pl.when(step == ni * nj - 1)
        def _():
            for s in range(min(2, ni * nj)):
                for cp in copies(s, 0, 0):
                    cp.wait()

    tile = pl.BlockSpec((tm, tn), lambda i, j: (i, j))
    return pl.pallas_call(
        body, grid=(ni, nj),
        in_specs=[pl.BlockSpec((tm, k), lambda i, j: (i, 0)), pl.BlockSpec((tn, k), lambda i, j: (j, 0)), tile, tile]
        + [ANY] * len(deps),
        out_specs=ANY, out_shape=SDS((t, 2 * f), BF16),
        scratch_shapes=[pltpu.VMEM((2, 2, tm, tn), BF16), pltpu.SemaphoreType.DMA((2, 2))],
        compiler_params=_params(("arbitrary", "arbitrary")), name="d_act_swiglu")(dy, w, gate, up, *deps)


def _ffn_out_loss(act, w, x_res, target):
    t, d = x_res.shape
    k = act.shape[1]
    tm, tn = _tile(t, 1024), _tile(d, MM_TILE_N)

    def body(a_ref, w_ref, r_ref, t_ref, dy_ref, dyb_ref, l_ref):
        @pl.when(jnp.logical_and(pl.program_id(0) == 0, pl.program_id(1) == 0))
        def _():
            l_ref[...] = jnp.zeros_like(l_ref)

        y = jnp.dot(a_ref[...], w_ref[...], preferred_element_type=F32) + r_ref[...]
        e = y - t_ref[...]
        dy = e * (1.0 / d)
        dy_ref[...] = dy
        dyb_ref[...] = dy.astype(BF16)
        l_ref[...] += (0.5 / d) * jnp.sum(jnp.sum(e * e, axis=-1, keepdims=True), axis=0, keepdims=True)

    tile = pl.BlockSpec((tm, tn), lambda i, j: (i, j))
    return pl.pallas_call(
        body, grid=(t // tm, d // tn),
        in_specs=[pl.BlockSpec((tm, k), lambda i, j: (i, 0)), pl.BlockSpec((k, tn), lambda i, j: (0, j)), tile, tile],
        out_specs=[tile, tile, pl.BlockSpec((1, 1), lambda i, j: (0, 0))],
        out_shape=[SDS((t, d), F32), SDS((t, d), BF16), SDS((1, 1), F32)],
        compiler_params=_params(("arbitrary", "arbitrary")), name="ffn_out_loss")(act, w, x_res, target)


def _rel_onehot(qi):
    p = lax.broadcasted_iota(jnp.int32, (REL_LANES, BAND), 1)
    r = lax.broadcasted_iota(jnp.int32, (REL_LANES, BAND), 0)
    idx = jnp.clip(qi + PAD - p, -REL_FUTURE, REL_PAST) + REL_FUTURE
    return (idx == r).astype(F32)


def _relbias_expand(rb):
    h = rb.shape[0]

    def body(rb_ref, o_ref):
        def step(qi, _):
            o_ref[qi] = _fdot(rb_ref[...], _rel_onehot(qi))
            return 0

        lax.fori_loop(0, CHUNK, step, 0)

    return pl.pallas_call(body, out_shape=SDS((CHUNK, h, BAND), F32), compiler_params=_params(),
                          name="relbias_expand")(rb)


def _relbias_reduce(dbias):
    h = dbias.shape[1]

    rows_per_pass = 4

    def body(db_ref, o_ref):
        def step(i, acc):
            parts = []
            for u in range(rows_per_pass):
                qi = i * rows_per_pass + u
                xv = db_ref[qi]
                hi = xv.astype(BF16)
                rest = xv - hi.astype(F32)
                mid = rest.astype(BF16)
                low = (rest - mid.astype(F32)).astype(BF16)
                parts.append(lax.dot_general(jnp.concatenate([hi, mid, low], axis=0), _rel_onehot(qi).astype(BF16), NT,
                                             preferred_element_type=F32))
            for part in parts:
                acc = acc + (part[0:h] + part[h:2 * h] + part[2 * h:3 * h])
            return acc

        o_ref[...] = lax.fori_loop(0, CHUNK // rows_per_pass, step, jnp.zeros((h, REL_LANES), F32))

    return pl.pallas_call(body, out_shape=SDS((h, REL_LANES), F32), compiler_params=_params(),
                          name="relbias_reduce")(dbias)


def _lower_bound(l_ref):
    l0, l1 = l_ref[0:1, :], l_ref[1:2, :]
    m = jnp.maximum(l0, l1)
    e0, e1 = jnp.exp(l0 - m), jnp.exp(l1 - m)
    return e0 / (e0 + e1)


def _tri(lower):
    r = lax.broadcasted_iota(jnp.int32, (CHUNK, CHUNK), 0)
    c = lax.broadcasted_iota(jnp.int32, (CHUNK, CHUNK), 1)
    return r >= c if lower else r <= c


def _hgrn_intra(qs, kk, b_s):
    rows = lax.broadcasted_iota(jnp.int32, (CHUNK, HEAD), 0)
    b = b_s[...]
    out = []
    for i in range(CHUNK // SUB):
        lo = i * SUB
        ref = jnp.zeros((1, HEAD), F32) if i == 0 else b_s[lo - 1:lo, :]
        eq = jnp.exp(b[lo:lo + SUB] - ref)
        qt = _split(qs[lo:lo + SUB] * eq)
        e = jnp.where(rows < lo + SUB, jnp.exp(jnp.minimum(ref - b, EXP_CLAMP)), 0.0)
        kt = _split(kk * e)
        out.append((eq, qt, e, kt))
    return out


def _hgrn_scores(blocks):
    tr = lax.broadcasted_iota(jnp.int32, (SUB, CHUNK), 0)
    tc = lax.broadcasted_iota(jnp.int32, (SUB, CHUNK), 1)
    return jnp.concatenate([jnp.where(tc <= tr + i * SUB, _dot3(qt, kt, NT), 0.0)
                            for i, (_, qt, _, kt) in enumerate(blocks)], axis=0)


def _hgrn_fwd(proj, lb_logits, gain, n_heads):
    t = proj.shape[0]
    nc = t // CHUNK
    da = n_heads * HEAD
    hp = MIX_HEADS
    wide = hp * HEAD

    def body(q_ref, f_ref, i_ref, g_ref, l_ref, gain_ref, y_ref, o_ref, st_ref, state, b_s):
        state[...] = jnp.zeros_like(state)
        lb_all = _lower_bound(l_ref)
        tril = _tri(True).astype(F32)

        def chunks(i, _):
            dot = functools.partial(lax.dot_general, preferred_element_type=F32)
            items = []
            for u in range(MIX_UNROLL):
                for hh in range(hp):
                    j = i * MIX_UNROLL + u
                    sl = pl.ds(pl.multiple_of(j * CHUNK, CHUNK), CHUNK)
                    cols = slice(hh * HEAD, (hh + 1) * HEAD)
                    lb = lb_all[:, cols]
                    fg = lb + (1.0 - lb) * _sigmoid(f_ref[sl, cols])
                    qv = q_ref[sl, cols]
                    gv = g_ref[sl, cols]
                    items.append(dict(hh=hh, j=j, sl=sl, cols=cols, lf=jnp.log(fg), kk=1.0 - fg, qs=qv * _sigmoid(qv),
                                      vb=i_ref[sl, cols].astype(BF16), gate=gv * _sigmoid(gv)))
            for it in items:
                it["b"] = _fdot(tril, it["lf"])
            for slot, it in enumerate(items):
                b = it["b"]
                b_s[slot] = b
                it["blocks"] = _hgrn_intra(it["qs"], it["kk"], b_s.at[slot])
                it["ebl"] = jnp.exp(b_s[slot, CHUNK - 1:CHUNK, :])
                it["qe"] = (it["qs"] * jnp.exp(b)).astype(BF16)
                it["kd"] = (it["kk"] * jnp.exp(b_s[slot, CHUNK - 1:CHUNK, :] - b)).astype(BF16)
            for it in items:
                it["a"] = _hgrn_scores(it["blocks"]).astype(BF16)
            for it in items:
                it["kv"] = dot(it["vb"], it["kd"], TN)
                it["o"] = dot(it["a"], it["vb"], NN)
            s_now = [state[hh] for hh in range(hp)]
            for it in items:
                it["s_in"] = s_now[it["hh"]]
                s_now[it["hh"]] = it["s_in"] * it["ebl"] + it["kv"]
            for hh in range(hp):
                state[hh] = s_now[hh]
            for it in items:
                it["o"] = it["o"] + dot(it["qe"], it["s_in"].astype(BF16), NT)
            for it in items:
                o, sl, cols = it["o"], it["sl"], it["cols"]
                st_ref[it["hh"], it["j"]] = it["s_in"]
                o_ref[sl, cols] = o
                rr = lax.rsqrt(jnp.mean(o * o, axis=-1, keepdims=True) + EPS)
                y_ref[sl, cols] = (o * rr * gain_ref[:, cols] * it["gate"]).astype(BF16)
            return 0

        assert nc % MIX_UNROLL == 0, (nc, MIX_UNROLL)
        lax.fori_loop(0, nc // MIX_UNROLL, chunks, 0)

    col = lambda k: pl.BlockSpec((t, wide), lambda h: (0, k * (n_heads // hp) + h))
    vec = pl.BlockSpec((1, wide), lambda h: (0, h))
    return pl.pallas_call(
        body, grid=(n_heads // hp,),
        in_specs=[col(0), col(1), col(2), col(3), pl.BlockSpec((2, wide), lambda h: (0, h)), vec],
        out_specs=[pl.BlockSpec((t, wide), lambda h: (0, h)), pl.BlockSpec((t, wide), lambda h: (0, h)),
                   pl.BlockSpec((hp, nc, HEAD, HEAD), lambda h: (h, 0, 0, 0))],
        out_shape=[SDS((t, da), BF16), SDS((t, da), F32), SDS((n_heads, nc, HEAD, HEAD), F32)],
        scratch_shapes=[pltpu.VMEM((hp, HEAD, HEAD), F32), pltpu.VMEM((hp * MIX_UNROLL, CHUNK, HEAD), F32)],
        compiler_params=_params(("parallel",)), name="hgrn_fwd")(proj, proj, proj, proj, lb_logits, gain)


def _write_column_groups(res, dproj_ref, sems, col0, stride, h, width):
    copies = [pltpu.make_async_copy(
        res.at[p], dproj_ref.at[:, pl.ds(pl.multiple_of((col0 + p * stride + h) * width, HEAD), width)], sems.at[p])
        for p in range(res.shape[0])]
    for cp in copies:
        cp.start()
    for cp in copies:
        cp.wait()


def _hgrn_bwd(dproj, proj, o_pre, states, dy, lb_logits, gain, n_heads, deps=()):
    t = proj.shape[0]
    nc = t // CHUNK
    da = n_heads * HEAD
    hp = MIX_HEADS
    wide = hp * HEAD

    def body(*refs):
        (q_ref, f_ref, i_ref, g_ref, o_ref, st_ref, dy_ref, l_ref, gain_ref,
         dproj_ref, dl_ref, dgain_ref, res, dstate, b_s, out_sems) = refs[1 + len(deps):]

        def compute():
            dstate[...] = jnp.zeros_like(dstate)
            lb_all = _lower_bound(l_ref)
            tril_m, tril, triu = _tri(True), _tri(True).astype(F32), _tri(False).astype(F32)
            last = lax.broadcasted_iota(jnp.int32, (CHUNK, HEAD), 0) == CHUNK - 1

            def chunks(i, carry):
                dot = functools.partial(lax.dot_general, preferred_element_type=F32)
                items = []
                for u in range(MIX_UNROLL_BWD):
                    for hh in range(hp):
                        j = nc - 1 - (i * MIX_UNROLL_BWD + u)
                        sl = pl.ds(pl.multiple_of(j * CHUNK, CHUNK), CHUNK)
                        cols = slice(hh * HEAD, (hh + 1) * HEAD)
                        lb, gain_v = lb_all[:, cols], gain_ref[:, cols]
                        sg = _sigmoid(f_ref[sl, cols])
                        fg = lb + (1.0 - lb) * sg
                        qv = q_ref[sl, cols]
                        sq = _sigmoid(qv)
                        gv = g_ref[sl, cols]
                        sgg = _sigmoid(gv)
                        silg = gv * sgg
                        o = o_ref[sl, cols]
                        dyv = dy_ref[sl, cols]
                        rr = lax.rsqrt(jnp.mean(o * o, axis=-1, keepdims=True) + EPS)
                        on = o * rr
                        don = dyv * gain_v * silg
                        do = (rr * don - o * (rr * rr * rr) * jnp.mean(don * o, axis=-1, keepdims=True)).astype(BF16)
                        items.append(dict(
                            hh=hh, j=j, sl=sl, cols=cols, lb=lb, sg=sg, fg=fg, kk=1.0 - fg, qv=qv, sq=sq, qs=qv * sq,
                            vb=i_ref[sl, cols].astype(BF16), do=do, dg=dyv * on * gain_v * _dsilu(gv, sgg),
                            dgain=jnp.sum(dyv * on * silg, axis=0, keepdims=True)))
                for it in items:
                    it["b"] = _fdot(tril, jnp.log(it["fg"]))
                for slot, it in enumerate(items):
                    b = it["b"]
                    b_s[slot] = b
                    it["blocks"] = _hgrn_intra(it["qs"], it["kk"], b_s.at[slot])
                    bl = b_s[slot, CHUNK - 1:CHUNK, :]
                    it["eb"], it["ebl"], it["ekd"] = jnp.exp(b), jnp.exp(bl), jnp.exp(bl - b)
                    it["s_in"] = st_ref[it["hh"], it["j"]]
                for it in items:
                    it["a"] = _hgrn_scores(it["blocks"]).astype(BF16)
                    it["da"] = jnp.where(tril_m, dot(it["do"], it["vb"], NT), 0.0)
                for it in items:
                    dq_rows = []
                    dk = jnp.zeros((CHUNK, HEAD), F32)
                    for blk, (eq, qt, e, kt) in enumerate(it["blocks"]):
                        da_i = _split(it["da"][blk * SUB:(blk + 1) * SUB])
                        dq_rows.append(eq * _dot3(da_i, kt, NN))
                        dk = dk + e * _dot3(da_i, qt, TN)
                    it["dq"] = jnp.concatenate(dq_rows, axis=0) + dot(it["do"], it["s_in"].astype(BF16), NN) * it["eb"]
                    it["dk"] = dk
                    it["dv"] = dot(it["a"], it["do"], TN)
                    it["g"] = dot(it["do"], (it["qs"] * it["eb"]).astype(BF16), TN)
                ds_now = [dstate[hh] for hh in range(hp)]
                for it in items:
                    it["ds_out"] = ds_now[it["hh"]]
                    ds_now[it["hh"]] = it["ds_out"] * it["ebl"] + it["g"]
                for hh in range(hp):
                    dstate[hh] = ds_now[hh]
                for it in items:
                    dsb = it["ds_out"].astype(BF16)
                    it["dv"] = it["dv"] + dot((it["kk"] * it["ekd"]).astype(BF16), dsb, NT)
                    it["dk_state"] = it["ekd"] * dot(it["vb"], dsb, NN)
                for it in items:
                    kk, dk_state = it["kk"], it["dk_state"]
                    it["dk"] = it["dk"] + dk_state
                    extra = (jnp.sum(kk * dk_state, axis=0, keepdims=True)
                             + it["ebl"] * jnp.sum(it["s_in"] * it["ds_out"], axis=0, keepdims=True))
                    it["db"] = it["qs"] * it["dq"] - kk * it["dk"] + jnp.where(last, extra, 0.0)
                for it in items:
                    it["dlf"] = _fdot(triu, it["db"])
                carry = list(carry)
                for it in items:
                    hh, sl, cols, sg, lb = it["hh"], it["sl"], it["cols"], it["sg"], it["lb"]
                    dfg = it["dlf"] / it["fg"] - it["dk"]
                    dlb_acc, dgain_acc = carry[hh]
                    carry[hh] = (dlb_acc + jnp.sum(dfg * (1.0 - sg), axis=0, keepdims=True), dgain_acc + it["dgain"])
                    res[0, sl, cols] = (it["dq"] * _dsilu(it["qv"], it["sq"])).astype(BF16)
                    res[1, sl, cols] = (dfg * (1.0 - lb) * sg * (1.0 - sg)).astype(BF16)
                    res[2, sl, cols] = it["dv"].astype(BF16)
                    res[3, sl, cols] = it["dg"].astype(BF16)
                return tuple(carry)

            assert nc % MIX_UNROLL_BWD == 0, (nc, MIX_UNROLL_BWD)
            zero = jnp.zeros((1, HEAD), F32)
            sums = lax.fori_loop(0, nc // MIX_UNROLL_BWD, chunks, ((zero, zero),) * hp)
            for hh, (dlb, dgain) in enumerate(sums):
                cols = slice(hh * HEAD, (hh + 1) * HEAD)
                lb = lb_all[:, cols]
                dgain_ref[:, cols] = dgain
                dl0 = dlb * lb * (1.0 - lb)
                dl_ref[0:1, cols] = dl0
                dl_ref[1:2, cols] = -dl0

        compute()
        _write_column_groups(res, dproj_ref, out_sems, 0, ng, pl.program_id(0), wide)

    ng = n_heads // hp
    col = lambda k: pl.BlockSpec((t, wide), lambda h: (0, k * ng + h))
    head = pl.BlockSpec((t, wide), lambda h: (0, h))
    vec = pl.BlockSpec((1, wide), lambda h: (0, h))
    return pl.pallas_call(
        body, grid=(ng,),
        in_specs=[ANY] * (1 + len(deps)) + [col(0), col(1), col(2), col(3), head,
                  pl.BlockSpec((hp, nc, HEAD, HEAD), lambda h: (h, 0, 0, 0)),
                  head, pl.BlockSpec((2, wide), lambda h: (0, h)), vec],
        out_specs=[ANY, pl.BlockSpec((2, wide), lambda h: (0, h)), vec],
        out_shape=[SDS(dproj.shape, BF16), SDS((2, da), F32), SDS((1, da), F32)],
        scratch_shapes=[pltpu.VMEM((4, t, wide), BF16), pltpu.VMEM((hp, HEAD, HEAD), F32),
                        pltpu.VMEM((hp * MIX_UNROLL_BWD, CHUNK, HEAD), F32), pltpu.SemaphoreType.DMA((4,))],
        input_output_aliases={0: 0}, compiler_params=_params(("arbitrary",)),
        name="hgrn_bwd")(dproj, *deps, proj, proj, proj, proj, o_pre, states, dy, lb_logits, gain)


ROWS = 256


def _head_norm(x_ref, gain, dst, dst_off, t):
    def step(i, _):
        sl = pl.ds(pl.multiple_of(i * ROWS, ROWS), ROWS)
        xv = x_ref[sl, :]
        r = lax.rsqrt(jnp.mean(xv * xv, axis=-1, keepdims=True) + EPS)
        dst[pl.ds(pl.multiple_of(dst_off + i * ROWS, ROWS), ROWS), :] = (xv * r * gain).astype(BF16)
        return 0

    lax.fori_loop(0, t // ROWS, step, 0)


def _head_norm_bwd(x_ref, gain, dn_ref, dn_off, out, slot, t):
    def step(i, acc):
        sl = pl.ds(pl.multiple_of(i * ROWS, ROWS), ROWS)
        xv = x_ref[sl, :]
        dn = dn_ref[pl.ds(pl.multiple_of(dn_off + i * ROWS, ROWS), ROWS), :]
        r = lax.rsqrt(jnp.mean(xv * xv, axis=-1, keepdims=True) + EPS)
        u = dn * gain
        out[slot, sl, :] = (r * u - xv * (r * r * r) * jnp.mean(u * xv, axis=-1, keepdims=True)).astype(out.dtype)
        return acc + jnp.sum(dn * (xv * r), axis=0, keepdims=True)

    return lax.fori_loop(0, t // ROWS, step, jnp.zeros((1, HEAD), F32))


def _attn_scores(qn, kpad, n):
    qc = qn[pl.ds(pl.multiple_of(n * CHUNK, CHUNK), CHUNK), :]
    band = pl.ds(pl.multiple_of(n * CHUNK, CHUNK), BAND)
    return qc, band, lax.dot_general(qc, kpad[band, :], NT, preferred_element_type=F32)


def _attn_softmax(raw, bias_ref, n):
    s = raw * (HEAD ** -0.5) + bias_ref[0]
    col = lax.broadcasted_iota(jnp.int32, (CHUNK, BAND), 1)
    s = jnp.where(col >= PAD - n * CHUNK, s, -jnp.inf)
    p = jnp.exp(s - jnp.max(s, axis=-1, keepdims=True))
    return p / jnp.sum(p, axis=-1, keepdims=True)


def _attn_fwd(proj, q_gain, k_gain, bias, n_heads, col0):
    t = proj.shape[0]
    nc = t // CHUNK

    def body(q_ref, k_ref, v_ref, qg_ref, kg_ref, bias_ref, y_ref, qn, kpad, vpad):
        kpad[0:PAD, :] = jnp.zeros((PAD, HEAD), BF16)
        vpad[0:PAD, :] = jnp.zeros((PAD, HEAD), BF16)
        _head_norm(q_ref, qg_ref[...], qn, 0, t)
        _head_norm(k_ref, kg_ref[...], kpad, PAD, t)

        def copy_v(i, _):
            vpad[pl.ds(pl.multiple_of(PAD + i * ROWS, ROWS), ROWS), :] = v_ref[
                pl.ds(pl.multiple_of(i * ROWS, ROWS), ROWS), :].astype(BF16)
            return 0

        lax.fori_loop(0, t // ROWS, copy_v, 0)

        def chunks(i, _):
            ns = [i * ATT_UNROLL + u for u in range(ATT_UNROLL)]
            scored = [_attn_scores(qn, kpad, n) for n in ns]
            probs = [_attn_softmax(raw, bias_ref, n).astype(BF16) for n, (_, _, raw) in zip(ns, scored)]
            outs = [lax.dot_general(p, vpad[band, :], NN, preferred_element_type=F32).astype(BF16)
                    for p, (_, band, _) in zip(probs, scored)]
            for n, o in zip(ns, outs):
                y_ref[pl.ds(pl.multiple_of(n * CHUNK, CHUNK), CHUNK), :] = o
            return 0

        assert nc % ATT_UNROLL == 0, (nc, ATT_UNROLL)
        lax.fori_loop(0, nc // ATT_UNROLL, chunks, 0)

    col = lambda k: pl.BlockSpec((t, HEAD), lambda h: (0, col0 + k * n_heads + h))
    vec = pl.BlockSpec((1, HEAD), lambda h: (0, 0))
    return pl.pallas_call(
        body, grid=(n_heads,),
        in_specs=[col(0), col(1), col(2), vec, vec, pl.BlockSpec((1, CHUNK, BAND), lambda h: (h, 0, 0))],
        out_specs=pl.BlockSpec((t, HEAD), lambda h: (0, h)), out_shape=SDS((t, n_heads * HEAD), BF16),
        scratch_shapes=[pltpu.VMEM((t, HEAD), BF16), pltpu.VMEM((t + PAD, HEAD), BF16), pltpu.VMEM((t + PAD, HEAD), BF16)],
        compiler_params=_params(("parallel",)), name="attn_fwd")(proj, proj, proj, q_gain, k_gain, bias)


def _attn_bwd(dproj, proj, q_gain, k_gain, bias, dy, n_heads, col0, deps=()):
    t = proj.shape[0]
    nc = t // CHUNK

    def body(*refs):
        (q_ref, k_ref, v_ref, qg_ref, kg_ref, bias_ref, dy_ref,
         dproj_ref, dbias_ref, dqg_ref, dkg_ref, qn, kpad, vpad, dqn, dk_acc, dv_acc, res,
         out_sems) = refs[1 + len(deps):]
        h = pl.program_id(0)

        def compute():
            kpad[0:PAD, :] = jnp.zeros((PAD, HEAD), BF16)
            vpad[0:PAD, :] = jnp.zeros((PAD, HEAD), BF16)
            _head_norm(q_ref, qg_ref[...], qn, 0, t)
            _head_norm(k_ref, kg_ref[...], kpad, PAD, t)

            def prep(i, _):
                sl = pl.ds(pl.multiple_of(PAD + i * ROWS, ROWS), ROWS)
                vpad[sl, :] = v_ref[pl.ds(pl.multiple_of(i * ROWS, ROWS), ROWS), :].astype(BF16)
                return 0

            lax.fori_loop(0, t // ROWS, prep, 0)

            def clear(i, _):
                sl = pl.ds(pl.multiple_of(i * ROWS, ROWS), ROWS)
                dk_acc[sl, :] = jnp.zeros((ROWS, HEAD), F32)
                dv_acc[sl, :] = jnp.zeros((ROWS, HEAD), F32)
                return 0

            lax.fori_loop(0, (t + PAD) // ROWS, clear, 0)
            dbias_ref[0] = jnp.zeros((CHUNK, BAND), F32)

            def chunks(i, _):
                dot = functools.partial(lax.dot_general, preferred_element_type=F32)
                ns = [i * ATT_UNROLL_BWD + u for u in range(ATT_UNROLL_BWD)]
                scored = [_attn_scores(qn, kpad, n) for n in ns]
                dos = [dy_ref[pl.ds(pl.multiple_of(n * CHUNK, CHUNK), CHUNK), :].astype(BF16) for n in ns]
                dps = [dot(do, vpad[band, :], NT) for do, (_, band, _) in zip(dos, scored)]
                ps, dss = [], []
                for n, (_, _, raw), dp in zip(ns, scored, dps):
                    p = _attn_softmax(raw, bias_ref, n)
                    ds = p * (dp - jnp.sum(dp * p, axis=-1, keepdims=True))
                    dbias_ref[0] += ds
                    ps.append(p.astype(BF16))
                    dss.append((ds * (HEAD ** -0.5)).astype(BF16))
                dqs = [dot(d, kpad[band, :], NN) for d, (_, band, _) in zip(dss, scored)]
                dks = [dot(d, qc, TN) for d, (qc, _, _) in zip(dss, scored)]
                dvs = [dot(p, do, TN) for p, do in zip(ps, dos)]
                for n, (_, band, _), dq, dk, dv in zip(ns, scored, dqs, dks, dvs):
                    dqn[pl.ds(pl.multiple_of(n * CHUNK, CHUNK), CHUNK), :] = dq
                    dk_acc[band, :] += dk
                    dv_acc[band, :] += dv
                return 0

            assert nc % ATT_UNROLL_BWD == 0, (nc, ATT_UNROLL_BWD)
            lax.fori_loop(0, nc // ATT_UNROLL_BWD, chunks, 0)
            dqg = _head_norm_bwd(q_ref, qg_ref[...], dqn, 0, res, 0, t)
            dkg = _head_norm_bwd(k_ref, kg_ref[...], dk_acc, PAD, res, 1, t)

            def put_v(i, _):
                sl = pl.ds(pl.multiple_of(i * ROWS, ROWS), ROWS)
                res[2, sl, :] = dv_acc[pl.ds(pl.multiple_of(PAD + i * ROWS, ROWS), ROWS), :].astype(BF16)
                return 0

            lax.fori_loop(0, t // ROWS, put_v, 0)

            @pl.when(h == 0)
            def _():
                dqg_ref[...] = jnp.zeros_like(dqg_ref)
                dkg_ref[...] = jnp.zeros_like(dkg_ref)

            dqg_ref[...] += dqg
            dkg_ref[...] += dkg

        compute()
        _write_column_groups(res, dproj_ref, out_sems, col0, n_heads, h, HEAD)

    col = lambda k: pl.BlockSpec((t, HEAD), lambda h: (0, col0 + k * n_heads + h))
    vec = pl.BlockSpec((1, HEAD), lambda h: (0, 0))
    btile = pl.BlockSpec((1, CHUNK, BAND), lambda h: (h, 0, 0))
    return pl.pallas_call(
        body, grid=(n_heads,),
        in_specs=[ANY] * (1 + len(deps)) + [col(0), col(1), col(2), vec, vec, btile,
                                            pl.BlockSpec((t, HEAD), lambda h: (0, h))],
        out_specs=[ANY, btile, vec, vec],
        out_shape=[SDS(dproj.shape, BF16), SDS((n_heads, CHUNK, BAND), F32), SDS((1, HEAD), F32), SDS((1, HEAD), F32)],
        scratch_shapes=[pltpu.VMEM((t, HEAD), BF16), pltpu.VMEM((t + PAD, HEAD), BF16), pltpu.VMEM((t + PAD, HEAD), BF16),
                        pltpu.VMEM((t, HEAD), F32), pltpu.VMEM((t + PAD, HEAD), F32), pltpu.VMEM((t + PAD, HEAD), F32),
                        pltpu.VMEM((3, t, HEAD), BF16), pltpu.SemaphoreType.DMA((3,))],
        input_output_aliases={0: 0}, compiler_params=_params(("arbitrary",)),
        name="attn_bwd")(dproj, *deps, proj, proj, proj, q_gain, k_gain, bias, dy)


def _place():
    x, y, c = lax.axis_index("x"), lax.axis_index("y"), lax.axis_index("c")
    others = [(1 - x, y), (x, 1 - y), (1 - x, 1 - y)]
    return x, y, c, others


def _chunk_of(ref, kind, chip, half, shard_shape):
    r, n = shard_shape
    hr = r // 2
    if kind == "col":
        rows = pl.ds(0, r) if half is None else pl.ds(half * hr, hr)
        return ref.at[rows, pl.ds(chip * n, n)]
    rows = pl.ds(chip * r, r) if half is None else pl.ds(chip * r + half * hr, hr)
    return ref.at[rows, :]


EFFECT = pltpu.SideEffectType.DATAFLOW_SIDE_EFFECTING


def _start_copies(name, bufs, plan, n, deps):
    nb, nd = len(bufs), len(deps)

    def body(*refs):
        send, recv, token = refs[nb + nd], refs[nb + nd + 1], refs[-1]
        for cp in plan(refs[:nb], send, recv)[0]:
            cp.start()
        token[...] = jnp.zeros_like(token)

    out = pl.pallas_call(
        body, name=name,
        out_shape=(pltpu.SemaphoreType.DMA((n,)), pltpu.SemaphoreType.DMA((n,)),
                   *[pltpu.HBM(b.shape, b.dtype) for b in bufs], SDS((8, 128), F32)),
        in_specs=[HBM] * nb + [ANY] * nd,
        out_specs=(SEM, SEM, *[HBM] * nb, pl.BlockSpec(memory_space=pltpu.VMEM)),
        input_output_aliases={i: 2 + i for i in range(nb)},
        compiler_params=pltpu.CompilerParams(has_side_effects=EFFECT),
    )(*[pltpu.with_memory_space_constraint(b, pltpu.HBM) for b in bufs], *deps)
    return out[0], out[1], list(out[2:2 + nb]), out[-1]


def _wait_copies(name, bufs, send, recv, plan, after):
    nb = len(bufs)

    def body(*refs):
        sends, recvs = plan(refs[:nb], refs[nb], refs[nb + 1])
        for cp in sends:
            cp.wait_send()
        for cp in recvs:
            cp.wait_recv()

    out = pl.pallas_call(
        body, name=name, out_shape=tuple(pltpu.HBM(b.shape, b.dtype) for b in bufs),
        in_specs=[HBM] * nb + [SEM, SEM] + [ANY] * len(after), out_specs=tuple([HBM] * nb),
        input_output_aliases={i: i for i in range(nb)},
        compiler_params=pltpu.CompilerParams(has_side_effects=EFFECT),
    )(*bufs, send, recv, *after)
    return list(out)


def _remote(src, dst, send, recv, i, dev):
    return pltpu.make_async_remote_copy(src_ref=src, dst_ref=dst, send_sem=send.at[i], recv_sem=recv.at[i],
                                        device_id=dev, device_id_type=MESH)


ALL_RELATIONS = (0, 1, 2)


def _plan_gather_ici(kinds, shapes, rels=ALL_RELATIONS):
    def plan(refs, send, recv):
        x, y, c, others = _place()
        sends, recvs = [], []
        for w, (kind, ss) in enumerate(zip(kinds, shapes)):
            for p in rels:
                px, py = others[p]
                mine = _chunk_of(refs[w], kind, 2 * x + y, c, ss)
                theirs = _chunk_of(refs[w], kind, 2 * px + py, c, ss)
                sends.append(_remote(mine, mine, send, recv, 3 * w + p, (px, py, c)))
                recvs.append(_remote(theirs, theirs, send, recv, 3 * w + p, (px, py, c)))
        return sends, recvs

    return plan, 3 * len(kinds)


def _plan_gather_pass(kinds, shapes, rels=ALL_RELATIONS):
    def plan(refs, send, recv):
        x, y, c, others = _place()
        sends, recvs = [], []
        for w, (kind, ss) in enumerate(zip(kinds, shapes)):
            for i, p in enumerate(rels):
                px, py = others[p]
                got = _chunk_of(refs[w], kind, 2 * px + py, c, ss)
                coming = _chunk_of(refs[w], kind, 2 * px + py, 1 - c, ss)
                sends.append(_remote(got, got, send, recv, len(rels) * w + i, (x, y, 1 - c)))
                recvs.append(_remote(coming, coming, send, recv, len(rels) * w + i, (x, y, 1 - c)))
        return sends, recvs

    return plan, len(rels) * len(kinds)


def _plan_pair(kinds, shapes):
    nw = len(kinds)

    def plan(refs, send, recv):
        x, y, c, _ = _place()
        sends = []
        for w, (kind, ss) in enumerate(zip(kinds, shapes)):
            for k in range(4):
                sends.append(_remote(_chunk_of(refs[w], kind, k, 1 - c, ss), refs[nw + w].at[k], send, recv,
                                     4 * w + k, (x, y, 1 - c)))
        return sends, sends

    return plan, 4 * nw


def _plan_chip(nw):
    def plan(refs, send, recv):
        x, y, c, others = _place()
        sends = []
        for w in range(nw):
            for p, (px, py) in enumerate(others):
                sends.append(_remote(refs[w].at[p], refs[nw + w].at[p], send, recv, 3 * w + p, (px, py, c)))
        return sends, sends

    return plan, 3 * nw


def _plan_share(slabs):
    def plan(refs, send, recv):
        x, y, c, _ = _place()
        sends, recvs, i = [], [], 0
        for w, ns in enumerate(slabs):
            for s in range(ns):
                sends.append(_remote(refs[w].at[s, c], refs[w].at[s, c], send, recv, i, (x, y, 1 - c)))
                recvs.append(_remote(refs[w].at[s, 1 - c], refs[w].at[s, 1 - c], send, recv, i, (x, y, 1 - c)))
                i += 1
        return sends, recvs

    return plan, sum(slabs)


def _grad_half_spec(kind, tr, tn, nr, nn, chunk):
    if kind == "col":
        return pl.BlockSpec((tr, tn), lambda *a: (a[-1][1] * nr + a[-3], chunk(*a) * nn + a[-2]))
    return pl.BlockSpec((tr, tn), lambda *a: ((2 * chunk(*a) + a[-1][1]) * nr + a[-3], a[-2]))


def _pair_add(grad, got, kind, shard_shape, pos, name):
    r, n = shard_shape
    hr = r // 2
    tr, tn = _tile(hr, 256, 16), _tile(n, 1408)
    nr, nn = hr // tr, n // tn
    g_spec = _grad_half_spec(kind, tr, tn, nr, nn, lambda p, i, j, pos_: pos_[2 + p])
    r_spec = pl.BlockSpec((1, tr, tn), lambda p, i, j, pos_: (pos_[2 + p], i, j))
    o_spec = pl.BlockSpec((1, tr, tn), lambda p, i, j, pos_: (p, i, j))

    def body(pos_ref, g_ref, r_ref, o_ref):
        o_ref[0] = (g_ref[...] + r_ref[0]).astype(BF16)

    return pl.pallas_call(
        body,
        grid_spec=pltpu.PrefetchScalarGridSpec(num_scalar_prefetch=1, grid=(3, nr, nn), in_specs=[g_spec, r_spec],
                                               out_specs=o_spec),
        out_shape=SDS((3, hr, n), BF16),
        compiler_params=_params(("parallel", "parallel", "parallel")), name=name)(pos, grad, got)


def _chip_add(grad, got, got16, kind, shard_shape, pos, name, slab=0, slabs=1, prev=None):
    r, n = shard_shape
    hr = r // 2
    tr, tn = _tile(hr, 256, 16), _tile(n, 1408)
    nr, nn = hr // tr, n // tn
    g_spec = _grad_half_spec(kind, tr, tn, nr, nn, lambda i, j, pos_: pos_[0])
    r_spec = pl.BlockSpec((1, tr, tn), lambda i, j, pos_: (pos_[0], i, j))
    oth = pl.BlockSpec((3, tr, tn), lambda i, j, pos_: (0, i, j))

    def body(pos_ref, g_ref, r_ref, oth_ref, *rest):
        own = g_ref[...] + r_ref[0]
        rest[-1][0, 0] = ((own + oth_ref[0].astype(F32)) + oth_ref[1].astype(F32)) + oth_ref[2].astype(F32)

    return pl.pallas_call(
        body,
        grid_spec=pltpu.PrefetchScalarGridSpec(
            num_scalar_prefetch=1, grid=(nr, nn), in_specs=[g_spec, r_spec, oth] + [ANY] * (prev is not None),
            out_specs=pl.BlockSpec((1, 1, tr, tn), lambda i, j, pos_: (slab, pos_[1], i, j))),
        out_shape=SDS((slabs, 2, hr, n), F32), input_output_aliases={4: 0} if prev is not None else {},
        compiler_params=_params(("parallel", "parallel")),
        name=name)(pos, grad, got, got16, *(() if prev is None else (prev,)))


def _adamw_math(w, g, m, v):
    m = ADAM_B1 * m + (1.0 - ADAM_B1) * g
    v = ADAM_B2 * v + (1.0 - ADAM_B2) * (g * g)
    m_hat = m / (1.0 - ADAM_B1 ** ADAM_STEP)
    v_hat = v / (1.0 - ADAM_B2 ** ADAM_STEP)
    return -ADAM_LR * (m_hat / (jnp.sqrt(v_hat) + ADAM_EPS) + ADAM_WD * w), m, v


def _adamw(w, g, m, v, name):
    r, n = w.shape
    tr, tn = _tile(r, 256, 16), _tile(n, 1408)

    def body(w_ref, g_ref, m_ref, v_ref, d_ref, nm_ref, nv_ref, go_ref):
        gv = g_ref[...]
        d_ref[...], nm_ref[...], nv_ref[...] = _adamw_math(w_ref[...], gv, m_ref[...], v_ref[...])
        go_ref[...] = gv

    tile = pl.BlockSpec((tr, tn), lambda i, j: (i, j))
    return pl.pallas_call(
        body, grid=(r // tr, n // tn), in_specs=[tile] * 4, out_specs=[tile] * 4, out_shape=[SDS((r, n), F32)] * 4,
        compiler_params=_params(("parallel", "parallel")), name=name)(w, g, m, v)


def _small_allreduce_adamw(g, w, m, v, deps=()):
    length = g.shape[1]

    def body(*refs):
        g_ref, w_ref, m_ref, v_ref = refs[:4]
        gs_ref, d_ref, nm_ref, nv_ref, buf, send, recv = refs[4 + len(deps):]
        x, y, c = lax.axis_index("x"), lax.axis_index("y"), lax.axis_index("c")
        me = 4 * x + 2 * y + c
        buf[me] = g_ref[...]
        cps = []
        for d in range(1, 8):
            peer = (x ^ (d >> 2), y ^ ((d >> 1) & 1), c ^ (d & 1))
            cp = pltpu.make_async_remote_copy(src_ref=buf.at[me], dst_ref=buf.at[me], send_sem=send.at[d - 1],
                                              recv_sem=recv.at[d - 1], device_id=peer, device_id_type=MESH)
            cp.start()
            cps.append(cp)
        for cp in cps:
            cp.wait()
        total = buf[0]
        for d in range(1, 8):
            total = total + buf[d]
        gs_ref[...] = total
        d_ref[...], nm_ref[...], nv_ref[...] = _adamw_math(w_ref[...], total, m_ref[...], v_ref[...])

    vm = pl.BlockSpec(memory_space=pltpu.VMEM)
    return pl.pallas_call(
        body, in_specs=[vm] * 4 + [ANY] * len(deps), out_specs=[vm] * 4, out_shape=[SDS((1, length), F32)] * 4,
        scratch_shapes=[pltpu.VMEM((8, 1, length), F32), pltpu.SemaphoreType.DMA((7,)), pltpu.SemaphoreType.DMA((7,))],
        compiler_params=pltpu.CompilerParams(has_side_effects=True), name="small_allreduce_adamw")(g, w, m, v, *deps)


def kernel(x, w_in, b_gate, norm_mix, norm_ffn, hgrn_lb_logits, hgrn_out_gain, q_gain, k_gain, rel_bias, w_proj_a, w_proj_b, w_out, w_ffn_in, w_ffn_out, loss_target, m_w_in, m_b_gate, m_norm_mix, m_norm_ffn, m_hgrn_lb_logits, m_hgrn_out_gain, m_q_gain, m_k_gain, m_rel_bias, m_w_proj_a, m_w_proj_b, m_w_out, m_w_ffn_in, m_w_ffn_out, v_w_in, v_b_gate, v_norm_mix, v_norm_ffn, v_hgrn_lb_logits, v_hgrn_out_gain, v_q_gain, v_k_gain, v_rel_bias, v_w_proj_a, v_w_proj_b, v_w_out, v_w_ffn_in, v_w_ffn_out):
    t, d = x.shape[1], x.shape[2]
    d_a = hgrn_out_gain.shape[1]
    h_a = d_a // HEAD
    h_b = rel_bias.shape[1]
    d_b = h_b * HEAD
    x0 = x.reshape(t, d)
    target = loss_target.reshape(t, d)
    ax, ay = lax.axis_index("x"), lax.axis_index("y")
    pos = jnp.stack([2 * ax + ay, lax.axis_index("c"), 2 * (1 - ax) + ay, 2 * ax + 1 - ay,
                     2 * (1 - ax) + 1 - ay]).astype(jnp.int32)

    names = ["w_in", "w_proj_a", "w_proj_b", "w_out", "w_ffn_in", "w_ffn_out"]
    big = dict(zip(names, [w_in[0], w_proj_a[0], w_proj_b[0], w_out[0], w_ffn_in[0], w_ffn_out[0]]))
    big_m = dict(zip(names, [m_w_in[0], m_w_proj_a[0], m_w_proj_b[0], m_w_out[0], m_w_ffn_in[0], m_w_ffn_out[0]]))
    big_v = dict(zip(names, [v_w_in[0], v_w_proj_a[0], v_w_proj_b[0], v_w_out[0], v_w_ffn_in[0], v_w_ffn_out[0]]))
    kind = dict(zip(names, ["col", "col", "col", "row", "col", "row"]))
    shape = {nm: big[nm].shape for nm in names}

    def gather_start(tag, group, deps):
        plan, n = _plan_gather_ici([kind[g] for g in group], [shape[g] for g in group])
        fulls = [_cast_into_full(big[g], kind[g], pos, "cast_" + g) for g in group]
        send, recv, bufs, token = _start_copies("gather_ici_start_" + tag, fulls, plan, n, deps)
        return (tag, group, plan, send, recv, bufs), token

    def gather_pass(state, after, rels=ALL_RELATIONS, part=""):
        tag, group, _, send, recv, bufs = state
        kinds_, shapes_ = [kind[g] for g in group], [shape[g] for g in group]
        bufs = _wait_copies("gather_ici_wait_" + tag + part, bufs, send, recv,
                            _plan_gather_ici(kinds_, shapes_, rels)[0], after)
        plan, n = _plan_gather_pass(kinds_, shapes_, rels)
        send2, recv2, bufs, token = _start_copies("gather_pass_start_" + tag + part, bufs, plan, n, ())
        return (tag + part, group, plan, send2, recv2, bufs), token

    def gather_done(state, after):
        tag, group, plan, send, recv, bufs = state
        return _wait_copies("gather_pass_wait_" + tag, bufs, send, recv, plan, after)

    def reduce_start(tag, group, grads, deps):
        plan, n = _plan_pair([kind[g] for g in group], [shape[g] for g in group])
        lands = [lax.empty((4, shape[g][0] // 2, shape[g][1]), F32) for g in group]
        send, recv, bufs, token = _start_copies("pair_start_" + tag, list(grads) + lands, plan, n, deps)
        return dict(tag=tag, group=group, plan=plan, send=send, recv=recv, bufs=bufs), token

    def reduce_pair_done(st, after):
        tag, group, nw = st["tag"], st["group"], len(st["group"])
        bufs = _wait_copies("pair_wait_" + tag, st["bufs"], st["send"], st["recv"], st["plan"], after)
        grads, gots = bufs[:nw], bufs[nw:]
        parts = [_pair_add(g, l, kind[nm], shape[nm], pos, "pair_add_" + nm) for g, l, nm in zip(grads, gots, group)]
        lands = [lax.empty((3, shape[g][0] // 2, shape[g][1]), BF16) for g in group]
        plan, n = _plan_chip(nw)
        send, recv, bufs, token = _start_copies("chip_start_" + tag, parts + lands, plan, n, ())
        return dict(st, plan=plan, send=send, recv=recv, bufs=bufs, grads=grads, gots=gots), token

    def reduce_chip_wait(st, after, slab=0, slabs=1, prev=None):
        tag, group, nw = st["tag"], st["group"], len(st["group"])
        bufs = _wait_copies("chip_wait_" + tag, st["bufs"], st["send"], st["recv"], st["plan"], after)
        return [_chip_add(g, l, got16, kind[nm], shape[nm], pos, "chip_add_" + nm, slab, slabs, prev)
                for g, l, got16, nm in zip(st["grads"], st["gots"], bufs[nw:], group)]

    def reduce_share(tag, group, finals, slabs=1):
        plan, n = _plan_share([slabs] * len(finals))
        send, recv, bufs, token = _start_copies("share_start_" + tag, finals, plan, n, ())
        return dict(tag=tag, group=group, plan=plan, send=send, recv=recv, bufs=bufs), token

    def reduce_chip_done(st, after):
        return reduce_share(st["tag"], st["group"], reduce_chip_wait(st, after))

    g_big, upd = {}, {}

    def reduce_finish(st, after):
        bufs = _wait_copies("share_wait_" + st["tag"], st["bufs"], st["send"], st["recv"], st["plan"], after)
        for full, nm in zip(bufs, st["group"]):
            upd[nm] = _adamw(big[nm], full.reshape(shape[nm]), big_m[nm], big_v[nm], "adamw_" + nm)
            g_big[nm] = upd[nm][3]

    ga, token = gather_start("a", ["w_in"], ())
    gb, token = gather_start("b", ["w_proj_a", "w_proj_b", "w_out"], (token,))
    gc, token = gather_start("c", ["w_ffn_in"], (token,))
    gd, token = gather_start("d", ["w_ffn_out"], (token,))
    h1, r1 = _rmsnorm_fwd(x0, norm_mix, "rmsnorm_mix")
    rb = jnp.pad(rel_bias[0], ((0, 0), (0, REL_LANES - N_REL)))
    bias = _relbias_expand(rb).transpose(1, 0, 2)
    proj = _matmul_chunks(h1, big["w_in"], (0,), None, pos, "proj_in_own", own_shard=True)
    ici_a = ga
    ga, token = gather_pass(ici_a, (h1, bias, proj, token), rels=(0, 1), part="_near")
    (wg_in,) = gather_done(ga, ())
    proj = _matmul_chunks(h1, wg_in, (2, 3), proj, pos, "proj_in_near")
    ga, token = gather_pass(ici_a[:5] + ([wg_in],), (proj,), rels=(2,), part="_far")
    (wg_in,) = gather_done(ga, ())
    proj = _matmul_chunks(h1, wg_in, (4,), proj, pos, "proj_in_far")
    y_a, o_pre, states = _hgrn_fwd(proj, hgrn_lb_logits, hgrn_out_gain, h_a)
    gb, token = gather_pass(gb, (y_a,))
    col_b = 4 * d_a // HEAD
    y_b = _attn_fwd(proj, q_gain, k_gain, bias, h_b, col_b)
    wg_pa, wg_pb, wg_out = gather_done(gb, (y_b,))
    gate_off = 4 * d_a + 3 * d_b
    pa, pb, merged = _proj_merge(y_a, y_b, wg_pa, wg_pb, proj, b_gate, gate_off, deps=(token,))
    x2 = _matmul(merged, wg_out, res=x0, name="out_proj")
    gc, token = gather_pass(gc, (x2,))
    h2, r2 = _rmsnorm_fwd(x2, norm_ffn, "rmsnorm_ffn")
    (wg_fin,) = gather_done(gc, (h2,))
    ff_gate, ff_up, act = _ffn_in_swiglu(h2, wg_fin, deps=(token,))
    gd, token = gather_pass(gd, (act,))
    (wg_fout,) = gather_done(gd, ())
    dy, dy16, loss_part = _ffn_out_loss(act, wg_fout, x2, target)

    g_fout = _matmul(act, dy16, ta=True, name="dw_ffn_out")
    r_fout, token = reduce_start("fout", ["w_ffn_out"], [g_fout], ())
    dgu = _d_act_swiglu(dy16, wg_fout, ff_gate, ff_up, deps=(token,))
    r_fout, token = reduce_pair_done(r_fout, (dgu,))
    g_fin = _matmul(h2, dgu, ta=True, name="dw_ffn_in", deps=(token,))
    r_fin, token = reduce_start("fin", ["w_ffn_in"], [g_fin], ())
    dh2 = _matmul(dgu, wg_fin, tb=True, name="d_h2", deps=(token,))
    r_fout, token_a = reduce_chip_done(r_fout, (dh2,))
    r_fin, token_b = reduce_pair_done(r_fin, (dh2,))
    dx2, dx2_16, g_norm_ffn = _rmsnorm_bwd(dh2, x2, r2, norm_ffn, dy, "rmsnorm_ffn_bwd", deps=(token_a, token_b))
    dmerged = _matmul(dx2_16, wg_out, tb=True, name="d_merged")
    dp_ab, dproj, g_bgate = _merge_bwd(dmerged, proj, b_gate, pa, pb, gate_off)
    g_out = _matmul(merged, dx2_16, ta=True, name="dw_out")
    g_pa = _matmul(y_a, dp_ab, ta=True, name="dw_proj_a", b_lead=0)
    g_pb = _matmul(y_b, dp_ab, ta=True, name="dw_proj_b", b_lead=1)
    r_mid, token = reduce_start("mid", ["w_proj_a", "w_proj_b", "w_out"], [g_pa, g_pb, g_out], ())
    dy_a = _matmul(dp_ab, wg_pa, tb=True, name="d_y_a", deps=(token,), a_lead=0)
    dy_b = _matmul(dp_ab, wg_pb, tb=True, name="d_y_b", a_lead=1)
    r_fin, token_a = reduce_chip_done(r_fin, (dy_b,))
    r_mid, token_b = reduce_pair_done(r_mid, (dy_b,))
    dproj, dbias, g_qg, g_kg = _attn_bwd(dproj, proj, q_gain, k_gain, bias, dy_b, h_b, col_b, deps=(token_a, token_b))
    r_mid, token = reduce_chip_done(r_mid, (dbias,))
    dproj, g_lb, g_gain = _hgrn_bwd(dproj, proj, o_pre, states, dy_a, hgrn_lb_logits, hgrn_out_gain, h_a, deps=(token,))
    g_in = _matmul(h1, dproj, ta=True, name="dw_in")
    r_in, token = reduce_start("in", ["w_in"], [g_in], ())
    g_rb = _relbias_reduce(dbias.transpose(1, 0, 2))[:, :N_REL]
    reduce_finish(r_mid, (token,))
    reduce_finish(r_fout, (token,))
    r_in, token = reduce_pair_done(r_in, (g_rb, upd["w_out"][0], upd["w_ffn_out"][0]))
    dh1 = _matmul(dproj, wg_in, tb=True, name="d_h1", deps=(token,))
    dx, _, g_norm_mix = _rmsnorm_bwd(dh1, x0, r1, norm_mix, dx2, "rmsnorm_mix_bwd")
    reduce_finish(r_fin, (dx,))
    r_in, token = reduce_chip_done(r_in, (upd["w_ffn_in"][0], upd["w_proj_a"][0], upd["w_proj_b"][0]))

    small_w = [b_gate, norm_mix, norm_ffn, hgrn_lb_logits, hgrn_out_gain, q_gain, k_gain, rel_bias]
    small_m = [m_b_gate, m_norm_mix, m_norm_ffn, m_hgrn_lb_logits, m_hgrn_out_gain, m_q_gain, m_k_gain, m_rel_bias]
    small_v = [v_b_gate, v_norm_mix, v_norm_ffn, v_hgrn_lb_logits, v_hgrn_out_gain, v_q_gain, v_k_gain, v_rel_bias]
    small_g = [g_bgate, g_norm_mix, g_norm_ffn, g_lb, g_gain, g_qg, g_kg, g_rb]
    sizes = [w.size for w in small_w]
    length = -(-(sum(sizes) + 1) // 128) * 128

    def pack(parts_):
        flat = jnp.concatenate([p.reshape(1, -1) for p in parts_], axis=1)
        return jnp.pad(flat, ((0, 0), (0, length - flat.shape[1])))

    one = jnp.ones((1, 1), F32)
    packed = _small_allreduce_adamw(pack(small_g + [loss_part]), pack(small_w + [one]), pack(small_m + [one]),
                                    pack(small_v + [one]), deps=(token,))

    def unpack(vec):
        out, at = [], 0
        for w, n in zip(small_w, sizes):
            out.append(vec[0, at:at + n].reshape(w.shape))
            at += n
        return out, vec[0, at]

    (sg, loss), (sd, _), (sm, _), (sv, _) = [unpack(p) for p in packed]
    reduce_finish(r_in, (packed[0],))

    def ordered(small, bigs):
        bigs = [bigs[nm][None] for nm in names]
        return [bigs[0]] + small + bigs[1:]

    return (loss, dx.reshape(x.shape), *ordered(sg, g_big), *ordered(sd, {nm: upd[nm][0] for nm in names}),
            *ordered(sm, {nm: upd[nm][1] for nm in names}), *ordered(sv, {nm: upd[nm][2] for nm in names}))
```

```python
import functools

import jax
import jax.numpy as jnp
from jax import lax
from jax.experimental import pallas as pl
from jax.experimental.pallas import tpu as pltpu

F32 = jnp.float32
BF16 = jnp.bfloat16
SDS = jax.ShapeDtypeStruct
MESH = pl.DeviceIdType.MESH
HIGHEST = lax.Precision.HIGHEST

CHUNK = 64
SUB = 16
HEAD = 128
N_PAST = 8
BAND = (N_PAST + 1) * CHUNK
PAD = N_PAST * CHUNK
REL_FUTURE = CHUNK - 1
REL_PAST = 2 * CHUNK - 1
N_REL = REL_FUTURE + REL_PAST + 1
REL_LANES = 256
EPS = 1e-6
MIX_HEADS = 2
MIX_UNROLL = 4
MIX_UNROLL_BWD = 4
ATT_UNROLL = 8
ATT_UNROLL_BWD = 4
EXP_CLAMP = 80.0

ADAM_LR = 0.001
ADAM_B1 = 0.9
ADAM_B2 = 0.999
ADAM_EPS = 1e-08
ADAM_WD = 0.01
ADAM_STEP = 10

VMEM_LIMIT = 56 * 1024 * 1024

HBM = pl.BlockSpec(memory_space=pltpu.HBM)
ANY = pl.BlockSpec(memory_space=pl.ANY)
SEM = pl.BlockSpec(memory_space=pltpu.SEMAPHORE)

NT = (((1,), (1,)), ((), ()))
TN = (((0,), (0,)), ((), ()))
NN = (((1,), (0,)), ((), ()))


def _params(sem=None, **kw):
    return pltpu.CompilerParams(dimension_semantics=sem, vmem_limit_bytes=VMEM_LIMIT, **kw)


def _tile(n, pref, unit=128):
    if n <= pref:
        return n
    t = pref - pref % unit
    while n % t:
        t -= unit
    return t


def _loop(n, unroll, step, init):
    assert n % unroll == 0, (n, unroll)

    def several(i, carry):
        for u in range(unroll):
            carry = step(i * unroll + u, carry)
        return carry

    return lax.fori_loop(0, n // unroll, several, init)


def _sigmoid(x):
    return 1.0 / (1.0 + jnp.exp(-x))


def _dsilu(x, s):
    return s * (1.0 + x * (1.0 - s))


def _bdot(a, b, dims=NN):
    return lax.dot_general(a.astype(BF16), b.astype(BF16), dims, preferred_element_type=F32)


def _split(a):
    hi = a.astype(BF16)
    return hi, (a - hi.astype(F32)).astype(BF16)


def _dot3(a, b, dims):
    dot = lambda u, v: lax.dot_general(u, v, dims, preferred_element_type=F32)
    return dot(a[0], b[1]) + dot(a[1], b[0]) + dot(a[0], b[0])


def _fdot(a, b):
    return lax.dot_general(a, b, NN, precision=HIGHEST, preferred_element_type=F32)


MM_TILE_K = 5632
MM_TILE_N = 512


def _matmul_chunks(h, w, which, prev, pos, name, own_shard=False, deps=()):
    t, d = h.shape
    nc_ = w.shape[1] if own_shard else w.shape[1] // 4
    tm, tn = _tile(t, 1024), _tile(nc_, 1408)
    nn = nc_ // tn

    def chunk(q, p):
        sel = p[which[0]]
        for i in range(1, len(which)):
            sel = jnp.where(q == i, p[which[i]], sel)
        return sel

    def body(p_ref, h_ref, w_ref, *rest):
        rest[-1][...] = jnp.dot(h_ref[...], w_ref[...].astype(BF16), preferred_element_type=F32)

    if own_shard:
        w_spec = pl.BlockSpec((d, tn), lambda q, i, j, p: (0, j))
    else:
        w_spec = pl.BlockSpec((d, tn), lambda q, i, j, p: (0, chunk(q, p) * nn + j))
    n_extra = len(deps) + (prev is not None)
    return pl.pallas_call(
        body,
        grid_spec=pltpu.PrefetchScalarGridSpec(
            num_scalar_prefetch=1, grid=(len(which), t // tm, nn),
            in_specs=[pl.BlockSpec((tm, d), lambda q, i, j, p: (i, 0)), w_spec] + [ANY] * n_extra,
            out_specs=pl.BlockSpec((tm, tn), lambda q, i, j, p: (i, chunk(q, p) * nn + j))),
        out_shape=SDS((t, 4 * nc_), F32), input_output_aliases={3 + len(deps): 0} if prev is not None else {},
        compiler_params=_params(("arbitrary", "arbitrary", "arbitrary")),
        name=name)(pos, h, w, *deps, *(() if prev is None else (prev,)))


def _matmul(a, b, *, ta=False, tb=False, res=None, out_dtype=F32, name, deps=(), a_lead=None, b_lead=None):
    a2, b2 = a.shape[-2:], b.shape[-2:]
    m, k = (a2[1], a2[0]) if ta else a2
    n = b2[0] if tb else b2[1]
    if k > MM_TILE_K:
        tk, tm, tn = _tile(k, MM_TILE_K // 2), _tile(m, 1024), _tile(n, 1024)
    else:
        tk = k
        tm, tn = _tile(m, 2048 if tk <= MM_TILE_K // 2 else 1024), _tile(n, MM_TILE_N)
    nk = k // tk
    dims = ((((0,) if ta else (1,)), ((1,) if tb else (0,))), ((), ()))

    def body(*refs):
        n_in = 2 + (res is not None)
        a_ref, b_ref = refs[:2]
        r_ref = refs[2] if res is not None else None
        o_ref = refs[n_in + len(deps)]
        part = lax.dot_general(a_ref[...].astype(BF16), b_ref[...].astype(BF16), dims, preferred_element_type=F32)

        def finish(out):
            if r_ref is not None:
                out = out + r_ref[...]
            o_ref[...] = out.astype(o_ref.dtype)

        if nk == 1:
            finish(part)
            return
        acc_ref = refs[-1]
        kk = pl.program_id(2)

        @pl.when(kk == 0)
        def _():
            acc_ref[...] = part

        @pl.when(jnp.logical_and(kk > 0, kk < nk - 1))
        def _():
            acc_ref[...] += part

        @pl.when(kk == nk - 1)
        def _():
            finish(acc_ref[...] + part)

    def spec(block, index, lead):
        if lead is None:
            return pl.BlockSpec(block, index)
        return pl.BlockSpec((None,) + block, lambda i, j, l: (lead,) + index(i, j, l))

    a_spec = spec((tk, tm), lambda i, j, l: (l, i), a_lead) if ta else spec((tm, tk), lambda i, j, l: (i, l), a_lead)
    b_spec = spec((tn, tk), lambda i, j, l: (j, l), b_lead) if tb else spec((tk, tn), lambda i, j, l: (l, j), b_lead)
    o_spec = pl.BlockSpec((tm, tn), lambda i, j, l: (i, j))
    in_specs = [a_spec, b_spec] + ([o_spec] if res is not None else []) + [ANY] * len(deps)
    args = (a, b) + ((res,) if res is not None else ()) + tuple(deps)
    return pl.pallas_call(
        body, grid=(m // tm, n // tn, nk), in_specs=in_specs, out_specs=o_spec,
        out_shape=SDS((m, n), out_dtype), scratch_shapes=[pltpu.VMEM((tm, tn), F32)] if nk > 1 else [],
        compiler_params=_params(("parallel", "parallel", "arbitrary")), name=name)(*args)


def _cast_into_full(w, kind, pos, name):
    r, n = w.shape
    tr = _tile(r, 512, 16)
    nr = r // tr
    if kind == "col":
        shape, o_spec = (r, 4 * n), pl.BlockSpec((tr, n), lambda i, p: (i, p[0]))
    else:
        shape, o_spec = (4 * r, n), pl.BlockSpec((tr, n), lambda i, p: (p[0] * nr + i, 0))

    def body(p_ref, w_ref, o_ref):
        o_ref[...] = w_ref[...].astype(BF16)

    return pl.pallas_call(
        body,
        grid_spec=pltpu.PrefetchScalarGridSpec(num_scalar_prefetch=1, grid=(nr,),
                                               in_specs=[pl.BlockSpec((tr, n), lambda i, p: (i, 0))], out_specs=o_spec),
        out_shape=SDS(shape, BF16), compiler_params=_params(("parallel",)), name=name)(pos, w)


def _rmsnorm_fwd(x, gain, name):
    t, d = x.shape
    tm = _tile(t, 256)

    def body(x_ref, g_ref, h_ref, r_ref):
        xv = x_ref[...]
        r = lax.rsqrt(jnp.mean(xv * xv, axis=-1, keepdims=True) + EPS)
        h_ref[...] = (xv * r * g_ref[...]).astype(BF16)
        r_ref[...] = r

    return pl.pallas_call(
        body, grid=(t // tm,),
        in_specs=[pl.BlockSpec((tm, d), lambda i: (i, 0)), pl.BlockSpec((1, d), lambda i: (0, 0))],
        out_specs=[pl.BlockSpec((tm, d), lambda i: (i, 0)), pl.BlockSpec((tm, 1), lambda i: (i, 0))],
        out_shape=[SDS((t, d), BF16), SDS((t, 1), F32)], compiler_params=_params(("parallel",)), name=name)(x, gain)


def _rmsnorm_bwd(dh, x, r, gain, dres, name, deps=()):
    t, d = x.shape
    tm = _tile(t, 256)

    def body(dh_ref, x_ref, r_ref, g_ref, dres_ref, *rest):
        dx_ref, dxb_ref, dg_ref = rest[len(deps):]

        @pl.when(pl.program_id(0) == 0)
        def _():
            dg_ref[...] = jnp.zeros_like(dg_ref)

        dhv, xv, rv = dh_ref[...], x_ref[...], r_ref[...]
        dg_ref[...] += jnp.sum(dhv * (xv * rv), axis=0, keepdims=True)
        u = dhv * g_ref[...]
        dx = dres_ref[...] + rv * u - xv * (rv * rv * rv) * jnp.mean(u * xv, axis=-1, keepdims=True)
        dx_ref[...] = dx
        dxb_ref[...] = dx.astype(BF16)

    row = pl.BlockSpec((tm, d), lambda i: (i, 0))
    vec = pl.BlockSpec((1, d), lambda i: (0, 0))
    return pl.pallas_call(
        body, grid=(t // tm,),
        in_specs=[row, row, pl.BlockSpec((tm, 1), lambda i: (i, 0)), vec, row] + [ANY] * len(deps),
        out_specs=[row, row, vec], out_shape=[SDS((t, d), F32), SDS((t, d), BF16), SDS((1, d), F32)],
        compiler_params=_params(("arbitrary",)), name=name)(dh, x, r, gain, dres, *deps)


def _proj_merge(y_a, y_b, w_a, w_b, proj, b_gate, off, deps=()):
    t, ka = y_a.shape
    kb = y_b.shape[1]
    d = w_a.shape[1]
    tm, tc = _tile(t, 1024), _tile(d, MM_TILE_N)
    nj = d // tc
    oa, ob = off // tc, off // tc + nj

    def body(ya_ref, yb_ref, wa_ref, wb_ref, la_ref, lb_ref, ba_ref, bb_ref, *rest):
        pa_ref, pb_ref, o_ref = rest[len(deps):]
        pa = jnp.dot(ya_ref[...], wa_ref[...], preferred_element_type=F32)
        pb = jnp.dot(yb_ref[...], wb_ref[...], preferred_element_type=F32)
        pa_ref[...] = pa
        pb_ref[...] = pb
        ga = _sigmoid(la_ref[...] + ba_ref[...])
        gb = _sigmoid(lb_ref[...] + bb_ref[...])
        o_ref[...] = (ga * pa + gb * pb).astype(BF16)

    tile = pl.BlockSpec((tm, tc), lambda i, j: (i, j))
    return pl.pallas_call(
        body, grid=(t // tm, nj),
        in_specs=[pl.BlockSpec((tm, ka), lambda i, j: (i, 0)), pl.BlockSpec((tm, kb), lambda i, j: (i, 0)),
                  pl.BlockSpec((ka, tc), lambda i, j: (0, j)), pl.BlockSpec((kb, tc), lambda i, j: (0, j)),
                  pl.BlockSpec((tm, tc), lambda i, j: (i, oa + j)), pl.BlockSpec((tm, tc), lambda i, j: (i, ob + j)),
                  pl.BlockSpec((1, tc), lambda i, j: (0, j)), pl.BlockSpec((1, tc), lambda i, j: (0, nj + j))]
        + [ANY] * len(deps),
        out_specs=[tile, tile, tile], out_shape=[SDS((t, d), F32), SDS((t, d), F32), SDS((t, d), BF16)],
        compiler_params=_params(("parallel", "parallel")),
        name="proj_merge")(y_a, y_b, w_a, w_b, proj, proj, b_gate, b_gate, *deps)


def _d_merged_gates(dx, w, proj, b_gate, pa, pb, off, deps=()):
    t, k = dx.shape
    d = w.shape[0]
    tm, tc = _tile(t, 1024), _tile(d, MM_TILE_N)
    nj, ni = d // tc, t // tm
    o0 = off // tc

    def body(dx_ref, w_ref, la_ref, lb_ref, ba_ref, bb_ref, pa_ref, pb_ref, *rest):
        dp_ref, dproj_ref, db_ref, stage, sems = rest[len(deps):]
        j, i = pl.program_id(0), pl.program_id(1)
        step = j * ni + i
        slot = step % 2

        def copies(s, ii, jj):
            rows = pl.ds(pl.multiple_of(ii * tm, tm), tm)
            return [pltpu.make_async_copy(
                stage.at[s, br], dproj_ref.at[rows, pl.ds(pl.multiple_of(off + br * d + jj * tc, 128), tc)],
                sems.at[s, br]) for br in range(2)]

        @pl.when(step >= 2)
        def _():
            for cp in copies(slot, 0, 0):
                cp.wait()

        dm = lax.dot_general(dx_ref[...], w_ref[...], NT, preferred_element_type=F32)

        @pl.when(i == 0)
        def _():
            db_ref[...] = jnp.zeros_like(db_ref)

        for br, (l_ref, b_ref, p_ref) in enumerate(((la_ref, ba_ref, pa_ref), (lb_ref, bb_ref, pb_ref))):
            g = _sigmoid(l_ref[...] + b_ref[...])
            dp_ref[br] = (dm * g).astype(BF16)
            dl = dm * p_ref[...] * g * (1.0 - g)
            stage[slot, br] = dl.astype(BF16)
            db_ref[br] += jnp.sum(dl, axis=0, keepdims=True)
        for cp in copies(slot, i, j):
            cp.start()

        @pl.when(step == ni * nj - 1)
        def _():
            for s in range(min(2, ni * nj)):
                for cp in copies(s, 0, 0):
                    cp.wait()

    tile = pl.BlockSpec((tm, tc), lambda j, i: (i, j))
    return pl.pallas_call(
        body, grid=(nj, ni),
        in_specs=[pl.BlockSpec((tm, k), lambda j, i: (i, 0)), pl.BlockSpec((tc, k), lambda j, i: (j, 0)),
                  pl.BlockSpec((tm, tc), lambda j, i: (i, o0 + j)), pl.BlockSpec((tm, tc), lambda j, i: (i, o0 + nj + j)),
                  pl.BlockSpec((1, tc), lambda j, i: (0, j)), pl.BlockSpec((1, tc), lambda j, i: (0, nj + j)),
                  tile, tile] + [ANY] * len(deps),
        out_specs=[pl.BlockSpec((2, tm, tc), lambda j, i: (0, i, j)), ANY, pl.BlockSpec((2, 1, tc), lambda j, i: (0, 0, j))],
        out_shape=[SDS((2, t, d), BF16), SDS(proj.shape, BF16), SDS((2, 1, d), F32)],
        scratch_shapes=[pltpu.VMEM((2, 2, tm, tc), BF16), pltpu.SemaphoreType.DMA((2, 2))],
        compiler_params=_params(("arbitrary", "arbitrary")),
        name="d_merged_gates")(dx, w, proj, proj, b_gate, b_gate, pa, pb, *deps)


def _ffn_in_swiglu(h, w, deps=()):
    t, d = h.shape
    f = w.shape[1] // 2
    tm, tn = _tile(t, 2048), _tile(f, MM_TILE_N)
    nj = f // tn

    def body(h_ref, wg_ref, wu_ref, *rest):
        g_ref, u_ref, a_ref = rest[len(deps):]
        hv = h_ref[...]
        g = jnp.dot(hv, wg_ref[...], preferred_element_type=F32)
        u = jnp.dot(hv, wu_ref[...], preferred_element_type=F32)
        g_ref[...] = g
        u_ref[...] = u
        a_ref[...] = (g * _sigmoid(g) * u).astype(BF16)

    tile = pl.BlockSpec((tm, tn), lambda i, j: (i, j))
    return pl.pallas_call(
        body, grid=(t // tm, nj),
        in_specs=[pl.BlockSpec((tm, d), lambda i, j: (i, 0)), pl.BlockSpec((d, tn), lambda i, j: (0, j)),
                  pl.BlockSpec((d, tn), lambda i, j: (0, nj + j))] + [ANY] * len(deps),
        out_specs=[tile, tile, tile], out_shape=[SDS((t, f), F32), SDS((t, f), F32), SDS((t, f), BF16)],
        compiler_params=_params(("parallel", "parallel")), name="ffn_in_swiglu")(h, w, w, *deps)


def _d_act_swiglu(dy, w, gate, up, deps=()):
    t, k = dy.shape
    f = w.shape[0]
    tm, tn = _tile(t, 1024), _tile(f, MM_TILE_N)
    ni, nj = t // tm, f // tn

    def body(dy_ref, w_ref, g_ref, u_ref, *rest):
        out_ref, stage, sems = rest[len(deps):]
        i, j = pl.program_id(0), pl.program_id(1)
        step = i * nj + j
        slot = step % 2

        def copies(s, ii, jj):
            rows = pl.ds(pl.multiple_of(ii * tm, tm), tm)
            return [pltpu.make_async_copy(
                stage.at[s, half], out_ref.at[rows, pl.ds(pl.multiple_of(half * f + jj * tn, 128), tn)],
                sems.at[s, half]) for half in range(2)]

        @pl.when(step >= 2)
        def _():
            for cp in copies(slot, 0, 0):
                cp.wait()

        dact = lax.dot_general(dy_ref[...], w_ref[...], NT, preferred_element_type=F32)
        g = g_ref[...]
        sg = _sigmoid(g)
        stage[slot, 0] = (dact * u_ref[...] * _dsilu(g, sg)).astype(BF16)
        stage[slot, 1] = (dact * (g * sg)).astype(BF16)
        for cp in copies(slot, i, j):
            cp.start()

        @pl.when(step == ni * nj - 1)
        def _():
            for s in range(min(2, ni * nj)):
                for cp in copies(s, 0, 0):
                    cp.wait()

    tile = pl.BlockSpec((tm, tn), lambda i, j: (i, j))
    return pl.pallas_call(
        body, grid=(ni, nj),
        in_specs=[pl.BlockSpec((tm, k), lambda i, j: (i, 0)), pl.BlockSpec((tn, k), lambda i, j: (j, 0)), tile, tile]
        + [ANY] * len(deps),
        out_specs=ANY, out_shape=SDS((t, 2 * f), BF16),
        scratch_shapes=[pltpu.VMEM((2, 2, tm, tn), BF16), pltpu.SemaphoreType.DMA((2, 2))],
        compiler_params=_params(("arbitrary", "arbitrary")), name="d_act_swiglu")(dy, w, gate, up, *deps)


def _ffn_out_loss(act, w, x_res, target):
    t, d = x_res.shape
    k = act.shape[1]
    tm, tn = _tile(t, 1024), _tile(d, MM_TILE_N)

    def body(a_ref, w_ref, r_ref, t_ref, dy_ref, dyb_ref, l_ref):
        @pl.when(jnp.logical_and(pl.program_id(0) == 0, pl.program_id(1) == 0))
        def _():
            l_ref[...] = jnp.zeros_like(l_ref)

        y = jnp.dot(a_ref[...], w_ref[...], preferred_element_type=F32) + r_ref[...]
        e = y - t_ref[...]
        dy = e * (1.0 / d)
        dy_ref[...] = dy
        dyb_ref[...] = dy.astype(BF16)
        l_ref[...] += (0.5 / d) * jnp.sum(jnp.sum(e * e, axis=-1, keepdims=True), axis=0, keepdims=True)

    tile = pl.BlockSpec((tm, tn), lambda i, j: (i, j))
    return pl.pallas_call(
        body, grid=(t // tm, d // tn),
        in_specs=[pl.BlockSpec((tm, k), lambda i, j: (i, 0)), pl.BlockSpec((k, tn), lambda i, j: (0, j)), tile, tile],
        out_specs=[tile, tile, pl.BlockSpec((1, 1), lambda i, j: (0, 0))],
        out_shape=[SDS((t, d), F32), SDS((t, d), BF16), SDS((1, 1), F32)],
        compiler_params=_params(("arbitrary", "arbitrary")), name="ffn_out_loss")(act, w, x_res, target)


def _rel_onehot(qi):
    p = lax.broadcasted_iota(jnp.int32, (REL_LANES, BAND), 1)
    r = lax.broadcasted_iota(jnp.int32, (REL_LANES, BAND), 0)
    idx = jnp.clip(qi + PAD - p, -REL_FUTURE, REL_PAST) + REL_FUTURE
    return (idx == r).astype(F32)


def _relbias_expand(rb):
    h = rb.shape[0]

    def body(rb_ref, o_ref):
        def step(qi, _):
            o_ref[qi] = _fdot(rb_ref[...], _rel_onehot(qi))
            return 0

        lax.fori_loop(0, CHUNK, step, 0)

    return pl.pallas_call(body, out_shape=SDS((CHUNK, h, BAND), F32), compiler_params=_params(),
                          name="relbias_expand")(rb)


def _relbias_reduce(dbias):
    h = dbias.shape[1]

    rows_per_pass = 4

    def body(db_ref, o_ref):
        def step(i, acc):
            parts = []
            for u in range(rows_per_pass):
                qi = i * rows_per_pass + u
                xv = db_ref[qi]
                hi = xv.astype(BF16)
                rest = xv - hi.astype(F32)
                mid = rest.astype(BF16)
                low = (rest - mid.astype(F32)).astype(BF16)
                parts.append(lax.dot_general(jnp.concatenate([hi, mid, low], axis=0), _rel_onehot(qi).astype(BF16), NT,
                                             preferred_element_type=F32))
            for part in parts:
                acc = acc + (part[0:h] + part[h:2 * h] + part[2 * h:3 * h])
            return acc

        o_ref[...] = lax.fori_loop(0, CHUNK // rows_per_pass, step, jnp.zeros((h, REL_LANES), F32))

    return pl.pallas_call(body, out_shape=SDS((h, REL_LANES), F32), compiler_params=_params(),
                          name="relbias_reduce")(dbias)


def _lower_bound(l_ref):
    l0, l1 = l_ref[0:1, :], l_ref[1:2, :]
    m = jnp.maximum(l0, l1)
    e0, e1 = jnp.exp(l0 - m), jnp.exp(l1 - m)
    return e0 / (e0 + e1)


def _tri(lower):
    r = lax.broadcasted_iota(jnp.int32, (CHUNK, CHUNK), 0)
    c = lax.broadcasted_iota(jnp.int32, (CHUNK, CHUNK), 1)
    return r >= c if lower else r <= c


def _hgrn_intra(qs, kk, b_s):
    rows = lax.broadcasted_iota(jnp.int32, (CHUNK, HEAD), 0)
    b = b_s[...]
    out = []
    for i in range(CHUNK // SUB):
        lo = i * SUB
        ref = jnp.zeros((1, HEAD), F32) if i == 0 else b_s[lo - 1:lo, :]
        eq = jnp.exp(b[lo:lo + SUB] - ref)
        qt = _split(qs[lo:lo + SUB] * eq)
        e = jnp.where(rows < lo + SUB, jnp.exp(jnp.minimum(ref - b, EXP_CLAMP)), 0.0)
        kt = _split(kk * e)
        out.append((eq, qt, e, kt))
    return out


def _hgrn_scores(blocks):
    tr = lax.broadcasted_iota(jnp.int32, (SUB, CHUNK), 0)
    tc = lax.broadcasted_iota(jnp.int32, (SUB, CHUNK), 1)
    return jnp.concatenate([jnp.where(tc <= tr + i * SUB, _dot3(qt, kt, NT), 0.0)
                            for i, (_, qt, _, kt) in enumerate(blocks)], axis=0)


def _hgrn_fwd(proj, lb_logits, gain, n_heads):
    t = proj.shape[0]
    nc = t // CHUNK
    da = n_heads * HEAD
    hp = MIX_HEADS
    wide = hp * HEAD

    def body(q_ref, f_ref, i_ref, g_ref, l_ref, gain_ref, y_ref, o_ref, st_ref, state, b_s):
        state[...] = jnp.zeros_like(state)
        lb_all = _lower_bound(l_ref)
        tril = _tri(True).astype(F32)

        def chunks(i, _):
            dot = functools.partial(lax.dot_general, preferred_element_type=F32)
            items = []
            for u in range(MIX_UNROLL):
                for hh in range(hp):
                    j = i * MIX_UNROLL + u
                    sl = pl.ds(pl.multiple_of(j * CHUNK, CHUNK), CHUNK)
                    cols = slice(hh * HEAD, (hh + 1) * HEAD)
                    lb = lb_all[:, cols]
                    fg = lb + (1.0 - lb) * _sigmoid(f_ref[sl, cols])
                    qv = q_ref[sl, cols]
                    gv = g_ref[sl, cols]
                    items.append(dict(hh=hh, j=j, sl=sl, cols=cols, lf=jnp.log(fg), kk=1.0 - fg, qs=qv * _sigmoid(qv),
                                      vb=i_ref[sl, cols].astype(BF16), gate=gv * _sigmoid(gv)))
            for it in items:
                it["b"] = _fdot(tril, it["lf"])
            for slot, it in enumerate(items):
                b = it["b"]
                b_s[slot] = b
                it["blocks"] = _hgrn_intra(it["qs"], it["kk"], b_s.at[slot])
                it["ebl"] = jnp.exp(b_s[slot, CHUNK - 1:CHUNK, :])
                it["qe"] = (it["qs"] * jnp.exp(b)).astype(BF16)
                it["kd"] = (it["kk"] * jnp.exp(b_s[slot, CHUNK - 1:CHUNK, :] - b)).astype(BF16)
            for it in items:
                it["a"] = _hgrn_scores(it["blocks"]).astype(BF16)
            for it in items:
                it["kv"] = dot(it["vb"], it["kd"], TN)
                it["o"] = dot(it["a"], it["vb"], NN)
            s_now = [state[hh] for hh in range(hp)]
            for it in items:
                it["s_in"] = s_now[it["hh"]]
                s_now[it["hh"]] = it["s_in"] * it["ebl"] + it["kv"]
            for hh in range(hp):
                state[hh] = s_now[hh]
            for it in items:
                it["o"] = it["o"] + dot(it["qe"], it["s_in"].astype(BF16), NT)
            for it in items:
                o, sl, cols = it["o"], it["sl"], it["cols"]
                st_ref[it["hh"], it["j"]] = it["s_in"]
                o_ref[sl, cols] = o
                rr = lax.rsqrt(jnp.mean(o * o, axis=-1, keepdims=True) + EPS)
                y_ref[sl, cols] = (o * rr * gain_ref[:, cols] * it["gate"]).astype(BF16)
            return 0

        assert nc % MIX_UNROLL == 0, (nc, MIX_UNROLL)
        lax.fori_loop(0, nc // MIX_UNROLL, chunks, 0)

    col = lambda k: pl.BlockSpec((t, wide), lambda h: (0, k * (n_heads // hp) + h))
    vec = pl.BlockSpec((1, wide), lambda h: (0, h))
    return pl.pallas_call(
        body, grid=(n_heads // hp,),
        in_specs=[col(0), col(1), col(2), col(3), pl.BlockSpec((2, wide), lambda h: (0, h)), vec],
        out_specs=[pl.BlockSpec((t, wide), lambda h: (0, h)), pl.BlockSpec((t, wide), lambda h: (0, h)),
                   pl.BlockSpec((hp, nc, HEAD, HEAD), lambda h: (h, 0, 0, 0))],
        out_shape=[SDS((t, da), BF16), SDS((t, da), F32), SDS((n_heads, nc, HEAD, HEAD), F32)],
        scratch_shapes=[pltpu.VMEM((hp, HEAD, HEAD), F32), pltpu.VMEM((hp * MIX_UNROLL, CHUNK, HEAD), F32)],
        compiler_params=_params(("parallel",)), name="hgrn_fwd")(proj, proj, proj, proj, lb_logits, gain)


def _write_column_groups(res, dproj_ref, sems, col0, stride, h, width):
    copies = [pltpu.make_async_copy(
        res.at[p], dproj_ref.at[:, pl.ds(pl.multiple_of((col0 + p * stride + h) * width, HEAD), width)], sems.at[p])
        for p in range(res.shape[0])]
    for cp in copies:
        cp.start()
    for cp in copies:
        cp.wait()


def _hgrn_bwd(dproj, proj, o_pre, states, dy, lb_logits, gain, n_heads, deps=()):
    t = proj.shape[0]
    nc = t // CHUNK
    da = n_heads * HEAD
    hp = MIX_HEADS
    wide = hp * HEAD

    def body(*refs):
        (q_ref, f_ref, i_ref, g_ref, o_ref, st_ref, dy_ref, l_ref, gain_ref,
         dproj_ref, dl_ref, dgain_ref, res, dstate, b_s, out_sems) = refs[1 + len(deps):]

        def compute():
            dstate[...] = jnp.zeros_like(dstate)
            lb_all = _lower_bound(l_ref)
            tril_m, tril, triu = _tri(True), _tri(True).astype(F32), _tri(False).astype(F32)
            last = lax.broadcasted_iota(jnp.int32, (CHUNK, HEAD), 0) == CHUNK - 1

            def chunks(i, carry):
                dot = functools.partial(lax.dot_general, preferred_element_type=F32)
                items = []
                for u in range(MIX_UNROLL_BWD):
                    for hh in range(hp):
                        j = nc - 1 - (i * MIX_UNROLL_BWD + u)
                        sl = pl.ds(pl.multiple_of(j * CHUNK, CHUNK), CHUNK)
                        cols = slice(hh * HEAD, (hh + 1) * HEAD)
                        lb, gain_v = lb_all[:, cols], gain_ref[:, cols]
                        sg = _sigmoid(f_ref[sl, cols])
                        fg = lb + (1.0 - lb) * sg
                        qv = q_ref[sl, cols]
                        sq = _sigmoid(qv)
                        gv = g_ref[sl, cols]
                        sgg = _sigmoid(gv)
                        silg = gv * sgg
                        o = o_ref[sl, cols]
                        dyv = dy_ref[sl, cols]
                        rr = lax.rsqrt(jnp.mean(o * o, axis=-1, keepdims=True) + EPS)
                        on = o * rr
                        don = dyv * gain_v * silg
                        do = (rr * don - o * (rr * rr * rr) * jnp.mean(don * o, axis=-1, keepdims=True)).astype(BF16)
                        items.append(dict(
                            hh=hh, j=j, sl=sl, cols=cols, lb=lb, sg=sg, fg=fg, kk=1.0 - fg, qv=qv, sq=sq, qs=qv * sq,
                            vb=i_ref[sl, cols].astype(BF16), do=do, dg=dyv * on * gain_v * _dsilu(gv, sgg),
                            dgain=jnp.sum(dyv * on * silg, axis=0, keepdims=True)))
                for it in items:
                    it["b"] = _fdot(tril, jnp.log(it["fg"]))
                for slot, it in enumerate(items):
                    b = it["b"]
                    b_s[slot] = b
                    it["blocks"] = _hgrn_intra(it["qs"], it["kk"], b_s.at[slot])
                    bl = b_s[slot, CHUNK - 1:CHUNK, :]
                    it["eb"], it["ebl"], it["ekd"] = jnp.exp(b), jnp.exp(bl), jnp.exp(bl - b)
                    it["s_in"] = st_ref[it["hh"], it["j"]]
                for it in items:
                    it["a"] = _hgrn_scores(it["blocks"]).astype(BF16)
                    it["da"] = jnp.where(tril_m, dot(it["do"], it["vb"], NT), 0.0)
                for it in items:
                    dq_rows = []
                    dk = jnp.zeros((CHUNK, HEAD), F32)
                    for blk, (eq, qt, e, kt) in enumerate(it["blocks"]):
                        da_i = _split(it["da"][blk * SUB:(blk + 1) * SUB])
                        dq_rows.append(eq * _dot3(da_i, kt, NN))
                        dk = dk + e * _dot3(da_i, qt, TN)
                    it["dq"] = jnp.concatenate(dq_rows, axis=0) + dot(it["do"], it["s_in"].astype(BF16), NN) * it["eb"]
                    it["dk"] = dk
                    it["dv"] = dot(it["a"], it["do"], TN)
                    it["g"] = dot(it["do"], (it["qs"] * it["eb"]).astype(BF16), TN)
                ds_now = [dstate[hh] for hh in range(hp)]
                for it in items:
                    it["ds_out"] = ds_now[it["hh"]]
                    ds_now[it["hh"]] = it["ds_out"] * it["ebl"] + it["g"]
                for hh in range(hp):
                    dstate[hh] = ds_now[hh]
                for it in items:
                    dsb = it["ds_out"].astype(BF16)
                    it["dv"] = it["dv"] + dot((it["kk"] * it["ekd"]).astype(BF16), dsb, NT)
                    it["dk_state"] = it["ekd"] * dot(it["vb"], dsb, NN)
                for it in items:
                    kk, dk_state = it["kk"], it["dk_state"]
                    it["dk"] = it["dk"] + dk_state
                    extra = (jnp.sum(kk * dk_state, axis=0, keepdims=True)
                             + it["ebl"] * jnp.sum(it["s_in"] * it["ds_out"], axis=0, keepdims=True))
                    it["db"] = it["qs"] * it["dq"] - kk * it["dk"] + jnp.where(last, extra, 0.0)
                for it in items:
                    it["dlf"] = _fdot(triu, it["db"])
                carry = list(carry)
                for it in items:
                    hh, sl, cols, sg, lb = it["hh"], it["sl"], it["cols"], it["sg"], it["lb"]
                    dfg = it["dlf"] / it["fg"] - it["dk"]
                    dlb_acc, dgain_acc = carry[hh]
                    carry[hh] = (dlb_acc + jnp.sum(dfg * (1.0 - sg), axis=0, keepdims=True), dgain_acc + it["dgain"])
                    res[0, sl, cols] = (it["dq"] * _dsilu(it["qv"], it["sq"])).astype(BF16)
                    res[1, sl, cols] = (dfg * (1.0 - lb) * sg * (1.0 - sg)).astype(BF16)
                    res[2, sl, cols] = it["dv"].astype(BF16)
                    res[3, sl, cols] = it["dg"].astype(BF16)
                return tuple(carry)

            assert nc % MIX_UNROLL_BWD == 0, (nc, MIX_UNROLL_BWD)
            zero = jnp.zeros((1, HEAD), F32)
            sums = lax.fori_loop(0, nc // MIX_UNROLL_BWD, chunks, ((zero, zero),) * hp)
            for hh, (dlb, dgain) in enumerate(sums):
                cols = slice(hh * HEAD, (hh + 1) * HEAD)
                lb = lb_all[:, cols]
                dgain_ref[:, cols] = dgain
                dl0 = dlb * lb * (1.0 - lb)
                dl_ref[0:1, cols] = dl0
                dl_ref[1:2, cols] = -dl0

        compute()
        _write_column_groups(res, dproj_ref, out_sems, 0, ng, pl.program_id(0), wide)

    ng = n_heads // hp
    col = lambda k: pl.BlockSpec((t, wide), lambda h: (0, k * ng + h))
    head = pl.BlockSpec((t, wide), lambda h: (0, h))
    vec = pl.BlockSpec((1, wide), lambda h: (0, h))
    return pl.pallas_call(
        body, grid=(ng,),
        in_specs=[ANY] * (1 + len(deps)) + [col(0), col(1), col(2), col(3), head,
                  pl.BlockSpec((hp, nc, HEAD, HEAD), lambda h: (h, 0, 0, 0)),
                  head, pl.BlockSpec((2, wide), lambda h: (0, h)), vec],
        out_specs=[ANY, pl.BlockSpec((2, wide), lambda h: (0, h)), vec],
        out_shape=[SDS(dproj.shape, BF16), SDS((2, da), F32), SDS((1, da), F32)],
        scratch_shapes=[pltpu.VMEM((4, t, wide), BF16), pltpu.VMEM((hp, HEAD, HEAD), F32),
                        pltpu.VMEM((hp * MIX_UNROLL_BWD, CHUNK, HEAD), F32), pltpu.SemaphoreType.DMA((4,))],
        input_output_aliases={0: 0}, compiler_params=_params(("arbitrary",)),
        name="hgrn_bwd")(dproj, *deps, proj, proj, proj, proj, o_pre, states, dy, lb_logits, gain)


ROWS = 256


def _head_norm(x_ref, gain, dst, dst_off, t):
    def step(i, _):
        sl = pl.ds(pl.multiple_of(i * ROWS, ROWS), ROWS)
        xv = x_ref[sl, :]
        r = lax.rsqrt(jnp.mean(xv * xv, axis=-1, keepdims=True) + EPS)
        dst[pl.ds(pl.multiple_of(dst_off + i * ROWS, ROWS), ROWS), :] = (xv * r * gain).astype(BF16)
        return 0

    lax.fori_loop(0, t // ROWS, step, 0)


def _head_norm_bwd(x_ref, gain, dn_ref, dn_off, out, slot, t):
    def step(i, acc):
        sl = pl.ds(pl.multiple_of(i * ROWS, ROWS), ROWS)
        xv = x_ref[sl, :]
        dn = dn_ref[pl.ds(pl.multiple_of(dn_off + i * ROWS, ROWS), ROWS), :]
        r = lax.rsqrt(jnp.mean(xv * xv, axis=-1, keepdims=True) + EPS)
        u = dn * gain
        out[slot, sl, :] = (r * u - xv * (r * r * r) * jnp.mean(u * xv, axis=-1, keepdims=True)).astype(out.dtype)
        return acc + jnp.sum(dn * (xv * r), axis=0, keepdims=True)

    return lax.fori_loop(0, t // ROWS, step, jnp.zeros((1, HEAD), F32))


def _attn_scores(qn, kpad, n):
    qc = qn[pl.ds(pl.multiple_of(n * CHUNK, CHUNK), CHUNK), :]
    band = pl.ds(pl.multiple_of(n * CHUNK, CHUNK), BAND)
    return qc, band, lax.dot_general(qc, kpad[band, :], NT, preferred_element_type=F32)


def _attn_softmax(raw, bias_ref, n):
    s = raw * (HEAD ** -0.5) + bias_ref[0]
    col = lax.broadcasted_iota(jnp.int32, (CHUNK, BAND), 1)
    s = jnp.where(col >= PAD - n * CHUNK, s, -jnp.inf)
    p = jnp.exp(s - jnp.max(s, axis=-1, keepdims=True))
    return p / jnp.sum(p, axis=-1, keepdims=True)


def _attn_fwd(proj, q_gain, k_gain, bias, n_heads, col0):
    t = proj.shape[0]
    nc = t // CHUNK

    def body(q_ref, k_ref, v_ref, qg_ref, kg_ref, bias_ref, y_ref, qn, kpad, vpad):
        kpad[0:PAD, :] = jnp.zeros((PAD, HEAD), BF16)
        vpad[0:PAD, :] = jnp.zeros((PAD, HEAD), BF16)
        _head_norm(q_ref, qg_ref[...], qn, 0, t)
        _head_norm(k_ref, kg_ref[...], kpad, PAD, t)

        def copy_v(i, _):
            vpad[pl.ds(pl.multiple_of(PAD + i * ROWS, ROWS), ROWS), :] = v_ref[
                pl.ds(pl.multiple_of(i * ROWS, ROWS), ROWS), :].astype(BF16)
            return 0

        lax.fori_loop(0, t // ROWS, copy_v, 0)

        def chunks(i, _):
            ns = [i * ATT_UNROLL + u for u in range(ATT_UNROLL)]
            scored = [_attn_scores(qn, kpad, n) for n in ns]
            probs = [_attn_softmax(raw, bias_ref, n).astype(BF16) for n, (_, _, raw) in zip(ns, scored)]
            outs = [lax.dot_general(p, vpad[band, :], NN, preferred_element_type=F32).astype(BF16)
                    for p, (_, band, _) in zip(probs, scored)]
            for n, o in zip(ns, outs):
                y_ref[pl.ds(pl.multiple_of(n * CHUNK, CHUNK), CHUNK), :] = o
            return 0

        assert nc % ATT_UNROLL == 0, (nc, ATT_UNROLL)
        lax.fori_loop(0, nc // ATT_UNROLL, chunks, 0)

    col = lambda k: pl.BlockSpec((t, HEAD), lambda h: (0, col0 + k * n_heads + h))
    vec = pl.BlockSpec((1, HEAD), lambda h: (0, 0))
    return pl.pallas_call(
        body, grid=(n_heads,),
        in_specs=[col(0), col(1), col(2), vec, vec, pl.BlockSpec((1, CHUNK, BAND), lambda h: (h, 0, 0))],
        out_specs=pl.BlockSpec((t, HEAD), lambda h: (0, h)), out_shape=SDS((t, n_heads * HEAD), BF16),
        scratch_shapes=[pltpu.VMEM((t, HEAD), BF16), pltpu.VMEM((t + PAD, HEAD), BF16), pltpu.VMEM((t + PAD, HEAD), BF16)],
        compiler_params=_params(("parallel",)), name="attn_fwd")(proj, proj, proj, q_gain, k_gain, bias)


def _attn_bwd(dproj, proj, q_gain, k_gain, bias, dy, n_heads, col0, deps=()):
    t = proj.shape[0]
    nc = t // CHUNK

    def body(*refs):
        (q_ref, k_ref, v_ref, qg_ref, kg_ref, bias_ref, dy_ref,
         dproj_ref, dbias_ref, dqg_ref, dkg_ref, qn, kpad, vpad, dqn, dk_acc, dv_acc, res,
         out_sems) = refs[1 + len(deps):]
        h = pl.program_id(0)

        def compute():
            kpad[0:PAD, :] = jnp.zeros((PAD, HEAD), BF16)
            vpad[0:PAD, :] = jnp.zeros((PAD, HEAD), BF16)
            _head_norm(q_ref, qg_ref[...], qn, 0, t)
            _head_norm(k_ref, kg_ref[...], kpad, PAD, t)

            def prep(i, _):
                sl = pl.ds(pl.multiple_of(PAD + i * ROWS, ROWS), ROWS)
                vpad[sl, :] = v_ref[pl.ds(pl.multiple_of(i * ROWS, ROWS), ROWS), :].astype(BF16)
                return 0

            lax.fori_loop(0, t // ROWS, prep, 0)

            def clear(i, _):
                sl = pl.ds(pl.multiple_of(i * ROWS, ROWS), ROWS)
                dk_acc[sl, :] = jnp.zeros((ROWS, HEAD), F32)
                dv_acc[sl, :] = jnp.zeros((ROWS, HEAD), F32)
                return 0

            lax.fori_loop(0, (t + PAD) // ROWS, clear, 0)
            dbias_ref[0] = jnp.zeros((CHUNK, BAND), F32)

            def chunks(i, _):
                dot = functools.partial(lax.dot_general, preferred_element_type=F32)
                ns = [i * ATT_UNROLL_BWD + u for u in range(ATT_UNROLL_BWD)]
                scored = [_attn_scores(qn, kpad, n) for n in ns]
                dos = [dy_ref[pl.ds(pl.multiple_of(n * CHUNK, CHUNK), CHUNK), :].astype(BF16) for n in ns]
                dps = [dot(do, vpad[band, :], NT) for do, (_, band, _) in zip(dos, scored)]
                ps, dss = [], []
                for n, (_, _, raw), dp in zip(ns, scored, dps):
                    p = _attn_softmax(raw, bias_ref, n)
                    ds = p * (dp - jnp.sum(dp * p, axis=-1, keepdims=True))
                    dbias_ref[0] += ds
                    ps.append(p.astype(BF16))
                    dss.append((ds * (HEAD ** -0.5)).astype(BF16))
                dqs = [dot(d, kpad[band, :], NN) for d, (_, band, _) in zip(dss, scored)]
                dks = [dot(d, qc, TN) for d, (qc, _, _) in zip(dss, scored)]
                dvs = [dot(p, do, TN) for p, do in zip(ps, dos)]
                for n, (_, band, _), dq, dk, dv in zip(ns, scored, dqs, dks, dvs):
                    dqn[pl.ds(pl.multiple_of(n * CHUNK, CHUNK), CHUNK), :] = dq
                    dk_acc[band, :] += dk
                    dv_acc[band, :] += dv
                return 0

            assert nc % ATT_UNROLL_BWD == 0, (nc, ATT_UNROLL_BWD)
            lax.fori_loop(0, nc // ATT_UNROLL_BWD, chunks, 0)
            dqg = _head_norm_bwd(q_ref, qg_ref[...], dqn, 0, res, 0, t)
            dkg = _head_norm_bwd(k_ref, kg_ref[...], dk_acc, PAD, res, 1, t)

            def put_v(i, _):
                sl = pl.ds(pl.multiple_of(i * ROWS, ROWS), ROWS)
                res[2, sl, :] = dv_acc[pl.ds(pl.multiple_of(PAD + i * ROWS, ROWS), ROWS), :].astype(BF16)
                return 0

            lax.fori_loop(0, t // ROWS, put_v, 0)

            @pl.when(h == 0)
            def _():
                dqg_ref[...] = jnp.zeros_like(dqg_ref)
                dkg_ref[...] = jnp.zeros_like(dkg_ref)

            dqg_ref[...] += dqg
            dkg_ref[...] += dkg

        compute()
        _write_column_groups(res, dproj_ref, out_sems, col0, n_heads, h, HEAD)

    col = lambda k: pl.BlockSpec((t, HEAD), lambda h: (0, col0 + k * n_heads + h))
    vec = pl.BlockSpec((1, HEAD), lambda h: (0, 0))
    btile = pl.BlockSpec((1, CHUNK, BAND), lambda h: (h, 0, 0))
    return pl.pallas_call(
        body, grid=(n_heads,),
        in_specs=[ANY] * (1 + len(deps)) + [col(0), col(1), col(2), vec, vec, btile,
                                            pl.BlockSpec((t, HEAD), lambda h: (0, h))],
        out_specs=[ANY, btile, vec, vec],
        out_shape=[SDS(dproj.shape, BF16), SDS((n_heads, CHUNK, BAND), F32), SDS((1, HEAD), F32), SDS((1, HEAD), F32)],
        scratch_shapes=[pltpu.VMEM((t, HEAD), BF16), pltpu.VMEM((t + PAD, HEAD), BF16), pltpu.VMEM((t + PAD, HEAD), BF16),
                        pltpu.VMEM((t, HEAD), F32), pltpu.VMEM((t + PAD, HEAD), F32), pltpu.VMEM((t + PAD, HEAD), F32),
                        pltpu.VMEM((3, t, HEAD), BF16), pltpu.SemaphoreType.DMA((3,))],
        input_output_aliases={0: 0}, compiler_params=_params(("arbitrary",)),
        name="attn_bwd")(dproj, *deps, proj, proj, proj, q_gain, k_gain, bias, dy)


def _place():
    x, y, c = lax.axis_index("x"), lax.axis_index("y"), lax.axis_index("c")
    others = [(1 - x, y), (x, 1 - y), (1 - x, 1 - y)]
    return x, y, c, others


def _chunk_of(ref, kind, chip, half, shard_shape):
    r, n = shard_shape
    hr = r // 2
    if kind == "col":
        rows = pl.ds(0, r) if half is None else pl.ds(half * hr, hr)
        return ref.at[rows, pl.ds(chip * n, n)]
    rows = pl.ds(chip * r, r) if half is None else pl.ds(chip * r + half * hr, hr)
    return ref.at[rows, :]


EFFECT = pltpu.SideEffectType.DATAFLOW_SIDE_EFFECTING


def _start_copies(name, bufs, plan, n, deps):
    nb, nd = len(bufs), len(deps)

    def body(*refs):
        send, recv, token = refs[nb + nd], refs[nb + nd + 1], refs[-1]
        for cp in plan(refs[:nb], send, recv)[0]:
            cp.start()
        token[...] = jnp.zeros_like(token)

    out = pl.pallas_call(
        body, name=name,
        out_shape=(pltpu.SemaphoreType.DMA((n,)), pltpu.SemaphoreType.DMA((n,)),
                   *[pltpu.HBM(b.shape, b.dtype) for b in bufs], SDS((8, 128), F32)),
        in_specs=[HBM] * nb + [ANY] * nd,
        out_specs=(SEM, SEM, *[HBM] * nb, pl.BlockSpec(memory_space=pltpu.VMEM)),
        input_output_aliases={i: 2 + i for i in range(nb)},
        compiler_params=pltpu.CompilerParams(has_side_effects=EFFECT),
    )(*[pltpu.with_memory_space_constraint(b, pltpu.HBM) for b in bufs], *deps)
    return out[0], out[1], list(out[2:2 + nb]), out[-1]


def _wait_copies(name, bufs, send, recv, plan, after):
    nb = len(bufs)

    def body(*refs):
        sends, recvs = plan(refs[:nb], refs[nb], refs[nb + 1])
        for cp in sends:
            cp.wait_send()
        for cp in recvs:
            cp.wait_recv()

    out = pl.pallas_call(
        body, name=name, out_shape=tuple(pltpu.HBM(b.shape, b.dtype) for b in bufs),
        in_specs=[HBM] * nb + [SEM, SEM] + [ANY] * len(after), out_specs=tuple([HBM] * nb),
        input_output_aliases={i: i for i in range(nb)},
        compiler_params=pltpu.CompilerParams(has_side_effects=EFFECT),
    )(*bufs, send, recv, *after)
    return list(out)


def _remote(src, dst, send, recv, i, dev):
    return pltpu.make_async_remote_copy(src_ref=src, dst_ref=dst, send_sem=send.at[i], recv_sem=recv.at[i],
                                        device_id=dev, device_id_type=MESH)


ALL_RELATIONS = (0, 1, 2)


def _plan_gather_ici(kinds, shapes, rels=ALL_RELATIONS):
    def plan(refs, send, recv):
        x, y, c, others = _place()
        sends, recvs = [], []
        for w, (kind, ss) in enumerate(zip(kinds, shapes)):
            for p in rels:
                px, py = others[p]
                mine = _chunk_of(refs[w], kind, 2 * x + y, c, ss)
                theirs = _chunk_of(refs[w], kind, 2 * px + py, c, ss)
                sends.append(_remote(mine, mine, send, recv, 3 * w + p, (px, py, c)))
                recvs.append(_remote(theirs, theirs, send, recv, 3 * w + p, (px, py, c)))
        return sends, recvs

    return plan, 3 * len(kinds)


def _plan_gather_pass(kinds, shapes, rels=ALL_RELATIONS):
    def plan(refs, send, recv):
        x, y, c, others = _place()
        sends, recvs = [], []
        for w, (kind, ss) in enumerate(zip(kinds, shapes)):
            for i, p in enumerate(rels):
                px, py = others[p]
                got = _chunk_of(refs[w], kind, 2 * px + py, c, ss)
                coming = _chunk_of(refs[w], kind, 2 * px + py, 1 - c, ss)
                sends.append(_remote(got, got, send, recv, len(rels) * w + i, (x, y, 1 - c)))
                recvs.append(_remote(coming, coming, send, recv, len(rels) * w + i, (x, y, 1 - c)))
        return sends, recvs

    return plan, len(rels) * len(kinds)


def _plan_pair(kinds, shapes):
    nw = len(kinds)

    def plan(refs, send, recv):
        x, y, c, _ = _place()
        sends = []
        for w, (kind, ss) in enumerate(zip(kinds, shapes)):
            for k in range(4):
                sends.append(_remote(_chunk_of(refs[w], kind, k, 1 - c, ss), refs[nw + w].at[k], send, recv,
                                     4 * w + k, (x, y, 1 - c)))
        return sends, sends

    return plan, 4 * nw


def _plan_chip(nw):
    def plan(refs, send, recv):
        x, y, c, others = _place()
        sends = []
        for w in range(nw):
            for p, (px, py) in enumerate(others):
                sends.append(_remote(refs[w].at[p], refs[nw + w].at[p], send, recv, 3 * w + p, (px, py, c)))
        return sends, sends

    return plan, 3 * nw


def _plan_share(slabs):
    def plan(refs, send, recv):
        x, y, c, _ = _place()
        sends, recvs, i = [], [], 0
        for w, ns in enumerate(slabs):
            for s in range(ns):
                sends.append(_remote(refs[w].at[s, c], refs[w].at[s, c], send, recv, i, (x, y, 1 - c)))
                recvs.append(_remote(refs[w].at[s, 1 - c], refs[w].at[s, 1 - c], send, recv, i, (x, y, 1 - c)))
                i += 1
        return sends, recvs

    return plan, sum(slabs)


def _grad_half_spec(kind, tr, tn, nr, nn, chunk):
    if kind == "col":
        return pl.BlockSpec((tr, tn), lambda *a: (a[-1][1] * nr + a[-3], chunk(*a) * nn + a[-2]))
    return pl.BlockSpec((tr, tn), lambda *a: ((2 * chunk(*a) + a[-1][1]) * nr + a[-3], a[-2]))


def _pair_add(grad, got, kind, shard_shape, pos, name):
    r, n = shard_shape
    hr = r // 2
    tr, tn = _tile(hr, 256, 16), _tile(n, 1408)
    nr, nn = hr // tr, n // tn
    g_spec = _grad_half_spec(kind, tr, tn, nr, nn, lambda p, i, j, pos_: pos_[2 + p])
    r_spec = pl.BlockSpec((1, tr, tn), lambda p, i, j, pos_: (pos_[2 + p], i, j))
    o_spec = pl.BlockSpec((1, tr, tn), lambda p, i, j, pos_: (p, i, j))

    def body(pos_ref, g_ref, r_ref, o_ref):
        o_ref[0] = (g_ref[...] + r_ref[0]).astype(BF16)

    return pl.pallas_call(
        body,
        grid_spec=pltpu.PrefetchScalarGridSpec(num_scalar_prefetch=1, grid=(3, nr, nn), in_specs=[g_spec, r_spec],
                                               out_specs=o_spec),
        out_shape=SDS((3, hr, n), BF16),
        compiler_params=_params(("parallel", "parallel", "parallel")), name=name)(pos, grad, got)


def _chip_add(grad, got, got16, kind, shard_shape, pos, name, slab=0, slabs=1, prev=None):
    r, n = shard_shape
    hr = r // 2
    tr, tn = _tile(hr, 256, 16), _tile(n, 1408)
    nr, nn = hr // tr, n // tn
    g_spec = _grad_half_spec(kind, tr, tn, nr, nn, lambda i, j, pos_: pos_[0])
    r_spec = pl.BlockSpec((1, tr, tn), lambda i, j, pos_: (pos_[0], i, j))
    oth = pl.BlockSpec((3, tr, tn), lambda i, j, pos_: (0, i, j))

    def body(pos_ref, g_ref, r_ref, oth_ref, *rest):
        own = g_ref[...] + r_ref[0]
        rest[-1][0, 0] = ((own + oth_ref[0].astype(F32)) + oth_ref[1].astype(F32)) + oth_ref[2].astype(F32)

    return pl.pallas_call(
        body,
        grid_spec=pltpu.PrefetchScalarGridSpec(
            num_scalar_prefetch=1, grid=(nr, nn), in_specs=[g_spec, r_spec, oth] + [ANY] * (prev is not None),
            out_specs=pl.BlockSpec((1, 1, tr, tn), lambda i, j, pos_: (slab, pos_[1], i, j))),
        out_shape=SDS((slabs, 2, hr, n), F32), input_output_aliases={4: 0} if prev is not None else {},
        compiler_params=_params(("parallel", "parallel")),
        name=name)(pos, grad, got, got16, *(() if prev is None else (prev,)))


def _adamw_math(w, g, m, v):
    m = ADAM_B1 * m + (1.0 - ADAM_B1) * g
    v = ADAM_B2 * v + (1.0 - ADAM_B2) * (g * g)
    m_hat = m / (1.0 - ADAM_B1 ** ADAM_STEP)
    v_hat = v / (1.0 - ADAM_B2 ** ADAM_STEP)
    return -ADAM_LR * (m_hat / (jnp.sqrt(v_hat) + ADAM_EPS) + ADAM_WD * w), m, v


def _adamw(w, g, m, v, name):
    r, n = w.shape
    tr, tn = _tile(r, 256, 16), _tile(n, 1408)

    def body(w_ref, g_ref, m_ref, v_ref, d_ref, nm_ref, nv_ref, go_ref):
        gv = g_ref[...]
        d_ref[...], nm_ref[...], nv_ref[...] = _adamw_math(w_ref[...], gv, m_ref[...], v_ref[...])
        go_ref[...] = gv

    tile = pl.BlockSpec((tr, tn), lambda i, j: (i, j))
    return pl.pallas_call(
        body, grid=(r // tr, n // tn), in_specs=[tile] * 4, out_specs=[tile] * 4, out_shape=[SDS((r, n), F32)] * 4,
        compiler_params=_params(("parallel", "parallel")), name=name)(w, g, m, v)


def _small_allreduce_adamw(g, w, m, v, deps=()):
    length = g.shape[1]

    def body(*refs):
        g_ref, w_ref, m_ref, v_ref = refs[:4]
        gs_ref, d_ref, nm_ref, nv_ref, buf, send, recv = refs[4 + len(deps):]
        x, y, c = lax.axis_index("x"), lax.axis_index("y"), lax.axis_index("c")
        me = 4 * x + 2 * y + c
        buf[me] = g_ref[...]
        cps = []
        for d in range(1, 8):
            peer = (x ^ (d >> 2), y ^ ((d >> 1) & 1), c ^ (d & 1))
            cp = pltpu.make_async_remote_copy(src_ref=buf.at[me], dst_ref=buf.at[me], send_sem=send.at[d - 1],
                                              recv_sem=recv.at[d - 1], device_id=peer, device_id_type=MESH)
            cp.start()
            cps.append(cp)
        for cp in cps:
            cp.wait()
        total = buf[0]
        for d in range(1, 8):
            total = total + buf[d]
        gs_ref[...] = total
        d_ref[...], nm_ref[...], nv_ref[...] = _adamw_math(w_ref[...], total, m_ref[...], v_ref[...])

    vm = pl.BlockSpec(memory_space=pltpu.VMEM)
    return pl.pallas_call(
        body, in_specs=[vm] * 4 + [ANY] * len(deps), out_specs=[vm] * 4, out_shape=[SDS((1, length), F32)] * 4,
        scratch_shapes=[pltpu.VMEM((8, 1, length), F32), pltpu.SemaphoreType.DMA((7,)), pltpu.SemaphoreType.DMA((7,))],
        compiler_params=pltpu.CompilerParams(has_side_effects=True), name="small_allreduce_adamw")(g, w, m, v, *deps)


def kernel(x, w_in, b_gate, norm_mix, norm_ffn, hgrn_lb_logits, hgrn_out_gain, q_gain, k_gain, rel_bias, w_proj_a, w_proj_b, w_out, w_ffn_in, w_ffn_out, loss_target, m_w_in, m_b_gate, m_norm_mix, m_norm_ffn, m_hgrn_lb_logits, m_hgrn_out_gain, m_q_gain, m_k_gain, m_rel_bias, m_w_proj_a, m_w_proj_b, m_w_out, m_w_ffn_in, m_w_ffn_out, v_w_in, v_b_gate, v_norm_mix, v_norm_ffn, v_hgrn_lb_logits, v_hgrn_out_gain, v_q_gain, v_k_gain, v_rel_bias, v_w_proj_a, v_w_proj_b, v_w_out, v_w_ffn_in, v_w_ffn_out):
    t, d = x.shape[1], x.shape[2]
    d_a = hgrn_out_gain.shape[1]
    h_a = d_a // HEAD
    h_b = rel_bias.shape[1]
    d_b = h_b * HEAD
    x0 = x.reshape(t, d)
    target = loss_target.reshape(t, d)
    ax, ay = lax.axis_index("x"), lax.axis_index("y")
    pos = jnp.stack([2 * ax + ay, lax.axis_index("c"), 2 * (1 - ax) + ay, 2 * ax + 1 - ay,
                     2 * (1 - ax) + 1 - ay]).astype(jnp.int32)

    names = ["w_in", "w_proj_a", "w_proj_b", "w_out", "w_ffn_in", "w_ffn_out"]
    big = dict(zip(names, [w_in[0], w_proj_a[0], w_proj_b[0], w_out[0], w_ffn_in[0], w_ffn_out[0]]))
    big_m = dict(zip(names, [m_w_in[0], m_w_proj_a[0], m_w_proj_b[0], m_w_out[0], m_w_ffn_in[0], m_w_ffn_out[0]]))
    big_v = dict(zip(names, [v_w_in[0], v_w_proj_a[0], v_w_proj_b[0], v_w_out[0], v_w_ffn_in[0], v_w_ffn_out[0]]))
    kind = dict(zip(names, ["col", "col", "col", "row", "col", "row"]))
    shape = {nm: big[nm].shape for nm in names}

    def gather_start(tag, group, deps):
        plan, n = _plan_gather_ici([kind[g] for g in group], [shape[g] for g in group])
        fulls = [_cast_into_full(big[g], kind[g], pos, "cast_" + g) for g in group]
        send, recv, bufs, token = _start_copies("gather_ici_start_" + tag, fulls, plan, n, deps)
        return (tag, group, plan, send, recv, bufs), token

    def gather_pass(state, after, rels=ALL_RELATIONS, part=""):
        tag, group, _, send, recv, bufs = state
        kinds_, shapes_ = [kind[g] for g in group], [shape[g] for g in group]
        bufs = _wait_copies("gather_ici_wait_" + tag + part, bufs, send, recv,
                            _plan_gather_ici(kinds_, shapes_, rels)[0], after)
        plan, n = _plan_gather_pass(kinds_, shapes_, rels)
        send2, recv2, bufs, token = _start_copies("gather_pass_start_" + tag + part, bufs, plan, n, ())
        return (tag + part, group, plan, send2, recv2, bufs), token

    def gather_done(state, after):
        tag, group, plan, send, recv, bufs = state
        return _wait_copies("gather_pass_wait_" + tag, bufs, send, recv, plan, after)

    def reduce_start(tag, group, grads, deps):
        plan, n = _plan_pair([kind[g] for g in group], [shape[g] for g in group])
        lands = [lax.empty((4, shape[g][0] // 2, shape[g][1]), F32) for g in group]
        send, recv, bufs, token = _start_copies("pair_start_" + tag, list(grads) + lands, plan, n, deps)
        return dict(tag=tag, group=group, plan=plan, send=send, recv=recv, bufs=bufs), token

    def reduce_pair_done(st, after):
        tag, group, nw = st["tag"], st["group"], len(st["group"])
        bufs = _wait_copies("pair_wait_" + tag, st["bufs"], st["send"], st["recv"], st["plan"], after)
        grads, gots = bufs[:nw], bufs[nw:]
        parts = [_pair_add(g, l, kind[nm], shape[nm], pos, "pair_add_" + nm) for g, l, nm in zip(grads, gots, group)]
        lands = [lax.empty((3, shape[g][0] // 2, shape[g][1]), BF16) for g in group]
        plan, n = _plan_chip(nw)
        send, recv, bufs, token = _start_copies("chip_start_" + tag, parts + lands, plan, n, ())
        return dict(st, plan=plan, send=send, recv=recv, bufs=bufs, grads=grads, gots=gots), token

    def reduce_chip_wait(st, after, slab=0, slabs=1, prev=None):
        tag, group, nw = st["tag"], st["group"], len(st["group"])
        bufs = _wait_copies("chip_wait_" + tag, st["bufs"], st["send"], st["recv"], st["plan"], after)
        return [_chip_add(g, l, got16, kind[nm], shape[nm], pos, "chip_add_" + nm, slab, slabs, prev)
                for g, l, got16, nm in zip(st["grads"], st["gots"], bufs[nw:], group)]

    def reduce_share(tag, group, finals, slabs=1):
        plan, n = _plan_share([slabs] * len(finals))
        send, recv, bufs, token = _start_copies("share_start_" + tag, finals, plan, n, ())
        return dict(tag=tag, group=group, plan=plan, send=send, recv=recv, bufs=bufs), token

    def reduce_chip_done(st, after):
        return reduce_share(st["tag"], st["group"], reduce_chip_wait(st, after))

    g_big, upd = {}, {}

    def reduce_finish(st, after):
        bufs = _wait_copies("share_wait_" + st["tag"], st["bufs"], st["send"], st["recv"], st["plan"], after)
        for full, nm in zip(bufs, st["group"]):
            upd[nm] = _adamw(big[nm], full.reshape(shape[nm]), big_m[nm], big_v[nm], "adamw_" + nm)
            g_big[nm] = upd[nm][3]

    ga, token = gather_start("a", ["w_in"], ())
    gb, token = gather_start("b", ["w_proj_a", "w_proj_b", "w_out"], (token,))
    gc, token = gather_start("c", ["w_ffn_in"], (token,))
    gd, token = gather_start("d", ["w_ffn_out"], (token,))
    h1, r1 = _rmsnorm_fwd(x0, norm_mix, "rmsnorm_mix")
    rb = jnp.pad(rel_bias[0], ((0, 0), (0, REL_LANES - N_REL)))
    bias = _relbias_expand(rb).transpose(1, 0, 2)
    proj = _matmul_chunks(h1, big["w_in"], (0,), None, pos, "proj_in_own", own_shard=True)
    ici_a = ga
    ga, token = gather_pass(ici_a, (h1, bias, proj, token), rels=(0, 1), part="_near")
    (wg_in,) = gather_done(ga, ())
    proj = _matmul_chunks(h1, wg_in, (2, 3), proj, pos, "proj_in_near")
    ga, token = gather_pass(ici_a[:5] + ([wg_in],), (proj,), rels=(2,), part="_far")
    (wg_in,) = gather_done(ga, ())
    proj = _matmul_chunks(h1, wg_in, (4,), proj, pos, "proj_in_far")
    y_a, o_pre, states = _hgrn_fwd(proj, hgrn_lb_logits, hgrn_out_gain, h_a)
    gb, token = gather_pass(gb, (y_a,))
    col_b = 4 * d_a // HEAD
    y_b = _attn_fwd(proj, q_gain, k_gain, bias, h_b, col_b)
    wg_pa, wg_pb, wg_out = gather_done(gb, (y_b,))
    gate_off = 4 * d_a + 3 * d_b
    pa, pb, merged = _proj_merge(y_a, y_b, wg_pa, wg_pb, proj, b_gate, gate_off, deps=(token,))
    x2 = _matmul(merged, wg_out, res=x0, name="out_proj")
    gc, token = gather_pass(gc, (x2,))
    h2, r2 = _rmsnorm_fwd(x2, norm_ffn, "rmsnorm_ffn")
    (wg_fin,) = gather_done(gc, (h2,))
    ff_gate, ff_up, act = _ffn_in_swiglu(h2, wg_fin, deps=(token,))
    gd, token = gather_pass(gd, (act,))
    (wg_fout,) = gather_done(gd, ())
    dy, dy16, loss_part = _ffn_out_loss(act, wg_fout, x2, target)

    g_fout = _matmul(act, dy16, ta=True, name="dw_ffn_out")
    r_fout, token = reduce_start("fout", ["w_ffn_out"], [g_fout], ())
    dgu = _d_act_swiglu(dy16, wg_fout, ff_gate, ff_up, deps=(token,))
    r_fout, token = reduce_pair_done(r_fout, (dgu,))
    g_fin = _matmul(h2, dgu, ta=True, name="dw_ffn_in", deps=(token,))
    r_fin, token = reduce_start("fin", ["w_ffn_in"], [g_fin], ())
    dh2 = _matmul(dgu, wg_fin, tb=True, name="d_h2", deps=(token,))
    r_fout, token_a = reduce_chip_done(r_fout, (dh2,))
    r_fin, token_b = reduce_pair_done(r_fin, (dh2,))
    dx2, dx2_16, g_norm_ffn = _rmsnorm_bwd(dh2, x2, r2, norm_ffn, dy, "rmsnorm_ffn_bwd", deps=(token_a, token_b))
    dp_ab, dproj, g_bgate = _d_merged_gates(dx2_16, wg_out, proj, b_gate, pa, pb, gate_off)
    g_out = _matmul(merged, dx2_16, ta=True, name="dw_out")
    g_pa = _matmul(y_a, dp_ab, ta=True, name="dw_proj_a", b_lead=0)
    g_pb = _matmul(y_b, dp_ab, ta=True, name="dw_proj_b", b_lead=1)
    r_mid, token = reduce_start("mid", ["w_proj_a", "w_proj_b", "w_out"], [g_pa, g_pb, g_out], ())
    dy_a = _matmul(dp_ab, wg_pa, tb=True, name="d_y_a", deps=(token,), a_lead=0)
    dy_b = _matmul(dp_ab, wg_pb, tb=True, name="d_y_b", a_lead=1)
    r_fin, token_a = reduce_chip_done(r_fin, (dy_b,))
    r_mid, token_b = reduce_pair_done(r_mid, (dy_b,))
    dproj, dbias, g_qg, g_kg = _attn_bwd(dproj, proj, q_gain, k_gain, bias, dy_b, h_b, col_b, deps=(token_a, token_b))
    r_mid, token = reduce_chip_done(r_mid, (dbias,))
    dproj, g_lb, g_gain = _hgrn_bwd(dproj, proj, o_pre, states, dy_a, hgrn_lb_logits, hgrn_out_gain, h_a, deps=(token,))
    g_in = _matmul(h1, dproj, ta=True, name="dw_in")
    r_in, token = reduce_start("in", ["w_in"], [g_in], ())
    g_rb = _relbias_reduce(dbias.transpose(1, 0, 2))[:, :N_REL]
    reduce_finish(r_mid, (token,))
    reduce_finish(r_fout, (token,))
    r_in, token = reduce_pair_done(r_in, (g_rb, upd["w_out"][0], upd["w_ffn_out"][0]))
    dh1 = _matmul(dproj, wg_in, tb=True, name="d_h1", deps=(token,))
    dx, _, g_norm_mix = _rmsnorm_bwd(dh1, x0, r1, norm_mix, dx2, "rmsnorm_mix_bwd")
    reduce_finish(r_fin, (dx,))
    r_in, token = reduce_chip_done(r_in, (upd["w_ffn_in"][0], upd["w_proj_a"][0], upd["w_proj_b"][0]))

    small_w = [b_gate, norm_mix, norm_ffn, hgrn_lb_logits, hgrn_out_gain, q_gain, k_gain, rel_bias]
    small_m = [m_b_gate, m_norm_mix, m_norm_ffn, m_hgrn_lb_logits, m_hgrn_out_gain, m_q_gain, m_k_gain, m_rel_bias]
    small_v = [v_b_gate, v_norm_mix, v_norm_ffn, v_hgrn_lb_logits, v_hgrn_out_gain, v_q_gain, v_k_gain, v_rel_bias]
    small_g = [g_bgate, g_norm_mix, g_norm_ffn, g_lb, g_gain, g_qg, g_kg, g_rb]
    sizes = [w.size for w in small_w]
    length = -(-(sum(sizes) + 1) // 128) * 128

    def pack(parts_):
        flat = jnp.concatenate([p.reshape(1, -1) for p in parts_], axis=1)
        return jnp.pad(flat, ((0, 0), (0, length - flat.shape[1])))

    one = jnp.ones((1, 1), F32)
    packed = _small_allreduce_adamw(pack(small_g + [loss_part]), pack(small_w + [one]), pack(small_m + [one]),
                                    pack(small_v + [one]), deps=(token,))

    def unpack(vec):
        out, at = [], 0
        for w, n in zip(small_w, sizes):
            out.append(vec[0, at:at + n].reshape(w.shape))
            at += n
        return out, vec[0, at]

    (sg, loss), (sd, _), (sm, _), (sv, _) = [unpack(p) for p in packed]
    reduce_finish(r_in, (packed[0],))

    def ordered(small, bigs):
        bigs = [bigs[nm][None] for nm in names]
        return [bigs[0]] + small + bigs[1:]

    return (loss, dx.reshape(x.shape), *ordered(sg, g_big), *ordered(sd, {nm: upd[nm][0] for nm in names}),
            *ordered(sm, {nm: upd[nm][1] for nm in names}), *ordered(sv, {nm: upd[nm][2] for nm in names}))
```

```python
import functools

import jax
import jax.numpy as jnp
from jax import lax
from jax.experimental import pallas as pl
from jax.experimental.pallas import tpu as pltpu

F32 = jnp.float32
BF16 = jnp.bfloat16
SDS = jax.ShapeDtypeStruct
MESH = pl.DeviceIdType.MESH
HIGHEST = lax.Precision.HIGHEST

CHUNK = 64
SUB = 16
HEAD = 128
N_PAST = 8
BAND = (N_PAST + 1) * CHUNK
PAD = N_PAST * CHUNK
REL_FUTURE = CHUNK - 1
REL_PAST = 2 * CHUNK - 1
N_REL = REL_FUTURE + REL_PAST + 1
REL_LANES = 256
EPS = 1e-6
MIX_HEADS = 2
MIX_UNROLL = 4
MIX_UNROLL_BWD = 4
ATT_UNROLL = 8
ATT_UNROLL_BWD = 4
EXP_CLAMP = 80.0

ADAM_LR = 0.001
ADAM_B1 = 0.9
ADAM_B2 = 0.999
ADAM_EPS = 1e-08
ADAM_WD = 0.01
ADAM_STEP = 10

VMEM_LIMIT = 56 * 1024 * 1024

HBM = pl.BlockSpec(memory_space=pltpu.HBM)
ANY = pl.BlockSpec(memory_space=pl.ANY)
SEM = pl.BlockSpec(memory_space=pltpu.SEMAPHORE)

NT = (((1,), (1,)), ((), ()))
TN = (((0,), (0,)), ((), ()))
NN = (((1,), (0,)), ((), ()))


def _params(sem=None, **kw):
    return pltpu.CompilerParams(dimension_semantics=sem, vmem_limit_bytes=VMEM_LIMIT, **kw)


def _tile(n, pref, unit=128):
    if n <= pref:
        return n
    t = pref - pref % unit
    while n % t:
        t -= unit
    return t


def _sigmoid(x):
    return 1.0 / (1.0 + jnp.exp(-x))


def _dsilu(x, s):
    return s * (1.0 + x * (1.0 - s))


def _split(a):
    hi = a.astype(BF16)
    return hi, (a - hi.astype(F32)).astype(BF16)


def _dot3(a, b, dims):
    dot = lambda u, v: lax.dot_general(u, v, dims, preferred_element_type=F32)
    return dot(a[0], b[1]) + dot(a[1], b[0]) + dot(a[0], b[0])


def _fdot(a, b):
    return lax.dot_general(a, b, NN, precision=HIGHEST, preferred_element_type=F32)


MM_TILE_K = 5632
MM_TILE_N = 512


def _matmul_chunks(h, w, which, prev, pos, name, own_shard=False, deps=()):
    t, d = h.shape
    nc_ = w.shape[1] if own_shard else w.shape[1] // 4
    tm, tn = _tile(t, 1024), _tile(nc_, 1408)
    nn = nc_ // tn

    def chunk(q, p):
        sel = p[which[0]]
        for i in range(1, len(which)):
            sel = jnp.where(q == i, p[which[i]], sel)
        return sel

    def body(p_ref, h_ref, w_ref, *rest):
        rest[-1][...] = jnp.dot(h_ref[...], w_ref[...].astype(BF16), preferred_element_type=F32)

    if own_shard:
        w_spec = pl.BlockSpec((d, tn), lambda q, i, j, p: (0, j))
    else:
        w_spec = pl.BlockSpec((d, tn), lambda q, i, j, p: (0, chunk(q, p) * nn + j))
    n_extra = len(deps) + (prev is not None)
    return pl.pallas_call(
        body,
        grid_spec=pltpu.PrefetchScalarGridSpec(
            num_scalar_prefetch=1, grid=(len(which), t // tm, nn),
            in_specs=[pl.BlockSpec((tm, d), lambda q, i, j, p: (i, 0)), w_spec] + [ANY] * n_extra,
            out_specs=pl.BlockSpec((tm, tn), lambda q, i, j, p: (i, chunk(q, p) * nn + j))),
        out_shape=SDS((t, 4 * nc_), F32), input_output_aliases={3 + len(deps): 0} if prev is not None else {},
        compiler_params=_params(("arbitrary", "arbitrary", "arbitrary")),
        name=name)(pos, h, w, *deps, *(() if prev is None else (prev,)))


def _matmul(a, b, *, ta=False, tb=False, res=None, out_dtype=F32, name, deps=(), a_lead=None, b_lead=None):
    a2, b2 = a.shape[-2:], b.shape[-2:]
    m, k = (a2[1], a2[0]) if ta else a2
    n = b2[0] if tb else b2[1]
    if k > MM_TILE_K:
        tk, tm, tn = _tile(k, MM_TILE_K // 2), _tile(m, 1024), _tile(n, 1024)
    else:
        tk = k
        tm, tn = _tile(m, 2048 if tk <= MM_TILE_K // 2 else 1024), _tile(n, MM_TILE_N)
    nk = k // tk
    dims = ((((0,) if ta else (1,)), ((1,) if tb else (0,))), ((), ()))

    def body(*refs):
        n_in = 2 + (res is not None)
        a_ref, b_ref = refs[:2]
        r_ref = refs[2] if res is not None else None
        o_ref = refs[n_in + len(deps)]
        part = lax.dot_general(a_ref[...].astype(BF16), b_ref[...].astype(BF16), dims, preferred_element_type=F32)

        def finish(out):
            if r_ref is not None:
                out = out + r_ref[...]
            o_ref[...] = out.astype(o_ref.dtype)

        if nk == 1:
            finish(part)
            return
        acc_ref = refs[-1]
        kk = pl.program_id(2)

        @pl.when(kk == 0)
        def _():
            acc_ref[...] = part

        @pl.when(jnp.logical_and(kk > 0, kk < nk - 1))
        def _():
            acc_ref[...] += part

        @pl.when(kk == nk - 1)
        def _():
            finish(acc_ref[...] + part)

    def spec(block, index, lead):
        if lead is None:
            return pl.BlockSpec(block, index)
        return pl.BlockSpec((None,) + block, lambda i, j, l: (lead,) + index(i, j, l))

    a_spec = spec((tk, tm), lambda i, j, l: (l, i), a_lead) if ta else spec((tm, tk), lambda i, j, l: (i, l), a_lead)
    b_spec = spec((tn, tk), lambda i, j, l: (j, l), b_lead) if tb else spec((tk, tn), lambda i, j, l: (l, j), b_lead)
    o_spec = pl.BlockSpec((tm, tn), lambda i, j, l: (i, j))
    in_specs = [a_spec, b_spec] + ([o_spec] if res is not None else []) + [ANY] * len(deps)
    args = (a, b) + ((res,) if res is not None else ()) + tuple(deps)
    return pl.pallas_call(
        body, grid=(m // tm, n // tn, nk), in_specs=in_specs, out_specs=o_spec,
        out_shape=SDS((m, n), out_dtype), scratch_shapes=[pltpu.VMEM((tm, tn), F32)] if nk > 1 else [],
        compiler_params=_params(("parallel", "parallel", "arbitrary")), name=name)(*args)


def _cast_into_full(w, kind, pos, name):
    r, n = w.shape
    tr = _tile(r, 512, 16)
    nr = r // tr
    if kind == "col":
        shape, o_spec = (r, 4 * n), pl.BlockSpec((tr, n), lambda i, p: (i, p[0]))
    else:
        shape, o_spec = (4 * r, n), pl.BlockSpec((tr, n), lambda i, p: (p[0] * nr + i, 0))

    def body(p_ref, w_ref, o_ref):
        o_ref[...] = w_ref[...].astype(BF16)

    return pl.pallas_call(
        body,
        grid_spec=pltpu.PrefetchScalarGridSpec(num_scalar_prefetch=1, grid=(nr,),
                                               in_specs=[pl.BlockSpec((tr, n), lambda i, p: (i, 0))], out_specs=o_spec),
        out_shape=SDS(shape, BF16), compiler_params=_params(("parallel",)), name=name)(pos, w)


def _rmsnorm_fwd(x, gain, name):
    t, d = x.shape
    tm = _tile(t, 256)

    def body(x_ref, g_ref, h_ref, r_ref):
        xv = x_ref[...]
        r = lax.rsqrt(jnp.mean(xv * xv, axis=-1, keepdims=True) + EPS)
        h_ref[...] = (xv * r * g_ref[...]).astype(BF16)
        r_ref[...] = r

    return pl.pallas_call(
        body, grid=(t // tm,),
        in_specs=[pl.BlockSpec((tm, d), lambda i: (i, 0)), pl.BlockSpec((1, d), lambda i: (0, 0))],
        out_specs=[pl.BlockSpec((tm, d), lambda i: (i, 0)), pl.BlockSpec((tm, 1), lambda i: (i, 0))],
        out_shape=[SDS((t, d), BF16), SDS((t, 1), F32)], compiler_params=_params(("parallel",)), name=name)(x, gain)


def _rmsnorm_bwd(dh, x, r, gain, dres, name, deps=()):
    t, d = x.shape
    tm = _tile(t, 256)

    def body(dh_ref, x_ref, r_ref, g_ref, dres_ref, *rest):
        dx_ref, dxb_ref, dg_ref = rest[len(deps):]

        @pl.when(pl.program_id(0) == 0)
        def _():
            dg_ref[...] = jnp.zeros_like(dg_ref)

        dhv, xv, rv = dh_ref[...], x_ref[...], r_ref[...]
        dg_ref[...] += jnp.sum(dhv * (xv * rv), axis=0, keepdims=True)
        u = dhv * g_ref[...]
        dx = dres_ref[...] + rv * u - xv * (rv * rv * rv) * jnp.mean(u * xv, axis=-1, keepdims=True)
        dx_ref[...] = dx
        dxb_ref[...] = dx.astype(BF16)

    row = pl.BlockSpec((tm, d), lambda i: (i, 0))
    vec = pl.BlockSpec((1, d), lambda i: (0, 0))
    return pl.pallas_call(
        body, grid=(t // tm,),
        in_specs=[row, row, pl.BlockSpec((tm, 1), lambda i: (i, 0)), vec, row] + [ANY] * len(deps),
        out_specs=[row, row, vec], out_shape=[SDS((t, d), F32), SDS((t, d), BF16), SDS((1, d), F32)],
        compiler_params=_params(("arbitrary",)), name=name)(dh, x, r, gain, dres, *deps)


def _proj_merge(y_a, y_b, w_a, w_b, proj, b_gate, off, deps=()):
    t, ka = y_a.shape
    kb = y_b.shape[1]
    d = w_a.shape[1]
    tm, tc = _tile(t, 1024), _tile(d, MM_TILE_N)
    nj = d // tc
    oa, ob = off // tc, off // tc + nj

    def body(ya_ref, yb_ref, wa_ref, wb_ref, la_ref, lb_ref, ba_ref, bb_ref, *rest):
        pa_ref, pb_ref, o_ref = rest[len(deps):]
        pa = jnp.dot(ya_ref[...], wa_ref[...], preferred_element_type=F32)
        pb = jnp.dot(yb_ref[...], wb_ref[...], preferred_element_type=F32)
        pa_ref[...] = pa
        pb_ref[...] = pb
        ga = _sigmoid(la_ref[...] + ba_ref[...])
        gb = _sigmoid(lb_ref[...] + bb_ref[...])
        o_ref[...] = (ga * pa + gb * pb).astype(BF16)

    tile = pl.BlockSpec((tm, tc), lambda i, j: (i, j))
    return pl.pallas_call(
        body, grid=(t // tm, nj),
        in_specs=[pl.BlockSpec((tm, ka), lambda i, j: (i, 0)), pl.BlockSpec((tm, kb), lambda i, j: (i, 0)),
                  pl.BlockSpec((ka, tc), lambda i, j: (0, j)), pl.BlockSpec((kb, tc), lambda i, j: (0, j)),
                  pl.BlockSpec((tm, tc), lambda i, j: (i, oa + j)), pl.BlockSpec((tm, tc), lambda i, j: (i, ob + j)),
                  pl.BlockSpec((1, tc), lambda i, j: (0, j)), pl.BlockSpec((1, tc), lambda i, j: (0, nj + j))]
        + [ANY] * len(deps),
        out_specs=[tile, tile, tile], out_shape=[SDS((t, d), F32), SDS((t, d), F32), SDS((t, d), BF16)],
        compiler_params=_params(("parallel", "parallel")),
        name="proj_merge")(y_a, y_b, w_a, w_b, proj, proj, b_gate, b_gate, *deps)


def _d_merged_gates(dx, w, proj, b_gate, pa, pb, off, deps=()):
    t, k = dx.shape
    d = w.shape[0]
    tm, tc = _tile(t, 1024), _tile(d, MM_TILE_N)
    nj, ni = d // tc, t // tm
    o0 = off // tc

    def body(dx_ref, w_ref, la_ref, lb_ref, ba_ref, bb_ref, pa_ref, pb_ref, *rest):
        dp_ref, dproj_ref, db_ref, stage, sems = rest[len(deps):]
        j, i = pl.program_id(0), pl.program_id(1)
        step = j * ni + i
        slot = step % 2

        def copies(s, ii, jj):
            rows = pl.ds(pl.multiple_of(ii * tm, tm), tm)
            return [pltpu.make_async_copy(
                stage.at[s, br], dproj_ref.at[rows, pl.ds(pl.multiple_of(off + br * d + jj * tc, 128), tc)],
                sems.at[s, br]) for br in range(2)]

        @pl.when(step >= 2)
        def _():
            for cp in copies(slot, 0, 0):
                cp.wait()

        dm = lax.dot_general(dx_ref[...], w_ref[...], NT, preferred_element_type=F32)

        @pl.when(i == 0)
        def _():
            db_ref[...] = jnp.zeros_like(db_ref)

        for br, (l_ref, b_ref, p_ref) in enumerate(((la_ref, ba_ref, pa_ref), (lb_ref, bb_ref, pb_ref))):
            g = _sigmoid(l_ref[...] + b_ref[...])
            dp_ref[br] = (dm * g).astype(BF16)
            dl = dm * p_ref[...] * g * (1.0 - g)
            stage[slot, br] = dl.astype(BF16)
            db_ref[br] += jnp.sum(dl, axis=0, keepdims=True)
        for cp in copies(slot, i, j):
            cp.start()

        @pl.when(step == ni * nj - 1)
        def _():
            for s in range(min(2, ni * nj)):
                for cp in copies(s, 0, 0):
                    cp.wait()

    tile = pl.BlockSpec((tm, tc), lambda j, i: (i, j))
    return pl.pallas_call(
        body, grid=(nj, ni),
        in_specs=[pl.BlockSpec((tm, k), lambda j, i: (i, 0)), pl.BlockSpec((tc, k), lambda j, i: (j, 0)),
                  pl.BlockSpec((tm, tc), lambda j, i: (i, o0 + j)), pl.BlockSpec((tm, tc), lambda j, i: (i, o0 + nj + j)),
                  pl.BlockSpec((1, tc), lambda j, i: (0, j)), pl.BlockSpec((1, tc), lambda j, i: (0, nj + j)),
                  tile, tile] + [ANY] * len(deps),
        out_specs=[pl.BlockSpec((2, tm, tc), lambda j, i: (0, i, j)), ANY, pl.BlockSpec((2, 1, tc), lambda j, i: (0, 0, j))],
        out_shape=[SDS((2, t, d), BF16), SDS(proj.shape, BF16), SDS((2, 1, d), F32)],
        scratch_shapes=[pltpu.VMEM((2, 2, tm, tc), BF16), pltpu.SemaphoreType.DMA((2, 2))],
        compiler_params=_params(("arbitrary", "arbitrary")),
        name="d_merged_gates")(dx, w, proj, proj, b_gate, b_gate, pa, pb, *deps)


def _ffn_in_swiglu(h, w, deps=()):
    t, d = h.shape
    f = w.shape[1] // 2
    tm, tn = _tile(t, 2048), _tile(f, MM_TILE_N)
    nj = f // tn

    def body(h_ref, wg_ref, wu_ref, *rest):
        g_ref, u_ref, a_ref = rest[len(deps):]
        hv = h_ref[...]
        g = jnp.dot(hv, wg_ref[...], preferred_element_type=F32)
        u = jnp.dot(hv, wu_ref[...], preferred_element_type=F32)
        g_ref[...] = g
        u_ref[...] = u
        a_ref[...] = (g * _sigmoid(g) * u).astype(BF16)

    tile = pl.BlockSpec((tm, tn), lambda i, j: (i, j))
    return pl.pallas_call(
        body, grid=(t // tm, nj),
        in_specs=[pl.BlockSpec((tm, d), lambda i, j: (i, 0)), pl.BlockSpec((d, tn), lambda i, j: (0, j)),
                  pl.BlockSpec((d, tn), lambda i, j: (0, nj + j))] + [ANY] * len(deps),
        out_specs=[tile, tile, tile], out_shape=[SDS((t, f), F32), SDS((t, f), F32), SDS((t, f), BF16)],
        compiler_params=_params(("parallel", "parallel")), name="ffn_in_swiglu")(h, w, w, *deps)


def _d_act_swiglu(dy, w, gate, up, deps=()):
    t, k = dy.shape
    f = w.shape[0]
    tm, tn = _tile(t, 1024), _tile(f, MM_TILE_N)
    ni, nj = t // tm, f // tn

    def body(dy_ref, w_ref, g_ref, u_ref, *rest):
        out_ref, stage, sems = rest[len(deps):]
        i, j = pl.program_id(0), pl.program_id(1)
        step = i * nj + j
        slot = step % 2

        def copies(s, ii, jj):
            rows = pl.ds(pl.multiple_of(ii * tm, tm), tm)
            return [pltpu.make_async_copy(
                stage.at[s, half], out_ref.at[rows, pl.ds(pl.multiple_of(half * f + jj * tn, 128), tn)],
                sems.at[s, half]) for half in range(2)]

        @pl.when(step >= 2)
        def _():
            for cp in copies(slot, 0, 0):
                cp.wait()

        dact = lax.dot_general(dy_ref[...], w_ref[...], NT, preferred_element_type=F32)
        g = g_ref[...]
        sg = _sigmoid(g)
        stage[slot, 0] = (dact * u_ref[...] * _dsilu(g, sg)).astype(BF16)
        stage[slot, 1] = (dact * (g * sg)).astype(BF16)
        for cp in copies(slot, i, j):
            cp.start()

        @pl.when(step == ni * nj - 1)
        def _():
            for s in range(min(2, ni * nj)):
                for cp in copies(s, 0, 0):
                    cp.wait()

    tile = pl.BlockSpec((tm, tn), lambda i, j: (i, j))
    return pl.pallas_call(
        body, grid=(ni, nj),
        in_specs=[pl.BlockSpec((tm, k), lambda i, j: (i, 0)), pl.BlockSpec((tn, k), lambda i, j: (j, 0)), tile, tile]
        + [ANY] * len(deps),
        out_specs=ANY, out_shape=SDS((t, 2 * f), BF16),
        scratch_shapes=[pltpu.VMEM((2, 2, tm, tn), BF16), pltpu.SemaphoreType.DMA((2, 2))],
        compiler_params=_params(("arbitrary", "arbitrary")), name="d_act_swiglu")(dy, w, gate, up, *deps)


def _ffn_out_loss(act, w, x_res, target):
    t, d = x_res.shape
    k = act.shape[1]
    tm, tn = _tile(t, 1024), _tile(d, MM_TILE_N)

    def body(a_ref, w_ref, r_ref, t_ref, dy_ref, dyb_ref, l_ref):
        @pl.when(jnp.logical_and(pl.program_id(0) == 0, pl.program_id(1) == 0))
        def _():
            l_ref[...] = jnp.zeros_like(l_ref)

        y = jnp.dot(a_ref[...], w_ref[...], preferred_element_type=F32) + r_ref[...]
        e = y - t_ref[...]
        dy = e * (1.0 / d)
        dy_ref[...] = dy
        dyb_ref[...] = dy.astype(BF16)
        l_ref[...] += (0.5 / d) * jnp.sum(jnp.sum(e * e, axis=-1, keepdims=True), axis=0, keepdims=True)

    tile = pl.BlockSpec((tm, tn), lambda i, j: (i, j))
    return pl.pallas_call(
        body, grid=(t // tm, d // tn),
        in_specs=[pl.BlockSpec((tm, k), lambda i, j: (i, 0)), pl.BlockSpec((k, tn), lambda i, j: (0, j)), tile, tile],
        out_specs=[tile, tile, pl.BlockSpec((1, 1), lambda i, j: (0, 0))],
        out_shape=[SDS((t, d), F32), SDS((t, d), BF16), SDS((1, 1), F32)],
        compiler_params=_params(("arbitrary", "arbitrary")), name="ffn_out_loss")(act, w, x_res, target)


def _rel_onehot(qi):
    p = lax.broadcasted_iota(jnp.int32, (REL_LANES, BAND), 1)
    r = lax.broadcasted_iota(jnp.int32, (REL_LANES, BAND), 0)
    idx = jnp.clip(qi + PAD - p, -REL_FUTURE, REL_PAST) + REL_FUTURE
    return (idx == r).astype(F32)


def _relbias_expand(rb):
    h = rb.shape[0]

    def body(rb_ref, o_ref):
        def step(qi, _):
            o_ref[qi] = _fdot(rb_ref[...], _rel_onehot(qi))
            return 0

        lax.fori_loop(0, CHUNK, step, 0)

    return pl.pallas_call(body, out_shape=SDS((CHUNK, h, BAND), F32), compiler_params=_params(),
                          name="relbias_expand")(rb)


def _relbias_reduce(dbias):
    h = dbias.shape[1]

    rows_per_pass = 4

    def body(db_ref, o_ref):
        def step(i, acc):
            parts = []
            for u in range(rows_per_pass):
                qi = i * rows_per_pass + u
                xv = db_ref[qi]
                hi = xv.astype(BF16)
                rest = xv - hi.astype(F32)
                mid = rest.astype(BF16)
                low = (rest - mid.astype(F32)).astype(BF16)
                parts.append(lax.dot_general(jnp.concatenate([hi, mid, low], axis=0), _rel_onehot(qi).astype(BF16), NT,
                                             preferred_element_type=F32))
            for part in parts:
                acc = acc + (part[0:h] + part[h:2 * h] + part[2 * h:3 * h])
            return acc

        o_ref[...] = lax.fori_loop(0, CHUNK // rows_per_pass, step, jnp.zeros((h, REL_LANES), F32))

    return pl.pallas_call(body, out_shape=SDS((h, REL_LANES), F32), compiler_params=_params(),
                          name="relbias_reduce")(dbias)


def _lower_bound(l_ref):
    l0, l1 = l_ref[0:1, :], l_ref[1:2, :]
    m = jnp.maximum(l0, l1)
    e0, e1 = jnp.exp(l0 - m), jnp.exp(l1 - m)
    return e0 / (e0 + e1)


def _tri(lower):
    r = lax.broadcasted_iota(jnp.int32, (CHUNK, CHUNK), 0)
    c = lax.broadcasted_iota(jnp.int32, (CHUNK, CHUNK), 1)
    return r >= c if lower else r <= c


def _hgrn_intra(qs, kk, b_s):
    rows = lax.broadcasted_iota(jnp.int32, (CHUNK, HEAD), 0)
    b = b_s[...]
    out = []
    for i in range(CHUNK // SUB):
        lo = i * SUB
        ref = jnp.zeros((1, HEAD), F32) if i == 0 else b_s[lo - 1:lo, :]
        eq = jnp.exp(b[lo:lo + SUB] - ref)
        qt = _split(qs[lo:lo + SUB] * eq)
        e = jnp.where(rows < lo + SUB, jnp.exp(jnp.minimum(ref - b, EXP_CLAMP)), 0.0)
        kt = _split(kk * e)
        out.append((eq, qt, e, kt))
    return out


def _hgrn_scores(blocks):
    tr = lax.broadcasted_iota(jnp.int32, (SUB, CHUNK), 0)
    tc = lax.broadcasted_iota(jnp.int32, (SUB, CHUNK), 1)
    return jnp.concatenate([jnp.where(tc <= tr + i * SUB, _dot3(qt, kt, NT), 0.0)
                            for i, (_, qt, _, kt) in enumerate(blocks)], axis=0)


def _hgrn_fwd(proj, lb_logits, gain, n_heads):
    t = proj.shape[0]
    nc = t // CHUNK
    da = n_heads * HEAD
    hp = MIX_HEADS
    wide = hp * HEAD

    def body(q_ref, f_ref, i_ref, g_ref, l_ref, gain_ref, y_ref, o_ref, st_ref, state, b_s):
        state[...] = jnp.zeros_like(state)
        lb_all = _lower_bound(l_ref)
        tril = _tri(True).astype(F32)

        def chunks(i, _):
            dot = functools.partial(lax.dot_general, preferred_element_type=F32)
            items = []
            for u in range(MIX_UNROLL):
                for hh in range(hp):
                    j = i * MIX_UNROLL + u
                    sl = pl.ds(pl.multiple_of(j * CHUNK, CHUNK), CHUNK)
                    cols = slice(hh * HEAD, (hh + 1) * HEAD)
                    lb = lb_all[:, cols]
                    fg = lb + (1.0 - lb) * _sigmoid(f_ref[sl, cols])
                    qv = q_ref[sl, cols]
                    gv = g_ref[sl, cols]
                    items.append(dict(hh=hh, j=j, sl=sl, cols=cols, lf=jnp.log(fg), kk=1.0 - fg, qs=qv * _sigmoid(qv),
                                      vb=i_ref[sl, cols].astype(BF16), gate=gv * _sigmoid(gv)))
            for it in items:
                it["b"] = _fdot(tril, it["lf"])
            for slot, it in enumerate(items):
                b = it["b"]
                b_s[slot] = b
                it["blocks"] = _hgrn_intra(it["qs"], it["kk"], b_s.at[slot])
                it["ebl"] = jnp.exp(b_s[slot, CHUNK - 1:CHUNK, :])
                it["qe"] = (it["qs"] * jnp.exp(b)).astype(BF16)
                it["kd"] = (it["kk"] * jnp.exp(b_s[slot, CHUNK - 1:CHUNK, :] - b)).astype(BF16)
            for it in items:
                it["a"] = _hgrn_scores(it["blocks"]).astype(BF16)
            for it in items:
                it["kv"] = dot(it["vb"], it["kd"], TN)
                it["o"] = dot(it["a"], it["vb"], NN)
            s_now = [state[hh] for hh in range(hp)]
            for it in items:
                it["s_in"] = s_now[it["hh"]]
                s_now[it["hh"]] = it["s_in"] * it["ebl"] + it["kv"]
            for hh in range(hp):
                state[hh] = s_now[hh]
            for it in items:
                it["o"] = it["o"] + dot(it["qe"], it["s_in"].astype(BF16), NT)
            for it in items:
                o, sl, cols = it["o"], it["sl"], it["cols"]
                st_ref[it["hh"], it["j"]] = it["s_in"]
                o_ref[sl, cols] = o
                rr = lax.rsqrt(jnp.mean(o * o, axis=-1, keepdims=True) + EPS)
                y_ref[sl, cols] = (o * rr * gain_ref[:, cols] * it["gate"]).astype(BF16)
            return 0

        assert nc % MIX_UNROLL == 0, (nc, MIX_UNROLL)
        lax.fori_loop(0, nc // MIX_UNROLL, chunks, 0)

    col = lambda k: pl.BlockSpec((t, wide), lambda h: (0, k * (n_heads // hp) + h))
    vec = pl.BlockSpec((1, wide), lambda h: (0, h))
    return pl.pallas_call(
        body, grid=(n_heads // hp,),
        in_specs=[col(0), col(1), col(2), col(3), pl.BlockSpec((2, wide), lambda h: (0, h)), vec],
        out_specs=[pl.BlockSpec((t, wide), lambda h: (0, h)), pl.BlockSpec((t, wide), lambda h: (0, h)),
                   pl.BlockSpec((hp, nc, HEAD, HEAD), lambda h: (h, 0, 0, 0))],
        out_shape=[SDS((t, da), BF16), SDS((t, da), F32), SDS((n_heads, nc, HEAD, HEAD), F32)],
        scratch_shapes=[pltpu.VMEM((hp, HEAD, HEAD), F32), pltpu.VMEM((hp * MIX_UNROLL, CHUNK, HEAD), F32)],
        compiler_params=_params(("parallel",)), name="hgrn_fwd")(proj, proj, proj, proj, lb_logits, gain)


def _write_column_groups(res, dproj_ref, sems, col0, stride, h, width):
    copies = [pltpu.make_async_copy(
        res.at[p], dproj_ref.at[:, pl.ds(pl.multiple_of((col0 + p * stride + h) * width, HEAD), width)], sems.at[p])
        for p in range(res.shape[0])]
    for cp in copies:
        cp.start()
    for cp in copies:
        cp.wait()


def _hgrn_bwd(dproj, proj, o_pre, states, dy, lb_logits, gain, n_heads, deps=()):
    t = proj.shape[0]
    nc = t // CHUNK
    da = n_heads * HEAD
    hp = MIX_HEADS
    wide = hp * HEAD

    def body(*refs):
        (q_ref, f_ref, i_ref, g_ref, o_ref, st_ref, dy_ref, l_ref, gain_ref,
         dproj_ref, dl_ref, dgain_ref, res, dstate, b_s, out_sems) = refs[1 + len(deps):]

        def compute():
            dstate[...] = jnp.zeros_like(dstate)
            lb_all = _lower_bound(l_ref)
            tril_m, tril, triu = _tri(True), _tri(True).astype(F32), _tri(False).astype(F32)
            last = lax.broadcasted_iota(jnp.int32, (CHUNK, HEAD), 0) == CHUNK - 1

            def chunks(i, carry):
                dot = functools.partial(lax.dot_general, preferred_element_type=F32)
                items = []
                for u in range(MIX_UNROLL_BWD):
                    for hh in range(hp):
                        j = nc - 1 - (i * MIX_UNROLL_BWD + u)
                        sl = pl.ds(pl.multiple_of(j * CHUNK, CHUNK), CHUNK)
                        cols = slice(hh * HEAD, (hh + 1) * HEAD)
                        lb, gain_v = lb_all[:, cols], gain_ref[:, cols]
                        sg = _sigmoid(f_ref[sl, cols])
                        fg = lb + (1.0 - lb) * sg
                        qv = q_ref[sl, cols]
                        sq = _sigmoid(qv)
                        gv = g_ref[sl, cols]
                        sgg = _sigmoid(gv)
                        silg = gv * sgg
                        o = o_ref[sl, cols]
                        dyv = dy_ref[sl, cols]
                        rr = lax.rsqrt(jnp.mean(o * o, axis=-1, keepdims=True) + EPS)
                        on = o * rr
                        don = dyv * gain_v * silg
                        do = (rr * don - o * (rr * rr * rr) * jnp.mean(don * o, axis=-1, keepdims=True)).astype(BF16)
                        items.append(dict(
                            hh=hh, j=j, sl=sl, cols=cols, lb=lb, sg=sg, fg=fg, kk=1.0 - fg, qv=qv, sq=sq, qs=qv * sq,
                            vb=i_ref[sl, cols].astype(BF16), do=do, dg=dyv * on * gain_v * _dsilu(gv, sgg),
                            dgain=jnp.sum(dyv * on * silg, axis=0, keepdims=True)))
                for it in items:
                    it["b"] = _fdot(tril, jnp.log(it["fg"]))
                for slot, it in enumerate(items):
                    b = it["b"]
                    b_s[slot] = b
                    it["blocks"] = _hgrn_intra(it["qs"], it["kk"], b_s.at[slot])
                    bl = b_s[slot, CHUNK - 1:CHUNK, :]
                    it["eb"], it["ebl"], it["ekd"] = jnp.exp(b), jnp.exp(bl), jnp.exp(bl - b)
                    it["s_in"] = st_ref[it["hh"], it["j"]]
                for it in items:
                    it["a"] = _hgrn_scores(it["blocks"]).astype(BF16)
                    it["da"] = jnp.where(tril_m, dot(it["do"], it["vb"], NT), 0.0)
                for it in items:
                    dq_rows = []
                    dk = jnp.zeros((CHUNK, HEAD), F32)
                    for blk, (eq, qt, e, kt) in enumerate(it["blocks"]):
                        da_i = _split(it["da"][blk * SUB:(blk + 1) * SUB])
                        dq_rows.append(eq * _dot3(da_i, kt, NN))
                        dk = dk + e * _dot3(da_i, qt, TN)
                    it["dq"] = jnp.concatenate(dq_rows, axis=0) + dot(it["do"], it["s_in"].astype(BF16), NN) * it["eb"]
                    it["dk"] = dk
                    it["dv"] = dot(it["a"], it["do"], TN)
                    it["g"] = dot(it["do"], (it["qs"] * it["eb"]).astype(BF16), TN)
                ds_now = [dstate[hh] for hh in range(hp)]
                for it in items:
                    it["ds_out"] = ds_now[it["hh"]]
                    ds_now[it["hh"]] = it["ds_out"] * it["ebl"] + it["g"]
                for hh in range(hp):
                    dstate[hh] = ds_now[hh]
                for it in items:
                    dsb = it["ds_out"].astype(BF16)
                    it["dv"] = it["dv"] + dot((it["kk"] * it["ekd"]).astype(BF16), dsb, NT)
                    it["dk_state"] = it["ekd"] * dot(it["vb"], dsb, NN)
                for it in items:
                    kk, dk_state = it["kk"], it["dk_state"]
                    it["dk"] = it["dk"] + dk_state
                    extra = (jnp.sum(kk * dk_state, axis=0, keepdims=True)
                             + it["ebl"] * jnp.sum(it["s_in"] * it["ds_out"], axis=0, keepdims=True))
                    it["db"] = it["qs"] * it["dq"] - kk * it["dk"] + jnp.where(last, extra, 0.0)
                for it in items:
                    it["dlf"] = _fdot(triu, it["db"])
                carry = list(carry)
                for it in items:
                    hh, sl, cols, sg, lb = it["hh"], it["sl"], it["cols"], it["sg"], it["lb"]
                    dfg = it["dlf"] / it["fg"] - it["dk"]
                    dlb_acc, dgain_acc = carry[hh]
                    carry[hh] = (dlb_acc + jnp.sum(dfg * (1.0 - sg), axis=0, keepdims=True), dgain_acc + it["dgain"])
                    res[0, sl, cols] = (it["dq"] * _dsilu(it["qv"], it["sq"])).astype(BF16)
                    res[1, sl, cols] = (dfg * (1.0 - lb) * sg * (1.0 - sg)).astype(BF16)
                    res[2, sl, cols] = it["dv"].astype(BF16)
                    res[3, sl, cols] = it["dg"].astype(BF16)
                return tuple(carry)

            assert nc % MIX_UNROLL_BWD == 0, (nc, MIX_UNROLL_BWD)
            zero = jnp.zeros((1, HEAD), F32)
            sums = lax.fori_loop(0, nc // MIX_UNROLL_BWD, chunks, ((zero, zero),) * hp)
            for hh, (dlb, dgain) in enumerate(sums):
                cols = slice(hh * HEAD, (hh + 1) * HEAD)
                lb = lb_all[:, cols]
                dgain_ref[:, cols] = dgain
                dl0 = dlb * lb * (1.0 - lb)
                dl_ref[0:1, cols] = dl0
                dl_ref[1:2, cols] = -dl0

        compute()
        _write_column_groups(res, dproj_ref, out_sems, 0, ng, pl.program_id(0), wide)

    ng = n_heads // hp
    col = lambda k: pl.BlockSpec((t, wide), lambda h: (0, k * ng + h))
    head = pl.BlockSpec((t, wide), lambda h: (0, h))
    vec = pl.BlockSpec((1, wide), lambda h: (0, h))
    return pl.pallas_call(
        body, grid=(ng,),
        in_specs=[ANY] * (1 + len(deps)) + [col(0), col(1), col(2), col(3), head,
                  pl.BlockSpec((hp, nc, HEAD, HEAD), lambda h: (h, 0, 0, 0)),
                  head, pl.BlockSpec((2, wide), lambda h: (0, h)), vec],
        out_specs=[ANY, pl.BlockSpec((2, wide), lambda h: (0, h)), vec],
        out_shape=[SDS(dproj.shape, BF16), SDS((2, da), F32), SDS((1, da), F32)],
        scratch_shapes=[pltpu.VMEM((4, t, wide), BF16), pltpu.VMEM((hp, HEAD, HEAD), F32),
                        pltpu.VMEM((hp * MIX_UNROLL_BWD, CHUNK, HEAD), F32), pltpu.SemaphoreType.DMA((4,))],
        input_output_aliases={0: 0}, compiler_params=_params(("arbitrary",)),
        name="hgrn_bwd")(dproj, *deps, proj, proj, proj, proj, o_pre, states, dy, lb_logits, gain)


ROWS = 256


def _head_norm(x_ref, gain, dst, dst_off, t):
    def step(i, _):
        sl = pl.ds(pl.multiple_of(i * ROWS, ROWS), ROWS)
        xv = x_ref[sl, :]
        r = lax.rsqrt(jnp.mean(xv * xv, axis=-1, keepdims=True) + EPS)
        dst[pl.ds(pl.multiple_of(dst_off + i * ROWS, ROWS), ROWS), :] = (xv * r * gain).astype(BF16)
        return 0

    lax.fori_loop(0, t // ROWS, step, 0)


def _head_norm_bwd(x_ref, gain, dn_ref, dn_off, out, slot, t):
    def step(i, acc):
        sl = pl.ds(pl.multiple_of(i * ROWS, ROWS), ROWS)
        xv = x_ref[sl, :]
        dn = dn_ref[pl.ds(pl.multiple_of(dn_off + i * ROWS, ROWS), ROWS), :]
        r = lax.rsqrt(jnp.mean(xv * xv, axis=-1, keepdims=True) + EPS)
        u = dn * gain
        out[slot, sl, :] = (r * u - xv * (r * r * r) * jnp.mean(u * xv, axis=-1, keepdims=True)).astype(out.dtype)
        return acc + jnp.sum(dn * (xv * r), axis=0, keepdims=True)

    return lax.fori_loop(0, t // ROWS, step, jnp.zeros((1, HEAD), F32))


def _attn_scores(qn, kpad, n):
    qc = qn[pl.ds(pl.multiple_of(n * CHUNK, CHUNK), CHUNK), :]
    band = pl.ds(pl.multiple_of(n * CHUNK, CHUNK), BAND)
    return qc, band, lax.dot_general(qc, kpad[band, :], NT, preferred_element_type=F32)


def _attn_softmax(raw, bias_ref, n):
    s = raw * (HEAD ** -0.5) + bias_ref[0]
    col = lax.broadcasted_iota(jnp.int32, (CHUNK, BAND), 1)
    s = jnp.where(col >= PAD - n * CHUNK, s, -jnp.inf)
    p = jnp.exp(s - jnp.max(s, axis=-1, keepdims=True))
    return p / jnp.sum(p, axis=-1, keepdims=True)


def _attn_fwd(proj, q_gain, k_gain, bias, n_heads, col0):
    t = proj.shape[0]
    nc = t // CHUNK

    def body(q_ref, k_ref, v_ref, qg_ref, kg_ref, bias_ref, y_ref, qn, kpad, vpad):
        kpad[0:PAD, :] = jnp.zeros((PAD, HEAD), BF16)
        vpad[0:PAD, :] = jnp.zeros((PAD, HEAD), BF16)
        _head_norm(q_ref, qg_ref[...], qn, 0, t)
        _head_norm(k_ref, kg_ref[...], kpad, PAD, t)

        def copy_v(i, _):
            vpad[pl.ds(pl.multiple_of(PAD + i * ROWS, ROWS), ROWS), :] = v_ref[
                pl.ds(pl.multiple_of(i * ROWS, ROWS), ROWS), :].astype(BF16)
            return 0

        lax.fori_loop(0, t // ROWS, copy_v, 0)

        def chunks(i, _):
            ns = [i * ATT_UNROLL + u for u in range(ATT_UNROLL)]
            scored = [_attn_scores(qn, kpad, n) for n in ns]
            probs = [_attn_softmax(raw, bias_ref, n).astype(BF16) for n, (_, _, raw) in zip(ns, scored)]
            outs = [lax.dot_general(p, vpad[band, :], NN, preferred_element_type=F32).astype(BF16)
                    for p, (_, band, _) in zip(probs, scored)]
            for n, o in zip(ns, outs):
                y_ref[pl.ds(pl.multiple_of(n * CHUNK, CHUNK), CHUNK), :] = o
            return 0

        assert nc % ATT_UNROLL == 0, (nc, ATT_UNROLL)
        lax.fori_loop(0, nc // ATT_UNROLL, chunks, 0)

    col = lambda k: pl.BlockSpec((t, HEAD), lambda h: (0, col0 + k * n_heads + h))
    vec = pl.BlockSpec((1, HEAD), lambda h: (0, 0))
    return pl.pallas_call(
        body, grid=(n_heads,),
        in_specs=[col(0), col(1), col(2), vec, vec, pl.BlockSpec((1, CHUNK, BAND), lambda h: (h, 0, 0))],
        out_specs=pl.BlockSpec((t, HEAD), lambda h: (0, h)), out_shape=SDS((t, n_heads * HEAD), BF16),
        scratch_shapes=[pltpu.VMEM((t, HEAD), BF16), pltpu.VMEM((t + PAD, HEAD), BF16), pltpu.VMEM((t + PAD, HEAD), BF16)],
        compiler_params=_params(("parallel",)), name="attn_fwd")(proj, proj, proj, q_gain, k_gain, bias)


def _attn_bwd(dproj, proj, q_gain, k_gain, bias, dy, n_heads, col0, deps=()):
    t = proj.shape[0]
    nc = t // CHUNK

    def body(*refs):
        (q_ref, k_ref, v_ref, qg_ref, kg_ref, bias_ref, dy_ref,
         dproj_ref, dbias_ref, dqg_ref, dkg_ref, qn, kpad, vpad, dqn, dk_acc, dv_acc, res,
         out_sems) = refs[1 + len(deps):]
        h = pl.program_id(0)

        def compute():
            kpad[0:PAD, :] = jnp.zeros((PAD, HEAD), BF16)
            vpad[0:PAD, :] = jnp.zeros((PAD, HEAD), BF16)
            _head_norm(q_ref, qg_ref[...], qn, 0, t)
            _head_norm(k_ref, kg_ref[...], kpad, PAD, t)

            def prep(i, _):
                sl = pl.ds(pl.multiple_of(PAD + i * ROWS, ROWS), ROWS)
                vpad[sl, :] = v_ref[pl.ds(pl.multiple_of(i * ROWS, ROWS), ROWS), :].astype(BF16)
                return 0

            lax.fori_loop(0, t // ROWS, prep, 0)

            def clear(i, _):
                sl = pl.ds(pl.multiple_of(i * ROWS, ROWS), ROWS)
                dk_acc[sl, :] = jnp.zeros((ROWS, HEAD), F32)
                dv_acc[sl, :] = jnp.zeros((ROWS, HEAD), F32)
                return 0

            lax.fori_loop(0, (t + PAD) // ROWS, clear, 0)
            dbias_ref[0] = jnp.zeros((CHUNK, BAND), F32)

            def chunks(i, _):
                dot = functools.partial(lax.dot_general, preferred_element_type=F32)
                ns = [i * ATT_UNROLL_BWD + u for u in range(ATT_UNROLL_BWD)]
                scored = [_attn_scores(qn, kpad, n) for n in ns]
                dos = [dy_ref[pl.ds(pl.multiple_of(n * CHUNK, CHUNK), CHUNK), :].astype(BF16) for n in ns]
                dps = [dot(do, vpad[band, :], NT) for do, (_, band, _) in zip(dos, scored)]
                ps, dss = [], []
                for n, (_, _, raw), dp in zip(ns, scored, dps):
                    p = _attn_softmax(raw, bias_ref, n)
                    ds = p * (dp - jnp.sum(dp * p, axis=-1, keepdims=True))
                    dbias_ref[0] += ds
                    ps.append(p.astype(BF16))
                    dss.append((ds * (HEAD ** -0.5)).astype(BF16))
                dqs = [dot(d, kpad[band, :], NN) for d, (_, band, _) in zip(dss, scored)]
                dks = [dot(d, qc, TN) for d, (qc, _, _) in zip(dss, scored)]
                dvs = [dot(p, do, TN) for p, do in zip(ps, dos)]
                for n, (_, band, _), dq, dk, dv in zip(ns, scored, dqs, dks, dvs):
                    dqn[pl.ds(pl.multiple_of(n * CHUNK, CHUNK), CHUNK), :] = dq
                    dk_acc[band, :] += dk
                    dv_acc[band, :] += dv
                return 0

            assert nc % ATT_UNROLL_BWD == 0, (nc, ATT_UNROLL_BWD)
            lax.fori_loop(0, nc // ATT_UNROLL_BWD, chunks, 0)
            dqg = _head_norm_bwd(q_ref, qg_ref[...], dqn, 0, res, 0, t)
            dkg = _head_norm_bwd(k_ref, kg_ref[...], dk_acc, PAD, res, 1, t)

            def put_v(i, _):
                sl = pl.ds(pl.multiple_of(i * ROWS, ROWS), ROWS)
                res[2, sl, :] = dv_acc[pl.ds(pl.multiple_of(PAD + i * ROWS, ROWS), ROWS), :].astype(BF16)
                return 0

            lax.fori_loop(0, t // ROWS, put_v, 0)

            @pl.when(h == 0)
            def _():
                dqg_ref[...] = jnp.zeros_like(dqg_ref)
                dkg_ref[...] = jnp.zeros_like(dkg_ref)

            dqg_ref[...] += dqg
            dkg_ref[...] += dkg

        compute()
        _write_column_groups(res, dproj_ref, out_sems, col0, n_heads, h, HEAD)

    col = lambda k: pl.BlockSpec((t, HEAD), lambda h: (0, col0 + k * n_heads + h))
    vec = pl.BlockSpec((1, HEAD), lambda h: (0, 0))
    btile = pl.BlockSpec((1, CHUNK, BAND), lambda h: (h, 0, 0))
    return pl.pallas_call(
        body, grid=(n_heads,),
        in_specs=[ANY] * (1 + len(deps)) + [col(0), col(1), col(2), vec, vec, btile,
                                            pl.BlockSpec((t, HEAD), lambda h: (0, h))],
        out_specs=[ANY, btile, vec, vec],
        out_shape=[SDS(dproj.shape, BF16), SDS((n_heads, CHUNK, BAND), F32), SDS((1, HEAD), F32), SDS((1, HEAD), F32)],
        scratch_shapes=[pltpu.VMEM((t, HEAD), BF16), pltpu.VMEM((t + PAD, HEAD), BF16), pltpu.VMEM((t + PAD, HEAD), BF16),
                        pltpu.VMEM((t, HEAD), F32), pltpu.VMEM((t + PAD, HEAD), F32), pltpu.VMEM((t + PAD, HEAD), F32),
                        pltpu.VMEM((3, t, HEAD), BF16), pltpu.SemaphoreType.DMA((3,))],
        input_output_aliases={0: 0}, compiler_params=_params(("arbitrary",)),
        name="attn_bwd")(dproj, *deps, proj, proj, proj, q_gain, k_gain, bias, dy)


def _place():
    x, y, c = lax.axis_index("x"), lax.axis_index("y"), lax.axis_index("c")
    others = [(1 - x, y), (x, 1 - y), (1 - x, 1 - y)]
    return x, y, c, others


def _chunk_of(ref, kind, chip, half, shard_shape):
    r, n = shard_shape
    hr = r // 2
    if kind == "col":
        rows = pl.ds(0, r) if half is None else pl.ds(half * hr, hr)
        return ref.at[rows, pl.ds(chip * n, n)]
    rows = pl.ds(chip * r, r) if half is None else pl.ds(chip * r + half * hr, hr)
    return ref.at[rows, :]


EFFECT = pltpu.SideEffectType.DATAFLOW_SIDE_EFFECTING


def _start_copies(name, bufs, plan, n, deps):
    nb, nd = len(bufs), len(deps)

    def body(*refs):
        send, recv, token = refs[nb + nd], refs[nb + nd + 1], refs[-1]
        for cp in plan(refs[:nb], send, recv)[0]:
            cp.start()
        token[...] = jnp.zeros_like(token)

    out = pl.pallas_call(
        body, name=name,
        out_shape=(pltpu.SemaphoreType.DMA((n,)), pltpu.SemaphoreType.DMA((n,)),
                   *[pltpu.HBM(b.shape, b.dtype) for b in bufs], SDS((8, 128), F32)),
        in_specs=[HBM] * nb + [ANY] * nd,
        out_specs=(SEM, SEM, *[HBM] * nb, pl.BlockSpec(memory_space=pltpu.VMEM)),
        input_output_aliases={i: 2 + i for i in range(nb)},
        compiler_params=pltpu.CompilerParams(has_side_effects=EFFECT),
    )(*[pltpu.with_memory_space_constraint(b, pltpu.HBM) for b in bufs], *deps)
    return out[0], out[1], list(out[2:2 + nb]), out[-1]


def _wait_copies(name, bufs, send, recv, plan, after):
    nb = len(bufs)

    def body(*refs):
        sends, recvs = plan(refs[:nb], refs[nb], refs[nb + 1])
        for cp in sends:
            cp.wait_send()
        for cp in recvs:
            cp.wait_recv()

    out = pl.pallas_call(
        body, name=name, out_shape=tuple(pltpu.HBM(b.shape, b.dtype) for b in bufs),
        in_specs=[HBM] * nb + [SEM, SEM] + [ANY] * len(after), out_specs=tuple([HBM] * nb),
        input_output_aliases={i: i for i in range(nb)},
        compiler_params=pltpu.CompilerParams(has_side_effects=EFFECT),
    )(*bufs, send, recv, *after)
    return list(out)


def _remote(src, dst, send, recv, i, dev):
    return pltpu.make_async_remote_copy(src_ref=src, dst_ref=dst, send_sem=send.at[i], recv_sem=recv.at[i],
                                        device_id=dev, device_id_type=MESH)


ALL_RELATIONS = (0, 1, 2)


def _plan_gather_ici(kinds, shapes, rels=ALL_RELATIONS):
    def plan(refs, send, recv):
        x, y, c, others = _place()
        sends, recvs = [], []
        for w, (kind, ss) in enumerate(zip(kinds, shapes)):
            for p in rels:
                px, py = others[p]
                mine = _chunk_of(refs[w], kind, 2 * x + y, c, ss)
                theirs = _chunk_of(refs[w], kind, 2 * px + py, c, ss)
                sends.append(_remote(mine, mine, send, recv, 3 * w + p, (px, py, c)))
                recvs.append(_remote(theirs, theirs, send, recv, 3 * w + p, (px, py, c)))
        return sends, recvs

    return plan, 3 * len(kinds)


def _plan_gather_pass(kinds, shapes, rels=ALL_RELATIONS):
    def plan(refs, send, recv):
        x, y, c, others = _place()
        sends, recvs = [], []
        for w, (kind, ss) in enumerate(zip(kinds, shapes)):
            for i, p in enumerate(rels):
                px, py = others[p]
                got = _chunk_of(refs[w], kind, 2 * px + py, c, ss)
                coming = _chunk_of(refs[w], kind, 2 * px + py, 1 - c, ss)
                sends.append(_remote(got, got, send, recv, len(rels) * w + i, (x, y, 1 - c)))
                recvs.append(_remote(coming, coming, send, recv, len(rels) * w + i, (x, y, 1 - c)))
        return sends, recvs

    return plan, len(rels) * len(kinds)


def _plan_pair(kinds, shapes):
    nw = len(kinds)

    def plan(refs, send, recv):
        x, y, c, _ = _place()
        sends = []
        for w, (kind, ss) in enumerate(zip(kinds, shapes)):
            for k in range(4):
                sends.append(_remote(_chunk_of(refs[w], kind, k, 1 - c, ss), refs[nw + w].at[k], send, recv,
                                     4 * w + k, (x, y, 1 - c)))
        return sends, sends

    return plan, 4 * nw


def _plan_chip(nw):
    def plan(refs, send, recv):
        x, y, c, others = _place()
        sends = []
        for w in range(nw):
            for p, (px, py) in enumerate(others):
                sends.append(_remote(refs[w].at[p], refs[nw + w].at[p], send, recv, 3 * w + p, (px, py, c)))
        return sends, sends

    return plan, 3 * nw


def _plan_share(nw):
    def plan(refs, send, recv):
        x, y, c, _ = _place()
        sends = [_remote(refs[w].at[c], refs[w].at[c], send, recv, w, (x, y, 1 - c)) for w in range(nw)]
        recvs = [_remote(refs[w].at[1 - c], refs[w].at[1 - c], send, recv, w, (x, y, 1 - c)) for w in range(nw)]
        return sends, recvs

    return plan, nw


def _grad_half_spec(kind, tr, tn, nr, nn, chunk):
    if kind == "col":
        return pl.BlockSpec((tr, tn), lambda *a: (a[-1][1] * nr + a[-3], chunk(*a) * nn + a[-2]))
    return pl.BlockSpec((tr, tn), lambda *a: ((2 * chunk(*a) + a[-1][1]) * nr + a[-3], a[-2]))


def _pair_add(grad, got, kind, shard_shape, pos, name):
    r, n = shard_shape
    hr = r // 2
    tr, tn = _tile(hr, 256, 16), _tile(n, 1408)
    nr, nn = hr // tr, n // tn
    g_spec = _grad_half_spec(kind, tr, tn, nr, nn, lambda p, i, j, pos_: pos_[2 + p])
    r_spec = pl.BlockSpec((1, tr, tn), lambda p, i, j, pos_: (pos_[2 + p], i, j))
    o_spec = pl.BlockSpec((1, tr, tn), lambda p, i, j, pos_: (p, i, j))

    def body(pos_ref, g_ref, r_ref, o_ref):
        o_ref[0] = (g_ref[...] + r_ref[0]).astype(BF16)

    return pl.pallas_call(
        body,
        grid_spec=pltpu.PrefetchScalarGridSpec(num_scalar_prefetch=1, grid=(3, nr, nn), in_specs=[g_spec, r_spec],
                                               out_specs=o_spec),
        out_shape=SDS((3, hr, n), BF16),
        compiler_params=_params(("parallel", "parallel", "parallel")), name=name)(pos, grad, got)


def _chip_add(grad, got, got16, kind, shard_shape, pos, name):
    r, n = shard_shape
    hr = r // 2
    tr, tn = _tile(hr, 256, 16), _tile(n, 1408)
    nr, nn = hr // tr, n // tn
    g_spec = _grad_half_spec(kind, tr, tn, nr, nn, lambda i, j, pos_: pos_[0])
    r_spec = pl.BlockSpec((1, tr, tn), lambda i, j, pos_: (pos_[0], i, j))
    oth = pl.BlockSpec((3, tr, tn), lambda i, j, pos_: (0, i, j))

    def body(pos_ref, g_ref, r_ref, oth_ref, o_ref):
        own = g_ref[...] + r_ref[0]
        o_ref[0] = ((own + oth_ref[0].astype(F32)) + oth_ref[1].astype(F32)) + oth_ref[2].astype(F32)

    return pl.pallas_call(
        body,
        grid_spec=pltpu.PrefetchScalarGridSpec(num_scalar_prefetch=1, grid=(nr, nn), in_specs=[g_spec, r_spec, oth],
                                               out_specs=pl.BlockSpec((1, tr, tn), lambda i, j, pos_: (pos_[1], i, j))),
        out_shape=SDS((2, hr, n), F32), compiler_params=_params(("parallel", "parallel")),
        name=name)(pos, grad, got, got16)


def _adamw_math(w, g, m, v):
    m = ADAM_B1 * m + (1.0 - ADAM_B1) * g
    v = ADAM_B2 * v + (1.0 - ADAM_B2) * (g * g)
    m_hat = m / (1.0 - ADAM_B1 ** ADAM_STEP)
    v_hat = v / (1.0 - ADAM_B2 ** ADAM_STEP)
    return -ADAM_LR * (m_hat / (jnp.sqrt(v_hat) + ADAM_EPS) + ADAM_WD * w), m, v


def _adamw(w, g, m, v, name):
    r, n = w.shape
    tr, tn = _tile(r, 256, 16), _tile(n, 1408)

    def body(w_ref, g_ref, m_ref, v_ref, d_ref, nm_ref, nv_ref, go_ref):
        gv = g_ref[...]
        d_ref[...], nm_ref[...], nv_ref[...] = _adamw_math(w_ref[...], gv, m_ref[...], v_ref[...])
        go_ref[...] = gv

    tile = pl.BlockSpec((tr, tn), lambda i, j: (i, j))
    return pl.pallas_call(
        body, grid=(r // tr, n // tn), in_specs=[tile] * 4, out_specs=[tile] * 4, out_shape=[SDS((r, n), F32)] * 4,
        compiler_params=_params(("parallel", "parallel")), name=name)(w, g, m, v)


def _small_allreduce_adamw(g, w, m, v, deps=()):
    length = g.shape[1]

    def body(*refs):
        g_ref, w_ref, m_ref, v_ref = refs[:4]
        gs_ref, d_ref, nm_ref, nv_ref, buf, send, recv = refs[4 + len(deps):]
        x, y, c = lax.axis_index("x"), lax.axis_index("y"), lax.axis_index("c")
        me = 4 * x + 2 * y + c
        buf[me] = g_ref[...]
        cps = []
        for d in range(1, 8):
            peer = (x ^ (d >> 2), y ^ ((d >> 1) & 1), c ^ (d & 1))
            cp = pltpu.make_async_remote_copy(src_ref=buf.at[me], dst_ref=buf.at[me], send_sem=send.at[d - 1],
                                              recv_sem=recv.at[d - 1], device_id=peer, device_id_type=MESH)
            cp.start()
            cps.append(cp)
        for cp in cps:
            cp.wait()
        total = buf[0]
        for d in range(1, 8):
            total = total + buf[d]
        gs_ref[...] = total
        d_ref[...], nm_ref[...], nv_ref[...] = _adamw_math(w_ref[...], total, m_ref[...], v_ref[...])

    vm = pl.BlockSpec(memory_space=pltpu.VMEM)
    return pl.pallas_call(
        body, in_specs=[vm] * 4 + [ANY] * len(deps), out_specs=[vm] * 4, out_shape=[SDS((1, length), F32)] * 4,
        scratch_shapes=[pltpu.VMEM((8, 1, length), F32), pltpu.SemaphoreType.DMA((7,)), pltpu.SemaphoreType.DMA((7,))],
        compiler_params=pltpu.CompilerParams(has_side_effects=True), name="small_allreduce_adamw")(g, w, m, v, *deps)


def kernel(x, w_in, b_gate, norm_mix, norm_ffn, hgrn_lb_logits, hgrn_out_gain, q_gain, k_gain, rel_bias, w_proj_a, w_proj_b, w_out, w_ffn_in, w_ffn_out, loss_target, m_w_in, m_b_gate, m_norm_mix, m_norm_ffn, m_hgrn_lb_logits, m_hgrn_out_gain, m_q_gain, m_k_gain, m_rel_bias, m_w_proj_a, m_w_proj_b, m_w_out, m_w_ffn_in, m_w_ffn_out, v_w_in, v_b_gate, v_norm_mix, v_norm_ffn, v_hgrn_lb_logits, v_hgrn_out_gain, v_q_gain, v_k_gain, v_rel_bias, v_w_proj_a, v_w_proj_b, v_w_out, v_w_ffn_in, v_w_ffn_out):
    t, d = x.shape[1], x.shape[2]
    d_a = hgrn_out_gain.shape[1]
    h_a = d_a // HEAD
    h_b = rel_bias.shape[1]
    d_b = h_b * HEAD
    x0 = x.reshape(t, d)
    target = loss_target.reshape(t, d)
    ax, ay = lax.axis_index("x"), lax.axis_index("y")
    pos = jnp.stack([2 * ax + ay, lax.axis_index("c"), 2 * (1 - ax) + ay, 2 * ax + 1 - ay,
                     2 * (1 - ax) + 1 - ay]).astype(jnp.int32)

    names = ["w_in", "w_proj_a", "w_proj_b", "w_out", "w_ffn_in", "w_ffn_out"]
    big = dict(zip(names, [w_in[0], w_proj_a[0], w_proj_b[0], w_out[0], w_ffn_in[0], w_ffn_out[0]]))
    big_m = dict(zip(names, [m_w_in[0], m_w_proj_a[0], m_w_proj_b[0], m_w_out[0], m_w_ffn_in[0], m_w_ffn_out[0]]))
    big_v = dict(zip(names, [v_w_in[0], v_w_proj_a[0], v_w_proj_b[0], v_w_out[0], v_w_ffn_in[0], v_w_ffn_out[0]]))
    kind = dict(zip(names, ["col", "col", "col", "row", "col", "row"]))
    shape = {nm: big[nm].shape for nm in names}

    def gather_start(tag, group, deps):
        plan, n = _plan_gather_ici([kind[g] for g in group], [shape[g] for g in group])
        fulls = [_cast_into_full(big[g], kind[g], pos, "cast_" + g) for g in group]
        send, recv, bufs, token = _start_copies("gather_ici_start_" + tag, fulls, plan, n, deps)
        return (tag, group, plan, send, recv, bufs), token

    def gather_pass(state, after, rels=ALL_RELATIONS, part=""):
        tag, group, _, send, recv, bufs = state
        kinds_, shapes_ = [kind[g] for g in group], [shape[g] for g in group]
        bufs = _wait_copies("gather_ici_wait_" + tag + part, bufs, send, recv,
                            _plan_gather_ici(kinds_, shapes_, rels)[0], after)
        plan, n = _plan_gather_pass(kinds_, shapes_, rels)
        send2, recv2, bufs, token = _start_copies("gather_pass_start_" + tag + part, bufs, plan, n, ())
        return (tag + part, group, plan, send2, recv2, bufs), token

    def gather_done(state, after):
        tag, group, plan, send, recv, bufs = state
        return _wait_copies("gather_pass_wait_" + tag, bufs, send, recv, plan, after)

    def reduce_start(tag, group, grads, deps):
        plan, n = _plan_pair([kind[g] for g in group], [shape[g] for g in group])
        lands = [lax.empty((4, shape[g][0] // 2, shape[g][1]), F32) for g in group]
        send, recv, bufs, token = _start_copies("pair_start_" + tag, list(grads) + lands, plan, n, deps)
        return dict(tag=tag, group=group, plan=plan, send=send, recv=recv, bufs=bufs), token

    def reduce_pair_done(st, after):
        tag, group, nw = st["tag"], st["group"], len(st["group"])
        bufs = _wait_copies("pair_wait_" + tag, st["bufs"], st["send"], st["recv"], st["plan"], after)
        grads, gots = bufs[:nw], bufs[nw:]
        parts = [_pair_add(g, l, kind[nm], shape[nm], pos, "pair_add_" + nm) for g, l, nm in zip(grads, gots, group)]
        lands = [lax.empty((3, shape[g][0] // 2, shape[g][1]), BF16) for g in group]
        plan, n = _plan_chip(nw)
        send, recv, bufs, token = _start_copies("chip_start_" + tag, parts + lands, plan, n, ())
        return dict(st, plan=plan, send=send, recv=recv, bufs=bufs, grads=grads, gots=gots), token

    def reduce_chip_done(st, after):
        tag, group, nw = st["tag"], st["group"], len(st["group"])
        bufs = _wait_copies("chip_wait_" + tag, st["bufs"], st["send"], st["recv"], st["plan"], after)
        finals = [_chip_add(g, l, got16, kind[nm], shape[nm], pos, "chip_add_" + nm)
                  for g, l, got16, nm in zip(st["grads"], st["gots"], bufs[nw:], group)]
        plan, n = _plan_share(nw)
        send, recv, bufs, token = _start_copies("share_start_" + tag, finals, plan, n, ())
        return dict(st, plan=plan, send=send, recv=recv, bufs=bufs), token

    g_big, upd = {}, {}

    def reduce_finish(st, after):
        bufs = _wait_copies("share_wait_" + st["tag"], st["bufs"], st["send"], st["recv"], st["plan"], after)
        for full, nm in zip(bufs, st["group"]):
            upd[nm] = _adamw(big[nm], full.reshape(shape[nm]), big_m[nm], big_v[nm], "adamw_" + nm)
            g_big[nm] = upd[nm][3]

    ga, token = gather_start("a", ["w_in"], ())
    gb, token = gather_start("b", ["w_proj_a", "w_proj_b", "w_out"], (token,))
    gc, token = gather_start("c", ["w_ffn_in"], (token,))
    gd, token = gather_start("d", ["w_ffn_out"], (token,))
    h1, r1 = _rmsnorm_fwd(x0, norm_mix, "rmsnorm_mix")
    rb = jnp.pad(rel_bias[0], ((0, 0), (0, REL_LANES - N_REL)))
    bias = _relbias_expand(rb).transpose(1, 0, 2)
    proj = _matmul_chunks(h1, big["w_in"], (0,), None, pos, "proj_in_own", own_shard=True)
    ici_a = ga
    ga, token = gather_pass(ici_a, (h1, bias, proj, token), rels=(0, 1), part="_near")
    (wg_in,) = gather_done(ga, ())
    proj = _matmul_chunks(h1, wg_in, (2, 3), proj, pos, "proj_in_near")
    ga, token = gather_pass(ici_a[:5] + ([wg_in],), (proj,), rels=(2,), part="_far")
    (wg_in,) = gather_done(ga, ())
    proj = _matmul_chunks(h1, wg_in, (4,), proj, pos, "proj_in_far")
    y_a, o_pre, states = _hgrn_fwd(proj, hgrn_lb_logits, hgrn_out_gain, h_a)
    gb, token = gather_pass(gb, (y_a,))
    col_b = 4 * d_a // HEAD
    y_b = _attn_fwd(proj, q_gain, k_gain, bias, h_b, col_b)
    wg_pa, wg_pb, wg_out = gather_done(gb, (y_b,))
    gate_off = 4 * d_a + 3 * d_b
    pa, pb, merged = _proj_merge(y_a, y_b, wg_pa, wg_pb, proj, b_gate, gate_off, deps=(token,))
    x2 = _matmul(merged, wg_out, res=x0, name="out_proj")
    gc, token = gather_pass(gc, (x2,))
    h2, r2 = _rmsnorm_fwd(x2, norm_ffn, "rmsnorm_ffn")
    (wg_fin,) = gather_done(gc, (h2,))
    ff_gate, ff_up, act = _ffn_in_swiglu(h2, wg_fin, deps=(token,))
    gd, token = gather_pass(gd, (act,))
    (wg_fout,) = gather_done(gd, ())
    dy, dy16, loss_part = _ffn_out_loss(act, wg_fout, x2, target)

    g_fout = _matmul(act, dy16, ta=True, name="dw_ffn_out")
    r_fout, token = reduce_start("fout", ["w_ffn_out"], [g_fout], ())
    dgu = _d_act_swiglu(dy16, wg_fout, ff_gate, ff_up, deps=(token,))
    r_fout, token = reduce_pair_done(r_fout, (dgu,))
    g_fin = _matmul(h2, dgu, ta=True, name="dw_ffn_in", deps=(token,))
    r_fin, token = reduce_start("fin", ["w_ffn_in"], [g_fin], ())
    dh2 = _matmul(dgu, wg_fin, tb=True, name="d_h2", deps=(token,))
    r_fout, token_a = reduce_chip_done(r_fout, (dh2,))
    r_fin, token_b = reduce_pair_done(r_fin, (dh2,))
    dx2, dx2_16, g_norm_ffn = _rmsnorm_bwd(dh2, x2, r2, norm_ffn, dy, "rmsnorm_ffn_bwd", deps=(token_a, token_b))
    dp_ab, dproj, g_bgate = _d_merged_gates(dx2_16, wg_out, proj, b_gate, pa, pb, gate_off)
    g_out = _matmul(merged, dx2_16, ta=True, name="dw_out")
    g_pa = _matmul(y_a, dp_ab, ta=True, name="dw_proj_a", b_lead=0)
    g_pb = _matmul(y_b, dp_ab, ta=True, name="dw_proj_b", b_lead=1)
    r_mid, token = reduce_start("mid", ["w_proj_a", "w_proj_b", "w_out"], [g_pa, g_pb, g_out], ())
    dy_a = _matmul(dp_ab, wg_pa, tb=True, name="d_y_a", deps=(token,), a_lead=0)
    dy_b = _matmul(dp_ab, wg_pb, tb=True, name="d_y_b", a_lead=1)
    r_mid, token_b = reduce_pair_done(r_mid, (dy_b,))
    dproj, dbias, g_qg, g_kg = _attn_bwd(dproj, proj, q_gain, k_gain, bias, dy_b, h_b, col_b, deps=(token_b,))
    r_fin, token_a = reduce_chip_done(r_fin, (dbias,))
    r_mid, token = reduce_chip_done(r_mid, (dbias,))
    dproj, g_lb, g_gain = _hgrn_bwd(dproj, proj, o_pre, states, dy_a, hgrn_lb_logits, hgrn_out_gain, h_a,
                                    deps=(token, token_a))
    g_in = _matmul(h1, dproj, ta=True, name="dw_in")
    r_in, token = reduce_start("in", ["w_in"], [g_in], ())
    g_rb = _relbias_reduce(dbias.transpose(1, 0, 2))[:, :N_REL]
    reduce_finish(r_mid, (token,))
    reduce_finish(r_fout, (token,))
    r_in, token = reduce_pair_done(r_in, (g_rb, upd["w_out"][0], upd["w_ffn_out"][0]))
    dh1 = _matmul(dproj, wg_in, tb=True, name="d_h1", deps=(token,))
    dx, _, g_norm_mix = _rmsnorm_bwd(dh1, x0, r1, norm_mix, dx2, "rmsnorm_mix_bwd")
    reduce_finish(r_fin, (dx,))
    r_in, token = reduce_chip_done(r_in, (upd["w_ffn_in"][0], upd["w_proj_a"][0], upd["w_proj_b"][0]))

    small_w = [b_gate, norm_mix, norm_ffn, hgrn_lb_logits, hgrn_out_gain, q_gain, k_gain, rel_bias]
    small_m = [m_b_gate, m_norm_mix, m_norm_ffn, m_hgrn_lb_logits, m_hgrn_out_gain, m_q_gain, m_k_gain, m_rel_bias]
    small_v = [v_b_gate, v_norm_mix, v_norm_ffn, v_hgrn_lb_logits, v_hgrn_out_gain, v_q_gain, v_k_gain, v_rel_bias]
    small_g = [g_bgate, g_norm_mix, g_norm_ffn, g_lb, g_gain, g_qg, g_kg, g_rb]
    sizes = [w.size for w in small_w]
    length = -(-(sum(sizes) + 1) // 128) * 128

    def pack(parts_):
        flat = jnp.concatenate([p.reshape(1, -1) for p in parts_], axis=1)
        return jnp.pad(flat, ((0, 0), (0, length - flat.shape[1])))

    one = jnp.ones((1, 1), F32)
    packed = _small_allreduce_adamw(pack(small_g + [loss_part]), pack(small_w + [one]), pack(small_m + [one]),
                                    pack(small_v + [one]), deps=(token,))

    def unpack(vec):
        out, at = [], 0
        for w, n in zip(small_w, sizes):
            out.append(vec[0, at:at + n].reshape(w.shape))
            at += n
        return out, vec[0, at]

    (sg, loss), (sd, _), (sm, _), (sv, _) = [unpack(p) for p in packed]
    reduce_finish(r_in, (packed[0],))

    def ordered(small, bigs):
        bigs = [bigs[nm][None] for nm in names]
        return [bigs[0]] + small + bigs[1:]

    return (loss, dx.reshape(x.shape), *ordered(sg, g_big), *ordered(sd, {nm: upd[nm][0] for nm in names}),
            *ordered(sm, {nm: upd[nm][1] for nm in names}), *ordered(sv, {nm: upd[nm][2] for nm in names}))
```

```python
import functools

import jax
import jax.numpy as jnp
from jax import lax
from jax.experimental import pallas as pl
from jax.experimental.pallas import tpu as pltpu

F32 = jnp.float32
BF16 = jnp.bfloat16
SDS = jax.ShapeDtypeStruct
MESH = pl.DeviceIdType.MESH
HIGHEST = lax.Precision.HIGHEST

CHUNK = 64
SUB = 16
HEAD = 128
N_PAST = 8
BAND = (N_PAST + 1) * CHUNK
PAD = N_PAST * CHUNK
REL_FUTURE = CHUNK - 1
REL_PAST = 2 * CHUNK - 1
N_REL = REL_FUTURE + REL_PAST + 1
REL_LANES = 256
EPS = 1e-6
MIX_HEADS = 2
MIX_UNROLL = 4
MIX_UNROLL_BWD = 4
ATT_UNROLL = 8
ATT_UNROLL_BWD = 4
EXP_CLAMP = 80.0

ADAM_LR = 0.001
ADAM_B1 = 0.9
ADAM_B2 = 0.999
ADAM_EPS = 1e-08
ADAM_WD = 0.01
ADAM_STEP = 10

VMEM_LIMIT = 56 * 1024 * 1024

HBM = pl.BlockSpec(memory_space=pltpu.HBM)
ANY = pl.BlockSpec(memory_space=pl.ANY)
SEM = pl.BlockSpec(memory_space=pltpu.SEMAPHORE)

NT = (((1,), (1,)), ((), ()))
TN = (((0,), (0,)), ((), ()))
NN = (((1,), (0,)), ((), ()))


def _params(sem=None, **kw):
    return pltpu.CompilerParams(dimension_semantics=sem, vmem_limit_bytes=VMEM_LIMIT, **kw)


def _tile(n, pref, unit=128):
    if n <= pref:
        return n
    t = pref - pref % unit
    while n % t:
        t -= unit
    return t


STREAM_BLOCK = 256 * 1408


def _row_tile(rows, cols):
    return _tile(rows, max(16, STREAM_BLOCK // cols), 16)


def _sigmoid(x):
    return 1.0 / (1.0 + jnp.exp(-x))


def _dsilu(x, s):
    return s * (1.0 + x * (1.0 - s))


def _split(a):
    hi = a.astype(BF16)
    return hi, (a - hi.astype(F32)).astype(BF16)


def _dot3(a, b, dims):
    dot = lambda u, v: lax.dot_general(u, v, dims, preferred_element_type=F32)
    return dot(a[0], b[1]) + dot(a[1], b[0]) + dot(a[0], b[0])


def _fdot(a, b):
    return lax.dot_general(a, b, NN, precision=HIGHEST, preferred_element_type=F32)


MM_TILE_K = 5632
MM_TILE_N = 512


def _matmul_chunks(h, w, which, prev, pos, name, own_shard=False, deps=()):
    t, d = h.shape
    nc_ = w.shape[1] if own_shard else w.shape[1] // 4
    tm, tn = _tile(t, 1024), _tile(nc_, 1408)
    nn = nc_ // tn

    def chunk(q, p):
        sel = p[which[0]]
        for i in range(1, len(which)):
            sel = jnp.where(q == i, p[which[i]], sel)
        return sel

    def body(p_ref, h_ref, w_ref, *rest):
        rest[-1][...] = jnp.dot(h_ref[...], w_ref[...].astype(BF16), preferred_element_type=F32)

    if own_shard:
        w_spec = pl.BlockSpec((d, tn), lambda q, i, j, p: (0, j))
    else:
        w_spec = pl.BlockSpec((d, tn), lambda q, i, j, p: (0, chunk(q, p) * nn + j))
    n_extra = len(deps) + (prev is not None)
    return pl.pallas_call(
        body,
        grid_spec=pltpu.PrefetchScalarGridSpec(
            num_scalar_prefetch=1, grid=(len(which), t // tm, nn),
            in_specs=[pl.BlockSpec((tm, d), lambda q, i, j, p: (i, 0)), w_spec] + [ANY] * n_extra,
            out_specs=pl.BlockSpec((tm, tn), lambda q, i, j, p: (i, chunk(q, p) * nn + j))),
        out_shape=SDS((t, 4 * nc_), F32), input_output_aliases={3 + len(deps): 0} if prev is not None else {},
        compiler_params=_params(("arbitrary", "arbitrary", "arbitrary")),
        name=name)(pos, h, w, *deps, *(() if prev is None else (prev,)))


def _matmul(a, b, *, ta=False, tb=False, res=None, out_dtype=F32, name, deps=(), a_lead=None, b_lead=None):
    a2, b2 = a.shape[-2:], b.shape[-2:]
    m, k = (a2[1], a2[0]) if ta else a2
    n = b2[0] if tb else b2[1]
    if k > MM_TILE_K:
        tk, tm, tn = _tile(k, MM_TILE_K // 2), _tile(m, 1024), _tile(n, 1024)
    else:
        tk = k
        tm, tn = _tile(m, 2048 if tk <= MM_TILE_K // 2 else 1024), _tile(n, MM_TILE_N)
    nk = k // tk
    dims = ((((0,) if ta else (1,)), ((1,) if tb else (0,))), ((), ()))

    def body(*refs):
        n_in = 2 + (res is not None)
        a_ref, b_ref = refs[:2]
        r_ref = refs[2] if res is not None else None
        o_ref = refs[n_in + len(deps)]
        part = lax.dot_general(a_ref[...].astype(BF16), b_ref[...].astype(BF16), dims, preferred_element_type=F32)

        def finish(out):
            if r_ref is not None:
                out = out + r_ref[...]
            o_ref[...] = out.astype(o_ref.dtype)

        if nk == 1:
            finish(part)
            return
        acc_ref = refs[-1]
        kk = pl.program_id(2)

        @pl.when(kk == 0)
        def _():
            acc_ref[...] = part

        @pl.when(jnp.logical_and(kk > 0, kk < nk - 1))
        def _():
            acc_ref[...] += part

        @pl.when(kk == nk - 1)
        def _():
            finish(acc_ref[...] + part)

    def spec(block, index, lead):
        if lead is None:
            return pl.BlockSpec(block, index)
        return pl.BlockSpec((None,) + block, lambda i, j, l: (lead,) + index(i, j, l))

    a_spec = spec((tk, tm), lambda i, j, l: (l, i), a_lead) if ta else spec((tm, tk), lambda i, j, l: (i, l), a_lead)
    b_spec = spec((tn, tk), lambda i, j, l: (j, l), b_lead) if tb else spec((tk, tn), lambda i, j, l: (l, j), b_lead)
    o_spec = pl.BlockSpec((tm, tn), lambda i, j, l: (i, j))
    in_specs = [a_spec, b_spec] + ([o_spec] if res is not None else []) + [ANY] * len(deps)
    args = (a, b) + ((res,) if res is not None else ()) + tuple(deps)
    return pl.pallas_call(
        body, grid=(m // tm, n // tn, nk), in_specs=in_specs, out_specs=o_spec,
        out_shape=SDS((m, n), out_dtype), scratch_shapes=[pltpu.VMEM((tm, tn), F32)] if nk > 1 else [],
        compiler_params=_params(("parallel", "parallel", "arbitrary")), name=name)(*args)


def _cast_into_full(w, kind, pos, name):
    r, n = w.shape
    tr = _tile(r, 512, 16)
    nr = r // tr
    if kind == "col":
        shape, o_spec = (r, 4 * n), pl.BlockSpec((tr, n), lambda i, p: (i, p[0]))
    else:
        shape, o_spec = (4 * r, n), pl.BlockSpec((tr, n), lambda i, p: (p[0] * nr + i, 0))

    def body(p_ref, w_ref, o_ref):
        o_ref[...] = w_ref[...].astype(BF16)

    return pl.pallas_call(
        body,
        grid_spec=pltpu.PrefetchScalarGridSpec(num_scalar_prefetch=1, grid=(nr,),
                                               in_specs=[pl.BlockSpec((tr, n), lambda i, p: (i, 0))], out_specs=o_spec),
        out_shape=SDS(shape, BF16), compiler_params=_params(("parallel",)), name=name)(pos, w)


def _rmsnorm_fwd(x, gain, name):
    t, d = x.shape
    tm = _tile(t, 256)

    def body(x_ref, g_ref, h_ref, r_ref):
        xv = x_ref[...]
        r = lax.rsqrt(jnp.mean(xv * xv, axis=-1, keepdims=True) + EPS)
        h_ref[...] = (xv * r * g_ref[...]).astype(BF16)
        r_ref[...] = r

    return pl.pallas_call(
        body, grid=(t // tm,),
        in_specs=[pl.BlockSpec((tm, d), lambda i: (i, 0)), pl.BlockSpec((1, d), lambda i: (0, 0))],
        out_specs=[pl.BlockSpec((tm, d), lambda i: (i, 0)), pl.BlockSpec((tm, 1), lambda i: (i, 0))],
        out_shape=[SDS((t, d), BF16), SDS((t, 1), F32)], compiler_params=_params(("parallel",)), name=name)(x, gain)


def _rmsnorm_bwd(dh, x, r, gain, dres, name, deps=()):
    t, d = x.shape
    tm = _tile(t, 256)

    def body(dh_ref, x_ref, r_ref, g_ref, dres_ref, *rest):
        dx_ref, dxb_ref, dg_ref = rest[len(deps):]

        @pl.when(pl.program_id(0) == 0)
        def _():
            dg_ref[...] = jnp.zeros_like(dg_ref)

        dhv, xv, rv = dh_ref[...], x_ref[...], r_ref[...]
        dg_ref[...] += jnp.sum(dhv * (xv * rv), axis=0, keepdims=True)
        u = dhv * g_ref[...]
        dx = dres_ref[...] + rv * u - xv * (rv * rv * rv) * jnp.mean(u * xv, axis=-1, keepdims=True)
        dx_ref[...] = dx
        dxb_ref[...] = dx.astype(BF16)

    row = pl.BlockSpec((tm, d), lambda i: (i, 0))
    vec = pl.BlockSpec((1, d), lambda i: (0, 0))
    return pl.pallas_call(
        body, grid=(t // tm,),
        in_specs=[row, row, pl.BlockSpec((tm, 1), lambda i: (i, 0)), vec, row] + [ANY] * len(deps),
        out_specs=[row, row, vec], out_shape=[SDS((t, d), F32), SDS((t, d), BF16), SDS((1, d), F32)],
        compiler_params=_params(("arbitrary",)), name=name)(dh, x, r, gain, dres, *deps)


def _proj_merge(y_a, y_b, w_a, w_b, proj, b_gate, off, deps=()):
    t, ka = y_a.shape
    kb = y_b.shape[1]
    d = w_a.shape[1]
    tm, tc = _tile(t, 1024), _tile(d, MM_TILE_N)
    nj = d // tc
    oa, ob = off // tc, off // tc + nj

    def body(ya_ref, yb_ref, wa_ref, wb_ref, la_ref, lb_ref, ba_ref, bb_ref, *rest):
        pa_ref, pb_ref, o_ref = rest[len(deps):]
        pa = jnp.dot(ya_ref[...], wa_ref[...], preferred_element_type=F32)
        pb = jnp.dot(yb_ref[...], wb_ref[...], preferred_element_type=F32)
        pa_ref[...] = pa
        pb_ref[...] = pb
        ga = _sigmoid(la_ref[...] + ba_ref[...])
        gb = _sigmoid(lb_ref[...] + bb_ref[...])
        o_ref[...] = (ga * pa + gb * pb).astype(BF16)

    tile = pl.BlockSpec((tm, tc), lambda i, j: (i, j))
    return pl.pallas_call(
        body, grid=(t // tm, nj),
        in_specs=[pl.BlockSpec((tm, ka), lambda i, j: (i, 0)), pl.BlockSpec((tm, kb), lambda i, j: (i, 0)),
                  pl.BlockSpec((ka, tc), lambda i, j: (0, j)), pl.BlockSpec((kb, tc), lambda i, j: (0, j)),
                  pl.BlockSpec((tm, tc), lambda i, j: (i, oa + j)), pl.BlockSpec((tm, tc), lambda i, j: (i, ob + j)),
                  pl.BlockSpec((1, tc), lambda i, j: (0, j)), pl.BlockSpec((1, tc), lambda i, j: (0, nj + j))]
        + [ANY] * len(deps),
        out_specs=[tile, tile, tile], out_shape=[SDS((t, d), F32), SDS((t, d), F32), SDS((t, d), BF16)],
        compiler_params=_params(("parallel", "parallel")),
        name="proj_merge")(y_a, y_b, w_a, w_b, proj, proj, b_gate, b_gate, *deps)


def _d_merged_gates(dx, w, proj, b_gate, pa, pb, off, deps=()):
    t, k = dx.shape
    d = w.shape[0]
    tm, tc = _tile(t, 1024), _tile(d, MM_TILE_N)
    nj, ni = d // tc, t // tm
    o0 = off // tc

    def body(dx_ref, w_ref, la_ref, lb_ref, ba_ref, bb_ref, pa_ref, pb_ref, *rest):
        dp_ref, dproj_ref, db_ref, stage, sems = rest[len(deps):]
        j, i = pl.program_id(0), pl.program_id(1)
        step = j * ni + i
        slot = step % 2

        def copies(s, ii, jj):
            rows = pl.ds(pl.multiple_of(ii * tm, tm), tm)
            return [pltpu.make_async_copy(
                stage.at[s, br], dproj_ref.at[rows, pl.ds(pl.multiple_of(off + br * d + jj * tc, 128), tc)],
                sems.at[s, br]) for br in range(2)]

        @pl.when(step >= 2)
        def _():
            for cp in copies(slot, 0, 0):
                cp.wait()

        dm = lax.dot_general(dx_ref[...], w_ref[...], NT, preferred_element_type=F32)

        @pl.when(i == 0)
        def _():
            db_ref[...] = jnp.zeros_like(db_ref)

        for br, (l_ref, b_ref, p_ref) in enumerate(((la_ref, ba_ref, pa_ref), (lb_ref, bb_ref, pb_ref))):
            g = _sigmoid(l_ref[...] + b_ref[...])
            dp_ref[br] = (dm * g).astype(BF16)
            dl = dm * p_ref[...] * g * (1.0 - g)
            stage[slot, br] = dl.astype(BF16)
            db_ref[br] += jnp.sum(dl, axis=0, keepdims=True)
        for cp in copies(slot, i, j):
            cp.start()

        @pl.when(step == ni * nj - 1)
        def _():
            for s in range(min(2, ni * nj)):
                for cp in copies(s, 0, 0):
                    cp.wait()

    tile = pl.BlockSpec((tm, tc), lambda j, i: (i, j))
    return pl.pallas_call(
        body, grid=(nj, ni),
        in_specs=[pl.BlockSpec((tm, k), lambda j, i: (i, 0)), pl.BlockSpec((tc, k), lambda j, i: (j, 0)),
                  pl.BlockSpec((tm, tc), lambda j, i: (i, o0 + j)), pl.BlockSpec((tm, tc), lambda j, i: (i, o0 + nj + j)),
                  pl.BlockSpec((1, tc), lambda j, i: (0, j)), pl.BlockSpec((1, tc), lambda j, i: (0, nj + j)),
                  tile, tile] + [ANY] * len(deps),
        out_specs=[pl.BlockSpec((2, tm, tc), lambda j, i: (0, i, j)), ANY, pl.BlockSpec((2, 1, tc), lambda j, i: (0, 0, j))],
        out_shape=[SDS((2, t, d), BF16), SDS(proj.shape, BF16), SDS((2, 1, d), F32)],
        scratch_shapes=[pltpu.VMEM((2, 2, tm, tc), BF16), pltpu.SemaphoreType.DMA((2, 2))],
        compiler_params=_params(("arbitrary", "arbitrary")),
        name="d_merged_gates")(dx, w, proj, proj, b_gate, b_gate, pa, pb, *deps)


def _ffn_in_swiglu(h, w, deps=()):
    t, d = h.shape
    f = w.shape[1] // 2
    tm, tn = _tile(t, 2048), _tile(f, MM_TILE_N)
    nj = f // tn

    def body(h_ref, wg_ref, wu_ref, *rest):
        g_ref, u_ref, a_ref = rest[len(deps):]
        hv = h_ref[...]
        g = jnp.dot(hv, wg_ref[...], preferred_element_type=F32)
        u = jnp.dot(hv, wu_ref[...], preferred_element_type=F32)
        g_ref[...] = g
        u_ref[...] = u
        a_ref[...] = (g * _sigmoid(g) * u).astype(BF16)

    tile = pl.BlockSpec((tm, tn), lambda i, j: (i, j))
    return pl.pallas_call(
        body, grid=(t // tm, nj),
        in_specs=[pl.BlockSpec((tm, d), lambda i, j: (i, 0)), pl.BlockSpec((d, tn), lambda i, j: (0, j)),
                  pl.BlockSpec((d, tn), lambda i, j: (0, nj + j))] + [ANY] * len(deps),
        out_specs=[tile, tile, tile], out_shape=[SDS((t, f), F32), SDS((t, f), F32), SDS((t, f), BF16)],
        compiler_params=_params(("parallel", "parallel")), name="ffn_in_swiglu")(h, w, w, *deps)


def _d_act_swiglu(dy, w, gate, up, deps=()):
    t, k = dy.shape
    f = w.shape[0]
    tm, tn = _tile(t, 1024), _tile(f, MM_TILE_N)
    ni, nj = t // tm, f // tn

    def body(dy_ref, w_ref, g_ref, u_ref, *rest):
        out_ref, stage, sems = rest[len(deps):]
        i, j = pl.program_id(0), pl.program_id(1)
        step = i * nj + j
        slot = step % 2

        def copies(s, ii, jj):
            rows = pl.ds(pl.multiple_of(ii * tm, tm), tm)
            return [pltpu.make_async_copy(
                stage.at[s, half], out_ref.at[rows, pl.ds(pl.multiple_of(half * f + jj * tn, 128), tn)],
                sems.at[s, half]) for half in range(2)]

        @pl.when(step >= 2)
        def _():
            for cp in copies(slot, 0, 0):
                cp.wait()

        dact = lax.dot_general(dy_ref[...], w_ref[...], NT, preferred_element_type=F32)
        g = g_ref[...]
        sg = _sigmoid(g)
        stage[slot, 0] = (dact * u_ref[...] * _dsilu(g, sg)).astype(BF16)
        stage[slot, 1] = (dact * (g * sg)).astype(BF16)
        for cp in copies(slot, i, j):
            cp.start()

        @pl.when(step == ni * nj - 1)
        def _():
            for s in range(min(2, ni * nj)):
                for cp in copies(s, 0, 0):
                    cp.wait()

    tile = pl.BlockSpec((tm, tn), lambda i, j: (i, j))
    return pl.pallas_call(
        body, grid=(ni, nj),
        in_specs=[pl.BlockSpec((tm, k), lambda i, j: (i, 0)), pl.BlockSpec((tn, k), lambda i, j: (j, 0)), tile, tile]
        + [ANY] * len(deps),
        out_specs=ANY, out_shape=SDS((t, 2 * f), BF16),
        scratch_shapes=[pltpu.VMEM((2, 2, tm, tn), BF16), pltpu.SemaphoreType.DMA((2, 2))],
        compiler_params=_params(("arbitrary", "arbitrary")), name="d_act_swiglu")(dy, w, gate, up, *deps)


def _ffn_out_loss(act, w, x_res, target):
    t, d = x_res.shape
    k = act.shape[1]
    tm, tn = _tile(t, 1024), _tile(d, MM_TILE_N)

    def body(a_ref, w_ref, r_ref, t_ref, dy_ref, dyb_ref, l_ref):
        @pl.when(jnp.logical_and(pl.program_id(0) == 0, pl.program_id(1) == 0))
        def _():
            l_ref[...] = jnp.zeros_like(l_ref)

        y = jnp.dot(a_ref[...], w_ref[...], preferred_element_type=F32) + r_ref[...]
        e = y - t_ref[...]
        dy = e * (1.0 / d)
        dy_ref[...] = dy
        dyb_ref[...] = dy.astype(BF16)
        l_ref[...] += (0.5 / d) * jnp.sum(jnp.sum(e * e, axis=-1, keepdims=True), axis=0, keepdims=True)

    tile = pl.BlockSpec((tm, tn), lambda i, j: (i, j))
    return pl.pallas_call(
        body, grid=(t // tm, d // tn),
        in_specs=[pl.BlockSpec((tm, k), lambda i, j: (i, 0)), pl.BlockSpec((k, tn), lambda i, j: (0, j)), tile, tile],
        out_specs=[tile, tile, pl.BlockSpec((1, 1), lambda i, j: (0, 0))],
        out_shape=[SDS((t, d), F32), SDS((t, d), BF16), SDS((1, 1), F32)],
        compiler_params=_params(("arbitrary", "arbitrary")), name="ffn_out_loss")(act, w, x_res, target)


def _rel_onehot(qi):
    p = lax.broadcasted_iota(jnp.int32, (REL_LANES, BAND), 1)
    r = lax.broadcasted_iota(jnp.int32, (REL_LANES, BAND), 0)
    idx = jnp.clip(qi + PAD - p, -REL_FUTURE, REL_PAST) + REL_FUTURE
    return (idx == r).astype(F32)


def _relbias_expand(rb):
    h = rb.shape[0]

    def body(rb_ref, o_ref):
        def step(qi, _):
            o_ref[qi] = _fdot(rb_ref[...], _rel_onehot(qi))
            return 0

        lax.fori_loop(0, CHUNK, step, 0)

    return pl.pallas_call(body, out_shape=SDS((CHUNK, h, BAND), F32), compiler_params=_params(),
                          name="relbias_expand")(rb)


def _relbias_reduce(dbias):
    h = dbias.shape[1]

    rows_per_pass = 4

    def body(db_ref, o_ref):
        def step(i, acc):
            parts = []
            for u in range(rows_per_pass):
                qi = i * rows_per_pass + u
                xv = db_ref[qi]
                hi = xv.astype(BF16)
                rest = xv - hi.astype(F32)
                mid = rest.astype(BF16)
                low = (rest - mid.astype(F32)).astype(BF16)
                parts.append(lax.dot_general(jnp.concatenate([hi, mid, low], axis=0), _rel_onehot(qi).astype(BF16), NT,
                                             preferred_element_type=F32))
            for part in parts:
                acc = acc + (part[0:h] + part[h:2 * h] + part[2 * h:3 * h])
            return acc

        o_ref[...] = lax.fori_loop(0, CHUNK // rows_per_pass, step, jnp.zeros((h, REL_LANES), F32))

    return pl.pallas_call(body, out_shape=SDS((h, REL_LANES), F32), compiler_params=_params(),
                          name="relbias_reduce")(dbias)


def _lower_bound(l_ref):
    l0, l1 = l_ref[0:1, :], l_ref[1:2, :]
    m = jnp.maximum(l0, l1)
    e0, e1 = jnp.exp(l0 - m), jnp.exp(l1 - m)
    return e0 / (e0 + e1)


def _tri(lower):
    r = lax.broadcasted_iota(jnp.int32, (CHUNK, CHUNK), 0)
    c = lax.broadcasted_iota(jnp.int32, (CHUNK, CHUNK), 1)
    return r >= c if lower else r <= c


def _hgrn_intra(qs, kk, b_s):
    rows = lax.broadcasted_iota(jnp.int32, (CHUNK, HEAD), 0)
    b = b_s[...]
    out = []
    for i in range(CHUNK // SUB):
        lo = i * SUB
        ref = jnp.zeros((1, HEAD), F32) if i == 0 else b_s[lo - 1:lo, :]
        eq = jnp.exp(b[lo:lo + SUB] - ref)
        qt = _split(qs[lo:lo + SUB] * eq)
        e = jnp.where(rows < lo + SUB, jnp.exp(jnp.minimum(ref - b, EXP_CLAMP)), 0.0)
        kt = _split(kk * e)
        out.append((eq, qt, e, kt))
    return out


def _hgrn_scores(blocks):
    tr = lax.broadcasted_iota(jnp.int32, (SUB, CHUNK), 0)
    tc = lax.broadcasted_iota(jnp.int32, (SUB, CHUNK), 1)
    return jnp.concatenate([jnp.where(tc <= tr + i * SUB, _dot3(qt, kt, NT), 0.0)
                            for i, (_, qt, _, kt) in enumerate(blocks)], axis=0)


def _hgrn_fwd(proj, lb_logits, gain, n_heads):
    t = proj.shape[0]
    nc = t // CHUNK
    da = n_heads * HEAD
    hp = MIX_HEADS
    wide = hp * HEAD

    def body(q_ref, f_ref, i_ref, g_ref, l_ref, gain_ref, y_ref, o_ref, st_ref, state, b_s):
        state[...] = jnp.zeros_like(state)
        lb_all = _lower_bound(l_ref)
        tril = _tri(True).astype(F32)

        def chunks(i, _):
            dot = functools.partial(lax.dot_general, preferred_element_type=F32)
            items = []
            for u in range(MIX_UNROLL):
                for hh in range(hp):
                    j = i * MIX_UNROLL + u
                    sl = pl.ds(pl.multiple_of(j * CHUNK, CHUNK), CHUNK)
                    cols = slice(hh * HEAD, (hh + 1) * HEAD)
                    lb = lb_all[:, cols]
                    fg = lb + (1.0 - lb) * _sigmoid(f_ref[sl, cols])
                    qv = q_ref[sl, cols]
                    gv = g_ref[sl, cols]
                    items.append(dict(hh=hh, j=j, sl=sl, cols=cols, lf=jnp.log(fg), kk=1.0 - fg, qs=qv * _sigmoid(qv),
                                      vb=i_ref[sl, cols].astype(BF16), gate=gv * _sigmoid(gv)))
            for it in items:
                it["b"] = _fdot(tril, it["lf"])
            for slot, it in enumerate(items):
                b = it["b"]
                b_s[slot] = b
                it["blocks"] = _hgrn_intra(it["qs"], it["kk"], b_s.at[slot])
                it["ebl"] = jnp.exp(b_s[slot, CHUNK - 1:CHUNK, :])
                it["qe"] = (it["qs"] * jnp.exp(b)).astype(BF16)
                it["kd"] = (it["kk"] * jnp.exp(b_s[slot, CHUNK - 1:CHUNK, :] - b)).astype(BF16)
            for it in items:
                it["a"] = _hgrn_scores(it["blocks"]).astype(BF16)
            for it in items:
                it["kv"] = dot(it["vb"], it["kd"], TN)
                it["o"] = dot(it["a"], it["vb"], NN)
            s_now = [state[hh] for hh in range(hp)]
            for it in items:
                it["s_in"] = s_now[it["hh"]]
                s_now[it["hh"]] = it["s_in"] * it["ebl"] + it["kv"]
            for hh in range(hp):
                state[hh] = s_now[hh]
            for it in items:
                it["o"] = it["o"] + dot(it["qe"], it["s_in"].astype(BF16), NT)
            for it in items:
                o, sl, cols = it["o"], it["sl"], it["cols"]
                st_ref[it["hh"], it["j"]] = it["s_in"]
                o_ref[sl, cols] = o
                rr = lax.rsqrt(jnp.mean(o * o, axis=-1, keepdims=True) + EPS)
                y_ref[sl, cols] = (o * rr * gain_ref[:, cols] * it["gate"]).astype(BF16)
            return 0

        assert nc % MIX_UNROLL == 0, (nc, MIX_UNROLL)
        lax.fori_loop(0, nc // MIX_UNROLL, chunks, 0)

    col = lambda k: pl.BlockSpec((t, wide), lambda h: (0, k * (n_heads // hp) + h))
    vec = pl.BlockSpec((1, wide), lambda h: (0, h))
    return pl.pallas_call(
        body, grid=(n_heads // hp,),
        in_specs=[col(0), col(1), col(2), col(3), pl.BlockSpec((2, wide), lambda h: (0, h)), vec],
        out_specs=[pl.BlockSpec((t, wide), lambda h: (0, h)), pl.BlockSpec((t, wide), lambda h: (0, h)),
                   pl.BlockSpec((hp, nc, HEAD, HEAD), lambda h: (h, 0, 0, 0))],
        out_shape=[SDS((t, da), BF16), SDS((t, da), F32), SDS((n_heads, nc, HEAD, HEAD), F32)],
        scratch_shapes=[pltpu.VMEM((hp, HEAD, HEAD), F32), pltpu.VMEM((hp * MIX_UNROLL, CHUNK, HEAD), F32)],
        compiler_params=_params(("parallel",)), name="hgrn_fwd")(proj, proj, proj, proj, lb_logits, gain)


def _write_column_groups(res, dproj_ref, sems, col0, stride, h, width):
    copies = [pltpu.make_async_copy(
        res.at[p], dproj_ref.at[:, pl.ds(pl.multiple_of((col0 + p * stride + h) * width, HEAD), width)], sems.at[p])
        for p in range(res.shape[0])]
    for cp in copies:
        cp.start()
    for cp in copies:
        cp.wait()


def _hgrn_bwd(dproj, proj, o_pre, states, dy, lb_logits, gain, n_heads, deps=()):
    t = proj.shape[0]
    nc = t // CHUNK
    da = n_heads * HEAD
    hp = MIX_HEADS
    wide = hp * HEAD

    def body(*refs):
        (q_ref, f_ref, i_ref, g_ref, o_ref, st_ref, dy_ref, l_ref, gain_ref,
         dproj_ref, dl_ref, dgain_ref, res, dstate, b_s, out_sems) = refs[1 + len(deps):]

        def compute():
            dstate[...] = jnp.zeros_like(dstate)
            lb_all = _lower_bound(l_ref)
            tril_m, tril, triu = _tri(True), _tri(True).astype(F32), _tri(False).astype(F32)
            last = lax.broadcasted_iota(jnp.int32, (CHUNK, HEAD), 0) == CHUNK - 1

            def chunks(i, carry):
                dot = functools.partial(lax.dot_general, preferred_element_type=F32)
                items = []
                for u in range(MIX_UNROLL_BWD):
                    for hh in range(hp):
                        j = nc - 1 - (i * MIX_UNROLL_BWD + u)
                        sl = pl.ds(pl.multiple_of(j * CHUNK, CHUNK), CHUNK)
                        cols = slice(hh * HEAD, (hh + 1) * HEAD)
                        lb, gain_v = lb_all[:, cols], gain_ref[:, cols]
                        sg = _sigmoid(f_ref[sl, cols])
                        fg = lb + (1.0 - lb) * sg
                        qv = q_ref[sl, cols]
                        sq = _sigmoid(qv)
                        gv = g_ref[sl, cols]
                        sgg = _sigmoid(gv)
                        silg = gv * sgg
                        o = o_ref[sl, cols]
                        dyv = dy_ref[sl, cols]
                        rr = lax.rsqrt(jnp.mean(o * o, axis=-1, keepdims=True) + EPS)
                        on = o * rr
                        don = dyv * gain_v * silg
                        do = (rr * don - o * (rr * rr * rr) * jnp.mean(don * o, axis=-1, keepdims=True)).astype(BF16)
                        items.append(dict(
                            hh=hh, j=j, sl=sl, cols=cols, lb=lb, sg=sg, fg=fg, kk=1.0 - fg, qv=qv, sq=sq, qs=qv * sq,
                            vb=i_ref[sl, cols].astype(BF16), do=do, dg=dyv * on * gain_v * _dsilu(gv, sgg),
                            dgain=jnp.sum(dyv * on * silg, axis=0, keepdims=True)))
                for it in items:
                    it["b"] = _fdot(tril, jnp.log(it["fg"]))
                for slot, it in enumerate(items):
                    b = it["b"]
                    b_s[slot] = b
                    it["blocks"] = _hgrn_intra(it["qs"], it["kk"], b_s.at[slot])
                    bl = b_s[slot, CHUNK - 1:CHUNK, :]
                    it["eb"], it["ebl"], it["ekd"] = jnp.exp(b), jnp.exp(bl), jnp.exp(bl - b)
                    it["s_in"] = st_ref[it["hh"], it["j"]]
                for it in items:
                    it["a"] = _hgrn_scores(it["blocks"]).astype(BF16)
                    it["da"] = jnp.where(tril_m, dot(it["do"], it["vb"], NT), 0.0)
                for it in items:
                    dq_rows = []
                    dk = jnp.zeros((CHUNK, HEAD), F32)
                    for blk, (eq, qt, e, kt) in enumerate(it["blocks"]):
                        da_i = _split(it["da"][blk * SUB:(blk + 1) * SUB])
                        dq_rows.append(eq * _dot3(da_i, kt, NN))
                        dk = dk + e * _dot3(da_i, qt, TN)
                    it["dq"] = jnp.concatenate(dq_rows, axis=0) + dot(it["do"], it["s_in"].astype(BF16), NN) * it["eb"]
                    it["dk"] = dk
                    it["dv"] = dot(it["a"], it["do"], TN)
                    it["g"] = dot(it["do"], (it["qs"] * it["eb"]).astype(BF16), TN)
                ds_now = [dstate[hh] for hh in range(hp)]
                for it in items:
                    it["ds_out"] = ds_now[it["hh"]]
                    ds_now[it["hh"]] = it["ds_out"] * it["ebl"] + it["g"]
                for hh in range(hp):
                    dstate[hh] = ds_now[hh]
                for it in items:
                    dsb = it["ds_out"].astype(BF16)
                    it["dv"] = it["dv"] + dot((it["kk"] * it["ekd"]).astype(BF16), dsb, NT)
                    it["dk_state"] = it["ekd"] * dot(it["vb"], dsb, NN)
                for it in items:
                    kk, dk_state = it["kk"], it["dk_state"]
                    it["dk"] = it["dk"] + dk_state
                    extra = (jnp.sum(kk * dk_state, axis=0, keepdims=True)
                             + it["ebl"] * jnp.sum(it["s_in"] * it["ds_out"], axis=0, keepdims=True))
                    it["db"] = it["qs"] * it["dq"] - kk * it["dk"] + jnp.where(last, extra, 0.0)
                for it in items:
                    it["dlf"] = _fdot(triu, it["db"])
                carry = list(carry)
                for it in items:
                    hh, sl, cols, sg, lb = it["hh"], it["sl"], it["cols"], it["sg"], it["lb"]
                    dfg = it["dlf"] / it["fg"] - it["dk"]
                    dlb_acc, dgain_acc = carry[hh]
                    carry[hh] = (dlb_acc + jnp.sum(dfg * (1.0 - sg), axis=0, keepdims=True), dgain_acc + it["dgain"])
                    res[0, sl, cols] = (it["dq"] * _dsilu(it["qv"], it["sq"])).astype(BF16)
                    res[1, sl, cols] = (dfg * (1.0 - lb) * sg * (1.0 - sg)).astype(BF16)
                    res[2, sl, cols] = it["dv"].astype(BF16)
                    res[3, sl, cols] = it["dg"].astype(BF16)
                return tuple(carry)

            assert nc % MIX_UNROLL_BWD == 0, (nc, MIX_UNROLL_BWD)
            zero = jnp.zeros((1, HEAD), F32)
            sums = lax.fori_loop(0, nc // MIX_UNROLL_BWD, chunks, ((zero, zero),) * hp)
            for hh, (dlb, dgain) in enumerate(sums):
                cols = slice(hh * HEAD, (hh + 1) * HEAD)
                lb = lb_all[:, cols]
                dgain_ref[:, cols] = dgain
                dl0 = dlb * lb * (1.0 - lb)
                dl_ref[0:1, cols] = dl0
                dl_ref[1:2, cols] = -dl0

        compute()
        _write_column_groups(res, dproj_ref, out_sems, 0, ng, pl.program_id(0), wide)

    ng = n_heads // hp
    col = lambda k: pl.BlockSpec((t, wide), lambda h: (0, k * ng + h))
    head = pl.BlockSpec((t, wide), lambda h: (0, h))
    vec = pl.BlockSpec((1, wide), lambda h: (0, h))
    return pl.pallas_call(
        body, grid=(ng,),
        in_specs=[ANY] * (1 + len(deps)) + [col(0), col(1), col(2), col(3), head,
                  pl.BlockSpec((hp, nc, HEAD, HEAD), lambda h: (h, 0, 0, 0)),
                  head, pl.BlockSpec((2, wide), lambda h: (0, h)), vec],
        out_specs=[ANY, pl.BlockSpec((2, wide), lambda h: (0, h)), vec],
        out_shape=[SDS(dproj.shape, BF16), SDS((2, da), F32), SDS((1, da), F32)],
        scratch_shapes=[pltpu.VMEM((4, t, wide), BF16), pltpu.VMEM((hp, HEAD, HEAD), F32),
                        pltpu.VMEM((hp * MIX_UNROLL_BWD, CHUNK, HEAD), F32), pltpu.SemaphoreType.DMA((4,))],
        input_output_aliases={0: 0}, compiler_params=_params(("arbitrary",)),
        name="hgrn_bwd")(dproj, *deps, proj, proj, proj, proj, o_pre, states, dy, lb_logits, gain)


ROWS = 256


def _head_norm(x_ref, gain, dst, dst_off, t):
    def step(i, _):
        sl = pl.ds(pl.multiple_of(i * ROWS, ROWS), ROWS)
        xv = x_ref[sl, :]
        r = lax.rsqrt(jnp.mean(xv * xv, axis=-1, keepdims=True) + EPS)
        dst[pl.ds(pl.multiple_of(dst_off + i * ROWS, ROWS), ROWS), :] = (xv * r * gain).astype(BF16)
        return 0

    lax.fori_loop(0, t // ROWS, step, 0)


def _head_norm_bwd(x_ref, gain, dn_ref, dn_off, out, slot, t):
    def step(i, acc):
        sl = pl.ds(pl.multiple_of(i * ROWS, ROWS), ROWS)
        xv = x_ref[sl, :]
        dn = dn_ref[pl.ds(pl.multiple_of(dn_off + i * ROWS, ROWS), ROWS), :]
        r = lax.rsqrt(jnp.mean(xv * xv, axis=-1, keepdims=True) + EPS)
        u = dn * gain
        out[slot, sl, :] = (r * u - xv * (r * r * r) * jnp.mean(u * xv, axis=-1, keepdims=True)).astype(out.dtype)
        return acc + jnp.sum(dn * (xv * r), axis=0, keepdims=True)

    return lax.fori_loop(0, t // ROWS, step, jnp.zeros((1, HEAD), F32))


def _attn_scores(qn, kpad, n):
    qc = qn[pl.ds(pl.multiple_of(n * CHUNK, CHUNK), CHUNK), :]
    band = pl.ds(pl.multiple_of(n * CHUNK, CHUNK), BAND)
    return qc, band, lax.dot_general(qc, kpad[band, :], NT, preferred_element_type=F32)


def _attn_softmax(raw, bias_ref, n):
    s = raw * (HEAD ** -0.5) + bias_ref[0]
    col = lax.broadcasted_iota(jnp.int32, (CHUNK, BAND), 1)
    s = jnp.where(col >= PAD - n * CHUNK, s, -jnp.inf)
    p = jnp.exp(s - jnp.max(s, axis=-1, keepdims=True))
    return p / jnp.sum(p, axis=-1, keepdims=True)


def _attn_fwd(proj, q_gain, k_gain, bias, n_heads, col0):
    t = proj.shape[0]
    nc = t // CHUNK

    def body(q_ref, k_ref, v_ref, qg_ref, kg_ref, bias_ref, y_ref, qn, kpad, vpad):
        kpad[0:PAD, :] = jnp.zeros((PAD, HEAD), BF16)
        vpad[0:PAD, :] = jnp.zeros((PAD, HEAD), BF16)
        _head_norm(q_ref, qg_ref[...], qn, 0, t)
        _head_norm(k_ref, kg_ref[...], kpad, PAD, t)

        def copy_v(i, _):
            vpad[pl.ds(pl.multiple_of(PAD + i * ROWS, ROWS), ROWS), :] = v_ref[
                pl.ds(pl.multiple_of(i * ROWS, ROWS), ROWS), :].astype(BF16)
            return 0

        lax.fori_loop(0, t // ROWS, copy_v, 0)

        def chunks(i, _):
            ns = [i * ATT_UNROLL + u for u in range(ATT_UNROLL)]
            scored = [_attn_scores(qn, kpad, n) for n in ns]
            probs = [_attn_softmax(raw, bias_ref, n).astype(BF16) for n, (_, _, raw) in zip(ns, scored)]
            outs = [lax.dot_general(p, vpad[band, :], NN, preferred_element_type=F32).astype(BF16)
                    for p, (_, band, _) in zip(probs, scored)]
            for n, o in zip(ns, outs):
                y_ref[pl.ds(pl.multiple_of(n * CHUNK, CHUNK), CHUNK), :] = o
            return 0

        assert nc % ATT_UNROLL == 0, (nc, ATT_UNROLL)
        lax.fori_loop(0, nc // ATT_UNROLL, chunks, 0)

    col = lambda k: pl.BlockSpec((t, HEAD), lambda h: (0, col0 + k * n_heads + h))
    vec = pl.BlockSpec((1, HEAD), lambda h: (0, 0))
    return pl.pallas_call(
        body, grid=(n_heads,),
        in_specs=[col(0), col(1), col(2), vec, vec, pl.BlockSpec((1, CHUNK, BAND), lambda h: (h, 0, 0))],
        out_specs=pl.BlockSpec((t, HEAD), lambda h: (0, h)), out_shape=SDS((t, n_heads * HEAD), BF16),
        scratch_shapes=[pltpu.VMEM((t, HEAD), BF16), pltpu.VMEM((t + PAD, HEAD), BF16), pltpu.VMEM((t + PAD, HEAD), BF16)],
        compiler_params=_params(("parallel",)), name="attn_fwd")(proj, proj, proj, q_gain, k_gain, bias)


def _attn_bwd(dproj, proj, q_gain, k_gain, bias, dy, n_heads, col0, deps=()):
    t = proj.shape[0]
    nc = t // CHUNK

    def body(*refs):
        (q_ref, k_ref, v_ref, qg_ref, kg_ref, bias_ref, dy_ref,
         dproj_ref, dbias_ref, dqg_ref, dkg_ref, qn, kpad, vpad, dqn, dk_acc, dv_acc, res,
         out_sems) = refs[1 + len(deps):]
        h = pl.program_id(0)

        def compute():
            kpad[0:PAD, :] = jnp.zeros((PAD, HEAD), BF16)
            vpad[0:PAD, :] = jnp.zeros((PAD, HEAD), BF16)
            _head_norm(q_ref, qg_ref[...], qn, 0, t)
            _head_norm(k_ref, kg_ref[...], kpad, PAD, t)

            def prep(i, _):
                sl = pl.ds(pl.multiple_of(PAD + i * ROWS, ROWS), ROWS)
                vpad[sl, :] = v_ref[pl.ds(pl.multiple_of(i * ROWS, ROWS), ROWS), :].astype(BF16)
                return 0

            lax.fori_loop(0, t // ROWS, prep, 0)

            def clear(i, _):
                sl = pl.ds(pl.multiple_of(i * ROWS, ROWS), ROWS)
                dk_acc[sl, :] = jnp.zeros((ROWS, HEAD), F32)
                dv_acc[sl, :] = jnp.zeros((ROWS, HEAD), F32)
                return 0

            lax.fori_loop(0, (t + PAD) // ROWS, clear, 0)
            dbias_ref[0] = jnp.zeros((CHUNK, BAND), F32)

            def chunks(i, _):
                dot = functools.partial(lax.dot_general, preferred_element_type=F32)
                ns = [i * ATT_UNROLL_BWD + u for u in range(ATT_UNROLL_BWD)]
                scored = [_attn_scores(qn, kpad, n) for n in ns]
                dos = [dy_ref[pl.ds(pl.multiple_of(n * CHUNK, CHUNK), CHUNK), :].astype(BF16) for n in ns]
                dps = [dot(do, vpad[band, :], NT) for do, (_, band, _) in zip(dos, scored)]
                ps, dss = [], []
                for n, (_, _, raw), dp in zip(ns, scored, dps):
                    p = _attn_softmax(raw, bias_ref, n)
                    ds = p * (dp - jnp.sum(dp * p, axis=-1, keepdims=True))
                    dbias_ref[0] += ds
                    ps.append(p.astype(BF16))
                    dss.append((ds * (HEAD ** -0.5)).astype(BF16))
                dqs = [dot(d, kpad[band, :], NN) for d, (_, band, _) in zip(dss, scored)]
                dks = [dot(d, qc, TN) for d, (qc, _, _) in zip(dss, scored)]
                dvs = [dot(p, do, TN) for p, do in zip(ps, dos)]
                for n, (_, band, _), dq, dk, dv in zip(ns, scored, dqs, dks, dvs):
                    dqn[pl.ds(pl.multiple_of(n * CHUNK, CHUNK), CHUNK), :] = dq
                    dk_acc[band, :] += dk
                    dv_acc[band, :] += dv
                return 0

            assert nc % ATT_UNROLL_BWD == 0, (nc, ATT_UNROLL_BWD)
            lax.fori_loop(0, nc // ATT_UNROLL_BWD, chunks, 0)
            dqg = _head_norm_bwd(q_ref, qg_ref[...], dqn, 0, res, 0, t)
            dkg = _head_norm_bwd(k_ref, kg_ref[...], dk_acc, PAD, res, 1, t)

            def put_v(i, _):
                sl = pl.ds(pl.multiple_of(i * ROWS, ROWS), ROWS)
                res[2, sl, :] = dv_acc[pl.ds(pl.multiple_of(PAD + i * ROWS, ROWS), ROWS), :].astype(BF16)
                return 0

            lax.fori_loop(0, t // ROWS, put_v, 0)

            @pl.when(h == 0)
            def _():
                dqg_ref[...] = jnp.zeros_like(dqg_ref)
                dkg_ref[...] = jnp.zeros_like(dkg_ref)

            dqg_ref[...] += dqg
            dkg_ref[...] += dkg

        compute()
        _write_column_groups(res, dproj_ref, out_sems, col0, n_heads, h, HEAD)

    col = lambda k: pl.BlockSpec((t, HEAD), lambda h: (0, col0 + k * n_heads + h))
    vec = pl.BlockSpec((1, HEAD), lambda h: (0, 0))
    btile = pl.BlockSpec((1, CHUNK, BAND), lambda h: (h, 0, 0))
    return pl.pallas_call(
        body, grid=(n_heads,),
        in_specs=[ANY] * (1 + len(deps)) + [col(0), col(1), col(2), vec, vec, btile,
                                            pl.BlockSpec((t, HEAD), lambda h: (0, h))],
        out_specs=[ANY, btile, vec, vec],
        out_shape=[SDS(dproj.shape, BF16), SDS((n_heads, CHUNK, BAND), F32), SDS((1, HEAD), F32), SDS((1, HEAD), F32)],
        scratch_shapes=[pltpu.VMEM((t, HEAD), BF16), pltpu.VMEM((t + PAD, HEAD), BF16), pltpu.VMEM((t + PAD, HEAD), BF16),
                        pltpu.VMEM((t, HEAD), F32), pltpu.VMEM((t + PAD, HEAD), F32), pltpu.VMEM((t + PAD, HEAD), F32),
                        pltpu.VMEM((3, t, HEAD), BF16), pltpu.SemaphoreType.DMA((3,))],
        input_output_aliases={0: 0}, compiler_params=_params(("arbitrary",)),
        name="attn_bwd")(dproj, *deps, proj, proj, proj, q_gain, k_gain, bias, dy)


def _place():
    x, y, c = lax.axis_index("x"), lax.axis_index("y"), lax.axis_index("c")
    others = [(1 - x, y), (x, 1 - y), (1 - x, 1 - y)]
    return x, y, c, others


def _chunk_of(ref, kind, chip, half, shard_shape):
    r, n = shard_shape
    hr = r // 2
    if kind == "col":
        rows = pl.ds(0, r) if half is None else pl.ds(half * hr, hr)
        return ref.at[rows, pl.ds(chip * n, n)]
    rows = pl.ds(chip * r, r) if half is None else pl.ds(chip * r + half * hr, hr)
    return ref.at[rows, :]


EFFECT = pltpu.SideEffectType.DATAFLOW_SIDE_EFFECTING


def _start_copies(name, bufs, plan, n, deps):
    nb, nd = len(bufs), len(deps)

    def body(*refs):
        send, recv, token = refs[nb + nd], refs[nb + nd + 1], refs[-1]
        for cp in plan(refs[:nb], send, recv)[0]:
            cp.start()
        token[...] = jnp.zeros_like(token)

    out = pl.pallas_call(
        body, name=name,
        out_shape=(pltpu.SemaphoreType.DMA((n,)), pltpu.SemaphoreType.DMA((n,)),
                   *[pltpu.HBM(b.shape, b.dtype) for b in bufs], SDS((8, 128), F32)),
        in_specs=[HBM] * nb + [ANY] * nd,
        out_specs=(SEM, SEM, *[HBM] * nb, pl.BlockSpec(memory_space=pltpu.VMEM)),
        input_output_aliases={i: 2 + i for i in range(nb)},
        compiler_params=pltpu.CompilerParams(has_side_effects=EFFECT),
    )(*[pltpu.with_memory_space_constraint(b, pltpu.HBM) for b in bufs], *deps)
    return out[0], out[1], list(out[2:2 + nb]), out[-1]


def _wait_copies(name, bufs, send, recv, plan, after):
    nb = len(bufs)

    def body(*refs):
        sends, recvs = plan(refs[:nb], refs[nb], refs[nb + 1])
        for cp in sends:
            cp.wait_send()
        for cp in recvs:
            cp.wait_recv()

    out = pl.pallas_call(
        body, name=name, out_shape=tuple(pltpu.HBM(b.shape, b.dtype) for b in bufs),
        in_specs=[HBM] * nb + [SEM, SEM] + [ANY] * len(after), out_specs=tuple([HBM] * nb),
        input_output_aliases={i: i for i in range(nb)},
        compiler_params=pltpu.CompilerParams(has_side_effects=EFFECT),
    )(*bufs, send, recv, *after)
    return list(out)


def _remote(src, dst, send, recv, i, dev):
    return pltpu.make_async_remote_copy(src_ref=src, dst_ref=dst, send_sem=send.at[i], recv_sem=recv.at[i],
                                        device_id=dev, device_id_type=MESH)


ALL_RELATIONS = (0, 1, 2)


def _plan_gather_ici(kinds, shapes, rels=ALL_RELATIONS):
    def plan(refs, send, recv):
        x, y, c, others = _place()
        sends, recvs = [], []
        for w, (kind, ss) in enumerate(zip(kinds, shapes)):
            for p in rels:
                px, py = others[p]
                mine = _chunk_of(refs[w], kind, 2 * x + y, c, ss)
                theirs = _chunk_of(refs[w], kind, 2 * px + py, c, ss)
                sends.append(_remote(mine, mine, send, recv, 3 * w + p, (px, py, c)))
                recvs.append(_remote(theirs, theirs, send, recv, 3 * w + p, (px, py, c)))
        return sends, recvs

    return plan, 3 * len(kinds)


def _plan_gather_pass(kinds, shapes, rels=ALL_RELATIONS):
    def plan(refs, send, recv):
        x, y, c, others = _place()
        sends, recvs = [], []
        for w, (kind, ss) in enumerate(zip(kinds, shapes)):
            for i, p in enumerate(rels):
                px, py = others[p]
                got = _chunk_of(refs[w], kind, 2 * px + py, c, ss)
                coming = _chunk_of(refs[w], kind, 2 * px + py, 1 - c, ss)
                sends.append(_remote(got, got, send, recv, len(rels) * w + i, (x, y, 1 - c)))
                recvs.append(_remote(coming, coming, send, recv, len(rels) * w + i, (x, y, 1 - c)))
        return sends, recvs

    return plan, len(rels) * len(kinds)


def _plan_pair(kinds, shapes):
    nw = len(kinds)

    def plan(refs, send, recv):
        x, y, c, _ = _place()
        sends = []
        for w, (kind, ss) in enumerate(zip(kinds, shapes)):
            for k in range(4):
                sends.append(_remote(_chunk_of(refs[w], kind, k, 1 - c, ss), refs[nw + w].at[k], send, recv,
                                     4 * w + k, (x, y, 1 - c)))
        return sends, sends

    return plan, 4 * nw


def _plan_chip(nw):
    def plan(refs, send, recv):
        x, y, c, others = _place()
        sends = []
        for w in range(nw):
            for p, (px, py) in enumerate(others):
                sends.append(_remote(refs[w].at[p], refs[nw + w].at[p], send, recv, 3 * w + p, (px, py, c)))
        return sends, sends

    return plan, 3 * nw


def _plan_share(nw):
    def plan(refs, send, recv):
        x, y, c, _ = _place()
        sends = [_remote(refs[w].at[c], refs[w].at[c], send, recv, w, (x, y, 1 - c)) for w in range(nw)]
        recvs = [_remote(refs[w].at[1 - c], refs[w].at[1 - c], send, recv, w, (x, y, 1 - c)) for w in range(nw)]
        return sends, recvs

    return plan, nw


def _grad_half_spec(kind, tr, tn, nr, nn, chunk):
    if kind == "col":
        return pl.BlockSpec((tr, tn), lambda *a: (a[-1][1] * nr + a[-3], chunk(*a) * nn + a[-2]))
    return pl.BlockSpec((tr, tn), lambda *a: ((2 * chunk(*a) + a[-1][1]) * nr + a[-3], a[-2]))


def _pair_add(grad, got, kind, shard_shape, pos, name):
    r, n = shard_shape
    hr = r // 2
    tr, tn = _row_tile(hr, n), n
    nr, nn = hr // tr, n // tn
    g_spec = _grad_half_spec(kind, tr, tn, nr, nn, lambda p, i, j, pos_: pos_[2 + p])
    r_spec = pl.BlockSpec((1, tr, tn), lambda p, i, j, pos_: (pos_[2 + p], i, j))
    o_spec = pl.BlockSpec((1, tr, tn), lambda p, i, j, pos_: (p, i, j))

    def body(pos_ref, g_ref, r_ref, o_ref):
        o_ref[0] = (g_ref[...] + r_ref[0]).astype(BF16)

    return pl.pallas_call(
        body,
        grid_spec=pltpu.PrefetchScalarGridSpec(num_scalar_prefetch=1, grid=(3, nr, nn), in_specs=[g_spec, r_spec],
                                               out_specs=o_spec),
        out_shape=SDS((3, hr, n), BF16),
        compiler_params=_params(("parallel", "parallel", "parallel")), name=name)(pos, grad, got)


def _chip_add(grad, got, got16, kind, shard_shape, pos, name):
    r, n = shard_shape
    hr = r // 2
    tr, tn = _row_tile(hr, n), n
    nr, nn = hr // tr, n // tn
    g_spec = _grad_half_spec(kind, tr, tn, nr, nn, lambda i, j, pos_: pos_[0])
    r_spec = pl.BlockSpec((1, tr, tn), lambda i, j, pos_: (pos_[0], i, j))
    oth = pl.BlockSpec((3, tr, tn), lambda i, j, pos_: (0, i, j))

    def body(pos_ref, g_ref, r_ref, oth_ref, o_ref):
        own = g_ref[...] + r_ref[0]
        o_ref[0] = ((own + oth_ref[0].astype(F32)) + oth_ref[1].astype(F32)) + oth_ref[2].astype(F32)

    return pl.pallas_call(
        body,
        grid_spec=pltpu.PrefetchScalarGridSpec(num_scalar_prefetch=1, grid=(nr, nn), in_specs=[g_spec, r_spec, oth],
                                               out_specs=pl.BlockSpec((1, tr, tn), lambda i, j, pos_: (pos_[1], i, j))),
        out_shape=SDS((2, hr, n), F32), compiler_params=_params(("parallel", "parallel")),
        name=name)(pos, grad, got, got16)


def _adamw_math(w, g, m, v):
    m = ADAM_B1 * m + (1.0 - ADAM_B1) * g
    v = ADAM_B2 * v + (1.0 - ADAM_B2) * (g * g)
    m_hat = m / (1.0 - ADAM_B1 ** ADAM_STEP)
    v_hat = v / (1.0 - ADAM_B2 ** ADAM_STEP)
    return -ADAM_LR * (m_hat / (jnp.sqrt(v_hat) + ADAM_EPS) + ADAM_WD * w), m, v


def _adamw(w, g, m, v, name):
    r, n = w.shape
    tr, tn = _row_tile(r, n), n

    def body(w_ref, g_ref, m_ref, v_ref, d_ref, nm_ref, nv_ref, go_ref):
        gv = g_ref[...]
        d_ref[...], nm_ref[...], nv_ref[...] = _adamw_math(w_ref[...], gv, m_ref[...], v_ref[...])
        go_ref[...] = gv

    tile = pl.BlockSpec((tr, tn), lambda i, j: (i, j))
    return pl.pallas_call(
        body, grid=(r // tr, n // tn), in_specs=[tile] * 4, out_specs=[tile] * 4, out_shape=[SDS((r, n), F32)] * 4,
        compiler_params=_params(("parallel", "parallel")), name=name)(w, g, m, v)


def _small_allreduce_adamw(g, w, m, v, deps=()):
    length = g.shape[1]

    def body(*refs):
        g_ref, w_ref, m_ref, v_ref = refs[:4]
        gs_ref, d_ref, nm_ref, nv_ref, buf, send, recv = refs[4 + len(deps):]
        x, y, c = lax.axis_index("x"), lax.axis_index("y"), lax.axis_index("c")
        me = 4 * x + 2 * y + c
        buf[me] = g_ref[...]
        cps = []
        for d in range(1, 8):
            peer = (x ^ (d >> 2), y ^ ((d >> 1) & 1), c ^ (d & 1))
            cp = pltpu.make_async_remote_copy(src_ref=buf.at[me], dst_ref=buf.at[me], send_sem=send.at[d - 1],
                                              recv_sem=recv.at[d - 1], device_id=peer, device_id_type=MESH)
            cp.start()
            cps.append(cp)
        for cp in cps:
            cp.wait()
        total = buf[0]
        for d in range(1, 8):
            total = total + buf[d]
        gs_ref[...] = total
        d_ref[...], nm_ref[...], nv_ref[...] = _adamw_math(w_ref[...], total, m_ref[...], v_ref[...])

    vm = pl.BlockSpec(memory_space=pltpu.VMEM)
    return pl.pallas_call(
        body, in_specs=[vm] * 4 + [ANY] * len(deps), out_specs=[vm] * 4, out_shape=[SDS((1, length), F32)] * 4,
        scratch_shapes=[pltpu.VMEM((8, 1, length), F32), pltpu.SemaphoreType.DMA((7,)), pltpu.SemaphoreType.DMA((7,))],
        compiler_params=pltpu.CompilerParams(has_side_effects=True), name="small_allreduce_adamw")(g, w, m, v, *deps)


def kernel(x, w_in, b_gate, norm_mix, norm_ffn, hgrn_lb_logits, hgrn_out_gain, q_gain, k_gain, rel_bias, w_proj_a, w_proj_b, w_out, w_ffn_in, w_ffn_out, loss_target, m_w_in, m_b_gate, m_norm_mix, m_norm_ffn, m_hgrn_lb_logits, m_hgrn_out_gain, m_q_gain, m_k_gain, m_rel_bias, m_w_proj_a, m_w_proj_b, m_w_out, m_w_ffn_in, m_w_ffn_out, v_w_in, v_b_gate, v_norm_mix, v_norm_ffn, v_hgrn_lb_logits, v_hgrn_out_gain, v_q_gain, v_k_gain, v_rel_bias, v_w_proj_a, v_w_proj_b, v_w_out, v_w_ffn_in, v_w_ffn_out):
    t, d = x.shape[1], x.shape[2]
    d_a = hgrn_out_gain.shape[1]
    h_a = d_a // HEAD
    h_b = rel_bias.shape[1]
    d_b = h_b * HEAD
    x0 = x.reshape(t, d)
    target = loss_target.reshape(t, d)
    ax, ay = lax.axis_index("x"), lax.axis_index("y")
    pos = jnp.stack([2 * ax + ay, lax.axis_index("c"), 2 * (1 - ax) + ay, 2 * ax + 1 - ay,
                     2 * (1 - ax) + 1 - ay]).astype(jnp.int32)

    names = ["w_in", "w_proj_a", "w_proj_b", "w_out", "w_ffn_in", "w_ffn_out"]
    big = dict(zip(names, [w_in[0], w_proj_a[0], w_proj_b[0], w_out[0], w_ffn_in[0], w_ffn_out[0]]))
    big_m = dict(zip(names, [m_w_in[0], m_w_proj_a[0], m_w_proj_b[0], m_w_out[0], m_w_ffn_in[0], m_w_ffn_out[0]]))
    big_v = dict(zip(names, [v_w_in[0], v_w_proj_a[0], v_w_proj_b[0], v_w_out[0], v_w_ffn_in[0], v_w_ffn_out[0]]))
    kind = dict(zip(names, ["col", "col", "col", "row", "col", "row"]))
    shape = {nm: big[nm].shape for nm in names}

    def gather_start(tag, group, deps):
        plan, n = _plan_gather_ici([kind[g] for g in group], [shape[g] for g in group])
        fulls = [_cast_into_full(big[g], kind[g], pos, "cast_" + g) for g in group]
        send, recv, bufs, token = _start_copies("gather_ici_start_" + tag, fulls, plan, n, deps)
        return (tag, group, plan, send, recv, bufs), token

    def gather_pass(state, after, rels=ALL_RELATIONS, part=""):
        tag, group, _, send, recv, bufs = state
        kinds_, shapes_ = [kind[g] for g in group], [shape[g] for g in group]
        bufs = _wait_copies("gather_ici_wait_" + tag + part, bufs, send, recv,
                            _plan_gather_ici(kinds_, shapes_, rels)[0], after)
        plan, n = _plan_gather_pass(kinds_, shapes_, rels)
        send2, recv2, bufs, token = _start_copies("gather_pass_start_" + tag + part, bufs, plan, n, ())
        return (tag + part, group, plan, send2, recv2, bufs), token

    def gather_done(state, after):
        tag, group, plan, send, recv, bufs = state
        return _wait_copies("gather_pass_wait_" + tag, bufs, send, recv, plan, after)

    def reduce_start(tag, group, grads, deps):
        plan, n = _plan_pair([kind[g] for g in group], [shape[g] for g in group])
        lands = [lax.empty((4, shape[g][0] // 2, shape[g][1]), F32) for g in group]
        send, recv, bufs, token = _start_copies("pair_start_" + tag, list(grads) + lands, plan, n, deps)
        return dict(tag=tag, group=group, plan=plan, send=send, recv=recv, bufs=bufs), token

    def reduce_pair_done(st, after):
        tag, group, nw = st["tag"], st["group"], len(st["group"])
        bufs = _wait_copies("pair_wait_" + tag, st["bufs"], st["send"], st["recv"], st["plan"], after)
        grads, gots = bufs[:nw], bufs[nw:]
        parts = [_pair_add(g, l, kind[nm], shape[nm], pos, "pair_add_" + nm) for g, l, nm in zip(grads, gots, group)]
        lands = [lax.empty((3, shape[g][0] // 2, shape[g][1]), BF16) for g in group]
        plan, n = _plan_chip(nw)
        send, recv, bufs, token = _start_copies("chip_start_" + tag, parts + lands, plan, n, ())
        return dict(st, plan=plan, send=send, recv=recv, bufs=bufs, grads=grads, gots=gots), token

    def reduce_chip_done(st, after):
        tag, group, nw = st["tag"], st["group"], len(st["group"])
        bufs = _wait_copies("chip_wait_" + tag, st["bufs"], st["send"], st["recv"], st["plan"], after)
        finals = [_chip_add(g, l, got16, kind[nm], shape[nm], pos, "chip_add_" + nm)
                  for g, l, got16, nm in zip(st["grads"], st["gots"], bufs[nw:], group)]
        plan, n = _plan_share(nw)
        send, recv, bufs, token = _start_copies("share_start_" + tag, finals, plan, n, ())
        return dict(st, plan=plan, send=send, recv=recv, bufs=bufs), token

    g_big, upd = {}, {}

    def reduce_finish(st, after):
        bufs = _wait_copies("share_wait_" + st["tag"], st["bufs"], st["send"], st["recv"], st["plan"], after)
        for full, nm in zip(bufs, st["group"]):
            upd[nm] = _adamw(big[nm], full.reshape(shape[nm]), big_m[nm], big_v[nm], "adamw_" + nm)
            g_big[nm] = upd[nm][3]

    ga, token = gather_start("a", ["w_in"], ())
    gb, token = gather_start("b", ["w_proj_a", "w_proj_b", "w_out"], (token,))
    gc, token = gather_start("c", ["w_ffn_in"], (token,))
    gd, token = gather_start("d", ["w_ffn_out"], (token,))
    h1, r1 = _rmsnorm_fwd(x0, norm_mix, "rmsnorm_mix")
    rb = jnp.pad(rel_bias[0], ((0, 0), (0, REL_LANES - N_REL)))
    bias = _relbias_expand(rb).transpose(1, 0, 2)
    proj = _matmul_chunks(h1, big["w_in"], (0,), None, pos, "proj_in_own", own_shard=True)
    ici_a = ga
    ga, token = gather_pass(ici_a, (h1, bias, proj, token), rels=(0, 1), part="_near")
    (wg_in,) = gather_done(ga, ())
    proj = _matmul_chunks(h1, wg_in, (2, 3), proj, pos, "proj_in_near")
    ga, token = gather_pass(ici_a[:5] + ([wg_in],), (proj,), rels=(2,), part="_far")
    (wg_in,) = gather_done(ga, ())
    proj = _matmul_chunks(h1, wg_in, (4,), proj, pos, "proj_in_far")
    y_a, o_pre, states = _hgrn_fwd(proj, hgrn_lb_logits, hgrn_out_gain, h_a)
    gb, token = gather_pass(gb, (y_a,))
    col_b = 4 * d_a // HEAD
    y_b = _attn_fwd(proj, q_gain, k_gain, bias, h_b, col_b)
    wg_pa, wg_pb, wg_out = gather_done(gb, (y_b,))
    gate_off = 4 * d_a + 3 * d_b
    pa, pb, merged = _proj_merge(y_a, y_b, wg_pa, wg_pb, proj, b_gate, gate_off, deps=(token,))
    x2 = _matmul(merged, wg_out, res=x0, name="out_proj")
    gc, token = gather_pass(gc, (x2,))
    h2, r2 = _rmsnorm_fwd(x2, norm_ffn, "rmsnorm_ffn")
    (wg_fin,) = gather_done(gc, (h2,))
    ff_gate, ff_up, act = _ffn_in_swiglu(h2, wg_fin, deps=(token,))
    gd, token = gather_pass(gd, (act,))
    (wg_fout,) = gather_done(gd, ())
    dy, dy16, loss_part = _ffn_out_loss(act, wg_fout, x2, target)

    g_fout = _matmul(act, dy16, ta=True, name="dw_ffn_out")
    r_fout, token = reduce_start("fout", ["w_ffn_out"], [g_fout], ())
    dgu = _d_act_swiglu(dy16, wg_fout, ff_gate, ff_up, deps=(token,))
    r_fout, token = reduce_pair_done(r_fout, (dgu,))
    g_fin = _matmul(h2, dgu, ta=True, name="dw_ffn_in", deps=(token,))
    r_fin, token = reduce_start("fin", ["w_ffn_in"], [g_fin], ())
    dh2 = _matmul(dgu, wg_fin, tb=True, name="d_h2", deps=(token,))
    r_fout, token_a = reduce_chip_done(r_fout, (dh2,))
    r_fin, token_b = reduce_pair_done(r_fin, (dh2,))
    dx2, dx2_16, g_norm_ffn = _rmsnorm_bwd(dh2, x2, r2, norm_ffn, dy, "rmsnorm_ffn_bwd", deps=(token_a, token_b))
    dp_ab, dproj, g_bgate = _d_merged_gates(dx2_16, wg_out, proj, b_gate, pa, pb, gate_off)
    g_out = _matmul(merged, dx2_16, ta=True, name="dw_out")
    g_pa = _matmul(y_a, dp_ab, ta=True, name="dw_proj_a", b_lead=0)
    g_pb = _matmul(y_b, dp_ab, ta=True, name="dw_proj_b", b_lead=1)
    r_mid, token = reduce_start("mid", ["w_proj_a", "w_proj_b", "w_out"], [g_pa, g_pb, g_out], ())
    dy_a = _matmul(dp_ab, wg_pa, tb=True, name="d_y_a", deps=(token,), a_lead=0)
    dy_b = _matmul(dp_ab, wg_pb, tb=True, name="d_y_b", a_lead=1)
    r_mid, token_b = reduce_pair_done(r_mid, (dy_b,))
    dproj, dbias, g_qg, g_kg = _attn_bwd(dproj, proj, q_gain, k_gain, bias, dy_b, h_b, col_b, deps=(token_b,))
    r_fin, token_a = reduce_chip_done(r_fin, (dbias,))
    r_mid, token = reduce_chip_done(r_mid, (dbias,))
    dproj, g_lb, g_gain = _hgrn_bwd(dproj, proj, o_pre, states, dy_a, hgrn_lb_logits, hgrn_out_gain, h_a,
                                    deps=(token, token_a))
    g_in = _matmul(h1, dproj, ta=True, name="dw_in")
    r_in, token = reduce_start("in", ["w_in"], [g_in], ())
    g_rb = _relbias_reduce(dbias.transpose(1, 0, 2))[:, :N_REL]
    reduce_finish(r_mid, (token,))
    reduce_finish(r_fout, (token,))
    r_in, token = reduce_pair_done(r_in, (g_rb, upd["w_out"][0], upd["w_ffn_out"][0]))
    dh1 = _matmul(dproj, wg_in, tb=True, name="d_h1", deps=(token,))
    dx, _, g_norm_mix = _rmsnorm_bwd(dh1, x0, r1, norm_mix, dx2, "rmsnorm_mix_bwd")
    reduce_finish(r_fin, (dx,))
    r_in, token = reduce_chip_done(r_in, (upd["w_ffn_in"][0], upd["w_proj_a"][0], upd["w_proj_b"][0]))

    small_w = [b_gate, norm_mix, norm_ffn, hgrn_lb_logits, hgrn_out_gain, q_gain, k_gain, rel_bias]
    small_m = [m_b_gate, m_norm_mix, m_norm_ffn, m_hgrn_lb_logits, m_hgrn_out_gain, m_q_gain, m_k_gain, m_rel_bias]
    small_v = [v_b_gate, v_norm_mix, v_norm_ffn, v_hgrn_lb_logits, v_hgrn_out_gain, v_q_gain, v_k_gain, v_rel_bias]
    small_g = [g_bgate, g_norm_mix, g_norm_ffn, g_lb, g_gain, g_qg, g_kg, g_rb]
    sizes = [w.size for w in small_w]
    length = -(-(sum(sizes) + 1) // 128) * 128

    def pack(parts_):
        flat = jnp.concatenate([p.reshape(1, -1) for p in parts_], axis=1)
        return jnp.pad(flat, ((0, 0), (0, length - flat.shape[1])))

    one = jnp.ones((1, 1), F32)
    packed = _small_allreduce_adamw(pack(small_g + [loss_part]), pack(small_w + [one]), pack(small_m + [one]),
                                    pack(small_v + [one]), deps=(token,))

    def unpack(vec):
        out, at = [], 0
        for w, n in zip(small_w, sizes):
            out.append(vec[0, at:at + n].reshape(w.shape))
            at += n
        return out, vec[0, at]

    (sg, loss), (sd, _), (sm, _), (sv, _) = [unpack(p) for p in packed]
    reduce_finish(r_in, (packed[0],))

    def ordered(small, bigs):
        bigs = [bigs[nm][None] for nm in names]
        return [bigs[0]] + small + bigs[1:]

    return (loss, dx.reshape(x.shape), *ordered(sg, g_big), *ordered(sd, {nm: upd[nm][0] for nm in names}),
            *ordered(sm, {nm: upd[nm][1] for nm in names}), *ordered(sv, {nm: upd[nm][2] for nm in names}))
```

```python
import functools

import jax
import jax.numpy as jnp
from jax import lax
from jax.experimental import pallas as pl
from jax.experimental.pallas import tpu as pltpu

F32 = jnp.float32
BF16 = jnp.bfloat16
SDS = jax.ShapeDtypeStruct
MESH = pl.DeviceIdType.MESH
HIGHEST = lax.Precision.HIGHEST

CHUNK = 64
SUB = 16
HEAD = 128
N_PAST = 8
BAND = (N_PAST + 1) * CHUNK
PAD = N_PAST * CHUNK
REL_FUTURE = CHUNK - 1
REL_PAST = 2 * CHUNK - 1
N_REL = REL_FUTURE + REL_PAST + 1
REL_LANES = 256
EPS = 1e-6
MIX_HEADS = 2
MIX_UNROLL = 4
MIX_UNROLL_BWD = 4
ATT_UNROLL = 8
ATT_UNROLL_BWD = 4
EXP_CLAMP = 80.0

ADAM_LR = 0.001
ADAM_B1 = 0.9
ADAM_B2 = 0.999
ADAM_EPS = 1e-08
ADAM_WD = 0.01
ADAM_STEP = 10

VMEM_LIMIT = 56 * 1024 * 1024

HBM = pl.BlockSpec(memory_space=pltpu.HBM)
ANY = pl.BlockSpec(memory_space=pl.ANY)
SEM = pl.BlockSpec(memory_space=pltpu.SEMAPHORE)

NT = (((1,), (1,)), ((), ()))
TN = (((0,), (0,)), ((), ()))
NN = (((1,), (0,)), ((), ()))


def _params(sem=None, **kw):
    return pltpu.CompilerParams(dimension_semantics=sem, vmem_limit_bytes=VMEM_LIMIT, **kw)


def _tile(n, pref, unit=128):
    if n <= pref:
        return n
    t = pref - pref % unit
    while n % t:
        t -= unit
    return t


STREAM_BLOCK = 256 * 1408


def _row_tile(rows, cols):
    return _tile(rows, max(16, STREAM_BLOCK // cols), 16)


def _sigmoid(x):
    return 1.0 / (1.0 + jnp.exp(-x))


def _dsilu(x, s):
    return s * (1.0 + x * (1.0 - s))


def _split(a):
    hi = a.astype(BF16)
    return hi, (a - hi.astype(F32)).astype(BF16)


def _dot3(a, b, dims):
    dot = lambda u, v: lax.dot_general(u, v, dims, preferred_element_type=F32)
    return dot(a[0], b[1]) + dot(a[1], b[0]) + dot(a[0], b[0])


def _fdot(a, b):
    return lax.dot_general(a, b, NN, precision=HIGHEST, preferred_element_type=F32)


MM_TILE_K = 5632
MM_TILE_N = 512


def _matmul_chunks(h, w, which, prev, pos, name, own_shard=False, deps=()):
    t, d = h.shape
    nc_ = w.shape[1] if own_shard else w.shape[1] // 4
    tm, tn = _tile(t, 1024), _tile(nc_, 1408)
    nn = nc_ // tn

    def chunk(q, p):
        sel = p[which[0]]
        for i in range(1, len(which)):
            sel = jnp.where(q == i, p[which[i]], sel)
        return sel

    def body(p_ref, h_ref, w_ref, *rest):
        rest[-1][...] = jnp.dot(h_ref[...], w_ref[...].astype(BF16), preferred_element_type=F32)

    if own_shard:
        w_spec = pl.BlockSpec((d, tn), lambda q, i, j, p: (0, j))
    else:
        w_spec = pl.BlockSpec((d, tn), lambda q, i, j, p: (0, chunk(q, p) * nn + j))
    n_extra = len(deps) + (prev is not None)
    return pl.pallas_call(
        body,
        grid_spec=pltpu.PrefetchScalarGridSpec(
            num_scalar_prefetch=1, grid=(len(which), t // tm, nn),
            in_specs=[pl.BlockSpec((tm, d), lambda q, i, j, p: (i, 0)), w_spec] + [ANY] * n_extra,
            out_specs=pl.BlockSpec((tm, tn), lambda q, i, j, p: (i, chunk(q, p) * nn + j))),
        out_shape=SDS((t, 4 * nc_), F32), input_output_aliases={3 + len(deps): 0} if prev is not None else {},
        compiler_params=_params(("arbitrary", "arbitrary", "arbitrary")),
        name=name)(pos, h, w, *deps, *(() if prev is None else (prev,)))


def _matmul(a, b, *, ta=False, tb=False, res=None, out_dtype=F32, name, deps=(), a_lead=None, b_lead=None,
            also_bf16=False):
    a2, b2 = a.shape[-2:], b.shape[-2:]
    m, k = (a2[1], a2[0]) if ta else a2
    n = b2[0] if tb else b2[1]
    if k > MM_TILE_K:
        tk, tm, tn = _tile(k, MM_TILE_K // 2), _tile(m, 1024), _tile(n, 1024)
    else:
        tk = k
        tm, tn = _tile(m, 2048 if tk <= MM_TILE_K // 2 else 1024), _tile(n, MM_TILE_N)
    nk = k // tk
    dims = ((((0,) if ta else (1,)), ((1,) if tb else (0,))), ((), ()))

    def body(*refs):
        n_in = 2 + (res is not None)
        a_ref, b_ref = refs[:2]
        r_ref = refs[2] if res is not None else None
        o_ref = refs[n_in + len(deps)]
        part = lax.dot_general(a_ref[...].astype(BF16), b_ref[...].astype(BF16), dims, preferred_element_type=F32)

        def finish(out):
            if r_ref is not None:
                out = out + r_ref[...]
            o_ref[...] = out.astype(o_ref.dtype)
            if also_bf16:
                refs[n_in + len(deps) + 1][...] = out.astype(BF16)

        if nk == 1:
            finish(part)
            return
        acc_ref = refs[-1]
        kk = pl.program_id(2)

        @pl.when(kk == 0)
        def _():
            acc_ref[...] = part

        @pl.when(jnp.logical_and(kk > 0, kk < nk - 1))
        def _():
            acc_ref[...] += part

        @pl.when(kk == nk - 1)
        def _():
            finish(acc_ref[...] + part)

    def spec(block, index, lead):
        if lead is None:
            return pl.BlockSpec(block, index)
        return pl.BlockSpec((None,) + block, lambda i, j, l: (lead,) + index(i, j, l))

    a_spec = spec((tk, tm), lambda i, j, l: (l, i), a_lead) if ta else spec((tm, tk), lambda i, j, l: (i, l), a_lead)
    b_spec = spec((tn, tk), lambda i, j, l: (j, l), b_lead) if tb else spec((tk, tn), lambda i, j, l: (l, j), b_lead)
    o_spec = pl.BlockSpec((tm, tn), lambda i, j, l: (i, j))
    in_specs = [a_spec, b_spec] + ([o_spec] if res is not None else []) + [ANY] * len(deps)
    args = (a, b) + ((res,) if res is not None else ()) + tuple(deps)
    out_specs, out_shape = o_spec, SDS((m, n), out_dtype)
    if also_bf16:
        out_specs, out_shape = [o_spec, o_spec], [out_shape, SDS((m, n), BF16)]
    return pl.pallas_call(
        body, grid=(m // tm, n // tn, nk), in_specs=in_specs, out_specs=out_specs,
        out_shape=out_shape, scratch_shapes=[pltpu.VMEM((tm, tn), F32)] if nk > 1 else [],
        compiler_params=_params(("parallel", "parallel", "arbitrary")), name=name)(*args)


def _cast_into_full(w, kind, pos, name):
    r, n = w.shape
    tr = _tile(r, 512, 16)
    nr = r // tr
    if kind == "col":
        shape, o_spec = (r, 4 * n), pl.BlockSpec((tr, n), lambda i, p: (i, p[0]))
    else:
        shape, o_spec = (4 * r, n), pl.BlockSpec((tr, n), lambda i, p: (p[0] * nr + i, 0))

    def body(p_ref, w_ref, o_ref):
        o_ref[...] = w_ref[...].astype(BF16)

    return pl.pallas_call(
        body,
        grid_spec=pltpu.PrefetchScalarGridSpec(num_scalar_prefetch=1, grid=(nr,),
                                               in_specs=[pl.BlockSpec((tr, n), lambda i, p: (i, 0))], out_specs=o_spec),
        out_shape=SDS(shape, BF16), compiler_params=_params(("parallel",)), name=name)(pos, w)


def _rmsnorm_fwd(x, gain, name):
    t, d = x.shape
    tm = _tile(t, 256)

    def body(x_ref, g_ref, h_ref, r_ref):
        xv = x_ref[...]
        r = lax.rsqrt(jnp.mean(xv * xv, axis=-1, keepdims=True) + EPS)
        h_ref[...] = (xv * r * g_ref[...]).astype(BF16)
        r_ref[...] = r

    return pl.pallas_call(
        body, grid=(t // tm,),
        in_specs=[pl.BlockSpec((tm, d), lambda i: (i, 0)), pl.BlockSpec((1, d), lambda i: (0, 0))],
        out_specs=[pl.BlockSpec((tm, d), lambda i: (i, 0)), pl.BlockSpec((tm, 1), lambda i: (i, 0))],
        out_shape=[SDS((t, d), BF16), SDS((t, 1), F32)], compiler_params=_params(("parallel",)), name=name)(x, gain)


def _rmsnorm_bwd(dh, x, r, gain, dres, name, deps=()):
    t, d = x.shape
    tm = _tile(t, 256)

    def body(dh_ref, x_ref, r_ref, g_ref, dres_ref, *rest):
        dx_ref, dxb_ref, dg_ref = rest[len(deps):]

        @pl.when(pl.program_id(0) == 0)
        def _():
            dg_ref[...] = jnp.zeros_like(dg_ref)

        dhv, xv, rv = dh_ref[...], x_ref[...], r_ref[...]
        dg_ref[...] += jnp.sum(dhv * (xv * rv), axis=0, keepdims=True)
        u = dhv * g_ref[...]
        dx = dres_ref[...] + rv * u - xv * (rv * rv * rv) * jnp.mean(u * xv, axis=-1, keepdims=True)
        dx_ref[...] = dx
        dxb_ref[...] = dx.astype(BF16)

    row = pl.BlockSpec((tm, d), lambda i: (i, 0))
    vec = pl.BlockSpec((1, d), lambda i: (0, 0))
    return pl.pallas_call(
        body, grid=(t // tm,),
        in_specs=[row, row, pl.BlockSpec((tm, 1), lambda i: (i, 0)), vec, row] + [ANY] * len(deps),
        out_specs=[row, row, vec], out_shape=[SDS((t, d), F32), SDS((t, d), BF16), SDS((1, d), F32)],
        compiler_params=_params(("arbitrary",)), name=name)(dh, x, r, gain, dres, *deps)


def _proj_merge(y_a, y_b, w_a, w_b, proj, b_gate, off, deps=()):
    t, ka = y_a.shape
    kb = y_b.shape[1]
    d = w_a.shape[1]
    tm, tc = _tile(t, 1024), _tile(d, MM_TILE_N)
    nj = d // tc
    oa, ob = off // tc, off // tc + nj

    def body(ya_ref, yb_ref, wa_ref, wb_ref, la_ref, lb_ref, ba_ref, bb_ref, *rest):
        pa_ref, pb_ref, o_ref = rest[len(deps):]
        pa = jnp.dot(ya_ref[...], wa_ref[...], preferred_element_type=F32)
        pb = jnp.dot(yb_ref[...], wb_ref[...], preferred_element_type=F32)
        pa_ref[...] = pa
        pb_ref[...] = pb
        ga = _sigmoid(la_ref[...] + ba_ref[...])
        gb = _sigmoid(lb_ref[...] + bb_ref[...])
        o_ref[...] = (ga * pa + gb * pb).astype(BF16)

    tile = pl.BlockSpec((tm, tc), lambda i, j: (i, j))
    return pl.pallas_call(
        body, grid=(t // tm, nj),
        in_specs=[pl.BlockSpec((tm, ka), lambda i, j: (i, 0)), pl.BlockSpec((tm, kb), lambda i, j: (i, 0)),
                  pl.BlockSpec((ka, tc), lambda i, j: (0, j)), pl.BlockSpec((kb, tc), lambda i, j: (0, j)),
                  pl.BlockSpec((tm, tc), lambda i, j: (i, oa + j)), pl.BlockSpec((tm, tc), lambda i, j: (i, ob + j)),
                  pl.BlockSpec((1, tc), lambda i, j: (0, j)), pl.BlockSpec((1, tc), lambda i, j: (0, nj + j))]
        + [ANY] * len(deps),
        out_specs=[tile, tile, tile], out_shape=[SDS((t, d), F32), SDS((t, d), F32), SDS((t, d), BF16)],
        compiler_params=_params(("parallel", "parallel")),
        name="proj_merge")(y_a, y_b, w_a, w_b, proj, proj, b_gate, b_gate, *deps)


def _d_merged_gates(dx, w, proj, b_gate, pa, pb, off, deps=()):
    t, k = dx.shape
    d = w.shape[0]
    tm, tc = _tile(t, 1024), _tile(d, MM_TILE_N)
    nj, ni = d // tc, t // tm
    o0 = off // tc

    def body(dx_ref, w_ref, la_ref, lb_ref, ba_ref, bb_ref, pa_ref, pb_ref, *rest):
        dp_ref, dproj_ref, db_ref, stage, sems = rest[len(deps):]
        j, i = pl.program_id(0), pl.program_id(1)
        step = j * ni + i
        slot = step % 2

        def copies(s, ii, jj):
            rows = pl.ds(pl.multiple_of(ii * tm, tm), tm)
            return [pltpu.make_async_copy(
                stage.at[s, br], dproj_ref.at[rows, pl.ds(pl.multiple_of(off + br * d + jj * tc, 128), tc)],
                sems.at[s, br]) for br in range(2)]

        @pl.when(step >= 2)
        def _():
            for cp in copies(slot, 0, 0):
                cp.wait()

        dm = lax.dot_general(dx_ref[...], w_ref[...], NT, preferred_element_type=F32)

        @pl.when(i == 0)
        def _():
            db_ref[...] = jnp.zeros_like(db_ref)

        for br, (l_ref, b_ref, p_ref) in enumerate(((la_ref, ba_ref, pa_ref), (lb_ref, bb_ref, pb_ref))):
            g = _sigmoid(l_ref[...] + b_ref[...])
            dp_ref[br] = (dm * g).astype(BF16)
            dl = dm * p_ref[...] * g * (1.0 - g)
            stage[slot, br] = dl.astype(BF16)
            db_ref[br] += jnp.sum(dl, axis=0, keepdims=True)
        for cp in copies(slot, i, j):
            cp.start()

        @pl.when(step == ni * nj - 1)
        def _():
            for s in range(min(2, ni * nj)):
                for cp in copies(s, 0, 0):
                    cp.wait()

    tile = pl.BlockSpec((tm, tc), lambda j, i: (i, j))
    return pl.pallas_call(
        body, grid=(nj, ni),
        in_specs=[pl.BlockSpec((tm, k), lambda j, i: (i, 0)), pl.BlockSpec((tc, k), lambda j, i: (j, 0)),
                  pl.BlockSpec((tm, tc), lambda j, i: (i, o0 + j)), pl.BlockSpec((tm, tc), lambda j, i: (i, o0 + nj + j)),
                  pl.BlockSpec((1, tc), lambda j, i: (0, j)), pl.BlockSpec((1, tc), lambda j, i: (0, nj + j)),
                  tile, tile] + [ANY] * len(deps),
        out_specs=[pl.BlockSpec((2, tm, tc), lambda j, i: (0, i, j)), ANY, pl.BlockSpec((2, 1, tc), lambda j, i: (0, 0, j))],
        out_shape=[SDS((2, t, d), BF16), SDS(proj.shape, BF16), SDS((2, 1, d), F32)],
        scratch_shapes=[pltpu.VMEM((2, 2, tm, tc), BF16), pltpu.SemaphoreType.DMA((2, 2))],
        compiler_params=_params(("arbitrary", "arbitrary")),
        name="d_merged_gates")(dx, w, proj, proj, b_gate, b_gate, pa, pb, *deps)


def _ffn_in_swiglu(h, w, deps=()):
    t, d = h.shape
    f = w.shape[1] // 2
    tm, tn = _tile(t, 2048), _tile(f, MM_TILE_N)
    nj = f // tn

    def body(h_ref, wg_ref, wu_ref, *rest):
        g_ref, u_ref, a_ref = rest[len(deps):]
        hv = h_ref[...]
        g = jnp.dot(hv, wg_ref[...], preferred_element_type=F32)
        u = jnp.dot(hv, wu_ref[...], preferred_element_type=F32)
        g_ref[...] = g
        u_ref[...] = u
        a_ref[...] = (g * _sigmoid(g) * u).astype(BF16)

    tile = pl.BlockSpec((tm, tn), lambda i, j: (i, j))
    return pl.pallas_call(
        body, grid=(t // tm, nj),
        in_specs=[pl.BlockSpec((tm, d), lambda i, j: (i, 0)), pl.BlockSpec((d, tn), lambda i, j: (0, j)),
                  pl.BlockSpec((d, tn), lambda i, j: (0, nj + j))] + [ANY] * len(deps),
        out_specs=[tile, tile, tile], out_shape=[SDS((t, f), F32), SDS((t, f), F32), SDS((t, f), BF16)],
        compiler_params=_params(("parallel", "parallel")), name="ffn_in_swiglu")(h, w, w, *deps)


def _d_act_swiglu(dy, w, gate, up, deps=()):
    t, k = dy.shape
    f = w.shape[0]
    tm, tn = _tile(t, 1024), _tile(f, MM_TILE_N)
    ni, nj = t // tm, f // tn

    def body(dy_ref, w_ref, g_ref, u_ref, *rest):
        out_ref, stage, sems = rest[len(deps):]
        i, j = pl.program_id(0), pl.program_id(1)
        step = i * nj + j
        slot = step % 2

        def copies(s, ii, jj):
            rows = pl.ds(pl.multiple_of(ii * tm, tm), tm)
            return [pltpu.make_async_copy(
                stage.at[s, half], out_ref.at[rows, pl.ds(pl.multiple_of(half * f + jj * tn, 128), tn)],
                sems.at[s, half]) for half in range(2)]

        @pl.when(step >= 2)
        def _():
            for cp in copies(slot, 0, 0):
                cp.wait()

        dact = lax.dot_general(dy_ref[...], w_ref[...], NT, preferred_element_type=F32)
        g = g_ref[...]
        sg = _sigmoid(g)
        stage[slot, 0] = (dact * u_ref[...] * _dsilu(g, sg)).astype(BF16)
        stage[slot, 1] = (dact * (g * sg)).astype(BF16)
        for cp in copies(slot, i, j):
            cp.start()

        @pl.when(step == ni * nj - 1)
        def _():
            for s in range(min(2, ni * nj)):
                for cp in copies(s, 0, 0):
                    cp.wait()

    tile = pl.BlockSpec((tm, tn), lambda i, j: (i, j))
    return pl.pallas_call(
        body, grid=(ni, nj),
        in_specs=[pl.BlockSpec((tm, k), lambda i, j: (i, 0)), pl.BlockSpec((tn, k), lambda i, j: (j, 0)), tile, tile]
        + [ANY] * len(deps),
        out_specs=ANY, out_shape=SDS((t, 2 * f), BF16),
        scratch_shapes=[pltpu.VMEM((2, 2, tm, tn), BF16), pltpu.SemaphoreType.DMA((2, 2))],
        compiler_params=_params(("arbitrary", "arbitrary")), name="d_act_swiglu")(dy, w, gate, up, *deps)


def _ffn_out_loss(act, w, x_res, target):
    t, d = x_res.shape
    k = act.shape[1]
    tm, tn = _tile(t, 1024), _tile(d, MM_TILE_N)

    def body(a_ref, w_ref, r_ref, t_ref, dy_ref, dyb_ref, l_ref):
        @pl.when(jnp.logical_and(pl.program_id(0) == 0, pl.program_id(1) == 0))
        def _():
            l_ref[...] = jnp.zeros_like(l_ref)

        y = jnp.dot(a_ref[...], w_ref[...], preferred_element_type=F32) + r_ref[...]
        e = y - t_ref[...]
        dy = e * (1.0 / d)
        dy_ref[...] = dy
        dyb_ref[...] = dy.astype(BF16)
        l_ref[...] += (0.5 / d) * jnp.sum(jnp.sum(e * e, axis=-1, keepdims=True), axis=0, keepdims=True)

    tile = pl.BlockSpec((tm, tn), lambda i, j: (i, j))
    return pl.pallas_call(
        body, grid=(t // tm, d // tn),
        in_specs=[pl.BlockSpec((tm, k), lambda i, j: (i, 0)), pl.BlockSpec((k, tn), lambda i, j: (0, j)), tile, tile],
        out_specs=[tile, tile, pl.BlockSpec((1, 1), lambda i, j: (0, 0))],
        out_shape=[SDS((t, d), F32), SDS((t, d), BF16), SDS((1, 1), F32)],
        compiler_params=_params(("arbitrary", "arbitrary")), name="ffn_out_loss")(act, w, x_res, target)


def _rel_onehot(qi):
    p = lax.broadcasted_iota(jnp.int32, (REL_LANES, BAND), 1)
    r = lax.broadcasted_iota(jnp.int32, (REL_LANES, BAND), 0)
    idx = jnp.clip(qi + PAD - p, -REL_FUTURE, REL_PAST) + REL_FUTURE
    return (idx == r).astype(F32)


def _relbias_expand(rb):
    h = rb.shape[0]

    def body(rb_ref, o_ref):
        def step(qi, _):
            o_ref[qi] = _fdot(rb_ref[...], _rel_onehot(qi))
            return 0

        lax.fori_loop(0, CHUNK, step, 0)

    return pl.pallas_call(body, out_shape=SDS((CHUNK, h, BAND), F32), compiler_params=_params(),
                          name="relbias_expand")(rb)


def _relbias_reduce(dbias):
    h = dbias.shape[1]

    rows_per_pass = 4

    def body(db_ref, o_ref):
        def step(i, acc):
            parts = []
            for u in range(rows_per_pass):
                qi = i * rows_per_pass + u
                xv = db_ref[qi]
                hi = xv.astype(BF16)
                rest = xv - hi.astype(F32)
                mid = rest.astype(BF16)
                low = (rest - mid.astype(F32)).astype(BF16)
                parts.append(lax.dot_general(jnp.concatenate([hi, mid, low], axis=0), _rel_onehot(qi).astype(BF16), NT,
                                             preferred_element_type=F32))
            for part in parts:
                acc = acc + (part[0:h] + part[h:2 * h] + part[2 * h:3 * h])
            return acc

        o_ref[...] = lax.fori_loop(0, CHUNK // rows_per_pass, step, jnp.zeros((h, REL_LANES), F32))

    return pl.pallas_call(body, out_shape=SDS((h, REL_LANES), F32), compiler_params=_params(),
                          name="relbias_reduce")(dbias)


def _lower_bound(l_ref):
    l0, l1 = l_ref[0:1, :], l_ref[1:2, :]
    m = jnp.maximum(l0, l1)
    e0, e1 = jnp.exp(l0 - m), jnp.exp(l1 - m)
    return e0 / (e0 + e1)


def _tri(lower):
    r = lax.broadcasted_iota(jnp.int32, (CHUNK, CHUNK), 0)
    c = lax.broadcasted_iota(jnp.int32, (CHUNK, CHUNK), 1)
    return r >= c if lower else r <= c


def _hgrn_intra(qs, kk, b_s):
    rows = lax.broadcasted_iota(jnp.int32, (CHUNK, HEAD), 0)
    b = b_s[...]
    out = []
    for i in range(CHUNK // SUB):
        lo = i * SUB
        ref = jnp.zeros((1, HEAD), F32) if i == 0 else b_s[lo - 1:lo, :]
        eq = jnp.exp(b[lo:lo + SUB] - ref)
        qt = _split(qs[lo:lo + SUB] * eq)
        e = jnp.where(rows < lo + SUB, jnp.exp(jnp.minimum(ref - b, EXP_CLAMP)), 0.0)
        kt = _split(kk * e)
        out.append((eq, qt, e, kt))
    return out


def _hgrn_scores(blocks):
    tr = lax.broadcasted_iota(jnp.int32, (SUB, CHUNK), 0)
    tc = lax.broadcasted_iota(jnp.int32, (SUB, CHUNK), 1)
    return jnp.concatenate([jnp.where(tc <= tr + i * SUB, _dot3(qt, kt, NT), 0.0)
                            for i, (_, qt, _, kt) in enumerate(blocks)], axis=0)


def _hgrn_fwd(proj, lb_logits, gain, n_heads):
    t = proj.shape[0]
    nc = t // CHUNK
    da = n_heads * HEAD
    hp = MIX_HEADS
    wide = hp * HEAD

    def body(q_ref, f_ref, i_ref, g_ref, l_ref, gain_ref, y_ref, o_ref, st_ref, state, b_s):
        state[...] = jnp.zeros_like(state)
        lb_all = _lower_bound(l_ref)
        tril = _tri(True).astype(F32)

        def chunks(i, _):
            dot = functools.partial(lax.dot_general, preferred_element_type=F32)
            items = []
            for u in range(MIX_UNROLL):
                for hh in range(hp):
                    j = i * MIX_UNROLL + u
                    sl = pl.ds(pl.multiple_of(j * CHUNK, CHUNK), CHUNK)
                    cols = slice(hh * HEAD, (hh + 1) * HEAD)
                    lb = lb_all[:, cols]
                    fg = lb + (1.0 - lb) * _sigmoid(f_ref[sl, cols])
                    qv = q_ref[sl, cols]
                    gv = g_ref[sl, cols]
                    items.append(dict(hh=hh, j=j, sl=sl, cols=cols, lf=jnp.log(fg), kk=1.0 - fg, qs=qv * _sigmoid(qv),
                                      vb=i_ref[sl, cols].astype(BF16), gate=gv * _sigmoid(gv)))
            for it in items:
                it["b"] = _fdot(tril, it["lf"])
            for slot, it in enumerate(items):
                b = it["b"]
                b_s[slot] = b
                it["blocks"] = _hgrn_intra(it["qs"], it["kk"], b_s.at[slot])
                it["ebl"] = jnp.exp(b_s[slot, CHUNK - 1:CHUNK, :])
                it["qe"] = (it["qs"] * jnp.exp(b)).astype(BF16)
                it["kd"] = (it["kk"] * jnp.exp(b_s[slot, CHUNK - 1:CHUNK, :] - b)).astype(BF16)
            for it in items:
                it["a"] = _hgrn_scores(it["blocks"]).astype(BF16)
            for it in items:
                it["kv"] = dot(it["vb"], it["kd"], TN)
                it["o"] = dot(it["a"], it["vb"], NN)
            s_now = [state[hh] for hh in range(hp)]
            for it in items:
                it["s_in"] = s_now[it["hh"]]
                s_now[it["hh"]] = it["s_in"] * it["ebl"] + it["kv"]
            for hh in range(hp):
                state[hh] = s_now[hh]
            for it in items:
                it["o"] = it["o"] + dot(it["qe"], it["s_in"].astype(BF16), NT)
            for it in items:
                o, sl, cols = it["o"], it["sl"], it["cols"]
                st_ref[it["hh"], it["j"]] = it["s_in"]
                o_ref[sl, cols] = o
                rr = lax.rsqrt(jnp.mean(o * o, axis=-1, keepdims=True) + EPS)
                y_ref[sl, cols] = (o * rr * gain_ref[:, cols] * it["gate"]).astype(BF16)
            return 0

        assert nc % MIX_UNROLL == 0, (nc, MIX_UNROLL)
        lax.fori_loop(0, nc // MIX_UNROLL, chunks, 0)

    col = lambda k: pl.BlockSpec((t, wide), lambda h: (0, k * (n_heads // hp) + h))
    vec = pl.BlockSpec((1, wide), lambda h: (0, h))
    return pl.pallas_call(
        body, grid=(n_heads // hp,),
        in_specs=[col(0), col(1), col(2), col(3), pl.BlockSpec((2, wide), lambda h: (0, h)), vec],
        out_specs=[pl.BlockSpec((t, wide), lambda h: (0, h)), pl.BlockSpec((t, wide), lambda h: (0, h)),
                   pl.BlockSpec((hp, nc, HEAD, HEAD), lambda h: (h, 0, 0, 0))],
        out_shape=[SDS((t, da), BF16), SDS((t, da), F32), SDS((n_heads, nc, HEAD, HEAD), F32)],
        scratch_shapes=[pltpu.VMEM((hp, HEAD, HEAD), F32), pltpu.VMEM((hp * MIX_UNROLL, CHUNK, HEAD), F32)],
        compiler_params=_params(("parallel",)), name="hgrn_fwd")(proj, proj, proj, proj, lb_logits, gain)


def _write_column_groups(res, dproj_ref, sems, col0, stride, h, width):
    copies = [pltpu.make_async_copy(
        res.at[p], dproj_ref.at[:, pl.ds(pl.multiple_of((col0 + p * stride + h) * width, HEAD), width)], sems.at[p])
        for p in range(res.shape[0])]
    for cp in copies:
        cp.start()
    for cp in copies:
        cp.wait()


def _hgrn_bwd(dproj, proj, o_pre, states, dy, lb_logits, gain, n_heads, deps=()):
    t = proj.shape[0]
    nc = t // CHUNK
    da = n_heads * HEAD
    hp = MIX_HEADS
    wide = hp * HEAD

    def body(*refs):
        (q_ref, f_ref, i_ref, g_ref, o_ref, st_ref, dy_ref, l_ref, gain_ref,
         dproj_ref, dl_ref, dgain_ref, res, dstate, b_s, out_sems) = refs[1 + len(deps):]

        def compute():
            dstate[...] = jnp.zeros_like(dstate)
            lb_all = _lower_bound(l_ref)
            tril_m, tril, triu = _tri(True), _tri(True).astype(F32), _tri(False).astype(F32)
            last = lax.broadcasted_iota(jnp.int32, (CHUNK, HEAD), 0) == CHUNK - 1

            def chunks(i, carry):
                dot = functools.partial(lax.dot_general, preferred_element_type=F32)
                items = []
                for u in range(MIX_UNROLL_BWD):
                    for hh in range(hp):
                        j = nc - 1 - (i * MIX_UNROLL_BWD + u)
                        sl = pl.ds(pl.multiple_of(j * CHUNK, CHUNK), CHUNK)
                        cols = slice(hh * HEAD, (hh + 1) * HEAD)
                        lb, gain_v = lb_all[:, cols], gain_ref[:, cols]
                        sg = _sigmoid(f_ref[sl, cols])
                        fg = lb + (1.0 - lb) * sg
                        qv = q_ref[sl, cols]
                        sq = _sigmoid(qv)
                        gv = g_ref[sl, cols]
                        sgg = _sigmoid(gv)
                        silg = gv * sgg
                        o = o_ref[sl, cols]
                        dyv = dy_ref[sl, cols]
                        rr = lax.rsqrt(jnp.mean(o * o, axis=-1, keepdims=True) + EPS)
                        on = o * rr
                        don = dyv * gain_v * silg
                        do = (rr * don - o * (rr * rr * rr) * jnp.mean(don * o, axis=-1, keepdims=True)).astype(BF16)
                        items.append(dict(
                            hh=hh, j=j, sl=sl, cols=cols, lb=lb, sg=sg, fg=fg, kk=1.0 - fg, qv=qv, sq=sq, qs=qv * sq,
                            vb=i_ref[sl, cols].astype(BF16), do=do, dg=dyv * on * gain_v * _dsilu(gv, sgg),
                            dgain=jnp.sum(dyv * on * silg, axis=0, keepdims=True)))
                for it in items:
                    it["b"] = _fdot(tril, jnp.log(it["fg"]))
                for slot, it in enumerate(items):
                    b = it["b"]
                    b_s[slot] = b
                    it["blocks"] = _hgrn_intra(it["qs"], it["kk"], b_s.at[slot])
                    bl = b_s[slot, CHUNK - 1:CHUNK, :]
                    it["eb"], it["ebl"], it["ekd"] = jnp.exp(b), jnp.exp(bl), jnp.exp(bl - b)
                    it["s_in"] = st_ref[it["hh"], it["j"]]
                for it in items:
                    it["a"] = _hgrn_scores(it["blocks"]).astype(BF16)
                    it["da"] = jnp.where(tril_m, dot(it["do"], it["vb"], NT), 0.0)
                for it in items:
                    dq_rows = []
                    dk = jnp.zeros((CHUNK, HEAD), F32)
                    for blk, (eq, qt, e, kt) in enumerate(it["blocks"]):
                        da_i = _split(it["da"][blk * SUB:(blk + 1) * SUB])
                        dq_rows.append(eq * _dot3(da_i, kt, NN))
                        dk = dk + e * _dot3(da_i, qt, TN)
                    it["dq"] = jnp.concatenate(dq_rows, axis=0) + dot(it["do"], it["s_in"].astype(BF16), NN) * it["eb"]
                    it["dk"] = dk
                    it["dv"] = dot(it["a"], it["do"], TN)
                    it["g"] = dot(it["do"], (it["qs"] * it["eb"]).astype(BF16), TN)
                ds_now = [dstate[hh] for hh in range(hp)]
                for it in items:
                    it["ds_out"] = ds_now[it["hh"]]
                    ds_now[it["hh"]] = it["ds_out"] * it["ebl"] + it["g"]
                for hh in range(hp):
                    dstate[hh] = ds_now[hh]
                for it in items:
                    dsb = it["ds_out"].astype(BF16)
                    it["dv"] = it["dv"] + dot((it["kk"] * it["ekd"]).astype(BF16), dsb, NT)
                    it["dk_state"] = it["ekd"] * dot(it["vb"], dsb, NN)
                for it in items:
                    kk, dk_state = it["kk"], it["dk_state"]
                    it["dk"] = it["dk"] + dk_state
                    extra = (jnp.sum(kk * dk_state, axis=0, keepdims=True)
                             + it["ebl"] * jnp.sum(it["s_in"] * it["ds_out"], axis=0, keepdims=True))
                    it["db"] = it["qs"] * it["dq"] - kk * it["dk"] + jnp.where(last, extra, 0.0)
                for it in items:
                    it["dlf"] = _fdot(triu, it["db"])
                carry = list(carry)
                for it in items:
                    hh, sl, cols, sg, lb = it["hh"], it["sl"], it["cols"], it["sg"], it["lb"]
                    dfg = it["dlf"] / it["fg"] - it["dk"]
                    dlb_acc, dgain_acc = carry[hh]
                    carry[hh] = (dlb_acc + jnp.sum(dfg * (1.0 - sg), axis=0, keepdims=True), dgain_acc + it["dgain"])
                    res[0, sl, cols] = (it["dq"] * _dsilu(it["qv"], it["sq"])).astype(BF16)
                    res[1, sl, cols] = (dfg * (1.0 - lb) * sg * (1.0 - sg)).astype(BF16)
                    res[2, sl, cols] = it["dv"].astype(BF16)
                    res[3, sl, cols] = it["dg"].astype(BF16)
                return tuple(carry)

            assert nc % MIX_UNROLL_BWD == 0, (nc, MIX_UNROLL_BWD)
            zero = jnp.zeros((1, HEAD), F32)
            sums = lax.fori_loop(0, nc // MIX_UNROLL_BWD, chunks, ((zero, zero),) * hp)
            for hh, (dlb, dgain) in enumerate(sums):
                cols = slice(hh * HEAD, (hh + 1) * HEAD)
                lb = lb_all[:, cols]
                dgain_ref[:, cols] = dgain
                dl0 = dlb * lb * (1.0 - lb)
                dl_ref[0:1, cols] = dl0
                dl_ref[1:2, cols] = -dl0

        compute()
        _write_column_groups(res, dproj_ref, out_sems, 0, ng, pl.program_id(0), wide)

    ng = n_heads // hp
    col = lambda k: pl.BlockSpec((t, wide), lambda h: (0, k * ng + h))
    head = pl.BlockSpec((t, wide), lambda h: (0, h))
    vec = pl.BlockSpec((1, wide), lambda h: (0, h))
    return pl.pallas_call(
        body, grid=(ng,),
        in_specs=[ANY] * (1 + len(deps)) + [col(0), col(1), col(2), col(3), head,
                  pl.BlockSpec((hp, nc, HEAD, HEAD), lambda h: (h, 0, 0, 0)),
                  head, pl.BlockSpec((2, wide), lambda h: (0, h)), vec],
        out_specs=[ANY, pl.BlockSpec((2, wide), lambda h: (0, h)), vec],
        out_shape=[SDS(dproj.shape, BF16), SDS((2, da), F32), SDS((1, da), F32)],
        scratch_shapes=[pltpu.VMEM((4, t, wide), BF16), pltpu.VMEM((hp, HEAD, HEAD), F32),
                        pltpu.VMEM((hp * MIX_UNROLL_BWD, CHUNK, HEAD), F32), pltpu.SemaphoreType.DMA((4,))],
        input_output_aliases={0: 0}, compiler_params=_params(("arbitrary",)),
        name="hgrn_bwd")(dproj, *deps, proj, proj, proj, proj, o_pre, states, dy, lb_logits, gain)


ROWS = 256


def _head_norm(x_ref, gain, dst, dst_off, t):
    def step(i, _):
        sl = pl.ds(pl.multiple_of(i * ROWS, ROWS), ROWS)
        xv = x_ref[sl, :]
        r = lax.rsqrt(jnp.mean(xv * xv, axis=-1, keepdims=True) + EPS)
        dst[pl.ds(pl.multiple_of(dst_off + i * ROWS, ROWS), ROWS), :] = (xv * r * gain).astype(BF16)
        return 0

    lax.fori_loop(0, t // ROWS, step, 0)


def _head_norm_bwd(x_ref, gain, dn_ref, dn_off, out, slot, t):
    def step(i, acc):
        sl = pl.ds(pl.multiple_of(i * ROWS, ROWS), ROWS)
        xv = x_ref[sl, :]
        dn = dn_ref[pl.ds(pl.multiple_of(dn_off + i * ROWS, ROWS), ROWS), :]
        r = lax.rsqrt(jnp.mean(xv * xv, axis=-1, keepdims=True) + EPS)
        u = dn * gain
        out[slot, sl, :] = (r * u - xv * (r * r * r) * jnp.mean(u * xv, axis=-1, keepdims=True)).astype(out.dtype)
        return acc + jnp.sum(dn * (xv * r), axis=0, keepdims=True)

    return lax.fori_loop(0, t // ROWS, step, jnp.zeros((1, HEAD), F32))


def _attn_scores(qn, kpad, n):
    qc = qn[pl.ds(pl.multiple_of(n * CHUNK, CHUNK), CHUNK), :]
    band = pl.ds(pl.multiple_of(n * CHUNK, CHUNK), BAND)
    return qc, band, lax.dot_general(qc, kpad[band, :], NT, preferred_element_type=F32)


def _attn_softmax(raw, bias_ref, n):
    s = raw * (HEAD ** -0.5) + bias_ref[0]
    col = lax.broadcasted_iota(jnp.int32, (CHUNK, BAND), 1)
    s = jnp.where(col >= PAD - n * CHUNK, s, -jnp.inf)
    p = jnp.exp(s - jnp.max(s, axis=-1, keepdims=True))
    return p / jnp.sum(p, axis=-1, keepdims=True)


def _attn_fwd(proj, q_gain, k_gain, bias, n_heads, col0):
    t = proj.shape[0]
    nc = t // CHUNK

    def body(q_ref, k_ref, v_ref, qg_ref, kg_ref, bias_ref, y_ref, qn, kpad, vpad):
        kpad[0:PAD, :] = jnp.zeros((PAD, HEAD), BF16)
        vpad[0:PAD, :] = jnp.zeros((PAD, HEAD), BF16)
        _head_norm(q_ref, qg_ref[...], qn, 0, t)
        _head_norm(k_ref, kg_ref[...], kpad, PAD, t)

        def copy_v(i, _):
            vpad[pl.ds(pl.multiple_of(PAD + i * ROWS, ROWS), ROWS), :] = v_ref[
                pl.ds(pl.multiple_of(i * ROWS, ROWS), ROWS), :].astype(BF16)
            return 0

        lax.fori_loop(0, t // ROWS, copy_v, 0)

        def chunks(i, _):
            ns = [i * ATT_UNROLL + u for u in range(ATT_UNROLL)]
            scored = [_attn_scores(qn, kpad, n) for n in ns]
            probs = [_attn_softmax(raw, bias_ref, n).astype(BF16) for n, (_, _, raw) in zip(ns, scored)]
            outs = [lax.dot_general(p, vpad[band, :], NN, preferred_element_type=F32).astype(BF16)
                    for p, (_, band, _) in zip(probs, scored)]
            for n, o in zip(ns, outs):
                y_ref[pl.ds(pl.multiple_of(n * CHUNK, CHUNK), CHUNK), :] = o
            return 0

        assert nc % ATT_UNROLL == 0, (nc, ATT_UNROLL)
        lax.fori_loop(0, nc // ATT_UNROLL, chunks, 0)

    col = lambda k: pl.BlockSpec((t, HEAD), lambda h: (0, col0 + k * n_heads + h))
    vec = pl.BlockSpec((1, HEAD), lambda h: (0, 0))
    return pl.pallas_call(
        body, grid=(n_heads,),
        in_specs=[col(0), col(1), col(2), vec, vec, pl.BlockSpec((1, CHUNK, BAND), lambda h: (h, 0, 0))],
        out_specs=pl.BlockSpec((t, HEAD), lambda h: (0, h)), out_shape=SDS((t, n_heads * HEAD), BF16),
        scratch_shapes=[pltpu.VMEM((t, HEAD), BF16), pltpu.VMEM((t + PAD, HEAD), BF16), pltpu.VMEM((t + PAD, HEAD), BF16)],
        compiler_params=_params(("parallel",)), name="attn_fwd")(proj, proj, proj, q_gain, k_gain, bias)


def _attn_bwd(dproj, proj, q_gain, k_gain, bias, dy, n_heads, col0, deps=()):
    t = proj.shape[0]
    nc = t // CHUNK

    def body(*refs):
        (q_ref, k_ref, v_ref, qg_ref, kg_ref, bias_ref, dy_ref,
         dproj_ref, dbias_ref, dqg_ref, dkg_ref, qn, kpad, vpad, dqn, dk_acc, dv_acc, res,
         out_sems) = refs[1 + len(deps):]
        h = pl.program_id(0)

        def compute():
            kpad[0:PAD, :] = jnp.zeros((PAD, HEAD), BF16)
            vpad[0:PAD, :] = jnp.zeros((PAD, HEAD), BF16)
            _head_norm(q_ref, qg_ref[...], qn, 0, t)
            _head_norm(k_ref, kg_ref[...], kpad, PAD, t)

            def prep(i, _):
                sl = pl.ds(pl.multiple_of(PAD + i * ROWS, ROWS), ROWS)
                vpad[sl, :] = v_ref[pl.ds(pl.multiple_of(i * ROWS, ROWS), ROWS), :].astype(BF16)
                return 0

            lax.fori_loop(0, t // ROWS, prep, 0)

            def clear(i, _):
                sl = pl.ds(pl.multiple_of(i * ROWS, ROWS), ROWS)
                dk_acc[sl, :] = jnp.zeros((ROWS, HEAD), F32)
                dv_acc[sl, :] = jnp.zeros((ROWS, HEAD), F32)
                return 0

            lax.fori_loop(0, (t + PAD) // ROWS, clear, 0)
            dbias_ref[0] = jnp.zeros((CHUNK, BAND), F32)

            def chunks(i, _):
                dot = functools.partial(lax.dot_general, preferred_element_type=F32)
                ns = [i * ATT_UNROLL_BWD + u for u in range(ATT_UNROLL_BWD)]
                scored = [_attn_scores(qn, kpad, n) for n in ns]
                dos = [dy_ref[pl.ds(pl.multiple_of(n * CHUNK, CHUNK), CHUNK), :].astype(BF16) for n in ns]
                dps = [dot(do, vpad[band, :], NT) for do, (_, band, _) in zip(dos, scored)]
                ps, dss = [], []
                for n, (_, _, raw), dp in zip(ns, scored, dps):
                    p = _attn_softmax(raw, bias_ref, n)
                    ds = p * (dp - jnp.sum(dp * p, axis=-1, keepdims=True))
                    dbias_ref[0] += ds
                    ps.append(p.astype(BF16))
                    dss.append((ds * (HEAD ** -0.5)).astype(BF16))
                dqs = [dot(d, kpad[band, :], NN) for d, (_, band, _) in zip(dss, scored)]
                dks = [dot(d, qc, TN) for d, (qc, _, _) in zip(dss, scored)]
                dvs = [dot(p, do, TN) for p, do in zip(ps, dos)]
                for n, (_, band, _), dq, dk, dv in zip(ns, scored, dqs, dks, dvs):
                    dqn[pl.ds(pl.multiple_of(n * CHUNK, CHUNK), CHUNK), :] = dq
                    dk_acc[band, :] += dk
                    dv_acc[band, :] += dv
                return 0

            assert nc % ATT_UNROLL_BWD == 0, (nc, ATT_UNROLL_BWD)
            lax.fori_loop(0, nc // ATT_UNROLL_BWD, chunks, 0)
            dqg = _head_norm_bwd(q_ref, qg_ref[...], dqn, 0, res, 0, t)
            dkg = _head_norm_bwd(k_ref, kg_ref[...], dk_acc, PAD, res, 1, t)

            def put_v(i, _):
                sl = pl.ds(pl.multiple_of(i * ROWS, ROWS), ROWS)
                res[2, sl, :] = dv_acc[pl.ds(pl.multiple_of(PAD + i * ROWS, ROWS), ROWS), :].astype(BF16)
                return 0

            lax.fori_loop(0, t // ROWS, put_v, 0)

            @pl.when(h == 0)
            def _():
                dqg_ref[...] = jnp.zeros_like(dqg_ref)
                dkg_ref[...] = jnp.zeros_like(dkg_ref)

            dqg_ref[...] += dqg
            dkg_ref[...] += dkg

        compute()
        _write_column_groups(res, dproj_ref, out_sems, col0, n_heads, h, HEAD)

    col = lambda k: pl.BlockSpec((t, HEAD), lambda h: (0, col0 + k * n_heads + h))
    vec = pl.BlockSpec((1, HEAD), lambda h: (0, 0))
    btile = pl.BlockSpec((1, CHUNK, BAND), lambda h: (h, 0, 0))
    return pl.pallas_call(
        body, grid=(n_heads,),
        in_specs=[ANY] * (1 + len(deps)) + [col(0), col(1), col(2), vec, vec, btile,
                                            pl.BlockSpec((t, HEAD), lambda h: (0, h))],
        out_specs=[ANY, btile, vec, vec],
        out_shape=[SDS(dproj.shape, BF16), SDS((n_heads, CHUNK, BAND), F32), SDS((1, HEAD), F32), SDS((1, HEAD), F32)],
        scratch_shapes=[pltpu.VMEM((t, HEAD), BF16), pltpu.VMEM((t + PAD, HEAD), BF16), pltpu.VMEM((t + PAD, HEAD), BF16),
                        pltpu.VMEM((t, HEAD), F32), pltpu.VMEM((t + PAD, HEAD), F32), pltpu.VMEM((t + PAD, HEAD), F32),
                        pltpu.VMEM((3, t, HEAD), BF16), pltpu.SemaphoreType.DMA((3,))],
        input_output_aliases={0: 0}, compiler_params=_params(("arbitrary",)),
        name="attn_bwd")(dproj, *deps, proj, proj, proj, q_gain, k_gain, bias, dy)


def _place():
    x, y, c = lax.axis_index("x"), lax.axis_index("y"), lax.axis_index("c")
    others = [(1 - x, y), (x, 1 - y), (1 - x, 1 - y)]
    return x, y, c, others


def _chunk_of(ref, kind, chip, half, shard_shape):
    r, n = shard_shape
    hr = r // 2
    if kind == "col":
        rows = pl.ds(0, r) if half is None else pl.ds(half * hr, hr)
        return ref.at[rows, pl.ds(chip * n, n)]
    rows = pl.ds(chip * r, r) if half is None else pl.ds(chip * r + half * hr, hr)
    return ref.at[rows, :]


EFFECT = pltpu.SideEffectType.DATAFLOW_SIDE_EFFECTING


def _start_copies(name, bufs, plan, n, deps):
    nb, nd = len(bufs), len(deps)

    def body(*refs):
        send, recv, token = refs[nb + nd], refs[nb + nd + 1], refs[-1]
        for cp in plan(refs[:nb], send, recv)[0]:
            cp.start()
        token[...] = jnp.zeros_like(token)

    out = pl.pallas_call(
        body, name=name,
        out_shape=(pltpu.SemaphoreType.DMA((n,)), pltpu.SemaphoreType.DMA((n,)),
                   *[pltpu.HBM(b.shape, b.dtype) for b in bufs], SDS((8, 128), F32)),
        in_specs=[HBM] * nb + [ANY] * nd,
        out_specs=(SEM, SEM, *[HBM] * nb, pl.BlockSpec(memory_space=pltpu.VMEM)),
        input_output_aliases={i: 2 + i for i in range(nb)},
        compiler_params=pltpu.CompilerParams(has_side_effects=EFFECT),
    )(*[pltpu.with_memory_space_constraint(b, pltpu.HBM) for b in bufs], *deps)
    return out[0], out[1], list(out[2:2 + nb]), out[-1]


def _wait_copies(name, bufs, send, recv, plan, after):
    nb = len(bufs)

    def body(*refs):
        sends, recvs = plan(refs[:nb], refs[nb], refs[nb + 1])
        for cp in sends:
            cp.wait_send()
        for cp in recvs:
            cp.wait_recv()

    out = pl.pallas_call(
        body, name=name, out_shape=tuple(pltpu.HBM(b.shape, b.dtype) for b in bufs),
        in_specs=[HBM] * nb + [SEM, SEM] + [ANY] * len(after), out_specs=tuple([HBM] * nb),
        input_output_aliases={i: i for i in range(nb)},
        compiler_params=pltpu.CompilerParams(has_side_effects=EFFECT),
    )(*bufs, send, recv, *after)
    return list(out)


def _remote(src, dst, send, recv, i, dev):
    return pltpu.make_async_remote_copy(src_ref=src, dst_ref=dst, send_sem=send.at[i], recv_sem=recv.at[i],
                                        device_id=dev, device_id_type=MESH)


ALL_RELATIONS = (0, 1, 2)


def _plan_gather_ici(kinds, shapes, rels=ALL_RELATIONS):
    def plan(refs, send, recv):
        x, y, c, others = _place()
        sends, recvs = [], []
        for w, (kind, ss) in enumerate(zip(kinds, shapes)):
            for p in rels:
                px, py = others[p]
                mine = _chunk_of(refs[w], kind, 2 * x + y, c, ss)
                theirs = _chunk_of(refs[w], kind, 2 * px + py, c, ss)
                sends.append(_remote(mine, mine, send, recv, 3 * w + p, (px, py, c)))
                recvs.append(_remote(theirs, theirs, send, recv, 3 * w + p, (px, py, c)))
        return sends, recvs

    return plan, 3 * len(kinds)


def _plan_gather_pass(kinds, shapes, rels=ALL_RELATIONS):
    def plan(refs, send, recv):
        x, y, c, others = _place()
        sends, recvs = [], []
        for w, (kind, ss) in enumerate(zip(kinds, shapes)):
            for i, p in enumerate(rels):
                px, py = others[p]
                got = _chunk_of(refs[w], kind, 2 * px + py, c, ss)
                coming = _chunk_of(refs[w], kind, 2 * px + py, 1 - c, ss)
                sends.append(_remote(got, got, send, recv, len(rels) * w + i, (x, y, 1 - c)))
                recvs.append(_remote(coming, coming, send, recv, len(rels) * w + i, (x, y, 1 - c)))
        return sends, recvs

    return plan, len(rels) * len(kinds)


def _plan_pair(kinds, shapes):
    nw = len(kinds)

    def plan(refs, send, recv):
        x, y, c, _ = _place()
        sends = []
        for w, (kind, ss) in enumerate(zip(kinds, shapes)):
            for k in range(4):
                sends.append(_remote(_chunk_of(refs[w], kind, k, 1 - c, ss), refs[nw + w].at[k], send, recv,
                                     4 * w + k, (x, y, 1 - c)))
        return sends, sends

    return plan, 4 * nw


def _plan_chip(nw):
    def plan(refs, send, recv):
        x, y, c, others = _place()
        sends = []
        for w in range(nw):
            for p, (px, py) in enumerate(others):
                sends.append(_remote(refs[w].at[p], refs[nw + w].at[p], send, recv, 3 * w + p, (px, py, c)))
        return sends, sends

    return plan, 3 * nw


def _plan_share(nw):
    def plan(refs, send, recv):
        x, y, c, _ = _place()
        sends = [_remote(refs[w].at[c], refs[w].at[c], send, recv, w, (x, y, 1 - c)) for w in range(nw)]
        recvs = [_remote(refs[w].at[1 - c], refs[w].at[1 - c], send, recv, w, (x, y, 1 - c)) for w in range(nw)]
        return sends, recvs

    return plan, nw


def _grad_half_spec(kind, tr, tn, nr, nn, chunk):
    if kind == "col":
        return pl.BlockSpec((tr, tn), lambda *a: (a[-1][1] * nr + a[-3], chunk(*a) * nn + a[-2]))
    return pl.BlockSpec((tr, tn), lambda *a: ((2 * chunk(*a) + a[-1][1]) * nr + a[-3], a[-2]))


def _pair_add(grad, got, kind, shard_shape, pos, name):
    r, n = shard_shape
    hr = r // 2
    tr, tn = _row_tile(hr, n), n
    nr, nn = hr // tr, n // tn
    g_spec = _grad_half_spec(kind, tr, tn, nr, nn, lambda p, i, j, pos_: pos_[2 + p])
    r_spec = pl.BlockSpec((1, tr, tn), lambda p, i, j, pos_: (pos_[2 + p], i, j))
    o_spec = pl.BlockSpec((1, tr, tn), lambda p, i, j, pos_: (p, i, j))

    def body(pos_ref, g_ref, r_ref, o_ref):
        o_ref[0] = (g_ref[...] + r_ref[0]).astype(BF16)

    return pl.pallas_call(
        body,
        grid_spec=pltpu.PrefetchScalarGridSpec(num_scalar_prefetch=1, grid=(3, nr, nn), in_specs=[g_spec, r_spec],
                                               out_specs=o_spec),
        out_shape=SDS((3, hr, n), BF16),
        compiler_params=_params(("parallel", "parallel", "parallel")), name=name)(pos, grad, got)


def _chip_add(grad, got, got16, kind, shard_shape, pos, name):
    r, n = shard_shape
    hr = r // 2
    tr, tn = _row_tile(hr, n), n
    nr, nn = hr // tr, n // tn
    g_spec = _grad_half_spec(kind, tr, tn, nr, nn, lambda i, j, pos_: pos_[0])
    r_spec = pl.BlockSpec((1, tr, tn), lambda i, j, pos_: (pos_[0], i, j))
    oth = pl.BlockSpec((3, tr, tn), lambda i, j, pos_: (0, i, j))

    def body(pos_ref, g_ref, r_ref, oth_ref, o_ref):
        own = g_ref[...] + r_ref[0]
        o_ref[0] = ((own + oth_ref[0].astype(F32)) + oth_ref[1].astype(F32)) + oth_ref[2].astype(F32)

    return pl.pallas_call(
        body,
        grid_spec=pltpu.PrefetchScalarGridSpec(num_scalar_prefetch=1, grid=(nr, nn), in_specs=[g_spec, r_spec, oth],
                                               out_specs=pl.BlockSpec((1, tr, tn), lambda i, j, pos_: (pos_[1], i, j))),
        out_shape=SDS((2, hr, n), F32), compiler_params=_params(("parallel", "parallel")),
        name=name)(pos, grad, got, got16)


def _adamw_math(w, g, m, v):
    m = ADAM_B1 * m + (1.0 - ADAM_B1) * g
    v = ADAM_B2 * v + (1.0 - ADAM_B2) * (g * g)
    m_hat = m / (1.0 - ADAM_B1 ** ADAM_STEP)
    v_hat = v / (1.0 - ADAM_B2 ** ADAM_STEP)
    return -ADAM_LR * (m_hat / (jnp.sqrt(v_hat) + ADAM_EPS) + ADAM_WD * w), m, v


def _adamw(w, g, m, v, name):
    r, n = w.shape
    tr, tn = _row_tile(r, n), n

    def body(w_ref, g_ref, m_ref, v_ref, d_ref, nm_ref, nv_ref, go_ref):
        gv = g_ref[...]
        d_ref[...], nm_ref[...], nv_ref[...] = _adamw_math(w_ref[...], gv, m_ref[...], v_ref[...])
        go_ref[...] = gv

    tile = pl.BlockSpec((tr, tn), lambda i, j: (i, j))
    return pl.pallas_call(
        body, grid=(r // tr, n // tn), in_specs=[tile] * 4, out_specs=[tile] * 4, out_shape=[SDS((r, n), F32)] * 4,
        compiler_params=_params(("parallel", "parallel")), name=name)(w, g, m, v)


def _small_allreduce_adamw(g, w, m, v, deps=()):
    length = g.shape[1]

    def body(*refs):
        g_ref, w_ref, m_ref, v_ref = refs[:4]
        gs_ref, d_ref, nm_ref, nv_ref, buf, send, recv = refs[4 + len(deps):]
        x, y, c = lax.axis_index("x"), lax.axis_index("y"), lax.axis_index("c")
        me = 4 * x + 2 * y + c
        buf[me] = g_ref[...]
        cps = []
        for d in range(1, 8):
            peer = (x ^ (d >> 2), y ^ ((d >> 1) & 1), c ^ (d & 1))
            cp = pltpu.make_async_remote_copy(src_ref=buf.at[me], dst_ref=buf.at[me], send_sem=send.at[d - 1],
                                              recv_sem=recv.at[d - 1], device_id=peer, device_id_type=MESH)
            cp.start()
            cps.append(cp)
        for cp in cps:
            cp.wait()
        total = buf[0]
        for d in range(1, 8):
            total = total + buf[d]
        gs_ref[...] = total
        d_ref[...], nm_ref[...], nv_ref[...] = _adamw_math(w_ref[...], total, m_ref[...], v_ref[...])

    vm = pl.BlockSpec(memory_space=pltpu.VMEM)
    return pl.pallas_call(
        body, in_specs=[vm] * 4 + [ANY] * len(deps), out_specs=[vm] * 4, out_shape=[SDS((1, length), F32)] * 4,
        scratch_shapes=[pltpu.VMEM((8, 1, length), F32), pltpu.SemaphoreType.DMA((7,)), pltpu.SemaphoreType.DMA((7,))],
        compiler_params=pltpu.CompilerParams(has_side_effects=True), name="small_allreduce_adamw")(g, w, m, v, *deps)


def kernel(x, w_in, b_gate, norm_mix, norm_ffn, hgrn_lb_logits, hgrn_out_gain, q_gain, k_gain, rel_bias, w_proj_a, w_proj_b, w_out, w_ffn_in, w_ffn_out, loss_target, m_w_in, m_b_gate, m_norm_mix, m_norm_ffn, m_hgrn_lb_logits, m_hgrn_out_gain, m_q_gain, m_k_gain, m_rel_bias, m_w_proj_a, m_w_proj_b, m_w_out, m_w_ffn_in, m_w_ffn_out, v_w_in, v_b_gate, v_norm_mix, v_norm_ffn, v_hgrn_lb_logits, v_hgrn_out_gain, v_q_gain, v_k_gain, v_rel_bias, v_w_proj_a, v_w_proj_b, v_w_out, v_w_ffn_in, v_w_ffn_out):
    t, d = x.shape[1], x.shape[2]
    d_a = hgrn_out_gain.shape[1]
    h_a = d_a // HEAD
    h_b = rel_bias.shape[1]
    d_b = h_b * HEAD
    x0 = x.reshape(t, d)
    target = loss_target.reshape(t, d)
    ax, ay = lax.axis_index("x"), lax.axis_index("y")
    pos = jnp.stack([2 * ax + ay, lax.axis_index("c"), 2 * (1 - ax) + ay, 2 * ax + 1 - ay,
                     2 * (1 - ax) + 1 - ay]).astype(jnp.int32)

    names = ["w_in", "w_proj_a", "w_proj_b", "w_out", "w_ffn_in", "w_ffn_out"]
    big = dict(zip(names, [w_in[0], w_proj_a[0], w_proj_b[0], w_out[0], w_ffn_in[0], w_ffn_out[0]]))
    big_m = dict(zip(names, [m_w_in[0], m_w_proj_a[0], m_w_proj_b[0], m_w_out[0], m_w_ffn_in[0], m_w_ffn_out[0]]))
    big_v = dict(zip(names, [v_w_in[0], v_w_proj_a[0], v_w_proj_b[0], v_w_out[0], v_w_ffn_in[0], v_w_ffn_out[0]]))
    kind = dict(zip(names, ["col", "col", "col", "row", "col", "row"]))
    shape = {nm: big[nm].shape for nm in names}

    def gather_start(tag, group, deps):
        plan, n = _plan_gather_ici([kind[g] for g in group], [shape[g] for g in group])
        fulls = [_cast_into_full(big[g], kind[g], pos, "cast_" + g) for g in group]
        send, recv, bufs, token = _start_copies("gather_ici_start_" + tag, fulls, plan, n, deps)
        return (tag, group, plan, send, recv, bufs), token

    def gather_pass(state, after, rels=ALL_RELATIONS, part=""):
        tag, group, _, send, recv, bufs = state
        kinds_, shapes_ = [kind[g] for g in group], [shape[g] for g in group]
        bufs = _wait_copies("gather_ici_wait_" + tag + part, bufs, send, recv,
                            _plan_gather_ici(kinds_, shapes_, rels)[0], after)
        plan, n = _plan_gather_pass(kinds_, shapes_, rels)
        send2, recv2, bufs, token = _start_copies("gather_pass_start_" + tag + part, bufs, plan, n, ())
        return (tag + part, group, plan, send2, recv2, bufs), token

    def gather_done(state, after):
        tag, group, plan, send, recv, bufs = state
        return _wait_copies("gather_pass_wait_" + tag, bufs, send, recv, plan, after)

    def reduce_start(tag, group, grads, deps, sent=None):
        plan, n = _plan_pair([kind[g] for g in group], [shape[g] for g in group])
        srcs = list(grads) if sent is None else list(sent)
        lands = [lax.empty((4, shape[g][0] // 2, shape[g][1]), s.dtype) for g, s in zip(group, srcs)]
        send, recv, bufs, token = _start_copies("pair_start_" + tag, srcs + lands, plan, n, deps)
        kept = None if sent is None else list(grads)
        return dict(tag=tag, group=group, plan=plan, send=send, recv=recv, bufs=bufs, kept=kept), token

    def reduce_pair_done(st, after):
        tag, group, nw = st["tag"], st["group"], len(st["group"])
        bufs = _wait_copies("pair_wait_" + tag, st["bufs"], st["send"], st["recv"], st["plan"], after)
        grads, gots = (bufs[:nw] if st["kept"] is None else st["kept"]), bufs[nw:]
        parts = [_pair_add(g, l, kind[nm], shape[nm], pos, "pair_add_" + nm) for g, l, nm in zip(grads, gots, group)]
        lands = [lax.empty((3, shape[g][0] // 2, shape[g][1]), BF16) for g in group]
        plan, n = _plan_chip(nw)
        send, recv, bufs, token = _start_copies("chip_start_" + tag, parts + lands, plan, n, ())
        return dict(st, plan=plan, send=send, recv=recv, bufs=bufs, grads=grads, gots=gots), token

    def reduce_chip_done(st, after):
        tag, group, nw = st["tag"], st["group"], len(st["group"])
        bufs = _wait_copies("chip_wait_" + tag, st["bufs"], st["send"], st["recv"], st["plan"], after)
        finals = [_chip_add(g, l, got16, kind[nm], shape[nm], pos, "chip_add_" + nm)
                  for g, l, got16, nm in zip(st["grads"], st["gots"], bufs[nw:], group)]
        plan, n = _plan_share(nw)
        send, recv, bufs, token = _start_copies("share_start_" + tag, finals, plan, n, ())
        return dict(st, plan=plan, send=send, recv=recv, bufs=bufs), token

    g_big, upd = {}, {}

    def reduce_finish(st, after):
        bufs = _wait_copies("share_wait_" + st["tag"], st["bufs"], st["send"], st["recv"], st["plan"], after)
        for full, nm in zip(bufs, st["group"]):
            upd[nm] = _adamw(big[nm], full.reshape(shape[nm]), big_m[nm], big_v[nm], "adamw_" + nm)
            g_big[nm] = upd[nm][3]

    ga, token = gather_start("a", ["w_in"], ())
    gb, token = gather_start("b", ["w_proj_a", "w_proj_b", "w_out"], (token,))
    gc, token = gather_start("c", ["w_ffn_in"], (token,))
    gd, token = gather_start("d", ["w_ffn_out"], (token,))
    h1, r1 = _rmsnorm_fwd(x0, norm_mix, "rmsnorm_mix")
    rb = jnp.pad(rel_bias[0], ((0, 0), (0, REL_LANES - N_REL)))
    bias = _relbias_expand(rb).transpose(1, 0, 2)
    proj = _matmul_chunks(h1, big["w_in"], (0,), None, pos, "proj_in_own", own_shard=True)
    ici_a = ga
    ga, token = gather_pass(ici_a, (h1, bias, proj, token), rels=(0, 1), part="_near")
    (wg_in,) = gather_done(ga, ())
    proj = _matmul_chunks(h1, wg_in, (2, 3), proj, pos, "proj_in_near")
    ga, token = gather_pass(ici_a[:5] + ([wg_in],), (proj,), rels=(2,), part="_far")
    (wg_in,) = gather_done(ga, ())
    proj = _matmul_chunks(h1, wg_in, (4,), proj, pos, "proj_in_far")
    y_a, o_pre, states = _hgrn_fwd(proj, hgrn_lb_logits, hgrn_out_gain, h_a)
    gb, token = gather_pass(gb, (y_a,))
    col_b = 4 * d_a // HEAD
    y_b = _attn_fwd(proj, q_gain, k_gain, bias, h_b, col_b)
    wg_pa, wg_pb, wg_out = gather_done(gb, (y_b,))
    gate_off = 4 * d_a + 3 * d_b
    pa, pb, merged = _proj_merge(y_a, y_b, wg_pa, wg_pb, proj, b_gate, gate_off, deps=(token,))
    x2 = _matmul(merged, wg_out, res=x0, name="out_proj")
    gc, token = gather_pass(gc, (x2,))
    h2, r2 = _rmsnorm_fwd(x2, norm_ffn, "rmsnorm_ffn")
    (wg_fin,) = gather_done(gc, (h2,))
    ff_gate, ff_up, act = _ffn_in_swiglu(h2, wg_fin, deps=(token,))
    gd, token = gather_pass(gd, (act,))
    (wg_fout,) = gather_done(gd, ())
    dy, dy16, loss_part = _ffn_out_loss(act, wg_fout, x2, target)

    g_fout = _matmul(act, dy16, ta=True, name="dw_ffn_out")
    r_fout, token = reduce_start("fout", ["w_ffn_out"], [g_fout], ())
    dgu = _d_act_swiglu(dy16, wg_fout, ff_gate, ff_up, deps=(token,))
    r_fout, token = reduce_pair_done(r_fout, (dgu,))
    g_fin = _matmul(h2, dgu, ta=True, name="dw_ffn_in", deps=(token,))
    r_fin, token = reduce_start("fin", ["w_ffn_in"], [g_fin], ())
    dh2 = _matmul(dgu, wg_fin, tb=True, name="d_h2", deps=(token,))
    r_fout, token_a = reduce_chip_done(r_fout, (dh2,))
    r_fin, token_b = reduce_pair_done(r_fin, (dh2,))
    dx2, dx2_16, g_norm_ffn = _rmsnorm_bwd(dh2, x2, r2, norm_ffn, dy, "rmsnorm_ffn_bwd", deps=(token_a, token_b))
    dp_ab, dproj, g_bgate = _d_merged_gates(dx2_16, wg_out, proj, b_gate, pa, pb, gate_off)
    g_out = _matmul(merged, dx2_16, ta=True, name="dw_out")
    g_pa = _matmul(y_a, dp_ab, ta=True, name="dw_proj_a", b_lead=0)
    g_pb = _matmul(y_b, dp_ab, ta=True, name="dw_proj_b", b_lead=1)
    r_mid, token = reduce_start("mid", ["w_proj_a", "w_proj_b", "w_out"], [g_pa, g_pb, g_out], ())
    dy_a = _matmul(dp_ab, wg_pa, tb=True, name="d_y_a", deps=(token,), a_lead=0)
    dy_b = _matmul(dp_ab, wg_pb, tb=True, name="d_y_b", a_lead=1)
    r_mid, token_b = reduce_pair_done(r_mid, (dy_b,))
    dproj, dbias, g_qg, g_kg = _attn_bwd(dproj, proj, q_gain, k_gain, bias, dy_b, h_b, col_b, deps=(token_b,))
    r_fin, token_a = reduce_chip_done(r_fin, (dbias,))
    r_mid, token = reduce_chip_done(r_mid, (dbias,))
    dproj, g_lb, g_gain = _hgrn_bwd(dproj, proj, o_pre, states, dy_a, hgrn_lb_logits, hgrn_out_gain, h_a,
                                    deps=(token, token_a))
    g_in, g_in16 = _matmul(h1, dproj, ta=True, name="dw_in", also_bf16=True)
    r_in, token = reduce_start("in", ["w_in"], [g_in], (), sent=[g_in16])
    g_rb = _relbias_reduce(dbias.transpose(1, 0, 2))[:, :N_REL]
    reduce_finish(r_mid, (token,))
    r_in, token = reduce_pair_done(r_in, (g_rb, upd["w_out"][0]))
    dh1 = _matmul(dproj, wg_in, tb=True, name="d_h1", deps=(token,))
    dx, _, g_norm_mix = _rmsnorm_bwd(dh1, x0, r1, norm_mix, dx2, "rmsnorm_mix_bwd")
    reduce_finish(r_fin, (dx,))
    reduce_finish(r_fout, (dx,))
    r_in, token = reduce_chip_done(r_in, (upd["w_ffn_in"][0], upd["w_ffn_out"][0], upd["w_proj_a"][0],
                                          upd["w_proj_b"][0]))

    small_w = [b_gate, norm_mix, norm_ffn, hgrn_lb_logits, hgrn_out_gain, q_gain, k_gain, rel_bias]
    small_m = [m_b_gate, m_norm_mix, m_norm_ffn, m_hgrn_lb_logits, m_hgrn_out_gain, m_q_gain, m_k_gain, m_rel_bias]
    small_v = [v_b_gate, v_norm_mix, v_norm_ffn, v_hgrn_lb_logits, v_hgrn_out_gain, v_q_gain, v_k_gain, v_rel_bias]
    small_g = [g_bgate, g_norm_mix, g_norm_ffn, g_lb, g_gain, g_qg, g_kg, g_rb]
    sizes = [w.size for w in small_w]
    length = -(-(sum(sizes) + 1) // 128) * 128

    def pack(parts_):
        flat = jnp.concatenate([p.reshape(1, -1) for p in parts_], axis=1)
        return jnp.pad(flat, ((0, 0), (0, length - flat.shape[1])))

    one = jnp.ones((1, 1), F32)
    packed = _small_allreduce_adamw(pack(small_g + [loss_part]), pack(small_w + [one]), pack(small_m + [one]),
                                    pack(small_v + [one]), deps=(token,))

    def unpack(vec):
        out, at = [], 0
        for w, n in zip(small_w, sizes):
            out.append(vec[0, at:at + n].reshape(w.shape))
            at += n
        return out, vec[0, at]

    (sg, loss), (sd, _), (sm, _), (sv, _) = [unpack(p) for p in packed]
    reduce_finish(r_in, (packed[0],))

    def ordered(small, bigs):
        bigs = [bigs[nm][None] for nm in names]
        return [bigs[0]] + small + bigs[1:]

    return (loss, dx.reshape(x.shape), *ordered(sg, g_big), *ordered(sd, {nm: upd[nm][0] for nm in names}),
            *ordered(sm, {nm: upd[nm][1] for nm in names}), *ordered(sv, {nm: upd[nm][2] for nm in names}))
```

```python
import functools

import jax
import jax.numpy as jnp
from jax import lax
from jax.experimental import pallas as pl
from jax.experimental.pallas import tpu as pltpu

F32 = jnp.float32
BF16 = jnp.bfloat16
SDS = jax.ShapeDtypeStruct
MESH = pl.DeviceIdType.MESH
HIGHEST = lax.Precision.HIGHEST

CHUNK = 64
SUB = 16
HEAD = 128
N_PAST = 8
BAND = (N_PAST + 1) * CHUNK
PAD = N_PAST * CHUNK
REL_FUTURE = CHUNK - 1
REL_PAST = 2 * CHUNK - 1
N_REL = REL_FUTURE + REL_PAST + 1
REL_LANES = 256
EPS = 1e-6
MIX_HEADS = 2
MIX_UNROLL = 4
MIX_UNROLL_BWD = 4
ATT_UNROLL = 8
ATT_UNROLL_BWD = 4
EXP_CLAMP = 80.0

ADAM_LR = 0.001
ADAM_B1 = 0.9
ADAM_B2 = 0.999
ADAM_EPS = 1e-08
ADAM_WD = 0.01
ADAM_STEP = 10

VMEM_LIMIT = 56 * 1024 * 1024

HBM = pl.BlockSpec(memory_space=pltpu.HBM)
ANY = pl.BlockSpec(memory_space=pl.ANY)
SEM = pl.BlockSpec(memory_space=pltpu.SEMAPHORE)

NT = (((1,), (1,)), ((), ()))
TN = (((0,), (0,)), ((), ()))
NN = (((1,), (0,)), ((), ()))


def _params(sem=None, **kw):
    return pltpu.CompilerParams(dimension_semantics=sem, vmem_limit_bytes=VMEM_LIMIT, **kw)


def _tile(n, pref, unit=128):
    if n <= pref:
        return n
    t = pref - pref % unit
    while n % t:
        t -= unit
    return t


STREAM_BLOCK = 256 * 1408


def _row_tile(rows, cols):
    return _tile(rows, max(16, STREAM_BLOCK // cols), 16)


def _sigmoid(x):
    return 1.0 / (1.0 + jnp.exp(-x))


def _dsilu(x, s):
    return s * (1.0 + x * (1.0 - s))


def _split(a):
    hi = a.astype(BF16)
    return hi, (a - hi.astype(F32)).astype(BF16)


def _dot3(a, b, dims):
    dot = lambda u, v: lax.dot_general(u, v, dims, preferred_element_type=F32)
    return dot(a[0], b[1]) + dot(a[1], b[0]) + dot(a[0], b[0])


def _fdot(a, b):
    return lax.dot_general(a, b, NN, precision=HIGHEST, preferred_element_type=F32)


MM_TILE_K = 5632
MM_TILE_N = 512


def _matmul_chunks(h, w, which, prev, pos, name, own_shard=False, deps=()):
    t, d = h.shape
    nc_ = w.shape[1] if own_shard else w.shape[1] // 4
    tm, tn = _tile(t, 1024), _tile(nc_, 1408)
    nn = nc_ // tn

    def chunk(q, p):
        sel = p[which[0]]
        for i in range(1, len(which)):
            sel = jnp.where(q == i, p[which[i]], sel)
        return sel

    def body(p_ref, h_ref, w_ref, *rest):
        rest[-1][...] = jnp.dot(h_ref[...], w_ref[...].astype(BF16), preferred_element_type=F32)

    if own_shard:
        w_spec = pl.BlockSpec((d, tn), lambda q, i, j, p: (0, j))
    else:
        w_spec = pl.BlockSpec((d, tn), lambda q, i, j, p: (0, chunk(q, p) * nn + j))
    n_extra = len(deps) + (prev is not None)
    return pl.pallas_call(
        body,
        grid_spec=pltpu.PrefetchScalarGridSpec(
            num_scalar_prefetch=1, grid=(len(which), t // tm, nn),
            in_specs=[pl.BlockSpec((tm, d), lambda q, i, j, p: (i, 0)), w_spec] + [ANY] * n_extra,
            out_specs=pl.BlockSpec((tm, tn), lambda q, i, j, p: (i, chunk(q, p) * nn + j))),
        out_shape=SDS((t, 4 * nc_), F32), input_output_aliases={3 + len(deps): 0} if prev is not None else {},
        compiler_params=_params(("arbitrary", "arbitrary", "arbitrary")),
        name=name)(pos, h, w, *deps, *(() if prev is None else (prev,)))


def _matmul(a, b, *, ta=False, tb=False, res=None, out_dtype=F32, name, deps=(), a_lead=None, b_lead=None,
            also_bf16=False):
    a2, b2 = a.shape[-2:], b.shape[-2:]
    m, k = (a2[1], a2[0]) if ta else a2
    n = b2[0] if tb else b2[1]
    if k > MM_TILE_K:
        tk, tm, tn = _tile(k, MM_TILE_K // 2), _tile(m, 1024), _tile(n, 1024)
    else:
        tk = k
        tm, tn = _tile(m, 2048 if tk <= MM_TILE_K // 2 else 1024), _tile(n, MM_TILE_N)
    nk = k // tk
    dims = ((((0,) if ta else (1,)), ((1,) if tb else (0,))), ((), ()))

    def body(*refs):
        n_in = 2 + (res is not None)
        a_ref, b_ref = refs[:2]
        r_ref = refs[2] if res is not None else None
        o_ref = refs[n_in + len(deps)]
        part = lax.dot_general(a_ref[...].astype(BF16), b_ref[...].astype(BF16), dims, preferred_element_type=F32)

        def finish(out):
            if r_ref is not None:
                out = out + r_ref[...]
            o_ref[...] = out.astype(o_ref.dtype)
            if also_bf16:
                refs[n_in + len(deps) + 1][...] = out.astype(BF16)

        if nk == 1:
            finish(part)
            return
        acc_ref = refs[-1]
        kk = pl.program_id(2)

        @pl.when(kk == 0)
        def _():
            acc_ref[...] = part

        @pl.when(jnp.logical_and(kk > 0, kk < nk - 1))
        def _():
            acc_ref[...] += part

        @pl.when(kk == nk - 1)
        def _():
            finish(acc_ref[...] + part)

    def spec(block, index, lead):
        if lead is None:
            return pl.BlockSpec(block, index)
        return pl.BlockSpec((None,) + block, lambda i, j, l: (lead,) + index(i, j, l))

    a_spec = spec((tk, tm), lambda i, j, l: (l, i), a_lead) if ta else spec((tm, tk), lambda i, j, l: (i, l), a_lead)
    b_spec = spec((tn, tk), lambda i, j, l: (j, l), b_lead) if tb else spec((tk, tn), lambda i, j, l: (l, j), b_lead)
    o_spec = pl.BlockSpec((tm, tn), lambda i, j, l: (i, j))
    in_specs = [a_spec, b_spec] + ([o_spec] if res is not None else []) + [ANY] * len(deps)
    args = (a, b) + ((res,) if res is not None else ()) + tuple(deps)
    out_specs, out_shape = o_spec, SDS((m, n), out_dtype)
    if also_bf16:
        out_specs, out_shape = [o_spec, o_spec], [out_shape, SDS((m, n), BF16)]
    return pl.pallas_call(
        body, grid=(m // tm, n // tn, nk), in_specs=in_specs, out_specs=out_specs,
        out_shape=out_shape, scratch_shapes=[pltpu.VMEM((tm, tn), F32)] if nk > 1 else [],
        compiler_params=_params(("parallel", "parallel", "arbitrary")), name=name)(*args)


def _cast_into_full(w, kind, pos, name):
    r, n = w.shape
    tr = _tile(r, 512, 16)
    nr = r // tr
    if kind == "col":
        shape, o_spec = (r, 4 * n), pl.BlockSpec((tr, n), lambda i, p: (i, p[0]))
    else:
        shape, o_spec = (4 * r, n), pl.BlockSpec((tr, n), lambda i, p: (p[0] * nr + i, 0))

    def body(p_ref, w_ref, o_ref):
        o_ref[...] = w_ref[...].astype(BF16)

    return pl.pallas_call(
        body,
        grid_spec=pltpu.PrefetchScalarGridSpec(num_scalar_prefetch=1, grid=(nr,),
                                               in_specs=[pl.BlockSpec((tr, n), lambda i, p: (i, 0))], out_specs=o_spec),
        out_shape=SDS(shape, BF16), compiler_params=_params(("parallel",)), name=name)(pos, w)


def _rmsnorm_fwd(x, gain, name):
    t, d = x.shape
    tm = _tile(t, 256)

    def body(x_ref, g_ref, h_ref, r_ref):
        xv = x_ref[...]
        r = lax.rsqrt(jnp.mean(xv * xv, axis=-1, keepdims=True) + EPS)
        h_ref[...] = (xv * r * g_ref[...]).astype(BF16)
        r_ref[...] = r

    return pl.pallas_call(
        body, grid=(t // tm,),
        in_specs=[pl.BlockSpec((tm, d), lambda i: (i, 0)), pl.BlockSpec((1, d), lambda i: (0, 0))],
        out_specs=[pl.BlockSpec((tm, d), lambda i: (i, 0)), pl.BlockSpec((tm, 1), lambda i: (i, 0))],
        out_shape=[SDS((t, d), BF16), SDS((t, 1), F32)], compiler_params=_params(("parallel",)), name=name)(x, gain)


def _rmsnorm_bwd(dh, x, r, gain, dres, name, deps=()):
    t, d = x.shape
    tm = _tile(t, 256)

    def body(dh_ref, x_ref, r_ref, g_ref, dres_ref, *rest):
        dx_ref, dxb_ref, dg_ref = rest[len(deps):]

        @pl.when(pl.program_id(0) == 0)
        def _():
            dg_ref[...] = jnp.zeros_like(dg_ref)

        dhv, xv, rv = dh_ref[...], x_ref[...], r_ref[...]
        dg_ref[...] += jnp.sum(dhv * (xv * rv), axis=0, keepdims=True)
        u = dhv * g_ref[...]
        dx = dres_ref[...] + rv * u - xv * (rv * rv * rv) * jnp.mean(u * xv, axis=-1, keepdims=True)
        dx_ref[...] = dx
        dxb_ref[...] = dx.astype(BF16)

    row = pl.BlockSpec((tm, d), lambda i: (i, 0))
    vec = pl.BlockSpec((1, d), lambda i: (0, 0))
    return pl.pallas_call(
        body, grid=(t // tm,),
        in_specs=[row, row, pl.BlockSpec((tm, 1), lambda i: (i, 0)), vec, row] + [ANY] * len(deps),
        out_specs=[row, row, vec], out_shape=[SDS((t, d), F32), SDS((t, d), BF16), SDS((1, d), F32)],
        compiler_params=_params(("arbitrary",)), name=name)(dh, x, r, gain, dres, *deps)


def _proj_merge(y_a, y_b, w_a, w_b, proj, b_gate, off, deps=()):
    t, ka = y_a.shape
    kb = y_b.shape[1]
    d = w_a.shape[1]
    tm, tc = _tile(t, 1024), _tile(d, MM_TILE_N)
    nj = d // tc
    oa, ob = off // tc, off // tc + nj

    def body(ya_ref, yb_ref, wa_ref, wb_ref, la_ref, lb_ref, ba_ref, bb_ref, *rest):
        pa_ref, pb_ref, o_ref = rest[len(deps):]
        pa = jnp.dot(ya_ref[...], wa_ref[...], preferred_element_type=F32)
        pb = jnp.dot(yb_ref[...], wb_ref[...], preferred_element_type=F32)
        pa_ref[...] = pa
        pb_ref[...] = pb
        ga = _sigmoid(la_ref[...] + ba_ref[...])
        gb = _sigmoid(lb_ref[...] + bb_ref[...])
        o_ref[...] = (ga * pa + gb * pb).astype(BF16)

    tile = pl.BlockSpec((tm, tc), lambda i, j: (i, j))
    return pl.pallas_call(
        body, grid=(t // tm, nj),
        in_specs=[pl.BlockSpec((tm, ka), lambda i, j: (i, 0)), pl.BlockSpec((tm, kb), lambda i, j: (i, 0)),
                  pl.BlockSpec((ka, tc), lambda i, j: (0, j)), pl.BlockSpec((kb, tc), lambda i, j: (0, j)),
                  pl.BlockSpec((tm, tc), lambda i, j: (i, oa + j)), pl.BlockSpec((tm, tc), lambda i, j: (i, ob + j)),
                  pl.BlockSpec((1, tc), lambda i, j: (0, j)), pl.BlockSpec((1, tc), lambda i, j: (0, nj + j))]
        + [ANY] * len(deps),
        out_specs=[tile, tile, tile], out_shape=[SDS((t, d), F32), SDS((t, d), F32), SDS((t, d), BF16)],
        compiler_params=_params(("parallel", "parallel")),
        name="proj_merge")(y_a, y_b, w_a, w_b, proj, proj, b_gate, b_gate, *deps)


def _d_merged_gates(dx, w, proj, b_gate, pa, pb, off, deps=()):
    t, k = dx.shape
    d = w.shape[0]
    tm, tc = _tile(t, 1024), _tile(d, MM_TILE_N)
    nj, ni = d // tc, t // tm
    o0 = off // tc

    def body(dx_ref, w_ref, la_ref, lb_ref, ba_ref, bb_ref, pa_ref, pb_ref, *rest):
        dp_ref, dproj_ref, db_ref, stage, sems = rest[len(deps):]
        j, i = pl.program_id(0), pl.program_id(1)
        step = j * ni + i
        slot = step % 2

        def copies(s, ii, jj):
            rows = pl.ds(pl.multiple_of(ii * tm, tm), tm)
            return [pltpu.make_async_copy(
                stage.at[s, br], dproj_ref.at[rows, pl.ds(pl.multiple_of(off + br * d + jj * tc, 128), tc)],
                sems.at[s, br]) for br in range(2)]

        @pl.when(step >= 2)
        def _():
            for cp in copies(slot, 0, 0):
                cp.wait()

        dm = lax.dot_general(dx_ref[...], w_ref[...], NT, preferred_element_type=F32)

        @pl.when(i == 0)
        def _():
            db_ref[...] = jnp.zeros_like(db_ref)

        for br, (l_ref, b_ref, p_ref) in enumerate(((la_ref, ba_ref, pa_ref), (lb_ref, bb_ref, pb_ref))):
            g = _sigmoid(l_ref[...] + b_ref[...])
            dp_ref[br] = (dm * g).astype(BF16)
            dl = dm * p_ref[...] * g * (1.0 - g)
            stage[slot, br] = dl.astype(BF16)
            db_ref[br] += jnp.sum(dl, axis=0, keepdims=True)
        for cp in copies(slot, i, j):
            cp.start()

        @pl.when(step == ni * nj - 1)
        def _():
            for s in range(min(2, ni * nj)):
                for cp in copies(s, 0, 0):
                    cp.wait()

    tile = pl.BlockSpec((tm, tc), lambda j, i: (i, j))
    return pl.pallas_call(
        body, grid=(nj, ni),
        in_specs=[pl.BlockSpec((tm, k), lambda j, i: (i, 0)), pl.BlockSpec((tc, k), lambda j, i: (j, 0)),
                  pl.BlockSpec((tm, tc), lambda j, i: (i, o0 + j)), pl.BlockSpec((tm, tc), lambda j, i: (i, o0 + nj + j)),
                  pl.BlockSpec((1, tc), lambda j, i: (0, j)), pl.BlockSpec((1, tc), lambda j, i: (0, nj + j)),
                  tile, tile] + [ANY] * len(deps),
        out_specs=[pl.BlockSpec((2, tm, tc), lambda j, i: (0, i, j)), ANY, pl.BlockSpec((2, 1, tc), lambda j, i: (0, 0, j))],
        out_shape=[SDS((2, t, d), BF16), SDS(proj.shape, BF16), SDS((2, 1, d), F32)],
        scratch_shapes=[pltpu.VMEM((2, 2, tm, tc), BF16), pltpu.SemaphoreType.DMA((2, 2))],
        compiler_params=_params(("arbitrary", "arbitrary")),
        name="d_merged_gates")(dx, w, proj, proj, b_gate, b_gate, pa, pb, *deps)


def _ffn_in_swiglu(h, w, deps=()):
    t, d = h.shape
    f = w.shape[1] // 2
    tm, tn = _tile(t, 2048), _tile(f, MM_TILE_N)
    nj = f // tn

    def body(h_ref, wg_ref, wu_ref, *rest):
        g_ref, u_ref, a_ref = rest[len(deps):]
        hv = h_ref[...]
        g = jnp.dot(hv, wg_ref[...], preferred_element_type=F32)
        u = jnp.dot(hv, wu_ref[...], preferred_element_type=F32)
        g_ref[...] = g
        u_ref[...] = u
        a_ref[...] = (g * _sigmoid(g) * u).astype(BF16)

    tile = pl.BlockSpec((tm, tn), lambda i, j: (i, j))
    return pl.pallas_call(
        body, grid=(t // tm, nj),
        in_specs=[pl.BlockSpec((tm, d), lambda i, j: (i, 0)), pl.BlockSpec((d, tn), lambda i, j: (0, j)),
                  pl.BlockSpec((d, tn), lambda i, j: (0, nj + j))] + [ANY] * len(deps),
        out_specs=[tile, tile, tile], out_shape=[SDS((t, f), F32), SDS((t, f), F32), SDS((t, f), BF16)],
        compiler_params=_params(("parallel", "parallel")), name="ffn_in_swiglu")(h, w, w, *deps)


def _d_act_swiglu(dy, w, gate, up, deps=()):
    t, k = dy.shape
    f = w.shape[0]
    tm, tn = _tile(t, 1024), _tile(f, MM_TILE_N)
    ni, nj = t // tm, f // tn

    def body(dy_ref, w_ref, g_ref, u_ref, *rest):
        out_ref, stage, sems = rest[len(deps):]
        i, j = pl.program_id(0), pl.program_id(1)
        step = i * nj + j
        slot = step % 2

        def copies(s, ii, jj):
            rows = pl.ds(pl.multiple_of(ii * tm, tm), tm)
            return [pltpu.make_async_copy(
                stage.at[s, half], out_ref.at[rows, pl.ds(pl.multiple_of(half * f + jj * tn, 128), tn)],
                sems.at[s, half]) for half in range(2)]

        @pl.when(step >= 2)
        def _():
            for cp in copies(slot, 0, 0):
                cp.wait()

        dact = lax.dot_general(dy_ref[...], w_ref[...], NT, preferred_element_type=F32)
        g = g_ref[...]
        sg = _sigmoid(g)
        stage[slot, 0] = (dact * u_ref[...] * _dsilu(g, sg)).astype(BF16)
        stage[slot, 1] = (dact * (g * sg)).astype(BF16)
        for cp in copies(slot, i, j):
            cp.start()

        @pl.when(step == ni * nj - 1)
        def _():
            for s in range(min(2, ni * nj)):
                for cp in copies(s, 0, 0):
                    cp.wait()

    tile = pl.BlockSpec((tm, tn), lambda i, j: (i, j))
    return pl.pallas_call(
        body, grid=(ni, nj),
        in_specs=[pl.BlockSpec((tm, k), lambda i, j: (i, 0)), pl.BlockSpec((tn, k), lambda i, j: (j, 0)), tile, tile]
        + [ANY] * len(deps),
        out_specs=ANY, out_shape=SDS((t, 2 * f), BF16),
        scratch_shapes=[pltpu.VMEM((2, 2, tm, tn), BF16), pltpu.SemaphoreType.DMA((2, 2))],
        compiler_params=_params(("arbitrary", "arbitrary")), name="d_act_swiglu")(dy, w, gate, up, *deps)


def _ffn_out_loss(act, w, x_res, target):
    t, d = x_res.shape
    k = act.shape[1]
    tm, tn = _tile(t, 1024), _tile(d, MM_TILE_N)

    def body(a_ref, w_ref, r_ref, t_ref, dy_ref, dyb_ref, l_ref):
        @pl.when(jnp.logical_and(pl.program_id(0) == 0, pl.program_id(1) == 0))
        def _():
            l_ref[...] = jnp.zeros_like(l_ref)

        y = jnp.dot(a_ref[...], w_ref[...], preferred_element_type=F32) + r_ref[...]
        e = y - t_ref[...]
        dy = e * (1.0 / d)
        dy_ref[...] = dy
        dyb_ref[...] = dy.astype(BF16)
        l_ref[...] += (0.5 / d) * jnp.sum(jnp.sum(e * e, axis=-1, keepdims=True), axis=0, keepdims=True)

    tile = pl.BlockSpec((tm, tn), lambda i, j: (i, j))
    return pl.pallas_call(
        body, grid=(t // tm, d // tn),
        in_specs=[pl.BlockSpec((tm, k), lambda i, j: (i, 0)), pl.BlockSpec((k, tn), lambda i, j: (0, j)), tile, tile],
        out_specs=[tile, tile, pl.BlockSpec((1, 1), lambda i, j: (0, 0))],
        out_shape=[SDS((t, d), F32), SDS((t, d), BF16), SDS((1, 1), F32)],
        compiler_params=_params(("arbitrary", "arbitrary")), name="ffn_out_loss")(act, w, x_res, target)


def _rel_onehot(qi):
    p = lax.broadcasted_iota(jnp.int32, (REL_LANES, BAND), 1)
    r = lax.broadcasted_iota(jnp.int32, (REL_LANES, BAND), 0)
    idx = jnp.clip(qi + PAD - p, -REL_FUTURE, REL_PAST) + REL_FUTURE
    return (idx == r).astype(F32)


def _relbias_expand(rb):
    h = rb.shape[0]

    def body(rb_ref, o_ref):
        def step(qi, _):
            o_ref[qi] = _fdot(rb_ref[...], _rel_onehot(qi))
            return 0

        lax.fori_loop(0, CHUNK, step, 0)

    return pl.pallas_call(body, out_shape=SDS((CHUNK, h, BAND), F32), compiler_params=_params(),
                          name="relbias_expand")(rb)


def _relbias_reduce(dbias):
    h = dbias.shape[1]

    rows_per_pass = 4

    def body(db_ref, o_ref):
        def step(i, acc):
            parts = []
            for u in range(rows_per_pass):
                qi = i * rows_per_pass + u
                xv = db_ref[qi]
                hi = xv.astype(BF16)
                rest = xv - hi.astype(F32)
                mid = rest.astype(BF16)
                low = (rest - mid.astype(F32)).astype(BF16)
                parts.append(lax.dot_general(jnp.concatenate([hi, mid, low], axis=0), _rel_onehot(qi).astype(BF16), NT,
                                             preferred_element_type=F32))
            for part in parts:
                acc = acc + (part[0:h] + part[h:2 * h] + part[2 * h:3 * h])
            return acc

        o_ref[...] = lax.fori_loop(0, CHUNK // rows_per_pass, step, jnp.zeros((h, REL_LANES), F32))

    return pl.pallas_call(body, out_shape=SDS((h, REL_LANES), F32), compiler_params=_params(),
                          name="relbias_reduce")(dbias)


def _lower_bound(l_ref):
    l0, l1 = l_ref[0:1, :], l_ref[1:2, :]
    m = jnp.maximum(l0, l1)
    e0, e1 = jnp.exp(l0 - m), jnp.exp(l1 - m)
    return e0 / (e0 + e1)


def _tri(lower):
    r = lax.broadcasted_iota(jnp.int32, (CHUNK, CHUNK), 0)
    c = lax.broadcasted_iota(jnp.int32, (CHUNK, CHUNK), 1)
    return r >= c if lower else r <= c


def _hgrn_intra(qs, kk, b_s):
    rows = lax.broadcasted_iota(jnp.int32, (CHUNK, HEAD), 0)
    b = b_s[...]
    out = []
    for i in range(CHUNK // SUB):
        lo = i * SUB
        ref = jnp.zeros((1, HEAD), F32) if i == 0 else b_s[lo - 1:lo, :]
        eq = jnp.exp(b[lo:lo + SUB] - ref)
        qt = _split(qs[lo:lo + SUB] * eq)
        e = jnp.where(rows < lo + SUB, jnp.exp(jnp.minimum(ref - b, EXP_CLAMP)), 0.0)
        kt = _split(kk * e)
        out.append((eq, qt, e, kt))
    return out


def _hgrn_scores(blocks):
    tr = lax.broadcasted_iota(jnp.int32, (SUB, CHUNK), 0)
    tc = lax.broadcasted_iota(jnp.int32, (SUB, CHUNK), 1)
    return jnp.concatenate([jnp.where(tc <= tr + i * SUB, _dot3(qt, kt, NT), 0.0)
                            for i, (_, qt, _, kt) in enumerate(blocks)], axis=0)


def _hgrn_fwd(proj, lb_logits, gain, n_heads):
    t = proj.shape[0]
    nc = t // CHUNK
    da = n_heads * HEAD
    hp = MIX_HEADS
    wide = hp * HEAD

    def body(q_ref, f_ref, i_ref, g_ref, l_ref, gain_ref, y_ref, o_ref, st_ref, state, b_s):
        state[...] = jnp.zeros_like(state)
        lb_all = _lower_bound(l_ref)
        tril = _tri(True).astype(F32)

        def chunks(i, _):
            dot = functools.partial(lax.dot_general, preferred_element_type=F32)
            items = []
            for u in range(MIX_UNROLL):
                for hh in range(hp):
                    j = i * MIX_UNROLL + u
                    sl = pl.ds(pl.multiple_of(j * CHUNK, CHUNK), CHUNK)
                    cols = slice(hh * HEAD, (hh + 1) * HEAD)
                    lb = lb_all[:, cols]
                    fg = lb + (1.0 - lb) * _sigmoid(f_ref[sl, cols])
                    qv = q_ref[sl, cols]
                    gv = g_ref[sl, cols]
                    items.append(dict(hh=hh, j=j, sl=sl, cols=cols, lf=jnp.log(fg), kk=1.0 - fg, qs=qv * _sigmoid(qv),
                                      vb=i_ref[sl, cols].astype(BF16), gate=gv * _sigmoid(gv)))
            for it in items:
                it["b"] = _fdot(tril, it["lf"])
            for slot, it in enumerate(items):
                b = it["b"]
                b_s[slot] = b
                it["blocks"] = _hgrn_intra(it["qs"], it["kk"], b_s.at[slot])
                it["ebl"] = jnp.exp(b_s[slot, CHUNK - 1:CHUNK, :])
                it["qe"] = (it["qs"] * jnp.exp(b)).astype(BF16)
                it["kd"] = (it["kk"] * jnp.exp(b_s[slot, CHUNK - 1:CHUNK, :] - b)).astype(BF16)
            for it in items:
                it["a"] = _hgrn_scores(it["blocks"]).astype(BF16)
            for it in items:
                it["kv"] = dot(it["vb"], it["kd"], TN)
                it["o"] = dot(it["a"], it["vb"], NN)
            s_now = [state[hh] for hh in range(hp)]
            for it in items:
                it["s_in"] = s_now[it["hh"]]
                s_now[it["hh"]] = it["s_in"] * it["ebl"] + it["kv"]
            for hh in range(hp):
                state[hh] = s_now[hh]
            for it in items:
                it["o"] = it["o"] + dot(it["qe"], it["s_in"].astype(BF16), NT)
            for it in items:
                o, sl, cols = it["o"], it["sl"], it["cols"]
                st_ref[it["hh"], it["j"]] = it["s_in"]
                o_ref[sl, cols] = o
                rr = lax.rsqrt(jnp.mean(o * o, axis=-1, keepdims=True) + EPS)
                y_ref[sl, cols] = (o * rr * gain_ref[:, cols] * it["gate"]).astype(BF16)
            return 0

        assert nc % MIX_UNROLL == 0, (nc, MIX_UNROLL)
        lax.fori_loop(0, nc // MIX_UNROLL, chunks, 0)

    col = lambda k: pl.BlockSpec((t, wide), lambda h: (0, k * (n_heads // hp) + h))
    vec = pl.BlockSpec((1, wide), lambda h: (0, h))
    return pl.pallas_call(
        body, grid=(n_heads // hp,),
        in_specs=[col(0), col(1), col(2), col(3), pl.BlockSpec((2, wide), lambda h: (0, h)), vec],
        out_specs=[pl.BlockSpec((t, wide), lambda h: (0, h)), pl.BlockSpec((t, wide), lambda h: (0, h)),
                   pl.BlockSpec((hp, nc, HEAD, HEAD), lambda h: (h, 0, 0, 0))],
        out_shape=[SDS((t, da), BF16), SDS((t, da), F32), SDS((n_heads, nc, HEAD, HEAD), F32)],
        scratch_shapes=[pltpu.VMEM((hp, HEAD, HEAD), F32), pltpu.VMEM((hp * MIX_UNROLL, CHUNK, HEAD), F32)],
        compiler_params=_params(("parallel",)), name="hgrn_fwd")(proj, proj, proj, proj, lb_logits, gain)


def _write_column_groups(res, dproj_ref, sems, col0, stride, h, width):
    copies = [pltpu.make_async_copy(
        res.at[p], dproj_ref.at[:, pl.ds(pl.multiple_of((col0 + p * stride + h) * width, HEAD), width)], sems.at[p])
        for p in range(res.shape[0])]
    for cp in copies:
        cp.start()
    for cp in copies:
        cp.wait()


def _hgrn_bwd(dproj, proj, o_pre, states, dy, lb_logits, gain, n_heads, deps=()):
    t = proj.shape[0]
    nc = t // CHUNK
    da = n_heads * HEAD
    hp = MIX_HEADS
    wide = hp * HEAD

    def body(*refs):
        (q_ref, f_ref, i_ref, g_ref, o_ref, st_ref, dy_ref, l_ref, gain_ref,
         dproj_ref, dl_ref, dgain_ref, res, dstate, b_s, out_sems) = refs[1 + len(deps):]

        def compute():
            dstate[...] = jnp.zeros_like(dstate)
            lb_all = _lower_bound(l_ref)
            tril_m, tril, triu = _tri(True), _tri(True).astype(F32), _tri(False).astype(F32)
            last = lax.broadcasted_iota(jnp.int32, (CHUNK, HEAD), 0) == CHUNK - 1

            def chunks(i, carry):
                dot = functools.partial(lax.dot_general, preferred_element_type=F32)
                items = []
                for u in range(MIX_UNROLL_BWD):
                    for hh in range(hp):
                        j = nc - 1 - (i * MIX_UNROLL_BWD + u)
                        sl = pl.ds(pl.multiple_of(j * CHUNK, CHUNK), CHUNK)
                        cols = slice(hh * HEAD, (hh + 1) * HEAD)
                        lb, gain_v = lb_all[:, cols], gain_ref[:, cols]
                        sg = _sigmoid(f_ref[sl, cols])
                        fg = lb + (1.0 - lb) * sg
                        qv = q_ref[sl, cols]
                        sq = _sigmoid(qv)
                        gv = g_ref[sl, cols]
                        sgg = _sigmoid(gv)
                        silg = gv * sgg
                        o = o_ref[sl, cols]
                        dyv = dy_ref[sl, cols]
                        rr = lax.rsqrt(jnp.mean(o * o, axis=-1, keepdims=True) + EPS)
                        on = o * rr
                        don = dyv * gain_v * silg
                        do = (rr * don - o * (rr * rr * rr) * jnp.mean(don * o, axis=-1, keepdims=True)).astype(BF16)
                        items.append(dict(
                            hh=hh, j=j, sl=sl, cols=cols, lb=lb, sg=sg, fg=fg, kk=1.0 - fg, qv=qv, sq=sq, qs=qv * sq,
                            vb=i_ref[sl, cols].astype(BF16), do=do, dg=dyv * on * gain_v * _dsilu(gv, sgg),
                            dgain=jnp.sum(dyv * on * silg, axis=0, keepdims=True)))
                for it in items:
                    it["b"] = _fdot(tril, jnp.log(it["fg"]))
                for slot, it in enumerate(items):
                    b = it["b"]
                    b_s[slot] = b
                    it["blocks"] = _hgrn_intra(it["qs"], it["kk"], b_s.at[slot])
                    bl = b_s[slot, CHUNK - 1:CHUNK, :]
                    it["eb"], it["ebl"], it["ekd"] = jnp.exp(b), jnp.exp(bl), jnp.exp(bl - b)
                    it["s_in"] = st_ref[it["hh"], it["j"]]
                for it in items:
                    it["a"] = _hgrn_scores(it["blocks"]).astype(BF16)
                    it["da"] = jnp.where(tril_m, dot(it["do"], it["vb"], NT), 0.0)
                for it in items:
                    dq_rows = []
                    dk = jnp.zeros((CHUNK, HEAD), F32)
                    for blk, (eq, qt, e, kt) in enumerate(it["blocks"]):
                        da_i = _split(it["da"][blk * SUB:(blk + 1) * SUB])
                        dq_rows.append(eq * _dot3(da_i, kt, NN))
                        dk = dk + e * _dot3(da_i, qt, TN)
                    it["dq"] = jnp.concatenate(dq_rows, axis=0) + dot(it["do"], it["s_in"].astype(BF16), NN) * it["eb"]
                    it["dk"] = dk
                    it["dv"] = dot(it["a"], it["do"], TN)
                    it["g"] = dot(it["do"], (it["qs"] * it["eb"]).astype(BF16), TN)
                ds_now = [dstate[hh] for hh in range(hp)]
                for it in items:
                    it["ds_out"] = ds_now[it["hh"]]
                    ds_now[it["hh"]] = it["ds_out"] * it["ebl"] + it["g"]
                for hh in range(hp):
                    dstate[hh] = ds_now[hh]
                for it in items:
                    dsb = it["ds_out"].astype(BF16)
                    it["dv"] = it["dv"] + dot((it["kk"] * it["ekd"]).astype(BF16), dsb, NT)
                    it["dk_state"] = it["ekd"] * dot(it["vb"], dsb, NN)
                for it in items:
                    kk, dk_state = it["kk"], it["dk_state"]
                    it["dk"] = it["dk"] + dk_state
                    extra = (jnp.sum(kk * dk_state, axis=0, keepdims=True)
                             + it["ebl"] * jnp.sum(it["s_in"] * it["ds_out"], axis=0, keepdims=True))
                    it["db"] = it["qs"] * it["dq"] - kk * it["dk"] + jnp.where(last, extra, 0.0)
                for it in items:
                    it["dlf"] = _fdot(triu, it["db"])
                carry = list(carry)
                for it in items:
                    hh, sl, cols, sg, lb = it["hh"], it["sl"], it["cols"], it["sg"], it["lb"]
                    dfg = it["dlf"] / it["fg"] - it["dk"]
                    dlb_acc, dgain_acc = carry[hh]
                    carry[hh] = (dlb_acc + jnp.sum(dfg * (1.0 - sg), axis=0, keepdims=True), dgain_acc + it["dgain"])
                    res[0, sl, cols] = (it["dq"] * _dsilu(it["qv"], it["sq"])).astype(BF16)
                    res[1, sl, cols] = (dfg * (1.0 - lb) * sg * (1.0 - sg)).astype(BF16)
                    res[2, sl, cols] = it["dv"].astype(BF16)
                    res[3, sl, cols] = it["dg"].astype(BF16)
                return tuple(carry)

            assert nc % MIX_UNROLL_BWD == 0, (nc, MIX_UNROLL_BWD)
            zero = jnp.zeros((1, HEAD), F32)
            sums = lax.fori_loop(0, nc // MIX_UNROLL_BWD, chunks, ((zero, zero),) * hp)
            for hh, (dlb, dgain) in enumerate(sums):
                cols = slice(hh * HEAD, (hh + 1) * HEAD)
                lb = lb_all[:, cols]
                dgain_ref[:, cols] = dgain
                dl0 = dlb * lb * (1.0 - lb)
                dl_ref[0:1, cols] = dl0
                dl_ref[1:2, cols] = -dl0

        compute()
        _write_column_groups(res, dproj_ref, out_sems, 0, ng, pl.program_id(0), wide)

    ng = n_heads // hp
    col = lambda k: pl.BlockSpec((t, wide), lambda h: (0, k * ng + h))
    head = pl.BlockSpec((t, wide), lambda h: (0, h))
    vec = pl.BlockSpec((1, wide), lambda h: (0, h))
    return pl.pallas_call(
        body, grid=(ng,),
        in_specs=[ANY] * (1 + len(deps)) + [col(0), col(1), col(2), col(3), head,
                  pl.BlockSpec((hp, nc, HEAD, HEAD), lambda h: (h, 0, 0, 0)),
                  head, pl.BlockSpec((2, wide), lambda h: (0, h)), vec],
        out_specs=[ANY, pl.BlockSpec((2, wide), lambda h: (0, h)), vec],
        out_shape=[SDS(dproj.shape, BF16), SDS((2, da), F32), SDS((1, da), F32)],
        scratch_shapes=[pltpu.VMEM((4, t, wide), BF16), pltpu.VMEM((hp, HEAD, HEAD), F32),
                        pltpu.VMEM((hp * MIX_UNROLL_BWD, CHUNK, HEAD), F32), pltpu.SemaphoreType.DMA((4,))],
        input_output_aliases={0: 0}, compiler_params=_params(("arbitrary",)),
        name="hgrn_bwd")(dproj, *deps, proj, proj, proj, proj, o_pre, states, dy, lb_logits, gain)


ROWS = 256


def _head_norm(x_ref, gain, dst, dst_off, t):
    def step(i, _):
        sl = pl.ds(pl.multiple_of(i * ROWS, ROWS), ROWS)
        xv = x_ref[sl, :]
        r = lax.rsqrt(jnp.mean(xv * xv, axis=-1, keepdims=True) + EPS)
        dst[pl.ds(pl.multiple_of(dst_off + i * ROWS, ROWS), ROWS), :] = (xv * r * gain).astype(BF16)
        return 0

    lax.fori_loop(0, t // ROWS, step, 0)


def _head_norm_bwd(x_ref, gain, dn_ref, dn_off, out, slot, t):
    def step(i, acc):
        sl = pl.ds(pl.multiple_of(i * ROWS, ROWS), ROWS)
        xv = x_ref[sl, :]
        dn = dn_ref[pl.ds(pl.multiple_of(dn_off + i * ROWS, ROWS), ROWS), :]
        r = lax.rsqrt(jnp.mean(xv * xv, axis=-1, keepdims=True) + EPS)
        u = dn * gain
        out[slot, sl, :] = (r * u - xv * (r * r * r) * jnp.mean(u * xv, axis=-1, keepdims=True)).astype(out.dtype)
        return acc + jnp.sum(dn * (xv * r), axis=0, keepdims=True)

    return lax.fori_loop(0, t // ROWS, step, jnp.zeros((1, HEAD), F32))


def _attn_scores(qn, kpad, n):
    qc = qn[pl.ds(pl.multiple_of(n * CHUNK, CHUNK), CHUNK), :]
    band = pl.ds(pl.multiple_of(n * CHUNK, CHUNK), BAND)
    return qc, band, lax.dot_general(qc, kpad[band, :], NT, preferred_element_type=F32)


def _attn_softmax(raw, bias_ref, n):
    s = raw * (HEAD ** -0.5) + bias_ref[0]
    col = lax.broadcasted_iota(jnp.int32, (CHUNK, BAND), 1)
    s = jnp.where(col >= PAD - n * CHUNK, s, -jnp.inf)
    p = jnp.exp(s - jnp.max(s, axis=-1, keepdims=True))
    return p / jnp.sum(p, axis=-1, keepdims=True)


def _attn_fwd(proj, q_gain, k_gain, bias, n_heads, col0):
    t = proj.shape[0]
    nc = t // CHUNK

    def body(q_ref, k_ref, v_ref, qg_ref, kg_ref, bias_ref, y_ref, qn, kpad, vpad):
        kpad[0:PAD, :] = jnp.zeros((PAD, HEAD), BF16)
        vpad[0:PAD, :] = jnp.zeros((PAD, HEAD), BF16)
        _head_norm(q_ref, qg_ref[...], qn, 0, t)
        _head_norm(k_ref, kg_ref[...], kpad, PAD, t)

        def copy_v(i, _):
            vpad[pl.ds(pl.multiple_of(PAD + i * ROWS, ROWS), ROWS), :] = v_ref[
                pl.ds(pl.multiple_of(i * ROWS, ROWS), ROWS), :].astype(BF16)
            return 0

        lax.fori_loop(0, t // ROWS, copy_v, 0)

        def chunks(i, _):
            ns = [i * ATT_UNROLL + u for u in range(ATT_UNROLL)]
            scored = [_attn_scores(qn, kpad, n) for n in ns]
            probs = [_attn_softmax(raw, bias_ref, n).astype(BF16) for n, (_, _, raw) in zip(ns, scored)]
            outs = [lax.dot_general(p, vpad[band, :], NN, preferred_element_type=F32).astype(BF16)
                    for p, (_, band, _) in zip(probs, scored)]
            for n, o in zip(ns, outs):
                y_ref[pl.ds(pl.multiple_of(n * CHUNK, CHUNK), CHUNK), :] = o
            return 0

        assert nc % ATT_UNROLL == 0, (nc, ATT_UNROLL)
        lax.fori_loop(0, nc // ATT_UNROLL, chunks, 0)

    col = lambda k: pl.BlockSpec((t, HEAD), lambda h: (0, col0 + k * n_heads + h))
    vec = pl.BlockSpec((1, HEAD), lambda h: (0, 0))
    return pl.pallas_call(
        body, grid=(n_heads,),
        in_specs=[col(0), col(1), col(2), vec, vec, pl.BlockSpec((1, CHUNK, BAND), lambda h: (h, 0, 0))],
        out_specs=pl.BlockSpec((t, HEAD), lambda h: (0, h)), out_shape=SDS((t, n_heads * HEAD), BF16),
        scratch_shapes=[pltpu.VMEM((t, HEAD), BF16), pltpu.VMEM((t + PAD, HEAD), BF16), pltpu.VMEM((t + PAD, HEAD), BF16)],
        compiler_params=_params(("parallel",)), name="attn_fwd")(proj, proj, proj, q_gain, k_gain, bias)


def _attn_bwd(dproj, proj, q_gain, k_gain, bias, dy, n_heads, col0, deps=()):
    t = proj.shape[0]
    nc = t // CHUNK

    def body(*refs):
        (q_ref, k_ref, v_ref, qg_ref, kg_ref, bias_ref, dy_ref,
         dproj_ref, dbias_ref, dqg_ref, dkg_ref, qn, kpad, vpad, dqn, dk_acc, dv_acc, res,
         out_sems) = refs[1 + len(deps):]
        h = pl.program_id(0)

        def compute():
            kpad[0:PAD, :] = jnp.zeros((PAD, HEAD), BF16)
            vpad[0:PAD, :] = jnp.zeros((PAD, HEAD), BF16)
            _head_norm(q_ref, qg_ref[...], qn, 0, t)
            _head_norm(k_ref, kg_ref[...], kpad, PAD, t)

            def prep(i, _):
                sl = pl.ds(pl.multiple_of(PAD + i * ROWS, ROWS), ROWS)
                vpad[sl, :] = v_ref[pl.ds(pl.multiple_of(i * ROWS, ROWS), ROWS), :].astype(BF16)
                return 0

            lax.fori_loop(0, t // ROWS, prep, 0)

            def clear(i, _):
                sl = pl.ds(pl.multiple_of(i * ROWS, ROWS), ROWS)
                dk_acc[sl, :] = jnp.zeros((ROWS, HEAD), F32)
                dv_acc[sl, :] = jnp.zeros((ROWS, HEAD), F32)
                return 0

            lax.fori_loop(0, (t + PAD) // ROWS, clear, 0)
            dbias_ref[0] = jnp.zeros((CHUNK, BAND), F32)

            def chunks(i, _):
                dot = functools.partial(lax.dot_general, preferred_element_type=F32)
                ns = [i * ATT_UNROLL_BWD + u for u in range(ATT_UNROLL_BWD)]
                scored = [_attn_scores(qn, kpad, n) for n in ns]
                dos = [dy_ref[pl.ds(pl.multiple_of(n * CHUNK, CHUNK), CHUNK), :].astype(BF16) for n in ns]
                dps = [dot(do, vpad[band, :], NT) for do, (_, band, _) in zip(dos, scored)]
                ps, dss = [], []
                for n, (_, _, raw), dp in zip(ns, scored, dps):
                    p = _attn_softmax(raw, bias_ref, n)
                    ds = p * (dp - jnp.sum(dp * p, axis=-1, keepdims=True))
                    dbias_ref[0] += ds
                    ps.append(p.astype(BF16))
                    dss.append((ds * (HEAD ** -0.5)).astype(BF16))
                dqs = [dot(d, kpad[band, :], NN) for d, (_, band, _) in zip(dss, scored)]
                dks = [dot(d, qc, TN) for d, (qc, _, _) in zip(dss, scored)]
                dvs = [dot(p, do, TN) for p, do in zip(ps, dos)]
                for n, (_, band, _), dq, dk, dv in zip(ns, scored, dqs, dks, dvs):
                    dqn[pl.ds(pl.multiple_of(n * CHUNK, CHUNK), CHUNK), :] = dq
                    dk_acc[band, :] += dk
                    dv_acc[band, :] += dv
                return 0

            assert nc % ATT_UNROLL_BWD == 0, (nc, ATT_UNROLL_BWD)
            lax.fori_loop(0, nc // ATT_UNROLL_BWD, chunks, 0)
            dqg = _head_norm_bwd(q_ref, qg_ref[...], dqn, 0, res, 0, t)
            dkg = _head_norm_bwd(k_ref, kg_ref[...], dk_acc, PAD, res, 1, t)

            def put_v(i, _):
                sl = pl.ds(pl.multiple_of(i * ROWS, ROWS), ROWS)
                res[2, sl, :] = dv_acc[pl.ds(pl.multiple_of(PAD + i * ROWS, ROWS), ROWS), :].astype(BF16)
                return 0

            lax.fori_loop(0, t // ROWS, put_v, 0)

            @pl.when(h == 0)
            def _():
                dqg_ref[...] = jnp.zeros_like(dqg_ref)
                dkg_ref[...] = jnp.zeros_like(dkg_ref)

            dqg_ref[...] += dqg
            dkg_ref[...] += dkg

        compute()
        _write_column_groups(res, dproj_ref, out_sems, col0, n_heads, h, HEAD)

    col = lambda k: pl.BlockSpec((t, HEAD), lambda h: (0, col0 + k * n_heads + h))
    vec = pl.BlockSpec((1, HEAD), lambda h: (0, 0))
    btile = pl.BlockSpec((1, CHUNK, BAND), lambda h: (h, 0, 0))
    return pl.pallas_call(
        body, grid=(n_heads,),
        in_specs=[ANY] * (1 + len(deps)) + [col(0), col(1), col(2), vec, vec, btile,
                                            pl.BlockSpec((t, HEAD), lambda h: (0, h))],
        out_specs=[ANY, btile, vec, vec],
        out_shape=[SDS(dproj.shape, BF16), SDS((n_heads, CHUNK, BAND), F32), SDS((1, HEAD), F32), SDS((1, HEAD), F32)],
        scratch_shapes=[pltpu.VMEM((t, HEAD), BF16), pltpu.VMEM((t + PAD, HEAD), BF16), pltpu.VMEM((t + PAD, HEAD), BF16),
                        pltpu.VMEM((t, HEAD), F32), pltpu.VMEM((t + PAD, HEAD), F32), pltpu.VMEM((t + PAD, HEAD), F32),
                        pltpu.VMEM((3, t, HEAD), BF16), pltpu.SemaphoreType.DMA((3,))],
        input_output_aliases={0: 0}, compiler_params=_params(("arbitrary",)),
        name="attn_bwd")(dproj, *deps, proj, proj, proj, q_gain, k_gain, bias, dy)


def _place():
    x, y, c = lax.axis_index("x"), lax.axis_index("y"), lax.axis_index("c")
    others = [(1 - x, y), (x, 1 - y), (1 - x, 1 - y)]
    return x, y, c, others


def _chunk_of(ref, kind, chip, half, shard_shape):
    r, n = shard_shape
    hr = r // 2
    if kind == "col":
        rows = pl.ds(0, r) if half is None else pl.ds(half * hr, hr)
        return ref.at[rows, pl.ds(chip * n, n)]
    rows = pl.ds(chip * r, r) if half is None else pl.ds(chip * r + half * hr, hr)
    return ref.at[rows, :]


EFFECT = pltpu.SideEffectType.DATAFLOW_SIDE_EFFECTING


def _start_copies(name, bufs, plan, n, deps):
    nb, nd = len(bufs), len(deps)

    def body(*refs):
        send, recv, token = refs[nb + nd], refs[nb + nd + 1], refs[-1]
        for cp in plan(refs[:nb], send, recv)[0]:
            cp.start()
        token[...] = jnp.zeros_like(token)

    out = pl.pallas_call(
        body, name=name,
        out_shape=(pltpu.SemaphoreType.DMA((n,)), pltpu.SemaphoreType.DMA((n,)),
                   *[pltpu.HBM(b.shape, b.dtype) for b in bufs], SDS((8, 128), F32)),
        in_specs=[HBM] * nb + [ANY] * nd,
        out_specs=(SEM, SEM, *[HBM] * nb, pl.BlockSpec(memory_space=pltpu.VMEM)),
        input_output_aliases={i: 2 + i for i in range(nb)},
        compiler_params=pltpu.CompilerParams(has_side_effects=EFFECT),
    )(*[pltpu.with_memory_space_constraint(b, pltpu.HBM) for b in bufs], *deps)
    return out[0], out[1], list(out[2:2 + nb]), out[-1]


def _wait_copies(name, bufs, send, recv, plan, after):
    nb = len(bufs)

    def body(*refs):
        sends, recvs = plan(refs[:nb], refs[nb], refs[nb + 1])
        for cp in sends:
            cp.wait_send()
        for cp in recvs:
            cp.wait_recv()

    out = pl.pallas_call(
        body, name=name, out_shape=tuple(pltpu.HBM(b.shape, b.dtype) for b in bufs),
        in_specs=[HBM] * nb + [SEM, SEM] + [ANY] * len(after), out_specs=tuple([HBM] * nb),
        input_output_aliases={i: i for i in range(nb)},
        compiler_params=pltpu.CompilerParams(has_side_effects=EFFECT),
    )(*bufs, send, recv, *after)
    return list(out)


def _remote(src, dst, send, recv, i, dev):
    return pltpu.make_async_remote_copy(src_ref=src, dst_ref=dst, send_sem=send.at[i], recv_sem=recv.at[i],
                                        device_id=dev, device_id_type=MESH)


ALL_RELATIONS = (0, 1, 2)


def _plan_gather_ici(kinds, shapes, rels=ALL_RELATIONS):
    def plan(refs, send, recv):
        x, y, c, others = _place()
        sends, recvs = [], []
        for w, (kind, ss) in enumerate(zip(kinds, shapes)):
            for p in rels:
                px, py = others[p]
                mine = _chunk_of(refs[w], kind, 2 * x + y, c, ss)
                theirs = _chunk_of(refs[w], kind, 2 * px + py, c, ss)
                sends.append(_remote(mine, mine, send, recv, 3 * w + p, (px, py, c)))
                recvs.append(_remote(theirs, theirs, send, recv, 3 * w + p, (px, py, c)))
        return sends, recvs

    return plan, 3 * len(kinds)


def _plan_gather_pass(kinds, shapes, rels=ALL_RELATIONS):
    def plan(refs, send, recv):
        x, y, c, others = _place()
        sends, recvs = [], []
        for w, (kind, ss) in enumerate(zip(kinds, shapes)):
            for i, p in enumerate(rels):
                px, py = others[p]
                got = _chunk_of(refs[w], kind, 2 * px + py, c, ss)
                coming = _chunk_of(refs[w], kind, 2 * px + py, 1 - c, ss)
                sends.append(_remote(got, got, send, recv, len(rels) * w + i, (x, y, 1 - c)))
                recvs.append(_remote(coming, coming, send, recv, len(rels) * w + i, (x, y, 1 - c)))
        return sends, recvs

    return plan, len(rels) * len(kinds)


def _plan_pair(kinds, shapes):
    nw = len(kinds)

    def plan(refs, send, recv):
        x, y, c, _ = _place()
        sends = []
        for w, (kind, ss) in enumerate(zip(kinds, shapes)):
            for k in range(4):
                sends.append(_remote(_chunk_of(refs[w], kind, k, 1 - c, ss), refs[nw + w].at[k], send, recv,
                                     4 * w + k, (x, y, 1 - c)))
        return sends, sends

    return plan, 4 * nw


def _plan_chip(nw):
    def plan(refs, send, recv):
        x, y, c, others = _place()
        sends = []
        for w in range(nw):
            for p, (px, py) in enumerate(others):
                sends.append(_remote(refs[w].at[p], refs[nw + w].at[p], send, recv, 3 * w + p, (px, py, c)))
        return sends, sends

    return plan, 3 * nw


def _plan_share(nw):
    def plan(refs, send, recv):
        x, y, c, _ = _place()
        sends = [_remote(refs[w].at[c], refs[w].at[c], send, recv, w, (x, y, 1 - c)) for w in range(nw)]
        recvs = [_remote(refs[w].at[1 - c], refs[w].at[1 - c], send, recv, w, (x, y, 1 - c)) for w in range(nw)]
        return sends, recvs

    return plan, nw


def _grad_half_spec(kind, tr, tn, nr, nn, chunk):
    if kind == "col":
        return pl.BlockSpec((tr, tn), lambda *a: (a[-1][1] * nr + a[-3], chunk(*a) * nn + a[-2]))
    return pl.BlockSpec((tr, tn), lambda *a: ((2 * chunk(*a) + a[-1][1]) * nr + a[-3], a[-2]))


def _pair_add(grad, got, kind, shard_shape, pos, name):
    r, n = shard_shape
    hr = r // 2
    tr, tn = _row_tile(hr, n), n
    nr, nn = hr // tr, n // tn
    g_spec = _grad_half_spec(kind, tr, tn, nr, nn, lambda p, i, j, pos_: pos_[2 + p])
    r_spec = pl.BlockSpec((1, tr, tn), lambda p, i, j, pos_: (pos_[2 + p], i, j))
    o_spec = pl.BlockSpec((1, tr, tn), lambda p, i, j, pos_: (p, i, j))

    def body(pos_ref, g_ref, r_ref, o_ref):
        o_ref[0] = (g_ref[...] + r_ref[0]).astype(BF16)

    return pl.pallas_call(
        body,
        grid_spec=pltpu.PrefetchScalarGridSpec(num_scalar_prefetch=1, grid=(3, nr, nn), in_specs=[g_spec, r_spec],
                                               out_specs=o_spec),
        out_shape=SDS((3, hr, n), BF16),
        compiler_params=_params(("parallel", "parallel", "parallel")), name=name)(pos, grad, got)


def _chip_add(grad, got, got16, kind, shard_shape, pos, name):
    r, n = shard_shape
    hr = r // 2
    tr, tn = _row_tile(hr, n), n
    nr, nn = hr // tr, n // tn
    g_spec = _grad_half_spec(kind, tr, tn, nr, nn, lambda i, j, pos_: pos_[0])
    r_spec = pl.BlockSpec((1, tr, tn), lambda i, j, pos_: (pos_[0], i, j))
    oth = pl.BlockSpec((3, tr, tn), lambda i, j, pos_: (0, i, j))

    def body(pos_ref, g_ref, r_ref, oth_ref, o_ref):
        own = g_ref[...] + r_ref[0]
        o_ref[0] = ((own + oth_ref[0].astype(F32)) + oth_ref[1].astype(F32)) + oth_ref[2].astype(F32)

    return pl.pallas_call(
        body,
        grid_spec=pltpu.PrefetchScalarGridSpec(num_scalar_prefetch=1, grid=(nr, nn), in_specs=[g_spec, r_spec, oth],
                                               out_specs=pl.BlockSpec((1, tr, tn), lambda i, j, pos_: (pos_[1], i, j))),
        out_shape=SDS((2, hr, n), F32), compiler_params=_params(("parallel", "parallel")),
        name=name)(pos, grad, got, got16)


def _adamw_math(w, g, m, v):
    m = ADAM_B1 * m + (1.0 - ADAM_B1) * g
    v = ADAM_B2 * v + (1.0 - ADAM_B2) * (g * g)
    m_hat = m / (1.0 - ADAM_B1 ** ADAM_STEP)
    v_hat = v / (1.0 - ADAM_B2 ** ADAM_STEP)
    return -ADAM_LR * (m_hat / (jnp.sqrt(v_hat) + ADAM_EPS) + ADAM_WD * w), m, v


def _adamw(w, g, m, v, name):
    r, n = w.shape
    tr, tn = _row_tile(r, n), n

    def body(w_ref, g_ref, m_ref, v_ref, d_ref, nm_ref, nv_ref, go_ref):
        gv = g_ref[...]
        d_ref[...], nm_ref[...], nv_ref[...] = _adamw_math(w_ref[...], gv, m_ref[...], v_ref[...])
        go_ref[...] = gv

    tile = pl.BlockSpec((tr, tn), lambda i, j: (i, j))
    return pl.pallas_call(
        body, grid=(r // tr, n // tn), in_specs=[tile] * 4, out_specs=[tile] * 4, out_shape=[SDS((r, n), F32)] * 4,
        compiler_params=_params(("parallel", "parallel")), name=name)(w, g, m, v)


def _small_allreduce_adamw(g, w, m, v, deps=()):
    length = g.shape[1]

    def body(*refs):
        g_ref, w_ref, m_ref, v_ref = refs[:4]
        gs_ref, d_ref, nm_ref, nv_ref, buf, send, recv = refs[4 + len(deps):]
        x, y, c = lax.axis_index("x"), lax.axis_index("y"), lax.axis_index("c")
        me = 4 * x + 2 * y + c
        buf[me] = g_ref[...]
        cps = []
        for d in range(1, 8):
            peer = (x ^ (d >> 2), y ^ ((d >> 1) & 1), c ^ (d & 1))
            cp = pltpu.make_async_remote_copy(src_ref=buf.at[me], dst_ref=buf.at[me], send_sem=send.at[d - 1],
                                              recv_sem=recv.at[d - 1], device_id=peer, device_id_type=MESH)
            cp.start()
            cps.append(cp)
        for cp in cps:
            cp.wait()
        total = buf[0]
        for d in range(1, 8):
            total = total + buf[d]
        gs_ref[...] = total
        d_ref[...], nm_ref[...], nv_ref[...] = _adamw_math(w_ref[...], total, m_ref[...], v_ref[...])

    vm = pl.BlockSpec(memory_space=pltpu.VMEM)
    return pl.pallas_call(
        body, in_specs=[vm] * 4 + [ANY] * len(deps), out_specs=[vm] * 4, out_shape=[SDS((1, length), F32)] * 4,
        scratch_shapes=[pltpu.VMEM((8, 1, length), F32), pltpu.SemaphoreType.DMA((7,)), pltpu.SemaphoreType.DMA((7,))],
        compiler_params=pltpu.CompilerParams(has_side_effects=True), name="small_allreduce_adamw")(g, w, m, v, *deps)


def kernel(x, w_in, b_gate, norm_mix, norm_ffn, hgrn_lb_logits, hgrn_out_gain, q_gain, k_gain, rel_bias, w_proj_a, w_proj_b, w_out, w_ffn_in, w_ffn_out, loss_target, m_w_in, m_b_gate, m_norm_mix, m_norm_ffn, m_hgrn_lb_logits, m_hgrn_out_gain, m_q_gain, m_k_gain, m_rel_bias, m_w_proj_a, m_w_proj_b, m_w_out, m_w_ffn_in, m_w_ffn_out, v_w_in, v_b_gate, v_norm_mix, v_norm_ffn, v_hgrn_lb_logits, v_hgrn_out_gain, v_q_gain, v_k_gain, v_rel_bias, v_w_proj_a, v_w_proj_b, v_w_out, v_w_ffn_in, v_w_ffn_out):
    t, d = x.shape[1], x.shape[2]
    d_a = hgrn_out_gain.shape[1]
    h_a = d_a // HEAD
    h_b = rel_bias.shape[1]
    d_b = h_b * HEAD
    x0 = x.reshape(t, d)
    target = loss_target.reshape(t, d)
    ax, ay = lax.axis_index("x"), lax.axis_index("y")
    pos = jnp.stack([2 * ax + ay, lax.axis_index("c"), 2 * (1 - ax) + ay, 2 * ax + 1 - ay,
                     2 * (1 - ax) + 1 - ay]).astype(jnp.int32)

    names = ["w_in", "w_proj_a", "w_proj_b", "w_out", "w_ffn_in", "w_ffn_out"]
    big = dict(zip(names, [w_in[0], w_proj_a[0], w_proj_b[0], w_out[0], w_ffn_in[0], w_ffn_out[0]]))
    big_m = dict(zip(names, [m_w_in[0], m_w_proj_a[0], m_w_proj_b[0], m_w_out[0], m_w_ffn_in[0], m_w_ffn_out[0]]))
    big_v = dict(zip(names, [v_w_in[0], v_w_proj_a[0], v_w_proj_b[0], v_w_out[0], v_w_ffn_in[0], v_w_ffn_out[0]]))
    kind = dict(zip(names, ["col", "col", "col", "row", "col", "row"]))
    shape = {nm: big[nm].shape for nm in names}

    def gather_start(tag, group, deps):
        plan, n = _plan_gather_ici([kind[g] for g in group], [shape[g] for g in group])
        fulls = [_cast_into_full(big[g], kind[g], pos, "cast_" + g) for g in group]
        send, recv, bufs, token = _start_copies("gather_ici_start_" + tag, fulls, plan, n, deps)
        return (tag, group, plan, send, recv, bufs), token

    def gather_pass(state, after, rels=ALL_RELATIONS, part=""):
        tag, group, _, send, recv, bufs = state
        kinds_, shapes_ = [kind[g] for g in group], [shape[g] for g in group]
        bufs = _wait_copies("gather_ici_wait_" + tag + part, bufs, send, recv,
                            _plan_gather_ici(kinds_, shapes_, rels)[0], after)
        plan, n = _plan_gather_pass(kinds_, shapes_, rels)
        send2, recv2, bufs, token = _start_copies("gather_pass_start_" + tag + part, bufs, plan, n, ())
        return (tag + part, group, plan, send2, recv2, bufs), token

    def gather_done(state, after):
        tag, group, plan, send, recv, bufs = state
        return _wait_copies("gather_pass_wait_" + tag, bufs, send, recv, plan, after)

    def reduce_start(tag, group, grads, deps, sent=None):
        plan, n = _plan_pair([kind[g] for g in group], [shape[g] for g in group])
        srcs = list(grads) if sent is None else list(sent)
        lands = [lax.empty((4, shape[g][0] // 2, shape[g][1]), s.dtype) for g, s in zip(group, srcs)]
        send, recv, bufs, token = _start_copies("pair_start_" + tag, srcs + lands, plan, n, deps)
        kept = None if sent is None else list(grads)
        return dict(tag=tag, group=group, plan=plan, send=send, recv=recv, bufs=bufs, kept=kept), token

    def reduce_pair_done(st, after):
        tag, group, nw = st["tag"], st["group"], len(st["group"])
        bufs = _wait_copies("pair_wait_" + tag, st["bufs"], st["send"], st["recv"], st["plan"], after)
        grads, gots = (bufs[:nw] if st["kept"] is None else st["kept"]), bufs[nw:]
        parts = [_pair_add(g, l, kind[nm], shape[nm], pos, "pair_add_" + nm) for g, l, nm in zip(grads, gots, group)]
        lands = [lax.empty((3, shape[g][0] // 2, shape[g][1]), BF16) for g in group]
        plan, n = _plan_chip(nw)
        send, recv, bufs, token = _start_copies("chip_start_" + tag, parts + lands, plan, n, ())
        return dict(st, plan=plan, send=send, recv=recv, bufs=bufs, grads=grads, gots=gots), token

    def reduce_chip_done(st, after):
        tag, group, nw = st["tag"], st["group"], len(st["group"])
        bufs = _wait_copies("chip_wait_" + tag, st["bufs"], st["send"], st["recv"], st["plan"], after)
        finals = [_chip_add(g, l, got16, kind[nm], shape[nm], pos, "chip_add_" + nm)
                  for g, l, got16, nm in zip(st["grads"], st["gots"], bufs[nw:], group)]
        plan, n = _plan_share(nw)
        send, recv, bufs, token = _start_copies("share_start_" + tag, finals, plan, n, ())
        return dict(st, plan=plan, send=send, recv=recv, bufs=bufs), token

    g_big, upd = {}, {}

    def reduce_finish(st, after):
        bufs = _wait_copies("share_wait_" + st["tag"], st["bufs"], st["send"], st["recv"], st["plan"], after)
        for full, nm in zip(bufs, st["group"]):
            upd[nm] = _adamw(big[nm], full.reshape(shape[nm]), big_m[nm], big_v[nm], "adamw_" + nm)
            g_big[nm] = upd[nm][3]

    ga, token = gather_start("a", ["w_in"], ())
    gb, token = gather_start("b", ["w_proj_a", "w_proj_b", "w_out"], (token,))
    gc, token = gather_start("c", ["w_ffn_in"], (token,))
    gd, token = gather_start("d", ["w_ffn_out"], (token,))
    h1, r1 = _rmsnorm_fwd(x0, norm_mix, "rmsnorm_mix")
    rb = jnp.pad(rel_bias[0], ((0, 0), (0, REL_LANES - N_REL)))
    bias = _relbias_expand(rb).transpose(1, 0, 2)
    proj = _matmul_chunks(h1, big["w_in"], (0,), None, pos, "proj_in_own", own_shard=True)
    ici_a = ga
    ga, token = gather_pass(ici_a, (h1, bias, proj, token), rels=(0, 1), part="_near")
    (wg_in,) = gather_done(ga, ())
    proj = _matmul_chunks(h1, wg_in, (2, 3), proj, pos, "proj_in_near")
    ga, token = gather_pass(ici_a[:5] + ([wg_in],), (proj,), rels=(2,), part="_far")
    (wg_in,) = gather_done(ga, ())
    proj = _matmul_chunks(h1, wg_in, (4,), proj, pos, "proj_in_far")
    y_a, o_pre, states = _hgrn_fwd(proj, hgrn_lb_logits, hgrn_out_gain, h_a)
    gb, token = gather_pass(gb, (y_a,))
    col_b = 4 * d_a // HEAD
    y_b = _attn_fwd(proj, q_gain, k_gain, bias, h_b, col_b)
    wg_pa, wg_pb, wg_out = gather_done(gb, (y_b,))
    gate_off = 4 * d_a + 3 * d_b
    pa, pb, merged = _proj_merge(y_a, y_b, wg_pa, wg_pb, proj, b_gate, gate_off, deps=(token,))
    x2 = _matmul(merged, wg_out, res=x0, name="out_proj")
    gc, token = gather_pass(gc, (x2,))
    h2, r2 = _rmsnorm_fwd(x2, norm_ffn, "rmsnorm_ffn")
    (wg_fin,) = gather_done(gc, (h2,))
    ff_gate, ff_up, act = _ffn_in_swiglu(h2, wg_fin, deps=(token,))
    gd, token = gather_pass(gd, (act,))
    (wg_fout,) = gather_done(gd, ())
    dy, dy16, loss_part = _ffn_out_loss(act, wg_fout, x2, target)

    g_fout, g_fout16 = _matmul(act, dy16, ta=True, name="dw_ffn_out", also_bf16=True)
    r_fout, token = reduce_start("fout", ["w_ffn_out"], [g_fout], (), sent=[g_fout16])
    dgu = _d_act_swiglu(dy16, wg_fout, ff_gate, ff_up, deps=(token,))
    r_fout, token = reduce_pair_done(r_fout, (dgu,))
    g_fin, g_fin16 = _matmul(h2, dgu, ta=True, name="dw_ffn_in", deps=(token,), also_bf16=True)
    r_fin, token = reduce_start("fin", ["w_ffn_in"], [g_fin], (), sent=[g_fin16])
    dh2 = _matmul(dgu, wg_fin, tb=True, name="d_h2", deps=(token,))
    r_fout, token_a = reduce_chip_done(r_fout, (dh2,))
    r_fin, token_b = reduce_pair_done(r_fin, (dh2,))
    dx2, dx2_16, g_norm_ffn = _rmsnorm_bwd(dh2, x2, r2, norm_ffn, dy, "rmsnorm_ffn_bwd", deps=(token_a, token_b))
    dp_ab, dproj, g_bgate = _d_merged_gates(dx2_16, wg_out, proj, b_gate, pa, pb, gate_off)
    g_out = _matmul(merged, dx2_16, ta=True, name="dw_out")
    g_pa = _matmul(y_a, dp_ab, ta=True, name="dw_proj_a", b_lead=0)
    g_pb = _matmul(y_b, dp_ab, ta=True, name="dw_proj_b", b_lead=1)
    r_mid, token = reduce_start("mid", ["w_proj_a", "w_proj_b", "w_out"], [g_pa, g_pb, g_out], ())
    dy_a = _matmul(dp_ab, wg_pa, tb=True, name="d_y_a", deps=(token,), a_lead=0)
    dy_b = _matmul(dp_ab, wg_pb, tb=True, name="d_y_b", a_lead=1)
    r_mid, token_b = reduce_pair_done(r_mid, (dy_b,))
    dproj, dbias, g_qg, g_kg = _attn_bwd(dproj, proj, q_gain, k_gain, bias, dy_b, h_b, col_b, deps=(token_b,))
    r_fin, token_a = reduce_chip_done(r_fin, (dbias,))
    r_mid, token = reduce_chip_done(r_mid, (dbias,))
    dproj, g_lb, g_gain = _hgrn_bwd(dproj, proj, o_pre, states, dy_a, hgrn_lb_logits, hgrn_out_gain, h_a,
                                    deps=(token, token_a))
    g_in, g_in16 = _matmul(h1, dproj, ta=True, name="dw_in", also_bf16=True)
    r_in, token = reduce_start("in", ["w_in"], [g_in], (), sent=[g_in16])
    g_rb = _relbias_reduce(dbias.transpose(1, 0, 2))[:, :N_REL]
    reduce_finish(r_mid, (token,))
    r_in, token = reduce_pair_done(r_in, (g_rb, upd["w_out"][0]))
    dh1 = _matmul(dproj, wg_in, tb=True, name="d_h1", deps=(token,))
    dx, _, g_norm_mix = _rmsnorm_bwd(dh1, x0, r1, norm_mix, dx2, "rmsnorm_mix_bwd")
    reduce_finish(r_fin, (dx,))
    reduce_finish(r_fout, (dx,))
    r_in, token = reduce_chip_done(r_in, (upd["w_ffn_in"][0], upd["w_ffn_out"][0], upd["w_proj_a"][0],
                                          upd["w_proj_b"][0]))

    small_w = [b_gate, norm_mix, norm_ffn, hgrn_lb_logits, hgrn_out_gain, q_gain, k_gain, rel_bias]
    small_m = [m_b_gate, m_norm_mix, m_norm_ffn, m_hgrn_lb_logits, m_hgrn_out_gain, m_q_gain, m_k_gain, m_rel_bias]
    small_v = [v_b_gate, v_norm_mix, v_norm_ffn, v_hgrn_lb_logits, v_hgrn_out_gain, v_q_gain, v_k_gain, v_rel_bias]
    small_g = [g_bgate, g_norm_mix, g_norm_ffn, g_lb, g_gain, g_qg, g_kg, g_rb]
    sizes = [w.size for w in small_w]
    length = -(-(sum(sizes) + 1) // 128) * 128

    def pack(parts_):
        flat = jnp.concatenate([p.reshape(1, -1) for p in parts_], axis=1)
        return jnp.pad(flat, ((0, 0), (0, length - flat.shape[1])))

    one = jnp.ones((1, 1), F32)
    packed = _small_allreduce_adamw(pack(small_g + [loss_part]), pack(small_w + [one]), pack(small_m + [one]),
                                    pack(small_v + [one]), deps=(token,))

    def unpack(vec):
        out, at = [], 0
        for w, n in zip(small_w, sizes):
            out.append(vec[0, at:at + n].reshape(w.shape))
            at += n
        return out, vec[0, at]

    (sg, loss), (sd, _), (sm, _), (sv, _) = [unpack(p) for p in packed]
    reduce_finish(r_in, (packed[0],))

    def ordered(small, bigs):
        bigs = [bigs[nm][None] for nm in names]
        return [bigs[0]] + small + bigs[1:]

    return (loss, dx.reshape(x.shape), *ordered(sg, g_big), *ordered(sd, {nm: upd[nm][0] for nm in names}),
            *ordered(sm, {nm: upd[nm][1] for nm in names}), *ordered(sv, {nm: upd[nm][2] for nm in names}))
```

```python
import functools

import jax
import jax.numpy as jnp
from jax import lax
from jax.experimental import pallas as pl
from jax.experimental.pallas import tpu as pltpu

F32 = jnp.float32
BF16 = jnp.bfloat16
SDS = jax.ShapeDtypeStruct
MESH = pl.DeviceIdType.MESH
HIGHEST = lax.Precision.HIGHEST

CHUNK = 64
SUB = 16
HEAD = 128
N_PAST = 8
BAND = (N_PAST + 1) * CHUNK
PAD = N_PAST * CHUNK
REL_FUTURE = CHUNK - 1
REL_PAST = 2 * CHUNK - 1
N_REL = REL_FUTURE + REL_PAST + 1
REL_LANES = 256
EPS = 1e-6
MIX_HEADS = 2
MIX_UNROLL = 4
MIX_UNROLL_BWD = 4
ATT_UNROLL = 8
ATT_UNROLL_BWD = 8
EXP_CLAMP = 80.0

ADAM_LR = 0.001
ADAM_B1 = 0.9
ADAM_B2 = 0.999
ADAM_EPS = 1e-08
ADAM_WD = 0.01
ADAM_STEP = 10

VMEM_LIMIT = 56 * 1024 * 1024

HBM = pl.BlockSpec(memory_space=pltpu.HBM)
ANY = pl.BlockSpec(memory_space=pl.ANY)
SEM = pl.BlockSpec(memory_space=pltpu.SEMAPHORE)

NT = (((1,), (1,)), ((), ()))
TN = (((0,), (0,)), ((), ()))
NN = (((1,), (0,)), ((), ()))


def _params(sem=None, **kw):
    return pltpu.CompilerParams(dimension_semantics=sem, vmem_limit_bytes=VMEM_LIMIT, **kw)


def _tile(n, pref, unit=128):
    if n <= pref:
        return n
    t = pref - pref % unit
    while n % t:
        t -= unit
    return t


STREAM_BLOCK = 256 * 1408


def _row_tile(rows, cols):
    return _tile(rows, max(16, STREAM_BLOCK // cols), 16)


def _sigmoid(x):
    return 1.0 / (1.0 + jnp.exp(-x))


def _dsilu(x, s):
    return s * (1.0 + x * (1.0 - s))


def _split(a):
    hi = a.astype(BF16)
    return hi, (a - hi.astype(F32)).astype(BF16)


def _dot3(a, b, dims):
    dot = lambda u, v: lax.dot_general(u, v, dims, preferred_element_type=F32)
    return dot(a[0], b[1]) + dot(a[1], b[0]) + dot(a[0], b[0])


def _fdot(a, b):
    return lax.dot_general(a, b, NN, precision=HIGHEST, preferred_element_type=F32)


MM_TILE_K = 5632
MM_TILE_N = 512


def _matmul_chunks(h, w, which, prev, pos, name, own_shard=False, deps=()):
    t, d = h.shape
    nc_ = w.shape[1] if own_shard else w.shape[1] // 4
    tm, tn = _tile(t, 1024), _tile(nc_, 1408)
    nn = nc_ // tn

    def chunk(q, p):
        sel = p[which[0]]
        for i in range(1, len(which)):
            sel = jnp.where(q == i, p[which[i]], sel)
        return sel

    def body(p_ref, h_ref, w_ref, *rest):
        rest[-1][...] = jnp.dot(h_ref[...], w_ref[...].astype(BF16), preferred_element_type=F32)

    if own_shard:
        w_spec = pl.BlockSpec((d, tn), lambda q, i, j, p: (0, j))
    else:
        w_spec = pl.BlockSpec((d, tn), lambda q, i, j, p: (0, chunk(q, p) * nn + j))
    n_extra = len(deps) + (prev is not None)
    return pl.pallas_call(
        body,
        grid_spec=pltpu.PrefetchScalarGridSpec(
            num_scalar_prefetch=1, grid=(len(which), t // tm, nn),
            in_specs=[pl.BlockSpec((tm, d), lambda q, i, j, p: (i, 0)), w_spec] + [ANY] * n_extra,
            out_specs=pl.BlockSpec((tm, tn), lambda q, i, j, p: (i, chunk(q, p) * nn + j))),
        out_shape=SDS((t, 4 * nc_), F32), input_output_aliases={3 + len(deps): 0} if prev is not None else {},
        compiler_params=_params(("arbitrary", "arbitrary", "arbitrary")),
        name=name)(pos, h, w, *deps, *(() if prev is None else (prev,)))


def _matmul(a, b, *, ta=False, tb=False, res=None, out_dtype=F32, name, deps=(), a_lead=None, b_lead=None,
            also_bf16=False):
    a2, b2 = a.shape[-2:], b.shape[-2:]
    m, k = (a2[1], a2[0]) if ta else a2
    n = b2[0] if tb else b2[1]
    if k > MM_TILE_K:
        tk, tm, tn = _tile(k, MM_TILE_K // 2), _tile(m, 1024), _tile(n, 1024)
    else:
        tk = k
        tm, tn = _tile(m, 2048 if tk <= MM_TILE_K // 2 else 1024), _tile(n, MM_TILE_N)
    nk = k // tk
    dims = ((((0,) if ta else (1,)), ((1,) if tb else (0,))), ((), ()))

    def body(*refs):
        n_in = 2 + (res is not None)
        a_ref, b_ref = refs[:2]
        r_ref = refs[2] if res is not None else None
        o_ref = refs[n_in + len(deps)]
        part = lax.dot_general(a_ref[...].astype(BF16), b_ref[...].astype(BF16), dims, preferred_element_type=F32)

        def finish(out):
            if r_ref is not None:
                out = out + r_ref[...]
            o_ref[...] = out.astype(o_ref.dtype)
            if also_bf16:
                refs[n_in + len(deps) + 1][...] = out.astype(BF16)

        if nk == 1:
            finish(part)
            return
        acc_ref = refs[-1]
        kk = pl.program_id(2)

        @pl.when(kk == 0)
        def _():
            acc_ref[...] = part

        @pl.when(jnp.logical_and(kk > 0, kk < nk - 1))
        def _():
            acc_ref[...] += part

        @pl.when(kk == nk - 1)
        def _():
            finish(acc_ref[...] + part)

    def spec(block, index, lead):
        if lead is None:
            return pl.BlockSpec(block, index)
        return pl.BlockSpec((None,) + block, lambda i, j, l: (lead,) + index(i, j, l))

    a_spec = spec((tk, tm), lambda i, j, l: (l, i), a_lead) if ta else spec((tm, tk), lambda i, j, l: (i, l), a_lead)
    b_spec = spec((tn, tk), lambda i, j, l: (j, l), b_lead) if tb else spec((tk, tn), lambda i, j, l: (l, j), b_lead)
    o_spec = pl.BlockSpec((tm, tn), lambda i, j, l: (i, j))
    in_specs = [a_spec, b_spec] + ([o_spec] if res is not None else []) + [ANY] * len(deps)
    args = (a, b) + ((res,) if res is not None else ()) + tuple(deps)
    out_specs, out_shape = o_spec, SDS((m, n), out_dtype)
    if also_bf16:
        out_specs, out_shape = [o_spec, o_spec], [out_shape, SDS((m, n), BF16)]
    return pl.pallas_call(
        body, grid=(m // tm, n // tn, nk), in_specs=in_specs, out_specs=out_specs,
        out_shape=out_shape, scratch_shapes=[pltpu.VMEM((tm, tn), F32)] if nk > 1 else [],
        compiler_params=_params(("parallel", "parallel", "arbitrary")), name=name)(*args)


def _cast_into_full(w, kind, pos, name):
    r, n = w.shape
    tr = _tile(r, 512, 16)
    nr = r // tr
    if kind == "col":
        shape, o_spec = (r, 4 * n), pl.BlockSpec((tr, n), lambda i, p: (i, p[0]))
    else:
        shape, o_spec = (4 * r, n), pl.BlockSpec((tr, n), lambda i, p: (p[0] * nr + i, 0))

    def body(p_ref, w_ref, o_ref):
        o_ref[...] = w_ref[...].astype(BF16)

    return pl.pallas_call(
        body,
        grid_spec=pltpu.PrefetchScalarGridSpec(num_scalar_prefetch=1, grid=(nr,),
                                               in_specs=[pl.BlockSpec((tr, n), lambda i, p: (i, 0))], out_specs=o_spec),
        out_shape=SDS(shape, BF16), compiler_params=_params(("parallel",)), name=name)(pos, w)


def _rmsnorm_fwd(x, gain, name):
    t, d = x.shape
    tm = _tile(t, 256)

    def body(x_ref, g_ref, h_ref, r_ref):
        xv = x_ref[...]
        r = lax.rsqrt(jnp.mean(xv * xv, axis=-1, keepdims=True) + EPS)
        h_ref[...] = (xv * r * g_ref[...]).astype(BF16)
        r_ref[...] = r

    return pl.pallas_call(
        body, grid=(t // tm,),
        in_specs=[pl.BlockSpec((tm, d), lambda i: (i, 0)), pl.BlockSpec((1, d), lambda i: (0, 0))],
        out_specs=[pl.BlockSpec((tm, d), lambda i: (i, 0)), pl.BlockSpec((tm, 1), lambda i: (i, 0))],
        out_shape=[SDS((t, d), BF16), SDS((t, 1), F32)], compiler_params=_params(("parallel",)), name=name)(x, gain)


def _rmsnorm_bwd(dh, x, r, gain, dres, name, deps=(), with_bf16=True):
    t, d = x.shape
    tm = _tile(t, 256)

    def body(dh_ref, x_ref, r_ref, g_ref, dres_ref, *rest):
        outs = rest[len(deps):]
        dx_ref, dg_ref = outs[0], outs[-1]

        @pl.when(pl.program_id(0) == 0)
        def _():
            dg_ref[...] = jnp.zeros_like(dg_ref)

        dhv, xv, rv = dh_ref[...], x_ref[...], r_ref[...]
        dg_ref[...] += jnp.sum(dhv * (xv * rv), axis=0, keepdims=True)
        u = dhv * g_ref[...]
        dx = dres_ref[...] + rv * u - xv * (rv * rv * rv) * jnp.mean(u * xv, axis=-1, keepdims=True)
        dx_ref[...] = dx
        if with_bf16:
            outs[1][...] = dx.astype(BF16)

    row = pl.BlockSpec((tm, d), lambda i: (i, 0))
    vec = pl.BlockSpec((1, d), lambda i: (0, 0))
    copies = [SDS((t, d), F32)] + ([SDS((t, d), BF16)] if with_bf16 else [])
    return pl.pallas_call(
        body, grid=(t // tm,),
        in_specs=[row, row, pl.BlockSpec((tm, 1), lambda i: (i, 0)), vec, row] + [ANY] * len(deps),
        out_specs=[row] * len(copies) + [vec], out_shape=copies + [SDS((1, d), F32)],
        compiler_params=_params(("arbitrary",)), name=name)(dh, x, r, gain, dres, *deps)


def _proj_merge(y_a, y_b, w_a, w_b, proj, b_gate, off, deps=()):
    t, ka = y_a.shape
    kb = y_b.shape[1]
    d = w_a.shape[1]
    tm, tc = _tile(t, 1024), _tile(d, MM_TILE_N)
    nj = d // tc
    oa, ob = off // tc, off // tc + nj

    def body(ya_ref, yb_ref, wa_ref, wb_ref, la_ref, lb_ref, ba_ref, bb_ref, *rest):
        pa_ref, pb_ref, o_ref = rest[len(deps):]
        pa = jnp.dot(ya_ref[...], wa_ref[...], preferred_element_type=F32)
        pb = jnp.dot(yb_ref[...], wb_ref[...], preferred_element_type=F32)
        pa_ref[...] = pa
        pb_ref[...] = pb
        ga = _sigmoid(la_ref[...] + ba_ref[...])
        gb = _sigmoid(lb_ref[...] + bb_ref[...])
        o_ref[...] = (ga * pa + gb * pb).astype(BF16)

    tile = pl.BlockSpec((tm, tc), lambda i, j: (i, j))
    return pl.pallas_call(
        body, grid=(t // tm, nj),
        in_specs=[pl.BlockSpec((tm, ka), lambda i, j: (i, 0)), pl.BlockSpec((tm, kb), lambda i, j: (i, 0)),
                  pl.BlockSpec((ka, tc), lambda i, j: (0, j)), pl.BlockSpec((kb, tc), lambda i, j: (0, j)),
                  pl.BlockSpec((tm, tc), lambda i, j: (i, oa + j)), pl.BlockSpec((tm, tc), lambda i, j: (i, ob + j)),
                  pl.BlockSpec((1, tc), lambda i, j: (0, j)), pl.BlockSpec((1, tc), lambda i, j: (0, nj + j))]
        + [ANY] * len(deps),
        out_specs=[tile, tile, tile], out_shape=[SDS((t, d), F32), SDS((t, d), F32), SDS((t, d), BF16)],
        compiler_params=_params(("parallel", "parallel")),
        name="proj_merge")(y_a, y_b, w_a, w_b, proj, proj, b_gate, b_gate, *deps)


def _d_merged_gates(dx, w, proj, b_gate, pa, pb, off, deps=()):
    t, k = dx.shape
    d = w.shape[0]
    tm, tc = _tile(t, 1024), _tile(d, MM_TILE_N)
    nj, ni = d // tc, t // tm
    o0 = off // tc

    def body(dx_ref, w_ref, la_ref, lb_ref, ba_ref, bb_ref, pa_ref, pb_ref, *rest):
        dp_ref, dproj_ref, db_ref, stage, sems = rest[len(deps):]
        j, i = pl.program_id(0), pl.program_id(1)
        step = j * ni + i
        slot = step % 2

        def copies(s, ii, jj):
            rows = pl.ds(pl.multiple_of(ii * tm, tm), tm)
            return [pltpu.make_async_copy(
                stage.at[s, br], dproj_ref.at[rows, pl.ds(pl.multiple_of(off + br * d + jj * tc, 128), tc)],
                sems.at[s, br]) for br in range(2)]

        @pl.when(step >= 2)
        def _():
            for cp in copies(slot, 0, 0):
                cp.wait()

        dm = lax.dot_general(dx_ref[...], w_ref[...], NT, preferred_element_type=F32)

        @pl.when(i == 0)
        def _():
            db_ref[...] = jnp.zeros_like(db_ref)

        for br, (l_ref, b_ref, p_ref) in enumerate(((la_ref, ba_ref, pa_ref), (lb_ref, bb_ref, pb_ref))):
            g = _sigmoid(l_ref[...] + b_ref[...])
            dp_ref[br] = (dm * g).astype(BF16)
            dl = dm * p_ref[...] * g * (1.0 - g)
            stage[slot, br] = dl.astype(BF16)
            db_ref[br] += jnp.sum(dl, axis=0, keepdims=True)
        for cp in copies(slot, i, j):
            cp.start()

        @pl.when(step == ni * nj - 1)
        def _():
            for s in range(min(2, ni * nj)):
                for cp in copies(s, 0, 0):
                    cp.wait()

    tile = pl.BlockSpec((tm, tc), lambda j, i: (i, j))
    return pl.pallas_call(
        body, grid=(nj, ni),
        in_specs=[pl.BlockSpec((tm, k), lambda j, i: (i, 0)), pl.BlockSpec((tc, k), lambda j, i: (j, 0)),
                  pl.BlockSpec((tm, tc), lambda j, i: (i, o0 + j)), pl.BlockSpec((tm, tc), lambda j, i: (i, o0 + nj + j)),
                  pl.BlockSpec((1, tc), lambda j, i: (0, j)), pl.BlockSpec((1, tc), lambda j, i: (0, nj + j)),
                  tile, tile] + [ANY] * len(deps),
        out_specs=[pl.BlockSpec((2, tm, tc), lambda j, i: (0, i, j)), ANY, pl.BlockSpec((2, 1, tc), lambda j, i: (0, 0, j))],
        out_shape=[SDS((2, t, d), BF16), SDS(proj.shape, BF16), SDS((2, 1, d), F32)],
        scratch_shapes=[pltpu.VMEM((2, 2, tm, tc), BF16), pltpu.SemaphoreType.DMA((2, 2))],
        compiler_params=_params(("arbitrary", "arbitrary")),
        name="d_merged_gates")(dx, w, proj, proj, b_gate, b_gate, pa, pb, *deps)


def _ffn_in_swiglu(h, w, deps=()):
    t, d = h.shape
    f = w.shape[1] // 2
    tm, tn = _tile(t, 2048), _tile(f, MM_TILE_N)
    nj = f // tn

    def body(h_ref, wg_ref, wu_ref, *rest):
        g_ref, u_ref, a_ref = rest[len(deps):]
        hv = h_ref[...]
        g = jnp.dot(hv, wg_ref[...], preferred_element_type=F32)
        u = jnp.dot(hv, wu_ref[...], preferred_element_type=F32)
        g_ref[...] = g
        u_ref[...] = u
        a_ref[...] = (g * _sigmoid(g) * u).astype(BF16)

    tile = pl.BlockSpec((tm, tn), lambda i, j: (i, j))
    return pl.pallas_call(
        body, grid=(t // tm, nj),
        in_specs=[pl.BlockSpec((tm, d), lambda i, j: (i, 0)), pl.BlockSpec((d, tn), lambda i, j: (0, j)),
                  pl.BlockSpec((d, tn), lambda i, j: (0, nj + j))] + [ANY] * len(deps),
        out_specs=[tile, tile, tile], out_shape=[SDS((t, f), F32), SDS((t, f), F32), SDS((t, f), BF16)],
        compiler_params=_params(("parallel", "parallel")), name="ffn_in_swiglu")(h, w, w, *deps)


def _d_act_swiglu(dy, w, gate, up, deps=()):
    t, k = dy.shape
    f = w.shape[0]
    tm, tn = _tile(t, 1024), _tile(f, MM_TILE_N)
    ni, nj = t // tm, f // tn

    def body(dy_ref, w_ref, g_ref, u_ref, *rest):
        out_ref, stage, sems = rest[len(deps):]
        i, j = pl.program_id(0), pl.program_id(1)
        step = i * nj + j
        slot = step % 2

        def copies(s, ii, jj):
            rows = pl.ds(pl.multiple_of(ii * tm, tm), tm)
            return [pltpu.make_async_copy(
                stage.at[s, half], out_ref.at[rows, pl.ds(pl.multiple_of(half * f + jj * tn, 128), tn)],
                sems.at[s, half]) for half in range(2)]

        @pl.when(step >= 2)
        def _():
            for cp in copies(slot, 0, 0):
                cp.wait()

        dact = lax.dot_general(dy_ref[...], w_ref[...], NT, preferred_element_type=F32)
        g = g_ref[...]
        sg = _sigmoid(g)
        stage[slot, 0] = (dact * u_ref[...] * _dsilu(g, sg)).astype(BF16)
        stage[slot, 1] = (dact * (g * sg)).astype(BF16)
        for cp in copies(slot, i, j):
            cp.start()

        @pl.when(step == ni * nj - 1)
        def _():
            for s in range(min(2, ni * nj)):
                for cp in copies(s, 0, 0):
                    cp.wait()

    tile = pl.BlockSpec((tm, tn), lambda i, j: (i, j))
    return pl.pallas_call(
        body, grid=(ni, nj),
        in_specs=[pl.BlockSpec((tm, k), lambda i, j: (i, 0)), pl.BlockSpec((tn, k), lambda i, j: (j, 0)), tile, tile]
        + [ANY] * len(deps),
        out_specs=ANY, out_shape=SDS((t, 2 * f), BF16),
        scratch_shapes=[pltpu.VMEM((2, 2, tm, tn), BF16), pltpu.SemaphoreType.DMA((2, 2))],
        compiler_params=_params(("arbitrary", "arbitrary")), name="d_act_swiglu")(dy, w, gate, up, *deps)


def _ffn_out_loss(act, w, x_res, target):
    t, d = x_res.shape
    k = act.shape[1]
    tm, tn = _tile(t, 1024), _tile(d, MM_TILE_N)

    def body(a_ref, w_ref, r_ref, t_ref, dy_ref, dyb_ref, l_ref):
        @pl.when(jnp.logical_and(pl.program_id(0) == 0, pl.program_id(1) == 0))
        def _():
            l_ref[...] = jnp.zeros_like(l_ref)

        y = jnp.dot(a_ref[...], w_ref[...], preferred_element_type=F32) + r_ref[...]
        e = y - t_ref[...]
        dy = e * (1.0 / d)
        dy_ref[...] = dy
        dyb_ref[...] = dy.astype(BF16)
        l_ref[...] += (0.5 / d) * jnp.sum(jnp.sum(e * e, axis=-1, keepdims=True), axis=0, keepdims=True)

    tile = pl.BlockSpec((tm, tn), lambda i, j: (i, j))
    return pl.pallas_call(
        body, grid=(t // tm, d // tn),
        in_specs=[pl.BlockSpec((tm, k), lambda i, j: (i, 0)), pl.BlockSpec((k, tn), lambda i, j: (0, j)), tile, tile],
        out_specs=[tile, tile, pl.BlockSpec((1, 1), lambda i, j: (0, 0))],
        out_shape=[SDS((t, d), F32), SDS((t, d), BF16), SDS((1, 1), F32)],
        compiler_params=_params(("arbitrary", "arbitrary")), name="ffn_out_loss")(act, w, x_res, target)


def _rel_onehot(qi):
    p = lax.broadcasted_iota(jnp.int32, (REL_LANES, BAND), 1)
    r = lax.broadcasted_iota(jnp.int32, (REL_LANES, BAND), 0)
    idx = jnp.clip(qi + PAD - p, -REL_FUTURE, REL_PAST) + REL_FUTURE
    return (idx == r).astype(F32)


def _relbias_expand(rb):
    h = rb.shape[0]

    def body(rb_ref, o_ref):
        def step(qi, _):
            o_ref[qi] = _fdot(rb_ref[...], _rel_onehot(qi))
            return 0

        lax.fori_loop(0, CHUNK, step, 0)

    return pl.pallas_call(body, out_shape=SDS((CHUNK, h, BAND), F32), compiler_params=_params(),
                          name="relbias_expand")(rb)


def _relbias_reduce(dbias):
    h = dbias.shape[1]

    rows_per_pass = 4

    def body(db_ref, o_ref):
        def step(i, acc):
            parts = []
            for u in range(rows_per_pass):
                qi = i * rows_per_pass + u
                xv = db_ref[qi]
                hi = xv.astype(BF16)
                rest = xv - hi.astype(F32)
                mid = rest.astype(BF16)
                low = (rest - mid.astype(F32)).astype(BF16)
                parts.append(lax.dot_general(jnp.concatenate([hi, mid, low], axis=0), _rel_onehot(qi).astype(BF16), NT,
                                             preferred_element_type=F32))
            for part in parts:
                acc = acc + (part[0:h] + part[h:2 * h] + part[2 * h:3 * h])
            return acc

        o_ref[...] = lax.fori_loop(0, CHUNK // rows_per_pass, step, jnp.zeros((h, REL_LANES), F32))

    return pl.pallas_call(body, out_shape=SDS((h, REL_LANES), F32), compiler_params=_params(),
                          name="relbias_reduce")(dbias)


def _lower_bound(l_ref):
    l0, l1 = l_ref[0:1, :], l_ref[1:2, :]
    m = jnp.maximum(l0, l1)
    e0, e1 = jnp.exp(l0 - m), jnp.exp(l1 - m)
    return e0 / (e0 + e1)


def _tri(lower):
    r = lax.broadcasted_iota(jnp.int32, (CHUNK, CHUNK), 0)
    c = lax.broadcasted_iota(jnp.int32, (CHUNK, CHUNK), 1)
    return r >= c if lower else r <= c


def _hgrn_intra(qs, kk, b_s):
    rows = lax.broadcasted_iota(jnp.int32, (CHUNK, HEAD), 0)
    b = b_s[...]
    out = []
    for i in range(CHUNK // SUB):
        lo = i * SUB
        ref = jnp.zeros((1, HEAD), F32) if i == 0 else b_s[lo - 1:lo, :]
        eq = jnp.exp(b[lo:lo + SUB] - ref)
        qt = _split(qs[lo:lo + SUB] * eq)
        e = jnp.where(rows < lo + SUB, jnp.exp(jnp.minimum(ref - b, EXP_CLAMP)), 0.0)
        kt = _split(kk * e)
        out.append((eq, qt, e, kt))
    return out


def _hgrn_scores(blocks):
    tr = lax.broadcasted_iota(jnp.int32, (SUB, CHUNK), 0)
    tc = lax.broadcasted_iota(jnp.int32, (SUB, CHUNK), 1)
    return jnp.concatenate([jnp.where(tc <= tr + i * SUB, _dot3(qt, kt, NT), 0.0)
                            for i, (_, qt, _, kt) in enumerate(blocks)], axis=0)


def _hgrn_fwd(proj, lb_logits, gain, n_heads):
    t = proj.shape[0]
    nc = t // CHUNK
    da = n_heads * HEAD
    hp = MIX_HEADS
    wide = hp * HEAD

    def body(q_ref, f_ref, i_ref, g_ref, l_ref, gain_ref, y_ref, o_ref, st_ref, state, b_s):
        state[...] = jnp.zeros_like(state)
        lb_all = _lower_bound(l_ref)
        tril = _tri(True).astype(F32)

        def chunks(i, _):
            dot = functools.partial(lax.dot_general, preferred_element_type=F32)
            items = []
            for u in range(MIX_UNROLL):
                for hh in range(hp):
                    j = i * MIX_UNROLL + u
                    sl = pl.ds(pl.multiple_of(j * CHUNK, CHUNK), CHUNK)
                    cols = slice(hh * HEAD, (hh + 1) * HEAD)
                    lb = lb_all[:, cols]
                    fg = lb + (1.0 - lb) * _sigmoid(f_ref[sl, cols])
                    qv = q_ref[sl, cols]
                    gv = g_ref[sl, cols]
                    items.append(dict(hh=hh, j=j, sl=sl, cols=cols, lf=jnp.log(fg), kk=1.0 - fg, qs=qv * _sigmoid(qv),
                                      vb=i_ref[sl, cols].astype(BF16), gate=gv * _sigmoid(gv)))
            for it in items:
                it["b"] = _fdot(tril, it["lf"])
            for slot, it in enumerate(items):
                b = it["b"]
                b_s[slot] = b
                it["blocks"] = _hgrn_intra(it["qs"], it["kk"], b_s.at[slot])
                it["ebl"] = jnp.exp(b_s[slot, CHUNK - 1:CHUNK, :])
                it["qe"] = (it["qs"] * jnp.exp(b)).astype(BF16)
                it["kd"] = (it["kk"] * jnp.exp(b_s[slot, CHUNK - 1:CHUNK, :] - b)).astype(BF16)
            for it in items:
                it["a"] = _hgrn_scores(it["blocks"]).astype(BF16)
            for it in items:
                it["kv"] = dot(it["vb"], it["kd"], TN)
                it["o"] = dot(it["a"], it["vb"], NN)
            s_now = [state[hh] for hh in range(hp)]
            for it in items:
                it["s_in"] = s_now[it["hh"]]
                s_now[it["hh"]] = it["s_in"] * it["ebl"] + it["kv"]
            for hh in range(hp):
                state[hh] = s_now[hh]
            for it in items:
                it["o"] = it["o"] + dot(it["qe"], it["s_in"].astype(BF16), NT)
            for it in items:
                o, sl, cols = it["o"], it["sl"], it["cols"]
                st_ref[it["hh"], it["j"]] = it["s_in"]
                o_ref[sl, cols] = o
                rr = lax.rsqrt(jnp.mean(o * o, axis=-1, keepdims=True) + EPS)
                y_ref[sl, cols] = (o * rr * gain_ref[:, cols] * it["gate"]).astype(BF16)
            return 0

        assert nc % MIX_UNROLL == 0, (nc, MIX_UNROLL)
        lax.fori_loop(0, nc // MIX_UNROLL, chunks, 0)

    col = lambda k: pl.BlockSpec((t, wide), lambda h: (0, k * (n_heads // hp) + h))
    vec = pl.BlockSpec((1, wide), lambda h: (0, h))
    return pl.pallas_call(
        body, grid=(n_heads // hp,),
        in_specs=[col(0), col(1), col(2), col(3), pl.BlockSpec((2, wide), lambda h: (0, h)), vec],
        out_specs=[pl.BlockSpec((t, wide), lambda h: (0, h)), pl.BlockSpec((t, wide), lambda h: (0, h)),
                   pl.BlockSpec((hp, nc, HEAD, HEAD), lambda h: (h, 0, 0, 0))],
        out_shape=[SDS((t, da), BF16), SDS((t, da), F32), SDS((n_heads, nc, HEAD, HEAD), F32)],
        scratch_shapes=[pltpu.VMEM((hp, HEAD, HEAD), F32), pltpu.VMEM((hp * MIX_UNROLL, CHUNK, HEAD), F32)],
        compiler_params=_params(("parallel",)), name="hgrn_fwd")(proj, proj, proj, proj, lb_logits, gain)


def _write_column_groups(res, dproj_ref, sems, col0, stride, h, width):
    copies = [pltpu.make_async_copy(
        res.at[p], dproj_ref.at[:, pl.ds(pl.multiple_of((col0 + p * stride + h) * width, HEAD), width)], sems.at[p])
        for p in range(res.shape[0])]
    for cp in copies:
        cp.start()
    for cp in copies:
        cp.wait()


def _hgrn_bwd(dproj, proj, o_pre, states, dy, lb_logits, gain, n_heads, deps=()):
    t = proj.shape[0]
    nc = t // CHUNK
    da = n_heads * HEAD
    hp = MIX_HEADS
    wide = hp * HEAD

    def body(*refs):
        (q_ref, f_ref, i_ref, g_ref, o_ref, st_ref, dy_ref, l_ref, gain_ref,
         dproj_ref, dl_ref, dgain_ref, res, dstate, b_s, out_sems) = refs[1 + len(deps):]

        def compute():
            dstate[...] = jnp.zeros_like(dstate)
            lb_all = _lower_bound(l_ref)
            tril_m, tril, triu = _tri(True), _tri(True).astype(F32), _tri(False).astype(F32)
            last = lax.broadcasted_iota(jnp.int32, (CHUNK, HEAD), 0) == CHUNK - 1

            def chunks(i, carry):
                dot = functools.partial(lax.dot_general, preferred_element_type=F32)
                items = []
                for u in range(MIX_UNROLL_BWD):
                    for hh in range(hp):
                        j = nc - 1 - (i * MIX_UNROLL_BWD + u)
                        sl = pl.ds(pl.multiple_of(j * CHUNK, CHUNK), CHUNK)
                        cols = slice(hh * HEAD, (hh + 1) * HEAD)
                        lb, gain_v = lb_all[:, cols], gain_ref[:, cols]
                        sg = _sigmoid(f_ref[sl, cols])
                        fg = lb + (1.0 - lb) * sg
                        qv = q_ref[sl, cols]
                        sq = _sigmoid(qv)
                        gv = g_ref[sl, cols]
                        sgg = _sigmoid(gv)
                        silg = gv * sgg
                        o = o_ref[sl, cols]
                        dyv = dy_ref[sl, cols]
                        rr = lax.rsqrt(jnp.mean(o * o, axis=-1, keepdims=True) + EPS)
                        on = o * rr
                        don = dyv * gain_v * silg
                        do = (rr * don - o * (rr * rr * rr) * jnp.mean(don * o, axis=-1, keepdims=True)).astype(BF16)
                        items.append(dict(
                            hh=hh, j=j, sl=sl, cols=cols, lb=lb, sg=sg, fg=fg, kk=1.0 - fg, qv=qv, sq=sq, qs=qv * sq,
                            vb=i_ref[sl, cols].astype(BF16), do=do, dg=dyv * on * gain_v * _dsilu(gv, sgg),
                            dgain=jnp.sum(dyv * on * silg, axis=0, keepdims=True)))
                for it in items:
                    it["b"] = _fdot(tril, jnp.log(it["fg"]))
                for slot, it in enumerate(items):
                    b = it["b"]
                    b_s[slot] = b
                    it["blocks"] = _hgrn_intra(it["qs"], it["kk"], b_s.at[slot])
                    bl = b_s[slot, CHUNK - 1:CHUNK, :]
                    it["eb"], it["ebl"], it["ekd"] = jnp.exp(b), jnp.exp(bl), jnp.exp(bl - b)
                    it["s_in"] = st_ref[it["hh"], it["j"]]
                for it in items:
                    it["a"] = _hgrn_scores(it["blocks"]).astype(BF16)
                    it["da"] = jnp.where(tril_m, dot(it["do"], it["vb"], NT), 0.0)
                for it in items:
                    dq_rows = []
                    dk = jnp.zeros((CHUNK, HEAD), F32)
                    for blk, (eq, qt, e, kt) in enumerate(it["blocks"]):
                        da_i = _split(it["da"][blk * SUB:(blk + 1) * SUB])
                        dq_rows.append(eq * _dot3(da_i, kt, NN))
                        dk = dk + e * _dot3(da_i, qt, TN)
                    it["dq"] = jnp.concatenate(dq_rows, axis=0) + dot(it["do"], it["s_in"].astype(BF16), NN) * it["eb"]
                    it["dk"] = dk
                    it["dv"] = dot(it["a"], it["do"], TN)
                    it["g"] = dot(it["do"], (it["qs"] * it["eb"]).astype(BF16), TN)
                ds_now = [dstate[hh] for hh in range(hp)]
                for it in items:
                    it["ds_out"] = ds_now[it["hh"]]
                    ds_now[it["hh"]] = it["ds_out"] * it["ebl"] + it["g"]
                for hh in range(hp):
                    dstate[hh] = ds_now[hh]
                for it in items:
                    dsb = it["ds_out"].astype(BF16)
                    it["dv"] = it["dv"] + dot((it["kk"] * it["ekd"]).astype(BF16), dsb, NT)
                    it["dk_state"] = it["ekd"] * dot(it["vb"], dsb, NN)
                for it in items:
                    kk, dk_state = it["kk"], it["dk_state"]
                    it["dk"] = it["dk"] + dk_state
                    extra = (jnp.sum(kk * dk_state, axis=0, keepdims=True)
                             + it["ebl"] * jnp.sum(it["s_in"] * it["ds_out"], axis=0, keepdims=True))
                    it["db"] = it["qs"] * it["dq"] - kk * it["dk"] + jnp.where(last, extra, 0.0)
                for it in items:
                    it["dlf"] = _fdot(triu, it["db"])
                carry = list(carry)
                for it in items:
                    hh, sl, cols, sg, lb = it["hh"], it["sl"], it["cols"], it["sg"], it["lb"]
                    dfg = it["dlf"] / it["fg"] - it["dk"]
                    dlb_acc, dgain_acc = carry[hh]
                    carry[hh] = (dlb_acc + jnp.sum(dfg * (1.0 - sg), axis=0, keepdims=True), dgain_acc + it["dgain"])
                    res[0, sl, cols] = (it["dq"] * _dsilu(it["qv"], it["sq"])).astype(BF16)
                    res[1, sl, cols] = (dfg * (1.0 - lb) * sg * (1.0 - sg)).astype(BF16)
                    res[2, sl, cols] = it["dv"].astype(BF16)
                    res[3, sl, cols] = it["dg"].astype(BF16)
                return tuple(carry)

            assert nc % MIX_UNROLL_BWD == 0, (nc, MIX_UNROLL_BWD)
            zero = jnp.zeros((1, HEAD), F32)
            sums = lax.fori_loop(0, nc // MIX_UNROLL_BWD, chunks, ((zero, zero),) * hp)
            for hh, (dlb, dgain) in enumerate(sums):
                cols = slice(hh * HEAD, (hh + 1) * HEAD)
                lb = lb_all[:, cols]
                dgain_ref[:, cols] = dgain
                dl0 = dlb * lb * (1.0 - lb)
                dl_ref[0:1, cols] = dl0
                dl_ref[1:2, cols] = -dl0

        compute()
        _write_column_groups(res, dproj_ref, out_sems, 0, ng, pl.program_id(0), wide)

    ng = n_heads // hp
    col = lambda k: pl.BlockSpec((t, wide), lambda h: (0, k * ng + h))
    head = pl.BlockSpec((t, wide), lambda h: (0, h))
    vec = pl.BlockSpec((1, wide), lambda h: (0, h))
    return pl.pallas_call(
        body, grid=(ng,),
        in_specs=[ANY] * (1 + len(deps)) + [col(0), col(1), col(2), col(3), head,
                  pl.BlockSpec((hp, nc, HEAD, HEAD), lambda h: (h, 0, 0, 0)),
                  head, pl.BlockSpec((2, wide), lambda h: (0, h)), vec],
        out_specs=[ANY, pl.BlockSpec((2, wide), lambda h: (0, h)), vec],
        out_shape=[SDS(dproj.shape, BF16), SDS((2, da), F32), SDS((1, da), F32)],
        scratch_shapes=[pltpu.VMEM((4, t, wide), BF16), pltpu.VMEM((hp, HEAD, HEAD), F32),
                        pltpu.VMEM((hp * MIX_UNROLL_BWD, CHUNK, HEAD), F32), pltpu.SemaphoreType.DMA((4,))],
        input_output_aliases={0: 0}, compiler_params=_params(("arbitrary",)),
        name="hgrn_bwd")(dproj, *deps, proj, proj, proj, proj, o_pre, states, dy, lb_logits, gain)


ROWS = 256


def _head_norm(x_ref, gain, dst, dst_off, t):
    def step(i, _):
        sl = pl.ds(pl.multiple_of(i * ROWS, ROWS), ROWS)
        xv = x_ref[sl, :]
        r = lax.rsqrt(jnp.mean(xv * xv, axis=-1, keepdims=True) + EPS)
        dst[pl.ds(pl.multiple_of(dst_off + i * ROWS, ROWS), ROWS), :] = (xv * r * gain).astype(BF16)
        return 0

    lax.fori_loop(0, t // ROWS, step, 0)


def _head_norm_bwd(x_ref, gain, dn_ref, dn_off, out, slot, t):
    def step(i, acc):
        sl = pl.ds(pl.multiple_of(i * ROWS, ROWS), ROWS)
        xv = x_ref[sl, :]
        dn = dn_ref[pl.ds(pl.multiple_of(dn_off + i * ROWS, ROWS), ROWS), :]
        r = lax.rsqrt(jnp.mean(xv * xv, axis=-1, keepdims=True) + EPS)
        u = dn * gain
        out[slot, sl, :] = (r * u - xv * (r * r * r) * jnp.mean(u * xv, axis=-1, keepdims=True)).astype(out.dtype)
        return acc + jnp.sum(dn * (xv * r), axis=0, keepdims=True)

    return lax.fori_loop(0, t // ROWS, step, jnp.zeros((1, HEAD), F32))


def _attn_scores(qn, kpad, n):
    qc = qn[pl.ds(pl.multiple_of(n * CHUNK, CHUNK), CHUNK), :]
    band = pl.ds(pl.multiple_of(n * CHUNK, CHUNK), BAND)
    return qc, band, lax.dot_general(qc, kpad[band, :], NT, preferred_element_type=F32)


def _attn_softmax(raw, bias_ref, n):
    s = raw * (HEAD ** -0.5) + bias_ref[0]
    col = lax.broadcasted_iota(jnp.int32, (CHUNK, BAND), 1)
    s = jnp.where(col >= PAD - n * CHUNK, s, -jnp.inf)
    p = jnp.exp(s - jnp.max(s, axis=-1, keepdims=True))
    return p / jnp.sum(p, axis=-1, keepdims=True)


def _attn_fwd(proj, q_gain, k_gain, bias, n_heads, col0):
    t = proj.shape[0]
    nc = t // CHUNK

    def body(q_ref, k_ref, v_ref, qg_ref, kg_ref, bias_ref, y_ref, qn, kpad, vpad):
        kpad[0:PAD, :] = jnp.zeros((PAD, HEAD), BF16)
        vpad[0:PAD, :] = jnp.zeros((PAD, HEAD), BF16)
        _head_norm(q_ref, qg_ref[...], qn, 0, t)
        _head_norm(k_ref, kg_ref[...], kpad, PAD, t)

        def copy_v(i, _):
            vpad[pl.ds(pl.multiple_of(PAD + i * ROWS, ROWS), ROWS), :] = v_ref[
                pl.ds(pl.multiple_of(i * ROWS, ROWS), ROWS), :].astype(BF16)
            return 0

        lax.fori_loop(0, t // ROWS, copy_v, 0)

        def chunks(i, _):
            ns = [i * ATT_UNROLL + u for u in range(ATT_UNROLL)]
            scored = [_attn_scores(qn, kpad, n) for n in ns]
            probs = [_attn_softmax(raw, bias_ref, n).astype(BF16) for n, (_, _, raw) in zip(ns, scored)]
            outs = [lax.dot_general(p, vpad[band, :], NN, preferred_element_type=F32).astype(BF16)
                    for p, (_, band, _) in zip(probs, scored)]
            for n, o in zip(ns, outs):
                y_ref[pl.ds(pl.multiple_of(n * CHUNK, CHUNK), CHUNK), :] = o
            return 0

        assert nc % ATT_UNROLL == 0, (nc, ATT_UNROLL)
        lax.fori_loop(0, nc // ATT_UNROLL, chunks, 0)

    col = lambda k: pl.BlockSpec((t, HEAD), lambda h: (0, col0 + k * n_heads + h))
    vec = pl.BlockSpec((1, HEAD), lambda h: (0, 0))
    return pl.pallas_call(
        body, grid=(n_heads,),
        in_specs=[col(0), col(1), col(2), vec, vec, pl.BlockSpec((1, CHUNK, BAND), lambda h: (h, 0, 0))],
        out_specs=pl.BlockSpec((t, HEAD), lambda h: (0, h)), out_shape=SDS((t, n_heads * HEAD), BF16),
        scratch_shapes=[pltpu.VMEM((t, HEAD), BF16), pltpu.VMEM((t + PAD, HEAD), BF16), pltpu.VMEM((t + PAD, HEAD), BF16)],
        compiler_params=_params(("parallel",)), name="attn_fwd")(proj, proj, proj, q_gain, k_gain, bias)


def _attn_bwd(dproj, proj, q_gain, k_gain, bias, dy, n_heads, col0, deps=()):
    t = proj.shape[0]
    nc = t // CHUNK

    def body(*refs):
        (q_ref, k_ref, v_ref, qg_ref, kg_ref, bias_ref, dy_ref,
         dproj_ref, dbias_ref, dqg_ref, dkg_ref, qn, kpad, vpad, dqn, dk_acc, dv_acc, res,
         out_sems) = refs[1 + len(deps):]
        h = pl.program_id(0)

        def compute():
            kpad[0:PAD, :] = jnp.zeros((PAD, HEAD), BF16)
            vpad[0:PAD, :] = jnp.zeros((PAD, HEAD), BF16)
            _head_norm(q_ref, qg_ref[...], qn, 0, t)
            _head_norm(k_ref, kg_ref[...], kpad, PAD, t)

            def prep(i, _):
                sl = pl.ds(pl.multiple_of(PAD + i * ROWS, ROWS), ROWS)
                vpad[sl, :] = v_ref[pl.ds(pl.multiple_of(i * ROWS, ROWS), ROWS), :].astype(BF16)
                return 0

            lax.fori_loop(0, t // ROWS, prep, 0)

            def clear(i, _):
                sl = pl.ds(pl.multiple_of(i * ROWS, ROWS), ROWS)
                dk_acc[sl, :] = jnp.zeros((ROWS, HEAD), F32)
                dv_acc[sl, :] = jnp.zeros((ROWS, HEAD), F32)
                return 0

            lax.fori_loop(0, (t + PAD) // ROWS, clear, 0)
            dbias_ref[0] = jnp.zeros((CHUNK, BAND), F32)

            def chunks(i, _):
                dot = functools.partial(lax.dot_general, preferred_element_type=F32)
                ns = [i * ATT_UNROLL_BWD + u for u in range(ATT_UNROLL_BWD)]
                scored = [_attn_scores(qn, kpad, n) for n in ns]
                dos = [dy_ref[pl.ds(pl.multiple_of(n * CHUNK, CHUNK), CHUNK), :].astype(BF16) for n in ns]
                dps = [dot(do, vpad[band, :], NT) for do, (_, band, _) in zip(dos, scored)]
                ps, dss = [], []
                for n, (_, _, raw), dp in zip(ns, scored, dps):
                    p = _attn_softmax(raw, bias_ref, n)
                    ds = p * (dp - jnp.sum(dp * p, axis=-1, keepdims=True))
                    dbias_ref[0] += ds
                    ps.append(p.astype(BF16))
                    dss.append((ds * (HEAD ** -0.5)).astype(BF16))
                dqs = [dot(d, kpad[band, :], NN) for d, (_, band, _) in zip(dss, scored)]
                dks = [dot(d, qc, TN) for d, (qc, _, _) in zip(dss, scored)]
                dvs = [dot(p, do, TN) for p, do in zip(ps, dos)]
                for n, (_, band, _), dq, dk, dv in zip(ns, scored, dqs, dks, dvs):
                    dqn[pl.ds(pl.multiple_of(n * CHUNK, CHUNK), CHUNK), :] = dq
                    dk_acc[band, :] += dk
                    dv_acc[band, :] += dv
                return 0

            assert nc % ATT_UNROLL_BWD == 0, (nc, ATT_UNROLL_BWD)
            lax.fori_loop(0, nc // ATT_UNROLL_BWD, chunks, 0)
            dqg = _head_norm_bwd(q_ref, qg_ref[...], dqn, 0, res, 0, t)
            dkg = _head_norm_bwd(k_ref, kg_ref[...], dk_acc, PAD, res, 1, t)

            def put_v(i, _):
                sl = pl.ds(pl.multiple_of(i * ROWS, ROWS), ROWS)
                res[2, sl, :] = dv_acc[pl.ds(pl.multiple_of(PAD + i * ROWS, ROWS), ROWS), :].astype(BF16)
                return 0

            lax.fori_loop(0, t // ROWS, put_v, 0)

            @pl.when(h == 0)
            def _():
                dqg_ref[...] = jnp.zeros_like(dqg_ref)
                dkg_ref[...] = jnp.zeros_like(dkg_ref)

            dqg_ref[...] += dqg
            dkg_ref[...] += dkg

        compute()
        _write_column_groups(res, dproj_ref, out_sems, col0, n_heads, h, HEAD)

    col = lambda k: pl.BlockSpec((t, HEAD), lambda h: (0, col0 + k * n_heads + h))
    vec = pl.BlockSpec((1, HEAD), lambda h: (0, 0))
    btile = pl.BlockSpec((1, CHUNK, BAND), lambda h: (h, 0, 0))
    return pl.pallas_call(
        body, grid=(n_heads,),
        in_specs=[ANY] * (1 + len(deps)) + [col(0), col(1), col(2), vec, vec, btile,
                                            pl.BlockSpec((t, HEAD), lambda h: (0, h))],
        out_specs=[ANY, btile, vec, vec],
        out_shape=[SDS(dproj.shape, BF16), SDS((n_heads, CHUNK, BAND), F32), SDS((1, HEAD), F32), SDS((1, HEAD), F32)],
        scratch_shapes=[pltpu.VMEM((t, HEAD), BF16), pltpu.VMEM((t + PAD, HEAD), BF16), pltpu.VMEM((t + PAD, HEAD), BF16),
                        pltpu.VMEM((t, HEAD), F32), pltpu.VMEM((t + PAD, HEAD), F32), pltpu.VMEM((t + PAD, HEAD), F32),
                        pltpu.VMEM((3, t, HEAD), BF16), pltpu.SemaphoreType.DMA((3,))],
        input_output_aliases={0: 0}, compiler_params=_params(("arbitrary",)),
        name="attn_bwd")(dproj, *deps, proj, proj, proj, q_gain, k_gain, bias, dy)


def _place():
    x, y, c = lax.axis_index("x"), lax.axis_index("y"), lax.axis_index("c")
    others = [(1 - x, y), (x, 1 - y), (1 - x, 1 - y)]
    return x, y, c, others


def _chunk_of(ref, kind, chip, half, shard_shape):
    r, n = shard_shape
    hr = r // 2
    if kind == "col":
        rows = pl.ds(0, r) if half is None else pl.ds(half * hr, hr)
        return ref.at[rows, pl.ds(chip * n, n)]
    rows = pl.ds(chip * r, r) if half is None else pl.ds(chip * r + half * hr, hr)
    return ref.at[rows, :]


EFFECT = pltpu.SideEffectType.DATAFLOW_SIDE_EFFECTING


def _start_copies(name, bufs, plan, n, deps):
    nb, nd = len(bufs), len(deps)

    def body(*refs):
        send, recv, token = refs[nb + nd], refs[nb + nd + 1], refs[-1]
        for cp in plan(refs[:nb], send, recv)[0]:
            cp.start()
        token[...] = jnp.zeros_like(token)

    out = pl.pallas_call(
        body, name=name,
        out_shape=(pltpu.SemaphoreType.DMA((n,)), pltpu.SemaphoreType.DMA((n,)),
                   *[pltpu.HBM(b.shape, b.dtype) for b in bufs], SDS((8, 128), F32)),
        in_specs=[HBM] * nb + [ANY] * nd,
        out_specs=(SEM, SEM, *[HBM] * nb, pl.BlockSpec(memory_space=pltpu.VMEM)),
        input_output_aliases={i: 2 + i for i in range(nb)},
        compiler_params=pltpu.CompilerParams(has_side_effects=EFFECT),
    )(*[pltpu.with_memory_space_constraint(b, pltpu.HBM) for b in bufs], *deps)
    return out[0], out[1], list(out[2:2 + nb]), out[-1]


def _wait_copies(name, bufs, send, recv, plan, after):
    nb = len(bufs)

    def body(*refs):
        sends, recvs = plan(refs[:nb], refs[nb], refs[nb + 1])
        for cp in sends:
            cp.wait_send()
        for cp in recvs:
            cp.wait_recv()

    out = pl.pallas_call(
        body, name=name, out_shape=tuple(pltpu.HBM(b.shape, b.dtype) for b in bufs),
        in_specs=[HBM] * nb + [SEM, SEM] + [ANY] * len(after), out_specs=tuple([HBM] * nb),
        input_output_aliases={i: i for i in range(nb)},
        compiler_params=pltpu.CompilerParams(has_side_effects=EFFECT),
    )(*bufs, send, recv, *after)
    return list(out)


def _remote(src, dst, send, recv, i, dev):
    return pltpu.make_async_remote_copy(src_ref=src, dst_ref=dst, send_sem=send.at[i], recv_sem=recv.at[i],
                                        device_id=dev, device_id_type=MESH)


ALL_RELATIONS = (0, 1, 2)


def _plan_gather_ici(kinds, shapes, rels=ALL_RELATIONS):
    def plan(refs, send, recv):
        x, y, c, others = _place()
        sends, recvs = [], []
        for w, (kind, ss) in enumerate(zip(kinds, shapes)):
            for p in rels:
                px, py = others[p]
                mine = _chunk_of(refs[w], kind, 2 * x + y, c, ss)
                theirs = _chunk_of(refs[w], kind, 2 * px + py, c, ss)
                sends.append(_remote(mine, mine, send, recv, 3 * w + p, (px, py, c)))
                recvs.append(_remote(theirs, theirs, send, recv, 3 * w + p, (px, py, c)))
        return sends, recvs

    return plan, 3 * len(kinds)


def _plan_gather_pass(kinds, shapes, rels=ALL_RELATIONS):
    def plan(refs, send, recv):
        x, y, c, others = _place()
        sends, recvs = [], []
        for w, (kind, ss) in enumerate(zip(kinds, shapes)):
            for i, p in enumerate(rels):
                px, py = others[p]
                got = _chunk_of(refs[w], kind, 2 * px + py, c, ss)
                coming = _chunk_of(refs[w], kind, 2 * px + py, 1 - c, ss)
                sends.append(_remote(got, got, send, recv, len(rels) * w + i, (x, y, 1 - c)))
                recvs.append(_remote(coming, coming, send, recv, len(rels) * w + i, (x, y, 1 - c)))
        return sends, recvs

    return plan, len(rels) * len(kinds)


def _plan_pair(kinds, shapes):
    nw = len(kinds)

    def plan(refs, send, recv):
        x, y, c, _ = _place()
        sends = []
        for w, (kind, ss) in enumerate(zip(kinds, shapes)):
            for k in range(4):
                sends.append(_remote(_chunk_of(refs[w], kind, k, 1 - c, ss), refs[nw + w].at[k], send, recv,
                                     4 * w + k, (x, y, 1 - c)))
        return sends, sends

    return plan, 4 * nw


def _plan_chip(nw):
    def plan(refs, send, recv):
        x, y, c, others = _place()
        sends = []
        for w in range(nw):
            for p, (px, py) in enumerate(others):
                sends.append(_remote(refs[w].at[p], refs[nw + w].at[p], send, recv, 3 * w + p, (px, py, c)))
        return sends, sends

    return plan, 3 * nw


def _plan_share(nw):
    def plan(refs, send, recv):
        x, y, c, _ = _place()
        sends = [_remote(refs[w].at[c], refs[w].at[c], send, recv, w, (x, y, 1 - c)) for w in range(nw)]
        recvs = [_remote(refs[w].at[1 - c], refs[w].at[1 - c], send, recv, w, (x, y, 1 - c)) for w in range(nw)]
        return sends, recvs

    return plan, nw


def _grad_half_spec(kind, tr, tn, nr, nn, chunk):
    if kind == "col":
        return pl.BlockSpec((tr, tn), lambda *a: (a[-1][1] * nr + a[-3], chunk(*a) * nn + a[-2]))
    return pl.BlockSpec((tr, tn), lambda *a: ((2 * chunk(*a) + a[-1][1]) * nr + a[-3], a[-2]))


def _pair_add(grad, got, kind, shard_shape, pos, name):
    r, n = shard_shape
    hr = r // 2
    tr, tn = _row_tile(hr, n), n
    nr, nn = hr // tr, n // tn
    g_spec = _grad_half_spec(kind, tr, tn, nr, nn, lambda p, i, j, pos_: pos_[2 + p])
    r_spec = pl.BlockSpec((1, tr, tn), lambda p, i, j, pos_: (pos_[2 + p], i, j))
    o_spec = pl.BlockSpec((1, tr, tn), lambda p, i, j, pos_: (p, i, j))

    def body(pos_ref, g_ref, r_ref, o_ref):
        o_ref[0] = (g_ref[...] + r_ref[0]).astype(BF16)

    return pl.pallas_call(
        body,
        grid_spec=pltpu.PrefetchScalarGridSpec(num_scalar_prefetch=1, grid=(3, nr, nn), in_specs=[g_spec, r_spec],
                                               out_specs=o_spec),
        out_shape=SDS((3, hr, n), BF16),
        compiler_params=_params(("parallel", "parallel", "parallel")), name=name)(pos, grad, got)


def _chip_add(grad, got, got16, kind, shard_shape, pos, name):
    r, n = shard_shape
    hr = r // 2
    tr, tn = _row_tile(hr, n), n
    nr, nn = hr // tr, n // tn
    g_spec = _grad_half_spec(kind, tr, tn, nr, nn, lambda i, j, pos_: pos_[0])
    r_spec = pl.BlockSpec((1, tr, tn), lambda i, j, pos_: (pos_[0], i, j))
    oth = pl.BlockSpec((3, tr, tn), lambda i, j, pos_: (0, i, j))

    def body(pos_ref, g_ref, r_ref, oth_ref, o_ref):
        own = g_ref[...] + r_ref[0]
        o_ref[0] = ((own + oth_ref[0].astype(F32)) + oth_ref[1].astype(F32)) + oth_ref[2].astype(F32)

    return pl.pallas_call(
        body,
        grid_spec=pltpu.PrefetchScalarGridSpec(num_scalar_prefetch=1, grid=(nr, nn), in_specs=[g_spec, r_spec, oth],
                                               out_specs=pl.BlockSpec((1, tr, tn), lambda i, j, pos_: (pos_[1], i, j))),
        out_shape=SDS((2, hr, n), F32), compiler_params=_params(("parallel", "parallel")),
        name=name)(pos, grad, got, got16)


def _adamw_math(w, g, m, v):
    m = ADAM_B1 * m + (1.0 - ADAM_B1) * g
    v = ADAM_B2 * v + (1.0 - ADAM_B2) * (g * g)
    m_hat = m / (1.0 - ADAM_B1 ** ADAM_STEP)
    v_hat = v / (1.0 - ADAM_B2 ** ADAM_STEP)
    return -ADAM_LR * (m_hat / (jnp.sqrt(v_hat) + ADAM_EPS) + ADAM_WD * w), m, v


def _adamw(w, g, m, v, name):
    r, n = w.shape
    tr, tn = _row_tile(r, n), n

    def body(w_ref, g_ref, m_ref, v_ref, d_ref, nm_ref, nv_ref, go_ref):
        gv = g_ref[...]
        d_ref[...], nm_ref[...], nv_ref[...] = _adamw_math(w_ref[...], gv, m_ref[...], v_ref[...])
        go_ref[...] = gv

    tile = pl.BlockSpec((tr, tn), lambda i, j: (i, j))
    return pl.pallas_call(
        body, grid=(r // tr, n // tn), in_specs=[tile] * 4, out_specs=[tile] * 4, out_shape=[SDS((r, n), F32)] * 4,
        compiler_params=_params(("parallel", "parallel")), name=name)(w, g, m, v)


def _small_allreduce_adamw(g, w, m, v, deps=()):
    length = g.shape[1]

    def body(*refs):
        g_ref, w_ref, m_ref, v_ref = refs[:4]
        gs_ref, d_ref, nm_ref, nv_ref, buf, send, recv = refs[4 + len(deps):]
        x, y, c = lax.axis_index("x"), lax.axis_index("y"), lax.axis_index("c")
        me = 4 * x + 2 * y + c
        buf[me] = g_ref[...]
        cps = []
        for d in range(1, 8):
            peer = (x ^ (d >> 2), y ^ ((d >> 1) & 1), c ^ (d & 1))
            cp = pltpu.make_async_remote_copy(src_ref=buf.at[me], dst_ref=buf.at[me], send_sem=send.at[d - 1],
                                              recv_sem=recv.at[d - 1], device_id=peer, device_id_type=MESH)
            cp.start()
            cps.append(cp)
        for cp in cps:
            cp.wait()
        total = buf[0]
        for d in range(1, 8):
            total = total + buf[d]
        gs_ref[...] = total
        d_ref[...], nm_ref[...], nv_ref[...] = _adamw_math(w_ref[...], total, m_ref[...], v_ref[...])

    vm = pl.BlockSpec(memory_space=pltpu.VMEM)
    return pl.pallas_call(
        body, in_specs=[vm] * 4 + [ANY] * len(deps), out_specs=[vm] * 4, out_shape=[SDS((1, length), F32)] * 4,
        scratch_shapes=[pltpu.VMEM((8, 1, length), F32), pltpu.SemaphoreType.DMA((7,)), pltpu.SemaphoreType.DMA((7,))],
        compiler_params=pltpu.CompilerParams(has_side_effects=True), name="small_allreduce_adamw")(g, w, m, v, *deps)


def kernel(x, w_in, b_gate, norm_mix, norm_ffn, hgrn_lb_logits, hgrn_out_gain, q_gain, k_gain, rel_bias, w_proj_a, w_proj_b, w_out, w_ffn_in, w_ffn_out, loss_target, m_w_in, m_b_gate, m_norm_mix, m_norm_ffn, m_hgrn_lb_logits, m_hgrn_out_gain, m_q_gain, m_k_gain, m_rel_bias, m_w_proj_a, m_w_proj_b, m_w_out, m_w_ffn_in, m_w_ffn_out, v_w_in, v_b_gate, v_norm_mix, v_norm_ffn, v_hgrn_lb_logits, v_hgrn_out_gain, v_q_gain, v_k_gain, v_rel_bias, v_w_proj_a, v_w_proj_b, v_w_out, v_w_ffn_in, v_w_ffn_out):
    t, d = x.shape[1], x.shape[2]
    d_a = hgrn_out_gain.shape[1]
    h_a = d_a // HEAD
    h_b = rel_bias.shape[1]
    d_b = h_b * HEAD
    x0 = x.reshape(t, d)
    target = loss_target.reshape(t, d)
    ax, ay = lax.axis_index("x"), lax.axis_index("y")
    pos = jnp.stack([2 * ax + ay, lax.axis_index("c"), 2 * (1 - ax) + ay, 2 * ax + 1 - ay,
                     2 * (1 - ax) + 1 - ay]).astype(jnp.int32)

    names = ["w_in", "w_proj_a", "w_proj_b", "w_out", "w_ffn_in", "w_ffn_out"]
    big = dict(zip(names, [w_in[0], w_proj_a[0], w_proj_b[0], w_out[0], w_ffn_in[0], w_ffn_out[0]]))
    big_m = dict(zip(names, [m_w_in[0], m_w_proj_a[0], m_w_proj_b[0], m_w_out[0], m_w_ffn_in[0], m_w_ffn_out[0]]))
    big_v = dict(zip(names, [v_w_in[0], v_w_proj_a[0], v_w_proj_b[0], v_w_out[0], v_w_ffn_in[0], v_w_ffn_out[0]]))
    kind = dict(zip(names, ["col", "col", "col", "row", "col", "row"]))
    shape = {nm: big[nm].shape for nm in names}

    def gather_start(tag, group, deps):
        plan, n = _plan_gather_ici([kind[g] for g in group], [shape[g] for g in group])
        fulls = [_cast_into_full(big[g], kind[g], pos, "cast_" + g) for g in group]
        send, recv, bufs, token = _start_copies("gather_ici_start_" + tag, fulls, plan, n, deps)
        return (tag, group, plan, send, recv, bufs), token

    def gather_pass(state, after, rels=ALL_RELATIONS, part=""):
        tag, group, _, send, recv, bufs = state
        kinds_, shapes_ = [kind[g] for g in group], [shape[g] for g in group]
        bufs = _wait_copies("gather_ici_wait_" + tag + part, bufs, send, recv,
                            _plan_gather_ici(kinds_, shapes_, rels)[0], after)
        plan, n = _plan_gather_pass(kinds_, shapes_, rels)
        send2, recv2, bufs, token = _start_copies("gather_pass_start_" + tag + part, bufs, plan, n, ())
        return (tag + part, group, plan, send2, recv2, bufs), token

    def gather_done(state, after):
        tag, group, plan, send, recv, bufs = state
        return _wait_copies("gather_pass_wait_" + tag, bufs, send, recv, plan, after)

    def reduce_start(tag, group, grads, deps, sent=None):
        plan, n = _plan_pair([kind[g] for g in group], [shape[g] for g in group])
        srcs = list(grads) if sent is None else list(sent)
        lands = [lax.empty((4, shape[g][0] // 2, shape[g][1]), s.dtype) for g, s in zip(group, srcs)]
        send, recv, bufs, token = _start_copies("pair_start_" + tag, srcs + lands, plan, n, deps)
        kept = None if sent is None else list(grads)
        return dict(tag=tag, group=group, plan=plan, send=send, recv=recv, bufs=bufs, kept=kept), token

    def reduce_pair_done(st, after):
        tag, group, nw = st["tag"], st["group"], len(st["group"])
        bufs = _wait_copies("pair_wait_" + tag, st["bufs"], st["send"], st["recv"], st["plan"], after)
        grads, gots = (bufs[:nw] if st["kept"] is None else st["kept"]), bufs[nw:]
        parts = [_pair_add(g, l, kind[nm], shape[nm], pos, "pair_add_" + nm) for g, l, nm in zip(grads, gots, group)]
        lands = [lax.empty((3, shape[g][0] // 2, shape[g][1]), BF16) for g in group]
        plan, n = _plan_chip(nw)
        send, recv, bufs, token = _start_copies("chip_start_" + tag, parts + lands, plan, n, ())
        return dict(st, plan=plan, send=send, recv=recv, bufs=bufs, grads=grads, gots=gots), token

    def reduce_chip_done(st, after):
        tag, group, nw = st["tag"], st["group"], len(st["group"])
        bufs = _wait_copies("chip_wait_" + tag, st["bufs"], st["send"], st["recv"], st["plan"], after)
        finals = [_chip_add(g, l, got16, kind[nm], shape[nm], pos, "chip_add_" + nm)
                  for g, l, got16, nm in zip(st["grads"], st["gots"], bufs[nw:], group)]
        plan, n = _plan_share(nw)
        send, recv, bufs, token = _start_copies("share_start_" + tag, finals, plan, n, ())
        return dict(st, plan=plan, send=send, recv=recv, bufs=bufs), token

    g_big, upd = {}, {}

    def reduce_finish(st, after):
        bufs = _wait_copies("share_wait_" + st["tag"], st["bufs"], st["send"], st["recv"], st["plan"], after)
        for full, nm in zip(bufs, st["group"]):
            upd[nm] = _adamw(big[nm], full.reshape(shape[nm]), big_m[nm], big_v[nm], "adamw_" + nm)
            g_big[nm] = upd[nm][3]

    ga, token = gather_start("a", ["w_in"], ())
    gb, token = gather_start("b", ["w_proj_a", "w_proj_b", "w_out"], (token,))
    gc, token = gather_start("c", ["w_ffn_in"], (token,))
    gd, token = gather_start("d", ["w_ffn_out"], (token,))
    h1, r1 = _rmsnorm_fwd(x0, norm_mix, "rmsnorm_mix")
    rb = jnp.pad(rel_bias[0], ((0, 0), (0, REL_LANES - N_REL)))
    bias = _relbias_expand(rb).transpose(1, 0, 2)
    proj = _matmul_chunks(h1, big["w_in"], (0,), None, pos, "proj_in_own", own_shard=True)
    ici_a = ga
    ga, token = gather_pass(ici_a, (h1, bias, proj, token), rels=(0, 1), part="_near")
    (wg_in,) = gather_done(ga, ())
    proj = _matmul_chunks(h1, wg_in, (2, 3), proj, pos, "proj_in_near")
    ga, token = gather_pass(ici_a[:5] + ([wg_in],), (proj,), rels=(2,), part="_far")
    (wg_in,) = gather_done(ga, ())
    proj = _matmul_chunks(h1, wg_in, (4,), proj, pos, "proj_in_far")
    y_a, o_pre, states = _hgrn_fwd(proj, hgrn_lb_logits, hgrn_out_gain, h_a)
    gb, token = gather_pass(gb, (y_a,))
    col_b = 4 * d_a // HEAD
    y_b = _attn_fwd(proj, q_gain, k_gain, bias, h_b, col_b)
    wg_pa, wg_pb, wg_out = gather_done(gb, (y_b,))
    gate_off = 4 * d_a + 3 * d_b
    pa, pb, merged = _proj_merge(y_a, y_b, wg_pa, wg_pb, proj, b_gate, gate_off, deps=(token,))
    x2 = _matmul(merged, wg_out, res=x0, name="out_proj")
    gc, token = gather_pass(gc, (x2,))
    h2, r2 = _rmsnorm_fwd(x2, norm_ffn, "rmsnorm_ffn")
    (wg_fin,) = gather_done(gc, (h2,))
    ff_gate, ff_up, act = _ffn_in_swiglu(h2, wg_fin, deps=(token,))
    gd, token = gather_pass(gd, (act,))
    (wg_fout,) = gather_done(gd, ())
    dy, dy16, loss_part = _ffn_out_loss(act, wg_fout, x2, target)

    g_fout, g_fout16 = _matmul(act, dy16, ta=True, name="dw_ffn_out", also_bf16=True)
    r_fout, token = reduce_start("fout", ["w_ffn_out"], [g_fout], (), sent=[g_fout16])
    dgu = _d_act_swiglu(dy16, wg_fout, ff_gate, ff_up, deps=(token,))
    r_fout, token = reduce_pair_done(r_fout, (dgu,))
    g_fin, g_fin16 = _matmul(h2, dgu, ta=True, name="dw_ffn_in", deps=(token,), also_bf16=True)
    r_fin, token = reduce_start("fin", ["w_ffn_in"], [g_fin], (), sent=[g_fin16])
    dh2 = _matmul(dgu, wg_fin, tb=True, name="d_h2", deps=(token,))
    r_fout, token_a = reduce_chip_done(r_fout, (dh2,))
    r_fin, token_b = reduce_pair_done(r_fin, (dh2,))
    dx2, dx2_16, g_norm_ffn = _rmsnorm_bwd(dh2, x2, r2, norm_ffn, dy, "rmsnorm_ffn_bwd", deps=(token_a, token_b))
    dp_ab, dproj, g_bgate = _d_merged_gates(dx2_16, wg_out, proj, b_gate, pa, pb, gate_off)
    g_out = _matmul(merged, dx2_16, ta=True, name="dw_out")
    g_pa = _matmul(y_a, dp_ab, ta=True, name="dw_proj_a", b_lead=0)
    g_pb = _matmul(y_b, dp_ab, ta=True, name="dw_proj_b", b_lead=1)
    r_mid, token = reduce_start("mid", ["w_proj_a", "w_proj_b", "w_out"], [g_pa, g_pb, g_out], ())
    dy_a = _matmul(dp_ab, wg_pa, tb=True, name="d_y_a", deps=(token,), a_lead=0)
    dy_b = _matmul(dp_ab, wg_pb, tb=True, name="d_y_b", a_lead=1)
    r_mid, token_b = reduce_pair_done(r_mid, (dy_b,))
    dproj, dbias, g_qg, g_kg = _attn_bwd(dproj, proj, q_gain, k_gain, bias, dy_b, h_b, col_b, deps=(token_b,))
    r_fin, token_a = reduce_chip_done(r_fin, (dbias,))
    r_mid, token = reduce_chip_done(r_mid, (dbias,))
    dproj, g_lb, g_gain = _hgrn_bwd(dproj, proj, o_pre, states, dy_a, hgrn_lb_logits, hgrn_out_gain, h_a,
                                    deps=(token, token_a))
    g_in, g_in16 = _matmul(h1, dproj, ta=True, name="dw_in", also_bf16=True)
    r_in, token = reduce_start("in", ["w_in"], [g_in], (), sent=[g_in16])
    g_rb = _relbias_reduce(dbias.transpose(1, 0, 2))[:, :N_REL]
    reduce_finish(r_mid, (token,))
    r_in, token = reduce_pair_done(r_in, (g_rb, upd["w_out"][0]))
    dh1 = _matmul(dproj, wg_in, tb=True, name="d_h1", deps=(token,))
    dx, g_norm_mix = _rmsnorm_bwd(dh1, x0, r1, norm_mix, dx2, "rmsnorm_mix_bwd", with_bf16=False)
    reduce_finish(r_fin, (dx,))
    reduce_finish(r_fout, (dx,))
    r_in, token = reduce_chip_done(r_in, (upd["w_ffn_in"][0], upd["w_ffn_out"][0], upd["w_proj_a"][0],
                                          upd["w_proj_b"][0]))

    small_w = [b_gate, norm_mix, norm_ffn, hgrn_lb_logits, hgrn_out_gain, q_gain, k_gain, rel_bias]
    small_m = [m_b_gate, m_norm_mix, m_norm_ffn, m_hgrn_lb_logits, m_hgrn_out_gain, m_q_gain, m_k_gain, m_rel_bias]
    small_v = [v_b_gate, v_norm_mix, v_norm_ffn, v_hgrn_lb_logits, v_hgrn_out_gain, v_q_gain, v_k_gain, v_rel_bias]
    small_g = [g_bgate, g_norm_mix, g_norm_ffn, g_lb, g_gain, g_qg, g_kg, g_rb]
    sizes = [w.size for w in small_w]
    length = -(-(sum(sizes) + 1) // 128) * 128

    def pack(parts_):
        flat = jnp.concatenate([p.reshape(1, -1) for p in parts_], axis=1)
        return jnp.pad(flat, ((0, 0), (0, length - flat.shape[1])))

    one = jnp.ones((1, 1), F32)
    packed = _small_allreduce_adamw(pack(small_g + [loss_part]), pack(small_w + [one]), pack(small_m + [one]),
                                    pack(small_v + [one]), deps=(token,))

    def unpack(vec):
        out, at = [], 0
        for w, n in zip(small_w, sizes):
            out.append(vec[0, at:at + n].reshape(w.shape))
            at += n
        return out, vec[0, at]

    (sg, loss), (sd, _), (sm, _), (sv, _) = [unpack(p) for p in packed]
    reduce_finish(r_in, (packed[0],))

    def ordered(small, bigs):
        bigs = [bigs[nm][None] for nm in names]
        return [bigs[0]] + small + bigs[1:]

    return (loss, dx.reshape(x.shape), *ordered(sg, g_big), *ordered(sd, {nm: upd[nm][0] for nm in names}),
            *ordered(sm, {nm: upd[nm][1] for nm in names}), *ordered(sv, {nm: upd[nm][2] for nm in names}))
```

```python
import functools

import jax
import jax.numpy as jnp
from jax import lax
from jax.experimental import pallas as pl
from jax.experimental.pallas import tpu as pltpu

F32 = jnp.float32
BF16 = jnp.bfloat16
SDS = jax.ShapeDtypeStruct
MESH = pl.DeviceIdType.MESH
HIGHEST = lax.Precision.HIGHEST

CHUNK = 64
SUB = 16
HEAD = 128
N_PAST = 8
BAND = (N_PAST + 1) * CHUNK
PAD = N_PAST * CHUNK
REL_FUTURE = CHUNK - 1
REL_PAST = 2 * CHUNK - 1
N_REL = REL_FUTURE + REL_PAST + 1
REL_LANES = 256
EPS = 1e-6
MIX_HEADS = 2
MIX_UNROLL = 4
MIX_UNROLL_BWD = 4
ATT_UNROLL = 8
ATT_UNROLL_BWD = 8
EXP_CLAMP = 80.0

ADAM_LR = 0.001
ADAM_B1 = 0.9
ADAM_B2 = 0.999
ADAM_EPS = 1e-08
ADAM_WD = 0.01
ADAM_STEP = 10

VMEM_LIMIT = 56 * 1024 * 1024

HBM = pl.BlockSpec(memory_space=pltpu.HBM)
ANY = pl.BlockSpec(memory_space=pl.ANY)
SEM = pl.BlockSpec(memory_space=pltpu.SEMAPHORE)

NT = (((1,), (1,)), ((), ()))
TN = (((0,), (0,)), ((), ()))
NN = (((1,), (0,)), ((), ()))


def _params(sem=None, **kw):
    return pltpu.CompilerParams(dimension_semantics=sem, vmem_limit_bytes=VMEM_LIMIT, **kw)


def _tile(n, pref, unit=128):
    if n <= pref:
        return n
    t = pref - pref % unit
    while n % t:
        t -= unit
    return t


STREAM_BLOCK = 256 * 1408


def _row_tile(rows, cols):
    return _tile(rows, max(16, STREAM_BLOCK // cols), 16)


def _sigmoid(x):
    return 1.0 / (1.0 + jnp.exp(-x))


def _dsilu(x, s):
    return s * (1.0 + x * (1.0 - s))


def _split(a):
    hi = a.astype(BF16)
    return hi, (a - hi.astype(F32)).astype(BF16)


def _dot3(a, b, dims):
    dot = lambda u, v: lax.dot_general(u, v, dims, preferred_element_type=F32)
    return dot(a[0], b[1]) + dot(a[1], b[0]) + dot(a[0], b[0])


def _fdot(a, b):
    return lax.dot_general(a, b, NN, precision=HIGHEST, preferred_element_type=F32)


MM_TILE_K = 5632
MM_TILE_N = 512


def _matmul_chunks(h, w, which, prev, pos, name, own_shard=False, deps=()):
    t, d = h.shape
    nc_ = w.shape[1] if own_shard else w.shape[1] // 4
    tm, tn = _tile(t, 1024), _tile(nc_, 1408)
    nn = nc_ // tn

    def chunk(q, p):
        sel = p[which[0]]
        for i in range(1, len(which)):
            sel = jnp.where(q == i, p[which[i]], sel)
        return sel

    def body(p_ref, h_ref, w_ref, *rest):
        rest[-1][...] = jnp.dot(h_ref[...], w_ref[...].astype(BF16), preferred_element_type=F32)

    if own_shard:
        w_spec = pl.BlockSpec((d, tn), lambda q, i, j, p: (0, j))
    else:
        w_spec = pl.BlockSpec((d, tn), lambda q, i, j, p: (0, chunk(q, p) * nn + j))
    n_extra = len(deps) + (prev is not None)
    return pl.pallas_call(
        body,
        grid_spec=pltpu.PrefetchScalarGridSpec(
            num_scalar_prefetch=1, grid=(len(which), t // tm, nn),
            in_specs=[pl.BlockSpec((tm, d), lambda q, i, j, p: (i, 0)), w_spec] + [ANY] * n_extra,
            out_specs=pl.BlockSpec((tm, tn), lambda q, i, j, p: (i, chunk(q, p) * nn + j))),
        out_shape=SDS((t, 4 * nc_), F32), input_output_aliases={3 + len(deps): 0} if prev is not None else {},
        compiler_params=_params(("arbitrary", "arbitrary", "arbitrary")),
        name=name)(pos, h, w, *deps, *(() if prev is None else (prev,)))


def _matmul(a, b, *, ta=False, tb=False, res=None, out_dtype=F32, name, deps=(), a_lead=None, b_lead=None,
            also_bf16=False):
    a2, b2 = a.shape[-2:], b.shape[-2:]
    m, k = (a2[1], a2[0]) if ta else a2
    n = b2[0] if tb else b2[1]
    if k > MM_TILE_K:
        tk, tm, tn = _tile(k, MM_TILE_K // 2), _tile(m, 1024), _tile(n, 1024)
    else:
        tk = k
        tm, tn = _tile(m, 2048 if tk <= MM_TILE_K // 2 else 1024), _tile(n, MM_TILE_N)
    nk = k // tk
    dims = ((((0,) if ta else (1,)), ((1,) if tb else (0,))), ((), ()))

    def body(*refs):
        n_in = 2 + (res is not None)
        a_ref, b_ref = refs[:2]
        r_ref = refs[2] if res is not None else None
        o_ref = refs[n_in + len(deps)]
        part = lax.dot_general(a_ref[...].astype(BF16), b_ref[...].astype(BF16), dims, preferred_element_type=F32)

        def finish(out):
            if r_ref is not None:
                out = out + r_ref[...]
            o_ref[...] = out.astype(o_ref.dtype)
            if also_bf16:
                refs[n_in + len(deps) + 1][...] = out.astype(BF16)

        if nk == 1:
            finish(part)
            return
        acc_ref = refs[-1]
        kk = pl.program_id(2)

        @pl.when(kk == 0)
        def _():
            acc_ref[...] = part

        @pl.when(jnp.logical_and(kk > 0, kk < nk - 1))
        def _():
            acc_ref[...] += part

        @pl.when(kk == nk - 1)
        def _():
            finish(acc_ref[...] + part)

    def spec(block, index, lead):
        if lead is None:
            return pl.BlockSpec(block, index)
        return pl.BlockSpec((None,) + block, lambda i, j, l: (lead,) + index(i, j, l))

    a_spec = spec((tk, tm), lambda i, j, l: (l, i), a_lead) if ta else spec((tm, tk), lambda i, j, l: (i, l), a_lead)
    b_spec = spec((tn, tk), lambda i, j, l: (j, l), b_lead) if tb else spec((tk, tn), lambda i, j, l: (l, j), b_lead)
    o_spec = pl.BlockSpec((tm, tn), lambda i, j, l: (i, j))
    in_specs = [a_spec, b_spec] + ([o_spec] if res is not None else []) + [ANY] * len(deps)
    args = (a, b) + ((res,) if res is not None else ()) + tuple(deps)
    out_specs, out_shape = o_spec, SDS((m, n), out_dtype)
    if also_bf16:
        out_specs, out_shape = [o_spec, o_spec], [out_shape, SDS((m, n), BF16)]
    return pl.pallas_call(
        body, grid=(m // tm, n // tn, nk), in_specs=in_specs, out_specs=out_specs,
        out_shape=out_shape, scratch_shapes=[pltpu.VMEM((tm, tn), F32)] if nk > 1 else [],
        compiler_params=_params(("parallel", "parallel", "arbitrary")), name=name)(*args)


def _cast_into_full(w, kind, pos, name):
    r, n = w.shape
    tr = _tile(r, 512, 16)
    nr = r // tr
    if kind == "col":
        shape, o_spec = (r, 4 * n), pl.BlockSpec((tr, n), lambda i, p: (i, p[0]))
    else:
        shape, o_spec = (4 * r, n), pl.BlockSpec((tr, n), lambda i, p: (p[0] * nr + i, 0))

    def body(p_ref, w_ref, o_ref):
        o_ref[...] = w_ref[...].astype(BF16)

    return pl.pallas_call(
        body,
        grid_spec=pltpu.PrefetchScalarGridSpec(num_scalar_prefetch=1, grid=(nr,),
                                               in_specs=[pl.BlockSpec((tr, n), lambda i, p: (i, 0))], out_specs=o_spec),
        out_shape=SDS(shape, BF16), compiler_params=_params(("parallel",)), name=name)(pos, w)


def _rmsnorm_fwd(x, gain, name):
    t, d = x.shape
    tm = _tile(t, 256)

    def body(x_ref, g_ref, h_ref, r_ref):
        xv = x_ref[...]
        r = lax.rsqrt(jnp.mean(xv * xv, axis=-1, keepdims=True) + EPS)
        h_ref[...] = (xv * r * g_ref[...]).astype(BF16)
        r_ref[...] = r

    return pl.pallas_call(
        body, grid=(t // tm,),
        in_specs=[pl.BlockSpec((tm, d), lambda i: (i, 0)), pl.BlockSpec((1, d), lambda i: (0, 0))],
        out_specs=[pl.BlockSpec((tm, d), lambda i: (i, 0)), pl.BlockSpec((tm, 1), lambda i: (i, 0))],
        out_shape=[SDS((t, d), BF16), SDS((t, 1), F32)], compiler_params=_params(("parallel",)), name=name)(x, gain)


def _rmsnorm_bwd(dh, x, r, gain, dres, name, deps=(), with_bf16=True):
    t, d = x.shape
    tm = _tile(t, 256)

    def body(dh_ref, x_ref, r_ref, g_ref, dres_ref, *rest):
        outs = rest[len(deps):]
        dx_ref, dg_ref = outs[0], outs[-1]

        @pl.when(pl.program_id(0) == 0)
        def _():
            dg_ref[...] = jnp.zeros_like(dg_ref)

        dhv, xv, rv = dh_ref[...], x_ref[...], r_ref[...]
        dg_ref[...] += jnp.sum(dhv * (xv * rv), axis=0, keepdims=True)
        u = dhv * g_ref[...]
        dx = dres_ref[...] + rv * u - xv * (rv * rv * rv) * jnp.mean(u * xv, axis=-1, keepdims=True)
        dx_ref[...] = dx
        if with_bf16:
            outs[1][...] = dx.astype(BF16)

    row = pl.BlockSpec((tm, d), lambda i: (i, 0))
    vec = pl.BlockSpec((1, d), lambda i: (0, 0))
    copies = [SDS((t, d), F32)] + ([SDS((t, d), BF16)] if with_bf16 else [])
    return pl.pallas_call(
        body, grid=(t // tm,),
        in_specs=[row, row, pl.BlockSpec((tm, 1), lambda i: (i, 0)), vec, row] + [ANY] * len(deps),
        out_specs=[row] * len(copies) + [vec], out_shape=copies + [SDS((1, d), F32)],
        compiler_params=_params(("arbitrary",)), name=name)(dh, x, r, gain, dres, *deps)


def _proj_merge(y_a, y_b, w_a, w_b, proj, b_gate, off, deps=()):
    t, ka = y_a.shape
    kb = y_b.shape[1]
    d = w_a.shape[1]
    tm, tc = _tile(t, 1024), _tile(d, MM_TILE_N)
    nj = d // tc
    oa, ob = off // tc, off // tc + nj

    def body(ya_ref, yb_ref, wa_ref, wb_ref, la_ref, lb_ref, ba_ref, bb_ref, *rest):
        pa_ref, pb_ref, o_ref = rest[len(deps):]
        pa = jnp.dot(ya_ref[...], wa_ref[...], preferred_element_type=F32)
        pb = jnp.dot(yb_ref[...], wb_ref[...], preferred_element_type=F32)
        pa_ref[...] = pa
        pb_ref[...] = pb
        ga = _sigmoid(la_ref[...] + ba_ref[...])
        gb = _sigmoid(lb_ref[...] + bb_ref[...])
        o_ref[...] = (ga * pa + gb * pb).astype(BF16)

    tile = pl.BlockSpec((tm, tc), lambda i, j: (i, j))
    return pl.pallas_call(
        body, grid=(t // tm, nj),
        in_specs=[pl.BlockSpec((tm, ka), lambda i, j: (i, 0)), pl.BlockSpec((tm, kb), lambda i, j: (i, 0)),
                  pl.BlockSpec((ka, tc), lambda i, j: (0, j)), pl.BlockSpec((kb, tc), lambda i, j: (0, j)),
                  pl.BlockSpec((tm, tc), lambda i, j: (i, oa + j)), pl.BlockSpec((tm, tc), lambda i, j: (i, ob + j)),
                  pl.BlockSpec((1, tc), lambda i, j: (0, j)), pl.BlockSpec((1, tc), lambda i, j: (0, nj + j))]
        + [ANY] * len(deps),
        out_specs=[tile, tile, tile], out_shape=[SDS((t, d), F32), SDS((t, d), F32), SDS((t, d), BF16)],
        compiler_params=_params(("parallel", "parallel")),
        name="proj_merge")(y_a, y_b, w_a, w_b, proj, proj, b_gate, b_gate, *deps)


def _d_merged_gates(dx, w, proj, b_gate, pa, pb, off, deps=()):
    t, k = dx.shape
    d = w.shape[0]
    tm, tc = _tile(t, 1024), _tile(d, MM_TILE_N)
    nj, ni = d // tc, t // tm
    o0 = off // tc

    def body(dx_ref, w_ref, la_ref, lb_ref, ba_ref, bb_ref, pa_ref, pb_ref, *rest):
        dp_ref, dproj_ref, db_ref, stage, sems = rest[len(deps):]
        j, i = pl.program_id(0), pl.program_id(1)
        step = j * ni + i
        slot = step % 2

        def copies(s, ii, jj):
            rows = pl.ds(pl.multiple_of(ii * tm, tm), tm)
            return [pltpu.make_async_copy(
                stage.at[s, br], dproj_ref.at[rows, pl.ds(pl.multiple_of(off + br * d + jj * tc, 128), tc)],
                sems.at[s, br]) for br in range(2)]

        @pl.when(step >= 2)
        def _():
            for cp in copies(slot, 0, 0):
                cp.wait()

        dm = lax.dot_general(dx_ref[...], w_ref[...], NT, preferred_element_type=F32)

        @pl.when(i == 0)
        def _():
            db_ref[...] = jnp.zeros_like(db_ref)

        for br, (l_ref, b_ref, p_ref) in enumerate(((la_ref, ba_ref, pa_ref), (lb_ref, bb_ref, pb_ref))):
            g = _sigmoid(l_ref[...] + b_ref[...])
            dp_ref[br] = (dm * g).astype(BF16)
            dl = dm * p_ref[...] * g * (1.0 - g)
            stage[slot, br] = dl.astype(BF16)
            db_ref[br] += jnp.sum(dl, axis=0, keepdims=True)
        for cp in copies(slot, i, j):
            cp.start()

        @pl.when(step == ni * nj - 1)
        def _():
            for s in range(min(2, ni * nj)):
                for cp in copies(s, 0, 0):
                    cp.wait()

    tile = pl.BlockSpec((tm, tc), lambda j, i: (i, j))
    return pl.pallas_call(
        body, grid=(nj, ni),
        in_specs=[pl.BlockSpec((tm, k), lambda j, i: (i, 0)), pl.BlockSpec((tc, k), lambda j, i: (j, 0)),
                  pl.BlockSpec((tm, tc), lambda j, i: (i, o0 + j)), pl.BlockSpec((tm, tc), lambda j, i: (i, o0 + nj + j)),
                  pl.BlockSpec((1, tc), lambda j, i: (0, j)), pl.BlockSpec((1, tc), lambda j, i: (0, nj + j)),
                  tile, tile] + [ANY] * len(deps),
        out_specs=[pl.BlockSpec((2, tm, tc), lambda j, i: (0, i, j)), ANY, pl.BlockSpec((2, 1, tc), lambda j, i: (0, 0, j))],
        out_shape=[SDS((2, t, d), BF16), SDS(proj.shape, BF16), SDS((2, 1, d), F32)],
        scratch_shapes=[pltpu.VMEM((2, 2, tm, tc), BF16), pltpu.SemaphoreType.DMA((2, 2))],
        compiler_params=_params(("arbitrary", "arbitrary")),
        name="d_merged_gates")(dx, w, proj, proj, b_gate, b_gate, pa, pb, *deps)


def _ffn_in_swiglu(h, w, deps=()):
    t, d = h.shape
    f = w.shape[1] // 2
    tm, tn = _tile(t, 2048), _tile(f, MM_TILE_N)
    nj = f // tn

    def body(h_ref, wg_ref, wu_ref, *rest):
        g_ref, u_ref, a_ref = rest[len(deps):]
        hv = h_ref[...]
        g = jnp.dot(hv, wg_ref[...], preferred_element_type=F32)
        u = jnp.dot(hv, wu_ref[...], preferred_element_type=F32)
        g_ref[...] = g
        u_ref[...] = u
        a_ref[...] = (g * _sigmoid(g) * u).astype(BF16)

    tile = pl.BlockSpec((tm, tn), lambda i, j: (i, j))
    return pl.pallas_call(
        body, grid=(t // tm, nj),
        in_specs=[pl.BlockSpec((tm, d), lambda i, j: (i, 0)), pl.BlockSpec((d, tn), lambda i, j: (0, j)),
                  pl.BlockSpec((d, tn), lambda i, j: (0, nj + j))] + [ANY] * len(deps),
        out_specs=[tile, tile, tile], out_shape=[SDS((t, f), F32), SDS((t, f), F32), SDS((t, f), BF16)],
        compiler_params=_params(("parallel", "parallel")), name="ffn_in_swiglu")(h, w, w, *deps)


def _d_act_swiglu(dy, w, gate, up, deps=()):
    t, k = dy.shape
    f = w.shape[0]
    tm, tn = _tile(t, 1024), _tile(f, MM_TILE_N)
    ni, nj = t // tm, f // tn

    def body(dy_ref, w_ref, g_ref, u_ref, *rest):
        out_ref, stage, sems = rest[len(deps):]
        i, j = pl.program_id(0), pl.program_id(1)
        step = i * nj + j
        slot = step % 2

        def copies(s, ii, jj):
            rows = pl.ds(pl.multiple_of(ii * tm, tm), tm)
            return [pltpu.make_async_copy(
                stage.at[s, half], out_ref.at[rows, pl.ds(pl.multiple_of(half * f + jj * tn, 128), tn)],
                sems.at[s, half]) for half in range(2)]

        @pl.when(step >= 2)
        def _():
            for cp in copies(slot, 0, 0):
                cp.wait()

        dact = lax.dot_general(dy_ref[...], w_ref[...], NT, preferred_element_type=F32)
        g = g_ref[...]
        sg = _sigmoid(g)
        stage[slot, 0] = (dact * u_ref[...] * _dsilu(g, sg)).astype(BF16)
        stage[slot, 1] = (dact * (g * sg)).astype(BF16)
        for cp in copies(slot, i, j):
            cp.start()

        @pl.when(step == ni * nj - 1)
        def _():
            for s in range(min(2, ni * nj)):
                for cp in copies(s, 0, 0):
                    cp.wait()

    tile = pl.BlockSpec((tm, tn), lambda i, j: (i, j))
    return pl.pallas_call(
        body, grid=(ni, nj),
        in_specs=[pl.BlockSpec((tm, k), lambda i, j: (i, 0)), pl.BlockSpec((tn, k), lambda i, j: (j, 0)), tile, tile]
        + [ANY] * len(deps),
        out_specs=ANY, out_shape=SDS((t, 2 * f), BF16),
        scratch_shapes=[pltpu.VMEM((2, 2, tm, tn), BF16), pltpu.SemaphoreType.DMA((2, 2))],
        compiler_params=_params(("arbitrary", "arbitrary")), name="d_act_swiglu")(dy, w, gate, up, *deps)


def _ffn_out_loss(act, w, x_res, target):
    t, d = x_res.shape
    k = act.shape[1]
    tm, tn = _tile(t, 1024), _tile(d, MM_TILE_N)

    def body(a_ref, w_ref, r_ref, t_ref, dy_ref, dyb_ref, l_ref):
        @pl.when(jnp.logical_and(pl.program_id(0) == 0, pl.program_id(1) == 0))
        def _():
            l_ref[...] = jnp.zeros_like(l_ref)

        y = jnp.dot(a_ref[...], w_ref[...], preferred_element_type=F32) + r_ref[...]
        e = y - t_ref[...]
        dy = e * (1.0 / d)
        dy_ref[...] = dy
        dyb_ref[...] = dy.astype(BF16)
        l_ref[...] += (0.5 / d) * jnp.sum(jnp.sum(e * e, axis=-1, keepdims=True), axis=0, keepdims=True)

    tile = pl.BlockSpec((tm, tn), lambda i, j: (i, j))
    return pl.pallas_call(
        body, grid=(t // tm, d // tn),
        in_specs=[pl.BlockSpec((tm, k), lambda i, j: (i, 0)), pl.BlockSpec((k, tn), lambda i, j: (0, j)), tile, tile],
        out_specs=[tile, tile, pl.BlockSpec((1, 1), lambda i, j: (0, 0))],
        out_shape=[SDS((t, d), F32), SDS((t, d), BF16), SDS((1, 1), F32)],
        compiler_params=_params(("arbitrary", "arbitrary")), name="ffn_out_loss")(act, w, x_res, target)


def _rel_onehot(qi):
    p = lax.broadcasted_iota(jnp.int32, (REL_LANES, BAND), 1)
    r = lax.broadcasted_iota(jnp.int32, (REL_LANES, BAND), 0)
    idx = jnp.clip(qi + PAD - p, -REL_FUTURE, REL_PAST) + REL_FUTURE
    return (idx == r).astype(F32)


def _relbias_expand(rb):
    h = rb.shape[0]

    def body(rb_ref, o_ref):
        def step(qi, _):
            o_ref[qi] = _fdot(rb_ref[...], _rel_onehot(qi))
            return 0

        lax.fori_loop(0, CHUNK, step, 0)

    return pl.pallas_call(body, out_shape=SDS((CHUNK, h, BAND), F32), compiler_params=_params(),
                          name="relbias_expand")(rb)


def _relbias_reduce(dbias):
    h = dbias.shape[1]

    rows_per_pass = 4

    def body(db_ref, o_ref):
        def step(i, acc):
            parts = []
            for u in range(rows_per_pass):
                qi = i * rows_per_pass + u
                xv = db_ref[qi]
                hi = xv.astype(BF16)
                rest = xv - hi.astype(F32)
                mid = rest.astype(BF16)
                low = (rest - mid.astype(F32)).astype(BF16)
                parts.append(lax.dot_general(jnp.concatenate([hi, mid, low], axis=0), _rel_onehot(qi).astype(BF16), NT,
                                             preferred_element_type=F32))
            for part in parts:
                acc = acc + (part[0:h] + part[h:2 * h] + part[2 * h:3 * h])
            return acc

        o_ref[...] = lax.fori_loop(0, CHUNK // rows_per_pass, step, jnp.zeros((h, REL_LANES), F32))

    return pl.pallas_call(body, out_shape=SDS((h, REL_LANES), F32), compiler_params=_params(),
                          name="relbias_reduce")(dbias)


def _lower_bound(l_ref):
    l0, l1 = l_ref[0:1, :], l_ref[1:2, :]
    m = jnp.maximum(l0, l1)
    e0, e1 = jnp.exp(l0 - m), jnp.exp(l1 - m)
    return e0 / (e0 + e1)


def _tri(lower):
    r = lax.broadcasted_iota(jnp.int32, (CHUNK, CHUNK), 0)
    c = lax.broadcasted_iota(jnp.int32, (CHUNK, CHUNK), 1)
    return r >= c if lower else r <= c


def _hgrn_intra(qs, kk, b_s):
    rows = lax.broadcasted_iota(jnp.int32, (CHUNK, HEAD), 0)
    b = b_s[...]
    out = []
    for i in range(CHUNK // SUB):
        lo = i * SUB
        ref = jnp.zeros((1, HEAD), F32) if i == 0 else b_s[lo - 1:lo, :]
        eq = jnp.exp(b[lo:lo + SUB] - ref)
        qt = _split(qs[lo:lo + SUB] * eq)
        e = jnp.where(rows < lo + SUB, jnp.exp(jnp.minimum(ref - b, EXP_CLAMP)), 0.0)
        kt = _split(kk * e)
        out.append((eq, qt, e, kt))
    return out


def _hgrn_scores(blocks):
    tr = lax.broadcasted_iota(jnp.int32, (SUB, CHUNK), 0)
    tc = lax.broadcasted_iota(jnp.int32, (SUB, CHUNK), 1)
    return jnp.concatenate([jnp.where(tc <= tr + i * SUB, _dot3(qt, kt, NT), 0.0)
                            for i, (_, qt, _, kt) in enumerate(blocks)], axis=0)


def _hgrn_fwd(proj, lb_logits, gain, n_heads):
    t = proj.shape[0]
    nc = t // CHUNK
    da = n_heads * HEAD
    hp = MIX_HEADS
    wide = hp * HEAD

    def body(q_ref, f_ref, i_ref, g_ref, l_ref, gain_ref, y_ref, o_ref, st_ref, state, b_s):
        state[...] = jnp.zeros_like(state)
        lb_all = _lower_bound(l_ref)
        tril = _tri(True).astype(F32)

        def chunks(i, _):
            dot = functools.partial(lax.dot_general, preferred_element_type=F32)
            items = []
            for u in range(MIX_UNROLL):
                for hh in range(hp):
                    j = i * MIX_UNROLL + u
                    sl = pl.ds(pl.multiple_of(j * CHUNK, CHUNK), CHUNK)
                    cols = slice(hh * HEAD, (hh + 1) * HEAD)
                    lb = lb_all[:, cols]
                    fg = lb + (1.0 - lb) * _sigmoid(f_ref[sl, cols])
                    qv = q_ref[sl, cols]
                    gv = g_ref[sl, cols]
                    items.append(dict(hh=hh, j=j, sl=sl, cols=cols, lf=jnp.log(fg), kk=1.0 - fg, qs=qv * _sigmoid(qv),
                                      vb=i_ref[sl, cols].astype(BF16), gate=gv * _sigmoid(gv)))
            for it in items:
                it["b"] = _fdot(tril, it["lf"])
            for slot, it in enumerate(items):
                b = it["b"]
                b_s[slot] = b
                it["blocks"] = _hgrn_intra(it["qs"], it["kk"], b_s.at[slot])
                it["ebl"] = jnp.exp(b_s[slot, CHUNK - 1:CHUNK, :])
                it["qe"] = (it["qs"] * jnp.exp(b)).astype(BF16)
                it["kd"] = (it["kk"] * jnp.exp(b_s[slot, CHUNK - 1:CHUNK, :] - b)).astype(BF16)
            for it in items:
                it["a"] = _hgrn_scores(it["blocks"]).astype(BF16)
            for it in items:
                it["kv"] = dot(it["vb"], it["kd"], TN)
                it["o"] = dot(it["a"], it["vb"], NN)
            s_now = [state[hh] for hh in range(hp)]
            for it in items:
                it["s_in"] = s_now[it["hh"]]
                s_now[it["hh"]] = it["s_in"] * it["ebl"] + it["kv"]
            for hh in range(hp):
                state[hh] = s_now[hh]
            for it in items:
                it["o"] = it["o"] + dot(it["qe"], it["s_in"].astype(BF16), NT)
            for it in items:
                o, sl, cols = it["o"], it["sl"], it["cols"]
                st_ref[it["hh"], it["j"]] = it["s_in"]
                o_ref[sl, cols] = o
                rr = lax.rsqrt(jnp.mean(o * o, axis=-1, keepdims=True) + EPS)
                y_ref[sl, cols] = (o * rr * gain_ref[:, cols] * it["gate"]).astype(BF16)
            return 0

        assert nc % MIX_UNROLL == 0, (nc, MIX_UNROLL)
        lax.fori_loop(0, nc // MIX_UNROLL, chunks, 0)

    col = lambda k: pl.BlockSpec((t, wide), lambda h: (0, k * (n_heads // hp) + h))
    vec = pl.BlockSpec((1, wide), lambda h: (0, h))
    return pl.pallas_call(
        body, grid=(n_heads // hp,),
        in_specs=[col(0), col(1), col(2), col(3), pl.BlockSpec((2, wide), lambda h: (0, h)), vec],
        out_specs=[pl.BlockSpec((t, wide), lambda h: (0, h)), pl.BlockSpec((t, wide), lambda h: (0, h)),
                   pl.BlockSpec((hp, nc, HEAD, HEAD), lambda h: (h, 0, 0, 0))],
        out_shape=[SDS((t, da), BF16), SDS((t, da), F32), SDS((n_heads, nc, HEAD, HEAD), F32)],
        scratch_shapes=[pltpu.VMEM((hp, HEAD, HEAD), F32), pltpu.VMEM((hp * MIX_UNROLL, CHUNK, HEAD), F32)],
        compiler_params=_params(("parallel",)), name="hgrn_fwd")(proj, proj, proj, proj, lb_logits, gain)


def _write_column_groups(res, dproj_ref, sems, col0, stride, h, width):
    copies = [pltpu.make_async_copy(
        res.at[p], dproj_ref.at[:, pl.ds(pl.multiple_of((col0 + p * stride + h) * width, HEAD), width)], sems.at[p])
        for p in range(res.shape[0])]
    for cp in copies:
        cp.start()
    for cp in copies:
        cp.wait()


def _hgrn_bwd(dproj, proj, o_pre, states, dy, lb_logits, gain, n_heads, deps=()):
    t = proj.shape[0]
    nc = t // CHUNK
    da = n_heads * HEAD
    hp = MIX_HEADS
    wide = hp * HEAD

    def body(*refs):
        (q_ref, f_ref, i_ref, g_ref, o_ref, st_ref, dy_ref, l_ref, gain_ref,
         dproj_ref, dl_ref, dgain_ref, res, dstate, b_s, out_sems) = refs[1 + len(deps):]

        def compute():
            dstate[...] = jnp.zeros_like(dstate)
            lb_all = _lower_bound(l_ref)
            tril_m, tril, triu = _tri(True), _tri(True).astype(F32), _tri(False).astype(F32)
            last = lax.broadcasted_iota(jnp.int32, (CHUNK, HEAD), 0) == CHUNK - 1

            def chunks(i, carry):
                dot = functools.partial(lax.dot_general, preferred_element_type=F32)
                items = []
                for u in range(MIX_UNROLL_BWD):
                    for hh in range(hp):
                        j = nc - 1 - (i * MIX_UNROLL_BWD + u)
                        sl = pl.ds(pl.multiple_of(j * CHUNK, CHUNK), CHUNK)
                        cols = slice(hh * HEAD, (hh + 1) * HEAD)
                        lb, gain_v = lb_all[:, cols], gain_ref[:, cols]
                        sg = _sigmoid(f_ref[sl, cols])
                        fg = lb + (1.0 - lb) * sg
                        qv = q_ref[sl, cols]
                        sq = _sigmoid(qv)
                        gv = g_ref[sl, cols]
                        sgg = _sigmoid(gv)
                        silg = gv * sgg
                        o = o_ref[sl, cols]
                        dyv = dy_ref[sl, cols]
                        rr = lax.rsqrt(jnp.mean(o * o, axis=-1, keepdims=True) + EPS)
                        on = o * rr
                        don = dyv * gain_v * silg
                        do = (rr * don - o * (rr * rr * rr) * jnp.mean(don * o, axis=-1, keepdims=True)).astype(BF16)
                        items.append(dict(
                            hh=hh, j=j, sl=sl, cols=cols, lb=lb, sg=sg, fg=fg, kk=1.0 - fg, qv=qv, sq=sq, qs=qv * sq,
                            vb=i_ref[sl, cols].astype(BF16), do=do, dg=dyv * on * gain_v * _dsilu(gv, sgg),
                            dgain=jnp.sum(dyv * on * silg, axis=0, keepdims=True)))
                for it in items:
                    it["b"] = _fdot(tril, jnp.log(it["fg"]))
                for slot, it in enumerate(items):
                    b = it["b"]
                    b_s[slot] = b
                    it["blocks"] = _hgrn_intra(it["qs"], it["kk"], b_s.at[slot])
                    bl = b_s[slot, CHUNK - 1:CHUNK, :]
                    it["eb"], it["ebl"], it["ekd"] = jnp.exp(b), jnp.exp(bl), jnp.exp(bl - b)
                    it["s_in"] = st_ref[it["hh"], it["j"]]
                for it in items:
                    it["a"] = _hgrn_scores(it["blocks"]).astype(BF16)
                    it["da"] = jnp.where(tril_m, dot(it["do"], it["vb"], NT), 0.0)
                for it in items:
                    dq_rows = []
                    dk = jnp.zeros((CHUNK, HEAD), F32)
                    for blk, (eq, qt, e, kt) in enumerate(it["blocks"]):
                        da_i = _split(it["da"][blk * SUB:(blk + 1) * SUB])
                        dq_rows.append(eq * _dot3(da_i, kt, NN))
                        dk = dk + e * _dot3(da_i, qt, TN)
                    it["dq"] = jnp.concatenate(dq_rows, axis=0) + dot(it["do"], it["s_in"].astype(BF16), NN) * it["eb"]
                    it["dk"] = dk
                    it["dv"] = dot(it["a"], it["do"], TN)
                    it["g"] = dot(it["do"], (it["qs"] * it["eb"]).astype(BF16), TN)
                ds_now = [dstate[hh] for hh in range(hp)]
                for it in items:
                    it["ds_out"] = ds_now[it["hh"]]
                    ds_now[it["hh"]] = it["ds_out"] * it["ebl"] + it["g"]
                for hh in range(hp):
                    dstate[hh] = ds_now[hh]
                for it in items:
                    dsb = it["ds_out"].astype(BF16)
                    it["dv"] = it["dv"] + dot((it["kk"] * it["ekd"]).astype(BF16), dsb, NT)
                    it["dk_state"] = it["ekd"] * dot(it["vb"], dsb, NN)
                for it in items:
                    kk, dk_state = it["kk"], it["dk_state"]
                    it["dk"] = it["dk"] + dk_state
                    extra = (jnp.sum(kk * dk_state, axis=0, keepdims=True)
                             + it["ebl"] * jnp.sum(it["s_in"] * it["ds_out"], axis=0, keepdims=True))
                    it["db"] = it["qs"] * it["dq"] - kk * it["dk"] + jnp.where(last, extra, 0.0)
                for it in items:
                    it["dlf"] = _fdot(triu, it["db"])
                carry = list(carry)
                for it in items:
                    hh, sl, cols, sg, lb = it["hh"], it["sl"], it["cols"], it["sg"], it["lb"]
                    dfg = it["dlf"] / it["fg"] - it["dk"]
                    dlb_acc, dgain_acc = carry[hh]
                    carry[hh] = (dlb_acc + jnp.sum(dfg * (1.0 - sg), axis=0, keepdims=True), dgain_acc + it["dgain"])
                    res[0, sl, cols] = (it["dq"] * _dsilu(it["qv"], it["sq"])).astype(BF16)
                    res[1, sl, cols] = (dfg * (1.0 - lb) * sg * (1.0 - sg)).astype(BF16)
                    res[2, sl, cols] = it["dv"].astype(BF16)
                    res[3, sl, cols] = it["dg"].astype(BF16)
                return tuple(carry)

            assert nc % MIX_UNROLL_BWD == 0, (nc, MIX_UNROLL_BWD)
            zero = jnp.zeros((1, HEAD), F32)
            sums = lax.fori_loop(0, nc // MIX_UNROLL_BWD, chunks, ((zero, zero),) * hp)
            for hh, (dlb, dgain) in enumerate(sums):
                cols = slice(hh * HEAD, (hh + 1) * HEAD)
                lb = lb_all[:, cols]
                dgain_ref[:, cols] = dgain
                dl0 = dlb * lb * (1.0 - lb)
                dl_ref[0:1, cols] = dl0
                dl_ref[1:2, cols] = -dl0

        compute()
        _write_column_groups(res, dproj_ref, out_sems, 0, ng, pl.program_id(0), wide)

    ng = n_heads // hp
    col = lambda k: pl.BlockSpec((t, wide), lambda h: (0, k * ng + h))
    head = pl.BlockSpec((t, wide), lambda h: (0, h))
    vec = pl.BlockSpec((1, wide), lambda h: (0, h))
    return pl.pallas_call(
        body, grid=(ng,),
        in_specs=[ANY] * (1 + len(deps)) + [col(0), col(1), col(2), col(3), head,
                  pl.BlockSpec((hp, nc, HEAD, HEAD), lambda h: (h, 0, 0, 0)),
                  head, pl.BlockSpec((2, wide), lambda h: (0, h)), vec],
        out_specs=[ANY, pl.BlockSpec((2, wide), lambda h: (0, h)), vec],
        out_shape=[SDS(dproj.shape, BF16), SDS((2, da), F32), SDS((1, da), F32)],
        scratch_shapes=[pltpu.VMEM((4, t, wide), BF16), pltpu.VMEM((hp, HEAD, HEAD), F32),
                        pltpu.VMEM((hp * MIX_UNROLL_BWD, CHUNK, HEAD), F32), pltpu.SemaphoreType.DMA((4,))],
        input_output_aliases={0: 0}, compiler_params=_params(("arbitrary",)),
        name="hgrn_bwd")(dproj, *deps, proj, proj, proj, proj, o_pre, states, dy, lb_logits, gain)


ROWS = 256


def _head_norm(x_ref, gain, dst, dst_off, t):
    def step(i, _):
        sl = pl.ds(pl.multiple_of(i * ROWS, ROWS), ROWS)
        xv = x_ref[sl, :]
        r = lax.rsqrt(jnp.mean(xv * xv, axis=-1, keepdims=True) + EPS)
        dst[pl.ds(pl.multiple_of(dst_off + i * ROWS, ROWS), ROWS), :] = (xv * r * gain).astype(BF16)
        return 0

    lax.fori_loop(0, t // ROWS, step, 0)


def _head_norm_bwd(x_ref, gain, dn_ref, dn_off, out, slot, t):
    def step(i, acc):
        sl = pl.ds(pl.multiple_of(i * ROWS, ROWS), ROWS)
        xv = x_ref[sl, :]
        dn = dn_ref[pl.ds(pl.multiple_of(dn_off + i * ROWS, ROWS), ROWS), :]
        r = lax.rsqrt(jnp.mean(xv * xv, axis=-1, keepdims=True) + EPS)
        u = dn * gain
        out[slot, sl, :] = (r * u - xv * (r * r * r) * jnp.mean(u * xv, axis=-1, keepdims=True)).astype(out.dtype)
        return acc + jnp.sum(dn * (xv * r), axis=0, keepdims=True)

    return lax.fori_loop(0, t // ROWS, step, jnp.zeros((1, HEAD), F32))


def _attn_scores(qn, kpad, n):
    qc = qn[pl.ds(pl.multiple_of(n * CHUNK, CHUNK), CHUNK), :]
    band = pl.ds(pl.multiple_of(n * CHUNK, CHUNK), BAND)
    return qc, band, lax.dot_general(qc, kpad[band, :], NT, preferred_element_type=F32)


def _attn_softmax(raw, bias_ref, n):
    s = raw * (HEAD ** -0.5) + bias_ref[0]
    col = lax.broadcasted_iota(jnp.int32, (CHUNK, BAND), 1)
    s = jnp.where(col >= PAD - n * CHUNK, s, -jnp.inf)
    p = jnp.exp(s - jnp.max(s, axis=-1, keepdims=True))
    return p / jnp.sum(p, axis=-1, keepdims=True)


def _attn_fwd(proj, q_gain, k_gain, bias, n_heads, col0):
    t = proj.shape[0]
    nc = t // CHUNK

    def body(q_ref, k_ref, v_ref, qg_ref, kg_ref, bias_ref, y_ref, qn, kpad, vpad):
        kpad[0:PAD, :] = jnp.zeros((PAD, HEAD), BF16)
        vpad[0:PAD, :] = jnp.zeros((PAD, HEAD), BF16)
        _head_norm(q_ref, qg_ref[...], qn, 0, t)
        _head_norm(k_ref, kg_ref[...], kpad, PAD, t)

        def copy_v(i, _):
            vpad[pl.ds(pl.multiple_of(PAD + i * ROWS, ROWS), ROWS), :] = v_ref[
                pl.ds(pl.multiple_of(i * ROWS, ROWS), ROWS), :].astype(BF16)
            return 0

        lax.fori_loop(0, t // ROWS, copy_v, 0)

        def chunks(i, _):
            ns = [i * ATT_UNROLL + u for u in range(ATT_UNROLL)]
            scored = [_attn_scores(qn, kpad, n) for n in ns]
            probs = [_attn_softmax(raw, bias_ref, n).astype(BF16) for n, (_, _, raw) in zip(ns, scored)]
            outs = [lax.dot_general(p, vpad[band, :], NN, preferred_element_type=F32).astype(BF16)
                    for p, (_, band, _) in zip(probs, scored)]
            for n, o in zip(ns, outs):
                y_ref[pl.ds(pl.multiple_of(n * CHUNK, CHUNK), CHUNK), :] = o
            return 0

        assert nc % ATT_UNROLL == 0, (nc, ATT_UNROLL)
        lax.fori_loop(0, nc // ATT_UNROLL, chunks, 0)

    col = lambda k: pl.BlockSpec((t, HEAD), lambda h: (0, col0 + k * n_heads + h))
    vec = pl.BlockSpec((1, HEAD), lambda h: (0, 0))
    return pl.pallas_call(
        body, grid=(n_heads,),
        in_specs=[col(0), col(1), col(2), vec, vec, pl.BlockSpec((1, CHUNK, BAND), lambda h: (h, 0, 0))],
        out_specs=pl.BlockSpec((t, HEAD), lambda h: (0, h)), out_shape=SDS((t, n_heads * HEAD), BF16),
        scratch_shapes=[pltpu.VMEM((t, HEAD), BF16), pltpu.VMEM((t + PAD, HEAD), BF16), pltpu.VMEM((t + PAD, HEAD), BF16)],
        compiler_params=_params(("parallel",)), name="attn_fwd")(proj, proj, proj, q_gain, k_gain, bias)


def _attn_bwd(dproj, proj, q_gain, k_gain, bias, dy, n_heads, col0, deps=()):
    t = proj.shape[0]
    nc = t // CHUNK

    def body(*refs):
        (q_ref, k_ref, v_ref, qg_ref, kg_ref, bias_ref, dy_ref,
         dproj_ref, dbias_ref, dqg_ref, dkg_ref, qn, kpad, vpad, dqn, dk_acc, dv_acc, res,
         out_sems) = refs[1 + len(deps):]
        h = pl.program_id(0)

        def compute():
            kpad[0:PAD, :] = jnp.zeros((PAD, HEAD), BF16)
            vpad[0:PAD, :] = jnp.zeros((PAD, HEAD), BF16)
            _head_norm(q_ref, qg_ref[...], qn, 0, t)
            _head_norm(k_ref, kg_ref[...], kpad, PAD, t)

            def prep(i, _):
                sl = pl.ds(pl.multiple_of(PAD + i * ROWS, ROWS), ROWS)
                vpad[sl, :] = v_ref[pl.ds(pl.multiple_of(i * ROWS, ROWS), ROWS), :].astype(BF16)
                return 0

            lax.fori_loop(0, t // ROWS, prep, 0)

            def clear(i, _):
                sl = pl.ds(pl.multiple_of(i * ROWS, ROWS), ROWS)
                dk_acc[sl, :] = jnp.zeros((ROWS, HEAD), F32)
                dv_acc[sl, :] = jnp.zeros((ROWS, HEAD), F32)
                return 0

            lax.fori_loop(0, (t + PAD) // ROWS, clear, 0)
            dbias_ref[0] = jnp.zeros((CHUNK, BAND), F32)

            def chunks(i, _):
                dot = functools.partial(lax.dot_general, preferred_element_type=F32)
                ns = [i * ATT_UNROLL_BWD + u for u in range(ATT_UNROLL_BWD)]
                scored = [_attn_scores(qn, kpad, n) for n in ns]
                dos = [dy_ref[pl.ds(pl.multiple_of(n * CHUNK, CHUNK), CHUNK), :].astype(BF16) for n in ns]
                dps = [dot(do, vpad[band, :], NT) for do, (_, band, _) in zip(dos, scored)]
                ps, dss = [], []
                for n, (_, _, raw), dp in zip(ns, scored, dps):
                    p = _attn_softmax(raw, bias_ref, n)
                    ds = p * (dp - jnp.sum(dp * p, axis=-1, keepdims=True))
                    dbias_ref[0] += ds
                    ps.append(p.astype(BF16))
                    dss.append((ds * (HEAD ** -0.5)).astype(BF16))
                dqs = [dot(d, kpad[band, :], NN) for d, (_, band, _) in zip(dss, scored)]
                dks = [dot(d, qc, TN) for d, (qc, _, _) in zip(dss, scored)]
                dvs = [dot(p, do, TN) for p, do in zip(ps, dos)]
                for n, (_, band, _), dq, dk, dv in zip(ns, scored, dqs, dks, dvs):
                    dqn[pl.ds(pl.multiple_of(n * CHUNK, CHUNK), CHUNK), :] = dq
                    dk_acc[band, :] += dk
                    dv_acc[band, :] += dv
                return 0

            assert nc % ATT_UNROLL_BWD == 0, (nc, ATT_UNROLL_BWD)
            lax.fori_loop(0, nc // ATT_UNROLL_BWD, chunks, 0)
            dqg = _head_norm_bwd(q_ref, qg_ref[...], dqn, 0, res, 0, t)
            dkg = _head_norm_bwd(k_ref, kg_ref[...], dk_acc, PAD, res, 1, t)

            def put_v(i, _):
                sl = pl.ds(pl.multiple_of(i * ROWS, ROWS), ROWS)
                res[2, sl, :] = dv_acc[pl.ds(pl.multiple_of(PAD + i * ROWS, ROWS), ROWS), :].astype(BF16)
                return 0

            lax.fori_loop(0, t // ROWS, put_v, 0)

            @pl.when(h == 0)
            def _():
                dqg_ref[...] = jnp.zeros_like(dqg_ref)
                dkg_ref[...] = jnp.zeros_like(dkg_ref)

            dqg_ref[...] += dqg
            dkg_ref[...] += dkg

        compute()
        _write_column_groups(res, dproj_ref, out_sems, col0, n_heads, h, HEAD)

    col = lambda k: pl.BlockSpec((t, HEAD), lambda h: (0, col0 + k * n_heads + h))
    vec = pl.BlockSpec((1, HEAD), lambda h: (0, 0))
    btile = pl.BlockSpec((1, CHUNK, BAND), lambda h: (h, 0, 0))
    return pl.pallas_call(
        body, grid=(n_heads,),
        in_specs=[ANY] * (1 + len(deps)) + [col(0), col(1), col(2), vec, vec, btile,
                                            pl.BlockSpec((t, HEAD), lambda h: (0, h))],
        out_specs=[ANY, btile, vec, vec],
        out_shape=[SDS(dproj.shape, BF16), SDS((n_heads, CHUNK, BAND), F32), SDS((1, HEAD), F32), SDS((1, HEAD), F32)],
        scratch_shapes=[pltpu.VMEM((t, HEAD), BF16), pltpu.VMEM((t + PAD, HEAD), BF16), pltpu.VMEM((t + PAD, HEAD), BF16),
                        pltpu.VMEM((t, HEAD), F32), pltpu.VMEM((t + PAD, HEAD), F32), pltpu.VMEM((t + PAD, HEAD), F32),
                        pltpu.VMEM((3, t, HEAD), BF16), pltpu.SemaphoreType.DMA((3,))],
        input_output_aliases={0: 0}, compiler_params=_params(("arbitrary",)),
        name="attn_bwd")(dproj, *deps, proj, proj, proj, q_gain, k_gain, bias, dy)


def _place():
    x, y, c = lax.axis_index("x"), lax.axis_index("y"), lax.axis_index("c")
    others = [(1 - x, y), (x, 1 - y), (1 - x, 1 - y)]
    return x, y, c, others


def _chunk_of(ref, kind, chip, half, shard_shape):
    r, n = shard_shape
    hr = r // 2
    if kind == "col":
        rows = pl.ds(0, r) if half is None else pl.ds(half * hr, hr)
        return ref.at[rows, pl.ds(chip * n, n)]
    rows = pl.ds(chip * r, r) if half is None else pl.ds(chip * r + half * hr, hr)
    return ref.at[rows, :]


EFFECT = pltpu.SideEffectType.DATAFLOW_SIDE_EFFECTING


def _start_copies(name, bufs, plan, n, deps):
    nb, nd = len(bufs), len(deps)

    def body(*refs):
        send, recv, token = refs[nb + nd], refs[nb + nd + 1], refs[-1]
        for cp in plan(refs[:nb], send, recv)[0]:
            cp.start()
        token[...] = jnp.zeros_like(token)

    out = pl.pallas_call(
        body, name=name,
        out_shape=(pltpu.SemaphoreType.DMA((n,)), pltpu.SemaphoreType.DMA((n,)),
                   *[pltpu.HBM(b.shape, b.dtype) for b in bufs], SDS((8, 128), F32)),
        in_specs=[HBM] * nb + [ANY] * nd,
        out_specs=(SEM, SEM, *[HBM] * nb, pl.BlockSpec(memory_space=pltpu.VMEM)),
        input_output_aliases={i: 2 + i for i in range(nb)},
        compiler_params=pltpu.CompilerParams(has_side_effects=EFFECT),
    )(*[pltpu.with_memory_space_constraint(b, pltpu.HBM) for b in bufs], *deps)
    return out[0], out[1], list(out[2:2 + nb]), out[-1]


def _wait_copies(name, bufs, send, recv, plan, after):
    nb = len(bufs)

    def body(*refs):
        sends, recvs = plan(refs[:nb], refs[nb], refs[nb + 1])
        for cp in sends:
            cp.wait_send()
        for cp in recvs:
            cp.wait_recv()

    out = pl.pallas_call(
        body, name=name, out_shape=tuple(pltpu.HBM(b.shape, b.dtype) for b in bufs),
        in_specs=[HBM] * nb + [SEM, SEM] + [ANY] * len(after), out_specs=tuple([HBM] * nb),
        input_output_aliases={i: i for i in range(nb)},
        compiler_params=pltpu.CompilerParams(has_side_effects=EFFECT),
    )(*bufs, send, recv, *after)
    return list(out)


def _remote(src, dst, send, recv, i, dev):
    return pltpu.make_async_remote_copy(src_ref=src, dst_ref=dst, send_sem=send.at[i], recv_sem=recv.at[i],
                                        device_id=dev, device_id_type=MESH)


ALL_RELATIONS = (0, 1, 2)


def _plan_gather_ici(kinds, shapes, rels=ALL_RELATIONS):
    def plan(refs, send, recv):
        x, y, c, others = _place()
        sends, recvs = [], []
        for w, (kind, ss) in enumerate(zip(kinds, shapes)):
            for p in rels:
                px, py = others[p]
                mine = _chunk_of(refs[w], kind, 2 * x + y, c, ss)
                theirs = _chunk_of(refs[w], kind, 2 * px + py, c, ss)
                sends.append(_remote(mine, mine, send, recv, 3 * w + p, (px, py, c)))
                recvs.append(_remote(theirs, theirs, send, recv, 3 * w + p, (px, py, c)))
        return sends, recvs

    return plan, 3 * len(kinds)


def _plan_gather_pass(kinds, shapes, rels=ALL_RELATIONS):
    def plan(refs, send, recv):
        x, y, c, others = _place()
        sends, recvs = [], []
        for w, (kind, ss) in enumerate(zip(kinds, shapes)):
            for i, p in enumerate(rels):
                px, py = others[p]
                got = _chunk_of(refs[w], kind, 2 * px + py, c, ss)
                coming = _chunk_of(refs[w], kind, 2 * px + py, 1 - c, ss)
                sends.append(_remote(got, got, send, recv, len(rels) * w + i, (x, y, 1 - c)))
                recvs.append(_remote(coming, coming, send, recv, len(rels) * w + i, (x, y, 1 - c)))
        return sends, recvs

    return plan, len(rels) * len(kinds)


def _plan_pair(kinds, shapes):
    nw = len(kinds)

    def plan(refs, send, recv):
        x, y, c, _ = _place()
        sends = []
        for w, (kind, ss) in enumerate(zip(kinds, shapes)):
            for k in range(4):
                sends.append(_remote(_chunk_of(refs[w], kind, k, 1 - c, ss), refs[nw + w].at[k], send, recv,
                                     4 * w + k, (x, y, 1 - c)))
        return sends, sends

    return plan, 4 * nw


def _plan_chip(nw):
    def plan(refs, send, recv):
        x, y, c, others = _place()
        sends = []
        for w in range(nw):
            for p, (px, py) in enumerate(others):
                sends.append(_remote(refs[w].at[p], refs[nw + w].at[p], send, recv, 3 * w + p, (px, py, c)))
        return sends, sends

    return plan, 3 * nw


def _plan_share(nw):
    def plan(refs, send, recv):
        x, y, c, _ = _place()
        sends = [_remote(refs[w].at[c], refs[w].at[c], send, recv, w, (x, y, 1 - c)) for w in range(nw)]
        recvs = [_remote(refs[w].at[1 - c], refs[w].at[1 - c], send, recv, w, (x, y, 1 - c)) for w in range(nw)]
        return sends, recvs

    return plan, nw


def _grad_half_spec(kind, tr, tn, nr, nn, chunk):
    if kind == "col":
        return pl.BlockSpec((tr, tn), lambda *a: (a[-1][1] * nr + a[-3], chunk(*a) * nn + a[-2]))
    return pl.BlockSpec((tr, tn), lambda *a: ((2 * chunk(*a) + a[-1][1]) * nr + a[-3], a[-2]))


def _pair_add(grad, got, kind, shard_shape, pos, name):
    r, n = shard_shape
    hr = r // 2
    tr, tn = _row_tile(hr, n), n
    nr, nn = hr // tr, n // tn
    g_spec = _grad_half_spec(kind, tr, tn, nr, nn, lambda p, i, j, pos_: pos_[2 + p])
    r_spec = pl.BlockSpec((1, tr, tn), lambda p, i, j, pos_: (pos_[2 + p], i, j))
    o_spec = pl.BlockSpec((1, tr, tn), lambda p, i, j, pos_: (p, i, j))

    def body(pos_ref, g_ref, r_ref, o_ref):
        o_ref[0] = (g_ref[...] + r_ref[0]).astype(BF16)

    return pl.pallas_call(
        body,
        grid_spec=pltpu.PrefetchScalarGridSpec(num_scalar_prefetch=1, grid=(3, nr, nn), in_specs=[g_spec, r_spec],
                                               out_specs=o_spec),
        out_shape=SDS((3, hr, n), BF16),
        compiler_params=_params(("parallel", "parallel", "parallel")), name=name)(pos, grad, got)


def _chip_add(grad, got, got16, kind, shard_shape, pos, name):
    r, n = shard_shape
    hr = r // 2
    tr, tn = _row_tile(hr, n), n
    nr, nn = hr // tr, n // tn
    g_spec = _grad_half_spec(kind, tr, tn, nr, nn, lambda i, j, pos_: pos_[0])
    r_spec = pl.BlockSpec((1, tr, tn), lambda i, j, pos_: (pos_[0], i, j))
    oth = pl.BlockSpec((3, tr, tn), lambda i, j, pos_: (0, i, j))

    def body(pos_ref, g_ref, r_ref, oth_ref, o_ref):
        own = g_ref[...] + r_ref[0]
        o_ref[0] = ((own + oth_ref[0].astype(F32)) + oth_ref[1].astype(F32)) + oth_ref[2].astype(F32)

    return pl.pallas_call(
        body,
        grid_spec=pltpu.PrefetchScalarGridSpec(num_scalar_prefetch=1, grid=(nr, nn), in_specs=[g_spec, r_spec, oth],
                                               out_specs=pl.BlockSpec((1, tr, tn), lambda i, j, pos_: (pos_[1], i, j))),
        out_shape=SDS((2, hr, n), F32), compiler_params=_params(("parallel", "parallel")),
        name=name)(pos, grad, got, got16)


def _adamw_math(w, g, m, v):
    m = ADAM_B1 * m + (1.0 - ADAM_B1) * g
    v = ADAM_B2 * v + (1.0 - ADAM_B2) * (g * g)
    m_hat = m / (1.0 - ADAM_B1 ** ADAM_STEP)
    v_hat = v / (1.0 - ADAM_B2 ** ADAM_STEP)
    return -ADAM_LR * (m_hat / (jnp.sqrt(v_hat) + ADAM_EPS) + ADAM_WD * w), m, v


def _adamw(w, g, m, v, name):
    r, n = w.shape
    tr, tn = _row_tile(r, n), n

    def body(w_ref, g_ref, m_ref, v_ref, d_ref, nm_ref, nv_ref, go_ref):
        gv = g_ref[...]
        d_ref[...], nm_ref[...], nv_ref[...] = _adamw_math(w_ref[...], gv, m_ref[...], v_ref[...])
        go_ref[...] = gv

    tile = pl.BlockSpec((tr, tn), lambda i, j: (i, j))
    return pl.pallas_call(
        body, grid=(r // tr, n // tn), in_specs=[tile] * 4, out_specs=[tile] * 4, out_shape=[SDS((r, n), F32)] * 4,
        compiler_params=_params(("parallel", "parallel")), name=name)(w, g, m, v)


def _small_allreduce_adamw(g, w, m, v, deps=()):
    length = g.shape[1]

    def body(*refs):
        g_ref, w_ref, m_ref, v_ref = refs[:4]
        gs_ref, d_ref, nm_ref, nv_ref, buf, send, recv = refs[4 + len(deps):]
        x, y, c = lax.axis_index("x"), lax.axis_index("y"), lax.axis_index("c")
        me = 4 * x + 2 * y + c
        buf[me] = g_ref[...]
        cps = []
        for d in range(1, 8):
            peer = (x ^ (d >> 2), y ^ ((d >> 1) & 1), c ^ (d & 1))
            cp = pltpu.make_async_remote_copy(src_ref=buf.at[me], dst_ref=buf.at[me], send_sem=send.at[d - 1],
                                              recv_sem=recv.at[d - 1], device_id=peer, device_id_type=MESH)
            cp.start()
            cps.append(cp)
        for cp in cps:
            cp.wait()
        total = buf[0]
        for d in range(1, 8):
            total = total + buf[d]
        gs_ref[...] = total
        d_ref[...], nm_ref[...], nv_ref[...] = _adamw_math(w_ref[...], total, m_ref[...], v_ref[...])

    vm = pl.BlockSpec(memory_space=pltpu.VMEM)
    return pl.pallas_call(
        body, in_specs=[vm] * 4 + [ANY] * len(deps), out_specs=[vm] * 4, out_shape=[SDS((1, length), F32)] * 4,
        scratch_shapes=[pltpu.VMEM((8, 1, length), F32), pltpu.SemaphoreType.DMA((7,)), pltpu.SemaphoreType.DMA((7,))],
        compiler_params=pltpu.CompilerParams(has_side_effects=True), name="small_allreduce_adamw")(g, w, m, v, *deps)


def kernel(x, w_in, b_gate, norm_mix, norm_ffn, hgrn_lb_logits, hgrn_out_gain, q_gain, k_gain, rel_bias, w_proj_a, w_proj_b, w_out, w_ffn_in, w_ffn_out, loss_target, m_w_in, m_b_gate, m_norm_mix, m_norm_ffn, m_hgrn_lb_logits, m_hgrn_out_gain, m_q_gain, m_k_gain, m_rel_bias, m_w_proj_a, m_w_proj_b, m_w_out, m_w_ffn_in, m_w_ffn_out, v_w_in, v_b_gate, v_norm_mix, v_norm_ffn, v_hgrn_lb_logits, v_hgrn_out_gain, v_q_gain, v_k_gain, v_rel_bias, v_w_proj_a, v_w_proj_b, v_w_out, v_w_ffn_in, v_w_ffn_out):
    t, d = x.shape[1], x.shape[2]
    d_a = hgrn_out_gain.shape[1]
    h_a = d_a // HEAD
    h_b = rel_bias.shape[1]
    d_b = h_b * HEAD
    x0 = x.reshape(t, d)
    target = loss_target.reshape(t, d)
    ax, ay = lax.axis_index("x"), lax.axis_index("y")
    pos = jnp.stack([2 * ax + ay, lax.axis_index("c"), 2 * (1 - ax) + ay, 2 * ax + 1 - ay,
                     2 * (1 - ax) + 1 - ay]).astype(jnp.int32)

    names = ["w_in", "w_proj_a", "w_proj_b", "w_out", "w_ffn_in", "w_ffn_out"]
    big = dict(zip(names, [w_in[0], w_proj_a[0], w_proj_b[0], w_out[0], w_ffn_in[0], w_ffn_out[0]]))
    big_m = dict(zip(names, [m_w_in[0], m_w_proj_a[0], m_w_proj_b[0], m_w_out[0], m_w_ffn_in[0], m_w_ffn_out[0]]))
    big_v = dict(zip(names, [v_w_in[0], v_w_proj_a[0], v_w_proj_b[0], v_w_out[0], v_w_ffn_in[0], v_w_ffn_out[0]]))
    kind = dict(zip(names, ["col", "col", "col", "row", "col", "row"]))
    shape = {nm: big[nm].shape for nm in names}

    def gather_start(tag, group, deps):
        plan, n = _plan_gather_ici([kind[g] for g in group], [shape[g] for g in group])
        fulls = [_cast_into_full(big[g], kind[g], pos, "cast_" + g) for g in group]
        send, recv, bufs, token = _start_copies("gather_ici_start_" + tag, fulls, plan, n, deps)
        return (tag, group, plan, send, recv, bufs), token

    def gather_pass(state, after, rels=ALL_RELATIONS, part=""):
        tag, group, _, send, recv, bufs = state
        kinds_, shapes_ = [kind[g] for g in group], [shape[g] for g in group]
        bufs = _wait_copies("gather_ici_wait_" + tag + part, bufs, send, recv,
                            _plan_gather_ici(kinds_, shapes_, rels)[0], after)
        plan, n = _plan_gather_pass(kinds_, shapes_, rels)
        send2, recv2, bufs, token = _start_copies("gather_pass_start_" + tag + part, bufs, plan, n, ())
        return (tag + part, group, plan, send2, recv2, bufs), token

    def gather_done(state, after):
        tag, group, plan, send, recv, bufs = state
        return _wait_copies("gather_pass_wait_" + tag, bufs, send, recv, plan, after)

    def reduce_start(tag, group, grads, deps, sent=None):
        plan, n = _plan_pair([kind[g] for g in group], [shape[g] for g in group])
        srcs = list(grads) if sent is None else list(sent)
        lands = [lax.empty((4, shape[g][0] // 2, shape[g][1]), s.dtype) for g, s in zip(group, srcs)]
        send, recv, bufs, token = _start_copies("pair_start_" + tag, srcs + lands, plan, n, deps)
        kept = None if sent is None else list(grads)
        return dict(tag=tag, group=group, plan=plan, send=send, recv=recv, bufs=bufs, kept=kept), token

    def reduce_pair_done(st, after):
        tag, group, nw = st["tag"], st["group"], len(st["group"])
        bufs = _wait_copies("pair_wait_" + tag, st["bufs"], st["send"], st["recv"], st["plan"], after)
        grads, gots = (bufs[:nw] if st["kept"] is None else st["kept"]), bufs[nw:]
        parts = [_pair_add(g, l, kind[nm], shape[nm], pos, "pair_add_" + nm) for g, l, nm in zip(grads, gots, group)]
        lands = [lax.empty((3, shape[g][0] // 2, shape[g][1]), BF16) for g in group]
        plan, n = _plan_chip(nw)
        send, recv, bufs, token = _start_copies("chip_start_" + tag, parts + lands, plan, n, ())
        return dict(st, plan=plan, send=send, recv=recv, bufs=bufs, grads=grads, gots=gots), token

    def reduce_chip_done(st, after):
        tag, group, nw = st["tag"], st["group"], len(st["group"])
        bufs = _wait_copies("chip_wait_" + tag, st["bufs"], st["send"], st["recv"], st["plan"], after)
        finals = [_chip_add(g, l, got16, kind[nm], shape[nm], pos, "chip_add_" + nm)
                  for g, l, got16, nm in zip(st["grads"], st["gots"], bufs[nw:], group)]
        plan, n = _plan_share(nw)
        send, recv, bufs, token = _start_copies("share_start_" + tag, finals, plan, n, ())
        return dict(st, plan=plan, send=send, recv=recv, bufs=bufs), token

    g_big, upd = {}, {}

    def reduce_finish(st, after):
        bufs = _wait_copies("share_wait_" + st["tag"], st["bufs"], st["send"], st["recv"], st["plan"], after)
        for full, nm in zip(bufs, st["group"]):
            upd[nm] = _adamw(big[nm], full.reshape(shape[nm]), big_m[nm], big_v[nm], "adamw_" + nm)
            g_big[nm] = upd[nm][3]

    ga, token = gather_start("a", ["w_in"], ())
    gb, token = gather_start("b", ["w_proj_a", "w_proj_b", "w_out"], (token,))
    gc, token = gather_start("c", ["w_ffn_in"], (token,))
    gd, token = gather_start("d", ["w_ffn_out"], (token,))
    h1, r1 = _rmsnorm_fwd(x0, norm_mix, "rmsnorm_mix")
    rb = jnp.pad(rel_bias[0], ((0, 0), (0, REL_LANES - N_REL)))
    bias = _relbias_expand(rb).transpose(1, 0, 2)
    proj = _matmul_chunks(h1, big["w_in"], (0,), None, pos, "proj_in_own", own_shard=True)
    ici_a = ga
    ga, token = gather_pass(ici_a, (h1, bias, proj, token), rels=(0, 1), part="_near")
    (wg_in,) = gather_done(ga, ())
    proj = _matmul_chunks(h1, wg_in, (2, 3), proj, pos, "proj_in_near")
    ga, token = gather_pass(ici_a[:5] + ([wg_in],), (proj,), rels=(2,), part="_far")
    (wg_in,) = gather_done(ga, ())
    proj = _matmul_chunks(h1, wg_in, (4,), proj, pos, "proj_in_far")
    y_a, o_pre, states = _hgrn_fwd(proj, hgrn_lb_logits, hgrn_out_gain, h_a)
    gb, token = gather_pass(gb, (y_a,))
    col_b = 4 * d_a // HEAD
    y_b = _attn_fwd(proj, q_gain, k_gain, bias, h_b, col_b)
    wg_pa, wg_pb, wg_out = gather_done(gb, (y_b,))
    gate_off = 4 * d_a + 3 * d_b
    pa, pb, merged = _proj_merge(y_a, y_b, wg_pa, wg_pb, proj, b_gate, gate_off, deps=(token,))
    x2 = _matmul(merged, wg_out, res=x0, name="out_proj")
    gc, token = gather_pass(gc, (x2,))
    h2, r2 = _rmsnorm_fwd(x2, norm_ffn, "rmsnorm_ffn")
    (wg_fin,) = gather_done(gc, (h2,))
    ff_gate, ff_up, act = _ffn_in_swiglu(h2, wg_fin, deps=(token,))
    gd, token = gather_pass(gd, (act,))
    (wg_fout,) = gather_done(gd, ())
    dy, dy16, loss_part = _ffn_out_loss(act, wg_fout, x2, target)

    g_fout, g_fout16 = _matmul(act, dy16, ta=True, name="dw_ffn_out", also_bf16=True)
    r_fout, token = reduce_start("fout", ["w_ffn_out"], [g_fout], (), sent=[g_fout16])
    dgu = _d_act_swiglu(dy16, wg_fout, ff_gate, ff_up, deps=(token,))
    r_fout, token = reduce_pair_done(r_fout, (dgu,))
    g_fin, g_fin16 = _matmul(h2, dgu, ta=True, name="dw_ffn_in", deps=(token,), also_bf16=True)
    r_fin, token = reduce_start("fin", ["w_ffn_in"], [g_fin], (), sent=[g_fin16])
    dh2 = _matmul(dgu, wg_fin, tb=True, name="d_h2", deps=(token,))
    r_fout, token_a = reduce_chip_done(r_fout, (dh2,))
    r_fin, token_b = reduce_pair_done(r_fin, (dh2,))
    dx2, dx2_16, g_norm_ffn = _rmsnorm_bwd(dh2, x2, r2, norm_ffn, dy, "rmsnorm_ffn_bwd", deps=(token_a, token_b))
    dp_ab, dproj, g_bgate = _d_merged_gates(dx2_16, wg_out, proj, b_gate, pa, pb, gate_off)
    g_out, g_out16 = _matmul(merged, dx2_16, ta=True, name="dw_out", also_bf16=True)
    g_pa, g_pa16 = _matmul(y_a, dp_ab, ta=True, name="dw_proj_a", b_lead=0, also_bf16=True)
    g_pb, g_pb16 = _matmul(y_b, dp_ab, ta=True, name="dw_proj_b", b_lead=1, also_bf16=True)
    r_mid, token = reduce_start("mid", ["w_proj_a", "w_proj_b", "w_out"], [g_pa, g_pb, g_out], (),
                                sent=[g_pa16, g_pb16, g_out16])
    dy_a = _matmul(dp_ab, wg_pa, tb=True, name="d_y_a", deps=(token,), a_lead=0)
    dy_b = _matmul(dp_ab, wg_pb, tb=True, name="d_y_b", a_lead=1)
    r_mid, token_b = reduce_pair_done(r_mid, (dy_b,))
    dproj, dbias, g_qg, g_kg = _attn_bwd(dproj, proj, q_gain, k_gain, bias, dy_b, h_b, col_b, deps=(token_b,))
    r_fin, token_a = reduce_chip_done(r_fin, (dbias,))
    r_mid, token = reduce_chip_done(r_mid, (dbias,))
    dproj, g_lb, g_gain = _hgrn_bwd(dproj, proj, o_pre, states, dy_a, hgrn_lb_logits, hgrn_out_gain, h_a,
                                    deps=(token, token_a))
    g_in, g_in16 = _matmul(h1, dproj, ta=True, name="dw_in", also_bf16=True)
    r_in, token = reduce_start("in", ["w_in"], [g_in], (), sent=[g_in16])
    g_rb = _relbias_reduce(dbias.transpose(1, 0, 2))[:, :N_REL]
    reduce_finish(r_mid, (token,))
    r_in, token = reduce_pair_done(r_in, (g_rb, upd["w_out"][0]))
    dh1 = _matmul(dproj, wg_in, tb=True, name="d_h1", deps=(token,))
    dx, g_norm_mix = _rmsnorm_bwd(dh1, x0, r1, norm_mix, dx2, "rmsnorm_mix_bwd", with_bf16=False)
    reduce_finish(r_fin, (dx,))
    reduce_finish(r_fout, (dx,))
    r_in, token = reduce_chip_done(r_in, (upd["w_ffn_in"][0], upd["w_ffn_out"][0], upd["w_proj_a"][0],
                                          upd["w_proj_b"][0]))

    small_w = [b_gate, norm_mix, norm_ffn, hgrn_lb_logits, hgrn_out_gain, q_gain, k_gain, rel_bias]
    small_m = [m_b_gate, m_norm_mix, m_norm_ffn, m_hgrn_lb_logits, m_hgrn_out_gain, m_q_gain, m_k_gain, m_rel_bias]
    small_v = [v_b_gate, v_norm_mix, v_norm_ffn, v_hgrn_lb_logits, v_hgrn_out_gain, v_q_gain, v_k_gain, v_rel_bias]
    small_g = [g_bgate, g_norm_mix, g_norm_ffn, g_lb, g_gain, g_qg, g_kg, g_rb]
    sizes = [w.size for w in small_w]
    length = -(-(sum(sizes) + 1) // 128) * 128

    def pack(parts_):
        flat = jnp.concatenate([p.reshape(1, -1) for p in parts_], axis=1)
        return jnp.pad(flat, ((0, 0), (0, length - flat.shape[1])))

    one = jnp.ones((1, 1), F32)
    packed = _small_allreduce_adamw(pack(small_g + [loss_part]), pack(small_w + [one]), pack(small_m + [one]),
                                    pack(small_v + [one]), deps=(token,))

    def unpack(vec):
        out, at = [], 0
        for w, n in zip(small_w, sizes):
            out.append(vec[0, at:at + n].reshape(w.shape))
            at += n
        return out, vec[0, at]

    (sg, loss), (sd, _), (sm, _), (sv, _) = [unpack(p) for p in packed]
    reduce_finish(r_in, (packed[0],))

    def ordered(small, bigs):
        bigs = [bigs[nm][None] for nm in names]
        return [bigs[0]] + small + bigs[1:]

    return (loss, dx.reshape(x.shape), *ordered(sg, g_big), *ordered(sd, {nm: upd[nm][0] for nm in names}),
            *ordered(sm, {nm: upd[nm][1] for nm in names}), *ordered(sv, {nm: upd[nm][2] for nm in names}))
```

```python
import functools

import jax
import jax.numpy as jnp
from jax import lax
from jax.experimental import pallas as pl
from jax.experimental.pallas import tpu as pltpu

F32 = jnp.float32
BF16 = jnp.bfloat16
SDS = jax.ShapeDtypeStruct
MESH = pl.DeviceIdType.MESH
HIGHEST = lax.Precision.HIGHEST

CHUNK = 64
SUB = 16
HEAD = 128
N_PAST = 8
BAND = (N_PAST + 1) * CHUNK
PAD = N_PAST * CHUNK
REL_FUTURE = CHUNK - 1
REL_PAST = 2 * CHUNK - 1
N_REL = REL_FUTURE + REL_PAST + 1
REL_LANES = 256
EPS = 1e-6
MIX_HEADS = 2
MIX_UNROLL = 4
MIX_UNROLL_BWD = 4
ATT_UNROLL = 8
ATT_UNROLL_BWD = 8
EXP_CLAMP = 80.0

ADAM_LR = 0.001
ADAM_B1 = 0.9
ADAM_B2 = 0.999
ADAM_EPS = 1e-08
ADAM_WD = 0.01
ADAM_STEP = 10

VMEM_LIMIT = 56 * 1024 * 1024

HBM = pl.BlockSpec(memory_space=pltpu.HBM)
ANY = pl.BlockSpec(memory_space=pl.ANY)
SEM = pl.BlockSpec(memory_space=pltpu.SEMAPHORE)

NT = (((1,), (1,)), ((), ()))
TN = (((0,), (0,)), ((), ()))
NN = (((1,), (0,)), ((), ()))


def _params(sem=None, **kw):
    return pltpu.CompilerParams(dimension_semantics=sem, vmem_limit_bytes=VMEM_LIMIT, **kw)


def _tile(n, pref, unit=128):
    if n <= pref:
        return n
    t = pref - pref % unit
    while n % t:
        t -= unit
    return t


STREAM_BLOCK = 256 * 1408


def _row_tile(rows, cols):
    return _tile(rows, max(16, STREAM_BLOCK // cols), 16)


def _sigmoid(x):
    return 1.0 / (1.0 + jnp.exp(-x))


def _dsilu(x, s):
    return s * (1.0 + x * (1.0 - s))


def _split(a):
    hi = a.astype(BF16)
    return hi, (a - hi.astype(F32)).astype(BF16)


def _dot3(a, b, dims):
    dot = lambda u, v: lax.dot_general(u, v, dims, preferred_element_type=F32)
    return dot(a[0], b[1]) + dot(a[1], b[0]) + dot(a[0], b[0])


def _fdot(a, b):
    return lax.dot_general(a, b, NN, precision=HIGHEST, preferred_element_type=F32)


MM_TILE_K = 5632
MM_TILE_N = 512


def _matmul_chunks(h, w, which, prev, pos, name, own_shard=False, deps=()):
    t, d = h.shape
    nc_ = w.shape[1] if own_shard else w.shape[1] // 4
    tm, tn = _tile(t, 1024), _tile(nc_, 1408)
    nn = nc_ // tn

    def chunk(q, p):
        sel = p[which[0]]
        for i in range(1, len(which)):
            sel = jnp.where(q == i, p[which[i]], sel)
        return sel

    def body(p_ref, h_ref, w_ref, *rest):
        rest[-1][...] = jnp.dot(h_ref[...], w_ref[...].astype(BF16), preferred_element_type=F32)

    if own_shard:
        w_spec = pl.BlockSpec((d, tn), lambda q, i, j, p: (0, j))
    else:
        w_spec = pl.BlockSpec((d, tn), lambda q, i, j, p: (0, chunk(q, p) * nn + j))
    n_extra = len(deps) + (prev is not None)
    return pl.pallas_call(
        body,
        grid_spec=pltpu.PrefetchScalarGridSpec(
            num_scalar_prefetch=1, grid=(len(which), t // tm, nn),
            in_specs=[pl.BlockSpec((tm, d), lambda q, i, j, p: (i, 0)), w_spec] + [ANY] * n_extra,
            out_specs=pl.BlockSpec((tm, tn), lambda q, i, j, p: (i, chunk(q, p) * nn + j))),
        out_shape=SDS((t, 4 * nc_), F32), input_output_aliases={3 + len(deps): 0} if prev is not None else {},
        compiler_params=_params(("arbitrary", "arbitrary", "arbitrary")),
        name=name)(pos, h, w, *deps, *(() if prev is None else (prev,)))


def _matmul(a, b, *, ta=False, tb=False, res=None, out_dtype=F32, name, deps=(), a_lead=None, b_lead=None,
            also_bf16=False):
    a2, b2 = a.shape[-2:], b.shape[-2:]
    m, k = (a2[1], a2[0]) if ta else a2
    n = b2[0] if tb else b2[1]
    if k > MM_TILE_K:
        tk, tm, tn = _tile(k, MM_TILE_K // 2), _tile(m, 1024), _tile(n, 1024)
    else:
        tk = k
        tm, tn = _tile(m, 2048 if tk <= MM_TILE_K // 2 else 1024), _tile(n, MM_TILE_N)
    nk = k // tk
    dims = ((((0,) if ta else (1,)), ((1,) if tb else (0,))), ((), ()))

    def body(*refs):
        n_in = 2 + (res is not None)
        a_ref, b_ref = refs[:2]
        r_ref = refs[2] if res is not None else None
        o_ref = refs[n_in + len(deps)]
        part = lax.dot_general(a_ref[...].astype(BF16), b_ref[...].astype(BF16), dims, preferred_element_type=F32)

        def finish(out):
            if r_ref is not None:
                out = out + r_ref[...]
            o_ref[...] = out.astype(o_ref.dtype)
            if also_bf16:
                refs[n_in + len(deps) + 1][...] = out.astype(BF16)

        if nk == 1:
            finish(part)
            return
        acc_ref = refs[-1]
        kk = pl.program_id(2)

        @pl.when(kk == 0)
        def _():
            acc_ref[...] = part

        @pl.when(jnp.logical_and(kk > 0, kk < nk - 1))
        def _():
            acc_ref[...] += part

        @pl.when(kk == nk - 1)
        def _():
            finish(acc_ref[...] + part)

    def spec(block, index, lead):
        if lead is None:
            return pl.BlockSpec(block, index)
        return pl.BlockSpec((None,) + block, lambda i, j, l: (lead,) + index(i, j, l))

    a_spec = spec((tk, tm), lambda i, j, l: (l, i), a_lead) if ta else spec((tm, tk), lambda i, j, l: (i, l), a_lead)
    b_spec = spec((tn, tk), lambda i, j, l: (j, l), b_lead) if tb else spec((tk, tn), lambda i, j, l: (l, j), b_lead)
    o_spec = pl.BlockSpec((tm, tn), lambda i, j, l: (i, j))
    in_specs = [a_spec, b_spec] + ([o_spec] if res is not None else []) + [ANY] * len(deps)
    args = (a, b) + ((res,) if res is not None else ()) + tuple(deps)
    out_specs, out_shape = o_spec, SDS((m, n), out_dtype)
    if also_bf16:
        out_specs, out_shape = [o_spec, o_spec], [out_shape, SDS((m, n), BF16)]
    return pl.pallas_call(
        body, grid=(m // tm, n // tn, nk), in_specs=in_specs, out_specs=out_specs,
        out_shape=out_shape, scratch_shapes=[pltpu.VMEM((tm, tn), F32)] if nk > 1 else [],
        compiler_params=_params(("parallel", "parallel", "arbitrary")), name=name)(*args)


def _cast_into_full(w, kind, pos, name):
    r, n = w.shape
    tr = _tile(r, 512, 16)
    nr = r // tr
    if kind == "col":
        shape, o_spec = (r, 4 * n), pl.BlockSpec((tr, n), lambda i, p: (i, p[0]))
    else:
        shape, o_spec = (4 * r, n), pl.BlockSpec((tr, n), lambda i, p: (p[0] * nr + i, 0))

    def body(p_ref, w_ref, o_ref):
        o_ref[...] = w_ref[...].astype(BF16)

    return pl.pallas_call(
        body,
        grid_spec=pltpu.PrefetchScalarGridSpec(num_scalar_prefetch=1, grid=(nr,),
                                               in_specs=[pl.BlockSpec((tr, n), lambda i, p: (i, 0))], out_specs=o_spec),
        out_shape=SDS(shape, BF16), compiler_params=_params(("parallel",)), name=name)(pos, w)


def _rmsnorm_fwd(x, gain, name):
    t, d = x.shape
    tm = _tile(t, 256)

    def body(x_ref, g_ref, h_ref, r_ref):
        xv = x_ref[...]
        r = lax.rsqrt(jnp.mean(xv * xv, axis=-1, keepdims=True) + EPS)
        h_ref[...] = (xv * r * g_ref[...]).astype(BF16)
        r_ref[...] = r

    return pl.pallas_call(
        body, grid=(t // tm,),
        in_specs=[pl.BlockSpec((tm, d), lambda i: (i, 0)), pl.BlockSpec((1, d), lambda i: (0, 0))],
        out_specs=[pl.BlockSpec((tm, d), lambda i: (i, 0)), pl.BlockSpec((tm, 1), lambda i: (i, 0))],
        out_shape=[SDS((t, d), BF16), SDS((t, 1), F32)], compiler_params=_params(("parallel",)), name=name)(x, gain)


def _rmsnorm_bwd(dh, x, r, gain, dres, name, deps=(), with_bf16=True):
    t, d = x.shape
    tm = _tile(t, 256)

    def body(dh_ref, x_ref, r_ref, g_ref, dres_ref, *rest):
        outs = rest[len(deps):]
        dx_ref, dg_ref = outs[0], outs[-1]

        @pl.when(pl.program_id(0) == 0)
        def _():
            dg_ref[...] = jnp.zeros_like(dg_ref)

        dhv, xv, rv = dh_ref[...], x_ref[...], r_ref[...]
        dg_ref[...] += jnp.sum(dhv * (xv * rv), axis=0, keepdims=True)
        u = dhv * g_ref[...]
        dx = dres_ref[...] + rv * u - xv * (rv * rv * rv) * jnp.mean(u * xv, axis=-1, keepdims=True)
        dx_ref[...] = dx
        if with_bf16:
            outs[1][...] = dx.astype(BF16)

    row = pl.BlockSpec((tm, d), lambda i: (i, 0))
    vec = pl.BlockSpec((1, d), lambda i: (0, 0))
    copies = [SDS((t, d), F32)] + ([SDS((t, d), BF16)] if with_bf16 else [])
    return pl.pallas_call(
        body, grid=(t // tm,),
        in_specs=[row, row, pl.BlockSpec((tm, 1), lambda i: (i, 0)), vec, row] + [ANY] * len(deps),
        out_specs=[row] * len(copies) + [vec], out_shape=copies + [SDS((1, d), F32)],
        compiler_params=_params(("arbitrary",)), name=name)(dh, x, r, gain, dres, *deps)


def _proj_merge(y_a, y_b, w_a, w_b, proj, b_gate, off, deps=()):
    t, ka = y_a.shape
    kb = y_b.shape[1]
    d = w_a.shape[1]
    tm, tc = _tile(t, 1024), _tile(d, MM_TILE_N)
    nj = d // tc
    oa, ob = off // tc, off // tc + nj

    def body(ya_ref, yb_ref, wa_ref, wb_ref, la_ref, lb_ref, ba_ref, bb_ref, *rest):
        pa_ref, pb_ref, o_ref = rest[len(deps):]
        pa = jnp.dot(ya_ref[...], wa_ref[...], preferred_element_type=F32)
        pb = jnp.dot(yb_ref[...], wb_ref[...], preferred_element_type=F32)
        pa_ref[...] = pa
        pb_ref[...] = pb
        ga = _sigmoid(la_ref[...] + ba_ref[...])
        gb = _sigmoid(lb_ref[...] + bb_ref[...])
        o_ref[...] = (ga * pa + gb * pb).astype(BF16)

    tile = pl.BlockSpec((tm, tc), lambda i, j: (i, j))
    return pl.pallas_call(
        body, grid=(t // tm, nj),
        in_specs=[pl.BlockSpec((tm, ka), lambda i, j: (i, 0)), pl.BlockSpec((tm, kb), lambda i, j: (i, 0)),
                  pl.BlockSpec((ka, tc), lambda i, j: (0, j)), pl.BlockSpec((kb, tc), lambda i, j: (0, j)),
                  pl.BlockSpec((tm, tc), lambda i, j: (i, oa + j)), pl.BlockSpec((tm, tc), lambda i, j: (i, ob + j)),
                  pl.BlockSpec((1, tc), lambda i, j: (0, j)), pl.BlockSpec((1, tc), lambda i, j: (0, nj + j))]
        + [ANY] * len(deps),
        out_specs=[tile, tile, tile], out_shape=[SDS((t, d), F32), SDS((t, d), F32), SDS((t, d), BF16)],
        compiler_params=_params(("parallel", "parallel")),
        name="proj_merge")(y_a, y_b, w_a, w_b, proj, proj, b_gate, b_gate, *deps)


def _d_merged_gates(dx, w, proj, b_gate, pa, pb, off, deps=()):
    t, k = dx.shape
    d = w.shape[0]
    tm, tc = _tile(t, 1024), _tile(d, MM_TILE_N)
    nj, ni = d // tc, t // tm
    o0 = off // tc

    def body(dx_ref, w_ref, la_ref, lb_ref, ba_ref, bb_ref, pa_ref, pb_ref, *rest):
        dp_ref, dproj_ref, db_ref, stage, sems = rest[len(deps):]
        j, i = pl.program_id(0), pl.program_id(1)
        step = j * ni + i
        slot = step % 2

        def copies(s, ii, jj):
            rows = pl.ds(pl.multiple_of(ii * tm, tm), tm)
            return [pltpu.make_async_copy(
                stage.at[s, br], dproj_ref.at[rows, pl.ds(pl.multiple_of(off + br * d + jj * tc, 128), tc)],
                sems.at[s, br]) for br in range(2)]

        @pl.when(step >= 2)
        def _():
            for cp in copies(slot, 0, 0):
                cp.wait()

        dm = lax.dot_general(dx_ref[...], w_ref[...], NT, preferred_element_type=F32)

        @pl.when(i == 0)
        def _():
            db_ref[...] = jnp.zeros_like(db_ref)

        for br, (l_ref, b_ref, p_ref) in enumerate(((la_ref, ba_ref, pa_ref), (lb_ref, bb_ref, pb_ref))):
            g = _sigmoid(l_ref[...] + b_ref[...])
            dp_ref[br] = (dm * g).astype(BF16)
            dl = dm * p_ref[...] * g * (1.0 - g)
            stage[slot, br] = dl.astype(BF16)
            db_ref[br] += jnp.sum(dl, axis=0, keepdims=True)
        for cp in copies(slot, i, j):
            cp.start()

        @pl.when(step == ni * nj - 1)
        def _():
            for s in range(min(2, ni * nj)):
                for cp in copies(s, 0, 0):
                    cp.wait()

    tile = pl.BlockSpec((tm, tc), lambda j, i: (i, j))
    return pl.pallas_call(
        body, grid=(nj, ni),
        in_specs=[pl.BlockSpec((tm, k), lambda j, i: (i, 0)), pl.BlockSpec((tc, k), lambda j, i: (j, 0)),
                  pl.BlockSpec((tm, tc), lambda j, i: (i, o0 + j)), pl.BlockSpec((tm, tc), lambda j, i: (i, o0 + nj + j)),
                  pl.BlockSpec((1, tc), lambda j, i: (0, j)), pl.BlockSpec((1, tc), lambda j, i: (0, nj + j)),
                  tile, tile] + [ANY] * len(deps),
        out_specs=[pl.BlockSpec((2, tm, tc), lambda j, i: (0, i, j)), ANY, pl.BlockSpec((2, 1, tc), lambda j, i: (0, 0, j))],
        out_shape=[SDS((2, t, d), BF16), SDS(proj.shape, BF16), SDS((2, 1, d), F32)],
        scratch_shapes=[pltpu.VMEM((2, 2, tm, tc), BF16), pltpu.SemaphoreType.DMA((2, 2))],
        compiler_params=_params(("arbitrary", "arbitrary")),
        name="d_merged_gates")(dx, w, proj, proj, b_gate, b_gate, pa, pb, *deps)


def _ffn_in_swiglu(h, w, deps=()):
    t, d = h.shape
    f = w.shape[1] // 2
    tm, tn = _tile(t, 2048), _tile(f, MM_TILE_N)
    nj = f // tn

    def body(h_ref, wg_ref, wu_ref, *rest):
        g_ref, u_ref, a_ref = rest[len(deps):]
        hv = h_ref[...]
        g = jnp.dot(hv, wg_ref[...], preferred_element_type=F32)
        u = jnp.dot(hv, wu_ref[...], preferred_element_type=F32)
        g_ref[...] = g
        u_ref[...] = u
        a_ref[...] = (g * _sigmoid(g) * u).astype(BF16)

    tile = pl.BlockSpec((tm, tn), lambda i, j: (i, j))
    return pl.pallas_call(
        body, grid=(t // tm, nj),
        in_specs=[pl.BlockSpec((tm, d), lambda i, j: (i, 0)), pl.BlockSpec((d, tn), lambda i, j: (0, j)),
                  pl.BlockSpec((d, tn), lambda i, j: (0, nj + j))] + [ANY] * len(deps),
        out_specs=[tile, tile, tile], out_shape=[SDS((t, f), F32), SDS((t, f), F32), SDS((t, f), BF16)],
        compiler_params=_params(("parallel", "parallel")), name="ffn_in_swiglu")(h, w, w, *deps)


def _d_act_swiglu(dy, w, gate, up, deps=()):
    t, k = dy.shape
    f = w.shape[0]
    tm, tn = _tile(t, 1024), _tile(f, MM_TILE_N)
    ni, nj = t // tm, f // tn

    def body(dy_ref, w_ref, g_ref, u_ref, *rest):
        out_ref, stage, sems = rest[len(deps):]
        i, j = pl.program_id(0), pl.program_id(1)
        step = i * nj + j
        slot = step % 2

        def copies(s, ii, jj):
            rows = pl.ds(pl.multiple_of(ii * tm, tm), tm)
            return [pltpu.make_async_copy(
                stage.at[s, half], out_ref.at[rows, pl.ds(pl.multiple_of(half * f + jj * tn, 128), tn)],
                sems.at[s, half]) for half in range(2)]

        @pl.when(step >= 2)
        def _():
            for cp in copies(slot, 0, 0):
                cp.wait()

        dact = lax.dot_general(dy_ref[...], w_ref[...], NT, preferred_element_type=F32)
        g = g_ref[...]
        sg = _sigmoid(g)
        stage[slot, 0] = (dact * u_ref[...] * _dsilu(g, sg)).astype(BF16)
        stage[slot, 1] = (dact * (g * sg)).astype(BF16)
        for cp in copies(slot, i, j):
            cp.start()

        @pl.when(step == ni * nj - 1)
        def _():
            for s in range(min(2, ni * nj)):
                for cp in copies(s, 0, 0):
                    cp.wait()

    tile = pl.BlockSpec((tm, tn), lambda i, j: (i, j))
    return pl.pallas_call(
        body, grid=(ni, nj),
        in_specs=[pl.BlockSpec((tm, k), lambda i, j: (i, 0)), pl.BlockSpec((tn, k), lambda i, j: (j, 0)), tile, tile]
        + [ANY] * len(deps),
        out_specs=ANY, out_shape=SDS((t, 2 * f), BF16),
        scratch_shapes=[pltpu.VMEM((2, 2, tm, tn), BF16), pltpu.SemaphoreType.DMA((2, 2))],
        compiler_params=_params(("arbitrary", "arbitrary")), name="d_act_swiglu")(dy, w, gate, up, *deps)


def _ffn_out_loss(act, w, x_res, target):
    t, d = x_res.shape
    k = act.shape[1]
    tm, tn = _tile(t, 1024), _tile(d, MM_TILE_N)

    def body(a_ref, w_ref, r_ref, t_ref, dy_ref, dyb_ref, l_ref):
        @pl.when(jnp.logical_and(pl.program_id(0) == 0, pl.program_id(1) == 0))
        def _():
            l_ref[...] = jnp.zeros_like(l_ref)

        y = jnp.dot(a_ref[...], w_ref[...], preferred_element_type=F32) + r_ref[...]
        e = y - t_ref[...]
        dy = e * (1.0 / d)
        dy_ref[...] = dy
        dyb_ref[...] = dy.astype(BF16)
        l_ref[...] += (0.5 / d) * jnp.sum(jnp.sum(e * e, axis=-1, keepdims=True), axis=0, keepdims=True)

    tile = pl.BlockSpec((tm, tn), lambda i, j: (i, j))
    return pl.pallas_call(
        body, grid=(t // tm, d // tn),
        in_specs=[pl.BlockSpec((tm, k), lambda i, j: (i, 0)), pl.BlockSpec((k, tn), lambda i, j: (0, j)), tile, tile],
        out_specs=[tile, tile, pl.BlockSpec((1, 1), lambda i, j: (0, 0))],
        out_shape=[SDS((t, d), F32), SDS((t, d), BF16), SDS((1, 1), F32)],
        compiler_params=_params(("arbitrary", "arbitrary")), name="ffn_out_loss")(act, w, x_res, target)


def _rel_onehot(qi):
    p = lax.broadcasted_iota(jnp.int32, (REL_LANES, BAND), 1)
    r = lax.broadcasted_iota(jnp.int32, (REL_LANES, BAND), 0)
    idx = jnp.clip(qi + PAD - p, -REL_FUTURE, REL_PAST) + REL_FUTURE
    return (idx == r).astype(F32)


def _relbias_expand(rb):
    h = rb.shape[0]

    def body(rb_ref, o_ref):
        def step(qi, _):
            o_ref[qi] = _fdot(rb_ref[...], _rel_onehot(qi))
            return 0

        lax.fori_loop(0, CHUNK, step, 0)

    return pl.pallas_call(body, out_shape=SDS((CHUNK, h, BAND), F32), compiler_params=_params(),
                          name="relbias_expand")(rb)


def _relbias_reduce(dbias):
    h = dbias.shape[1]

    rows_per_pass = 4

    def body(db_ref, o_ref):
        def step(i, acc):
            parts = []
            for u in range(rows_per_pass):
                qi = i * rows_per_pass + u
                xv = db_ref[qi]
                hi = xv.astype(BF16)
                rest = xv - hi.astype(F32)
                mid = rest.astype(BF16)
                low = (rest - mid.astype(F32)).astype(BF16)
                parts.append(lax.dot_general(jnp.concatenate([hi, mid, low], axis=0), _rel_onehot(qi).astype(BF16), NT,
                                             preferred_element_type=F32))
            for part in parts:
                acc = acc + (part[0:h] + part[h:2 * h] + part[2 * h:3 * h])
            return acc

        o_ref[...] = lax.fori_loop(0, CHUNK // rows_per_pass, step, jnp.zeros((h, REL_LANES), F32))

    return pl.pallas_call(body, out_shape=SDS((h, REL_LANES), F32), compiler_params=_params(),
                          name="relbias_reduce")(dbias)


def _lower_bound(l_ref):
    l0, l1 = l_ref[0:1, :], l_ref[1:2, :]
    m = jnp.maximum(l0, l1)
    e0, e1 = jnp.exp(l0 - m), jnp.exp(l1 - m)
    return e0 / (e0 + e1)


def _tri(lower):
    r = lax.broadcasted_iota(jnp.int32, (CHUNK, CHUNK), 0)
    c = lax.broadcasted_iota(jnp.int32, (CHUNK, CHUNK), 1)
    return r >= c if lower else r <= c


def _hgrn_intra(qs, kk, b_s):
    rows = lax.broadcasted_iota(jnp.int32, (CHUNK, HEAD), 0)
    b = b_s[...]
    out = []
    for i in range(CHUNK // SUB):
        lo = i * SUB
        ref = jnp.zeros((1, HEAD), F32) if i == 0 else b_s[lo - 1:lo, :]
        eq = jnp.exp(b[lo:lo + SUB] - ref)
        qt = _split(qs[lo:lo + SUB] * eq)
        e = jnp.where(rows < lo + SUB, jnp.exp(jnp.minimum(ref - b, EXP_CLAMP)), 0.0)
        kt = _split(kk * e)
        out.append((eq, qt, e, kt))
    return out


def _hgrn_scores(blocks):
    tr = lax.broadcasted_iota(jnp.int32, (SUB, CHUNK), 0)
    tc = lax.broadcasted_iota(jnp.int32, (SUB, CHUNK), 1)
    return jnp.concatenate([jnp.where(tc <= tr + i * SUB, _dot3(qt, kt, NT), 0.0)
                            for i, (_, qt, _, kt) in enumerate(blocks)], axis=0)


def _hgrn_fwd(proj, lb_logits, gain, n_heads):
    t = proj.shape[0]
    nc = t // CHUNK
    da = n_heads * HEAD
    hp = MIX_HEADS
    wide = hp * HEAD

    def body(q_ref, f_ref, i_ref, g_ref, l_ref, gain_ref, y_ref, o_ref, st_ref, state, b_s):
        state[...] = jnp.zeros_like(state)
        lb_all = _lower_bound(l_ref)
        tril = _tri(True).astype(F32)

        def chunks(i, _):
            dot = functools.partial(lax.dot_general, preferred_element_type=F32)
            items = []
            for u in range(MIX_UNROLL):
                for hh in range(hp):
                    j = i * MIX_UNROLL + u
                    sl = pl.ds(pl.multiple_of(j * CHUNK, CHUNK), CHUNK)
                    cols = slice(hh * HEAD, (hh + 1) * HEAD)
                    lb = lb_all[:, cols]
                    fg = lb + (1.0 - lb) * _sigmoid(f_ref[sl, cols])
                    qv = q_ref[sl, cols]
                    gv = g_ref[sl, cols]
                    items.append(dict(hh=hh, j=j, sl=sl, cols=cols, lf=jnp.log(fg), kk=1.0 - fg, qs=qv * _sigmoid(qv),
                                      vb=i_ref[sl, cols].astype(BF16), gate=gv * _sigmoid(gv)))
            for it in items:
                it["b"] = _fdot(tril, it["lf"])
            for slot, it in enumerate(items):
                b = it["b"]
                b_s[slot] = b
                it["blocks"] = _hgrn_intra(it["qs"], it["kk"], b_s.at[slot])
                it["ebl"] = jnp.exp(b_s[slot, CHUNK - 1:CHUNK, :])
                it["qe"] = (it["qs"] * jnp.exp(b)).astype(BF16)
                it["kd"] = (it["kk"] * jnp.exp(b_s[slot, CHUNK - 1:CHUNK, :] - b)).astype(BF16)
            for it in items:
                it["a"] = _hgrn_scores(it["blocks"]).astype(BF16)
            for it in items:
                it["kv"] = dot(it["vb"], it["kd"], TN)
                it["o"] = dot(it["a"], it["vb"], NN)
            s_now = [state[hh] for hh in range(hp)]
            for it in items:
                it["s_in"] = s_now[it["hh"]]
                s_now[it["hh"]] = it["s_in"] * it["ebl"] + it["kv"]
            for hh in range(hp):
                state[hh] = s_now[hh]
            for it in items:
                it["o"] = it["o"] + dot(it["qe"], it["s_in"].astype(BF16), NT)
            for it in items:
                o, sl, cols = it["o"], it["sl"], it["cols"]
                st_ref[it["hh"], it["j"]] = it["s_in"]
                o_ref[sl, cols] = o
                rr = lax.rsqrt(jnp.mean(o * o, axis=-1, keepdims=True) + EPS)
                y_ref[sl, cols] = (o * rr * gain_ref[:, cols] * it["gate"]).astype(BF16)
            return 0

        assert nc % MIX_UNROLL == 0, (nc, MIX_UNROLL)
        lax.fori_loop(0, nc // MIX_UNROLL, chunks, 0)

    col = lambda k: pl.BlockSpec((t, wide), lambda h: (0, k * (n_heads // hp) + h))
    vec = pl.BlockSpec((1, wide), lambda h: (0, h))
    return pl.pallas_call(
        body, grid=(n_heads // hp,),
        in_specs=[col(0), col(1), col(2), col(3), pl.BlockSpec((2, wide), lambda h: (0, h)), vec],
        out_specs=[pl.BlockSpec((t, wide), lambda h: (0, h)), pl.BlockSpec((t, wide), lambda h: (0, h)),
                   pl.BlockSpec((hp, nc, HEAD, HEAD), lambda h: (h, 0, 0, 0))],
        out_shape=[SDS((t, da), BF16), SDS((t, da), F32), SDS((n_heads, nc, HEAD, HEAD), F32)],
        scratch_shapes=[pltpu.VMEM((hp, HEAD, HEAD), F32), pltpu.VMEM((hp * MIX_UNROLL, CHUNK, HEAD), F32)],
        compiler_params=_params(("parallel",)), name="hgrn_fwd")(proj, proj, proj, proj, lb_logits, gain)


def _write_column_groups(res, dproj_ref, sems, col0, stride, h, width):
    copies = [pltpu.make_async_copy(
        res.at[p], dproj_ref.at[:, pl.ds(pl.multiple_of((col0 + p * stride + h) * width, HEAD), width)], sems.at[p])
        for p in range(res.shape[0])]
    for cp in copies:
        cp.start()
    for cp in copies:
        cp.wait()


def _hgrn_bwd(dproj, proj, o_pre, states, dy, lb_logits, gain, n_heads, deps=()):
    t = proj.shape[0]
    nc = t // CHUNK
    da = n_heads * HEAD
    hp = MIX_HEADS
    wide = hp * HEAD

    def body(*refs):
        (q_ref, f_ref, i_ref, g_ref, o_ref, st_ref, dy_ref, l_ref, gain_ref,
         dproj_ref, dl_ref, dgain_ref, res, dstate, b_s, out_sems) = refs[1 + len(deps):]

        def compute():
            dstate[...] = jnp.zeros_like(dstate)
            lb_all = _lower_bound(l_ref)
            tril_m, tril, triu = _tri(True), _tri(True).astype(F32), _tri(False).astype(F32)
            last = lax.broadcasted_iota(jnp.int32, (CHUNK, HEAD), 0) == CHUNK - 1

            def chunks(i, carry):
                dot = functools.partial(lax.dot_general, preferred_element_type=F32)
                items = []
                for u in range(MIX_UNROLL_BWD):
                    for hh in range(hp):
                        j = nc - 1 - (i * MIX_UNROLL_BWD + u)
                        sl = pl.ds(pl.multiple_of(j * CHUNK, CHUNK), CHUNK)
                        cols = slice(hh * HEAD, (hh + 1) * HEAD)
                        lb, gain_v = lb_all[:, cols], gain_ref[:, cols]
                        sg = _sigmoid(f_ref[sl, cols])
                        fg = lb + (1.0 - lb) * sg
                        qv = q_ref[sl, cols]
                        sq = _sigmoid(qv)
                        gv = g_ref[sl, cols]
                        sgg = _sigmoid(gv)
                        silg = gv * sgg
                        o = o_ref[sl, cols]
                        dyv = dy_ref[sl, cols]
                        rr = lax.rsqrt(jnp.mean(o * o, axis=-1, keepdims=True) + EPS)
                        on = o * rr
                        don = dyv * gain_v * silg
                        do = (rr * don - o * (rr * rr * rr) * jnp.mean(don * o, axis=-1, keepdims=True)).astype(BF16)
                        items.append(dict(
                            hh=hh, j=j, sl=sl, cols=cols, lb=lb, sg=sg, fg=fg, kk=1.0 - fg, qv=qv, sq=sq, qs=qv * sq,
                            vb=i_ref[sl, cols].astype(BF16), do=do, dg=dyv * on * gain_v * _dsilu(gv, sgg),
                            dgain=jnp.sum(dyv * on * silg, axis=0, keepdims=True)))
                for it in items:
                    it["b"] = _fdot(tril, jnp.log(it["fg"]))
                for slot, it in enumerate(items):
                    b = it["b"]
                    b_s[slot] = b
                    it["blocks"] = _hgrn_intra(it["qs"], it["kk"], b_s.at[slot])
                    bl = b_s[slot, CHUNK - 1:CHUNK, :]
                    it["eb"], it["ebl"], it["ekd"] = jnp.exp(b), jnp.exp(bl), jnp.exp(bl - b)
                    it["s_in"] = st_ref[it["hh"], it["j"]]
                for it in items:
                    it["a"] = _hgrn_scores(it["blocks"]).astype(BF16)
                    it["da"] = jnp.where(tril_m, dot(it["do"], it["vb"], NT), 0.0)
                for it in items:
                    dq_rows = []
                    dk = jnp.zeros((CHUNK, HEAD), F32)
                    for blk, (eq, qt, e, kt) in enumerate(it["blocks"]):
                        da_i = _split(it["da"][blk * SUB:(blk + 1) * SUB])
                        dq_rows.append(eq * _dot3(da_i, kt, NN))
                        dk = dk + e * _dot3(da_i, qt, TN)
                    it["dq"] = jnp.concatenate(dq_rows, axis=0) + dot(it["do"], it["s_in"].astype(BF16), NN) * it["eb"]
                    it["dk"] = dk
                    it["dv"] = dot(it["a"], it["do"], TN)
                    it["g"] = dot(it["do"], (it["qs"] * it["eb"]).astype(BF16), TN)
                ds_now = [dstate[hh] for hh in range(hp)]
                for it in items:
                    it["ds_out"] = ds_now[it["hh"]]
                    ds_now[it["hh"]] = it["ds_out"] * it["ebl"] + it["g"]
                for hh in range(hp):
                    dstate[hh] = ds_now[hh]
                for it in items:
                    dsb = it["ds_out"].astype(BF16)
                    it["dv"] = it["dv"] + dot((it["kk"] * it["ekd"]).astype(BF16), dsb, NT)
                    it["dk_state"] = it["ekd"] * dot(it["vb"], dsb, NN)
                for it in items:
                    kk, dk_state = it["kk"], it["dk_state"]
                    it["dk"] = it["dk"] + dk_state
                    extra = (jnp.sum(kk * dk_state, axis=0, keepdims=True)
                             + it["ebl"] * jnp.sum(it["s_in"] * it["ds_out"], axis=0, keepdims=True))
                    it["db"] = it["qs"] * it["dq"] - kk * it["dk"] + jnp.where(last, extra, 0.0)
                for it in items:
                    it["dlf"] = _fdot(triu, it["db"])
                carry = list(carry)
                for it in items:
                    hh, sl, cols, sg, lb = it["hh"], it["sl"], it["cols"], it["sg"], it["lb"]
                    dfg = it["dlf"] / it["fg"] - it["dk"]
                    dlb_acc, dgain_acc = carry[hh]
                    carry[hh] = (dlb_acc + jnp.sum(dfg * (1.0 - sg), axis=0, keepdims=True), dgain_acc + it["dgain"])
                    res[0, sl, cols] = (it["dq"] * _dsilu(it["qv"], it["sq"])).astype(BF16)
                    res[1, sl, cols] = (dfg * (1.0 - lb) * sg * (1.0 - sg)).astype(BF16)
                    res[2, sl, cols] = it["dv"].astype(BF16)
                    res[3, sl, cols] = it["dg"].astype(BF16)
                return tuple(carry)

            assert nc % MIX_UNROLL_BWD == 0, (nc, MIX_UNROLL_BWD)
            zero = jnp.zeros((1, HEAD), F32)
            sums = lax.fori_loop(0, nc // MIX_UNROLL_BWD, chunks, ((zero, zero),) * hp)
            for hh, (dlb, dgain) in enumerate(sums):
                cols = slice(hh * HEAD, (hh + 1) * HEAD)
                lb = lb_all[:, cols]
                dgain_ref[:, cols] = dgain
                dl0 = dlb * lb * (1.0 - lb)
                dl_ref[0:1, cols] = dl0
                dl_ref[1:2, cols] = -dl0

        compute()
        _write_column_groups(res, dproj_ref, out_sems, 0, ng, pl.program_id(0), wide)

    ng = n_heads // hp
    col = lambda k: pl.BlockSpec((t, wide), lambda h: (0, k * ng + h))
    head = pl.BlockSpec((t, wide), lambda h: (0, h))
    vec = pl.BlockSpec((1, wide), lambda h: (0, h))
    return pl.pallas_call(
        body, grid=(ng,),
        in_specs=[ANY] * (1 + len(deps)) + [col(0), col(1), col(2), col(3), head,
                  pl.BlockSpec((hp, nc, HEAD, HEAD), lambda h: (h, 0, 0, 0)),
                  head, pl.BlockSpec((2, wide), lambda h: (0, h)), vec],
        out_specs=[ANY, pl.BlockSpec((2, wide), lambda h: (0, h)), vec],
        out_shape=[SDS(dproj.shape, BF16), SDS((2, da), F32), SDS((1, da), F32)],
        scratch_shapes=[pltpu.VMEM((4, t, wide), BF16), pltpu.VMEM((hp, HEAD, HEAD), F32),
                        pltpu.VMEM((hp * MIX_UNROLL_BWD, CHUNK, HEAD), F32), pltpu.SemaphoreType.DMA((4,))],
        input_output_aliases={0: 0}, compiler_params=_params(("arbitrary",)),
        name="hgrn_bwd")(dproj, *deps, proj, proj, proj, proj, o_pre, states, dy, lb_logits, gain)


ROWS = 256


def _head_norm(x_ref, gain, dst, dst_off, t):
    def step(i, _):
        sl = pl.ds(pl.multiple_of(i * ROWS, ROWS), ROWS)
        xv = x_ref[sl, :]
        r = lax.rsqrt(jnp.mean(xv * xv, axis=-1, keepdims=True) + EPS)
        dst[pl.ds(pl.multiple_of(dst_off + i * ROWS, ROWS), ROWS), :] = (xv * r * gain).astype(BF16)
        return 0

    lax.fori_loop(0, t // ROWS, step, 0)


def _head_norm_bwd(x_ref, gain, dn_ref, dn_off, out, slot, t):
    def step(i, acc):
        sl = pl.ds(pl.multiple_of(i * ROWS, ROWS), ROWS)
        xv = x_ref[sl, :]
        dn = dn_ref[pl.ds(pl.multiple_of(dn_off + i * ROWS, ROWS), ROWS), :]
        r = lax.rsqrt(jnp.mean(xv * xv, axis=-1, keepdims=True) + EPS)
        u = dn * gain
        out[slot, sl, :] = (r * u - xv * (r * r * r) * jnp.mean(u * xv, axis=-1, keepdims=True)).astype(out.dtype)
        return acc + jnp.sum(dn * (xv * r), axis=0, keepdims=True)

    return lax.fori_loop(0, t // ROWS, step, jnp.zeros((1, HEAD), F32))


def _attn_scores(qn, kpad, n):
    qc = qn[pl.ds(pl.multiple_of(n * CHUNK, CHUNK), CHUNK), :]
    band = pl.ds(pl.multiple_of(n * CHUNK, CHUNK), BAND)
    return qc, band, lax.dot_general(qc, kpad[band, :], NT, preferred_element_type=F32)


def _attn_softmax(raw, bias_ref, n):
    s = raw * (HEAD ** -0.5) + bias_ref[0]
    col = lax.broadcasted_iota(jnp.int32, (CHUNK, BAND), 1)
    s = jnp.where(col >= PAD - n * CHUNK, s, -jnp.inf)
    p = jnp.exp(s - jnp.max(s, axis=-1, keepdims=True))
    return p / jnp.sum(p, axis=-1, keepdims=True)


def _attn_fwd(proj, q_gain, k_gain, bias, n_heads, col0):
    t = proj.shape[0]
    nc = t // CHUNK

    def body(q_ref, k_ref, v_ref, qg_ref, kg_ref, bias_ref, y_ref, qn, kpad, vpad):
        kpad[0:PAD, :] = jnp.zeros((PAD, HEAD), BF16)
        vpad[0:PAD, :] = jnp.zeros((PAD, HEAD), BF16)
        _head_norm(q_ref, qg_ref[...], qn, 0, t)
        _head_norm(k_ref, kg_ref[...], kpad, PAD, t)

        def copy_v(i, _):
            vpad[pl.ds(pl.multiple_of(PAD + i * ROWS, ROWS), ROWS), :] = v_ref[
                pl.ds(pl.multiple_of(i * ROWS, ROWS), ROWS), :].astype(BF16)
            return 0

        lax.fori_loop(0, t // ROWS, copy_v, 0)

        def chunks(i, _):
            ns = [i * ATT_UNROLL + u for u in range(ATT_UNROLL)]
            scored = [_attn_scores(qn, kpad, n) for n in ns]
            probs = [_attn_softmax(raw, bias_ref, n).astype(BF16) for n, (_, _, raw) in zip(ns, scored)]
            outs = [lax.dot_general(p, vpad[band, :], NN, preferred_element_type=F32).astype(BF16)
                    for p, (_, band, _) in zip(probs, scored)]
            for n, o in zip(ns, outs):
                y_ref[pl.ds(pl.multiple_of(n * CHUNK, CHUNK), CHUNK), :] = o
            return 0

        assert nc % ATT_UNROLL == 0, (nc, ATT_UNROLL)
        lax.fori_loop(0, nc // ATT_UNROLL, chunks, 0)

    col = lambda k: pl.BlockSpec((t, HEAD), lambda h: (0, col0 + k * n_heads + h))
    vec = pl.BlockSpec((1, HEAD), lambda h: (0, 0))
    return pl.pallas_call(
        body, grid=(n_heads,),
        in_specs=[col(0), col(1), col(2), vec, vec, pl.BlockSpec((1, CHUNK, BAND), lambda h: (h, 0, 0))],
        out_specs=pl.BlockSpec((t, HEAD), lambda h: (0, h)), out_shape=SDS((t, n_heads * HEAD), BF16),
        scratch_shapes=[pltpu.VMEM((t, HEAD), BF16), pltpu.VMEM((t + PAD, HEAD), BF16), pltpu.VMEM((t + PAD, HEAD), BF16)],
        compiler_params=_params(("parallel",)), name="attn_fwd")(proj, proj, proj, q_gain, k_gain, bias)


def _attn_bwd(dproj, proj, q_gain, k_gain, bias, dy, n_heads, col0, deps=()):
    t = proj.shape[0]
    nc = t // CHUNK

    def body(*refs):
        (q_ref, k_ref, v_ref, qg_ref, kg_ref, bias_ref, dy_ref,
         dproj_ref, dbias_ref, dqg_ref, dkg_ref, qn, kpad, vpad, dqn, dk_acc, dv_acc, res,
         out_sems) = refs[1 + len(deps):]
        h = pl.program_id(0)

        def compute():
            kpad[0:PAD, :] = jnp.zeros((PAD, HEAD), BF16)
            vpad[0:PAD, :] = jnp.zeros((PAD, HEAD), BF16)
            _head_norm(q_ref, qg_ref[...], qn, 0, t)
            _head_norm(k_ref, kg_ref[...], kpad, PAD, t)

            def prep(i, _):
                sl = pl.ds(pl.multiple_of(PAD + i * ROWS, ROWS), ROWS)
                vpad[sl, :] = v_ref[pl.ds(pl.multiple_of(i * ROWS, ROWS), ROWS), :].astype(BF16)
                return 0

            lax.fori_loop(0, t // ROWS, prep, 0)

            def clear(i, _):
                sl = pl.ds(pl.multiple_of(i * ROWS, ROWS), ROWS)
                dk_acc[sl, :] = jnp.zeros((ROWS, HEAD), F32)
                dv_acc[sl, :] = jnp.zeros((ROWS, HEAD), F32)
                return 0

            lax.fori_loop(0, (t + PAD) // ROWS, clear, 0)
            dbias_ref[0] = jnp.zeros((CHUNK, BAND), F32)

            def chunks(i, _):
                dot = functools.partial(lax.dot_general, preferred_element_type=F32)
                ns = [i * ATT_UNROLL_BWD + u for u in range(ATT_UNROLL_BWD)]
                scored = [_attn_scores(qn, kpad, n) for n in ns]
                dos = [dy_ref[pl.ds(pl.multiple_of(n * CHUNK, CHUNK), CHUNK), :].astype(BF16) for n in ns]
                dps = [dot(do, vpad[band, :], NT) for do, (_, band, _) in zip(dos, scored)]
                ps, dss = [], []
                for n, (_, _, raw), dp in zip(ns, scored, dps):
                    p = _attn_softmax(raw, bias_ref, n)
                    ds = p * (dp - jnp.sum(dp * p, axis=-1, keepdims=True))
                    dbias_ref[0] += ds
                    ps.append(p.astype(BF16))
                    dss.append((ds * (HEAD ** -0.5)).astype(BF16))
                dqs = [dot(d, kpad[band, :], NN) for d, (_, band, _) in zip(dss, scored)]
                dks = [dot(d, qc, TN) for d, (qc, _, _) in zip(dss, scored)]
                dvs = [dot(p, do, TN) for p, do in zip(ps, dos)]
                for n, (_, band, _), dq, dk, dv in zip(ns, scored, dqs, dks, dvs):
                    dqn[pl.ds(pl.multiple_of(n * CHUNK, CHUNK), CHUNK), :] = dq
                    dk_acc[band, :] += dk
                    dv_acc[band, :] += dv
                return 0

            assert nc % ATT_UNROLL_BWD == 0, (nc, ATT_UNROLL_BWD)
            lax.fori_loop(0, nc // ATT_UNROLL_BWD, chunks, 0)
            dqg = _head_norm_bwd(q_ref, qg_ref[...], dqn, 0, res, 0, t)
            dkg = _head_norm_bwd(k_ref, kg_ref[...], dk_acc, PAD, res, 1, t)

            def put_v(i, _):
                sl = pl.ds(pl.multiple_of(i * ROWS, ROWS), ROWS)
                res[2, sl, :] = dv_acc[pl.ds(pl.multiple_of(PAD + i * ROWS, ROWS), ROWS), :].astype(BF16)
                return 0

            lax.fori_loop(0, t // ROWS, put_v, 0)

            @pl.when(h == 0)
            def _():
                dqg_ref[...] = jnp.zeros_like(dqg_ref)
                dkg_ref[...] = jnp.zeros_like(dkg_ref)

            dqg_ref[...] += dqg
            dkg_ref[...] += dkg

        compute()
        _write_column_groups(res, dproj_ref, out_sems, col0, n_heads, h, HEAD)

    col = lambda k: pl.BlockSpec((t, HEAD), lambda h: (0, col0 + k * n_heads + h))
    vec = pl.BlockSpec((1, HEAD), lambda h: (0, 0))
    btile = pl.BlockSpec((1, CHUNK, BAND), lambda h: (h, 0, 0))
    return pl.pallas_call(
        body, grid=(n_heads,),
        in_specs=[ANY] * (1 + len(deps)) + [col(0), col(1), col(2), vec, vec, btile,
                                            pl.BlockSpec((t, HEAD), lambda h: (0, h))],
        out_specs=[ANY, btile, vec, vec],
        out_shape=[SDS(dproj.shape, BF16), SDS((n_heads, CHUNK, BAND), F32), SDS((1, HEAD), F32), SDS((1, HEAD), F32)],
        scratch_shapes=[pltpu.VMEM((t, HEAD), BF16), pltpu.VMEM((t + PAD, HEAD), BF16), pltpu.VMEM((t + PAD, HEAD), BF16),
                        pltpu.VMEM((t, HEAD), F32), pltpu.VMEM((t + PAD, HEAD), F32), pltpu.VMEM((t + PAD, HEAD), F32),
                        pltpu.VMEM((3, t, HEAD), BF16), pltpu.SemaphoreType.DMA((3,))],
        input_output_aliases={0: 0}, compiler_params=_params(("arbitrary",)),
        name="attn_bwd")(dproj, *deps, proj, proj, proj, q_gain, k_gain, bias, dy)


def _place():
    x, y, c = lax.axis_index("x"), lax.axis_index("y"), lax.axis_index("c")
    others = [(1 - x, y), (x, 1 - y), (1 - x, 1 - y)]
    return x, y, c, others


def _chunk_of(ref, kind, chip, half, shard_shape):
    r, n = shard_shape
    hr = r // 2
    if kind == "col":
        rows = pl.ds(0, r) if half is None else pl.ds(half * hr, hr)
        return ref.at[rows, pl.ds(chip * n, n)]
    rows = pl.ds(chip * r, r) if half is None else pl.ds(chip * r + half * hr, hr)
    return ref.at[rows, :]


EFFECT = pltpu.SideEffectType.DATAFLOW_SIDE_EFFECTING


def _start_copies(name, bufs, plan, n, deps):
    nb, nd = len(bufs), len(deps)

    def body(*refs):
        send, recv, token = refs[nb + nd], refs[nb + nd + 1], refs[-1]
        for cp in plan(refs[:nb], send, recv)[0]:
            cp.start()
        token[...] = jnp.zeros_like(token)

    out = pl.pallas_call(
        body, name=name,
        out_shape=(pltpu.SemaphoreType.DMA((n,)), pltpu.SemaphoreType.DMA((n,)),
                   *[pltpu.HBM(b.shape, b.dtype) for b in bufs], SDS((8, 128), F32)),
        in_specs=[HBM] * nb + [ANY] * nd,
        out_specs=(SEM, SEM, *[HBM] * nb, pl.BlockSpec(memory_space=pltpu.VMEM)),
        input_output_aliases={i: 2 + i for i in range(nb)},
        compiler_params=pltpu.CompilerParams(has_side_effects=EFFECT),
    )(*[pltpu.with_memory_space_constraint(b, pltpu.HBM) for b in bufs], *deps)
    return out[0], out[1], list(out[2:2 + nb]), out[-1]


def _wait_copies(name, bufs, send, recv, plan, after):
    nb = len(bufs)

    def body(*refs):
        sends, recvs = plan(refs[:nb], refs[nb], refs[nb + 1])
        for cp in sends:
            cp.wait_send()
        for cp in recvs:
            cp.wait_recv()

    out = pl.pallas_call(
        body, name=name, out_shape=tuple(pltpu.HBM(b.shape, b.dtype) for b in bufs),
        in_specs=[HBM] * nb + [SEM, SEM] + [ANY] * len(after), out_specs=tuple([HBM] * nb),
        input_output_aliases={i: i for i in range(nb)},
        compiler_params=pltpu.CompilerParams(has_side_effects=EFFECT),
    )(*bufs, send, recv, *after)
    return list(out)


def _remote(src, dst, send, recv, i, dev):
    return pltpu.make_async_remote_copy(src_ref=src, dst_ref=dst, send_sem=send.at[i], recv_sem=recv.at[i],
                                        device_id=dev, device_id_type=MESH)


ALL_RELATIONS = (0, 1, 2)


def _plan_gather_ici(kinds, shapes, rels=ALL_RELATIONS):
    def plan(refs, send, recv):
        x, y, c, others = _place()
        sends, recvs = [], []
        for w, (kind, ss) in enumerate(zip(kinds, shapes)):
            for p in rels:
                px, py = others[p]
                mine = _chunk_of(refs[w], kind, 2 * x + y, c, ss)
                theirs = _chunk_of(refs[w], kind, 2 * px + py, c, ss)
                sends.append(_remote(mine, mine, send, recv, 3 * w + p, (px, py, c)))
                recvs.append(_remote(theirs, theirs, send, recv, 3 * w + p, (px, py, c)))
        return sends, recvs

    return plan, 3 * len(kinds)


def _plan_gather_pass(kinds, shapes, rels=ALL_RELATIONS):
    def plan(refs, send, recv):
        x, y, c, others = _place()
        sends, recvs = [], []
        for w, (kind, ss) in enumerate(zip(kinds, shapes)):
            for i, p in enumerate(rels):
                px, py = others[p]
                got = _chunk_of(refs[w], kind, 2 * px + py, c, ss)
                coming = _chunk_of(refs[w], kind, 2 * px + py, 1 - c, ss)
                sends.append(_remote(got, got, send, recv, len(rels) * w + i, (x, y, 1 - c)))
                recvs.append(_remote(coming, coming, send, recv, len(rels) * w + i, (x, y, 1 - c)))
        return sends, recvs

    return plan, len(rels) * len(kinds)


def _plan_pair(kinds, shapes):
    nw = len(kinds)

    def plan(refs, send, recv):
        x, y, c, _ = _place()
        sends = []
        for w, (kind, ss) in enumerate(zip(kinds, shapes)):
            for k in range(4):
                sends.append(_remote(_chunk_of(refs[w], kind, k, 1 - c, ss), refs[nw + w].at[k], send, recv,
                                     4 * w + k, (x, y, 1 - c)))
        return sends, sends

    return plan, 4 * nw


def _plan_chip(nw):
    def plan(refs, send, recv):
        x, y, c, others = _place()
        sends = []
        for w in range(nw):
            for p, (px, py) in enumerate(others):
                sends.append(_remote(refs[w].at[p], refs[nw + w].at[p], send, recv, 3 * w + p, (px, py, c)))
        return sends, sends

    return plan, 3 * nw


def _plan_share(nw):
    def plan(refs, send, recv):
        x, y, c, _ = _place()
        sends = [_remote(refs[w].at[c], refs[w].at[c], send, recv, w, (x, y, 1 - c)) for w in range(nw)]
        recvs = [_remote(refs[w].at[1 - c], refs[w].at[1 - c], send, recv, w, (x, y, 1 - c)) for w in range(nw)]
        return sends, recvs

    return plan, nw


def _grad_half_spec(kind, tr, tn, nr, nn, chunk):
    if kind == "col":
        return pl.BlockSpec((tr, tn), lambda *a: (a[-1][1] * nr + a[-3], chunk(*a) * nn + a[-2]))
    return pl.BlockSpec((tr, tn), lambda *a: ((2 * chunk(*a) + a[-1][1]) * nr + a[-3], a[-2]))


def _pair_add(grad, got, kind, shard_shape, pos, name):
    r, n = shard_shape
    hr = r // 2
    tr, tn = _row_tile(hr, n), n
    nr, nn = hr // tr, n // tn
    g_spec = _grad_half_spec(kind, tr, tn, nr, nn, lambda p, i, j, pos_: pos_[2 + p])
    r_spec = pl.BlockSpec((1, tr, tn), lambda p, i, j, pos_: (pos_[2 + p], i, j))
    o_spec = pl.BlockSpec((1, tr, tn), lambda p, i, j, pos_: (p, i, j))

    def body(pos_ref, g_ref, r_ref, o_ref):
        o_ref[0] = (g_ref[...] + r_ref[0]).astype(BF16)

    return pl.pallas_call(
        body,
        grid_spec=pltpu.PrefetchScalarGridSpec(num_scalar_prefetch=1, grid=(3, nr, nn), in_specs=[g_spec, r_spec],
                                               out_specs=o_spec),
        out_shape=SDS((3, hr, n), BF16),
        compiler_params=_params(("parallel", "parallel", "parallel")), name=name)(pos, grad, got)


def _chip_add(grad, got, got16, kind, shard_shape, pos, name):
    r, n = shard_shape
    hr = r // 2
    tr, tn = _row_tile(hr, n), n
    nr, nn = hr // tr, n // tn
    g_spec = _grad_half_spec(kind, tr, tn, nr, nn, lambda i, j, pos_: pos_[0])
    r_spec = pl.BlockSpec((1, tr, tn), lambda i, j, pos_: (pos_[0], i, j))
    oth = pl.BlockSpec((3, tr, tn), lambda i, j, pos_: (0, i, j))

    def body(pos_ref, g_ref, r_ref, oth_ref, o_ref):
        own = g_ref[...] + r_ref[0]
        o_ref[0] = ((own + oth_ref[0].astype(F32)) + oth_ref[1].astype(F32)) + oth_ref[2].astype(F32)

    return pl.pallas_call(
        body,
        grid_spec=pltpu.PrefetchScalarGridSpec(num_scalar_prefetch=1, grid=(nr, nn), in_specs=[g_spec, r_spec, oth],
                                               out_specs=pl.BlockSpec((1, tr, tn), lambda i, j, pos_: (pos_[1], i, j))),
        out_shape=SDS((2, hr, n), F32), compiler_params=_params(("parallel", "parallel")),
        name=name)(pos, grad, got, got16)


def _adamw_math(w, g, m, v):
    m = ADAM_B1 * m + (1.0 - ADAM_B1) * g
    v = ADAM_B2 * v + (1.0 - ADAM_B2) * (g * g)
    m_hat = m / (1.0 - ADAM_B1 ** ADAM_STEP)
    v_hat = v / (1.0 - ADAM_B2 ** ADAM_STEP)
    return -ADAM_LR * (m_hat / (jnp.sqrt(v_hat) + ADAM_EPS) + ADAM_WD * w), m, v


def _adamw(w, g, m, v, name):
    r, n = w.shape
    tr, tn = _row_tile(r, n), n

    def body(w_ref, g_ref, m_ref, v_ref, d_ref, nm_ref, nv_ref, go_ref):
        gv = g_ref[...]
        d_ref[...], nm_ref[...], nv_ref[...] = _adamw_math(w_ref[...], gv, m_ref[...], v_ref[...])
        go_ref[...] = gv

    tile = pl.BlockSpec((tr, tn), lambda i, j: (i, j))
    return pl.pallas_call(
        body, grid=(r // tr, n // tn), in_specs=[tile] * 4, out_specs=[tile] * 4, out_shape=[SDS((r, n), F32)] * 4,
        compiler_params=_params(("parallel", "parallel")), name=name)(w, g, m, v)


def _small_allreduce_adamw(g, w, m, v, shapes, deps=()):
    length = g.shape[1]
    direct, at = [], 0
    for shp in shapes:
        size = 1
        for dim in shp:
            size *= dim
        if len(shp) == 2 and shp[1] % 128 == 0 and at % 128 == 0:
            direct.append((at, shp))
        at += size

    def body(*refs):
        g_ref, w_ref, m_ref, v_ref = refs[:4]
        outs = refs[4 + len(deps):]
        gs_ref, d_ref, nm_ref, nv_ref = outs[:4]
        buf, send, recv = outs[-3:]
        x, y, c = lax.axis_index("x"), lax.axis_index("y"), lax.axis_index("c")
        me = 4 * x + 2 * y + c
        buf[me] = g_ref[...]
        cps = []
        for d in range(1, 8):
            peer = (x ^ (d >> 2), y ^ ((d >> 1) & 1), c ^ (d & 1))
            cp = pltpu.make_async_remote_copy(src_ref=buf.at[me], dst_ref=buf.at[me], send_sem=send.at[d - 1],
                                              recv_sem=recv.at[d - 1], device_id=peer, device_id_type=MESH)
            cp.start()
            cps.append(cp)
        for cp in cps:
            cp.wait()
        total = buf[0]
        for d in range(1, 8):
            total = total + buf[d]
        results = (total,) + _adamw_math(w_ref[...], total, m_ref[...], v_ref[...])
        for k, vec in enumerate(results):
            outs[k][...] = vec
            for j, (off, (rows, cols)) in enumerate(direct):
                ref = outs[4 + k * len(direct) + j]
                for r in range(rows):
                    ref[r:r + 1, :] = vec[:, off + r * cols:off + (r + 1) * cols]

    vm = pl.BlockSpec(memory_space=pltpu.VMEM)
    out_shape = [SDS((1, length), F32)] * 4 + [SDS(shp, F32) for _ in range(4) for _, shp in direct]
    out = pl.pallas_call(
        body, in_specs=[vm] * 4 + [ANY] * len(deps), out_specs=[vm] * len(out_shape), out_shape=out_shape,
        scratch_shapes=[pltpu.VMEM((8, 1, length), F32), pltpu.SemaphoreType.DMA((7,)), pltpu.SemaphoreType.DMA((7,))],
        compiler_params=pltpu.CompilerParams(has_side_effects=True), name="small_allreduce_adamw")(g, w, m, v, *deps)
    own = {off: [out[4 + k * len(direct) + j] for k in range(4)] for j, (off, _) in enumerate(direct)}
    return out[:4], own


def kernel(x, w_in, b_gate, norm_mix, norm_ffn, hgrn_lb_logits, hgrn_out_gain, q_gain, k_gain, rel_bias, w_proj_a, w_proj_b, w_out, w_ffn_in, w_ffn_out, loss_target, m_w_in, m_b_gate, m_norm_mix, m_norm_ffn, m_hgrn_lb_logits, m_hgrn_out_gain, m_q_gain, m_k_gain, m_rel_bias, m_w_proj_a, m_w_proj_b, m_w_out, m_w_ffn_in, m_w_ffn_out, v_w_in, v_b_gate, v_norm_mix, v_norm_ffn, v_hgrn_lb_logits, v_hgrn_out_gain, v_q_gain, v_k_gain, v_rel_bias, v_w_proj_a, v_w_proj_b, v_w_out, v_w_ffn_in, v_w_ffn_out):
    t, d = x.shape[1], x.shape[2]
    d_a = hgrn_out_gain.shape[1]
    h_a = d_a // HEAD
    h_b = rel_bias.shape[1]
    d_b = h_b * HEAD
    x0 = x.reshape(t, d)
    target = loss_target.reshape(t, d)
    ax, ay = lax.axis_index("x"), lax.axis_index("y")
    pos = jnp.stack([2 * ax + ay, lax.axis_index("c"), 2 * (1 - ax) + ay, 2 * ax + 1 - ay,
                     2 * (1 - ax) + 1 - ay]).astype(jnp.int32)

    names = ["w_in", "w_proj_a", "w_proj_b", "w_out", "w_ffn_in", "w_ffn_out"]
    big = dict(zip(names, [w_in[0], w_proj_a[0], w_proj_b[0], w_out[0], w_ffn_in[0], w_ffn_out[0]]))
    big_m = dict(zip(names, [m_w_in[0], m_w_proj_a[0], m_w_proj_b[0], m_w_out[0], m_w_ffn_in[0], m_w_ffn_out[0]]))
    big_v = dict(zip(names, [v_w_in[0], v_w_proj_a[0], v_w_proj_b[0], v_w_out[0], v_w_ffn_in[0], v_w_ffn_out[0]]))
    kind = dict(zip(names, ["col", "col", "col", "row", "col", "row"]))
    shape = {nm: big[nm].shape for nm in names}

    def gather_start(tag, group, deps):
        plan, n = _plan_gather_ici([kind[g] for g in group], [shape[g] for g in group])
        fulls = [_cast_into_full(big[g], kind[g], pos, "cast_" + g) for g in group]
        send, recv, bufs, token = _start_copies("gather_ici_start_" + tag, fulls, plan, n, deps)
        return (tag, group, plan, send, recv, bufs), token

    def gather_pass(state, after, rels=ALL_RELATIONS, part=""):
        tag, group, _, send, recv, bufs = state
        kinds_, shapes_ = [kind[g] for g in group], [shape[g] for g in group]
        bufs = _wait_copies("gather_ici_wait_" + tag + part, bufs, send, recv,
                            _plan_gather_ici(kinds_, shapes_, rels)[0], after)
        plan, n = _plan_gather_pass(kinds_, shapes_, rels)
        send2, recv2, bufs, token = _start_copies("gather_pass_start_" + tag + part, bufs, plan, n, ())
        return (tag + part, group, plan, send2, recv2, bufs), token

    def gather_done(state, after):
        tag, group, plan, send, recv, bufs = state
        return _wait_copies("gather_pass_wait_" + tag, bufs, send, recv, plan, after)

    def reduce_start(tag, group, grads, deps, sent=None):
        plan, n = _plan_pair([kind[g] for g in group], [shape[g] for g in group])
        srcs = list(grads) if sent is None else list(sent)
        lands = [lax.empty((4, shape[g][0] // 2, shape[g][1]), s.dtype) for g, s in zip(group, srcs)]
        send, recv, bufs, token = _start_copies("pair_start_" + tag, srcs + lands, plan, n, deps)
        kept = None if sent is None else list(grads)
        return dict(tag=tag, group=group, plan=plan, send=send, recv=recv, bufs=bufs, kept=kept), token

    def reduce_pair_done(st, after):
        tag, group, nw = st["tag"], st["group"], len(st["group"])
        bufs = _wait_copies("pair_wait_" + tag, st["bufs"], st["send"], st["recv"], st["plan"], after)
        grads, gots = (bufs[:nw] if st["kept"] is None else st["kept"]), bufs[nw:]
        parts = [_pair_add(g, l, kind[nm], shape[nm], pos, "pair_add_" + nm) for g, l, nm in zip(grads, gots, group)]
        lands = [lax.empty((3, shape[g][0] // 2, shape[g][1]), BF16) for g in group]
        plan, n = _plan_chip(nw)
        send, recv, bufs, token = _start_copies("chip_start_" + tag, parts + lands, plan, n, ())
        return dict(st, plan=plan, send=send, recv=recv, bufs=bufs, grads=grads, gots=gots), token

    def reduce_chip_done(st, after):
        tag, group, nw = st["tag"], st["group"], len(st["group"])
        bufs = _wait_copies("chip_wait_" + tag, st["bufs"], st["send"], st["recv"], st["plan"], after)
        finals = [_chip_add(g, l, got16, kind[nm], shape[nm], pos, "chip_add_" + nm)
                  for g, l, got16, nm in zip(st["grads"], st["gots"], bufs[nw:], group)]
        plan, n = _plan_share(nw)
        send, recv, bufs, token = _start_copies("share_start_" + tag, finals, plan, n, ())
        return dict(st, plan=plan, send=send, recv=recv, bufs=bufs), token

    g_big, upd = {}, {}

    def reduce_finish(st, after):
        bufs = _wait_copies("share_wait_" + st["tag"], st["bufs"], st["send"], st["recv"], st["plan"], after)
        for full, nm in zip(bufs, st["group"]):
            upd[nm] = _adamw(big[nm], full.reshape(shape[nm]), big_m[nm], big_v[nm], "adamw_" + nm)
            g_big[nm] = upd[nm][3]

    ga, token = gather_start("a", ["w_in"], ())
    gb, token = gather_start("b", ["w_proj_a", "w_proj_b", "w_out"], (token,))
    gc, token = gather_start("c", ["w_ffn_in"], (token,))
    gd, token = gather_start("d", ["w_ffn_out"], (token,))
    h1, r1 = _rmsnorm_fwd(x0, norm_mix, "rmsnorm_mix")
    rb = jnp.pad(rel_bias[0], ((0, 0), (0, REL_LANES - N_REL)))
    bias = _relbias_expand(rb).transpose(1, 0, 2)
    proj = _matmul_chunks(h1, big["w_in"], (0,), None, pos, "proj_in_own", own_shard=True)
    ici_a = ga
    ga, token = gather_pass(ici_a, (h1, bias, proj, token), rels=(0, 1), part="_near")
    (wg_in,) = gather_done(ga, ())
    proj = _matmul_chunks(h1, wg_in, (2, 3), proj, pos, "proj_in_near")
    ga, token = gather_pass(ici_a[:5] + ([wg_in],), (proj,), rels=(2,), part="_far")
    (wg_in,) = gather_done(ga, ())
    proj = _matmul_chunks(h1, wg_in, (4,), proj, pos, "proj_in_far")
    y_a, o_pre, states = _hgrn_fwd(proj, hgrn_lb_logits, hgrn_out_gain, h_a)
    gb, token = gather_pass(gb, (y_a,))
    col_b = 4 * d_a // HEAD
    y_b = _attn_fwd(proj, q_gain, k_gain, bias, h_b, col_b)
    wg_pa, wg_pb, wg_out = gather_done(gb, (y_b,))
    gate_off = 4 * d_a + 3 * d_b
    pa, pb, merged = _proj_merge(y_a, y_b, wg_pa, wg_pb, proj, b_gate, gate_off, deps=(token,))
    x2 = _matmul(merged, wg_out, res=x0, name="out_proj")
    gc, token = gather_pass(gc, (x2,))
    h2, r2 = _rmsnorm_fwd(x2, norm_ffn, "rmsnorm_ffn")
    (wg_fin,) = gather_done(gc, (h2,))
    ff_gate, ff_up, act = _ffn_in_swiglu(h2, wg_fin, deps=(token,))
    gd, token = gather_pass(gd, (act,))
    (wg_fout,) = gather_done(gd, ())
    dy, dy16, loss_part = _ffn_out_loss(act, wg_fout, x2, target)

    g_fout, g_fout16 = _matmul(act, dy16, ta=True, name="dw_ffn_out", also_bf16=True)
    r_fout, token = reduce_start("fout", ["w_ffn_out"], [g_fout], (), sent=[g_fout16])
    dgu = _d_act_swiglu(dy16, wg_fout, ff_gate, ff_up, deps=(token,))
    r_fout, token = reduce_pair_done(r_fout, (dgu,))
    g_fin, g_fin16 = _matmul(h2, dgu, ta=True, name="dw_ffn_in", deps=(token,), also_bf16=True)
    r_fin, token = reduce_start("fin", ["w_ffn_in"], [g_fin], (), sent=[g_fin16])
    dh2 = _matmul(dgu, wg_fin, tb=True, name="d_h2", deps=(token,))
    r_fout, token_a = reduce_chip_done(r_fout, (dh2,))
    r_fin, token_b = reduce_pair_done(r_fin, (dh2,))
    dx2, dx2_16, g_norm_ffn = _rmsnorm_bwd(dh2, x2, r2, norm_ffn, dy, "rmsnorm_ffn_bwd", deps=(token_a, token_b))
    dp_ab, dproj, g_bgate = _d_merged_gates(dx2_16, wg_out, proj, b_gate, pa, pb, gate_off)
    g_out, g_out16 = _matmul(merged, dx2_16, ta=True, name="dw_out", also_bf16=True)
    g_pa, g_pa16 = _matmul(y_a, dp_ab, ta=True, name="dw_proj_a", b_lead=0, also_bf16=True)
    g_pb, g_pb16 = _matmul(y_b, dp_ab, ta=True, name="dw_proj_b", b_lead=1, also_bf16=True)
    r_mid, token = reduce_start("mid", ["w_proj_a", "w_proj_b", "w_out"], [g_pa, g_pb, g_out], (),
                                sent=[g_pa16, g_pb16, g_out16])
    dy_a = _matmul(dp_ab, wg_pa, tb=True, name="d_y_a", deps=(token,), a_lead=0)
    dy_b = _matmul(dp_ab, wg_pb, tb=True, name="d_y_b", a_lead=1)
    r_mid, token_b = reduce_pair_done(r_mid, (dy_b,))
    dproj, dbias, g_qg, g_kg = _attn_bwd(dproj, proj, q_gain, k_gain, bias, dy_b, h_b, col_b, deps=(token_b,))
    r_fin, token_a = reduce_chip_done(r_fin, (dbias,))
    r_mid, token = reduce_chip_done(r_mid, (dbias,))
    dproj, g_lb, g_gain = _hgrn_bwd(dproj, proj, o_pre, states, dy_a, hgrn_lb_logits, hgrn_out_gain, h_a,
                                    deps=(token, token_a))
    g_in, g_in16 = _matmul(h1, dproj, ta=True, name="dw_in", also_bf16=True)
    r_in, token = reduce_start("in", ["w_in"], [g_in], (), sent=[g_in16])
    g_rb = _relbias_reduce(dbias.transpose(1, 0, 2))[:, :N_REL]
    reduce_finish(r_mid, (token,))
    r_in, token = reduce_pair_done(r_in, (g_rb, upd["w_out"][0]))
    dh1 = _matmul(dproj, wg_in, tb=True, name="d_h1", deps=(token,))
    dx, g_norm_mix = _rmsnorm_bwd(dh1, x0, r1, norm_mix, dx2, "rmsnorm_mix_bwd", with_bf16=False)
    reduce_finish(r_fin, (dx,))
    reduce_finish(r_fout, (dx,))
    r_in, token = reduce_chip_done(r_in, (upd["w_ffn_in"][0], upd["w_ffn_out"][0], upd["w_proj_a"][0],
                                          upd["w_proj_b"][0]))

    small_w = [b_gate, norm_mix, norm_ffn, hgrn_lb_logits, hgrn_out_gain, q_gain, k_gain, rel_bias]
    small_m = [m_b_gate, m_norm_mix, m_norm_ffn, m_hgrn_lb_logits, m_hgrn_out_gain, m_q_gain, m_k_gain, m_rel_bias]
    small_v = [v_b_gate, v_norm_mix, v_norm_ffn, v_hgrn_lb_logits, v_hgrn_out_gain, v_q_gain, v_k_gain, v_rel_bias]
    small_g = [g_bgate, g_norm_mix, g_norm_ffn, g_lb, g_gain, g_qg, g_kg, g_rb]
    sizes = [w.size for w in small_w]
    length = -(-(sum(sizes) + 1) // 128) * 128

    def pack(parts_):
        flat = jnp.concatenate([p.reshape(1, -1) for p in parts_], axis=1)
        return jnp.pad(flat, ((0, 0), (0, length - flat.shape[1])))

    one = jnp.ones((1, 1), F32)
    packed, own = _small_allreduce_adamw(pack(small_g + [loss_part]), pack(small_w + [one]), pack(small_m + [one]),
                                         pack(small_v + [one]), [w.shape for w in small_w], deps=(token,))

    def unpack(k):
        out, at = [], 0
        for w, n in zip(small_w, sizes):
            out.append(own[at][k] if at in own else packed[k][0, at:at + n].reshape(w.shape))
            at += n
        return out, packed[k][0, at]

    (sg, loss), (sd, _), (sm, _), (sv, _) = [unpack(k) for k in range(4)]
    reduce_finish(r_in, (packed[0],))

    def ordered(small, bigs):
        bigs = [bigs[nm][None] for nm in names]
        return [bigs[0]] + small + bigs[1:]

    return (loss, dx.reshape(x.shape), *ordered(sg, g_big), *ordered(sd, {nm: upd[nm][0] for nm in names}),
            *ordered(sm, {nm: upd[nm][1] for nm in names}), *ordered(sv, {nm: upd[nm][2] for nm in names}))
```
